```python
import jax, jax.numpy as jnp
from jax import lax
import numpy as np

D_MODEL = 2048
BATCH = 8
SEQ = 4096
DEPTH = 1

MIX_WIDTH = D_MODEL
HEAD_DIM = 128
DSWA_HEADS = (MIX_WIDTH // 2) // HEAD_DIM
DSWA_CONFIGS = ((128, 1), (512, 4), (2048, 16))
ROT_DIM = HEAD_DIM // 4
ROPE_THETA = 500000.0
MLA_HEADS = (MIX_WIDTH // 2) // HEAD_DIM
Q_LORA_RANK = 512
KV_LORA_RANK = 512
QK_NOPE_DIM = 128
QK_ROPE_DIM = 64
V_HEAD_DIM = 128
D_FF = 4 * D_MODEL
Q_BLOCK = 128
NORM_EPS = 1e-6
NEG_INF = -1e30

A_WIDTH = DSWA_HEADS * HEAD_DIM
IN_SPLITS = (A_WIDTH, 2 * A_WIDTH, 3 * A_WIDTH,
             3 * A_WIDTH + Q_LORA_RANK,
             3 * A_WIDTH + Q_LORA_RANK + KV_LORA_RANK)
IN_COLS = 3 * A_WIDTH + Q_LORA_RANK + KV_LORA_RANK + QK_ROPE_DIM
OUT_ROWS = DSWA_HEADS * HEAD_DIM + MLA_HEADS * V_HEAD_DIM

kernel_name = "hybrid_dilated_swa_mla_sandwich"


def _rmsnorm(x, gain):
    xf = x.astype(jnp.float32)
    xf = xf * lax.rsqrt(jnp.mean(xf * xf, axis=-1, keepdims=True) + NORM_EPS)
    return xf.astype(x.dtype) * gain


def _rope(x, positions, rot_dim):
    inv_freq = ROPE_THETA ** (-jnp.arange(0, rot_dim, 2, dtype=jnp.float32) / rot_dim)
    ang = positions.astype(jnp.float32)[..., None] * inv_freq
    cos = jnp.cos(ang)[:, :, None, :]
    sin = jnp.sin(ang)[:, :, None, :]
    xr = x[..., :rot_dim].astype(jnp.float32)
    x1, x2 = xr[..., : rot_dim // 2], xr[..., rot_dim // 2:]
    rot = jnp.concatenate([x1 * cos - x2 * sin, x2 * cos + x1 * sin], axis=-1)
    return jnp.concatenate([rot.astype(x.dtype), x[..., rot_dim:]], axis=-1)


def _dilated_window_attention(q, k, v, window, dilation):
    B, S, H, D = q.shape
    steps = window // dilation
    span = dilation * Q_BLOCK
    s_pad = -(-S // span) * span
    pad = ((0, 0), (0, s_pad - S), (0, 0), (0, 0))
    q, k, v = jnp.pad(q, pad), jnp.pad(k, pad), jnp.pad(v, pad)
    nb = s_pad // span
    qb = q.reshape(B, nb, Q_BLOCK, dilation, H, D)
    kb = k.reshape(B, nb, Q_BLOCK, dilation, H, D)
    vb = v.reshape(B, nb, Q_BLOCK, dilation, H, D)

    def with_prev(t):
        prev = jnp.pad(t[:, :-1], ((0, 0), (1, 0), (0, 0), (0, 0), (0, 0), (0, 0)))
        return jnp.concatenate([prev, t], axis=2)

    kc, vc = with_prev(kb), with_prev(vb)
    s = jnp.einsum('bniphd,bnjphd->bnphij', qb, kc).astype(jnp.float32) * (D ** -0.5)
    i = jnp.arange(Q_BLOCK)[:, None]
    j = jnp.arange(2 * Q_BLOCK)[None, :]
    dist = i + Q_BLOCK - j
    band = (dist >= 0) & (dist <= steps)
    first = (jnp.arange(nb) == 0)[:, None, None] & (j < Q_BLOCK)[None]
    mask = band[None] & ~first
    s = jnp.where(mask[None, :, None, None], s, NEG_INF)
    m = jnp.max(s, axis=-1, keepdims=True)
    p = jnp.exp(s - m)
    denom = jnp.sum(p, axis=-1, keepdims=True)
    o = jnp.einsum('bnphij,bnjphd->bniphd', (p / denom).astype(v.dtype), vc)
    lse = (m + jnp.log(denom))[..., 0]
    o = o.reshape(B, s_pad, H, D)[:, :S]
    lse = jnp.transpose(lse, (0, 1, 4, 2, 3)).reshape(B, s_pad, H)[:, :S]
    return o, lse


def _causal_block_attention(q, k, v, scale):
    B, S, H, Dk = q.shape
    nb = S // Q_BLOCK
    qb = q.reshape(B, nb, Q_BLOCK, H, Dk).transpose(1, 0, 2, 3, 4)
    key_pos = jnp.arange(S)

    def one_block(args):
        n, qn = args
        s = jnp.einsum('bqhd,bkhd->bhqk', qn, k).astype(jnp.float32) * scale
        q_pos = n * Q_BLOCK + jnp.arange(Q_BLOCK)
        s = jnp.where((key_pos[None, :] <= q_pos[:, None])[None, None], s, NEG_INF)
        p = jax.nn.softmax(s, axis=-1)
        return jnp.einsum('bhqk,bkhd->bqhd', p.astype(v.dtype), v)

    o = lax.map(one_block, (jnp.arange(nb), qb))
    return o.transpose(1, 0, 2, 3, 4).reshape(B, S, H, v.shape[-1])


def _fwd_setup_inputs(seed: int = 0) -> dict:
    key = jax.random.key(seed)
    ks = jax.random.split(key, 16)
    f32 = jnp.float32

    def w(k, shape, fan_in):
        return jax.random.normal(k, shape, f32) * (fan_in ** -0.5)

    def gain(k, n):
        return 1.0 + 0.05 * jax.random.normal(k, (DEPTH, n), f32)

    x = jax.random.normal(ks[0], (BATCH, SEQ, D_MODEL), f32)
    offset = jax.random.randint(ks[1], (BATCH, 1), 0, 2048, dtype=jnp.int32)
    positions = offset + jnp.arange(SEQ, dtype=jnp.int32)[None, :]
    return {
        "x": x,
        "positions": positions,
        "norm_attn_pre": gain(ks[2], D_MODEL),
        "norm_attn_post": gain(ks[3], D_MODEL),
        "w_in": w(ks[4], (DEPTH, D_MODEL, IN_COLS), D_MODEL),
        "q_latent_norm": gain(ks[5], Q_LORA_RANK),
        "kv_latent_norm": gain(ks[6], KV_LORA_RANK),
        "w_uq": w(ks[7], (DEPTH, Q_LORA_RANK, MLA_HEADS * (QK_NOPE_DIM + QK_ROPE_DIM)), Q_LORA_RANK),
        "w_ukv": w(ks[8], (DEPTH, KV_LORA_RANK, MLA_HEADS * (QK_NOPE_DIM + V_HEAD_DIM)), KV_LORA_RANK),
        "w_out": w(ks[9], (DEPTH, OUT_ROWS, D_MODEL), OUT_ROWS),
        "norm_mlp_pre": gain(ks[10], D_MODEL),
        "norm_mlp_post": gain(ks[11], D_MODEL),
        "w_up": w(ks[12], (DEPTH, D_MODEL, D_FF), D_MODEL),
        "w_down": w(ks[13], (DEPTH, D_FF, D_MODEL), D_FF),
    }


def _fwd_reference(x, positions, norm_attn_pre, norm_attn_post, w_in, q_latent_norm,
              kv_latent_norm, w_uq, w_ukv, w_out, norm_mlp_pre, norm_mlp_post,
              w_up, w_down):
    B, S, _ = x.shape
    for layer in range(DEPTH):
        h = _rmsnorm(x, norm_attn_pre[layer])
        proj = h @ w_in[layer]
        a_q, a_k, a_v, c_q, c_kv, k_r = jnp.split(proj, IN_SPLITS, axis=-1)

        a_q = _rope(a_q.reshape(B, S, DSWA_HEADS, HEAD_DIM), positions, ROT_DIM)
        a_k = _rope(a_k.reshape(B, S, DSWA_HEADS, HEAD_DIM), positions, ROT_DIM)
        a_v = a_v.reshape(B, S, DSWA_HEADS, HEAD_DIM)
        outs, lses = [], []
        for window, dilation in DSWA_CONFIGS:
            o, lse = _dilated_window_attention(a_q, a_k, a_v, window, dilation)
            outs.append(o)
            lses.append(lse)
        alpha = jax.nn.softmax(jnp.stack(lses, axis=0), axis=0)
        a_out = jnp.sum(alpha[..., None].astype(a_v.dtype) * jnp.stack(outs, axis=0), axis=0)

        c_q = _rmsnorm(c_q, q_latent_norm[layer])
        q_b = (c_q @ w_uq[layer]).reshape(B, S, MLA_HEADS, QK_NOPE_DIM + QK_ROPE_DIM)
        q_nope, q_rope = q_b[..., :QK_NOPE_DIM], q_b[..., QK_NOPE_DIM:]
        q_rope = _rope(q_rope, positions, QK_ROPE_DIM)
        c_kv = _rmsnorm(c_kv, kv_latent_norm[layer])
        kv = (c_kv @ w_ukv[layer]).reshape(B, S, MLA_HEADS, QK_NOPE_DIM + V_HEAD_DIM)
        k_nope, v_b = kv[..., :QK_NOPE_DIM], kv[..., QK_NOPE_DIM:]
        k_rope = _rope(k_r[:, :, None, :], positions, QK_ROPE_DIM)
        q_full = jnp.concatenate([q_nope, q_rope], axis=-1)
        k_full = jnp.concatenate(
            [k_nope, jnp.broadcast_to(k_rope, (B, S, MLA_HEADS, QK_ROPE_DIM))], axis=-1)
        b_out = _causal_block_attention(q_full, k_full, v_b,
                                        (QK_NOPE_DIM + QK_ROPE_DIM) ** -0.5)

        mixed = jnp.concatenate([a_out.reshape(B, S, DSWA_HEADS * HEAD_DIM),
                                 b_out.reshape(B, S, MLA_HEADS * V_HEAD_DIM)], axis=-1)
        x = x + _rmsnorm(mixed @ w_out[layer], norm_attn_post[layer])

        h = _rmsnorm(x, norm_mlp_pre[layer])
        u = jnp.square(jax.nn.relu(h @ w_up[layer]))
        x = x + _rmsnorm(u @ w_down[layer], norm_mlp_post[layer])
    return x


import jax as _jax
import jax.numpy as _jnp

TWIN_FORMAT = 'train_step'
FWD_PARAMS = ['x', 'positions', 'norm_attn_pre', 'norm_attn_post', 'w_in', 'q_latent_norm', 'kv_latent_norm', 'w_uq', 'w_ukv', 'w_out', 'norm_mlp_pre', 'norm_mlp_post', 'w_up', 'w_down']
TWIN_WEIGHTS = ['norm_attn_pre', 'norm_attn_post', 'w_in', 'q_latent_norm', 'kv_latent_norm', 'w_uq', 'w_ukv', 'w_out', 'norm_mlp_pre', 'norm_mlp_post', 'w_up', 'w_down']
TWIN_DIFF_INPUT = 'x'
TWIN_INPUTS = ['x', 'positions', 'norm_attn_pre', 'norm_attn_post', 'w_in', 'q_latent_norm', 'kv_latent_norm', 'w_uq', 'w_ukv', 'w_out', 'norm_mlp_pre', 'norm_mlp_post', 'w_up', 'w_down', 'loss_target', 'm_norm_attn_pre', 'm_norm_attn_post', 'm_w_in', 'm_q_latent_norm', 'm_kv_latent_norm', 'm_w_uq', 'm_w_ukv', 'm_w_out', 'm_norm_mlp_pre', 'm_norm_mlp_post', 'm_w_up', 'm_w_down', 'v_norm_attn_pre', 'v_norm_attn_post', 'v_w_in', 'v_q_latent_norm', 'v_kv_latent_norm', 'v_w_uq', 'v_w_ukv', 'v_w_out', 'v_norm_mlp_pre', 'v_norm_mlp_post', 'v_w_up', 'v_w_down']
TWIN_OUTPUTS = ['loss', 'grad_x', 'grad_norm_attn_pre', 'grad_norm_attn_post', 'grad_w_in', 'grad_q_latent_norm', 'grad_kv_latent_norm', 'grad_w_uq', 'grad_w_ukv', 'grad_w_out', 'grad_norm_mlp_pre', 'grad_norm_mlp_post', 'grad_w_up', 'grad_w_down', 'delta_norm_attn_pre', 'delta_norm_attn_post', 'delta_w_in', 'delta_q_latent_norm', 'delta_kv_latent_norm', 'delta_w_uq', 'delta_w_ukv', 'delta_w_out', 'delta_norm_mlp_pre', 'delta_norm_mlp_post', 'delta_w_up', 'delta_w_down', 'new_m_norm_attn_pre', 'new_m_norm_attn_post', 'new_m_w_in', 'new_m_q_latent_norm', 'new_m_kv_latent_norm', 'new_m_w_uq', 'new_m_w_ukv', 'new_m_w_out', 'new_m_norm_mlp_pre', 'new_m_norm_mlp_post', 'new_m_w_up', 'new_m_w_down', 'new_v_norm_attn_pre', 'new_v_norm_attn_post', 'new_v_w_in', 'new_v_q_latent_norm', 'new_v_kv_latent_norm', 'new_v_w_uq', 'new_v_w_ukv', 'new_v_w_out', 'new_v_norm_mlp_pre', 'new_v_norm_mlp_post', 'new_v_w_up', 'new_v_w_down']
TWIN_LEAF_KINDS = {'loss': 'loss', 'grad_x': 'grad_x', 'grad_norm_attn_pre': 'grad_w', 'grad_norm_attn_post': 'grad_w', 'grad_w_in': 'grad_w', 'grad_q_latent_norm': 'grad_w', 'grad_kv_latent_norm': 'grad_w', 'grad_w_uq': 'grad_w', 'grad_w_ukv': 'grad_w', 'grad_w_out': 'grad_w', 'grad_norm_mlp_pre': 'grad_w', 'grad_norm_mlp_post': 'grad_w', 'grad_w_up': 'grad_w', 'grad_w_down': 'grad_w', 'delta_norm_attn_pre': 'delta_w', 'delta_norm_attn_post': 'delta_w', 'delta_w_in': 'delta_w', 'delta_q_latent_norm': 'delta_w', 'delta_kv_latent_norm': 'delta_w', 'delta_w_uq': 'delta_w', 'delta_w_ukv': 'delta_w', 'delta_w_out': 'delta_w', 'delta_norm_mlp_pre': 'delta_w', 'delta_norm_mlp_post': 'delta_w', 'delta_w_up': 'delta_w', 'delta_w_down': 'delta_w', 'new_m_norm_attn_pre': 'new_m', 'new_m_norm_attn_post': 'new_m', 'new_m_w_in': 'new_m', 'new_m_q_latent_norm': 'new_m', 'new_m_kv_latent_norm': 'new_m', 'new_m_w_uq': 'new_m', 'new_m_w_ukv': 'new_m', 'new_m_w_out': 'new_m', 'new_m_norm_mlp_pre': 'new_m', 'new_m_norm_mlp_post': 'new_m', 'new_m_w_up': 'new_m', 'new_m_w_down': 'new_m', 'new_v_norm_attn_pre': 'new_v', 'new_v_norm_attn_post': 'new_v', 'new_v_w_in': 'new_v', 'new_v_q_latent_norm': 'new_v', 'new_v_kv_latent_norm': 'new_v', 'new_v_w_uq': 'new_v', 'new_v_w_ukv': 'new_v', 'new_v_w_out': 'new_v', 'new_v_norm_mlp_pre': 'new_v', 'new_v_norm_mlp_post': 'new_v', 'new_v_w_up': 'new_v', 'new_v_w_down': 'new_v'}


def _forward(args):
    return _fwd_reference(*[args[k] for k in FWD_PARAMS])


def _output_shape():
    def fwd():
        inp = _fwd_setup_inputs(0)
        return _fwd_reference(*[inp[k] for k in FWD_PARAMS])
    out = _jax.eval_shape(fwd)
    return out.shape, out.dtype

N_MICROBATCH = 1
ADAM_LR = 0.001
ADAM_B1 = 0.9
ADAM_B2 = 0.999
ADAM_EPS = 1e-08
ADAM_WD = 0.01
ADAM_STEP = 10
PER_EXAMPLE_BATCH_AXIS = {'x': 0, 'positions': 0, 'loss_target': 0}
SHARED_INPUTS = []
_WEIGHT_DTYPES = {'norm_attn_pre': _jnp.float32, 'norm_attn_post': _jnp.float32, 'w_in': _jnp.float32, 'q_latent_norm': _jnp.float32, 'kv_latent_norm': _jnp.float32, 'w_uq': _jnp.float32, 'w_ukv': _jnp.float32, 'w_out': _jnp.float32, 'norm_mlp_pre': _jnp.float32, 'norm_mlp_post': _jnp.float32, 'w_up': _jnp.float32, 'w_down': _jnp.float32}
MOMENT_SCALE = {'norm_attn_pre': 7.162084e-01, 'norm_attn_post': 1.590477e+01, 'w_in': 5.014461e-01, 'q_latent_norm': 2.664927e-01, 'kv_latent_norm': 1.147987e+00, 'w_uq': 1.532170e-01, 'w_ukv': 4.952533e-01, 'w_out': 6.609950e-01, 'norm_mlp_pre': 3.887428e-01, 'norm_mlp_post': 1.636446e+01, 'w_up': 1.981675e-01, 'w_down': 7.134263e-01}


def _to_microbatches(a, axis):
    t = _jnp.moveaxis(a, axis, 0)
    t = t.reshape((N_MICROBATCH, t.shape[0] // N_MICROBATCH) + t.shape[1:])
    return _jnp.moveaxis(t, 1, axis + 1)


def setup_inputs(seed: int = 0) -> dict:
    inp = _fwd_setup_inputs(seed)
    key = _jax.random.fold_in(_jax.random.key(seed), 7919)
    shape, _ = _output_shape()
    out = dict(inp)
    out["loss_target"] = _jax.random.normal(_jax.random.fold_in(key, 0), shape, _jnp.float32)
    for i, name in enumerate(TWIN_WEIGHTS):
        w = inp[name].astype(_jnp.float32)
        if MOMENT_SCALE is None:
            s = _jnp.sqrt(_jnp.mean(_jnp.square(w)) + 1e-30)
        else:
            s = MOMENT_SCALE[name]
        km, kv = _jax.random.split(_jax.random.fold_in(key, i + 1))
        out[name] = w
        out["m_" + name] = s * _jax.random.normal(km, w.shape, _jnp.float32)
        out["v_" + name] = (s * s) * _jax.random.uniform(kv, w.shape, _jnp.float32, 0.5, 1.5)
    if N_MICROBATCH > 1:
        for name, axis in PER_EXAMPLE_BATCH_AXIS.items():
            out[name] = _to_microbatches(out[name], axis)
    return {'x': out['x'], 'positions': out['positions'], 'norm_attn_pre': out['norm_attn_pre'], 'norm_attn_post': out['norm_attn_post'], 'w_in': out['w_in'], 'q_latent_norm': out['q_latent_norm'], 'kv_latent_norm': out['kv_latent_norm'], 'w_uq': out['w_uq'], 'w_ukv': out['w_ukv'], 'w_out': out['w_out'], 'norm_mlp_pre': out['norm_mlp_pre'], 'norm_mlp_post': out['norm_mlp_post'], 'w_up': out['w_up'], 'w_down': out['w_down'], 'loss_target': out['loss_target'], 'm_norm_attn_pre': out['m_norm_attn_pre'], 'm_norm_attn_post': out['m_norm_attn_post'], 'm_w_in': out['m_w_in'], 'm_q_latent_norm': out['m_q_latent_norm'], 'm_kv_latent_norm': out['m_kv_latent_norm'], 'm_w_uq': out['m_w_uq'], 'm_w_ukv': out['m_w_ukv'], 'm_w_out': out['m_w_out'], 'm_norm_mlp_pre': out['m_norm_mlp_pre'], 'm_norm_mlp_post': out['m_norm_mlp_post'], 'm_w_up': out['m_w_up'], 'm_w_down': out['m_w_down'], 'v_norm_attn_pre': out['v_norm_attn_pre'], 'v_norm_attn_post': out['v_norm_attn_post'], 'v_w_in': out['v_w_in'], 'v_q_latent_norm': out['v_q_latent_norm'], 'v_kv_latent_norm': out['v_kv_latent_norm'], 'v_w_uq': out['v_w_uq'], 'v_w_ukv': out['v_w_ukv'], 'v_w_out': out['v_w_out'], 'v_norm_mlp_pre': out['v_norm_mlp_pre'], 'v_norm_mlp_post': out['v_norm_mlp_post'], 'v_w_up': out['v_w_up'], 'v_w_down': out['v_w_down']}


def _loss(weights, diff, rest, loss_target):
    with _jax.named_scope("forward"):
        args = {**rest, TWIN_DIFF_INPUT: diff, **{k: w.astype(_WEIGHT_DTYPES[k]) for k, w in weights.items()}}
        y = _forward(args)
    with _jax.named_scope("loss_head"):
        err = _jnp.square(y.astype(_jnp.float32) - loss_target)
        return 0.5 * _jnp.sum(_jnp.mean(err, axis=-1)) if err.ndim else 0.5 * err


def _adamw(w, g, m, v):
    m = ADAM_B1 * m + (1.0 - ADAM_B1) * g
    v = ADAM_B2 * v + (1.0 - ADAM_B2) * _jnp.square(g)
    m_hat = m / (1.0 - ADAM_B1 ** ADAM_STEP)
    v_hat = v / (1.0 - ADAM_B2 ** ADAM_STEP)
    delta = -ADAM_LR * (m_hat / (_jnp.sqrt(v_hat) + ADAM_EPS) + ADAM_WD * w)
    return delta, m, v


def reference(x, positions, norm_attn_pre, norm_attn_post, w_in, q_latent_norm, kv_latent_norm, w_uq, w_ukv, w_out, norm_mlp_pre, norm_mlp_post, w_up, w_down, loss_target, m_norm_attn_pre, m_norm_attn_post, m_w_in, m_q_latent_norm, m_kv_latent_norm, m_w_uq, m_w_ukv, m_w_out, m_norm_mlp_pre, m_norm_mlp_post, m_w_up, m_w_down, v_norm_attn_pre, v_norm_attn_post, v_w_in, v_q_latent_norm, v_kv_latent_norm, v_w_uq, v_w_ukv, v_w_out, v_norm_mlp_pre, v_norm_mlp_post, v_w_up, v_w_down):
    given = dict(x=x, positions=positions, norm_attn_pre=norm_attn_pre, norm_attn_post=norm_attn_post, w_in=w_in, q_latent_norm=q_latent_norm, kv_latent_norm=kv_latent_norm, w_uq=w_uq, w_ukv=w_ukv, w_out=w_out, norm_mlp_pre=norm_mlp_pre, norm_mlp_post=norm_mlp_post, w_up=w_up, w_down=w_down, loss_target=loss_target, m_norm_attn_pre=m_norm_attn_pre, m_norm_attn_post=m_norm_attn_post, m_w_in=m_w_in, m_q_latent_norm=m_q_latent_norm, m_kv_latent_norm=m_kv_latent_norm, m_w_uq=m_w_uq, m_w_ukv=m_w_ukv, m_w_out=m_w_out, m_norm_mlp_pre=m_norm_mlp_pre, m_norm_mlp_post=m_norm_mlp_post, m_w_up=m_w_up, m_w_down=m_w_down, v_norm_attn_pre=v_norm_attn_pre, v_norm_attn_post=v_norm_attn_post, v_w_in=v_w_in, v_q_latent_norm=v_q_latent_norm, v_kv_latent_norm=v_kv_latent_norm, v_w_uq=v_w_uq, v_w_ukv=v_w_ukv, v_w_out=v_w_out, v_norm_mlp_pre=v_norm_mlp_pre, v_norm_mlp_post=v_norm_mlp_post, v_w_up=v_w_up, v_w_down=v_w_down)
    weights = {n: given[n] for n in TWIN_WEIGHTS}
    shared = {n: given[n] for n in SHARED_INPUTS}
    per_example = {n: given[n] for n in ['x', 'positions']}
    grad_fn = _jax.value_and_grad(_loss, argnums=(0, 1))

    def one_microbatch(ex, loss_target):
        ex = dict(ex)
        diff = ex.pop(TWIN_DIFF_INPUT)
        return grad_fn(weights, diff, {**shared, **ex}, loss_target)

    if N_MICROBATCH == 1:
        loss, (grad_w, grad_x) = one_microbatch(per_example, given["loss_target"])
    else:
        def body(carry, xs):
            loss_sum, grad_sum = carry
            l_k, (gw_k, gx_k) = one_microbatch(xs[0], xs[1])
            with _jax.named_scope("update"):
                return (loss_sum + l_k, _jax.tree.map(_jnp.add, grad_sum, gw_k)), gx_k

        init = (_jnp.zeros((), _jnp.float32), _jax.tree.map(_jnp.zeros_like, weights))
        (loss, grad_w), grad_x = _jax.lax.scan(body, init, (per_example, given["loss_target"]))
    with _jax.named_scope("update"):
        delta_w, new_m, new_v = {}, {}, {}
        for n in TWIN_WEIGHTS:
            delta_w[n], new_m[n], new_v[n] = _adamw(weights[n], grad_w[n], given["m_" + n], given["v_" + n])
    return (loss, grad_x, *[grad_w[n] for n in TWIN_WEIGHTS], *[delta_w[n] for n in TWIN_WEIGHTS],
            *[new_m[n] for n in TWIN_WEIGHTS], *[new_v[n] for n in TWIN_WEIGHTS])
```

```python
import functools
import math

import jax
import jax.numpy as jnp
from jax import lax
from jax.experimental import pallas as pl
from jax.experimental.pallas import tpu as pltpu

F32 = jnp.float32
BF16 = jnp.bfloat16

N_DEV = 8
HEADS = 8
HD = 128
AW = HEADS * HD
Q_LORA = 512
KV_LORA = 512
ROPE_MLA = 64
ROT_A = 32
IN_COLS = 3 * AW + Q_LORA + KV_LORA + ROPE_MLA
IN_PAD = 3 * AW + Q_LORA + KV_LORA + HD
QBLK = 128
DSWA_DILATIONS = (1, 4, 16)
ROPE_THETA = 500000.0
NORM_EPS = 1e-6
NEG_INF = -1e30
SCALE_A = HD ** -0.5
SCALE_B = (HD + ROPE_MLA) ** -0.5

ADAM_LR = 0.001
ADAM_B1 = 0.9
ADAM_B2 = 0.999
ADAM_EPS = 1e-08
ADAM_WD = 0.01
ADAM_STEP = 10

VMEM_LIMIT = 48 * 1024 * 1024

NT = (((1,), (1,)), ((), ()))
NN = (((1,), (0,)), ((), ()))
TN = (((0,), (0,)), ((), ()))


def _dot(a, b, dims):
    return lax.dot_general(a, b, dims, preferred_element_type=F32)


def _params(*sem):
    return pltpu.CompilerParams(dimension_semantics=sem, vmem_limit_bytes=VMEM_LIMIT)


def _tile(n, want):
    t = min(n, want)
    while n % t:
        t //= 2
    return t


def _mm(a, b, mode, out_dtypes, name, epilogue=None, extras=(), tm=1024, tn=1024, tk=512):
    if mode == "tn":
        K, M = a.shape
    else:
        M, K = a.shape
    N = b.shape[0] if mode == "nt" else b.shape[1]
    tm, tk = _tile(M, tm), _tile(K, tk)
    if N % 128 == 0:
        units = N // 128
        tn = 128 * max(u for u in range(1, tn // 128 + 1) if units % u == 0)
    else:
        tn = N
    nk = K // tk
    dims = {"nn": NN, "nt": NT, "tn": TN}[mode]
    a_spec = (pl.BlockSpec((tk, tm), lambda i, j, k: (k, i)) if mode == "tn"
              else pl.BlockSpec((tm, tk), lambda i, j, k: (i, k)))
    b_spec = (pl.BlockSpec((tn, tk), lambda i, j, k: (j, k)) if mode == "nt"
              else pl.BlockSpec((tk, tn), lambda i, j, k: (k, j)))
    mn_spec = pl.BlockSpec((tm, tn), lambda i, j, k: (i, j))
    n_ex = len(extras)
    n_out = len(out_dtypes)

    def body(*refs):
        a_ref, b_ref = refs[:2]
        ex = refs[2:2 + n_ex]
        outs = refs[2 + n_ex:2 + n_ex + n_out]
        acc = refs[-1]
        k = pl.program_id(2)

        @pl.when(k == 0)
        def _():
            acc[...] = jnp.zeros_like(acc)

        acc[...] += _dot(a_ref[...], b_ref[...], dims)

        @pl.when(k == nk - 1)
        def _():
            res = (acc[...],) if epilogue is None else epilogue(acc[...], *[e[...] for e in ex])
            for o, r in zip(outs, res):
                o[...] = r.astype(o.dtype)

    out = pl.pallas_call(
        body, name=name, grid=(M // tm, N // tn, nk),
        in_specs=[a_spec, b_spec] + [mn_spec] * n_ex,
        out_specs=[mn_spec] * n_out,
        out_shape=[jax.ShapeDtypeStruct((M, N), dt) for dt in out_dtypes],
        scratch_shapes=[pltpu.VMEM((tm, tn), F32)],
        compiler_params=_params("parallel", "parallel", "arbitrary"),
    )(a, b, *extras)
    return out[0] if n_out == 1 else out


def _rms_fwd(x, gain, out_dtype, name, width=None, col_block=0, residual=None, tb=256):
    T = x.shape[0]
    W = x.shape[1] if width is None else width
    tb = _tile(T, tb)
    has_res = residual is not None

    def body(*refs):
        x_ref, g_ref = refs[:2]
        o_ref = refs[-1]
        xf = x_ref[...]
        y = xf * lax.rsqrt(jnp.mean(xf * xf, axis=-1, keepdims=True) + NORM_EPS) * g_ref[...]
        if has_res:
            y = refs[2][...] + y
        o_ref[...] = y.astype(o_ref.dtype)

    row = pl.BlockSpec((tb, W), lambda i: (i, 0))
    ins = [x, gain] + ([residual] if has_res else [])
    return pl.pallas_call(
        body, name=name, grid=(T // tb,),
        in_specs=[pl.BlockSpec((tb, W), lambda i: (i, col_block)),
                  pl.BlockSpec((1, W), lambda i: (0, 0))] + ([row] if has_res else []),
        out_specs=row, out_shape=jax.ShapeDtypeStruct((T, W), out_dtype),
        compiler_params=_params("parallel"),
    )(*ins)


def _rms_bwd(dy, x, gain, out_dtype, name, width=None, col_block=0, residual=None, tb=256):
    T = dy.shape[0]
    W = x.shape[1] if width is None else width
    tb = _tile(T, tb)
    has_res = residual is not None

    def body(*refs):
        dy_ref, x_ref, g_ref = refs[:3]
        dx_ref, dg_ref = refs[-2:]
        i = pl.program_id(0)
        xf = x_ref[...]
        r = lax.rsqrt(jnp.mean(xf * xf, axis=-1, keepdims=True) + NORM_EPS)
        xn = xf * r
        dyf = dy_ref[...].astype(F32)
        dyg = dyf * g_ref[...]
        dx = r * (dyg - xn * jnp.mean(dyg * xn, axis=-1, keepdims=True))
        if has_res:
            dx = refs[3][...] + dx
        dx_ref[...] = dx.astype(dx_ref.dtype)

        @pl.when(i == 0)
        def _():
            dg_ref[...] = jnp.zeros_like(dg_ref)

        dg_ref[...] += jnp.sum(dyf * xn, axis=0, keepdims=True)

    row = pl.BlockSpec((tb, W), lambda i: (i, 0))
    vec = pl.BlockSpec((1, W), lambda i: (0, 0))
    ins = [dy, x, gain] + ([residual] if has_res else [])
    return pl.pallas_call(
        body, name=name, grid=(T // tb,),
        in_specs=[row, pl.BlockSpec((tb, W), lambda i: (i, col_block)), vec] + ([row] if has_res else []),
        out_specs=[row, vec],
        out_shape=[jax.ShapeDtypeStruct((T, W), out_dtype), jax.ShapeDtypeStruct((1, W), F32)],
        compiler_params=_params("arbitrary"),
    )(*ins)


def _loss_head(x2, y2, gain, target, name, tb=256):
    T, D = x2.shape
    tb = _tile(T, tb)

    def body(x2_ref, y2_ref, g_ref, t_ref, dx3_ref, loss_ref):
        i = pl.program_id(0)
        yf = y2_ref[...]
        x3 = x2_ref[...] + yf * lax.rsqrt(jnp.mean(yf * yf, axis=-1, keepdims=True) + NORM_EPS) * g_ref[...]
        e = x3 - t_ref[...]
        dx3_ref[...] = e * (1.0 / D)

        @pl.when(i == 0)
        def _():
            loss_ref[...] = jnp.zeros_like(loss_ref)

        per_row = jnp.mean(e * e, axis=-1, keepdims=True)
        loss_ref[...] += 0.5 * jnp.sum(per_row, axis=0, keepdims=True)

    row = pl.BlockSpec((tb, D), lambda i: (i, 0))
    return pl.pallas_call(
        body, name=name, grid=(T // tb,),
        in_specs=[row, row, pl.BlockSpec((1, D), lambda i: (0, 0)), row],
        out_specs=[row, pl.BlockSpec((1, 1), lambda i: (0, 0))],
        out_shape=[jax.ShapeDtypeStruct((T, D), F32), jax.ShapeDtypeStruct((1, 1), F32)],
        compiler_params=_params("arbitrary"),
    )(x2, y2, gain, target)


def _rope_tables(positions, rot_dim):
    half = rot_dim // 2
    inv_freq = ROPE_THETA ** (-jnp.arange(0, rot_dim, 2, dtype=F32) / rot_dim)
    ang = positions.astype(F32)[:, None] * inv_freq[None, :]
    cos, sin = jnp.cos(ang), jnp.sin(ang)
    T = positions.shape[0]
    ones = jnp.ones((T, HD - rot_dim), F32)
    cos_t = jnp.concatenate([cos, cos, ones], axis=1)
    sin_t = jnp.concatenate([-sin, sin, jnp.zeros_like(ones)], axis=1)
    return cos_t, sin_t


def _rotate(x, cos_t, sin_t, half):
    lane = lax.broadcasted_iota(jnp.int32, x.shape, 1)
    swapped = jnp.where(lane < half, pltpu.roll(x, HD - half, 1), pltpu.roll(x, half, 1))
    return x * cos_t + swapped * sin_t


def _rope_apply(x, cos_t, sin_t, half, n_blocks, rope_lo, rope_hi, out_dtype, name, col_off=0, tb=512):
    T = x.shape[0]
    tb = _tile(T, tb)

    def body(x_ref, c_ref, s_ref, o_ref):
        j = pl.program_id(1)
        is_rope = jnp.logical_and(j >= rope_lo, j < rope_hi)

        @pl.when(is_rope)
        def _():
            o_ref[...] = _rotate(x_ref[...].astype(F32), c_ref[...], s_ref[...], half).astype(o_ref.dtype)

        @pl.when(jnp.logical_not(is_rope))
        def _():
            o_ref[...] = x_ref[...].astype(o_ref.dtype)

    tab = pl.BlockSpec((tb, HD), lambda i, j: (i, 0))
    return pl.pallas_call(
        body, name=name, grid=(T // tb, n_blocks),
        in_specs=[pl.BlockSpec((tb, HD), lambda i, j: (i, j + col_off)), tab, tab],
        out_specs=pl.BlockSpec((tb, HD), lambda i, j: (i, j)),
        out_shape=jax.ShapeDtypeStruct((T, n_blocks * HD), out_dtype),
        compiler_params=_params("parallel", "parallel"),
    )(x, cos_t, sin_t)


def _shared_key_grad(dkr_heads, cos_t, sin_t_neg, half, name, tb=512):
    T = dkr_heads.shape[0]
    tb = _tile(T, tb)

    def body(d_ref, c_ref, s_ref, o_ref):
        tot = d_ref[:, 0:HD]
        for h in range(1, HEADS):
            tot = tot + d_ref[:, h * HD:(h + 1) * HD]
        o_ref[...] = _rotate(tot, c_ref[...], s_ref[...], half).astype(o_ref.dtype)

    tab = pl.BlockSpec((tb, HD), lambda i: (i, 0))
    return pl.pallas_call(
        body, name=name, grid=(T // tb,),
        in_specs=[pl.BlockSpec((tb, AW), lambda i: (i, 0)), tab, tab],
        out_specs=tab, out_shape=jax.ShapeDtypeStruct((T, HD), BF16),
        compiler_params=_params("parallel"),
    )(dkr_heads, cos_t, sin_t_neg)


def _band_masks(n):
    row = lax.broadcasted_iota(jnp.int32, (QBLK, QBLK), 0)
    col = lax.broadcasted_iota(jnp.int32, (QBLK, QBLK), 1)
    return col <= row, jnp.logical_and(col >= row, n > 0)


def _dswa_specs(d):
    cur = pl.BlockSpec((QBLK, AW), lambda r, n: (n, r))
    prev = pl.BlockSpec((QBLK, AW), lambda r, n: (jnp.maximum(n - 1, 0), r))
    return cur, prev


def _dswa_fwd(qkv, d, name):
    T = qkv.shape[0]
    sd = T // d
    nb = sd // QBLK
    q = qkv[:, :AW].reshape(sd, d * AW)
    k = qkv[:, AW:2 * AW].reshape(sd, d * AW)
    v = qkv[:, 2 * AW:].reshape(sd, d * AW)

    def body(q_ref, kc_ref, kp_ref, vc_ref, vp_ref, o_ref, l_ref):
        cur_mask, prev_mask = _band_masks(pl.program_id(1))
        for h in range(HEADS):
            sl = slice(h * HD, (h + 1) * HD)
            qh = q_ref[:, sl]
            sc = jnp.where(cur_mask, _dot(qh, kc_ref[:, sl], NT) * SCALE_A, NEG_INF)
            sp = jnp.where(prev_mask, _dot(qh, kp_ref[:, sl], NT) * SCALE_A, NEG_INF)
            m = jnp.maximum(jnp.max(sc, axis=-1, keepdims=True), jnp.max(sp, axis=-1, keepdims=True))
            pc = jnp.exp(sc - m)
            pp = jnp.exp(sp - m)
            den = jnp.sum(pc, axis=-1, keepdims=True) + jnp.sum(pp, axis=-1, keepdims=True)
            o = (_dot((pc / den).astype(BF16), vc_ref[:, sl], NN)
                 + _dot((pp / den).astype(BF16), vp_ref[:, sl], NN))
            o_ref[:, sl] = o
            l_ref[:, sl] = jnp.broadcast_to(m + jnp.log(den), (QBLK, HD))

    cur, prev = _dswa_specs(d)
    o, lse = pl.pallas_call(
        body, name=name, grid=(d, nb),
        in_specs=[cur, cur, prev, cur, prev], out_specs=[cur, cur],
        out_shape=[jax.ShapeDtypeStruct((sd, d * AW), F32)] * 2,
        compiler_params=_params("parallel", "parallel"),
    )(q, k, k, v, v)
    return o.reshape(T, AW), lse.reshape(T, AW)


def _dswa_merge(outs, lses, name, tb=256):
    T = outs[0].shape[0]
    tb = _tile(T, tb)
    nc = len(outs)

    def body(*refs):
        o_refs, l_refs = refs[:nc], refs[nc:2 * nc]
        out_ref, outb_ref, lt_ref = refs[2 * nc:]
        ls = [l[...] for l in l_refs]
        m = functools.reduce(jnp.maximum, ls)
        es = [jnp.exp(l - m) for l in ls]
        tot = functools.reduce(lambda a, b: a + b, es)
        acc = (es[0] / tot) * o_refs[0][...]
        for c in range(1, nc):
            acc = acc + (es[c] / tot) * o_refs[c][...]
        out_ref[...] = acc
        outb_ref[...] = acc.astype(BF16)
        lt_ref[...] = m + jnp.log(tot)

    row = pl.BlockSpec((tb, AW), lambda i: (i, 0))
    return pl.pallas_call(
        body, name=name, grid=(T // tb,),
        in_specs=[row] * (2 * nc), out_specs=[row] * 3,
        out_shape=[jax.ShapeDtypeStruct((T, AW), F32), jax.ShapeDtypeStruct((T, AW), BF16),
                   jax.ShapeDtypeStruct((T, AW), F32)],
        compiler_params=_params("parallel"),
    )(*outs, *lses)


def _delta_prep(dout, col_block, out, name, tb=256):
    T = out.shape[0]
    tb = _tile(T, tb)

    def body(do_ref, o_ref, delta_ref, dob_ref):
        for h in range(HEADS):
            sl = slice(h * HD, (h + 1) * HD)
            doh = do_ref[:, sl]
            delta_ref[:, sl] = jnp.broadcast_to(jnp.sum(doh * o_ref[:, sl], axis=-1, keepdims=True), (tb, HD))
            dob_ref[:, sl] = doh.astype(BF16)

    row = pl.BlockSpec((tb, AW), lambda i: (i, 0))
    return pl.pallas_call(
        body, name=name, grid=(T // tb,),
        in_specs=[pl.BlockSpec((tb, AW), lambda i: (i, col_block)), row], out_specs=[row, row],
        out_shape=[jax.ShapeDtypeStruct((T, AW), F32), jax.ShapeDtypeStruct((T, AW), BF16)],
        compiler_params=_params("parallel"),
    )(dout, out)


def _dswa_bwd(qkv, dout_b, lse_tot, delta, d, name):
    T = qkv.shape[0]
    sd = T // d
    nb = sd // QBLK
    view = lambda a: a.reshape(sd, d * AW)
    q, k, v = view(qkv[:, :AW]), view(qkv[:, AW:2 * AW]), view(qkv[:, 2 * AW:])

    def body(q_ref, kc_ref, kp_ref, vc_ref, vp_ref, do_ref, l_ref, dl_ref,
             dq_ref, dkc_ref, dkp_ref, dvc_ref, dvp_ref):
        cur_mask, prev_mask = _band_masks(pl.program_id(1))
        for h in range(HEADS):
            sl = slice(h * HD, (h + 1) * HD)
            qh, kc, kp, vc, vp, doh = q_ref[:, sl], kc_ref[:, sl], kp_ref[:, sl], vc_ref[:, sl], vp_ref[:, sl], do_ref[:, sl]
            lse = l_ref[:, sl][:, 0:1]
            dlt = dl_ref[:, sl][:, 0:1]
            sc = jnp.where(cur_mask, _dot(qh, kc, NT) * SCALE_A, NEG_INF)
            sp = jnp.where(prev_mask, _dot(qh, kp, NT) * SCALE_A, NEG_INF)
            pc = jnp.exp(sc - lse)
            pp = jnp.exp(sp - lse)
            dsc = (pc * (_dot(doh, vc, NT) - dlt) * SCALE_A).astype(BF16)
            dsp = (pp * (_dot(doh, vp, NT) - dlt) * SCALE_A).astype(BF16)
            dq_ref[:, sl] = _dot(dsc, kc, NN) + _dot(dsp, kp, NN)
            dkc_ref[:, sl] = _dot(dsc, qh, TN)
            dkp_ref[:, sl] = _dot(dsp, qh, TN)
            dvc_ref[:, sl] = _dot(pc.astype(BF16), doh, TN)
            dvp_ref[:, sl] = _dot(pp.astype(BF16), doh, TN)

    cur, prev = _dswa_specs(d)
    res = pl.pallas_call(
        body, name=name, grid=(d, nb),
        in_specs=[cur, cur, prev, cur, prev, cur, cur, cur], out_specs=[cur] * 5,
        out_shape=[jax.ShapeDtypeStruct((sd, d * AW), F32)] * 5,
        compiler_params=_params("parallel", "parallel"),
    )(q, k, k, v, v, view(dout_b), view(lse_tot), view(delta))
    return [r.reshape(T, AW) for r in res]


def _dswa_combine(grads, cos_t, sin_t_neg, name):
    T = grads[0][0].shape[0]
    nbt = T // QBLK
    half = ROT_A // 2

    def body(*refs):
        c_ref, s_ref = refs[-5:-3]
        dq_ref, dk_ref, dv_ref = refs[-3:]
        i = pl.program_id(0)
        dq = dk = dv = None
        for ci, d in enumerate(DSWA_DILATIONS):
            gq, gkc, gkp, gvc, gvp = refs[5 * ci:5 * ci + 5]
            has_next = i + d < nbt
            gk = gkc[...] + jnp.where(has_next, gkp[...], 0.0)
            gv = gvc[...] + jnp.where(has_next, gvp[...], 0.0)
            dq = gq[...] if dq is None else dq + gq[...]
            dk = gk if dk is None else dk + gk
            dv = gv if dv is None else dv + gv
        for h in range(HEADS):
            sl = slice(h * HD, (h + 1) * HD)
            dq_ref[:, sl] = _rotate(dq[:, sl], c_ref[...], s_ref[...], half).astype(BF16)
            dk_ref[:, sl] = _rotate(dk[:, sl], c_ref[...], s_ref[...], half).astype(BF16)
        dv_ref[...] = dv.astype(BF16)

    row = pl.BlockSpec((QBLK, AW), lambda i: (i, 0))
    tab = pl.BlockSpec((QBLK, HD), lambda i: (i, 0))
    in_specs, ins = [], []
    for d, g in zip(DSWA_DILATIONS, grads):
        nxt = pl.BlockSpec((QBLK, AW), functools.partial(lambda i, d: (jnp.minimum(i + d, nbt - 1), 0), d=d))
        in_specs += [row, row, nxt, row, nxt]
        ins += list(g)
    return pl.pallas_call(
        body, name=name, grid=(nbt,),
        in_specs=in_specs + [tab, tab], out_specs=[row] * 3,
        out_shape=[jax.ShapeDtypeStruct((T, AW), BF16)] * 3,
        compiler_params=_params("parallel"),
    )(*ins, cos_t, sin_t_neg)


MLA_TQ = 512


def _mla_scores(qn, qr, kn, kr, qi, ki, tq):
    s = (_dot(qn, kn, NT) + _dot(qr, kr, NT)) * SCALE_B
    row = lax.broadcasted_iota(jnp.int32, s.shape, 0) + qi * tq
    col = lax.broadcasted_iota(jnp.int32, s.shape, 1) + ki * tq
    return jnp.where(col <= row, s, NEG_INF)


def _mla_fwd(q, kv, kr, name):
    T = q.shape[0]
    tq = _tile(T, MLA_TQ)
    nq = T // tq

    def body(qn_ref, qr_ref, kn_ref, v_ref, kr_ref, o_ref, l_ref, m_s, l_s, acc):
        qi, ki = pl.program_id(1), pl.program_id(2)

        @pl.when(ki == 0)
        def _():
            m_s[...] = jnp.full_like(m_s, NEG_INF)
            l_s[...] = jnp.zeros_like(l_s)
            acc[...] = jnp.zeros_like(acc)

        @pl.when(ki <= qi)
        def _():
            s = _mla_scores(qn_ref[...], qr_ref[...], kn_ref[...], kr_ref[...], qi, ki, tq)
            m_new = jnp.maximum(m_s[...], jnp.max(s, axis=-1, keepdims=True))
            alpha = jnp.exp(m_s[...] - m_new)
            p = jnp.exp(s - m_new)
            l_s[...] = alpha * l_s[...] + jnp.sum(p, axis=-1, keepdims=True)
            acc[...] = alpha * acc[...] + _dot(p.astype(BF16), v_ref[...], NN)
            m_s[...] = m_new

        @pl.when(ki == nq - 1)
        def _():
            o_ref[...] = acc[...] / l_s[...]
            l_ref[...] = jnp.broadcast_to(m_s[...] + jnp.log(l_s[...]), (tq, HD))

    kidx = lambda h, qi, ki: jnp.minimum(ki, qi)
    return pl.pallas_call(
        body, name=name, grid=(HEADS, nq, nq),
        in_specs=[pl.BlockSpec((tq, HD), lambda h, qi, ki: (qi, h)),
                  pl.BlockSpec((tq, HD), lambda h, qi, ki: (qi, HEADS + h)),
                  pl.BlockSpec((tq, HD), lambda h, qi, ki: (kidx(h, qi, ki), h)),
                  pl.BlockSpec((tq, HD), lambda h, qi, ki: (kidx(h, qi, ki), HEADS + h)),
                  pl.BlockSpec((tq, HD), lambda h, qi, ki: (kidx(h, qi, ki), 0))],
        out_specs=[pl.BlockSpec((tq, HD), lambda h, qi, ki: (qi, h))] * 2,
        out_shape=[jax.ShapeDtypeStruct((T, AW), F32)] * 2,
        scratch_shapes=[pltpu.VMEM((tq, 1), F32), pltpu.VMEM((tq, 1), F32), pltpu.VMEM((tq, HD), F32)],
        compiler_params=_params("parallel", "parallel", "arbitrary"),
    )(q, q, kv, kv, kr)


def _mla_ds(qn, qr, kn, kr, v, do, lse, dlt, qi, ki, tq):
    s = _mla_scores(qn, qr, kn, kr, qi, ki, tq)
    p = jnp.exp(s - lse)
    ds = (p * (_dot(do, v, NT) - dlt) * SCALE_B).astype(BF16)
    return p, ds


def _mla_bwd_q(q, kv, kr, dout_b, lse, delta, name):
    T = q.shape[0]
    tq = _tile(T, MLA_TQ)
    nq = T // tq

    def body(qn_ref, qr_ref, kn_ref, v_ref, kr_ref, do_ref, l_ref, dl_ref, dqn_ref, dqr_ref, an, ar):
        qi, ki = pl.program_id(1), pl.program_id(2)

        @pl.when(ki == 0)
        def _():
            an[...] = jnp.zeros_like(an)
            ar[...] = jnp.zeros_like(ar)

        @pl.when(ki <= qi)
        def _():
            _, ds = _mla_ds(qn_ref[...], qr_ref[...], kn_ref[...], kr_ref[...], v_ref[...], do_ref[...],
                            l_ref[:, 0:1], dl_ref[:, 0:1], qi, ki, tq)
            an[...] += _dot(ds, kn_ref[...], NN)
            ar[...] += _dot(ds, kr_ref[...], NN)

        @pl.when(ki == nq - 1)
        def _():
            dqn_ref[...] = an[...]
            dqr_ref[...] = ar[...]

    kidx = lambda h, qi, ki: jnp.minimum(ki, qi)
    qspec = pl.BlockSpec((tq, HD), lambda h, qi, ki: (qi, h))
    return pl.pallas_call(
        body, name=name, grid=(HEADS, nq, nq),
        in_specs=[qspec,
                  pl.BlockSpec((tq, HD), lambda h, qi, ki: (qi, HEADS + h)),
                  pl.BlockSpec((tq, HD), lambda h, qi, ki: (kidx(h, qi, ki), h)),
                  pl.BlockSpec((tq, HD), lambda h, qi, ki: (kidx(h, qi, ki), HEADS + h)),
                  pl.BlockSpec((tq, HD), lambda h, qi, ki: (kidx(h, qi, ki), 0)),
                  qspec, qspec, qspec],
        out_specs=[qspec, qspec],
        out_shape=[jax.ShapeDtypeStruct((T, AW), F32)] * 2,
        scratch_shapes=[pltpu.VMEM((tq, HD), F32)] * 2,
        compiler_params=_params("parallel", "parallel", "arbitrary"),
    )(q, q, kv, kv, kr, dout_b, lse, delta)


def _mla_bwd_kv(q, kv, kr, dout_b, lse, delta, name):
    T = q.shape[0]
    tq = _tile(T, MLA_TQ)
    nq = T // tq

    def body(qn_ref, qr_ref, kn_ref, v_ref, kr_ref, do_ref, l_ref, dl_ref, dkn_ref, dv_ref, dkr_ref, akn, av, akr):
        ki, qi = pl.program_id(1), pl.program_id(2)

        @pl.when(qi == 0)
        def _():
            akn[...] = jnp.zeros_like(akn)
            av[...] = jnp.zeros_like(av)
            akr[...] = jnp.zeros_like(akr)

        @pl.when(qi >= ki)
        def _():
            p, ds = _mla_ds(qn_ref[...], qr_ref[...], kn_ref[...], kr_ref[...], v_ref[...], do_ref[...],
                            l_ref[:, 0:1], dl_ref[:, 0:1], qi, ki, tq)
            av[...] += _dot(p.astype(BF16), do_ref[...], TN)
            akn[...] += _dot(ds, qn_ref[...], TN)
            akr[...] += _dot(ds, qr_ref[...], TN)

        @pl.when(qi == nq - 1)
        def _():
            dkn_ref[...] = akn[...]
            dv_ref[...] = av[...]
            dkr_ref[...] = akr[...]

    qidx = lambda h, ki, qi: jnp.maximum(qi, ki)
    qspec = pl.BlockSpec((tq, HD), lambda h, ki, qi: (qidx(h, ki, qi), h))
    kspec = pl.BlockSpec((tq, HD), lambda h, ki, qi: (ki, h))
    return pl.pallas_call(
        body, name=name, grid=(HEADS, nq, nq),
        in_specs=[qspec,
                  pl.BlockSpec((tq, HD), lambda h, ki, qi: (qidx(h, ki, qi), HEADS + h)),
                  kspec,
                  pl.BlockSpec((tq, HD), lambda h, ki, qi: (ki, HEADS + h)),
                  pl.BlockSpec((tq, HD), lambda h, ki, qi: (ki, 0)),
                  qspec, qspec, qspec],
        out_specs=[kspec] * 3,
        out_shape=[jax.ShapeDtypeStruct((T, AW), F32)] * 3,
        scratch_shapes=[pltpu.VMEM((tq, HD), F32)] * 3,
        compiler_params=_params("parallel", "parallel", "arbitrary"),
    )(q, q, kv, kv, kr, dout_b, lse, delta)


def _exchange(arrs, scatter, name):
    n = len(arrs)

    def body(*refs):
        ins, outs = refs[:n], refs[n:2 * n]
        send_sems, recv_sems, local_sems = refs[2 * n:]
        x, y, c = lax.axis_index("x"), lax.axis_index("y"), lax.axis_index("c")
        me = 4 * x + 2 * y + c
        copies = []
        for a in range(n):
            src = ins[a].at[me] if scatter else ins[a]
            cp = pltpu.make_async_copy(src, outs[a].at[me], local_sems.at[a])
            cp.start()
            copies.append(cp)
        remote = []
        for a in range(n):
            for rel in range(1, N_DEV):
                px = 1 - x if rel & 4 else x
                py = 1 - y if rel & 2 else y
                pc = 1 - c if rel & 1 else c
                peer = 4 * px + 2 * py + pc
                k = a * (N_DEV - 1) + rel - 1
                rc = pltpu.make_async_remote_copy(
                    src_ref=ins[a].at[peer] if scatter else ins[a],
                    dst_ref=outs[a].at[me],
                    send_sem=send_sems.at[k], recv_sem=recv_sems.at[k],
                    device_id=(px, py, pc), device_id_type=pl.DeviceIdType.MESH)
                rc.start()
                remote.append(rc)
        for rc in remote:
            rc.wait_send()
        for a in range(n):
            for rel in range(1, N_DEV):
                px = 1 - x if rel & 4 else x
                py = 1 - y if rel & 2 else y
                pc = 1 - c if rel & 1 else c
                peer = 4 * px + 2 * py + pc
                k = a * (N_DEV - 1) + rel - 1
                pltpu.make_async_remote_copy(
                    src_ref=ins[a].at[peer] if scatter else ins[a],
                    dst_ref=outs[a].at[peer],
                    send_sem=send_sems.at[k], recv_sem=recv_sems.at[k],
                    device_id=(px, py, pc), device_id_type=pl.DeviceIdType.MESH).wait_recv()
        for cp in copies:
            cp.wait()

    hbm = pl.BlockSpec(memory_space=pltpu.HBM)
    out_shape = [jax.ShapeDtypeStruct(a.shape if scatter else (N_DEV,) + a.shape, a.dtype) for a in arrs]
    return pl.pallas_call(
        body, name=name, in_specs=[hbm] * n, out_specs=[hbm] * n, out_shape=out_shape,
        scratch_shapes=[pltpu.SemaphoreType.DMA((n * (N_DEV - 1),)),
                        pltpu.SemaphoreType.DMA((n * (N_DEV - 1),)),
                        pltpu.SemaphoreType.DMA((n,))],
        compiler_params=pltpu.CompilerParams(has_side_effects=True),
    )(*arrs)


def _adamw(parts, w, m, v, name, tb=128):
    R, C = w.shape
    tb = _tile(R, tb)
    c1 = 1.0 - ADAM_B1
    c2 = 1.0 - ADAM_B2
    bc1 = 1.0 - ADAM_B1 ** ADAM_STEP
    bc2 = 1.0 - ADAM_B2 ** ADAM_STEP

    def body(p_ref, w_ref, m_ref, v_ref, g_ref, d_ref, nm_ref, nv_ref):
        g = p_ref[0].astype(F32)
        for j in range(1, N_DEV):
            g = g + p_ref[j].astype(F32)
        nm = ADAM_B1 * m_ref[...] + c1 * g
        nv = ADAM_B2 * v_ref[...] + c2 * (g * g)
        g_ref[...] = g
        nm_ref[...] = nm
        nv_ref[...] = nv
        d_ref[...] = -ADAM_LR * ((nm / bc1) / (jnp.sqrt(nv / bc2) + ADAM_EPS) + ADAM_WD * w_ref[...])

    row = pl.BlockSpec((tb, C), lambda i: (i, 0))
    return pl.pallas_call(
        body, name=name, grid=(R // tb,),
        in_specs=[pl.BlockSpec((N_DEV, tb, C), lambda i: (0, i, 0)), row, row, row],
        out_specs=[row] * 4, out_shape=[jax.ShapeDtypeStruct((R, C), F32)] * 4,
        compiler_params=_params("parallel"),
    )(parts, w, m, v)


def _cols_from_shards(g):
    return jnp.transpose(g, (1, 0, 2)).reshape(g.shape[1], N_DEV * g.shape[2])


def _cols_to_shards(w):
    return jnp.transpose(w.reshape(w.shape[0], N_DEV, w.shape[1] // N_DEV), (1, 0, 2))


def _split_heads(w, first):
    w3 = w.reshape(w.shape[0], HEADS, -1)
    return w3[:, :, :first].reshape(w.shape[0], -1), w3[:, :, first:].reshape(w.shape[0], -1)


def _join_heads(a, b):
    R = a.shape[0]
    return jnp.concatenate([a.reshape(R, HEADS, -1), b.reshape(R, HEADS, -1)], axis=2).reshape(R, -1)


def _pad_heads(w, width):
    w3 = w.reshape(w.shape[0], HEADS, -1)
    return jnp.pad(w3, ((0, 0), (0, 0), (0, width - w3.shape[2]))).reshape(w.shape[0], HEADS * width)


def _unpad_heads(w, k):
    return w.reshape(w.shape[0], HEADS, -1)[:, :, :k].reshape(w.shape[0], HEADS * k)


def kernel(x, positions, norm_attn_pre, norm_attn_post, w_in, q_latent_norm, kv_latent_norm, w_uq, w_ukv, w_out, norm_mlp_pre, norm_mlp_post, w_up, w_down, loss_target, m_norm_attn_pre, m_norm_attn_post, m_w_in, m_q_latent_norm, m_kv_latent_norm, m_w_uq, m_w_ukv, m_w_out, m_norm_mlp_pre, m_norm_mlp_post, m_w_up, m_w_down, v_norm_attn_pre, v_norm_attn_post, v_w_in, v_q_latent_norm, v_kv_latent_norm, v_w_uq, v_w_ukv, v_w_out, v_norm_mlp_pre, v_norm_mlp_post, v_w_up, v_w_down):
    xs = x[0]
    tgt = loss_target[0]
    pos = positions[0]
    T, D = xs.shape
    big = dict(w_in=(w_in, m_w_in, v_w_in), w_uq=(w_uq, m_w_uq, v_w_uq), w_ukv=(w_ukv, m_w_ukv, v_w_ukv),
               w_out=(w_out, m_w_out, v_w_out), w_up=(w_up, m_w_up, v_w_up), w_down=(w_down, m_w_down, v_w_down))
    big = {n: tuple(t[0] for t in ts) for n, ts in big.items()}
    big_names = ["w_in", "w_uq", "w_ukv", "w_out", "w_up", "w_down"]
    col_sharded = {"w_in", "w_uq", "w_ukv", "w_up"}

    gathered = _exchange([big[n][0].astype(BF16) for n in big_names], False, "gather_weights")
    full = {}
    for n, g in zip(big_names, gathered):
        full[n] = _cols_from_shards(g) if n in col_sharded else g.reshape(-1, g.shape[2])
    Wi = jnp.pad(full["w_in"], ((0, 0), (0, IN_PAD - IN_COLS)))
    uq_n, uq_r = _split_heads(full["w_uq"], HD)
    Wuq = jnp.concatenate([uq_n, _pad_heads(uq_r, HD)], axis=1)
    Wukv = jnp.concatenate(_split_heads(full["w_ukv"], HD), axis=1)
    Wo, Wup, Wdn = full["w_out"], full["w_up"], full["w_down"]

    cos_a, sin_a = _rope_tables(pos, ROT_A)
    cos_b, sin_b = _rope_tables(pos, ROPE_MLA)

    h1 = _rms_fwd(xs, norm_attn_pre, BF16, "norm_attn_pre_fwd")
    proj = _mm(h1, Wi, "nn", [F32], "proj_in", tn=1408)
    qkv = _rope_apply(proj, cos_a, sin_a, ROT_A // 2, 3 * HEADS, 0, 2 * HEADS, BF16, "rope_dswa")
    outs, lses = [], []
    for d in DSWA_DILATIONS:
        o, l = _dswa_fwd(qkv, d, f"dswa_fwd_d{d}")
        outs.append(o)
        lses.append(l)
    a_out, a_out_b, a_lse = _dswa_merge(outs, lses, "dswa_merge")

    cqn = _rms_fwd(proj, q_latent_norm, BF16, "q_latent_norm_fwd", width=Q_LORA, col_block=3 * AW // Q_LORA)
    ckvn = _rms_fwd(proj, kv_latent_norm, BF16, "kv_latent_norm_fwd", width=KV_LORA, col_block=3 * AW // KV_LORA + 1)
    qb = _mm(cqn, Wuq, "nn", [F32], "q_up")
    kvb = _mm(ckvn, Wukv, "nn", [BF16], "kv_up")
    q_mla = _rope_apply(qb, cos_b, sin_b, ROPE_MLA // 2, 2 * HEADS, HEADS, 2 * HEADS, BF16, "rope_mla_q")
    kr = _rope_apply(proj, cos_b, sin_b, ROPE_MLA // 2, 1, 0, 1, BF16, "rope_mla_k", col_off=(IN_PAD - HD) // HD)
    b_out, b_lse = _mla_fwd(q_mla, kvb, kr, "mla_fwd")

    mixed = jnp.concatenate([a_out_b, b_out.astype(BF16)], axis=1)
    y1 = _mm(mixed, Wo, "nn", [F32], "attn_out")
    x2 = _rms_fwd(y1, norm_attn_post, F32, "norm_attn_post_fwd", residual=xs)

    h2 = _rms_fwd(x2, norm_mlp_pre, BF16, "norm_mlp_pre_fwd")

    def relu2(z):
        r = jnp.maximum(z, 0.0)
        return r * r, r

    u, zr = _mm(h2, Wup, "nn", [BF16, BF16], "mlp_up", epilogue=relu2)
    y2 = _mm(u, Wdn, "nn", [F32], "mlp_down")
    dx3, loss_part = _loss_head(x2, y2, norm_mlp_post, tgt, "loss_head")

    dy2, dg_mlp_post = _rms_bwd(dx3, y2, norm_mlp_post, BF16, "norm_mlp_post_bwd")
    dz = _mm(dy2, Wdn, "nt", [BF16], "mlp_down_dx", epilogue=lambda du, r: (du * (2.0 * r.astype(F32)),), extras=(zr,))
    g_down = _mm(u, dy2, "tn", [BF16], "mlp_down_dw")
    g_up = _mm(h2, dz, "tn", [BF16], "mlp_up_dw")
    dh2 = _mm(dz, Wup, "nt", [F32], "mlp_up_dx")
    dx2, dg_mlp_pre = _rms_bwd(dh2, x2, norm_mlp_pre, F32, "norm_mlp_pre_bwd", residual=dx3)

    dy1, dg_attn_post = _rms_bwd(dx2, y1, norm_attn_post, BF16, "norm_attn_post_bwd")
    dmixed = _mm(dy1, Wo, "nt", [F32], "attn_out_dx")
    g_out = _mm(mixed, dy1, "tn", [BF16], "attn_out_dw")

    b_delta, b_dout = _delta_prep(dmixed, 1, b_out, "mla_delta")
    dqn, dqr = _mla_bwd_q(q_mla, kvb, kr, b_dout, b_lse, b_delta, "mla_bwd_q")
    dkn, dvb, dkr = _mla_bwd_kv(q_mla, kvb, kr, b_dout, b_lse, b_delta, "mla_bwd_kv")
    dqb = _rope_apply(jnp.concatenate([dqn, dqr], axis=1), cos_b, -sin_b, ROPE_MLA // 2, 2 * HEADS, HEADS, 2 * HEADS,
                      BF16, "rope_mla_q_bwd")
    dkvb = jnp.concatenate([dkn, dvb], axis=1).astype(BF16)
    d_kr = _shared_key_grad(dkr, cos_b, -sin_b, ROPE_MLA // 2, "rope_mla_k_bwd")
    g_uq_pad = _mm(cqn, dqb, "tn", [BF16], "q_up_dw")
    g_ukv_perm = _mm(ckvn, dkvb, "tn", [BF16], "kv_up_dw")
    dcqn = _mm(dqb, Wuq, "nt", [F32], "q_up_dx")
    dckvn = _mm(dkvb, Wukv, "nt", [F32], "kv_up_dx")
    d_cq, dg_q = _rms_bwd(dcqn, proj, q_latent_norm, BF16, "q_latent_norm_bwd", width=Q_LORA, col_block=3 * AW // Q_LORA)
    d_ckv, dg_kv = _rms_bwd(dckvn, proj, kv_latent_norm, BF16, "kv_latent_norm_bwd", width=KV_LORA,
                            col_block=3 * AW // KV_LORA + 1)

    a_delta, a_dout = _delta_prep(dmixed, 0, a_out, "dswa_delta")
    a_grads = [_dswa_bwd(qkv, a_dout, a_lse, a_delta, d, f"dswa_bwd_d{d}") for d in DSWA_DILATIONS]
    d_aq, d_ak, d_av = _dswa_combine(a_grads, cos_a, -sin_a, "dswa_combine")

    dproj = jnp.concatenate([d_aq, d_ak, d_av, d_cq, d_ckv, d_kr], axis=1)
    g_in_pad = _mm(h1, dproj, "tn", [BF16], "proj_in_dw", tn=1408)
    dh1 = _mm(dproj, Wi, "nt", [F32], "proj_in_dx")
    grad_x, dg_attn_pre = _rms_bwd(dh1, xs, norm_attn_pre, F32, "norm_attn_pre_bwd", residual=dx2)

    g_uq = _join_heads(g_uq_pad[:, :AW], _unpad_heads(g_uq_pad[:, AW:], ROPE_MLA))
    g_ukv = _join_heads(g_ukv_perm[:, :AW], g_ukv_perm[:, AW:])
    full_grads = dict(w_in=g_in_pad[:, :IN_COLS], w_uq=g_uq, w_ukv=g_ukv, w_out=g_out, w_up=g_up, w_down=g_down)
    chunks = []
    for n in big_names:
        g = full_grads[n]
        chunks.append(_cols_to_shards(g) if n in col_sharded else g.reshape(N_DEV, g.shape[0] // N_DEV, g.shape[1]))
    parts = _exchange(chunks, True, "scatter_grads")
    big_out = {n: _adamw(p, *big[n], f"adamw_{n}") for n, p in zip(big_names, parts)}

    gain_names = ["norm_attn_pre", "norm_attn_post", "q_latent_norm", "kv_latent_norm", "norm_mlp_pre", "norm_mlp_post"]
    gain_args = dict(norm_attn_pre=(norm_attn_pre, m_norm_attn_pre, v_norm_attn_pre),
                     norm_attn_post=(norm_attn_post, m_norm_attn_post, v_norm_attn_post),
                     q_latent_norm=(q_latent_norm, m_q_latent_norm, v_q_latent_norm),
                     kv_latent_norm=(kv_latent_norm, m_kv_latent_norm, v_kv_latent_norm),
                     norm_mlp_pre=(norm_mlp_pre, m_norm_mlp_pre, v_norm_mlp_pre),
                     norm_mlp_post=(norm_mlp_post, m_norm_mlp_post, v_norm_mlp_post))
    gain_grads = dict(norm_attn_pre=dg_attn_pre, norm_attn_post=dg_attn_post, q_latent_norm=dg_q,
                      kv_latent_norm=dg_kv, norm_mlp_pre=dg_mlp_pre, norm_mlp_post=dg_mlp_post)
    packed = jnp.concatenate([gain_grads[n] for n in gain_names], axis=1)
    gain_parts = _exchange([packed], False, "gather_gain_grads")[0]
    pack3 = lambda i: jnp.concatenate([gain_args[n][i] for n in gain_names], axis=1)
    gain_out = _adamw(gain_parts, pack3(0), pack3(1), pack3(2), "adamw_gains", tb=1)
    offs = [0]
    for n in gain_names:
        offs.append(offs[-1] + gain_args[n][0].shape[1])
    small_out = {n: tuple(o[:, offs[i]:offs[i + 1]] for o in gain_out) for i, n in enumerate(gain_names)}

    loss = lax.psum(loss_part[0, 0], ("x", "y", "c"))

    order = ["norm_attn_pre", "norm_attn_post", "w_in", "q_latent_norm", "kv_latent_norm", "w_uq", "w_ukv", "w_out",
             "norm_mlp_pre", "norm_mlp_post", "w_up", "w_down"]
    res = {n: (small_out[n] if n in small_out else tuple(o[None] for o in big_out[n])) for n in order}
    return (loss, grad_x[None], *[res[n][0] for n in order], *[res[n][1] for n in order],
            *[res[n][2] for n in order], *[res[n][3] for n in order])
```

```python
import functools
import math

import jax
import jax.numpy as jnp
from jax import lax
from jax.experimental import pallas as pl
from jax.experimental.pallas import tpu as pltpu

F32 = jnp.float32
BF16 = jnp.bfloat16

N_DEV = 8
HEADS = 8
HD = 128
AW = HEADS * HD
Q_LORA = 512
KV_LORA = 512
ROPE_MLA = 64
ROT_A = 32
IN_COLS = 3 * AW + Q_LORA + KV_LORA + ROPE_MLA
IN_PAD = 3 * AW + Q_LORA + KV_LORA + HD
QBLK = 128
DSWA_DILATIONS = (1, 4, 16)
ROPE_THETA = 500000.0
NORM_EPS = 1e-6
NEG_INF = -1e30
SCALE_A = HD ** -0.5
SCALE_B = (HD + ROPE_MLA) ** -0.5

ADAM_LR = 0.001
ADAM_B1 = 0.9
ADAM_B2 = 0.999
ADAM_EPS = 1e-08
ADAM_WD = 0.01
ADAM_STEP = 10

VMEM_LIMIT = 48 * 1024 * 1024

NT = (((1,), (1,)), ((), ()))
NN = (((1,), (0,)), ((), ()))
TN = (((0,), (0,)), ((), ()))


def _dot(a, b, dims):
    return lax.dot_general(a, b, dims, preferred_element_type=F32)


def _params(*sem):
    return pltpu.CompilerParams(dimension_semantics=sem, vmem_limit_bytes=VMEM_LIMIT)


def _tile(n, want):
    t = min(n, want)
    while n % t:
        t //= 2
    return t


def _tile128(n, want):
    if n % 128:
        return n
    units = n // 128
    return 128 * max(u for u in range(1, max(want // 128, 1) + 1) if units % u == 0)


class _Exchange:
    def __init__(self, arrs, scatter):
        self.arrs = list(arrs)
        self.scatter = scatter
        self.n = len(self.arrs)
        self.results = None
        hbm = pl.BlockSpec(memory_space=pltpu.HBM)
        self.specs = [hbm] * self.n
        self.out_shape = [jax.ShapeDtypeStruct(a.shape if scatter else (N_DEV,) + a.shape, a.dtype)
                          for a in self.arrs]
        n_sem = self.n * (N_DEV - 1)
        self.scratch = [pltpu.SemaphoreType.DMA((n_sem,)), pltpu.SemaphoreType.DMA((n_sem,)),
                        pltpu.SemaphoreType.DMA((self.n,))]

    def hooks(self, ins, outs, send_sems, recv_sems, local_sems):
        x, y, c = lax.axis_index("x"), lax.axis_index("y"), lax.axis_index("c")
        me = (x, y, c)
        sib = (x, y, 1 - c)
        chips = [(1 - x, y), (x, 1 - y), (1 - x, 1 - y)]
        slot = lambda p: 4 * p[0] + 2 * p[1] + p[2]

        def rcopy(a, k, src, dst, to):
            i = a * (N_DEV - 1) + k
            return pltpu.make_async_remote_copy(src_ref=src, dst_ref=dst, send_sem=send_sems.at[i],
                                                recv_sem=recv_sems.at[i], device_id=to,
                                                device_id_type=pl.DeviceIdType.MESH)

        def local(a):
            src = ins[a].at[slot(me)] if self.scatter else ins[a]
            return pltpu.make_async_copy(src, outs[a].at[slot(me)], local_sems.at[a])

        def peer(rel):
            return (1 - x if rel & 4 else x, 1 - y if rel & 2 else y, 1 - c if rel & 1 else c)

        if self.scatter:
            def start():
                for a in range(self.n):
                    local(a).start()
                    for rel in range(1, N_DEV):
                        rcopy(a, rel - 1, ins[a].at[slot(peer(rel))], outs[a].at[slot(me)], peer(rel)).start()

            def middle():
                pass

            def finish():
                for a in range(self.n):
                    for rel in range(1, N_DEV):
                        cp = rcopy(a, rel - 1, ins[a].at[slot(peer(rel))], outs[a].at[slot(peer(rel))], peer(rel))
                        cp.wait_send()
                        cp.wait_recv()
                    local(a).wait()
        else:
            def start():
                for a in range(self.n):
                    local(a).start()
                    rcopy(a, 0, ins[a], outs[a].at[slot(me)], sib).start()
                    for j, chip in enumerate(chips):
                        rcopy(a, 1 + j, ins[a], outs[a].at[slot(me)], (*chip, c)).start()

            def middle():
                for a in range(self.n):
                    for j, chip in enumerate(chips):
                        landed = outs[a].at[slot((*chip, c))]
                        rcopy(a, 1 + j, ins[a], landed, me).wait_recv()
                        rcopy(a, 4 + j, landed, landed, sib).start()

            def finish():
                for a in range(self.n):
                    rcopy(a, 0, ins[a], outs[a].at[slot(sib)], me).wait_recv()
                    for j, chip in enumerate(chips):
                        rcopy(a, 4 + j, ins[a], outs[a].at[slot((*chip, 1 - c))], me).wait_recv()
                    for k in range(N_DEV - 1):
                        rcopy(a, k, ins[a], outs[a].at[slot(me)], me).wait_send()
                    local(a).wait()

        return start, middle, finish

    def standalone(self, name):
        n = self.n

        def body(*refs):
            start, middle, finish = self.hooks(refs[:n], refs[n:2 * n], *refs[2 * n:])
            start()
            middle()
            finish()

        self.results = pl.pallas_call(
            body, name=name, in_specs=self.specs, out_specs=self.specs, out_shape=self.out_shape,
            scratch_shapes=self.scratch, compiler_params=pltpu.CompilerParams(has_side_effects=True),
        )(*self.arrs)
        return self.results


def _call(body, name, grid, in_specs, out_specs, out_shape, args, scratch=(), sem=(), comm=None):
    if comm is None:
        return pl.pallas_call(body, name=name, grid=grid, in_specs=in_specs, out_specs=out_specs,
                              out_shape=out_shape, scratch_shapes=list(scratch),
                              compiler_params=_params(*sem))(*args)
    ni, no, ns, n = len(in_specs), len(out_specs), len(scratch), comm.n
    steps = math.prod(grid)

    def wrapped(*refs):
        ins, c_ins = refs[:ni], refs[ni:ni + n]
        outs, c_outs = refs[ni + n:ni + n + no], refs[ni + n + no:ni + 2 * n + no]
        scr, c_scr = refs[ni + 2 * n + no:ni + 2 * n + no + ns], refs[ni + 2 * n + no + ns:]
        start, middle, finish = comm.hooks(c_ins, c_outs, *c_scr)
        step = pl.program_id(0)
        for ax in range(1, len(grid)):
            step = step * grid[ax] + pl.program_id(ax)
        pl.when(step == 0)(start)
        pl.when(step == steps // 2)(middle)
        body(*ins, *outs, *scr)
        pl.when(step == steps - 1)(finish)

    res = pl.pallas_call(
        wrapped, name=name, grid=grid, in_specs=list(in_specs) + comm.specs,
        out_specs=list(out_specs) + comm.specs, out_shape=list(out_shape) + comm.out_shape,
        scratch_shapes=list(scratch) + comm.scratch,
        compiler_params=pltpu.CompilerParams(dimension_semantics=("arbitrary",) * len(grid),
                                             vmem_limit_bytes=VMEM_LIMIT, has_side_effects=True),
    )(*args, *comm.arrs)
    comm.results = res[no:]
    return res[:no]


def _mm(a, b, mode, out_dtypes, name, epilogue=None, extras=(), tm=1024, tn=1024, tk=512, comm=None):
    if mode == "tn":
        K, M = a.shape
    else:
        M, K = a.shape
    N = b.shape[0] if mode == "nt" else b.shape[1]
    tm, tn, tk = _tile128(M, tm), _tile128(N, tn), _tile128(K, tk)
    nk = K // tk
    dims = {"nn": NN, "nt": NT, "tn": TN}[mode]
    a_spec = (pl.BlockSpec((tk, tm), lambda i, j, k: (k, i)) if mode == "tn"
              else pl.BlockSpec((tm, tk), lambda i, j, k: (i, k)))
    b_spec = (pl.BlockSpec((tn, tk), lambda i, j, k: (j, k)) if mode == "nt"
              else pl.BlockSpec((tk, tn), lambda i, j, k: (k, j)))
    mn_spec = pl.BlockSpec((tm, tn), lambda i, j, k: (i, j))
    n_ex = len(extras)
    n_out = len(out_dtypes)

    def body(*refs):
        a_ref, b_ref = refs[:2]
        ex = refs[2:2 + n_ex]
        outs = refs[2 + n_ex:2 + n_ex + n_out]
        acc = refs[-1]
        k = pl.program_id(2)

        @pl.when(k == 0)
        def _():
            acc[...] = jnp.zeros_like(acc)

        acc[...] += _dot(a_ref[...], b_ref[...], dims)

        @pl.when(k == nk - 1)
        def _():
            res = (acc[...],) if epilogue is None else epilogue(acc[...], *[e[...] for e in ex])
            for o, r in zip(outs, res):
                o[...] = r.astype(o.dtype)

    out = _call(
        body, name, (M // tm, N // tn, nk), [a_spec, b_spec] + [mn_spec] * n_ex, [mn_spec] * n_out,
        [jax.ShapeDtypeStruct((M, N), dt) for dt in out_dtypes], (a, b, *extras),
        scratch=[pltpu.VMEM((tm, tn), F32)], sem=("parallel", "parallel", "arbitrary"), comm=comm)
    return out[0] if n_out == 1 else out


def _rms_fwd(x, gain, out_dtype, name, width=None, col_block=0, residual=None, tb=256, comm=None):
    T = x.shape[0]
    W = x.shape[1] if width is None else width
    tb = _tile(T, tb)
    has_res = residual is not None

    def body(*refs):
        x_ref, g_ref = refs[:2]
        o_ref = refs[-1]
        xf = x_ref[...]
        y = xf * lax.rsqrt(jnp.mean(xf * xf, axis=-1, keepdims=True) + NORM_EPS) * g_ref[...]
        if has_res:
            y = refs[2][...] + y
        o_ref[...] = y.astype(o_ref.dtype)

    row = pl.BlockSpec((tb, W), lambda i: (i, 0))
    ins = [x, gain] + ([residual] if has_res else [])
    return _call(
        body, name, (T // tb,),
        [pl.BlockSpec((tb, W), lambda i: (i, col_block)),
         pl.BlockSpec((1, W), lambda i: (0, 0))] + ([row] if has_res else []),
        [row], [jax.ShapeDtypeStruct((T, W), out_dtype)], ins, sem=("parallel",), comm=comm)[0]


def _rms_bwd(dy, x, gain, out_dtype, name, width=None, col_block=0, residual=None, tb=256):
    T = dy.shape[0]
    W = x.shape[1] if width is None else width
    tb = _tile(T, tb)
    has_res = residual is not None

    def body(*refs):
        dy_ref, x_ref, g_ref = refs[:3]
        dx_ref, dg_ref = refs[-2:]
        i = pl.program_id(0)
        xf = x_ref[...]
        r = lax.rsqrt(jnp.mean(xf * xf, axis=-1, keepdims=True) + NORM_EPS)
        xn = xf * r
        dyf = dy_ref[...].astype(F32)
        dyg = dyf * g_ref[...]
        dx = r * (dyg - xn * jnp.mean(dyg * xn, axis=-1, keepdims=True))
        if has_res:
            dx = refs[3][...] + dx
        dx_ref[...] = dx.astype(dx_ref.dtype)

        @pl.when(i == 0)
        def _():
            dg_ref[...] = jnp.zeros_like(dg_ref)

        dg_ref[...] += jnp.sum(dyf * xn, axis=0, keepdims=True)

    row = pl.BlockSpec((tb, W), lambda i: (i, 0))
    vec = pl.BlockSpec((1, W), lambda i: (0, 0))
    ins = [dy, x, gain] + ([residual] if has_res else [])
    return pl.pallas_call(
        body, name=name, grid=(T // tb,),
        in_specs=[row, pl.BlockSpec((tb, W), lambda i: (i, col_block)), vec] + ([row] if has_res else []),
        out_specs=[row, vec],
        out_shape=[jax.ShapeDtypeStruct((T, W), out_dtype), jax.ShapeDtypeStruct((1, W), F32)],
        compiler_params=_params("arbitrary"),
    )(*ins)


def _loss_head(x2, y2, gain, target, name, tb=256):
    T, D = x2.shape
    tb = _tile(T, tb)

    def body(x2_ref, y2_ref, g_ref, t_ref, dx3_ref, loss_ref):
        i = pl.program_id(0)
        yf = y2_ref[...]
        x3 = x2_ref[...] + yf * lax.rsqrt(jnp.mean(yf * yf, axis=-1, keepdims=True) + NORM_EPS) * g_ref[...]
        e = x3 - t_ref[...]
        dx3_ref[...] = e * (1.0 / D)

        @pl.when(i == 0)
        def _():
            loss_ref[...] = jnp.zeros_like(loss_ref)

        per_row = jnp.mean(e * e, axis=-1, keepdims=True)
        loss_ref[...] += 0.5 * jnp.sum(per_row, axis=0, keepdims=True)

    row = pl.BlockSpec((tb, D), lambda i: (i, 0))
    return pl.pallas_call(
        body, name=name, grid=(T // tb,),
        in_specs=[row, row, pl.BlockSpec((1, D), lambda i: (0, 0)), row],
        out_specs=[row, pl.BlockSpec((1, 1), lambda i: (0, 0))],
        out_shape=[jax.ShapeDtypeStruct((T, D), F32), jax.ShapeDtypeStruct((1, 1), F32)],
        compiler_params=_params("arbitrary"),
    )(x2, y2, gain, target)


def _rope_tables(positions, rot_dim):
    half = rot_dim // 2
    inv_freq = ROPE_THETA ** (-jnp.arange(0, rot_dim, 2, dtype=F32) / rot_dim)
    ang = positions.astype(F32)[:, None] * inv_freq[None, :]
    cos, sin = jnp.cos(ang), jnp.sin(ang)
    T = positions.shape[0]
    ones = jnp.ones((T, HD - rot_dim), F32)
    cos_t = jnp.concatenate([cos, cos, ones], axis=1)
    sin_t = jnp.concatenate([-sin, sin, jnp.zeros_like(ones)], axis=1)
    return cos_t, sin_t


def _rotate(x, cos_t, sin_t, half):
    lane = lax.broadcasted_iota(jnp.int32, x.shape, 1)
    swapped = jnp.where(lane < half, pltpu.roll(x, HD - half, 1), pltpu.roll(x, half, 1))
    return x * cos_t + swapped * sin_t


def _rope_apply(x, cos_t, sin_t, half, n_blocks, rope_lo, rope_hi, out_dtype, name, col_off=0, tb=512):
    T = x.shape[0]
    tb = _tile(T, tb)

    def body(x_ref, c_ref, s_ref, o_ref):
        j = pl.program_id(1)
        is_rope = jnp.logical_and(j >= rope_lo, j < rope_hi)

        @pl.when(is_rope)
        def _():
            o_ref[...] = _rotate(x_ref[...].astype(F32), c_ref[...], s_ref[...], half).astype(o_ref.dtype)

        @pl.when(jnp.logical_not(is_rope))
        def _():
            o_ref[...] = x_ref[...].astype(o_ref.dtype)

    tab = pl.BlockSpec((tb, HD), lambda i, j: (i, 0))
    return pl.pallas_call(
        body, name=name, grid=(T // tb, n_blocks),
        in_specs=[pl.BlockSpec((tb, HD), lambda i, j: (i, j + col_off)), tab, tab],
        out_specs=pl.BlockSpec((tb, HD), lambda i, j: (i, j)),
        out_shape=jax.ShapeDtypeStruct((T, n_blocks * HD), out_dtype),
        compiler_params=_params("parallel", "parallel"),
    )(x, cos_t, sin_t)


def _shared_key_grad(dkr_heads, cos_t, sin_t_neg, half, name, tb=512):
    T = dkr_heads.shape[0]
    tb = _tile(T, tb)

    def body(d_ref, c_ref, s_ref, o_ref):
        tot = d_ref[:, 0:HD]
        for h in range(1, HEADS):
            tot = tot + d_ref[:, h * HD:(h + 1) * HD]
        o_ref[...] = _rotate(tot, c_ref[...], s_ref[...], half).astype(o_ref.dtype)

    tab = pl.BlockSpec((tb, HD), lambda i: (i, 0))
    return pl.pallas_call(
        body, name=name, grid=(T // tb,),
        in_specs=[pl.BlockSpec((tb, AW), lambda i: (i, 0)), tab, tab],
        out_specs=tab, out_shape=jax.ShapeDtypeStruct((T, HD), BF16),
        compiler_params=_params("parallel"),
    )(dkr_heads, cos_t, sin_t_neg)


def _band_masks(n):
    row = lax.broadcasted_iota(jnp.int32, (QBLK, QBLK), 0)
    col = lax.broadcasted_iota(jnp.int32, (QBLK, QBLK), 1)
    return col <= row, jnp.logical_and(col >= row, n > 0)


def _dswa_specs(d):
    cur = pl.BlockSpec((QBLK, AW), lambda r, n: (n, r))
    prev = pl.BlockSpec((QBLK, AW), lambda r, n: (jnp.maximum(n - 1, 0), r))
    return cur, prev


def _dswa_fwd(qkv, d, name):
    T = qkv.shape[0]
    sd = T // d
    nb = sd // QBLK
    q = qkv[:, :AW].reshape(sd, d * AW)
    k = qkv[:, AW:2 * AW].reshape(sd, d * AW)
    v = qkv[:, 2 * AW:].reshape(sd, d * AW)

    def body(q_ref, kc_ref, kp_ref, vc_ref, vp_ref, o_ref, l_ref):
        cur_mask, prev_mask = _band_masks(pl.program_id(1))
        for h in range(HEADS):
            sl = slice(h * HD, (h + 1) * HD)
            qh = q_ref[:, sl]
            sc = jnp.where(cur_mask, _dot(qh, kc_ref[:, sl], NT) * SCALE_A, NEG_INF)
            sp = jnp.where(prev_mask, _dot(qh, kp_ref[:, sl], NT) * SCALE_A, NEG_INF)
            m = jnp.maximum(jnp.max(sc, axis=-1, keepdims=True), jnp.max(sp, axis=-1, keepdims=True))
            pc = jnp.exp(sc - m)
            pp = jnp.exp(sp - m)
            den = jnp.sum(pc, axis=-1, keepdims=True) + jnp.sum(pp, axis=-1, keepdims=True)
            o = (_dot((pc / den).astype(BF16), vc_ref[:, sl], NN)
                 + _dot((pp / den).astype(BF16), vp_ref[:, sl], NN))
            o_ref[:, sl] = o
            l_ref[:, sl] = jnp.broadcast_to(m + jnp.log(den), (QBLK, HD))

    cur, prev = _dswa_specs(d)
    o, lse = pl.pallas_call(
        body, name=name, grid=(d, nb),
        in_specs=[cur, cur, prev, cur, prev], out_specs=[cur, cur],
        out_shape=[jax.ShapeDtypeStruct((sd, d * AW), F32)] * 2,
        compiler_params=_params("parallel", "parallel"),
    )(q, k, k, v, v)
    return o.reshape(T, AW), lse.reshape(T, AW)


def _dswa_merge(outs, lses, name, tb=256):
    T = outs[0].shape[0]
    tb = _tile(T, tb)
    nc = len(outs)

    def body(*refs):
        o_refs, l_refs = refs[:nc], refs[nc:2 * nc]
        out_ref, outb_ref, lt_ref = refs[2 * nc:]
        ls = [l[...] for l in l_refs]
        m = functools.reduce(jnp.maximum, ls)
        es = [jnp.exp(l - m) for l in ls]
        tot = functools.reduce(lambda a, b: a + b, es)
        acc = (es[0] / tot) * o_refs[0][...]
        for c in range(1, nc):
            acc = acc + (es[c] / tot) * o_refs[c][...]
        out_ref[...] = acc
        outb_ref[...] = acc.astype(BF16)
        lt_ref[...] = m + jnp.log(tot)

    row = pl.BlockSpec((tb, AW), lambda i: (i, 0))
    return pl.pallas_call(
        body, name=name, grid=(T // tb,),
        in_specs=[row] * (2 * nc), out_specs=[row] * 3,
        out_shape=[jax.ShapeDtypeStruct((T, AW), F32), jax.ShapeDtypeStruct((T, AW), BF16),
                   jax.ShapeDtypeStruct((T, AW), F32)],
        compiler_params=_params("parallel"),
    )(*outs, *lses)


def _delta_prep(dout, col_block, out, name, tb=256):
    T = out.shape[0]
    tb = _tile(T, tb)

    def body(do_ref, o_ref, delta_ref, dob_ref):
        for h in range(HEADS):
            sl = slice(h * HD, (h + 1) * HD)
            doh = do_ref[:, sl]
            delta_ref[:, sl] = jnp.broadcast_to(jnp.sum(doh * o_ref[:, sl], axis=-1, keepdims=True), (tb, HD))
            dob_ref[:, sl] = doh.astype(BF16)

    row = pl.BlockSpec((tb, AW), lambda i: (i, 0))
    return pl.pallas_call(
        body, name=name, grid=(T // tb,),
        in_specs=[pl.BlockSpec((tb, AW), lambda i: (i, col_block)), row], out_specs=[row, row],
        out_shape=[jax.ShapeDtypeStruct((T, AW), F32), jax.ShapeDtypeStruct((T, AW), BF16)],
        compiler_params=_params("parallel"),
    )(dout, out)


def _dswa_bwd(qkv, dout_b, lse_tot, delta, d, name, comm=None):
    T = qkv.shape[0]
    sd = T // d
    nb = sd // QBLK
    view = lambda a: a.reshape(sd, d * AW)
    q, k, v = view(qkv[:, :AW]), view(qkv[:, AW:2 * AW]), view(qkv[:, 2 * AW:])

    def body(q_ref, kc_ref, kp_ref, vc_ref, vp_ref, do_ref, l_ref, dl_ref,
             dq_ref, dkc_ref, dkp_ref, dvc_ref, dvp_ref):
        cur_mask, prev_mask = _band_masks(pl.program_id(1))
        for h in range(HEADS):
            sl = slice(h * HD, (h + 1) * HD)
            qh, kc, kp, vc, vp, doh = q_ref[:, sl], kc_ref[:, sl], kp_ref[:, sl], vc_ref[:, sl], vp_ref[:, sl], do_ref[:, sl]
            lse = l_ref[:, sl][:, 0:1]
            dlt = dl_ref[:, sl][:, 0:1]
            sc = jnp.where(cur_mask, _dot(qh, kc, NT) * SCALE_A, NEG_INF)
            sp = jnp.where(prev_mask, _dot(qh, kp, NT) * SCALE_A, NEG_INF)
            pc = jnp.exp(sc - lse)
            pp = jnp.exp(sp - lse)
            dsc = (pc * (_dot(doh, vc, NT) - dlt) * SCALE_A).astype(BF16)
            dsp = (pp * (_dot(doh, vp, NT) - dlt) * SCALE_A).astype(BF16)
            dq_ref[:, sl] = _dot(dsc, kc, NN) + _dot(dsp, kp, NN)
            dkc_ref[:, sl] = _dot(dsc, qh, TN)
            dkp_ref[:, sl] = _dot(dsp, qh, TN)
            dvc_ref[:, sl] = _dot(pc.astype(BF16), doh, TN)
            dvp_ref[:, sl] = _dot(pp.astype(BF16), doh, TN)

    cur, prev = _dswa_specs(d)
    res = _call(
        body, name, (d, nb), [cur, cur, prev, cur, prev, cur, cur, cur], [cur] * 5,
        [jax.ShapeDtypeStruct((sd, d * AW), F32)] * 5, (q, k, k, v, v, view(dout_b), view(lse_tot), view(delta)),
        sem=("parallel", "parallel"), comm=comm)
    return [r.reshape(T, AW) for r in res]


def _dswa_combine(grads, cos_t, sin_t_neg, name):
    T = grads[0][0].shape[0]
    nbt = T // QBLK
    half = ROT_A // 2

    def body(*refs):
        c_ref, s_ref = refs[-5:-3]
        dq_ref, dk_ref, dv_ref = refs[-3:]
        i = pl.program_id(0)
        dq = dk = dv = None
        for ci, d in enumerate(DSWA_DILATIONS):
            gq, gkc, gkp, gvc, gvp = refs[5 * ci:5 * ci + 5]
            has_next = i + d < nbt
            gk = gkc[...] + jnp.where(has_next, gkp[...], 0.0)
            gv = gvc[...] + jnp.where(has_next, gvp[...], 0.0)
            dq = gq[...] if dq is None else dq + gq[...]
            dk = gk if dk is None else dk + gk
            dv = gv if dv is None else dv + gv
        for h in range(HEADS):
            sl = slice(h * HD, (h + 1) * HD)
            dq_ref[:, sl] = _rotate(dq[:, sl], c_ref[...], s_ref[...], half).astype(BF16)
            dk_ref[:, sl] = _rotate(dk[:, sl], c_ref[...], s_ref[...], half).astype(BF16)
        dv_ref[...] = dv.astype(BF16)

    row = pl.BlockSpec((QBLK, AW), lambda i: (i, 0))
    tab = pl.BlockSpec((QBLK, HD), lambda i: (i, 0))
    in_specs, ins = [], []
    for d, g in zip(DSWA_DILATIONS, grads):
        nxt = pl.BlockSpec((QBLK, AW), functools.partial(lambda i, d: (jnp.minimum(i + d, nbt - 1), 0), d=d))
        in_specs += [row, row, nxt, row, nxt]
        ins += list(g)
    return pl.pallas_call(
        body, name=name, grid=(nbt,),
        in_specs=in_specs + [tab, tab], out_specs=[row] * 3,
        out_shape=[jax.ShapeDtypeStruct((T, AW), BF16)] * 3,
        compiler_params=_params("parallel"),
    )(*ins, cos_t, sin_t_neg)


MLA_TQ = 512


def _mla_scores(qn, qr, kn, kr, qi, ki, tq):
    s = (_dot(qn, kn, NT) + _dot(qr, kr, NT)) * SCALE_B
    row = lax.broadcasted_iota(jnp.int32, s.shape, 0) + qi * tq
    col = lax.broadcasted_iota(jnp.int32, s.shape, 1) + ki * tq
    return jnp.where(col <= row, s, NEG_INF)


def _mla_fwd(q, kv, kr, name, comm=None):
    T = q.shape[0]
    tq = _tile(T, MLA_TQ)
    nq = T // tq

    def body(qn_ref, qr_ref, kn_ref, v_ref, kr_ref, o_ref, l_ref, m_s, l_s, acc):
        qi, ki = pl.program_id(1), pl.program_id(2)

        @pl.when(ki == 0)
        def _():
            m_s[...] = jnp.full_like(m_s, NEG_INF)
            l_s[...] = jnp.zeros_like(l_s)
            acc[...] = jnp.zeros_like(acc)

        @pl.when(ki <= qi)
        def _():
            s = _mla_scores(qn_ref[...], qr_ref[...], kn_ref[...], kr_ref[...], qi, ki, tq)
            m_new = jnp.maximum(m_s[...], jnp.max(s, axis=-1, keepdims=True))
            alpha = jnp.exp(m_s[...] - m_new)
            p = jnp.exp(s - m_new)
            l_s[...] = alpha * l_s[...] + jnp.sum(p, axis=-1, keepdims=True)
            acc[...] = alpha * acc[...] + _dot(p.astype(BF16), v_ref[...], NN)
            m_s[...] = m_new

        @pl.when(ki == nq - 1)
        def _():
            o_ref[...] = acc[...] / l_s[...]
            l_ref[...] = jnp.broadcast_to(m_s[...] + jnp.log(l_s[...]), (tq, HD))

    kidx = lambda h, qi, ki: jnp.minimum(ki, qi)
    return _call(
        body, name, (HEADS, nq, nq),
        [pl.BlockSpec((tq, HD), lambda h, qi, ki: (qi, h)),
         pl.BlockSpec((tq, HD), lambda h, qi, ki: (qi, HEADS + h)),
         pl.BlockSpec((tq, HD), lambda h, qi, ki: (kidx(h, qi, ki), h)),
         pl.BlockSpec((tq, HD), lambda h, qi, ki: (kidx(h, qi, ki), HEADS + h)),
         pl.BlockSpec((tq, HD), lambda h, qi, ki: (kidx(h, qi, ki), 0))],
        [pl.BlockSpec((tq, HD), lambda h, qi, ki: (qi, h))] * 2,
        [jax.ShapeDtypeStruct((T, AW), F32)] * 2, (q, q, kv, kv, kr),
        scratch=[pltpu.VMEM((tq, 1), F32), pltpu.VMEM((tq, 1), F32), pltpu.VMEM((tq, HD), F32)],
        sem=("parallel", "parallel", "arbitrary"), comm=comm)


def _mla_ds(qn, qr, kn, kr, v, do, lse, dlt, qi, ki, tq):
    s = _mla_scores(qn, qr, kn, kr, qi, ki, tq)
    p = jnp.exp(s - lse)
    ds = (p * (_dot(do, v, NT) - dlt) * SCALE_B).astype(BF16)
    return p, ds


def _mla_bwd_q(q, kv, kr, dout_b, lse, delta, name, comm=None):
    T = q.shape[0]
    tq = _tile(T, MLA_TQ)
    nq = T // tq

    def body(qn_ref, qr_ref, kn_ref, v_ref, kr_ref, do_ref, l_ref, dl_ref, dqn_ref, dqr_ref, an, ar):
        qi, ki = pl.program_id(1), pl.program_id(2)

        @pl.when(ki == 0)
        def _():
            an[...] = jnp.zeros_like(an)
            ar[...] = jnp.zeros_like(ar)

        @pl.when(ki <= qi)
        def _():
            _, ds = _mla_ds(qn_ref[...], qr_ref[...], kn_ref[...], kr_ref[...], v_ref[...], do_ref[...],
                            l_ref[:, 0:1], dl_ref[:, 0:1], qi, ki, tq)
            an[...] += _dot(ds, kn_ref[...], NN)
            ar[...] += _dot(ds, kr_ref[...], NN)

        @pl.when(ki == nq - 1)
        def _():
            dqn_ref[...] = an[...]
            dqr_ref[...] = ar[...]

    kidx = lambda h, qi, ki: jnp.minimum(ki, qi)
    qspec = pl.BlockSpec((tq, HD), lambda h, qi, ki: (qi, h))
    return _call(
        body, name, (HEADS, nq, nq),
        [qspec,
         pl.BlockSpec((tq, HD), lambda h, qi, ki: (qi, HEADS + h)),
         pl.BlockSpec((tq, HD), lambda h, qi, ki: (kidx(h, qi, ki), h)),
         pl.BlockSpec((tq, HD), lambda h, qi, ki: (kidx(h, qi, ki), HEADS + h)),
         pl.BlockSpec((tq, HD), lambda h, qi, ki: (kidx(h, qi, ki), 0)),
         qspec, qspec, qspec],
        [qspec, qspec], [jax.ShapeDtypeStruct((T, AW), F32)] * 2, (q, q, kv, kv, kr, dout_b, lse, delta),
        scratch=[pltpu.VMEM((tq, HD), F32)] * 2, sem=("parallel", "parallel", "arbitrary"), comm=comm)


def _mla_bwd_kv(q, kv, kr, dout_b, lse, delta, name, comm=None):
    T = q.shape[0]
    tq = _tile(T, MLA_TQ)
    nq = T // tq

    def body(qn_ref, qr_ref, kn_ref, v_ref, kr_ref, do_ref, l_ref, dl_ref, dkn_ref, dv_ref, dkr_ref, akn, av, akr):
        ki, qi = pl.program_id(1), pl.program_id(2)

        @pl.when(qi == 0)
        def _():
            akn[...] = jnp.zeros_like(akn)
            av[...] = jnp.zeros_like(av)
            akr[...] = jnp.zeros_like(akr)

        @pl.when(qi >= ki)
        def _():
            p, ds = _mla_ds(qn_ref[...], qr_ref[...], kn_ref[...], kr_ref[...], v_ref[...], do_ref[...],
                            l_ref[:, 0:1], dl_ref[:, 0:1], qi, ki, tq)
            av[...] += _dot(p.astype(BF16), do_ref[...], TN)
            akn[...] += _dot(ds, qn_ref[...], TN)
            akr[...] += _dot(ds, qr_ref[...], TN)

        @pl.when(qi == nq - 1)
        def _():
            dkn_ref[...] = akn[...]
            dv_ref[...] = av[...]
            dkr_ref[...] = akr[...]

    qidx = lambda h, ki, qi: jnp.maximum(qi, ki)
    qspec = pl.BlockSpec((tq, HD), lambda h, ki, qi: (qidx(h, ki, qi), h))
    kspec = pl.BlockSpec((tq, HD), lambda h, ki, qi: (ki, h))
    return _call(
        body, name, (HEADS, nq, nq),
        [qspec,
         pl.BlockSpec((tq, HD), lambda h, ki, qi: (qidx(h, ki, qi), HEADS + h)),
         kspec,
         pl.BlockSpec((tq, HD), lambda h, ki, qi: (ki, HEADS + h)),
         pl.BlockSpec((tq, HD), lambda h, ki, qi: (ki, 0)),
         qspec, qspec, qspec],
        [kspec] * 3, [jax.ShapeDtypeStruct((T, AW), F32)] * 3, (q, q, kv, kv, kr, dout_b, lse, delta),
        scratch=[pltpu.VMEM((tq, HD), F32)] * 3, sem=("parallel", "parallel", "arbitrary"), comm=comm)


def _adamw(parts, w, m, v, name, tb=128):
    R, C = w.shape
    tb = _tile(R, tb)
    c1 = 1.0 - ADAM_B1
    c2 = 1.0 - ADAM_B2
    bc1 = 1.0 - ADAM_B1 ** ADAM_STEP
    bc2 = 1.0 - ADAM_B2 ** ADAM_STEP

    def body(p_ref, w_ref, m_ref, v_ref, g_ref, d_ref, nm_ref, nv_ref):
        g = p_ref[0].astype(F32)
        for j in range(1, N_DEV):
            g = g + p_ref[j].astype(F32)
        nm = ADAM_B1 * m_ref[...] + c1 * g
        nv = ADAM_B2 * v_ref[...] + c2 * (g * g)
        g_ref[...] = g
        nm_ref[...] = nm
        nv_ref[...] = nv
        d_ref[...] = -ADAM_LR * ((nm / bc1) / (jnp.sqrt(nv / bc2) + ADAM_EPS) + ADAM_WD * w_ref[...])

    row = pl.BlockSpec((tb, C), lambda i: (i, 0))
    return pl.pallas_call(
        body, name=name, grid=(R // tb,),
        in_specs=[pl.BlockSpec((N_DEV, tb, C), lambda i: (0, i, 0)), row, row, row],
        out_specs=[row] * 4, out_shape=[jax.ShapeDtypeStruct((R, C), F32)] * 4,
        compiler_params=_params("parallel"),
    )(parts, w, m, v)


def _cols_from_shards(g):
    return jnp.transpose(g, (1, 0, 2)).reshape(g.shape[1], N_DEV * g.shape[2])


def _cols_to_shards(w):
    return jnp.transpose(w.reshape(w.shape[0], N_DEV, w.shape[1] // N_DEV), (1, 0, 2))


def _split_heads(w, first):
    w3 = w.reshape(w.shape[0], HEADS, -1)
    return w3[:, :, :first].reshape(w.shape[0], -1), w3[:, :, first:].reshape(w.shape[0], -1)


def _join_heads(a, b):
    R = a.shape[0]
    return jnp.concatenate([a.reshape(R, HEADS, -1), b.reshape(R, HEADS, -1)], axis=2).reshape(R, -1)


def _pad_heads(w, width):
    w3 = w.reshape(w.shape[0], HEADS, -1)
    return jnp.pad(w3, ((0, 0), (0, 0), (0, width - w3.shape[2]))).reshape(w.shape[0], HEADS * width)


def _unpad_heads(w, k):
    return w.reshape(w.shape[0], HEADS, -1)[:, :, :k].reshape(w.shape[0], HEADS * k)


def kernel(x, positions, norm_attn_pre, norm_attn_post, w_in, q_latent_norm, kv_latent_norm, w_uq, w_ukv, w_out, norm_mlp_pre, norm_mlp_post, w_up, w_down, loss_target, m_norm_attn_pre, m_norm_attn_post, m_w_in, m_q_latent_norm, m_kv_latent_norm, m_w_uq, m_w_ukv, m_w_out, m_norm_mlp_pre, m_norm_mlp_post, m_w_up, m_w_down, v_norm_attn_pre, v_norm_attn_post, v_w_in, v_q_latent_norm, v_kv_latent_norm, v_w_uq, v_w_ukv, v_w_out, v_norm_mlp_pre, v_norm_mlp_post, v_w_up, v_w_down):
    xs = x[0]
    tgt = loss_target[0]
    pos = positions[0]
    T, D = xs.shape
    big = dict(w_in=(w_in, m_w_in, v_w_in), w_uq=(w_uq, m_w_uq, v_w_uq), w_ukv=(w_ukv, m_w_ukv, v_w_ukv),
               w_out=(w_out, m_w_out, v_w_out), w_up=(w_up, m_w_up, v_w_up), w_down=(w_down, m_w_down, v_w_down))
    big = {n: tuple(t[0] for t in ts) for n, ts in big.items()}
    big_names = ["w_in", "w_uq", "w_ukv", "w_out", "w_up", "w_down"]
    col_sharded = {"w_in", "w_uq", "w_ukv", "w_up"}

    wb = {n: big[n][0].astype(BF16) for n in big_names}

    def gathered(ex, i, n):
        g = ex.results[i]
        return _cols_from_shards(g) if n in col_sharded else g.reshape(-1, g.shape[2])

    def scatter_of(g, n):
        return _Exchange([_cols_to_shards(g) if n in col_sharded
                          else g.reshape(N_DEV, g.shape[0] // N_DEV, g.shape[1])], True)

    cos_a, sin_a = _rope_tables(pos, ROT_A)
    cos_b, sin_b = _rope_tables(pos, ROPE_MLA)

    ex_in = _Exchange([wb["w_in"]], False)
    h1 = _rms_fwd(xs, norm_attn_pre, BF16, "norm_attn_pre_fwd", comm=ex_in)
    Wi = jnp.pad(gathered(ex_in, 0, "w_in"), ((0, 0), (0, IN_PAD - IN_COLS)))
    ex_mid = _Exchange([wb["w_uq"], wb["w_ukv"], wb["w_out"]], False)
    proj = _mm(h1, Wi, "nn", [F32], "proj_in", tn=1408, comm=ex_mid)
    uq_n, uq_r = _split_heads(gathered(ex_mid, 0, "w_uq"), HD)
    Wuq = jnp.concatenate([uq_n, _pad_heads(uq_r, HD)], axis=1)
    Wukv = jnp.concatenate(_split_heads(gathered(ex_mid, 1, "w_ukv"), HD), axis=1)
    Wo = gathered(ex_mid, 2, "w_out")
    qkv = _rope_apply(proj, cos_a, sin_a, ROT_A // 2, 3 * HEADS, 0, 2 * HEADS, BF16, "rope_dswa")
    outs, lses = [], []
    for d in DSWA_DILATIONS:
        o, l = _dswa_fwd(qkv, d, f"dswa_fwd_d{d}")
        outs.append(o)
        lses.append(l)
    a_out, a_out_b, a_lse = _dswa_merge(outs, lses, "dswa_merge")

    cqn = _rms_fwd(proj, q_latent_norm, BF16, "q_latent_norm_fwd", width=Q_LORA, col_block=3 * AW // Q_LORA)
    ckvn = _rms_fwd(proj, kv_latent_norm, BF16, "kv_latent_norm_fwd", width=KV_LORA, col_block=3 * AW // KV_LORA + 1)
    qb = _mm(cqn, Wuq, "nn", [F32], "q_up")
    kvb = _mm(ckvn, Wukv, "nn", [BF16], "kv_up")
    q_mla = _rope_apply(qb, cos_b, sin_b, ROPE_MLA // 2, 2 * HEADS, HEADS, 2 * HEADS, BF16, "rope_mla_q")
    kr = _rope_apply(proj, cos_b, sin_b, ROPE_MLA // 2, 1, 0, 1, BF16, "rope_mla_k", col_off=(IN_PAD - HD) // HD)
    ex_up = _Exchange([wb["w_up"]], False)
    b_out, b_lse = _mla_fwd(q_mla, kvb, kr, "mla_fwd", comm=ex_up)
    Wup = gathered(ex_up, 0, "w_up")

    mixed = jnp.concatenate([a_out_b, b_out.astype(BF16)], axis=1)
    y1 = _mm(mixed, Wo, "nn", [F32], "attn_out")
    x2 = _rms_fwd(y1, norm_attn_post, F32, "norm_attn_post_fwd", residual=xs)

    h2 = _rms_fwd(x2, norm_mlp_pre, BF16, "norm_mlp_pre_fwd")

    def relu2(z):
        r = jnp.maximum(z, 0.0)
        return r * r, r

    ex_down = _Exchange([wb["w_down"]], False)
    u, zr = _mm(h2, Wup, "nn", [BF16, BF16], "mlp_up", epilogue=relu2, comm=ex_down)
    Wdn = gathered(ex_down, 0, "w_down")
    y2 = _mm(u, Wdn, "nn", [F32], "mlp_down")
    dx3, loss_part = _loss_head(x2, y2, norm_mlp_post, tgt, "loss_head")

    dy2, dg_mlp_post = _rms_bwd(dx3, y2, norm_mlp_post, BF16, "norm_mlp_post_bwd")
    dz = _mm(dy2, Wdn, "nt", [BF16], "mlp_down_dx", epilogue=lambda du, r: (du * (2.0 * r.astype(F32)),), extras=(zr,))
    g_down = _mm(u, dy2, "tn", [BF16], "mlp_down_dw")
    g_up = _mm(h2, dz, "tn", [BF16], "mlp_up_dw")
    dh2 = _mm(dz, Wup, "nt", [F32], "mlp_up_dx")
    dx2, dg_mlp_pre = _rms_bwd(dh2, x2, norm_mlp_pre, F32, "norm_mlp_pre_bwd", residual=dx3)

    dy1, dg_attn_post = _rms_bwd(dx2, y1, norm_attn_post, BF16, "norm_attn_post_bwd")
    dmixed = _mm(dy1, Wo, "nt", [F32], "attn_out_dx")
    g_out = _mm(mixed, dy1, "tn", [BF16], "attn_out_dw")

    b_delta, b_dout = _delta_prep(dmixed, 1, b_out, "mla_delta")
    sc_down = scatter_of(g_down, "w_down")
    dqn, dqr = _mla_bwd_q(q_mla, kvb, kr, b_dout, b_lse, b_delta, "mla_bwd_q", comm=sc_down)
    sc_up = scatter_of(g_up, "w_up")
    dkn, dvb, dkr = _mla_bwd_kv(q_mla, kvb, kr, b_dout, b_lse, b_delta, "mla_bwd_kv", comm=sc_up)
    dqb = _rope_apply(jnp.concatenate([dqn, dqr], axis=1), cos_b, -sin_b, ROPE_MLA // 2, 2 * HEADS, HEADS, 2 * HEADS,
                      BF16, "rope_mla_q_bwd")
    dkvb = jnp.concatenate([dkn, dvb], axis=1).astype(BF16)
    d_kr = _shared_key_grad(dkr, cos_b, -sin_b, ROPE_MLA // 2, "rope_mla_k_bwd")
    g_uq_pad = _mm(cqn, dqb, "tn", [BF16], "q_up_dw")
    g_ukv_perm = _mm(ckvn, dkvb, "tn", [BF16], "kv_up_dw")
    dcqn = _mm(dqb, Wuq, "nt", [F32], "q_up_dx")
    dckvn = _mm(dkvb, Wukv, "nt", [F32], "kv_up_dx")
    d_cq, dg_q = _rms_bwd(dcqn, proj, q_latent_norm, BF16, "q_latent_norm_bwd", width=Q_LORA, col_block=3 * AW // Q_LORA)
    d_ckv, dg_kv = _rms_bwd(dckvn, proj, kv_latent_norm, BF16, "kv_latent_norm_bwd", width=KV_LORA,
                            col_block=3 * AW // KV_LORA + 1)

    g_uq = _join_heads(g_uq_pad[:, :AW], _unpad_heads(g_uq_pad[:, AW:], ROPE_MLA))
    g_ukv = _join_heads(g_ukv_perm[:, :AW], g_ukv_perm[:, AW:])
    sc_out = scatter_of(g_out, "w_out")
    sc_uq = _Exchange([_cols_to_shards(g_uq), _cols_to_shards(g_ukv)], True)
    a_delta, a_dout = _delta_prep(dmixed, 0, a_out, "dswa_delta")
    a_grads = [_dswa_bwd(qkv, a_dout, a_lse, a_delta, d, f"dswa_bwd_d{d}", comm=cm)
               for d, cm in zip(DSWA_DILATIONS, (sc_out, sc_uq, None))]
    d_aq, d_ak, d_av = _dswa_combine(a_grads, cos_a, -sin_a, "dswa_combine")

    dproj = jnp.concatenate([d_aq, d_ak, d_av, d_cq, d_ckv, d_kr], axis=1)
    g_in_pad = _mm(h1, dproj, "tn", [BF16], "proj_in_dw", tn=1408)
    sc_in = scatter_of(g_in_pad[:, :IN_COLS], "w_in")
    dh1 = _mm(dproj, Wi, "nt", [F32], "proj_in_dx", comm=sc_in)
    grad_x, dg_attn_pre = _rms_bwd(dh1, xs, norm_attn_pre, F32, "norm_attn_pre_bwd", residual=dx2)

    parts = dict(w_in=sc_in.results[0], w_uq=sc_uq.results[0], w_ukv=sc_uq.results[1], w_out=sc_out.results[0],
                 w_up=sc_up.results[0], w_down=sc_down.results[0])
    big_out = {n: _adamw(parts[n], *big[n], f"adamw_{n}") for n in big_names}

    gain_names = ["norm_attn_pre", "norm_attn_post", "q_latent_norm", "kv_latent_norm", "norm_mlp_pre", "norm_mlp_post"]
    gain_args = dict(norm_attn_pre=(norm_attn_pre, m_norm_attn_pre, v_norm_attn_pre),
                     norm_attn_post=(norm_attn_post, m_norm_attn_post, v_norm_attn_post),
                     q_latent_norm=(q_latent_norm, m_q_latent_norm, v_q_latent_norm),
                     kv_latent_norm=(kv_latent_norm, m_kv_latent_norm, v_kv_latent_norm),
                     norm_mlp_pre=(norm_mlp_pre, m_norm_mlp_pre, v_norm_mlp_pre),
                     norm_mlp_post=(norm_mlp_post, m_norm_mlp_post, v_norm_mlp_post))
    gain_grads = dict(norm_attn_pre=dg_attn_pre, norm_attn_post=dg_attn_post, q_latent_norm=dg_q,
                      kv_latent_norm=dg_kv, norm_mlp_pre=dg_mlp_pre, norm_mlp_post=dg_mlp_post)
    packed = jnp.concatenate([gain_grads[n] for n in gain_names], axis=1)
    gain_parts = _Exchange([packed], False).standalone("gather_gain_grads")[0]
    pack3 = lambda i: jnp.concatenate([gain_args[n][i] for n in gain_names], axis=1)
    gain_out = _adamw(gain_parts, pack3(0), pack3(1), pack3(2), "adamw_gains", tb=1)
    offs = [0]
    for n in gain_names:
        offs.append(offs[-1] + gain_args[n][0].shape[1])
    small_out = {n: tuple(o[:, offs[i]:offs[i + 1]] for o in gain_out) for i, n in enumerate(gain_names)}

    loss = lax.psum(loss_part[0, 0], ("x", "y", "c"))

    order = ["norm_attn_pre", "norm_attn_post", "w_in", "q_latent_norm", "kv_latent_norm", "w_uq", "w_ukv", "w_out",
             "norm_mlp_pre", "norm_mlp_post", "w_up", "w_down"]
    res = {n: (small_out[n] if n in small_out else tuple(o[None] for o in big_out[n])) for n in order}
    return (loss, grad_x[None], *[res[n][0] for n in order], *[res[n][1] for n in order],
            *[res[n][2] for n in order], *[res[n][3] for n in order])
```

```python
import functools
import math

import jax
import jax.numpy as jnp
from jax import lax
from jax.experimental import pallas as pl
from jax.experimental.pallas import tpu as pltpu

F32 = jnp.float32
BF16 = jnp.bfloat16

N_DEV = 8
HEADS = 8
HD = 128
AW = HEADS * HD
Q_LORA = 512
KV_LORA = 512
ROPE_MLA = 64
ROT_A = 32
IN_COLS = 3 * AW + Q_LORA + KV_LORA + ROPE_MLA
IN_PAD = 3 * AW + Q_LORA + KV_LORA + HD
QBLK = 128
DSWA_DILATIONS = (1, 4, 16)
ROPE_THETA = 500000.0
NORM_EPS = 1e-6
NEG_INF = -1e30
SCALE_A = HD ** -0.5
SCALE_B = (HD + ROPE_MLA) ** -0.5

ADAM_LR = 0.001
ADAM_B1 = 0.9
ADAM_B2 = 0.999
ADAM_EPS = 1e-08
ADAM_WD = 0.01
ADAM_STEP = 10

VMEM_LIMIT = 48 * 1024 * 1024

NT = (((1,), (1,)), ((), ()))
NN = (((1,), (0,)), ((), ()))
TN = (((0,), (0,)), ((), ()))


def _dot(a, b, dims):
    return lax.dot_general(a, b, dims, preferred_element_type=F32)


def _params(*sem):
    return pltpu.CompilerParams(dimension_semantics=sem, vmem_limit_bytes=VMEM_LIMIT)


def _tile(n, want):
    t = min(n, want)
    while n % t:
        t //= 2
    return t


def _tile128(n, want):
    if n % 128:
        return n
    units = n // 128
    return 128 * max(u for u in range(1, max(want // 128, 1) + 1) if units % u == 0)


class _Exchange:
    def __init__(self, arrs, scatter):
        self.arrs = list(arrs)
        self.scatter = scatter
        self.n = len(self.arrs)
        self.results = None
        hbm = pl.BlockSpec(memory_space=pltpu.HBM)
        self.specs = [hbm] * self.n
        self.out_shape = [jax.ShapeDtypeStruct(a.shape if scatter else (N_DEV,) + a.shape, a.dtype)
                          for a in self.arrs]
        n_sem = self.n * (N_DEV - 1)
        self.scratch = [pltpu.SemaphoreType.DMA((n_sem,)), pltpu.SemaphoreType.DMA((n_sem,)),
                        pltpu.SemaphoreType.DMA((self.n,))]

    def hooks(self, ins, outs, send_sems, recv_sems, local_sems):
        x, y, c = lax.axis_index("x"), lax.axis_index("y"), lax.axis_index("c")
        me = (x, y, c)
        sib = (x, y, 1 - c)
        chips = [(1 - x, y), (x, 1 - y), (1 - x, 1 - y)]
        slot = lambda p: 4 * p[0] + 2 * p[1] + p[2]

        def rcopy(a, k, src, dst, to):
            i = a * (N_DEV - 1) + k
            return pltpu.make_async_remote_copy(src_ref=src, dst_ref=dst, send_sem=send_sems.at[i],
                                                recv_sem=recv_sems.at[i], device_id=to,
                                                device_id_type=pl.DeviceIdType.MESH)

        def local(a):
            src = ins[a].at[slot(me)] if self.scatter else ins[a]
            return pltpu.make_async_copy(src, outs[a].at[slot(me)], local_sems.at[a])

        def peer(rel):
            return (1 - x if rel & 4 else x, 1 - y if rel & 2 else y, 1 - c if rel & 1 else c)

        if self.scatter:
            def start():
                for a in range(self.n):
                    local(a).start()
                    for rel in range(1, N_DEV):
                        rcopy(a, rel - 1, ins[a].at[slot(peer(rel))], outs[a].at[slot(me)], peer(rel)).start()

            def middle():
                pass

            def finish():
                for a in range(self.n):
                    for rel in range(1, N_DEV):
                        cp = rcopy(a, rel - 1, ins[a].at[slot(peer(rel))], outs[a].at[slot(peer(rel))], peer(rel))
                        cp.wait_send()
                        cp.wait_recv()
                    local(a).wait()
        else:
            def start():
                for a in range(self.n):
                    local(a).start()
                    rcopy(a, 0, ins[a], outs[a].at[slot(me)], sib).start()
                    for j, chip in enumerate(chips):
                        rcopy(a, 1 + j, ins[a], outs[a].at[slot(me)], (*chip, c)).start()

            def middle():
                for a in range(self.n):
                    for j, chip in enumerate(chips):
                        landed = outs[a].at[slot((*chip, c))]
                        rcopy(a, 1 + j, ins[a], landed, me).wait_recv()
                        rcopy(a, 4 + j, landed, landed, sib).start()

            def finish():
                for a in range(self.n):
                    rcopy(a, 0, ins[a], outs[a].at[slot(sib)], me).wait_recv()
                    for j, chip in enumerate(chips):
                        rcopy(a, 4 + j, ins[a], outs[a].at[slot((*chip, 1 - c))], me).wait_recv()
                    for k in range(N_DEV - 1):
                        rcopy(a, k, ins[a], outs[a].at[slot(me)], me).wait_send()
                    local(a).wait()

        return start, middle, finish

    def standalone(self, name):
        n = self.n

        def body(*refs):
            start, middle, finish = self.hooks(refs[:n], refs[n:2 * n], *refs[2 * n:])
            start()
            middle()
            finish()

        self.results = pl.pallas_call(
            body, name=name, in_specs=self.specs, out_specs=self.specs, out_shape=self.out_shape,
            scratch_shapes=self.scratch, compiler_params=pltpu.CompilerParams(has_side_effects=True),
        )(*self.arrs)
        return self.results


def _call(body, name, grid, in_specs, out_specs, out_shape, args, scratch=(), sem=(), comm=None, prefetch=()):
    npf = len(prefetch)
    if comm is None:
        spec = pltpu.PrefetchScalarGridSpec(num_scalar_prefetch=npf, grid=grid, in_specs=list(in_specs),
                                            out_specs=list(out_specs), scratch_shapes=list(scratch))
        return pl.pallas_call(body, name=name, grid_spec=spec, out_shape=list(out_shape),
                              compiler_params=_params(*sem))(*prefetch, *args)
    ni, no, ns, n = len(in_specs), len(out_specs), len(scratch), comm.n
    steps = math.prod(grid)

    def wrapped(*refs):
        pf, refs = refs[:npf], refs[npf:]
        ins, c_ins = refs[:ni], refs[ni:ni + n]
        outs, c_outs = refs[ni + n:ni + n + no], refs[ni + n + no:ni + 2 * n + no]
        scr, c_scr = refs[ni + 2 * n + no:ni + 2 * n + no + ns], refs[ni + 2 * n + no + ns:]
        start, middle, finish = comm.hooks(c_ins, c_outs, *c_scr)
        step = pl.program_id(0)
        for ax in range(1, len(grid)):
            step = step * grid[ax] + pl.program_id(ax)
        pl.when(step == 0)(start)
        pl.when(step == steps // 2)(middle)
        body(*pf, *ins, *outs, *scr)
        pl.when(step == steps - 1)(finish)

    spec = pltpu.PrefetchScalarGridSpec(num_scalar_prefetch=npf, grid=grid, in_specs=list(in_specs) + comm.specs,
                                        out_specs=list(out_specs) + comm.specs,
                                        scratch_shapes=list(scratch) + comm.scratch)
    res = pl.pallas_call(
        wrapped, name=name, grid_spec=spec, out_shape=list(out_shape) + comm.out_shape,
        compiler_params=pltpu.CompilerParams(dimension_semantics=("arbitrary",) * len(grid),
                                             vmem_limit_bytes=VMEM_LIMIT, has_side_effects=True),
    )(*prefetch, *args, *comm.arrs)
    comm.results = res[no:]
    return res[:no]


def _mm(a, b, mode, out_dtypes, name, epilogue=None, extras=(), tm=1024, tn=1024, tk=2048, comm=None):
    if mode == "tn":
        K, M = a.shape
    else:
        M, K = a.shape
    N = b.shape[0] if mode == "nt" else b.shape[1]
    tm, tn, tk = _tile128(M, tm), _tile128(N, tn), _tile128(K, tk)
    nk = K // tk
    dims = {"nn": NN, "nt": NT, "tn": TN}[mode]
    a_spec = (pl.BlockSpec((tk, tm), lambda i, j, k: (k, i)) if mode == "tn"
              else pl.BlockSpec((tm, tk), lambda i, j, k: (i, k)))
    b_spec = (pl.BlockSpec((tn, tk), lambda i, j, k: (j, k)) if mode == "nt"
              else pl.BlockSpec((tk, tn), lambda i, j, k: (k, j)))
    mn_spec = pl.BlockSpec((tm, tn), lambda i, j, k: (i, j))
    n_ex = len(extras)
    n_out = len(out_dtypes)

    def finish(acc, ex, outs):
        res = (acc,) if epilogue is None else epilogue(acc, *[e[...] for e in ex])
        for o, r in zip(outs, res):
            o[...] = r.astype(o.dtype)

    def body(*refs):
        a_ref, b_ref = refs[:2]
        ex = refs[2:2 + n_ex]
        outs = refs[2 + n_ex:2 + n_ex + n_out]
        if nk == 1:
            finish(_dot(a_ref[...], b_ref[...], dims), ex, outs)
            return
        acc = refs[-1]
        k = pl.program_id(2)

        @pl.when(k == 0)
        def _():
            acc[...] = _dot(a_ref[...], b_ref[...], dims)

        @pl.when(jnp.logical_and(k > 0, k < nk - 1))
        def _():
            acc[...] += _dot(a_ref[...], b_ref[...], dims)

        @pl.when(k == nk - 1)
        def _():
            finish(acc[...] + _dot(a_ref[...], b_ref[...], dims), ex, outs)

    out = _call(
        body, name, (M // tm, N // tn, nk), [a_spec, b_spec] + [mn_spec] * n_ex, [mn_spec] * n_out,
        [jax.ShapeDtypeStruct((M, N), dt) for dt in out_dtypes], (a, b, *extras),
        scratch=[] if nk == 1 else [pltpu.VMEM((tm, tn), F32)], sem=("parallel", "parallel", "arbitrary"),
        comm=comm)
    return out[0] if n_out == 1 else out


def _rms_fwd(x, gain, out_dtype, name, width=None, col_block=0, residual=None, tb=256, comm=None):
    T = x.shape[0]
    W = x.shape[1] if width is None else width
    tb = _tile(T, tb)
    has_res = residual is not None

    def body(*refs):
        x_ref, g_ref = refs[:2]
        o_ref = refs[-1]
        xf = x_ref[...]
        y = xf * lax.rsqrt(jnp.mean(xf * xf, axis=-1, keepdims=True) + NORM_EPS) * g_ref[...]
        if has_res:
            y = refs[2][...] + y
        o_ref[...] = y.astype(o_ref.dtype)

    row = pl.BlockSpec((tb, W), lambda i: (i, 0))
    ins = [x, gain] + ([residual] if has_res else [])
    return _call(
        body, name, (T // tb,),
        [pl.BlockSpec((tb, W), lambda i: (i, col_block)),
         pl.BlockSpec((1, W), lambda i: (0, 0))] + ([row] if has_res else []),
        [row], [jax.ShapeDtypeStruct((T, W), out_dtype)], ins, sem=("parallel",), comm=comm)[0]


def _rms_bwd(dy, x, gain, out_dtype, name, width=None, col_block=0, residual=None, tb=256):
    T = dy.shape[0]
    W = x.shape[1] if width is None else width
    tb = _tile(T, tb)
    has_res = residual is not None

    def body(*refs):
        dy_ref, x_ref, g_ref = refs[:3]
        dx_ref, dg_ref = refs[-2:]
        i = pl.program_id(0)
        xf = x_ref[...]
        r = lax.rsqrt(jnp.mean(xf * xf, axis=-1, keepdims=True) + NORM_EPS)
        xn = xf * r
        dyf = dy_ref[...].astype(F32)
        dyg = dyf * g_ref[...]
        dx = r * (dyg - xn * jnp.mean(dyg * xn, axis=-1, keepdims=True))
        if has_res:
            dx = refs[3][...] + dx
        dx_ref[...] = dx.astype(dx_ref.dtype)

        @pl.when(i == 0)
        def _():
            dg_ref[...] = jnp.zeros_like(dg_ref)

        dg_ref[...] += jnp.sum(dyf * xn, axis=0, keepdims=True)

    row = pl.BlockSpec((tb, W), lambda i: (i, 0))
    vec = pl.BlockSpec((1, W), lambda i: (0, 0))
    ins = [dy, x, gain] + ([residual] if has_res else [])
    return pl.pallas_call(
        body, name=name, grid=(T // tb,),
        in_specs=[row, pl.BlockSpec((tb, W), lambda i: (i, col_block)), vec] + ([row] if has_res else []),
        out_specs=[row, vec],
        out_shape=[jax.ShapeDtypeStruct((T, W), out_dtype), jax.ShapeDtypeStruct((1, W), F32)],
        compiler_params=_params("arbitrary"),
    )(*ins)


def _loss_head(x2, y2, gain, target, name, tb=256):
    T, D = x2.shape
    tb = _tile(T, tb)

    def body(x2_ref, y2_ref, g_ref, t_ref, dx3_ref, loss_ref):
        i = pl.program_id(0)
        yf = y2_ref[...]
        x3 = x2_ref[...] + yf * lax.rsqrt(jnp.mean(yf * yf, axis=-1, keepdims=True) + NORM_EPS) * g_ref[...]
        e = x3 - t_ref[...]
        dx3_ref[...] = e * (1.0 / D)

        @pl.when(i == 0)
        def _():
            loss_ref[...] = jnp.zeros_like(loss_ref)

        per_row = jnp.mean(e * e, axis=-1, keepdims=True)
        loss_ref[...] += 0.5 * jnp.sum(per_row, axis=0, keepdims=True)

    row = pl.BlockSpec((tb, D), lambda i: (i, 0))
    return pl.pallas_call(
        body, name=name, grid=(T // tb,),
        in_specs=[row, row, pl.BlockSpec((1, D), lambda i: (0, 0)), row],
        out_specs=[row, pl.BlockSpec((1, 1), lambda i: (0, 0))],
        out_shape=[jax.ShapeDtypeStruct((T, D), F32), jax.ShapeDtypeStruct((1, 1), F32)],
        compiler_params=_params("arbitrary"),
    )(x2, y2, gain, target)


def _rope_tables(positions, rot_dim):
    half = rot_dim // 2
    inv_freq = ROPE_THETA ** (-jnp.arange(0, rot_dim, 2, dtype=F32) / rot_dim)
    ang = positions.astype(F32)[:, None] * inv_freq[None, :]
    cos, sin = jnp.cos(ang), jnp.sin(ang)
    T = positions.shape[0]
    ones = jnp.ones((T, HD - rot_dim), F32)
    cos_t = jnp.concatenate([cos, cos, ones], axis=1)
    sin_t = jnp.concatenate([-sin, sin, jnp.zeros_like(ones)], axis=1)
    return cos_t, sin_t


def _rotate(x, cos_t, sin_t, half):
    lane = lax.broadcasted_iota(jnp.int32, x.shape, 1)
    swapped = jnp.where(lane < half, pltpu.roll(x, HD - half, 1), pltpu.roll(x, half, 1))
    return x * cos_t + swapped * sin_t


def _rope_apply(x, cos_t, sin_t, half, n_blocks, rope_lo, rope_hi, out_dtype, name, col_off=0, tb=512,
                rope_mod=(1, 0)):
    T = x.shape[0]
    tb = _tile(T, tb)

    def body(x_ref, c_ref, s_ref, o_ref):
        j = pl.program_id(1)
        is_rope = jnp.logical_and(jnp.logical_and(j >= rope_lo, j < rope_hi), j % rope_mod[0] == rope_mod[1])

        @pl.when(is_rope)
        def _():
            o_ref[...] = _rotate(x_ref[...].astype(F32), c_ref[...], s_ref[...], half).astype(o_ref.dtype)

        @pl.when(jnp.logical_not(is_rope))
        def _():
            o_ref[...] = x_ref[...].astype(o_ref.dtype)

    tab = pl.BlockSpec((tb, HD), lambda i, j: (i, 0))
    return pl.pallas_call(
        body, name=name, grid=(T // tb, n_blocks),
        in_specs=[pl.BlockSpec((tb, HD), lambda i, j: (i, j + col_off)), tab, tab],
        out_specs=pl.BlockSpec((tb, HD), lambda i, j: (i, j)),
        out_shape=jax.ShapeDtypeStruct((T, n_blocks * HD), out_dtype),
        compiler_params=_params("parallel", "parallel"),
    )(x, cos_t, sin_t)


def _shared_key_grad(dk, cos_t, sin_t_neg, half, name, tb=512):
    T = dk.shape[0]
    tb = _tile(T, tb)

    def body(d_ref, c_ref, s_ref, o_ref):
        tot = d_ref[:, HD:2 * HD]
        for h in range(1, HEADS):
            tot = tot + d_ref[:, h * QK + HD:(h + 1) * QK]
        o_ref[...] = _rotate(tot, c_ref[...], s_ref[...], half).astype(o_ref.dtype)

    tab = pl.BlockSpec((tb, HD), lambda i: (i, 0))
    return pl.pallas_call(
        body, name=name, grid=(T // tb,),
        in_specs=[pl.BlockSpec((tb, HEADS * QK), lambda i: (i, 0)), tab, tab],
        out_specs=tab, out_shape=jax.ShapeDtypeStruct((T, HD), BF16),
        compiler_params=_params("parallel"),
    )(dk, cos_t, sin_t_neg)


def _band_masks(n):
    row = lax.broadcasted_iota(jnp.int32, (QBLK, QBLK), 0)
    col = lax.broadcasted_iota(jnp.int32, (QBLK, QBLK), 1)
    return col <= row, jnp.logical_and(col >= row, n > 0)


def _dswa_specs(d):
    cur = pl.BlockSpec((QBLK, AW), lambda r, n: (n, r))
    prev = pl.BlockSpec((QBLK, AW), lambda r, n: (jnp.maximum(n - 1, 0), r))
    return cur, prev


def _dswa_fwd(qkv, d, name):
    T = qkv.shape[0]
    sd = T // d
    nb = sd // QBLK
    q = qkv[:, :AW].reshape(sd, d * AW)
    k = qkv[:, AW:2 * AW].reshape(sd, d * AW)
    v = qkv[:, 2 * AW:].reshape(sd, d * AW)

    def body(q_ref, kc_ref, kp_ref, vc_ref, vp_ref, o_ref, l_ref):
        cur_mask, prev_mask = _band_masks(pl.program_id(1))
        for h in range(HEADS):
            sl = slice(h * HD, (h + 1) * HD)
            qh = q_ref[:, sl]
            sc = jnp.where(cur_mask, _dot(qh, kc_ref[:, sl], NT) * SCALE_A, NEG_INF)
            sp = jnp.where(prev_mask, _dot(qh, kp_ref[:, sl], NT) * SCALE_A, NEG_INF)
            m = jnp.maximum(jnp.max(sc, axis=-1, keepdims=True), jnp.max(sp, axis=-1, keepdims=True))
            pc = jnp.exp(sc - m)
            pp = jnp.exp(sp - m)
            den = jnp.sum(pc, axis=-1, keepdims=True) + jnp.sum(pp, axis=-1, keepdims=True)
            o = (_dot((pc / den).astype(BF16), vc_ref[:, sl], NN)
                 + _dot((pp / den).astype(BF16), vp_ref[:, sl], NN))
            o_ref[:, sl] = o
            l_ref[:, sl] = jnp.broadcast_to(m + jnp.log(den), (QBLK, HD))

    cur, prev = _dswa_specs(d)
    o, lse = pl.pallas_call(
        body, name=name, grid=(d, nb),
        in_specs=[cur, cur, prev, cur, prev], out_specs=[cur, cur],
        out_shape=[jax.ShapeDtypeStruct((sd, d * AW), F32)] * 2,
        compiler_params=_params("parallel", "parallel"),
    )(q, k, k, v, v)
    return o.reshape(T, AW), lse.reshape(T, AW)


def _dswa_merge(outs, lses, name, tb=256):
    T = outs[0].shape[0]
    tb = _tile(T, tb)
    nc = len(outs)

    def body(*refs):
        o_refs, l_refs = refs[:nc], refs[nc:2 * nc]
        out_ref, outb_ref, lt_ref = refs[2 * nc:]
        ls = [l[...] for l in l_refs]
        m = functools.reduce(jnp.maximum, ls)
        es = [jnp.exp(l - m) for l in ls]
        tot = functools.reduce(lambda a, b: a + b, es)
        acc = (es[0] / tot) * o_refs[0][...]
        for c in range(1, nc):
            acc = acc + (es[c] / tot) * o_refs[c][...]
        out_ref[...] = acc
        outb_ref[...] = acc.astype(BF16)
        lt_ref[...] = m + jnp.log(tot)

    row = pl.BlockSpec((tb, AW), lambda i: (i, 0))
    return pl.pallas_call(
        body, name=name, grid=(T // tb,),
        in_specs=[row] * (2 * nc), out_specs=[row] * 3,
        out_shape=[jax.ShapeDtypeStruct((T, AW), F32), jax.ShapeDtypeStruct((T, AW), BF16),
                   jax.ShapeDtypeStruct((T, AW), F32)],
        compiler_params=_params("parallel"),
    )(*outs, *lses)


def _delta_prep(dout, col_block, out, name, tb=256):
    T = out.shape[0]
    tb = _tile(T, tb)

    def body(do_ref, o_ref, delta_ref, dob_ref):
        for h in range(HEADS):
            sl = slice(h * HD, (h + 1) * HD)
            doh = do_ref[:, sl]
            delta_ref[:, sl] = jnp.broadcast_to(jnp.sum(doh * o_ref[:, sl], axis=-1, keepdims=True), (tb, HD))
            dob_ref[:, sl] = doh.astype(BF16)

    row = pl.BlockSpec((tb, AW), lambda i: (i, 0))
    return pl.pallas_call(
        body, name=name, grid=(T // tb,),
        in_specs=[pl.BlockSpec((tb, AW), lambda i: (i, col_block)), row], out_specs=[row, row],
        out_shape=[jax.ShapeDtypeStruct((T, AW), F32), jax.ShapeDtypeStruct((T, AW), BF16)],
        compiler_params=_params("parallel"),
    )(dout, out)


def _dswa_bwd(qkv, dout_b, lse_tot, delta, d, name, comm=None):
    T = qkv.shape[0]
    sd = T // d
    nb = sd // QBLK
    view = lambda a: a.reshape(sd, d * AW)
    q, k, v = view(qkv[:, :AW]), view(qkv[:, AW:2 * AW]), view(qkv[:, 2 * AW:])

    def body(q_ref, kc_ref, kp_ref, vc_ref, vp_ref, do_ref, l_ref, dl_ref,
             dq_ref, dkc_ref, dkp_ref, dvc_ref, dvp_ref):
        cur_mask, prev_mask = _band_masks(pl.program_id(1))
        for h in range(HEADS):
            sl = slice(h * HD, (h + 1) * HD)
            qh, kc, kp, vc, vp, doh = q_ref[:, sl], kc_ref[:, sl], kp_ref[:, sl], vc_ref[:, sl], vp_ref[:, sl], do_ref[:, sl]
            lse = l_ref[:, sl][:, 0:1]
            dlt = dl_ref[:, sl][:, 0:1]
            sc = jnp.where(cur_mask, _dot(qh, kc, NT) * SCALE_A, NEG_INF)
            sp = jnp.where(prev_mask, _dot(qh, kp, NT) * SCALE_A, NEG_INF)
            pc = jnp.exp(sc - lse)
            pp = jnp.exp(sp - lse)
            dsc = (pc * (_dot(doh, vc, NT) - dlt) * SCALE_A).astype(BF16)
            dsp = (pp * (_dot(doh, vp, NT) - dlt) * SCALE_A).astype(BF16)
            dq_ref[:, sl] = _dot(dsc, kc, NN) + _dot(dsp, kp, NN)
            dkc_ref[:, sl] = _dot(dsc, qh, TN)
            dkp_ref[:, sl] = _dot(dsp, qh, TN)
            dvc_ref[:, sl] = _dot(pc.astype(BF16), doh, TN)
            dvp_ref[:, sl] = _dot(pp.astype(BF16), doh, TN)

    cur, prev = _dswa_specs(d)
    res = _call(
        body, name, (d, nb), [cur, cur, prev, cur, prev, cur, cur, cur], [cur] * 5,
        [jax.ShapeDtypeStruct((sd, d * AW), F32)] * 5, (q, k, k, v, v, view(dout_b), view(lse_tot), view(delta)),
        sem=("parallel", "parallel"), comm=comm)
    return [r.reshape(T, AW) for r in res]


def _dswa_combine(grads, cos_t, sin_t_neg, name):
    T = grads[0][0].shape[0]
    nbt = T // QBLK
    half = ROT_A // 2

    def body(*refs):
        c_ref, s_ref = refs[-5:-3]
        dq_ref, dk_ref, dv_ref = refs[-3:]
        i = pl.program_id(0)
        dq = dk = dv = None
        for ci, d in enumerate(DSWA_DILATIONS):
            gq, gkc, gkp, gvc, gvp = refs[5 * ci:5 * ci + 5]
            has_next = i + d < nbt
            gk = gkc[...] + jnp.where(has_next, gkp[...], 0.0)
            gv = gvc[...] + jnp.where(has_next, gvp[...], 0.0)
            dq = gq[...] if dq is None else dq + gq[...]
            dk = gk if dk is None else dk + gk
            dv = gv if dv is None else dv + gv
        for h in range(HEADS):
            sl = slice(h * HD, (h + 1) * HD)
            dq_ref[:, sl] = _rotate(dq[:, sl], c_ref[...], s_ref[...], half).astype(BF16)
            dk_ref[:, sl] = _rotate(dk[:, sl], c_ref[...], s_ref[...], half).astype(BF16)
        dv_ref[...] = dv.astype(BF16)

    row = pl.BlockSpec((QBLK, AW), lambda i: (i, 0))
    tab = pl.BlockSpec((QBLK, HD), lambda i: (i, 0))
    in_specs, ins = [], []
    for d, g in zip(DSWA_DILATIONS, grads):
        nxt = pl.BlockSpec((QBLK, AW), functools.partial(lambda i, d: (jnp.minimum(i + d, nbt - 1), 0), d=d))
        in_specs += [row, row, nxt, row, nxt]
        ins += list(g)
    return pl.pallas_call(
        body, name=name, grid=(nbt,),
        in_specs=in_specs + [tab, tab], out_specs=[row] * 3,
        out_shape=[jax.ShapeDtypeStruct((T, AW), BF16)] * 3,
        compiler_params=_params("parallel"),
    )(*ins, cos_t, sin_t_neg)


MLA_TQ = 512
QK = 2 * HD


def _triangle(nq, key_major):
    pairs = [(q, k) for q in range(nq) for k in range(q + 1)]
    if key_major:
        pairs.sort(key=lambda p: (p[1], p[0]))
    return (jnp.array([p[0] for p in pairs], jnp.int32), jnp.array([p[1] for p in pairs], jnp.int32))


def _mla_specs(tq):
    q_spec = pl.BlockSpec((tq, HEADS * QK), lambda t, qi, ki: (qi[t], 0))
    k_spec = pl.BlockSpec((tq, HEADS * QK), lambda t, qi, ki: (ki[t], 0))
    v_spec = pl.BlockSpec((tq, AW), lambda t, qi, ki: (ki[t], 1))
    qrow = pl.BlockSpec((tq, AW), lambda t, qi, ki: (qi[t], 0))
    krow = pl.BlockSpec((tq, AW), lambda t, qi, ki: (ki[t], 0))
    return q_spec, k_spec, v_spec, qrow, krow


def _mla_scores(q_ref, k_ref, h, qi, ki, tq):
    s = _dot(q_ref[:, h * QK:(h + 1) * QK], k_ref[:, h * QK:(h + 1) * QK], NT) * SCALE_B
    row = lax.broadcasted_iota(jnp.int32, s.shape, 0) + qi * tq
    col = lax.broadcasted_iota(jnp.int32, s.shape, 1) + ki * tq
    return jnp.where(col <= row, s, NEG_INF)


def _mla_fwd(q, k, kv, name, comm=None):
    T = q.shape[0]
    tq = _tile(T, MLA_TQ)
    tables = _triangle(T // tq, False)

    def body(qi_ref, ki_ref, q_ref, k_ref, v_ref, o_ref, l_ref, m_s, l_s, acc):
        t = pl.program_id(0)
        qi, ki = qi_ref[t], ki_ref[t]

        @pl.when(ki == 0)
        def _():
            m_s[...] = jnp.full_like(m_s, NEG_INF)
            l_s[...] = jnp.zeros_like(l_s)
            acc[...] = jnp.zeros_like(acc)

        for h in range(HEADS):
            sl = slice(h * HD, (h + 1) * HD)
            s = _mla_scores(q_ref, k_ref, h, qi, ki, tq)
            m_new = jnp.maximum(m_s[h], jnp.max(s, axis=-1, keepdims=True))
            alpha = jnp.exp(m_s[h] - m_new)
            p = jnp.exp(s - m_new)
            l_s[h] = alpha * l_s[h] + jnp.sum(p, axis=-1, keepdims=True)
            acc[:, sl] = alpha * acc[:, sl] + _dot(p.astype(BF16), v_ref[:, sl], NN)
            m_s[h] = m_new

        @pl.when(ki == qi)
        def _():
            for h in range(HEADS):
                sl = slice(h * HD, (h + 1) * HD)
                o_ref[:, sl] = acc[:, sl] / l_s[h]
                l_ref[:, sl] = jnp.broadcast_to(m_s[h] + jnp.log(l_s[h]), (tq, HD))

    q_spec, k_spec, v_spec, qrow, _ = _mla_specs(tq)
    return _call(
        body, name, (tables[0].shape[0],), [q_spec, k_spec, v_spec], [qrow, qrow],
        [jax.ShapeDtypeStruct((T, AW), F32)] * 2, (q, k, kv),
        scratch=[pltpu.VMEM((HEADS, tq, 1), F32), pltpu.VMEM((HEADS, tq, 1), F32), pltpu.VMEM((tq, AW), F32)],
        sem=("arbitrary",), comm=comm, prefetch=tables)


def _mla_ds(q_ref, k_ref, v_ref, do_ref, l_ref, dl_ref, h, qi, ki, tq):
    sl = slice(h * HD, (h + 1) * HD)
    p = jnp.exp(_mla_scores(q_ref, k_ref, h, qi, ki, tq) - l_ref[:, sl][:, 0:1])
    ds = (p * (_dot(do_ref[:, sl], v_ref[:, sl], NT) - dl_ref[:, sl][:, 0:1]) * SCALE_B).astype(BF16)
    return p, ds


def _mla_bwd_q(q, k, kv, dout_b, lse, delta, name, comm=None):
    T = q.shape[0]
    tq = _tile(T, MLA_TQ)
    tables = _triangle(T // tq, False)

    def body(qi_ref, ki_ref, q_ref, k_ref, v_ref, do_ref, l_ref, dl_ref, dq_ref):
        t = pl.program_id(0)
        qi, ki = qi_ref[t], ki_ref[t]

        @pl.when(ki == 0)
        def _():
            dq_ref[...] = jnp.zeros_like(dq_ref)

        for h in range(HEADS):
            _, ds = _mla_ds(q_ref, k_ref, v_ref, do_ref, l_ref, dl_ref, h, qi, ki, tq)
            dq_ref[:, h * QK:(h + 1) * QK] += _dot(ds, k_ref[:, h * QK:(h + 1) * QK], NN)

    q_spec, k_spec, v_spec, qrow, _ = _mla_specs(tq)
    return _call(
        body, name, (tables[0].shape[0],), [q_spec, k_spec, v_spec, qrow, qrow, qrow], [q_spec],
        [jax.ShapeDtypeStruct((T, HEADS * QK), F32)], (q, k, kv, dout_b, lse, delta),
        sem=("arbitrary",), comm=comm, prefetch=tables)[0]


def _mla_bwd_kv(q, k, kv, dout_b, lse, delta, name, comm=None):
    T = q.shape[0]
    tq = _tile(T, MLA_TQ)
    tables = _triangle(T // tq, True)

    def body(qi_ref, ki_ref, q_ref, k_ref, v_ref, do_ref, l_ref, dl_ref, dk_ref, dv_ref):
        t = pl.program_id(0)
        qi, ki = qi_ref[t], ki_ref[t]

        @pl.when(qi == ki)
        def _():
            dk_ref[...] = jnp.zeros_like(dk_ref)
            dv_ref[...] = jnp.zeros_like(dv_ref)

        for h in range(HEADS):
            sl = slice(h * HD, (h + 1) * HD)
            p, ds = _mla_ds(q_ref, k_ref, v_ref, do_ref, l_ref, dl_ref, h, qi, ki, tq)
            dv_ref[:, sl] += _dot(p.astype(BF16), do_ref[:, sl], TN)
            dk_ref[:, h * QK:(h + 1) * QK] += _dot(ds, q_ref[:, h * QK:(h + 1) * QK], TN)

    q_spec, k_spec, v_spec, qrow, krow = _mla_specs(tq)
    return _call(
        body, name, (tables[0].shape[0],), [q_spec, k_spec, v_spec, qrow, qrow, qrow], [k_spec, krow],
        [jax.ShapeDtypeStruct((T, HEADS * QK), F32), jax.ShapeDtypeStruct((T, AW), F32)],
        (q, k, kv, dout_b, lse, delta), sem=("arbitrary",), comm=comm, prefetch=tables)


def _adamw(parts, w, m, v, name, tb=128):
    R, C = w.shape
    tb = _tile(R, tb)
    c1 = 1.0 - ADAM_B1
    c2 = 1.0 - ADAM_B2
    bc1 = 1.0 - ADAM_B1 ** ADAM_STEP
    bc2 = 1.0 - ADAM_B2 ** ADAM_STEP

    def body(p_ref, w_ref, m_ref, v_ref, g_ref, d_ref, nm_ref, nv_ref):
        g = p_ref[0].astype(F32)
        for j in range(1, N_DEV):
            g = g + p_ref[j].astype(F32)
        nm = ADAM_B1 * m_ref[...] + c1 * g
        nv = ADAM_B2 * v_ref[...] + c2 * (g * g)
        g_ref[...] = g
        nm_ref[...] = nm
        nv_ref[...] = nv
        d_ref[...] = -ADAM_LR * ((nm / bc1) / (jnp.sqrt(nv / bc2) + ADAM_EPS) + ADAM_WD * w_ref[...])

    row = pl.BlockSpec((tb, C), lambda i: (i, 0))
    return pl.pallas_call(
        body, name=name, grid=(R // tb,),
        in_specs=[pl.BlockSpec((N_DEV, tb, C), lambda i: (0, i, 0)), row, row, row],
        out_specs=[row] * 4, out_shape=[jax.ShapeDtypeStruct((R, C), F32)] * 4,
        compiler_params=_params("parallel"),
    )(parts, w, m, v)


def _cols_from_shards(g):
    return jnp.transpose(g, (1, 0, 2)).reshape(g.shape[1], N_DEV * g.shape[2])


def _cols_to_shards(w):
    return jnp.transpose(w.reshape(w.shape[0], N_DEV, w.shape[1] // N_DEV), (1, 0, 2))


def _split_heads(w, first):
    w3 = w.reshape(w.shape[0], HEADS, -1)
    return w3[:, :, :first].reshape(w.shape[0], -1), w3[:, :, first:].reshape(w.shape[0], -1)


def _join_heads(a, b):
    R = a.shape[0]
    return jnp.concatenate([a.reshape(R, HEADS, -1), b.reshape(R, HEADS, -1)], axis=2).reshape(R, -1)


def _pad_heads(w, width):
    w3 = w.reshape(w.shape[0], HEADS, -1)
    return jnp.pad(w3, ((0, 0), (0, 0), (0, width - w3.shape[2]))).reshape(w.shape[0], HEADS * width)


def _unpad_heads(w, k):
    return w.reshape(w.shape[0], HEADS, -1)[:, :, :k].reshape(w.shape[0], HEADS * k)


def kernel(x, positions, norm_attn_pre, norm_attn_post, w_in, q_latent_norm, kv_latent_norm, w_uq, w_ukv, w_out, norm_mlp_pre, norm_mlp_post, w_up, w_down, loss_target, m_norm_attn_pre, m_norm_attn_post, m_w_in, m_q_latent_norm, m_kv_latent_norm, m_w_uq, m_w_ukv, m_w_out, m_norm_mlp_pre, m_norm_mlp_post, m_w_up, m_w_down, v_norm_attn_pre, v_norm_attn_post, v_w_in, v_q_latent_norm, v_kv_latent_norm, v_w_uq, v_w_ukv, v_w_out, v_norm_mlp_pre, v_norm_mlp_post, v_w_up, v_w_down):
    xs = x[0]
    tgt = loss_target[0]
    pos = positions[0]
    T, D = xs.shape
    big = dict(w_in=(w_in, m_w_in, v_w_in), w_uq=(w_uq, m_w_uq, v_w_uq), w_ukv=(w_ukv, m_w_ukv, v_w_ukv),
               w_out=(w_out, m_w_out, v_w_out), w_up=(w_up, m_w_up, v_w_up), w_down=(w_down, m_w_down, v_w_down))
    big = {n: tuple(t[0] for t in ts) for n, ts in big.items()}
    big_names = ["w_in", "w_uq", "w_ukv", "w_out", "w_up", "w_down"]
    col_sharded = {"w_in", "w_uq", "w_ukv", "w_up"}

    wb = {n: big[n][0].astype(BF16) for n in big_names}

    def gathered(ex, i, n):
        g = ex.results[i]
        return _cols_from_shards(g) if n in col_sharded else g.reshape(-1, g.shape[2])

    def scatter_of(g, n):
        return _Exchange([_cols_to_shards(g) if n in col_sharded
                          else g.reshape(N_DEV, g.shape[0] // N_DEV, g.shape[1])], True)

    cos_a, sin_a = _rope_tables(pos, ROT_A)
    cos_b, sin_b = _rope_tables(pos, ROPE_MLA)

    ex_in = _Exchange([wb["w_in"]], False)
    h1 = _rms_fwd(xs, norm_attn_pre, BF16, "norm_attn_pre_fwd", comm=ex_in)
    Wi = jnp.pad(gathered(ex_in, 0, "w_in"), ((0, 0), (0, IN_PAD - IN_COLS)))
    ex_mid = _Exchange([wb["w_uq"], wb["w_ukv"], wb["w_out"]], False)
    proj = _mm(h1, Wi, "nn", [F32], "proj_in", tn=1408, comm=ex_mid)
    Wuq = _pad_heads(gathered(ex_mid, 0, "w_uq"), QK)
    Wukv = jnp.concatenate(_split_heads(gathered(ex_mid, 1, "w_ukv"), HD), axis=1)
    Wo = gathered(ex_mid, 2, "w_out")
    qkv = _rope_apply(proj, cos_a, sin_a, ROT_A // 2, 3 * HEADS, 0, 2 * HEADS, BF16, "rope_dswa")
    outs, lses = [], []
    for d in DSWA_DILATIONS:
        o, l = _dswa_fwd(qkv, d, f"dswa_fwd_d{d}")
        outs.append(o)
        lses.append(l)
    a_out, a_out_b, a_lse = _dswa_merge(outs, lses, "dswa_merge")

    cqn = _rms_fwd(proj, q_latent_norm, BF16, "q_latent_norm_fwd", width=Q_LORA, col_block=3 * AW // Q_LORA)
    ckvn = _rms_fwd(proj, kv_latent_norm, BF16, "kv_latent_norm_fwd", width=KV_LORA, col_block=3 * AW // KV_LORA + 1)
    qb = _mm(cqn, Wuq, "nn", [F32], "q_up")
    kvb = _mm(ckvn, Wukv, "nn", [BF16], "kv_up")
    q_mla = _rope_apply(qb, cos_b, sin_b, ROPE_MLA // 2, 2 * HEADS, 0, 2 * HEADS, BF16, "rope_mla_q", rope_mod=(2, 1))
    kr = _rope_apply(proj, cos_b, sin_b, ROPE_MLA // 2, 1, 0, 1, BF16, "rope_mla_k", col_off=(IN_PAD - HD) // HD)
    k_mla = jnp.concatenate([kvb[:, :AW].reshape(T, HEADS, HD), jnp.broadcast_to(kr[:, None, :], (T, HEADS, HD))],
                            axis=2).reshape(T, HEADS * QK)
    ex_up = _Exchange([wb["w_up"]], False)
    b_out, b_lse = _mla_fwd(q_mla, k_mla, kvb, "mla_fwd", comm=ex_up)
    Wup = gathered(ex_up, 0, "w_up")

    mixed = jnp.concatenate([a_out_b, b_out.astype(BF16)], axis=1)
    y1 = _mm(mixed, Wo, "nn", [F32], "attn_out")
    x2 = _rms_fwd(y1, norm_attn_post, F32, "norm_attn_post_fwd", residual=xs)

    h2 = _rms_fwd(x2, norm_mlp_pre, BF16, "norm_mlp_pre_fwd")

    def relu2(z):
        r = jnp.maximum(z, 0.0)
        return r * r, r

    ex_down = _Exchange([wb["w_down"]], False)
    u, zr = _mm(h2, Wup, "nn", [BF16, BF16], "mlp_up", epilogue=relu2, comm=ex_down)
    Wdn = gathered(ex_down, 0, "w_down")
    y2 = _mm(u, Wdn, "nn", [F32], "mlp_down")
    dx3, loss_part = _loss_head(x2, y2, norm_mlp_post, tgt, "loss_head")

    dy2, dg_mlp_post = _rms_bwd(dx3, y2, norm_mlp_post, BF16, "norm_mlp_post_bwd")
    dz = _mm(dy2, Wdn, "nt", [BF16], "mlp_down_dx", epilogue=lambda du, r: (du * (2.0 * r.astype(F32)),), extras=(zr,))
    g_down = _mm(u, dy2, "tn", [BF16], "mlp_down_dw")
    g_up = _mm(h2, dz, "tn", [BF16], "mlp_up_dw")
    dh2 = _mm(dz, Wup, "nt", [F32], "mlp_up_dx")
    dx2, dg_mlp_pre = _rms_bwd(dh2, x2, norm_mlp_pre, F32, "norm_mlp_pre_bwd", residual=dx3)

    dy1, dg_attn_post = _rms_bwd(dx2, y1, norm_attn_post, BF16, "norm_attn_post_bwd")
    dmixed = _mm(dy1, Wo, "nt", [F32], "attn_out_dx")
    g_out = _mm(mixed, dy1, "tn", [BF16], "attn_out_dw")

    b_delta, b_dout = _delta_prep(dmixed, 1, b_out, "mla_delta")
    sc_down = scatter_of(g_down, "w_down")
    dq_mla = _mla_bwd_q(q_mla, k_mla, kvb, b_dout, b_lse, b_delta, "mla_bwd_q", comm=sc_down)
    sc_up = scatter_of(g_up, "w_up")
    dk_mla, dvb = _mla_bwd_kv(q_mla, k_mla, kvb, b_dout, b_lse, b_delta, "mla_bwd_kv", comm=sc_up)
    dqb = _rope_apply(dq_mla, cos_b, -sin_b, ROPE_MLA // 2, 2 * HEADS, 0, 2 * HEADS, BF16, "rope_mla_q_bwd",
                      rope_mod=(2, 1))
    dkn = dk_mla.reshape(T, HEADS, QK)[:, :, :HD].reshape(T, AW)
    dkvb = jnp.concatenate([dkn, dvb], axis=1).astype(BF16)
    d_kr = _shared_key_grad(dk_mla, cos_b, -sin_b, ROPE_MLA // 2, "rope_mla_k_bwd")
    g_uq_pad = _mm(cqn, dqb, "tn", [BF16], "q_up_dw")
    g_ukv_perm = _mm(ckvn, dkvb, "tn", [BF16], "kv_up_dw")
    dcqn = _mm(dqb, Wuq, "nt", [F32], "q_up_dx")
    dckvn = _mm(dkvb, Wukv, "nt", [F32], "kv_up_dx")
    d_cq, dg_q = _rms_bwd(dcqn, proj, q_latent_norm, BF16, "q_latent_norm_bwd", width=Q_LORA, col_block=3 * AW // Q_LORA)
    d_ckv, dg_kv = _rms_bwd(dckvn, proj, kv_latent_norm, BF16, "kv_latent_norm_bwd", width=KV_LORA,
                            col_block=3 * AW // KV_LORA + 1)

    g_uq = _unpad_heads(g_uq_pad, HD + ROPE_MLA)
    g_ukv = _join_heads(g_ukv_perm[:, :AW], g_ukv_perm[:, AW:])
    sc_out = scatter_of(g_out, "w_out")
    sc_uq = _Exchange([_cols_to_shards(g_uq), _cols_to_shards(g_ukv)], True)
    a_delta, a_dout = _delta_prep(dmixed, 0, a_out, "dswa_delta")
    a_grads = [_dswa_bwd(qkv, a_dout, a_lse, a_delta, d, f"dswa_bwd_d{d}", comm=cm)
               for d, cm in zip(DSWA_DILATIONS, (sc_out, sc_uq, None))]
    d_aq, d_ak, d_av = _dswa_combine(a_grads, cos_a, -sin_a, "dswa_combine")

    dproj = jnp.concatenate([d_aq, d_ak, d_av, d_cq, d_ckv, d_kr], axis=1)
    g_in_pad = _mm(h1, dproj, "tn", [BF16], "proj_in_dw", tn=1408)
    sc_in = scatter_of(g_in_pad[:, :IN_COLS], "w_in")
    dh1 = _mm(dproj, Wi, "nt", [F32], "proj_in_dx", comm=sc_in)
    grad_x, dg_attn_pre = _rms_bwd(dh1, xs, norm_attn_pre, F32, "norm_attn_pre_bwd", residual=dx2)

    parts = dict(w_in=sc_in.results[0], w_uq=sc_uq.results[0], w_ukv=sc_uq.results[1], w_out=sc_out.results[0],
                 w_up=sc_up.results[0], w_down=sc_down.results[0])
    big_out = {n: _adamw(parts[n], *big[n], f"adamw_{n}") for n in big_names}

    gain_names = ["norm_attn_pre", "norm_attn_post", "q_latent_norm", "kv_latent_norm", "norm_mlp_pre", "norm_mlp_post"]
    gain_args = dict(norm_attn_pre=(norm_attn_pre, m_norm_attn_pre, v_norm_attn_pre),
                     norm_attn_post=(norm_attn_post, m_norm_attn_post, v_norm_attn_post),
                     q_latent_norm=(q_latent_norm, m_q_latent_norm, v_q_latent_norm),
                     kv_latent_norm=(kv_latent_norm, m_kv_latent_norm, v_kv_latent_norm),
                     norm_mlp_pre=(norm_mlp_pre, m_norm_mlp_pre, v_norm_mlp_pre),
                     norm_mlp_post=(norm_mlp_post, m_norm_mlp_post, v_norm_mlp_post))
    gain_grads = dict(norm_attn_pre=dg_attn_pre, norm_attn_post=dg_attn_post, q_latent_norm=dg_q,
                      kv_latent_norm=dg_kv, norm_mlp_pre=dg_mlp_pre, norm_mlp_post=dg_mlp_post)
    packed = jnp.concatenate([gain_grads[n] for n in gain_names], axis=1)
    gain_parts = _Exchange([packed], False).standalone("gather_gain_grads")[0]
    pack3 = lambda i: jnp.concatenate([gain_args[n][i] for n in gain_names], axis=1)
    gain_out = _adamw(gain_parts, pack3(0), pack3(1), pack3(2), "adamw_gains", tb=1)
    offs = [0]
    for n in gain_names:
        offs.append(offs[-1] + gain_args[n][0].shape[1])
    small_out = {n: tuple(o[:, offs[i]:offs[i + 1]] for o in gain_out) for i, n in enumerate(gain_names)}

    loss = lax.psum(loss_part[0, 0], ("x", "y", "c"))

    order = ["norm_attn_pre", "norm_attn_post", "w_in", "q_latent_norm", "kv_latent_norm", "w_uq", "w_ukv", "w_out",
             "norm_mlp_pre", "norm_mlp_post", "w_up", "w_down"]
    res = {n: (small_out[n] if n in small_out else tuple(o[None] for o in big_out[n])) for n in order}
    return (loss, grad_x[None], *[res[n][0] for n in order], *[res[n][1] for n in order],
            *[res[n][2] for n in order], *[res[n][3] for n in order])
```

```python
import functools
import math

import jax
import jax.numpy as jnp
from jax import lax
from jax.experimental import pallas as pl
from jax.experimental.pallas import tpu as pltpu

F32 = jnp.float32
BF16 = jnp.bfloat16

N_DEV = 8
HEADS = 8
HD = 128
AW = HEADS * HD
Q_LORA = 512
KV_LORA = 512
ROPE_MLA = 64
ROT_A = 32
IN_COLS = 3 * AW + Q_LORA + KV_LORA + ROPE_MLA
IN_PAD = 3 * AW + Q_LORA + KV_LORA + HD
QBLK = 128
DSWA_DILATIONS = (1, 4, 16)
ROPE_THETA = 500000.0
NORM_EPS = 1e-6
NEG_INF = -1e30
SCALE_A = HD ** -0.5
SCALE_B = (HD + ROPE_MLA) ** -0.5

ADAM_LR = 0.001
ADAM_B1 = 0.9
ADAM_B2 = 0.999
ADAM_EPS = 1e-08
ADAM_WD = 0.01
ADAM_STEP = 10

VMEM_LIMIT = 48 * 1024 * 1024

NT = (((1,), (1,)), ((), ()))
NN = (((1,), (0,)), ((), ()))
TN = (((0,), (0,)), ((), ()))


def _dot(a, b, dims):
    return lax.dot_general(a, b, dims, preferred_element_type=F32)


def _params(*sem):
    return pltpu.CompilerParams(dimension_semantics=sem, vmem_limit_bytes=VMEM_LIMIT)


def _tile(n, want):
    t = min(n, want)
    while n % t:
        t //= 2
    return t


def _tile128(n, want):
    if n % 128:
        return n
    units = n // 128
    return 128 * max(u for u in range(1, max(want // 128, 1) + 1) if units % u == 0)


class _Exchange:
    def __init__(self, arrs, mode):
        self.arrs = list(arrs)
        self.mode = mode
        self.n = len(self.arrs)
        self.results = None
        hbm = pl.BlockSpec(memory_space=pltpu.HBM)
        self.specs = [hbm] * self.n
        shape = {"gather": lambda a: (N_DEV,) + a.shape, "scatter": lambda a: a.shape,
                 "pair": lambda a: a.shape[1:], "chips": lambda a: a.shape}[mode]
        self.out_shape = [jax.ShapeDtypeStruct(shape(a), a.dtype) for a in self.arrs]
        n_sem = self.n * (N_DEV - 1)
        self.scratch = [pltpu.SemaphoreType.DMA((n_sem,)), pltpu.SemaphoreType.DMA((n_sem,)),
                        pltpu.SemaphoreType.DMA((self.n,))]

    def hooks(self, ins, outs, send_sems, recv_sems, local_sems):
        x, y, c = lax.axis_index("x"), lax.axis_index("y"), lax.axis_index("c")
        me = (x, y, c)
        sib = (x, y, 1 - c)
        chips = [(1 - x, y), (x, 1 - y), (1 - x, 1 - y)]
        slot = lambda p: 4 * p[0] + 2 * p[1] + p[2]
        chip_of = lambda p: 2 * p[0] + p[1]

        def rcopy(a, k, src, dst, to):
            i = a * (N_DEV - 1) + k
            return pltpu.make_async_remote_copy(src_ref=src, dst_ref=dst, send_sem=send_sems.at[i],
                                                recv_sem=recv_sems.at[i], device_id=to,
                                                device_id_type=pl.DeviceIdType.MESH)

        def local(a):
            if self.mode == "chips":
                return pltpu.make_async_copy(ins[a].at[chip_of(me)], outs[a].at[chip_of(me)], local_sems.at[a])
            src = ins[a].at[slot(me)] if self.mode == "scatter" else ins[a]
            return pltpu.make_async_copy(src, outs[a].at[slot(me)], local_sems.at[a])

        def peer(rel):
            return (1 - x if rel & 4 else x, 1 - y if rel & 2 else y, 1 - c if rel & 1 else c)

        if self.mode == "pair":
            def start():
                for a in range(self.n):
                    rcopy(a, 0, ins[a].at[1 - c], outs[a], sib).start()

            def middle():
                pass

            def finish():
                for a in range(self.n):
                    cp = rcopy(a, 0, ins[a].at[1 - c], outs[a], sib)
                    cp.wait_send()
                    cp.wait_recv()
        elif self.mode == "chips":
            def start():
                for a in range(self.n):
                    local(a).start()
                    for j, chip in enumerate(chips):
                        rcopy(a, j, ins[a].at[chip_of(chip)], outs[a].at[chip_of(me)], (*chip, c)).start()

            def middle():
                pass

            def finish():
                for a in range(self.n):
                    for j, chip in enumerate(chips):
                        cp = rcopy(a, j, ins[a].at[chip_of(chip)], outs[a].at[chip_of(chip)], (*chip, c))
                        cp.wait_send()
                        cp.wait_recv()
                    local(a).wait()
        elif self.mode == "scatter":
            def start():
                for a in range(self.n):
                    local(a).start()
                    for rel in range(1, N_DEV):
                        rcopy(a, rel - 1, ins[a].at[slot(peer(rel))], outs[a].at[slot(me)], peer(rel)).start()

            def middle():
                pass

            def finish():
                for a in range(self.n):
                    for rel in range(1, N_DEV):
                        cp = rcopy(a, rel - 1, ins[a].at[slot(peer(rel))], outs[a].at[slot(peer(rel))], peer(rel))
                        cp.wait_send()
                        cp.wait_recv()
                    local(a).wait()
        else:
            def start():
                for a in range(self.n):
                    local(a).start()
                    rcopy(a, 0, ins[a], outs[a].at[slot(me)], sib).start()
                    for j, chip in enumerate(chips):
                        rcopy(a, 1 + j, ins[a], outs[a].at[slot(me)], (*chip, c)).start()

            def middle():
                for a in range(self.n):
                    for j, chip in enumerate(chips):
                        landed = outs[a].at[slot((*chip, c))]
                        rcopy(a, 1 + j, ins[a], landed, me).wait_recv()
                        rcopy(a, 4 + j, landed, landed, sib).start()

            def finish():
                for a in range(self.n):
                    rcopy(a, 0, ins[a], outs[a].at[slot(sib)], me).wait_recv()
                    for j, chip in enumerate(chips):
                        rcopy(a, 4 + j, ins[a], outs[a].at[slot((*chip, 1 - c))], me).wait_recv()
                    for k in range(N_DEV - 1):
                        rcopy(a, k, ins[a], outs[a].at[slot(me)], me).wait_send()
                    local(a).wait()

        return start, middle, finish

    def standalone(self, name):
        n = self.n

        def body(*refs):
            start, middle, finish = self.hooks(refs[:n], refs[n:2 * n], *refs[2 * n:])
            start()
            middle()
            finish()

        self.results = pl.pallas_call(
            body, name=name, in_specs=self.specs, out_specs=self.specs, out_shape=self.out_shape,
            scratch_shapes=self.scratch, compiler_params=pltpu.CompilerParams(has_side_effects=True),
        )(*self.arrs)
        return self.results


def _call(body, name, grid, in_specs, out_specs, out_shape, args, scratch=(), sem=(), comm=None, prefetch=()):
    npf = len(prefetch)
    if comm is None:
        spec = pltpu.PrefetchScalarGridSpec(num_scalar_prefetch=npf, grid=grid, in_specs=list(in_specs),
                                            out_specs=list(out_specs), scratch_shapes=list(scratch))
        return pl.pallas_call(body, name=name, grid_spec=spec, out_shape=list(out_shape),
                              compiler_params=_params(*sem))(*prefetch, *args)
    ni, no, ns, n = len(in_specs), len(out_specs), len(scratch), comm.n
    steps = math.prod(grid)

    def wrapped(*refs):
        pf, refs = refs[:npf], refs[npf:]
        ins, c_ins = refs[:ni], refs[ni:ni + n]
        outs, c_outs = refs[ni + n:ni + n + no], refs[ni + n + no:ni + 2 * n + no]
        scr, c_scr = refs[ni + 2 * n + no:ni + 2 * n + no + ns], refs[ni + 2 * n + no + ns:]
        start, middle, finish = comm.hooks(c_ins, c_outs, *c_scr)
        step = pl.program_id(0)
        for ax in range(1, len(grid)):
            step = step * grid[ax] + pl.program_id(ax)
        pl.when(step == 0)(start)
        pl.when(step == steps // 2)(middle)
        body(*pf, *ins, *outs, *scr)
        pl.when(step == steps - 1)(finish)

    spec = pltpu.PrefetchScalarGridSpec(num_scalar_prefetch=npf, grid=grid, in_specs=list(in_specs) + comm.specs,
                                        out_specs=list(out_specs) + comm.specs,
                                        scratch_shapes=list(scratch) + comm.scratch)
    res = pl.pallas_call(
        wrapped, name=name, grid_spec=spec, out_shape=list(out_shape) + comm.out_shape,
        compiler_params=pltpu.CompilerParams(dimension_semantics=("arbitrary",) * len(grid),
                                             vmem_limit_bytes=VMEM_LIMIT, has_side_effects=True),
    )(*prefetch, *args, *comm.arrs)
    comm.results = res[no:]
    return res[:no]


def _mm(a, b, mode, out_dtypes, name, epilogue=None, extras=(), tm=1024, tn=1024, tk=2048, comm=None):
    if mode == "tn":
        K, M = a.shape
    else:
        M, K = a.shape
    N = b.shape[0] if mode == "nt" else b.shape[1]
    tm, tn, tk = _tile128(M, tm), _tile128(N, tn), _tile128(K, tk)
    nk = K // tk
    dims = {"nn": NN, "nt": NT, "tn": TN}[mode]
    a_spec = (pl.BlockSpec((tk, tm), lambda i, j, k: (k, i)) if mode == "tn"
              else pl.BlockSpec((tm, tk), lambda i, j, k: (i, k)))
    b_spec = (pl.BlockSpec((tn, tk), lambda i, j, k: (j, k)) if mode == "nt"
              else pl.BlockSpec((tk, tn), lambda i, j, k: (k, j)))
    mn_spec = pl.BlockSpec((tm, tn), lambda i, j, k: (i, j))
    n_ex = len(extras)
    n_out = len(out_dtypes)

    def finish(acc, ex, outs):
        res = (acc,) if epilogue is None else epilogue(acc, *[e[...] for e in ex])
        for o, r in zip(outs, res):
            o[...] = r.astype(o.dtype)

    def body(*refs):
        a_ref, b_ref = refs[:2]
        ex = refs[2:2 + n_ex]
        outs = refs[2 + n_ex:2 + n_ex + n_out]
        if nk == 1:
            finish(_dot(a_ref[...], b_ref[...], dims), ex, outs)
            return
        acc = refs[-1]
        k = pl.program_id(2)

        @pl.when(k == 0)
        def _():
            acc[...] = _dot(a_ref[...], b_ref[...], dims)

        @pl.when(jnp.logical_and(k > 0, k < nk - 1))
        def _():
            acc[...] += _dot(a_ref[...], b_ref[...], dims)

        @pl.when(k == nk - 1)
        def _():
            finish(acc[...] + _dot(a_ref[...], b_ref[...], dims), ex, outs)

    out = _call(
        body, name, (M // tm, N // tn, nk), [a_spec, b_spec] + [mn_spec] * n_ex, [mn_spec] * n_out,
        [jax.ShapeDtypeStruct((M, N), dt) for dt in out_dtypes], (a, b, *extras),
        scratch=[] if nk == 1 else [pltpu.VMEM((tm, tn), F32)], sem=("parallel", "parallel", "arbitrary"),
        comm=comm)
    return out[0] if n_out == 1 else out


def _rms_fwd(x, gain, out_dtype, name, width=None, col_block=0, residual=None, tb=256, comm=None):
    T = x.shape[0]
    W = x.shape[1] if width is None else width
    tb = _tile(T, tb)
    has_res = residual is not None

    def body(*refs):
        x_ref, g_ref = refs[:2]
        o_ref = refs[-1]
        xf = x_ref[...]
        y = xf * lax.rsqrt(jnp.mean(xf * xf, axis=-1, keepdims=True) + NORM_EPS) * g_ref[...]
        if has_res:
            y = refs[2][...] + y
        o_ref[...] = y.astype(o_ref.dtype)

    row = pl.BlockSpec((tb, W), lambda i: (i, 0))
    ins = [x, gain] + ([residual] if has_res else [])
    return _call(
        body, name, (T // tb,),
        [pl.BlockSpec((tb, W), lambda i: (i, col_block)),
         pl.BlockSpec((1, W), lambda i: (0, 0))] + ([row] if has_res else []),
        [row], [jax.ShapeDtypeStruct((T, W), out_dtype)], ins, sem=("parallel",), comm=comm)[0]


def _rms_bwd(dy, x, gain, out_dtype, name, width=None, col_block=0, residual=None, tb=256, comm=None):
    T = dy.shape[0]
    W = x.shape[1] if width is None else width
    tb = _tile(T, tb)
    has_res = residual is not None

    def body(*refs):
        dy_ref, x_ref, g_ref = refs[:3]
        dx_ref, dg_ref = refs[-2:]
        i = pl.program_id(0)
        xf = x_ref[...]
        r = lax.rsqrt(jnp.mean(xf * xf, axis=-1, keepdims=True) + NORM_EPS)
        xn = xf * r
        dyf = dy_ref[...].astype(F32)
        dyg = dyf * g_ref[...]
        dx = r * (dyg - xn * jnp.mean(dyg * xn, axis=-1, keepdims=True))
        if has_res:
            dx = refs[3][...] + dx
        dx_ref[...] = dx.astype(dx_ref.dtype)

        @pl.when(i == 0)
        def _():
            dg_ref[...] = jnp.zeros_like(dg_ref)

        dg_ref[...] += jnp.sum(dyf * xn, axis=0, keepdims=True)

    row = pl.BlockSpec((tb, W), lambda i: (i, 0))
    vec = pl.BlockSpec((1, W), lambda i: (0, 0))
    ins = [dy, x, gain] + ([residual] if has_res else [])
    return _call(
        body, name, (T // tb,),
        [row, pl.BlockSpec((tb, W), lambda i: (i, col_block)), vec] + ([row] if has_res else []), [row, vec],
        [jax.ShapeDtypeStruct((T, W), out_dtype), jax.ShapeDtypeStruct((1, W), F32)], ins,
        sem=("arbitrary",), comm=comm)


def _loss_head(x2, y2, gain, target, name, tb=256):
    T, D = x2.shape
    tb = _tile(T, tb)

    def body(x2_ref, y2_ref, g_ref, t_ref, dx3_ref, loss_ref):
        i = pl.program_id(0)
        yf = y2_ref[...]
        x3 = x2_ref[...] + yf * lax.rsqrt(jnp.mean(yf * yf, axis=-1, keepdims=True) + NORM_EPS) * g_ref[...]
        e = x3 - t_ref[...]
        dx3_ref[...] = e * (1.0 / D)

        @pl.when(i == 0)
        def _():
            loss_ref[...] = jnp.zeros_like(loss_ref)

        per_row = jnp.mean(e * e, axis=-1, keepdims=True)
        loss_ref[...] += 0.5 * jnp.sum(per_row, axis=0, keepdims=True)

    row = pl.BlockSpec((tb, D), lambda i: (i, 0))
    return pl.pallas_call(
        body, name=name, grid=(T // tb,),
        in_specs=[row, row, pl.BlockSpec((1, D), lambda i: (0, 0)), row],
        out_specs=[row, pl.BlockSpec((1, 1), lambda i: (0, 0))],
        out_shape=[jax.ShapeDtypeStruct((T, D), F32), jax.ShapeDtypeStruct((1, 1), F32)],
        compiler_params=_params("arbitrary"),
    )(x2, y2, gain, target)


def _rope_tables(positions, rot_dim):
    half = rot_dim // 2
    inv_freq = ROPE_THETA ** (-jnp.arange(0, rot_dim, 2, dtype=F32) / rot_dim)
    ang = positions.astype(F32)[:, None] * inv_freq[None, :]
    cos, sin = jnp.cos(ang), jnp.sin(ang)
    T = positions.shape[0]
    ones = jnp.ones((T, HD - rot_dim), F32)
    cos_t = jnp.concatenate([cos, cos, ones], axis=1)
    sin_t = jnp.concatenate([-sin, sin, jnp.zeros_like(ones)], axis=1)
    return cos_t, sin_t


def _rotate(x, cos_t, sin_t, half):
    lane = lax.broadcasted_iota(jnp.int32, x.shape, 1)
    swapped = jnp.where(lane < half, pltpu.roll(x, HD - half, 1), pltpu.roll(x, half, 1))
    return x * cos_t + swapped * sin_t


def _rope_apply(x, cos_t, sin_t, half, n_blocks, is_rope, out_dtype, name, window=0, tb=256):
    T = x.shape[0]
    tb = _tile(T, tb)
    W = n_blocks * HD

    def body(x_ref, c_ref, s_ref, o_ref):
        for j in range(n_blocks):
            sl = slice(j * HD, (j + 1) * HD)
            xj = x_ref[:, sl]
            if is_rope(j):
                xj = _rotate(xj.astype(F32), c_ref[...], s_ref[...], half)
            o_ref[:, sl] = xj.astype(o_ref.dtype)

    tab = pl.BlockSpec((tb, HD), lambda i: (i, 0))
    return pl.pallas_call(
        body, name=name, grid=(T // tb,),
        in_specs=[pl.BlockSpec((tb, W), lambda i: (i, window)), tab, tab],
        out_specs=pl.BlockSpec((tb, W), lambda i: (i, 0)),
        out_shape=jax.ShapeDtypeStruct((T, W), out_dtype),
        compiler_params=_params("parallel"),
    )(x, cos_t, sin_t)


DSWA_TB = 2048


def _deinterleave(src, dst_ref, d, dtype):
    rows = src.shape[0] // d
    for r in range(d):
        dst_ref[r] = src[pl.ds(r, rows, stride=d), :].astype(dtype)


def _rope_dswa(proj, cos_t, sin_t, name):
    T = proj.shape[0]
    tb = _tile(T, DSWA_TB)
    half = ROT_A // 2

    def body(x_ref, c_ref, s_ref, *rest):
        outs, scr = rest[:-1], rest[-1]
        j = pl.program_id(1)

        @pl.when(j < 2 * HEADS)
        def _():
            scr[...] = _rotate(x_ref[...], c_ref[...], s_ref[...], half)

        @pl.when(j >= 2 * HEADS)
        def _():
            scr[...] = x_ref[...]

        for o_ref, d in zip(outs, DSWA_DILATIONS):
            _deinterleave(scr, o_ref, d, BF16)

    blk = pl.BlockSpec((tb, HD), lambda i, j: (i, j))
    tab = pl.BlockSpec((tb, HD), lambda i, j: (i, 0))
    return pl.pallas_call(
        body, name=name, grid=(T // tb, 3 * HEADS),
        in_specs=[blk, tab, tab],
        out_specs=[pl.BlockSpec((d, tb // d, HD), lambda i, j: (0, i, j)) for d in DSWA_DILATIONS],
        out_shape=[jax.ShapeDtypeStruct((d, T // d, 3 * AW), BF16) for d in DSWA_DILATIONS],
        scratch_shapes=[pltpu.VMEM((tb, HD), F32)],
        compiler_params=_params("parallel", "parallel"),
    )(proj, cos_t, sin_t)


def _shared_key_grad(dk, cos_t, sin_t_neg, half, name, tb=512):
    T = dk.shape[0]
    tb = _tile(T, tb)

    def body(d_ref, c_ref, s_ref, o_ref):
        tot = d_ref[:, HD:2 * HD]
        for h in range(1, HEADS):
            tot = tot + d_ref[:, h * QK + HD:(h + 1) * QK]
        o_ref[...] = _rotate(tot, c_ref[...], s_ref[...], half).astype(o_ref.dtype)

    tab = pl.BlockSpec((tb, HD), lambda i: (i, 0))
    return pl.pallas_call(
        body, name=name, grid=(T // tb,),
        in_specs=[pl.BlockSpec((tb, HEADS * QK), lambda i: (i, 0)), tab, tab],
        out_specs=tab, out_shape=jax.ShapeDtypeStruct((T, HD), BF16),
        compiler_params=_params("parallel"),
    )(dk, cos_t, sin_t_neg)


def _band_masks(n):
    row = lax.broadcasted_iota(jnp.int32, (QBLK, QBLK), 0)
    col = lax.broadcasted_iota(jnp.int32, (QBLK, QBLK), 1)
    return col <= row, jnp.logical_and(col >= row, n > 0)


def _dswa_specs(nb, reverse=False):
    pos = (lambda n: nb - 1 - n) if reverse else (lambda n: n)
    cur = lambda c: pl.BlockSpec((None, QBLK, AW), lambda r, n: (r, pos(n), c))
    prev = lambda c: pl.BlockSpec((None, QBLK, AW), lambda r, n: (r, jnp.maximum(pos(n) - 1, 0), c))
    return cur, prev


def _relayout_spec(d, tb):
    return pl.BlockSpec((d, tb // d, HD), lambda i, h: (0, i, h))


def _dswa_fwd(qkv, name, comm=None):
    d, sd = qkv.shape[:2]
    nb = sd // QBLK

    def body(q_ref, kc_ref, kp_ref, vc_ref, vp_ref, o_ref, l_ref):
        cur_mask, prev_mask = _band_masks(pl.program_id(1))
        for h in range(HEADS):
            sl = slice(h * HD, (h + 1) * HD)
            qh = q_ref[:, sl]
            sc = jnp.where(cur_mask, _dot(qh, kc_ref[:, sl], NT) * SCALE_A, NEG_INF)
            sp = jnp.where(prev_mask, _dot(qh, kp_ref[:, sl], NT) * SCALE_A, NEG_INF)
            m = jnp.maximum(jnp.max(sc, axis=-1, keepdims=True), jnp.max(sp, axis=-1, keepdims=True))
            pc = jnp.exp(sc - m)
            pp = jnp.exp(sp - m)
            den = jnp.sum(pc, axis=-1, keepdims=True) + jnp.sum(pp, axis=-1, keepdims=True)
            o = (_dot((pc / den).astype(BF16), vc_ref[:, sl], NN)
                 + _dot((pp / den).astype(BF16), vp_ref[:, sl], NN))
            o_ref[:, sl] = o
            l_ref[:, sl] = jnp.broadcast_to(m + jnp.log(den), (QBLK, HD))

    cur, prev = _dswa_specs(nb)
    return _call(
        body, name, (d, nb), [cur(0), cur(1), prev(1), cur(2), prev(2)], [cur(0), cur(0)],
        [jax.ShapeDtypeStruct((d, sd, AW), F32)] * 2, (qkv, qkv, qkv, qkv, qkv),
        sem=("parallel", "parallel"), comm=comm)


def _dswa_merge(outs, lses, name):
    nc = len(DSWA_DILATIONS)
    T = outs[0].shape[0] * outs[0].shape[1]
    tb = _tile(T, DSWA_TB)

    def body(*refs):
        o_refs, l_refs = refs[:nc], refs[nc:2 * nc]
        out_ref, outb_ref = refs[2 * nc:2 * nc + 2]
        lt_refs = refs[2 * nc + 2:3 * nc + 2]
        o_nat, l_nat, lt_nat = refs[3 * nc + 2:4 * nc + 2], refs[4 * nc + 2:5 * nc + 2], refs[-1]
        for c, d in enumerate(DSWA_DILATIONS):
            for r in range(d):
                o_nat[c][pl.ds(r, tb // d, stride=d), :] = o_refs[c][r]
                l_nat[c][pl.ds(r, tb // d, stride=d), :] = l_refs[c][r]
        ls = [l[...] for l in l_nat]
        m = functools.reduce(jnp.maximum, ls)
        es = [jnp.exp(l - m) for l in ls]
        tot = functools.reduce(lambda a, b: a + b, es)
        acc = (es[0] / tot) * o_nat[0][...]
        for c in range(1, nc):
            acc = acc + (es[c] / tot) * o_nat[c][...]
        out_ref[...] = acc
        outb_ref[...] = acc.astype(BF16)
        lt_nat[...] = m + jnp.log(tot)
        for c, d in enumerate(DSWA_DILATIONS):
            _deinterleave(lt_nat, lt_refs[c], d, F32)

    nat = pl.BlockSpec((tb, HD), lambda i, h: (i, h))
    by_d = [_relayout_spec(d, tb) for d in DSWA_DILATIONS]
    res = pl.pallas_call(
        body, name=name, grid=(T // tb, HEADS),
        in_specs=by_d + by_d, out_specs=[nat, nat] + by_d,
        out_shape=[jax.ShapeDtypeStruct((T, AW), F32), jax.ShapeDtypeStruct((T, AW), BF16)]
        + [jax.ShapeDtypeStruct((d, T // d, AW), F32) for d in DSWA_DILATIONS],
        scratch_shapes=[pltpu.VMEM((tb, HD), F32)] * (2 * nc + 1),
        compiler_params=_params("parallel", "parallel"),
    )(*outs, *lses)
    return res[0], res[1], res[2:]


def _dswa_delta(dout, out, name):
    nc = len(DSWA_DILATIONS)
    T = out.shape[0]
    tb = _tile(T, DSWA_TB)

    def body(do_ref, o_ref, *rest):
        dl_refs, dob_refs, dl_nat = rest[:nc], rest[nc:2 * nc], rest[-1]
        dl_nat[...] = jnp.broadcast_to(jnp.sum(do_ref[...] * o_ref[...], axis=-1, keepdims=True), (tb, HD))
        for c, d in enumerate(DSWA_DILATIONS):
            _deinterleave(dl_nat, dl_refs[c], d, F32)
            _deinterleave(do_ref, dob_refs[c], d, BF16)

    nat = pl.BlockSpec((tb, HD), lambda i, h: (i, h))
    by_d = [_relayout_spec(d, tb) for d in DSWA_DILATIONS]
    res = pl.pallas_call(
        body, name=name, grid=(T // tb, HEADS),
        in_specs=[nat, nat], out_specs=by_d + by_d,
        out_shape=[jax.ShapeDtypeStruct((d, T // d, AW), F32) for d in DSWA_DILATIONS]
        + [jax.ShapeDtypeStruct((d, T // d, AW), BF16) for d in DSWA_DILATIONS],
        scratch_shapes=[pltpu.VMEM((tb, HD), F32)],
        compiler_params=_params("parallel", "parallel"),
    )(dout, out)
    return res[:nc], res[nc:]


def _delta_prep(dout, col_block, out, name, tb=256):
    T = out.shape[0]
    tb = _tile(T, tb)

    def body(do_ref, o_ref, delta_ref, dob_ref):
        for h in range(HEADS):
            sl = slice(h * HD, (h + 1) * HD)
            doh = do_ref[:, sl]
            delta_ref[:, sl] = jnp.broadcast_to(jnp.sum(doh * o_ref[:, sl], axis=-1, keepdims=True), (tb, HD))
            dob_ref[:, sl] = doh.astype(BF16)

    row = pl.BlockSpec((tb, AW), lambda i: (i, 0))
    return pl.pallas_call(
        body, name=name, grid=(T // tb,),
        in_specs=[pl.BlockSpec((tb, AW), lambda i: (i, col_block)), row], out_specs=[row, row],
        out_shape=[jax.ShapeDtypeStruct((T, AW), F32), jax.ShapeDtypeStruct((T, AW), BF16)],
        compiler_params=_params("parallel"),
    )(dout, out)


def _dswa_bwd(qkv, dout_b, lse_tot, delta, name, comm=None):
    d, sd = qkv.shape[:2]
    nb = sd // QBLK

    def body(q_ref, kc_ref, kp_ref, vc_ref, vp_ref, do_ref, l_ref, dl_ref, dq_ref, dk_ref, dv_ref, carry_k, carry_v):
        cur_mask, prev_mask = _band_masks(nb - 1 - pl.program_id(1))

        @pl.when(pl.program_id(1) == 0)
        def _():
            carry_k[...] = jnp.zeros_like(carry_k)
            carry_v[...] = jnp.zeros_like(carry_v)

        for h in range(HEADS):
            sl = slice(h * HD, (h + 1) * HD)
            qh, kc, kp, vc, vp, doh = q_ref[:, sl], kc_ref[:, sl], kp_ref[:, sl], vc_ref[:, sl], vp_ref[:, sl], do_ref[:, sl]
            lse = l_ref[:, sl][:, 0:1]
            dlt = dl_ref[:, sl][:, 0:1]
            sc = jnp.where(cur_mask, _dot(qh, kc, NT) * SCALE_A, NEG_INF)
            sp = jnp.where(prev_mask, _dot(qh, kp, NT) * SCALE_A, NEG_INF)
            pc = jnp.exp(sc - lse)
            pp = jnp.exp(sp - lse)
            dsc = (pc * (_dot(doh, vc, NT) - dlt) * SCALE_A).astype(BF16)
            dsp = (pp * (_dot(doh, vp, NT) - dlt) * SCALE_A).astype(BF16)
            dq_ref[:, sl] = _dot(dsc, kc, NN) + _dot(dsp, kp, NN)
            dk_ref[:, sl] = _dot(dsc, qh, TN) + carry_k[:, sl]
            dv_ref[:, sl] = _dot(pc.astype(BF16), doh, TN) + carry_v[:, sl]
            carry_k[:, sl] = _dot(dsp, qh, TN)
            carry_v[:, sl] = _dot(pp.astype(BF16), doh, TN)

    cur, prev = _dswa_specs(nb, reverse=True)
    return _call(
        body, name, (d, nb), [cur(0), cur(1), prev(1), cur(2), prev(2), cur(0), cur(0), cur(0)], [cur(0)] * 3,
        [jax.ShapeDtypeStruct((d, sd, AW), F32)] * 3, (qkv, qkv, qkv, qkv, qkv, dout_b, lse_tot, delta),
        scratch=[pltpu.VMEM((QBLK, AW), F32)] * 2, sem=("parallel", "arbitrary"), comm=comm)


def _dswa_combine(grads, cos_t, sin_t_neg, name):
    nc = len(DSWA_DILATIONS)
    T = grads[0][0].shape[0] * grads[0][0].shape[1]
    tb = _tile(T, DSWA_TB)
    half = ROT_A // 2

    def body(*refs):
        g_refs = refs[:3 * nc]
        c_ref, s_ref = refs[3 * nc:3 * nc + 2]
        outs, accs = refs[3 * nc + 2:3 * nc + 5], refs[3 * nc + 5:]
        for which in range(3):
            acc = accs[which]
            for c, d in enumerate(DSWA_DILATIONS):
                g = g_refs[3 * c + which]
                for r in range(d):
                    if c == 0:
                        acc[...] = g[r]
                    else:
                        acc[pl.ds(r, tb // d, stride=d), :] += g[r]
            val = acc[...]
            if which < 2:
                val = _rotate(val, c_ref[...], s_ref[...], half)
            outs[which][...] = val.astype(BF16)

    nat = pl.BlockSpec((tb, HD), lambda i, h: (i, h))
    tab = pl.BlockSpec((tb, HD), lambda i, h: (i, 0))
    in_specs, ins = [], []
    for d, g in zip(DSWA_DILATIONS, grads):
        in_specs += [_relayout_spec(d, tb)] * 3
        ins += list(g)
    return pl.pallas_call(
        body, name=name, grid=(T // tb, HEADS),
        in_specs=in_specs + [tab, tab], out_specs=[nat] * 3,
        out_shape=[jax.ShapeDtypeStruct((T, AW), BF16)] * 3,
        scratch_shapes=[pltpu.VMEM((tb, HD), F32)] * 3,
        compiler_params=_params("parallel", "parallel"),
    )(*ins, cos_t, sin_t_neg)


MLA_TQ = 512
QK = 2 * HD


def _triangle(nq, key_major):
    pairs = [(q, k) for q in range(nq) for k in range(q + 1)]
    if key_major:
        pairs.sort(key=lambda p: (p[1], p[0]))
    return (jnp.array([p[0] for p in pairs], jnp.int32), jnp.array([p[1] for p in pairs], jnp.int32))


def _mla_specs(tq):
    q_spec = pl.BlockSpec((tq, HEADS * QK), lambda t, qi, ki: (qi[t], 0))
    k_spec = pl.BlockSpec((tq, HEADS * QK), lambda t, qi, ki: (ki[t], 0))
    v_spec = pl.BlockSpec((tq, AW), lambda t, qi, ki: (ki[t], 1))
    qrow = pl.BlockSpec((tq, AW), lambda t, qi, ki: (qi[t], 0))
    krow = pl.BlockSpec((tq, AW), lambda t, qi, ki: (ki[t], 0))
    return q_spec, k_spec, v_spec, qrow, krow


def _mla_scores(q_ref, k_ref, h, qi, ki, tq):
    s = _dot(q_ref[:, h * QK:(h + 1) * QK], k_ref[:, h * QK:(h + 1) * QK], NT) * SCALE_B
    row = lax.broadcasted_iota(jnp.int32, s.shape, 0) + qi * tq
    col = lax.broadcasted_iota(jnp.int32, s.shape, 1) + ki * tq
    return jnp.where(col <= row, s, NEG_INF)


def _mla_fwd(q, k, kv, name, comm=None):
    T = q.shape[0]
    tq = _tile(T, MLA_TQ)
    tables = _triangle(T // tq, False)

    def body(qi_ref, ki_ref, q_ref, k_ref, v_ref, o_ref, l_ref, m_s, l_s, acc):
        t = pl.program_id(0)
        qi, ki = qi_ref[t], ki_ref[t]

        @pl.when(ki == 0)
        def _():
            m_s[...] = jnp.full_like(m_s, NEG_INF)
            l_s[...] = jnp.zeros_like(l_s)
            acc[...] = jnp.zeros_like(acc)

        for h in range(HEADS):
            sl = slice(h * HD, (h + 1) * HD)
            s = _mla_scores(q_ref, k_ref, h, qi, ki, tq)
            m_new = jnp.maximum(m_s[h], jnp.max(s, axis=-1, keepdims=True))
            alpha = jnp.exp(m_s[h] - m_new)
            p = jnp.exp(s - m_new)
            l_s[h] = alpha * l_s[h] + jnp.sum(p, axis=-1, keepdims=True)
            acc[:, sl] = alpha * acc[:, sl] + _dot(p.astype(BF16), v_ref[:, sl], NN)
            m_s[h] = m_new

        @pl.when(ki == qi)
        def _():
            for h in range(HEADS):
                sl = slice(h * HD, (h + 1) * HD)
                o_ref[:, sl] = acc[:, sl] / l_s[h]
                l_ref[:, sl] = jnp.broadcast_to(m_s[h] + jnp.log(l_s[h]), (tq, HD))

    q_spec, k_spec, v_spec, qrow, _ = _mla_specs(tq)
    return _call(
        body, name, (tables[0].shape[0],), [q_spec, k_spec, v_spec], [qrow, qrow],
        [jax.ShapeDtypeStruct((T, AW), F32)] * 2, (q, k, kv),
        scratch=[pltpu.VMEM((HEADS, tq, 1), F32), pltpu.VMEM((HEADS, tq, 1), F32), pltpu.VMEM((tq, AW), F32)],
        sem=("arbitrary",), comm=comm, prefetch=tables)


def _mla_ds(q_ref, k_ref, v_ref, do_ref, l_ref, dl_ref, h, qi, ki, tq):
    sl = slice(h * HD, (h + 1) * HD)
    p = jnp.exp(_mla_scores(q_ref, k_ref, h, qi, ki, tq) - l_ref[:, sl][:, 0:1])
    ds = (p * (_dot(do_ref[:, sl], v_ref[:, sl], NT) - dl_ref[:, sl][:, 0:1]) * SCALE_B).astype(BF16)
    return p, ds


def _mla_bwd_q(q, k, kv, dout_b, lse, delta, name, comm=None):
    T = q.shape[0]
    tq = _tile(T, MLA_TQ)
    tables = _triangle(T // tq, False)

    def body(qi_ref, ki_ref, q_ref, k_ref, v_ref, do_ref, l_ref, dl_ref, dq_ref):
        t = pl.program_id(0)
        qi, ki = qi_ref[t], ki_ref[t]

        @pl.when(ki == 0)
        def _():
            dq_ref[...] = jnp.zeros_like(dq_ref)

        for h in range(HEADS):
            _, ds = _mla_ds(q_ref, k_ref, v_ref, do_ref, l_ref, dl_ref, h, qi, ki, tq)
            dq_ref[:, h * QK:(h + 1) * QK] += _dot(ds, k_ref[:, h * QK:(h + 1) * QK], NN)

    q_spec, k_spec, v_spec, qrow, _ = _mla_specs(tq)
    return _call(
        body, name, (tables[0].shape[0],), [q_spec, k_spec, v_spec, qrow, qrow, qrow], [q_spec],
        [jax.ShapeDtypeStruct((T, HEADS * QK), F32)], (q, k, kv, dout_b, lse, delta),
        sem=("arbitrary",), comm=comm, prefetch=tables)[0]


def _mla_bwd_kv(q, k, kv, dout_b, lse, delta, name, comm=None):
    T = q.shape[0]
    tq = _tile(T, MLA_TQ)
    tables = _triangle(T // tq, True)

    def body(qi_ref, ki_ref, q_ref, k_ref, v_ref, do_ref, l_ref, dl_ref, dk_ref, dv_ref):
        t = pl.program_id(0)
        qi, ki = qi_ref[t], ki_ref[t]

        @pl.when(qi == ki)
        def _():
            dk_ref[...] = jnp.zeros_like(dk_ref)
            dv_ref[...] = jnp.zeros_like(dv_ref)

        for h in range(HEADS):
            sl = slice(h * HD, (h + 1) * HD)
            p, ds = _mla_ds(q_ref, k_ref, v_ref, do_ref, l_ref, dl_ref, h, qi, ki, tq)
            dv_ref[:, sl] += _dot(p.astype(BF16), do_ref[:, sl], TN)
            dk_ref[:, h * QK:(h + 1) * QK] += _dot(ds, q_ref[:, h * QK:(h + 1) * QK], TN)

    q_spec, k_spec, v_spec, qrow, krow = _mla_specs(tq)
    return _call(
        body, name, (tables[0].shape[0],), [q_spec, k_spec, v_spec, qrow, qrow, qrow], [k_spec, krow],
        [jax.ShapeDtypeStruct((T, HEADS * QK), F32), jax.ShapeDtypeStruct((T, AW), F32)],
        (q, k, kv, dout_b, lse, delta), sem=("arbitrary",), comm=comm, prefetch=tables)


def _pair_sum(by_core, from_sibling, name, tb=256):
    _, n_chip, R, C = by_core.shape
    tb = _tile(R, tb)
    core = jnp.reshape(lax.axis_index("c"), (1,)).astype(jnp.int32)

    def body(core_ref, mine_ref, theirs_ref, o_ref):
        o_ref[...] = (mine_ref[...].astype(F32) + theirs_ref[...].astype(F32)).astype(o_ref.dtype)

    blk = pl.BlockSpec((None, tb, C), lambda p, i, core_ref: (p, i, 0))
    return _call(
        body, name, (n_chip, R // tb),
        [pl.BlockSpec((None, None, tb, C), lambda p, i, core_ref: (core_ref[0], p, i, 0)), blk], [blk],
        [jax.ShapeDtypeStruct((n_chip, R, C), BF16)], (by_core, from_sibling),
        sem=("parallel", "parallel"), prefetch=(core,))[0]


def _adamw(parts, w, m, v, name, tb=128):
    R, C = w.shape
    n_parts = parts.shape[0]
    tb = _tile(R, tb)
    c1 = 1.0 - ADAM_B1
    c2 = 1.0 - ADAM_B2
    bc1 = 1.0 - ADAM_B1 ** ADAM_STEP
    bc2 = 1.0 - ADAM_B2 ** ADAM_STEP

    def body(p_ref, w_ref, m_ref, v_ref, g_ref, d_ref, nm_ref, nv_ref):
        g = p_ref[0].astype(F32)
        for j in range(1, n_parts):
            g = g + p_ref[j].astype(F32)
        nm = ADAM_B1 * m_ref[...] + c1 * g
        nv = ADAM_B2 * v_ref[...] + c2 * (g * g)
        g_ref[...] = g
        nm_ref[...] = nm
        nv_ref[...] = nv
        d_ref[...] = -ADAM_LR * ((nm / bc1) / (jnp.sqrt(nv / bc2) + ADAM_EPS) + ADAM_WD * w_ref[...])

    row = pl.BlockSpec((tb, C), lambda i: (i, 0))
    return pl.pallas_call(
        body, name=name, grid=(R // tb,),
        in_specs=[pl.BlockSpec((n_parts, tb, C), lambda i: (0, i, 0)), row, row, row],
        out_specs=[row] * 4, out_shape=[jax.ShapeDtypeStruct((R, C), F32)] * 4,
        compiler_params=_params("parallel"),
    )(parts, w, m, v)


def _cols_from_shards(g):
    return jnp.transpose(g, (1, 0, 2)).reshape(g.shape[1], N_DEV * g.shape[2])


def _cols_to_shards(w):
    return jnp.transpose(w.reshape(w.shape[0], N_DEV, w.shape[1] // N_DEV), (1, 0, 2))


def _split_heads(w, first):
    w3 = w.reshape(w.shape[0], HEADS, -1)
    return w3[:, :, :first].reshape(w.shape[0], -1), w3[:, :, first:].reshape(w.shape[0], -1)


def _join_heads(a, b):
    R = a.shape[0]
    return jnp.concatenate([a.reshape(R, HEADS, -1), b.reshape(R, HEADS, -1)], axis=2).reshape(R, -1)


def _pad_heads(w, width):
    w3 = w.reshape(w.shape[0], HEADS, -1)
    return jnp.pad(w3, ((0, 0), (0, 0), (0, width - w3.shape[2]))).reshape(w.shape[0], HEADS * width)


def _unpad_heads(w, k):
    return w.reshape(w.shape[0], HEADS, -1)[:, :, :k].reshape(w.shape[0], HEADS * k)


def kernel(x, positions, norm_attn_pre, norm_attn_post, w_in, q_latent_norm, kv_latent_norm, w_uq, w_ukv, w_out, norm_mlp_pre, norm_mlp_post, w_up, w_down, loss_target, m_norm_attn_pre, m_norm_attn_post, m_w_in, m_q_latent_norm, m_kv_latent_norm, m_w_uq, m_w_ukv, m_w_out, m_norm_mlp_pre, m_norm_mlp_post, m_w_up, m_w_down, v_norm_attn_pre, v_norm_attn_post, v_w_in, v_q_latent_norm, v_kv_latent_norm, v_w_uq, v_w_ukv, v_w_out, v_norm_mlp_pre, v_norm_mlp_post, v_w_up, v_w_down):
    xs = x[0]
    tgt = loss_target[0]
    pos = positions[0]
    T, D = xs.shape
    big = dict(w_in=(w_in, m_w_in, v_w_in), w_uq=(w_uq, m_w_uq, v_w_uq), w_ukv=(w_ukv, m_w_ukv, v_w_ukv),
               w_out=(w_out, m_w_out, v_w_out), w_up=(w_up, m_w_up, v_w_up), w_down=(w_down, m_w_down, v_w_down))
    big = {n: tuple(t[0] for t in ts) for n, ts in big.items()}
    big_names = ["w_in", "w_uq", "w_ukv", "w_out", "w_up", "w_down"]
    col_sharded = {"w_in", "w_uq", "w_ukv", "w_up"}

    wb = {n: big[n][0].astype(BF16) for n in big_names}

    def gathered(ex, i, n):
        g = ex.results[i]
        return _cols_from_shards(g) if n in col_sharded else g.reshape(-1, g.shape[2])

    def scatter_of(g, n):
        return _Exchange([_cols_to_shards(g) if n in col_sharded
                          else g.reshape(N_DEV, g.shape[0] // N_DEV, g.shape[1])], "scatter")

    def by_core(g, n):
        if n in col_sharded:
            return jnp.transpose(g.reshape(g.shape[0], 4, 2, g.shape[1] // N_DEV), (2, 1, 0, 3))
        return jnp.transpose(g.reshape(4, 2, g.shape[0] // N_DEV, g.shape[1]), (1, 0, 2, 3))

    cos_a, sin_a = _rope_tables(pos, ROT_A)
    cos_b, sin_b = _rope_tables(pos, ROPE_MLA)

    ex_in = _Exchange([wb["w_in"]], "gather")
    h1 = _rms_fwd(xs, norm_attn_pre, BF16, "norm_attn_pre_fwd", comm=ex_in)
    Wi = jnp.pad(gathered(ex_in, 0, "w_in"), ((0, 0), (0, IN_PAD - IN_COLS)))
    ex_mid = _Exchange([wb["w_uq"], wb["w_ukv"], wb["w_out"]], "gather")
    proj = _mm(h1, Wi, "nn", [F32], "proj_in", tn=1408, comm=ex_mid)
    Wuq = _pad_heads(gathered(ex_mid, 0, "w_uq"), QK)
    Wukv = jnp.concatenate(_split_heads(gathered(ex_mid, 1, "w_ukv"), HD), axis=1)
    Wo = gathered(ex_mid, 2, "w_out")
    qkv_by_d = _rope_dswa(proj, cos_a, sin_a, "rope_dswa")
    n_quarter = wb["w_down"].shape[0] // 4
    ex_down = [_Exchange([wb["w_down"][i * n_quarter:(i + 1) * n_quarter]], "gather") for i in range(4)]
    outs, lses = [], []
    for d, qkv, ex in zip(DSWA_DILATIONS, qkv_by_d, ex_down):
        o, l = _dswa_fwd(qkv, f"dswa_fwd_d{d}", comm=ex)
        outs.append(o)
        lses.append(l)
    a_out, a_out_b, a_lse_by_d = _dswa_merge(outs, lses, "dswa_merge")

    cqn = _rms_fwd(proj, q_latent_norm, BF16, "q_latent_norm_fwd", width=Q_LORA, col_block=3 * AW // Q_LORA)
    ckvn = _rms_fwd(proj, kv_latent_norm, BF16, "kv_latent_norm_fwd", width=KV_LORA, col_block=3 * AW // KV_LORA + 1)
    qb = _mm(cqn, Wuq, "nn", [F32], "q_up")
    kvb = _mm(ckvn, Wukv, "nn", [BF16], "kv_up")
    odd = lambda j: j % 2 == 1
    q_mla = _rope_apply(qb, cos_b, sin_b, ROPE_MLA // 2, 2 * HEADS, odd, BF16, "rope_mla_q")
    kr = _rope_apply(proj, cos_b, sin_b, ROPE_MLA // 2, 1, lambda j: True, BF16, "rope_mla_k",
                     window=(IN_PAD - HD) // HD)
    k_mla = jnp.concatenate([kvb[:, :AW].reshape(T, HEADS, HD), jnp.broadcast_to(kr[:, None, :], (T, HEADS, HD))],
                            axis=2).reshape(T, HEADS * QK)
    ex_up = _Exchange([wb["w_up"]], "gather")
    b_out, b_lse = _mla_fwd(q_mla, k_mla, kvb, "mla_fwd", comm=ex_up)
    Wup = gathered(ex_up, 0, "w_up")

    mixed = jnp.concatenate([a_out_b, b_out.astype(BF16)], axis=1)
    y1 = _mm(mixed, Wo, "nn", [F32], "attn_out")
    x2 = _rms_fwd(y1, norm_attn_post, F32, "norm_attn_post_fwd", residual=xs)

    h2 = _rms_fwd(x2, norm_mlp_pre, BF16, "norm_mlp_pre_fwd")

    def relu2(z):
        r = jnp.maximum(z, 0.0)
        return r * r, r

    u, zr = _mm(h2, Wup, "nn", [BF16, BF16], "mlp_up", epilogue=relu2, comm=ex_down[3])
    Wdn = jnp.concatenate([ex.results[0] for ex in ex_down], axis=1).reshape(-1, D)
    y2 = _mm(u, Wdn, "nn", [F32], "mlp_down")
    dx3, loss_part = _loss_head(x2, y2, norm_mlp_post, tgt, "loss_head")

    dy2, dg_mlp_post = _rms_bwd(dx3, y2, norm_mlp_post, BF16, "norm_mlp_post_bwd")
    dz = _mm(dy2, Wdn, "nt", [BF16], "mlp_down_dx", epilogue=lambda du, r: (du * (2.0 * r.astype(F32)),), extras=(zr,))
    g_down = _mm(u, dy2, "tn", [BF16], "mlp_down_dw")
    down_core = by_core(g_down, "w_down")
    pair_down = _Exchange([down_core], "pair")
    g_up = _mm(h2, dz, "tn", [BF16], "mlp_up_dw", comm=pair_down)
    down_chip = _pair_sum(down_core, pair_down.results[0], "pair_sum_w_down")
    up_core = by_core(g_up, "w_up")
    pair_up = _Exchange([up_core], "pair")
    dh2 = _mm(dz, Wup, "nt", [F32], "mlp_up_dx", comm=pair_up)
    up_chip = _pair_sum(up_core, pair_up.results[0], "pair_sum_w_up")
    dx2, dg_mlp_pre = _rms_bwd(dh2, x2, norm_mlp_pre, F32, "norm_mlp_pre_bwd", residual=dx3)

    dy1, dg_attn_post = _rms_bwd(dx2, y1, norm_attn_post, BF16, "norm_attn_post_bwd")
    dmixed = _mm(dy1, Wo, "nt", [F32], "attn_out_dx")
    g_out = _mm(mixed, dy1, "tn", [BF16], "attn_out_dw")

    b_delta, b_dout = _delta_prep(dmixed, 1, b_out, "mla_delta")
    sc_down = _Exchange([down_chip], "chips")
    dq_mla = _mla_bwd_q(q_mla, k_mla, kvb, b_dout, b_lse, b_delta, "mla_bwd_q", comm=sc_down)
    sc_up = _Exchange([up_chip], "chips")
    dk_mla, dvb = _mla_bwd_kv(q_mla, k_mla, kvb, b_dout, b_lse, b_delta, "mla_bwd_kv", comm=sc_up)
    dqb = _rope_apply(dq_mla, cos_b, -sin_b, ROPE_MLA // 2, 2 * HEADS, odd, BF16, "rope_mla_q_bwd")
    dkn = dk_mla.reshape(T, HEADS, QK)[:, :, :HD].reshape(T, AW)
    dkvb = jnp.concatenate([dkn, dvb], axis=1).astype(BF16)
    d_kr = _shared_key_grad(dk_mla, cos_b, -sin_b, ROPE_MLA // 2, "rope_mla_k_bwd")
    g_uq_pad = _mm(cqn, dqb, "tn", [BF16], "q_up_dw")
    g_ukv_perm = _mm(ckvn, dkvb, "tn", [BF16], "kv_up_dw")
    dcqn = _mm(dqb, Wuq, "nt", [F32], "q_up_dx")
    dckvn = _mm(dkvb, Wukv, "nt", [F32], "kv_up_dx")
    d_cq, dg_q = _rms_bwd(dcqn, proj, q_latent_norm, BF16, "q_latent_norm_bwd", width=Q_LORA, col_block=3 * AW // Q_LORA)
    d_ckv, dg_kv = _rms_bwd(dckvn, proj, kv_latent_norm, BF16, "kv_latent_norm_bwd", width=KV_LORA,
                            col_block=3 * AW // KV_LORA + 1)

    g_uq = _unpad_heads(g_uq_pad, HD + ROPE_MLA)
    g_ukv = _join_heads(g_ukv_perm[:, :AW], g_ukv_perm[:, AW:])
    sc_out = scatter_of(g_out, "w_out")
    sc_uq = _Exchange([_cols_to_shards(g_uq), _cols_to_shards(g_ukv)], "scatter")
    a_delta_by_d, a_dout_by_d = _dswa_delta(dmixed, a_out, "dswa_delta")
    a_grads = [_dswa_bwd(qkv_by_d[c], a_dout_by_d[c], a_lse_by_d[c], a_delta_by_d[c], f"dswa_bwd_d{d}", comm=cm)
               for c, (d, cm) in enumerate(zip(DSWA_DILATIONS, (sc_out, sc_uq, None)))]
    d_aq, d_ak, d_av = _dswa_combine(a_grads, cos_a, -sin_a, "dswa_combine")

    dproj = jnp.concatenate([d_aq, d_ak, d_av, d_cq, d_ckv, d_kr], axis=1)
    g_in_pad = _mm(h1, dproj, "tn", [BF16], "proj_in_dw", tn=1408)
    in_core = by_core(g_in_pad[:, :IN_COLS], "w_in")
    pair_in = _Exchange([in_core], "pair")
    dh1 = _mm(dproj, Wi, "nt", [F32], "proj_in_dx", comm=pair_in)
    sc_in = _Exchange([_pair_sum(in_core, pair_in.results[0], "pair_sum_w_in")], "chips")
    grad_x, dg_attn_pre = _rms_bwd(dh1, xs, norm_attn_pre, F32, "norm_attn_pre_bwd", residual=dx2, comm=sc_in)

    parts = dict(w_in=sc_in.results[0], w_uq=sc_uq.results[0], w_ukv=sc_uq.results[1], w_out=sc_out.results[0],
                 w_up=sc_up.results[0], w_down=sc_down.results[0])
    big_out = {n: _adamw(parts[n], *big[n], f"adamw_{n}") for n in big_names}

    gain_names = ["norm_attn_pre", "norm_attn_post", "q_latent_norm", "kv_latent_norm", "norm_mlp_pre", "norm_mlp_post"]
    gain_args = dict(norm_attn_pre=(norm_attn_pre, m_norm_attn_pre, v_norm_attn_pre),
                     norm_attn_post=(norm_attn_post, m_norm_attn_post, v_norm_attn_post),
                     q_latent_norm=(q_latent_norm, m_q_latent_norm, v_q_latent_norm),
                     kv_latent_norm=(kv_latent_norm, m_kv_latent_norm, v_kv_latent_norm),
                     norm_mlp_pre=(norm_mlp_pre, m_norm_mlp_pre, v_norm_mlp_pre),
                     norm_mlp_post=(norm_mlp_post, m_norm_mlp_post, v_norm_mlp_post))
    gain_grads = dict(norm_attn_pre=dg_attn_pre, norm_attn_post=dg_attn_post, q_latent_norm=dg_q,
                      kv_latent_norm=dg_kv, norm_mlp_pre=dg_mlp_pre, norm_mlp_post=dg_mlp_post)
    packed = jnp.concatenate([gain_grads[n] for n in gain_names], axis=1)
    gain_parts = _Exchange([packed], "gather").standalone("gather_gain_grads")[0]
    pack3 = lambda i: jnp.concatenate([gain_args[n][i] for n in gain_names], axis=1)
    gain_out = _adamw(gain_parts, pack3(0), pack3(1), pack3(2), "adamw_gains", tb=1)
    offs = [0]
    for n in gain_names:
        offs.append(offs[-1] + gain_args[n][0].shape[1])
    small_out = {n: tuple(o[:, offs[i]:offs[i + 1]] for o in gain_out) for i, n in enumerate(gain_names)}

    loss = lax.psum(loss_part[0, 0], ("x", "y", "c"))

    order = ["norm_attn_pre", "norm_attn_post", "w_in", "q_latent_norm", "kv_latent_norm", "w_uq", "w_ukv", "w_out",
             "norm_mlp_pre", "norm_mlp_post", "w_up", "w_down"]
    res = {n: (small_out[n] if n in small_out else tuple(o[None] for o in big_out[n])) for n in order}
    return (loss, grad_x[None], *[res[n][0] for n in order], *[res[n][1] for n in order],
            *[res[n][2] for n in order], *[res[n][3] for n in order])
```

```python
import functools
import math

import jax
import jax.numpy as jnp
from jax import lax
from jax.experimental import pallas as pl
from jax.experimental.pallas import tpu as pltpu

F32 = jnp.float32
BF16 = jnp.bfloat16

N_DEV = 8
HEADS = 8
HD = 128
AW = HEADS * HD
Q_LORA = 512
KV_LORA = 512
ROPE_MLA = 64
ROT_A = 32
IN_COLS = 3 * AW + Q_LORA + KV_LORA + ROPE_MLA
IN_PAD = 3 * AW + Q_LORA + KV_LORA + HD
QBLK = 128
DSWA_DILATIONS = (1, 4, 16)
ROPE_THETA = 500000.0
NORM_EPS = 1e-6
NEG_INF = -1e30
SCALE_A = HD ** -0.5
SCALE_B = (HD + ROPE_MLA) ** -0.5

ADAM_LR = 0.001
ADAM_B1 = 0.9
ADAM_B2 = 0.999
ADAM_EPS = 1e-08
ADAM_WD = 0.01
ADAM_STEP = 10

VMEM_LIMIT = 48 * 1024 * 1024

NT = (((1,), (1,)), ((), ()))
NN = (((1,), (0,)), ((), ()))
TN = (((0,), (0,)), ((), ()))


def _dot(a, b, dims):
    return lax.dot_general(a, b, dims, preferred_element_type=F32)


def _params(*sem):
    return pltpu.CompilerParams(dimension_semantics=sem, vmem_limit_bytes=VMEM_LIMIT)


def _tile(n, want):
    t = min(n, want)
    while n % t:
        t //= 2
    return t


def _tile128(n, want):
    if n % 128:
        return n
    units = n // 128
    return 128 * max(u for u in range(1, max(want // 128, 1) + 1) if units % u == 0)


class _Exchange:
    def __init__(self, arrs, mode):
        self.arrs = list(arrs)
        self.mode = mode
        self.n = len(self.arrs)
        self.results = None
        hbm = pl.BlockSpec(memory_space=pltpu.HBM)
        self.specs = [hbm] * self.n
        shape = {"gather": lambda a: (N_DEV,) + a.shape, "scatter": lambda a: a.shape,
                 "pair": lambda a: a.shape[1:], "chips": lambda a: a.shape}[mode]
        self.out_shape = [jax.ShapeDtypeStruct(shape(a), a.dtype) for a in self.arrs]
        n_sem = self.n * (N_DEV - 1)
        self.scratch = [pltpu.SemaphoreType.DMA((n_sem,)), pltpu.SemaphoreType.DMA((n_sem,)),
                        pltpu.SemaphoreType.DMA((self.n,))]

    def hooks(self, ins, outs, send_sems, recv_sems, local_sems):
        x, y, c = lax.axis_index("x"), lax.axis_index("y"), lax.axis_index("c")
        me = (x, y, c)
        sib = (x, y, 1 - c)
        chips = [(1 - x, y), (x, 1 - y), (1 - x, 1 - y)]
        slot = lambda p: 4 * p[0] + 2 * p[1] + p[2]
        chip_of = lambda p: 2 * p[0] + p[1]

        def rcopy(a, k, src, dst, to):
            i = a * (N_DEV - 1) + k
            return pltpu.make_async_remote_copy(src_ref=src, dst_ref=dst, send_sem=send_sems.at[i],
                                                recv_sem=recv_sems.at[i], device_id=to,
                                                device_id_type=pl.DeviceIdType.MESH)

        def local(a):
            if self.mode == "chips":
                return pltpu.make_async_copy(ins[a].at[chip_of(me)], outs[a].at[chip_of(me)], local_sems.at[a])
            src = ins[a].at[slot(me)] if self.mode == "scatter" else ins[a]
            return pltpu.make_async_copy(src, outs[a].at[slot(me)], local_sems.at[a])

        def peer(rel):
            return (1 - x if rel & 4 else x, 1 - y if rel & 2 else y, 1 - c if rel & 1 else c)

        if self.mode == "pair":
            def start():
                for a in range(self.n):
                    rcopy(a, 0, ins[a].at[1 - c], outs[a], sib).start()

            def middle():
                pass

            def finish():
                for a in range(self.n):
                    cp = rcopy(a, 0, ins[a].at[1 - c], outs[a], sib)
                    cp.wait_send()
                    cp.wait_recv()
        elif self.mode == "chips":
            def start():
                for a in range(self.n):
                    local(a).start()
                    for j, chip in enumerate(chips):
                        rcopy(a, j, ins[a].at[chip_of(chip)], outs[a].at[chip_of(me)], (*chip, c)).start()

            def middle():
                pass

            def finish():
                for a in range(self.n):
                    for j, chip in enumerate(chips):
                        cp = rcopy(a, j, ins[a].at[chip_of(chip)], outs[a].at[chip_of(chip)], (*chip, c))
                        cp.wait_send()
                        cp.wait_recv()
                    local(a).wait()
        elif self.mode == "scatter":
            def start():
                for a in range(self.n):
                    local(a).start()
                    for rel in range(1, N_DEV):
                        rcopy(a, rel - 1, ins[a].at[slot(peer(rel))], outs[a].at[slot(me)], peer(rel)).start()

            def middle():
                pass

            def finish():
                for a in range(self.n):
                    for rel in range(1, N_DEV):
                        cp = rcopy(a, rel - 1, ins[a].at[slot(peer(rel))], outs[a].at[slot(peer(rel))], peer(rel))
                        cp.wait_send()
                        cp.wait_recv()
                    local(a).wait()
        else:
            def start():
                for a in range(self.n):
                    local(a).start()
                    rcopy(a, 0, ins[a], outs[a].at[slot(me)], sib).start()
                    for j, chip in enumerate(chips):
                        rcopy(a, 1 + j, ins[a], outs[a].at[slot(me)], (*chip, c)).start()

            def middle():
                for a in range(self.n):
                    for j, chip in enumerate(chips):
                        landed = outs[a].at[slot((*chip, c))]
                        rcopy(a, 1 + j, ins[a], landed, me).wait_recv()
                        rcopy(a, 4 + j, landed, landed, sib).start()

            def finish():
                for a in range(self.n):
                    rcopy(a, 0, ins[a], outs[a].at[slot(sib)], me).wait_recv()
                    for j, chip in enumerate(chips):
                        rcopy(a, 4 + j, ins[a], outs[a].at[slot((*chip, 1 - c))], me).wait_recv()
                    for k in range(N_DEV - 1):
                        rcopy(a, k, ins[a], outs[a].at[slot(me)], me).wait_send()
                    local(a).wait()

        return start, middle, finish

    def standalone(self, name):
        n = self.n

        def body(*refs):
            start, middle, finish = self.hooks(refs[:n], refs[n:2 * n], *refs[2 * n:])
            start()
            middle()
            finish()

        self.results = pl.pallas_call(
            body, name=name, in_specs=self.specs, out_specs=self.specs, out_shape=self.out_shape,
            scratch_shapes=self.scratch, compiler_params=pltpu.CompilerParams(has_side_effects=True),
        )(*self.arrs)
        return self.results


def _call(body, name, grid, in_specs, out_specs, out_shape, args, scratch=(), sem=(), comm=None, prefetch=()):
    npf = len(prefetch)
    if comm is None:
        spec = pltpu.PrefetchScalarGridSpec(num_scalar_prefetch=npf, grid=grid, in_specs=list(in_specs),
                                            out_specs=list(out_specs), scratch_shapes=list(scratch))
        return pl.pallas_call(body, name=name, grid_spec=spec, out_shape=list(out_shape),
                              compiler_params=_params(*sem))(*prefetch, *args)
    ni, no, ns, n = len(in_specs), len(out_specs), len(scratch), comm.n
    steps = math.prod(grid)

    def wrapped(*refs):
        pf, refs = refs[:npf], refs[npf:]
        ins, c_ins = refs[:ni], refs[ni:ni + n]
        outs, c_outs = refs[ni + n:ni + n + no], refs[ni + n + no:ni + 2 * n + no]
        scr, c_scr = refs[ni + 2 * n + no:ni + 2 * n + no + ns], refs[ni + 2 * n + no + ns:]
        start, middle, finish = comm.hooks(c_ins, c_outs, *c_scr)
        step = pl.program_id(0)
        for ax in range(1, len(grid)):
            step = step * grid[ax] + pl.program_id(ax)
        pl.when(step == 0)(start)
        pl.when(step == steps // 2)(middle)
        body(*pf, *ins, *outs, *scr)
        pl.when(step == steps - 1)(finish)

    spec = pltpu.PrefetchScalarGridSpec(num_scalar_prefetch=npf, grid=grid, in_specs=list(in_specs) + comm.specs,
                                        out_specs=list(out_specs) + comm.specs,
                                        scratch_shapes=list(scratch) + comm.scratch)
    res = pl.pallas_call(
        wrapped, name=name, grid_spec=spec, out_shape=list(out_shape) + comm.out_shape,
        compiler_params=pltpu.CompilerParams(dimension_semantics=("arbitrary",) * len(grid),
                                             vmem_limit_bytes=VMEM_LIMIT, has_side_effects=True),
    )(*prefetch, *args, *comm.arrs)
    comm.results = res[no:]
    return res[:no]


def _mm(a, b, mode, out_dtypes, name, epilogue=None, extras=(), tm=1024, tn=1024, tk=2048, comm=None):
    if mode == "tn":
        K, M = a.shape
    else:
        M, K = a.shape
    N = b.shape[0] if mode == "nt" else b.shape[1]
    tm, tn, tk = _tile128(M, tm), _tile128(N, tn), _tile128(K, tk)
    nk = K // tk
    dims = {"nn": NN, "nt": NT, "tn": TN}[mode]
    a_spec = (pl.BlockSpec((tk, tm), lambda i, j, k: (k, i)) if mode == "tn"
              else pl.BlockSpec((tm, tk), lambda i, j, k: (i, k)))
    b_spec = (pl.BlockSpec((tn, tk), lambda i, j, k: (j, k)) if mode == "nt"
              else pl.BlockSpec((tk, tn), lambda i, j, k: (k, j)))
    mn_spec = pl.BlockSpec((tm, tn), lambda i, j, k: (i, j))
    n_ex = len(extras)
    n_out = len(out_dtypes)

    def finish(acc, ex, outs):
        res = (acc,) if epilogue is None else epilogue(acc, *[e[...] for e in ex])
        for o, r in zip(outs, res):
            o[...] = r.astype(o.dtype)

    def body(*refs):
        a_ref, b_ref = refs[:2]
        ex = refs[2:2 + n_ex]
        outs = refs[2 + n_ex:2 + n_ex + n_out]
        if nk == 1:
            finish(_dot(a_ref[...], b_ref[...], dims), ex, outs)
            return
        acc = refs[-1]
        k = pl.program_id(2)

        @pl.when(k == 0)
        def _():
            acc[...] = _dot(a_ref[...], b_ref[...], dims)

        @pl.when(jnp.logical_and(k > 0, k < nk - 1))
        def _():
            acc[...] += _dot(a_ref[...], b_ref[...], dims)

        @pl.when(k == nk - 1)
        def _():
            finish(acc[...] + _dot(a_ref[...], b_ref[...], dims), ex, outs)

    out = _call(
        body, name, (M // tm, N // tn, nk), [a_spec, b_spec] + [mn_spec] * n_ex, [mn_spec] * n_out,
        [jax.ShapeDtypeStruct((M, N), dt) for dt in out_dtypes], (a, b, *extras),
        scratch=[] if nk == 1 else [pltpu.VMEM((tm, tn), F32)], sem=("parallel", "parallel", "arbitrary"),
        comm=comm)
    return out[0] if n_out == 1 else out


def _rms_fwd(x, gain, out_dtype, name, width=None, col_block=0, residual=None, tb=256, comm=None):
    T = x.shape[0]
    W = x.shape[1] if width is None else width
    tb = _tile(T, tb)
    has_res = residual is not None

    def body(*refs):
        x_ref, g_ref = refs[:2]
        o_ref = refs[-1]
        xf = x_ref[...]
        y = xf * lax.rsqrt(jnp.mean(xf * xf, axis=-1, keepdims=True) + NORM_EPS) * g_ref[...]
        if has_res:
            y = refs[2][...] + y
        o_ref[...] = y.astype(o_ref.dtype)

    row = pl.BlockSpec((tb, W), lambda i: (i, 0))
    ins = [x, gain] + ([residual] if has_res else [])
    return _call(
        body, name, (T // tb,),
        [pl.BlockSpec((tb, W), lambda i: (i, col_block)),
         pl.BlockSpec((1, W), lambda i: (0, 0))] + ([row] if has_res else []),
        [row], [jax.ShapeDtypeStruct((T, W), out_dtype)], ins, sem=("parallel",), comm=comm)[0]


def _rms_bwd(dy, x, gain, out_dtype, name, width=None, col_block=0, residual=None, tb=256, comm=None):
    T = dy.shape[0]
    W = x.shape[1] if width is None else width
    tb = _tile(T, tb)
    has_res = residual is not None

    def body(*refs):
        dy_ref, x_ref, g_ref = refs[:3]
        dx_ref, dg_ref = refs[-2:]
        i = pl.program_id(0)
        xf = x_ref[...]
        r = lax.rsqrt(jnp.mean(xf * xf, axis=-1, keepdims=True) + NORM_EPS)
        xn = xf * r
        dyf = dy_ref[...].astype(F32)
        dyg = dyf * g_ref[...]
        dx = r * (dyg - xn * jnp.mean(dyg * xn, axis=-1, keepdims=True))
        if has_res:
            dx = refs[3][...] + dx
        dx_ref[...] = dx.astype(dx_ref.dtype)

        @pl.when(i == 0)
        def _():
            dg_ref[...] = jnp.zeros_like(dg_ref)

        dg_ref[...] += jnp.sum(dyf * xn, axis=0, keepdims=True)

    row = pl.BlockSpec((tb, W), lambda i: (i, 0))
    vec = pl.BlockSpec((1, W), lambda i: (0, 0))
    ins = [dy, x, gain] + ([residual] if has_res else [])
    return _call(
        body, name, (T // tb,),
        [row, pl.BlockSpec((tb, W), lambda i: (i, col_block)), vec] + ([row] if has_res else []), [row, vec],
        [jax.ShapeDtypeStruct((T, W), out_dtype), jax.ShapeDtypeStruct((1, W), F32)], ins,
        sem=("arbitrary",), comm=comm)


def _loss_head(x2, y2, gain, target, name, tb=256):
    T, D = x2.shape
    tb = _tile(T, tb)

    def body(x2_ref, y2_ref, g_ref, t_ref, dx3_ref, loss_ref):
        i = pl.program_id(0)
        yf = y2_ref[...]
        x3 = x2_ref[...] + yf * lax.rsqrt(jnp.mean(yf * yf, axis=-1, keepdims=True) + NORM_EPS) * g_ref[...]
        e = x3 - t_ref[...]
        dx3_ref[...] = e * (1.0 / D)

        @pl.when(i == 0)
        def _():
            loss_ref[...] = jnp.zeros_like(loss_ref)

        per_row = jnp.mean(e * e, axis=-1, keepdims=True)
        loss_ref[...] += 0.5 * jnp.sum(per_row, axis=0, keepdims=True)

    row = pl.BlockSpec((tb, D), lambda i: (i, 0))
    return pl.pallas_call(
        body, name=name, grid=(T // tb,),
        in_specs=[row, row, pl.BlockSpec((1, D), lambda i: (0, 0)), row],
        out_specs=[row, pl.BlockSpec((1, 1), lambda i: (0, 0))],
        out_shape=[jax.ShapeDtypeStruct((T, D), F32), jax.ShapeDtypeStruct((1, 1), F32)],
        compiler_params=_params("arbitrary"),
    )(x2, y2, gain, target)


def _rope_tables(positions, rot_dim):
    half = rot_dim // 2
    inv_freq = ROPE_THETA ** (-jnp.arange(0, rot_dim, 2, dtype=F32) / rot_dim)
    ang = positions.astype(F32)[:, None] * inv_freq[None, :]
    cos, sin = jnp.cos(ang), jnp.sin(ang)
    T = positions.shape[0]
    ones = jnp.ones((T, HD - rot_dim), F32)
    cos_t = jnp.concatenate([cos, cos, ones], axis=1)
    sin_t = jnp.concatenate([-sin, sin, jnp.zeros_like(ones)], axis=1)
    return cos_t, sin_t


def _rotate(x, cos_t, sin_t, half):
    lane = lax.broadcasted_iota(jnp.int32, x.shape, 1)
    swapped = jnp.where(lane < half, pltpu.roll(x, HD - half, 1), pltpu.roll(x, half, 1))
    return x * cos_t + swapped * sin_t


def _rope_apply(x, cos_t, sin_t, half, n_blocks, is_rope, out_dtype, name, window=0, tb=256):
    T = x.shape[0]
    tb = _tile(T, tb)
    W = n_blocks * HD

    def body(x_ref, c_ref, s_ref, o_ref):
        for j in range(n_blocks):
            sl = slice(j * HD, (j + 1) * HD)
            xj = x_ref[:, sl]
            if is_rope(j):
                xj = _rotate(xj.astype(F32), c_ref[...], s_ref[...], half)
            o_ref[:, sl] = xj.astype(o_ref.dtype)

    tab = pl.BlockSpec((tb, HD), lambda i: (i, 0))
    return pl.pallas_call(
        body, name=name, grid=(T // tb,),
        in_specs=[pl.BlockSpec((tb, W), lambda i: (i, window)), tab, tab],
        out_specs=pl.BlockSpec((tb, W), lambda i: (i, 0)),
        out_shape=jax.ShapeDtypeStruct((T, W), out_dtype),
        compiler_params=_params("parallel"),
    )(x, cos_t, sin_t)


DSWA_TB = 2048


def _deinterleave(src, dst_ref, d, dtype):
    rows = src.shape[0] // d
    for r in range(d):
        dst_ref[r] = src[pl.ds(r, rows, stride=d), :].astype(dtype)


def _rope_dswa(proj, cos_t, sin_t, name):
    T = proj.shape[0]
    tb = _tile(T, DSWA_TB)
    half = ROT_A // 2

    def body(x_ref, c_ref, s_ref, *rest):
        outs, scr = rest[:-1], rest[-1]
        j = pl.program_id(1)

        @pl.when(j < 2 * HEADS)
        def _():
            scr[...] = _rotate(x_ref[...], c_ref[...], s_ref[...], half)

        @pl.when(j >= 2 * HEADS)
        def _():
            scr[...] = x_ref[...]

        for o_ref, d in zip(outs, DSWA_DILATIONS):
            _deinterleave(scr, o_ref, d, BF16)

    blk = pl.BlockSpec((tb, HD), lambda i, j: (i, j))
    tab = pl.BlockSpec((tb, HD), lambda i, j: (i, 0))
    return pl.pallas_call(
        body, name=name, grid=(T // tb, 3 * HEADS),
        in_specs=[blk, tab, tab],
        out_specs=[pl.BlockSpec((d, tb // d, HD), lambda i, j: (0, i, j)) for d in DSWA_DILATIONS],
        out_shape=[jax.ShapeDtypeStruct((d, T // d, 3 * AW), BF16) for d in DSWA_DILATIONS],
        scratch_shapes=[pltpu.VMEM((tb, HD), F32)],
        compiler_params=_params("parallel", "parallel"),
    )(proj, cos_t, sin_t)


def _shared_key_grad(dk, cos_t, sin_t_neg, half, name, tb=512):
    T = dk.shape[0]
    tb = _tile(T, tb)

    def body(d_ref, c_ref, s_ref, o_ref):
        tot = d_ref[:, HD:2 * HD]
        for h in range(1, HEADS):
            tot = tot + d_ref[:, h * QK + HD:(h + 1) * QK]
        o_ref[...] = _rotate(tot, c_ref[...], s_ref[...], half).astype(o_ref.dtype)

    tab = pl.BlockSpec((tb, HD), lambda i: (i, 0))
    return pl.pallas_call(
        body, name=name, grid=(T // tb,),
        in_specs=[pl.BlockSpec((tb, HEADS * QK), lambda i: (i, 0)), tab, tab],
        out_specs=tab, out_shape=jax.ShapeDtypeStruct((T, HD), BF16),
        compiler_params=_params("parallel"),
    )(dk, cos_t, sin_t_neg)


def _band_mask(n):
    row = lax.broadcasted_iota(jnp.int32, (QBLK, 2 * QBLK), 0)
    col = lax.broadcasted_iota(jnp.int32, (QBLK, 2 * QBLK), 1)
    in_prev = jnp.logical_and(jnp.logical_and(col < QBLK, col >= row), n > 0)
    in_cur = jnp.logical_and(col >= QBLK, col - QBLK <= row)
    return jnp.logical_or(in_prev, in_cur)


def _dswa_specs(nb, reverse=False):
    pos = (lambda n: nb - 1 - n) if reverse else (lambda n: n)
    cur = lambda c: pl.BlockSpec((None, QBLK, AW), lambda r, n: (r, pos(n), c))
    prev = lambda c: pl.BlockSpec((None, QBLK, AW), lambda r, n: (r, jnp.maximum(pos(n) - 1, 0), c))
    return cur, prev


def _relayout_spec(d, tb):
    return pl.BlockSpec((d, tb // d, HD), lambda i, h: (0, i, h))


def _dswa_fwd(qkv, name, comm=None):
    d, sd = qkv.shape[:2]
    nb = sd // QBLK

    def body(q_ref, kc_ref, kp_ref, vc_ref, vp_ref, o_ref, l_ref):
        mask = _band_mask(pl.program_id(1))
        for h in range(HEADS):
            sl = slice(h * HD, (h + 1) * HD)
            keys = jnp.concatenate([kp_ref[:, sl], kc_ref[:, sl]], axis=0)
            vals = jnp.concatenate([vp_ref[:, sl], vc_ref[:, sl]], axis=0)
            s = jnp.where(mask, _dot(q_ref[:, sl], keys, NT) * SCALE_A, NEG_INF)
            m = jnp.max(s, axis=-1, keepdims=True)
            p = jnp.exp(s - m)
            den = jnp.sum(p, axis=-1, keepdims=True)
            o_ref[:, sl] = _dot((p / den).astype(BF16), vals, NN)
            l_ref[:, sl] = jnp.broadcast_to(m + jnp.log(den), (QBLK, HD))

    cur, prev = _dswa_specs(nb)
    return _call(
        body, name, (d, nb), [cur(0), cur(1), prev(1), cur(2), prev(2)], [cur(0), cur(0)],
        [jax.ShapeDtypeStruct((d, sd, AW), F32)] * 2, (qkv, qkv, qkv, qkv, qkv),
        sem=("parallel", "parallel"), comm=comm)


def _dswa_merge(outs, lses, name):
    nc = len(DSWA_DILATIONS)
    T = outs[0].shape[0] * outs[0].shape[1]
    tb = _tile(T, DSWA_TB)

    def body(*refs):
        o_refs, l_refs = refs[:nc], refs[nc:2 * nc]
        out_ref, outb_ref = refs[2 * nc:2 * nc + 2]
        lt_refs = refs[2 * nc + 2:3 * nc + 2]
        o_nat, l_nat, lt_nat = refs[3 * nc + 2:4 * nc + 2], refs[4 * nc + 2:5 * nc + 2], refs[-1]
        for c, d in enumerate(DSWA_DILATIONS):
            for r in range(d):
                o_nat[c][pl.ds(r, tb // d, stride=d), :] = o_refs[c][r]
                l_nat[c][pl.ds(r, tb // d, stride=d), :] = l_refs[c][r]
        ls = [l[...] for l in l_nat]
        m = functools.reduce(jnp.maximum, ls)
        es = [jnp.exp(l - m) for l in ls]
        tot = functools.reduce(lambda a, b: a + b, es)
        acc = (es[0] / tot) * o_nat[0][...]
        for c in range(1, nc):
            acc = acc + (es[c] / tot) * o_nat[c][...]
        out_ref[...] = acc
        outb_ref[...] = acc.astype(BF16)
        lt_nat[...] = m + jnp.log(tot)
        for c, d in enumerate(DSWA_DILATIONS):
            _deinterleave(lt_nat, lt_refs[c], d, F32)

    nat = pl.BlockSpec((tb, HD), lambda i, h: (i, h))
    by_d = [_relayout_spec(d, tb) for d in DSWA_DILATIONS]
    res = pl.pallas_call(
        body, name=name, grid=(T // tb, HEADS),
        in_specs=by_d + by_d, out_specs=[nat, nat] + by_d,
        out_shape=[jax.ShapeDtypeStruct((T, AW), F32), jax.ShapeDtypeStruct((T, AW), BF16)]
        + [jax.ShapeDtypeStruct((d, T // d, AW), F32) for d in DSWA_DILATIONS],
        scratch_shapes=[pltpu.VMEM((tb, HD), F32)] * (2 * nc + 1),
        compiler_params=_params("parallel", "parallel"),
    )(*outs, *lses)
    return res[0], res[1], res[2:]


def _dswa_delta(dout, out, name):
    nc = len(DSWA_DILATIONS)
    T = out.shape[0]
    tb = _tile(T, DSWA_TB)

    def body(do_ref, o_ref, *rest):
        dl_refs, dob_refs, dl_nat = rest[:nc], rest[nc:2 * nc], rest[-1]
        dl_nat[...] = jnp.broadcast_to(jnp.sum(do_ref[...] * o_ref[...], axis=-1, keepdims=True), (tb, HD))
        for c, d in enumerate(DSWA_DILATIONS):
            _deinterleave(dl_nat, dl_refs[c], d, F32)
            _deinterleave(do_ref, dob_refs[c], d, BF16)

    nat = pl.BlockSpec((tb, HD), lambda i, h: (i, h))
    by_d = [_relayout_spec(d, tb) for d in DSWA_DILATIONS]
    res = pl.pallas_call(
        body, name=name, grid=(T // tb, HEADS),
        in_specs=[nat, nat], out_specs=by_d + by_d,
        out_shape=[jax.ShapeDtypeStruct((d, T // d, AW), F32) for d in DSWA_DILATIONS]
        + [jax.ShapeDtypeStruct((d, T // d, AW), BF16) for d in DSWA_DILATIONS],
        scratch_shapes=[pltpu.VMEM((tb, HD), F32)],
        compiler_params=_params("parallel", "parallel"),
    )(dout, out)
    return res[:nc], res[nc:]


def _delta_prep(dout, col_block, out, name, tb=256):
    T = out.shape[0]
    tb = _tile(T, tb)

    def body(do_ref, o_ref, delta_ref, dob_ref):
        for h in range(HEADS):
            sl = slice(h * HD, (h + 1) * HD)
            doh = do_ref[:, sl]
            delta_ref[:, sl] = jnp.broadcast_to(jnp.sum(doh * o_ref[:, sl], axis=-1, keepdims=True), (tb, HD))
            dob_ref[:, sl] = doh.astype(BF16)

    row = pl.BlockSpec((tb, AW), lambda i: (i, 0))
    return pl.pallas_call(
        body, name=name, grid=(T // tb,),
        in_specs=[pl.BlockSpec((tb, AW), lambda i: (i, col_block)), row], out_specs=[row, row],
        out_shape=[jax.ShapeDtypeStruct((T, AW), F32), jax.ShapeDtypeStruct((T, AW), BF16)],
        compiler_params=_params("parallel"),
    )(dout, out)


def _dswa_bwd(qkv, dout_b, lse_tot, delta, name, comm=None):
    d, sd = qkv.shape[:2]
    nb = sd // QBLK

    def body(q_ref, kc_ref, kp_ref, vc_ref, vp_ref, do_ref, l_ref, dl_ref, dq_ref, dk_ref, dv_ref, carry_k, carry_v):
        mask = _band_mask(nb - 1 - pl.program_id(1))

        @pl.when(pl.program_id(1) == 0)
        def _():
            carry_k[...] = jnp.zeros_like(carry_k)
            carry_v[...] = jnp.zeros_like(carry_v)

        for h in range(HEADS):
            sl = slice(h * HD, (h + 1) * HD)
            qh, doh = q_ref[:, sl], do_ref[:, sl]
            keys = jnp.concatenate([kp_ref[:, sl], kc_ref[:, sl]], axis=0)
            vals = jnp.concatenate([vp_ref[:, sl], vc_ref[:, sl]], axis=0)
            s = jnp.where(mask, _dot(qh, keys, NT) * SCALE_A, NEG_INF)
            p = jnp.exp(s - l_ref[:, sl][:, 0:1])
            ds = (p * (_dot(doh, vals, NT) - dl_ref[:, sl][:, 0:1]) * SCALE_A).astype(BF16)
            dq_ref[:, sl] = _dot(ds, keys, NN)
            dk = _dot(ds, qh, TN)
            dv = _dot(p.astype(BF16), doh, TN)
            dk_ref[:, sl] = dk[QBLK:] + carry_k[:, sl]
            dv_ref[:, sl] = dv[QBLK:] + carry_v[:, sl]
            carry_k[:, sl] = dk[:QBLK]
            carry_v[:, sl] = dv[:QBLK]

    cur, prev = _dswa_specs(nb, reverse=True)
    return _call(
        body, name, (d, nb), [cur(0), cur(1), prev(1), cur(2), prev(2), cur(0), cur(0), cur(0)], [cur(0)] * 3,
        [jax.ShapeDtypeStruct((d, sd, AW), F32)] * 3, (qkv, qkv, qkv, qkv, qkv, dout_b, lse_tot, delta),
        scratch=[pltpu.VMEM((QBLK, AW), F32)] * 2, sem=("parallel", "arbitrary"), comm=comm)


def _dswa_combine(grads, cos_t, sin_t_neg, name):
    nc = len(DSWA_DILATIONS)
    T = grads[0][0].shape[0] * grads[0][0].shape[1]
    tb = _tile(T, DSWA_TB)
    half = ROT_A // 2

    def body(*refs):
        g_refs = refs[:3 * nc]
        c_ref, s_ref = refs[3 * nc:3 * nc + 2]
        outs, accs = refs[3 * nc + 2:3 * nc + 5], refs[3 * nc + 5:]
        for which in range(3):
            acc = accs[which]
            for c, d in enumerate(DSWA_DILATIONS):
                g = g_refs[3 * c + which]
                for r in range(d):
                    if c == 0:
                        acc[...] = g[r]
                    else:
                        acc[pl.ds(r, tb // d, stride=d), :] += g[r]
            val = acc[...]
            if which < 2:
                val = _rotate(val, c_ref[...], s_ref[...], half)
            outs[which][...] = val.astype(BF16)

    nat = pl.BlockSpec((tb, HD), lambda i, h: (i, h))
    tab = pl.BlockSpec((tb, HD), lambda i, h: (i, 0))
    in_specs, ins = [], []
    for d, g in zip(DSWA_DILATIONS, grads):
        in_specs += [_relayout_spec(d, tb)] * 3
        ins += list(g)
    return pl.pallas_call(
        body, name=name, grid=(T // tb, HEADS),
        in_specs=in_specs + [tab, tab], out_specs=[nat] * 3,
        out_shape=[jax.ShapeDtypeStruct((T, AW), BF16)] * 3,
        scratch_shapes=[pltpu.VMEM((tb, HD), F32)] * 3,
        compiler_params=_params("parallel", "parallel"),
    )(*ins, cos_t, sin_t_neg)


MLA_TQ = 512
QK = 2 * HD


def _triangle(nq, key_major):
    pairs = [(q, k) for q in range(nq) for k in range(q + 1)]
    if key_major:
        pairs.sort(key=lambda p: (p[1], p[0]))
    return (jnp.array([p[0] for p in pairs], jnp.int32), jnp.array([p[1] for p in pairs], jnp.int32))


def _mla_specs(tq):
    q_spec = pl.BlockSpec((tq, HEADS * QK), lambda t, qi, ki: (qi[t], 0))
    k_spec = pl.BlockSpec((tq, HEADS * QK), lambda t, qi, ki: (ki[t], 0))
    v_spec = pl.BlockSpec((tq, AW), lambda t, qi, ki: (ki[t], 1))
    qrow = pl.BlockSpec((tq, AW), lambda t, qi, ki: (qi[t], 0))
    krow = pl.BlockSpec((tq, AW), lambda t, qi, ki: (ki[t], 0))
    return q_spec, k_spec, v_spec, qrow, krow


def _mla_scores(q_ref, k_ref, h, qi, ki, tq):
    s = _dot(q_ref[:, h * QK:(h + 1) * QK], k_ref[:, h * QK:(h + 1) * QK], NT) * SCALE_B
    row = lax.broadcasted_iota(jnp.int32, s.shape, 0) + qi * tq
    col = lax.broadcasted_iota(jnp.int32, s.shape, 1) + ki * tq
    return jnp.where(col <= row, s, NEG_INF)


def _mla_fwd(q, k, v1, name, comm=None):
    T = q.shape[0]
    tq = _tile(T, MLA_TQ)
    tables = _triangle(T // tq, False)

    def body(qi_ref, ki_ref, q_ref, k_ref, v_ref, o_ref, l_ref, m_s, acc):
        t = pl.program_id(0)
        qi, ki = qi_ref[t], ki_ref[t]

        @pl.when(ki == 0)
        def _():
            m_s[...] = jnp.full_like(m_s, NEG_INF)
            acc[...] = jnp.zeros_like(acc)

        updates = []
        for h in range(HEADS):
            s = _mla_scores(q_ref, k_ref, h, qi, ki, tq)
            m_new = jnp.maximum(m_s[h], jnp.max(s, axis=-1, keepdims=True))
            p = jnp.exp(s - m_new).astype(BF16)
            updates.append((m_new, jnp.exp(m_s[h] - m_new), _dot(p, v_ref[:, h * QK:(h + 1) * QK], NN)))
        for h, (m_new, alpha, pv) in enumerate(updates):
            acc[:, h * QK:(h + 1) * QK] = alpha * acc[:, h * QK:(h + 1) * QK] + pv
            m_s[h] = m_new

        @pl.when(ki == qi)
        def _():
            for h in range(HEADS):
                sl = slice(h * HD, (h + 1) * HD)
                den = acc[:, h * QK + HD:h * QK + HD + 1]
                o_ref[:, sl] = acc[:, h * QK:h * QK + HD] / den
                l_ref[:, sl] = jnp.broadcast_to(m_s[h] + jnp.log(den), (tq, HD))

    q_spec, k_spec, _, qrow, _ = _mla_specs(tq)
    return _call(
        body, name, (tables[0].shape[0],), [q_spec, k_spec, k_spec], [qrow, qrow],
        [jax.ShapeDtypeStruct((T, AW), F32)] * 2, (q, k, v1),
        scratch=[pltpu.VMEM((HEADS, tq, 1), F32), pltpu.VMEM((tq, HEADS * QK), F32)],
        sem=("arbitrary",), comm=comm, prefetch=tables)


def _mla_ds(q_ref, k_ref, v_ref, do_ref, l_ref, dl_ref, h, qi, ki, tq):
    sl = slice(h * HD, (h + 1) * HD)
    p = jnp.exp(_mla_scores(q_ref, k_ref, h, qi, ki, tq) - l_ref[:, sl][:, 0:1])
    ds = (p * (_dot(do_ref[:, sl], v_ref[:, sl], NT) - dl_ref[:, sl][:, 0:1]) * SCALE_B).astype(BF16)
    return p, ds


def _mla_bwd_q(q, k, kv, dout_b, lse, delta, name, comm=None):
    T = q.shape[0]
    tq = _tile(T, MLA_TQ)
    tables = _triangle(T // tq, False)

    def body(qi_ref, ki_ref, q_ref, k_ref, v_ref, do_ref, l_ref, dl_ref, dq_ref):
        t = pl.program_id(0)
        qi, ki = qi_ref[t], ki_ref[t]

        @pl.when(ki == 0)
        def _():
            dq_ref[...] = jnp.zeros_like(dq_ref)

        for h in range(HEADS):
            _, ds = _mla_ds(q_ref, k_ref, v_ref, do_ref, l_ref, dl_ref, h, qi, ki, tq)
            dq_ref[:, h * QK:(h + 1) * QK] += _dot(ds, k_ref[:, h * QK:(h + 1) * QK], NN)

    q_spec, k_spec, v_spec, qrow, _ = _mla_specs(tq)
    return _call(
        body, name, (tables[0].shape[0],), [q_spec, k_spec, v_spec, qrow, qrow, qrow], [q_spec],
        [jax.ShapeDtypeStruct((T, HEADS * QK), F32)], (q, k, kv, dout_b, lse, delta),
        sem=("arbitrary",), comm=comm, prefetch=tables)[0]


def _mla_bwd_kv(q, k, kv, dout_b, lse, delta, name, comm=None):
    T = q.shape[0]
    tq = _tile(T, MLA_TQ)
    tables = _triangle(T // tq, True)

    def body(qi_ref, ki_ref, q_ref, k_ref, v_ref, do_ref, l_ref, dl_ref, dk_ref, dv_ref):
        t = pl.program_id(0)
        qi, ki = qi_ref[t], ki_ref[t]

        @pl.when(qi == ki)
        def _():
            dk_ref[...] = jnp.zeros_like(dk_ref)
            dv_ref[...] = jnp.zeros_like(dv_ref)

        for h in range(HEADS):
            sl = slice(h * HD, (h + 1) * HD)
            p, ds = _mla_ds(q_ref, k_ref, v_ref, do_ref, l_ref, dl_ref, h, qi, ki, tq)
            dv_ref[:, sl] += _dot(p.astype(BF16), do_ref[:, sl], TN)
            dk_ref[:, h * QK:(h + 1) * QK] += _dot(ds, q_ref[:, h * QK:(h + 1) * QK], TN)

    q_spec, k_spec, v_spec, qrow, krow = _mla_specs(tq)
    return _call(
        body, name, (tables[0].shape[0],), [q_spec, k_spec, v_spec, qrow, qrow, qrow], [k_spec, krow],
        [jax.ShapeDtypeStruct((T, HEADS * QK), F32), jax.ShapeDtypeStruct((T, AW), F32)],
        (q, k, kv, dout_b, lse, delta), sem=("arbitrary",), comm=comm, prefetch=tables)


def _pair_sum(by_core, from_sibling, name, tb=256):
    _, n_chip, R, C = by_core.shape
    tb = _tile(R, tb)
    core = jnp.reshape(lax.axis_index("c"), (1,)).astype(jnp.int32)

    def body(core_ref, mine_ref, theirs_ref, o_ref):
        o_ref[...] = (mine_ref[...].astype(F32) + theirs_ref[...].astype(F32)).astype(o_ref.dtype)

    blk = pl.BlockSpec((None, tb, C), lambda p, i, core_ref: (p, i, 0))
    return _call(
        body, name, (n_chip, R // tb),
        [pl.BlockSpec((None, None, tb, C), lambda p, i, core_ref: (core_ref[0], p, i, 0)), blk], [blk],
        [jax.ShapeDtypeStruct((n_chip, R, C), BF16)], (by_core, from_sibling),
        sem=("parallel", "parallel"), prefetch=(core,))[0]


def _adamw(parts, w, m, v, name, tb=128, comm=None):
    R, C = w.shape
    n_parts = parts.shape[0]
    tb = _tile(R, tb)
    c1 = 1.0 - ADAM_B1
    c2 = 1.0 - ADAM_B2
    bc1 = 1.0 - ADAM_B1 ** ADAM_STEP
    bc2 = 1.0 - ADAM_B2 ** ADAM_STEP

    def body(p_ref, w_ref, m_ref, v_ref, g_ref, d_ref, nm_ref, nv_ref):
        g = p_ref[0].astype(F32)
        for j in range(1, n_parts):
            g = g + p_ref[j].astype(F32)
        nm = ADAM_B1 * m_ref[...] + c1 * g
        nv = ADAM_B2 * v_ref[...] + c2 * (g * g)
        g_ref[...] = g
        nm_ref[...] = nm
        nv_ref[...] = nv
        d_ref[...] = -ADAM_LR * ((nm / bc1) / (jnp.sqrt(nv / bc2) + ADAM_EPS) + ADAM_WD * w_ref[...])

    row = pl.BlockSpec((tb, C), lambda i: (i, 0))
    return _call(
        body, name, (R // tb,), [pl.BlockSpec((n_parts, tb, C), lambda i: (0, i, 0)), row, row, row], [row] * 4,
        [jax.ShapeDtypeStruct((R, C), F32)] * 4, (parts, w, m, v), sem=("parallel",), comm=comm)


def _cols_from_shards(g):
    return jnp.transpose(g, (1, 0, 2)).reshape(g.shape[1], N_DEV * g.shape[2])


def _cols_to_shards(w):
    return jnp.transpose(w.reshape(w.shape[0], N_DEV, w.shape[1] // N_DEV), (1, 0, 2))


def _split_heads(w, first):
    w3 = w.reshape(w.shape[0], HEADS, -1)
    return w3[:, :, :first].reshape(w.shape[0], -1), w3[:, :, first:].reshape(w.shape[0], -1)


def _join_heads(a, b):
    R = a.shape[0]
    return jnp.concatenate([a.reshape(R, HEADS, -1), b.reshape(R, HEADS, -1)], axis=2).reshape(R, -1)


def _pad_heads(w, width):
    w3 = w.reshape(w.shape[0], HEADS, -1)
    return jnp.pad(w3, ((0, 0), (0, 0), (0, width - w3.shape[2]))).reshape(w.shape[0], HEADS * width)


def _unpad_heads(w, k):
    return w.reshape(w.shape[0], HEADS, -1)[:, :, :k].reshape(w.shape[0], HEADS * k)


def kernel(x, positions, norm_attn_pre, norm_attn_post, w_in, q_latent_norm, kv_latent_norm, w_uq, w_ukv, w_out, norm_mlp_pre, norm_mlp_post, w_up, w_down, loss_target, m_norm_attn_pre, m_norm_attn_post, m_w_in, m_q_latent_norm, m_kv_latent_norm, m_w_uq, m_w_ukv, m_w_out, m_norm_mlp_pre, m_norm_mlp_post, m_w_up, m_w_down, v_norm_attn_pre, v_norm_attn_post, v_w_in, v_q_latent_norm, v_kv_latent_norm, v_w_uq, v_w_ukv, v_w_out, v_norm_mlp_pre, v_norm_mlp_post, v_w_up, v_w_down):
    xs = x[0]
    tgt = loss_target[0]
    pos = positions[0]
    T, D = xs.shape
    big = dict(w_in=(w_in, m_w_in, v_w_in), w_uq=(w_uq, m_w_uq, v_w_uq), w_ukv=(w_ukv, m_w_ukv, v_w_ukv),
               w_out=(w_out, m_w_out, v_w_out), w_up=(w_up, m_w_up, v_w_up), w_down=(w_down, m_w_down, v_w_down))
    big = {n: tuple(t[0] for t in ts) for n, ts in big.items()}
    big_names = ["w_in", "w_uq", "w_ukv", "w_out", "w_up", "w_down"]
    col_sharded = {"w_in", "w_uq", "w_ukv", "w_up"}

    wb = {n: big[n][0].astype(BF16) for n in big_names}

    def gathered(ex, i, n):
        g = ex.results[i]
        return _cols_from_shards(g) if n in col_sharded else g.reshape(-1, g.shape[2])

    def scatter_of(g, n):
        return _Exchange([_cols_to_shards(g) if n in col_sharded
                          else g.reshape(N_DEV, g.shape[0] // N_DEV, g.shape[1])], "scatter")

    def by_core(g, n):
        if n in col_sharded:
            return jnp.transpose(g.reshape(g.shape[0], 4, 2, g.shape[1] // N_DEV), (2, 1, 0, 3))
        return jnp.transpose(g.reshape(4, 2, g.shape[0] // N_DEV, g.shape[1]), (1, 0, 2, 3))

    cos_a, sin_a = _rope_tables(pos, ROT_A)
    cos_b, sin_b = _rope_tables(pos, ROPE_MLA)

    ex_in = _Exchange([wb["w_in"]], "gather")
    h1 = _rms_fwd(xs, norm_attn_pre, BF16, "norm_attn_pre_fwd", comm=ex_in)
    Wi = jnp.pad(gathered(ex_in, 0, "w_in"), ((0, 0), (0, IN_PAD - IN_COLS)))
    ex_mid = _Exchange([wb["w_uq"], wb["w_ukv"], wb["w_out"]], "gather")
    proj = _mm(h1, Wi, "nn", [F32], "proj_in", tn=1408, comm=ex_mid)
    Wuq = _pad_heads(gathered(ex_mid, 0, "w_uq"), QK)
    Wukv = jnp.concatenate(_split_heads(gathered(ex_mid, 1, "w_ukv"), HD), axis=1)
    Wo = gathered(ex_mid, 2, "w_out")
    qkv_by_d = _rope_dswa(proj, cos_a, sin_a, "rope_dswa")
    n_quarter = wb["w_down"].shape[0] // 4
    ex_down = [_Exchange([wb["w_down"][i * n_quarter:(i + 1) * n_quarter]], "gather") for i in range(4)]
    outs, lses = [], []
    for d, qkv, ex in zip(DSWA_DILATIONS, qkv_by_d, ex_down):
        o, l = _dswa_fwd(qkv, f"dswa_fwd_d{d}", comm=ex)
        outs.append(o)
        lses.append(l)
    a_out, a_out_b, a_lse_by_d = _dswa_merge(outs, lses, "dswa_merge")

    cqn = _rms_fwd(proj, q_latent_norm, BF16, "q_latent_norm_fwd", width=Q_LORA, col_block=3 * AW // Q_LORA)
    ckvn = _rms_fwd(proj, kv_latent_norm, BF16, "kv_latent_norm_fwd", width=KV_LORA, col_block=3 * AW // KV_LORA + 1)
    qb = _mm(cqn, Wuq, "nn", [F32], "q_up")
    kvb = _mm(ckvn, Wukv, "nn", [BF16], "kv_up")
    odd = lambda j: j % 2 == 1
    q_mla = _rope_apply(qb, cos_b, sin_b, ROPE_MLA // 2, 2 * HEADS, odd, BF16, "rope_mla_q")
    kr = _rope_apply(proj, cos_b, sin_b, ROPE_MLA // 2, 1, lambda j: True, BF16, "rope_mla_k",
                     window=(IN_PAD - HD) // HD)
    k_mla = jnp.concatenate([kvb[:, :AW].reshape(T, HEADS, HD), jnp.broadcast_to(kr[:, None, :], (T, HEADS, HD))],
                            axis=2).reshape(T, HEADS * QK)
    v1_mla = jnp.concatenate([kvb[:, AW:].reshape(T, HEADS, HD), jnp.ones((T, HEADS, 1), BF16),
                              jnp.zeros((T, HEADS, HD - 1), BF16)], axis=2).reshape(T, HEADS * QK)
    ex_up = _Exchange([wb["w_up"]], "gather")
    b_out, b_lse = _mla_fwd(q_mla, k_mla, v1_mla, "mla_fwd", comm=ex_up)
    Wup = gathered(ex_up, 0, "w_up")

    mixed = jnp.concatenate([a_out_b, b_out.astype(BF16)], axis=1)
    y1 = _mm(mixed, Wo, "nn", [F32], "attn_out")
    x2 = _rms_fwd(y1, norm_attn_post, F32, "norm_attn_post_fwd", residual=xs)

    h2 = _rms_fwd(x2, norm_mlp_pre, BF16, "norm_mlp_pre_fwd")

    def relu2(z):
        r = jnp.maximum(z, 0.0)
        return r * r, r

    u, zr = _mm(h2, Wup, "nn", [BF16, BF16], "mlp_up", epilogue=relu2, comm=ex_down[3])
    Wdn = jnp.concatenate([ex.results[0] for ex in ex_down], axis=1).reshape(-1, D)
    y2 = _mm(u, Wdn, "nn", [F32], "mlp_down")
    dx3, loss_part = _loss_head(x2, y2, norm_mlp_post, tgt, "loss_head")

    dy2, dg_mlp_post = _rms_bwd(dx3, y2, norm_mlp_post, BF16, "norm_mlp_post_bwd")
    dz = _mm(dy2, Wdn, "nt", [BF16], "mlp_down_dx", epilogue=lambda du, r: (du * (2.0 * r.astype(F32)),), extras=(zr,))
    g_down = _mm(u, dy2, "tn", [BF16], "mlp_down_dw")
    down_core = by_core(g_down, "w_down")
    pair_down = _Exchange([down_core], "pair")
    g_up = _mm(h2, dz, "tn", [BF16], "mlp_up_dw", comm=pair_down)
    down_chip = _pair_sum(down_core, pair_down.results[0], "pair_sum_w_down")
    up_core = by_core(g_up, "w_up")
    pair_up = _Exchange([up_core], "pair")
    dh2 = _mm(dz, Wup, "nt", [F32], "mlp_up_dx", comm=pair_up)
    up_chip = _pair_sum(up_core, pair_up.results[0], "pair_sum_w_up")
    dx2, dg_mlp_pre = _rms_bwd(dh2, x2, norm_mlp_pre, F32, "norm_mlp_pre_bwd", residual=dx3)

    dy1, dg_attn_post = _rms_bwd(dx2, y1, norm_attn_post, BF16, "norm_attn_post_bwd")
    dmixed = _mm(dy1, Wo, "nt", [F32], "attn_out_dx")
    g_out = _mm(mixed, dy1, "tn", [BF16], "attn_out_dw")

    b_delta, b_dout = _delta_prep(dmixed, 1, b_out, "mla_delta")
    sc_down = _Exchange([down_chip], "chips")
    dq_mla = _mla_bwd_q(q_mla, k_mla, kvb, b_dout, b_lse, b_delta, "mla_bwd_q", comm=sc_down)
    sc_up = _Exchange([up_chip], "chips")
    dk_mla, dvb = _mla_bwd_kv(q_mla, k_mla, kvb, b_dout, b_lse, b_delta, "mla_bwd_kv", comm=sc_up)
    dqb = _rope_apply(dq_mla, cos_b, -sin_b, ROPE_MLA // 2, 2 * HEADS, odd, BF16, "rope_mla_q_bwd")
    dkn = dk_mla.reshape(T, HEADS, QK)[:, :, :HD].reshape(T, AW)
    dkvb = jnp.concatenate([dkn, dvb], axis=1).astype(BF16)
    d_kr = _shared_key_grad(dk_mla, cos_b, -sin_b, ROPE_MLA // 2, "rope_mla_k_bwd")
    g_uq_pad = _mm(cqn, dqb, "tn", [BF16], "q_up_dw")
    g_ukv_perm = _mm(ckvn, dkvb, "tn", [BF16], "kv_up_dw")
    dcqn = _mm(dqb, Wuq, "nt", [F32], "q_up_dx")
    dckvn = _mm(dkvb, Wukv, "nt", [F32], "kv_up_dx")
    d_cq, dg_q = _rms_bwd(dcqn, proj, q_latent_norm, BF16, "q_latent_norm_bwd", width=Q_LORA, col_block=3 * AW // Q_LORA)
    d_ckv, dg_kv = _rms_bwd(dckvn, proj, kv_latent_norm, BF16, "kv_latent_norm_bwd", width=KV_LORA,
                            col_block=3 * AW // KV_LORA + 1)

    g_uq = _unpad_heads(g_uq_pad, HD + ROPE_MLA)
    g_ukv = _join_heads(g_ukv_perm[:, :AW], g_ukv_perm[:, AW:])
    sc_out = scatter_of(g_out, "w_out")
    sc_uq = _Exchange([_cols_to_shards(g_uq), _cols_to_shards(g_ukv)], "scatter")
    a_delta_by_d, a_dout_by_d = _dswa_delta(dmixed, a_out, "dswa_delta")
    a_grads = [_dswa_bwd(qkv_by_d[c], a_dout_by_d[c], a_lse_by_d[c], a_delta_by_d[c], f"dswa_bwd_d{d}", comm=cm)
               for c, (d, cm) in enumerate(zip(DSWA_DILATIONS, (sc_out, sc_uq, None)))]
    d_aq, d_ak, d_av = _dswa_combine(a_grads, cos_a, -sin_a, "dswa_combine")

    dproj = jnp.concatenate([d_aq, d_ak, d_av, d_cq, d_ckv, d_kr], axis=1)
    g_in_pad = _mm(h1, dproj, "tn", [BF16], "proj_in_dw", tn=1408)
    in_core = by_core(g_in_pad[:, :IN_COLS], "w_in")
    pair_in = _Exchange([in_core], "pair")
    dh1 = _mm(dproj, Wi, "nt", [F32], "proj_in_dx", comm=pair_in)
    in_chip = _pair_sum(in_core, pair_in.results[0], "pair_sum_w_in")
    cuts = [0, 3 * D // 8, 11 * D // 16, D]
    sc_in = [_Exchange([in_chip[:, a:b]], "chips") for a, b in zip(cuts[:-1], cuts[1:])]
    grad_x, dg_attn_pre = _rms_bwd(dh1, xs, norm_attn_pre, F32, "norm_attn_pre_bwd", residual=dx2, comm=sc_in[0])

    big_out = dict(w_down=_adamw(sc_down.results[0], *big["w_down"], "adamw_w_down", comm=sc_in[1]),
                   w_up=_adamw(sc_up.results[0], *big["w_up"], "adamw_w_up", comm=sc_in[2]))
    parts = dict(w_in=jnp.concatenate([sc.results[0] for sc in sc_in], axis=1), w_uq=sc_uq.results[0],
                 w_ukv=sc_uq.results[1], w_out=sc_out.results[0])
    big_out.update({n: _adamw(parts[n], *big[n], f"adamw_{n}") for n in parts})

    gain_names = ["norm_attn_pre", "norm_attn_post", "q_latent_norm", "kv_latent_norm", "norm_mlp_pre", "norm_mlp_post"]
    gain_args = dict(norm_attn_pre=(norm_attn_pre, m_norm_attn_pre, v_norm_attn_pre),
                     norm_attn_post=(norm_attn_post, m_norm_attn_post, v_norm_attn_post),
                     q_latent_norm=(q_latent_norm, m_q_latent_norm, v_q_latent_norm),
                     kv_latent_norm=(kv_latent_norm, m_kv_latent_norm, v_kv_latent_norm),
                     norm_mlp_pre=(norm_mlp_pre, m_norm_mlp_pre, v_norm_mlp_pre),
                     norm_mlp_post=(norm_mlp_post, m_norm_mlp_post, v_norm_mlp_post))
    gain_grads = dict(norm_attn_pre=dg_attn_pre, norm_attn_post=dg_attn_post, q_latent_norm=dg_q,
                      kv_latent_norm=dg_kv, norm_mlp_pre=dg_mlp_pre, norm_mlp_post=dg_mlp_post)
    packed = jnp.concatenate([gain_grads[n] for n in gain_names], axis=1)
    gain_parts = _Exchange([packed], "gather").standalone("gather_gain_grads")[0]
    pack3 = lambda i: jnp.concatenate([gain_args[n][i] for n in gain_names], axis=1)
    gain_out = _adamw(gain_parts, pack3(0), pack3(1), pack3(2), "adamw_gains", tb=1)
    offs = [0]
    for n in gain_names:
        offs.append(offs[-1] + gain_args[n][0].shape[1])
    small_out = {n: tuple(o[:, offs[i]:offs[i + 1]] for o in gain_out) for i, n in enumerate(gain_names)}

    loss = lax.psum(loss_part[0, 0], ("x", "y", "c"))

    order = ["norm_attn_pre", "norm_attn_post", "w_in", "q_latent_norm", "kv_latent_norm", "w_uq", "w_ukv", "w_out",
             "norm_mlp_pre", "norm_mlp_post", "w_up", "w_down"]
    res = {n: (small_out[n] if n in small_out else tuple(o[None] for o in big_out[n])) for n in order}
    return (loss, grad_x[None], *[res[n][0] for n in order], *[res[n][1] for n in order],
            *[res[n][2] for n in order], *[res[n][3] for n in order])
```

```python
import functools
import math

import jax
import jax.numpy as jnp
from jax import lax
from jax.experimental import pallas as pl
from jax.experimental.pallas import tpu as pltpu

F32 = jnp.float32
BF16 = jnp.bfloat16

N_DEV = 8
HEADS = 8
HD = 128
AW = HEADS * HD
Q_LORA = 512
KV_LORA = 512
ROPE_MLA = 64
ROT_A = 32
IN_COLS = 3 * AW + Q_LORA + KV_LORA + ROPE_MLA
IN_PAD = 3 * AW + Q_LORA + KV_LORA + HD
QBLK = 128
DSWA_DILATIONS = (1, 4, 16)
ROPE_THETA = 500000.0
NORM_EPS = 1e-6
NEG_INF = -1e30
SCALE_A = HD ** -0.5
SCALE_B = (HD + ROPE_MLA) ** -0.5

ADAM_LR = 0.001
ADAM_B1 = 0.9
ADAM_B2 = 0.999
ADAM_EPS = 1e-08
ADAM_WD = 0.01
ADAM_STEP = 10

VMEM_LIMIT = 48 * 1024 * 1024

NT = (((1,), (1,)), ((), ()))
NN = (((1,), (0,)), ((), ()))
TN = (((0,), (0,)), ((), ()))


def _dot(a, b, dims):
    return lax.dot_general(a, b, dims, preferred_element_type=F32)


def _params(*sem):
    return pltpu.CompilerParams(dimension_semantics=sem, vmem_limit_bytes=VMEM_LIMIT)


def _tile(n, want):
    t = min(n, want)
    while n % t:
        t //= 2
    return t


def _tile128(n, want):
    if n % 128:
        return n
    units = n // 128
    return 128 * max(u for u in range(1, max(want // 128, 1) + 1) if units % u == 0)


class _Exchange:
    def __init__(self, arrs, mode):
        self.arrs = list(arrs)
        self.mode = mode
        self.n = len(self.arrs)
        self.results = None
        hbm = pl.BlockSpec(memory_space=pltpu.HBM)
        self.specs = [hbm] * self.n
        shape = {"gather": lambda a: (N_DEV,) + a.shape, "scatter": lambda a: a.shape,
                 "pair": lambda a: (4,) + a.shape[1:], "chips": lambda a: a.shape}[mode]
        self.out_shape = [jax.ShapeDtypeStruct(shape(a), a.dtype) for a in self.arrs]
        n_sem = self.n * (N_DEV - 1)
        self.scratch = [pltpu.SemaphoreType.DMA((n_sem,)), pltpu.SemaphoreType.DMA((n_sem,)),
                        pltpu.SemaphoreType.DMA((self.n,))]

    def hooks(self, ins, outs, send_sems, recv_sems, local_sems):
        x, y, c = lax.axis_index("x"), lax.axis_index("y"), lax.axis_index("c")
        me = (x, y, c)
        sib = (x, y, 1 - c)
        chips = [(1 - x, y), (x, 1 - y), (1 - x, 1 - y)]
        slot = lambda p: 4 * p[0] + 2 * p[1] + p[2]
        chip_of = lambda p: 2 * p[0] + p[1]

        def rcopy(a, k, src, dst, to):
            i = a * (N_DEV - 1) + k
            return pltpu.make_async_remote_copy(src_ref=src, dst_ref=dst, send_sem=send_sems.at[i],
                                                recv_sem=recv_sems.at[i], device_id=to,
                                                device_id_type=pl.DeviceIdType.MESH)

        def local(a):
            if self.mode == "chips":
                return pltpu.make_async_copy(ins[a].at[chip_of(me)], outs[a].at[chip_of(me)], local_sems.at[a])
            src = ins[a].at[slot(me)] if self.mode == "scatter" else ins[a]
            return pltpu.make_async_copy(src, outs[a].at[slot(me)], local_sems.at[a])

        def peer(rel):
            return (1 - x if rel & 4 else x, 1 - y if rel & 2 else y, 1 - c if rel & 1 else c)

        if self.mode == "pair":
            def start():
                for a in range(self.n):
                    for p in range(4):
                        rcopy(a, p, ins[a].at[2 * p + 1 - c], outs[a].at[p], sib).start()

            def middle():
                pass

            def finish():
                for a in range(self.n):
                    for p in range(4):
                        cp = rcopy(a, p, ins[a].at[2 * p + 1 - c], outs[a].at[p], sib)
                        cp.wait_send()
                        cp.wait_recv()
        elif self.mode == "chips":
            def start():
                for a in range(self.n):
                    local(a).start()
                    for j, chip in enumerate(chips):
                        rcopy(a, j, ins[a].at[chip_of(chip)], outs[a].at[chip_of(me)], (*chip, c)).start()

            def middle():
                pass

            def finish():
                for a in range(self.n):
                    for j, chip in enumerate(chips):
                        cp = rcopy(a, j, ins[a].at[chip_of(chip)], outs[a].at[chip_of(chip)], (*chip, c))
                        cp.wait_send()
                        cp.wait_recv()
                    local(a).wait()
        elif self.mode == "scatter":
            def start():
                for a in range(self.n):
                    local(a).start()
                    for rel in range(1, N_DEV):
                        rcopy(a, rel - 1, ins[a].at[slot(peer(rel))], outs[a].at[slot(me)], peer(rel)).start()

            def middle():
                pass

            def finish():
                for a in range(self.n):
                    for rel in range(1, N_DEV):
                        cp = rcopy(a, rel - 1, ins[a].at[slot(peer(rel))], outs[a].at[slot(peer(rel))], peer(rel))
                        cp.wait_send()
                        cp.wait_recv()
                    local(a).wait()
        else:
            def start():
                for a in range(self.n):
                    local(a).start()
                    rcopy(a, 0, ins[a], outs[a].at[slot(me)], sib).start()
                    for j, chip in enumerate(chips):
                        rcopy(a, 1 + j, ins[a], outs[a].at[slot(me)], (*chip, c)).start()

            def middle():
                for a in range(self.n):
                    for j, chip in enumerate(chips):
                        landed = outs[a].at[slot((*chip, c))]
                        rcopy(a, 1 + j, ins[a], landed, me).wait_recv()
                        rcopy(a, 4 + j, landed, landed, sib).start()

            def finish():
                for a in range(self.n):
                    rcopy(a, 0, ins[a], outs[a].at[slot(sib)], me).wait_recv()
                    for j, chip in enumerate(chips):
                        rcopy(a, 4 + j, ins[a], outs[a].at[slot((*chip, 1 - c))], me).wait_recv()
                    for k in range(N_DEV - 1):
                        rcopy(a, k, ins[a], outs[a].at[slot(me)], me).wait_send()
                    local(a).wait()

        return start, middle, finish

    def standalone(self, name):
        n = self.n

        def body(*refs):
            start, middle, finish = self.hooks(refs[:n], refs[n:2 * n], *refs[2 * n:])
            start()
            middle()
            finish()

        self.results = pl.pallas_call(
            body, name=name, in_specs=self.specs, out_specs=self.specs, out_shape=self.out_shape,
            scratch_shapes=self.scratch, compiler_params=pltpu.CompilerParams(has_side_effects=True),
        )(*self.arrs)
        return self.results


def _call(body, name, grid, in_specs, out_specs, out_shape, args, scratch=(), sem=(), comm=None, prefetch=()):
    npf = len(prefetch)
    if comm is None:
        spec = pltpu.PrefetchScalarGridSpec(num_scalar_prefetch=npf, grid=grid, in_specs=list(in_specs),
                                            out_specs=list(out_specs), scratch_shapes=list(scratch))
        return pl.pallas_call(body, name=name, grid_spec=spec, out_shape=list(out_shape),
                              compiler_params=_params(*sem))(*prefetch, *args)
    ni, no, ns, n = len(in_specs), len(out_specs), len(scratch), comm.n
    steps = math.prod(grid)

    def wrapped(*refs):
        pf, refs = refs[:npf], refs[npf:]
        ins, c_ins = refs[:ni], refs[ni:ni + n]
        outs, c_outs = refs[ni + n:ni + n + no], refs[ni + n + no:ni + 2 * n + no]
        scr, c_scr = refs[ni + 2 * n + no:ni + 2 * n + no + ns], refs[ni + 2 * n + no + ns:]
        start, middle, finish = comm.hooks(c_ins, c_outs, *c_scr)
        step = pl.program_id(0)
        for ax in range(1, len(grid)):
            step = step * grid[ax] + pl.program_id(ax)
        pl.when(step == 0)(start)
        pl.when(step == steps // 2)(middle)
        body(*pf, *ins, *outs, *scr)
        pl.when(step == steps - 1)(finish)

    spec = pltpu.PrefetchScalarGridSpec(num_scalar_prefetch=npf, grid=grid, in_specs=list(in_specs) + comm.specs,
                                        out_specs=list(out_specs) + comm.specs,
                                        scratch_shapes=list(scratch) + comm.scratch)
    res = pl.pallas_call(
        wrapped, name=name, grid_spec=spec, out_shape=list(out_shape) + comm.out_shape,
        compiler_params=pltpu.CompilerParams(dimension_semantics=("arbitrary",) * len(grid),
                                             vmem_limit_bytes=VMEM_LIMIT, has_side_effects=True),
    )(*prefetch, *args, *comm.arrs)
    comm.results = res[no:]
    return res[:no]


def _mm(a, b, mode, out_dtypes, name, epilogue=None, extras=(), tm=1024, tn=1024, tk=2048, comm=None,
        b_shards=False, out_shards=False):
    if mode == "tn":
        K, M = a.shape
    else:
        M, K = a.shape
    if b_shards:
        N = b.shape[1] if mode == "nt" else N_DEV * b.shape[2]
    else:
        N = b.shape[0] if mode == "nt" else b.shape[1]
    tm, tn, tk = _tile128(M, tm), _tile128(N, tn), _tile128(K, tk)
    if b_shards and mode == "nt":
        tk = K // N_DEV
    elif b_shards or out_shards:
        tn = N // N_DEV
    nk = K // tk
    dims = {"nn": NN, "nt": NT, "tn": TN}[mode]
    a_spec = (pl.BlockSpec((tk, tm), lambda i, j, k: (k, i)) if mode == "tn"
              else pl.BlockSpec((tm, tk), lambda i, j, k: (i, k)))
    if b_shards:
        b_spec = (pl.BlockSpec((None, tn, tk), lambda i, j, k: (k, j, 0)) if mode == "nt"
                  else pl.BlockSpec((None, tk, tn), lambda i, j, k: (j, k, 0)))
    else:
        b_spec = (pl.BlockSpec((tn, tk), lambda i, j, k: (j, k)) if mode == "nt"
                  else pl.BlockSpec((tk, tn), lambda i, j, k: (k, j)))
    mn_spec = pl.BlockSpec((tm, tn), lambda i, j, k: (i, j))
    out_spec = pl.BlockSpec((None, tm, tn), lambda i, j, k: (j, i, 0)) if out_shards else mn_spec
    out_dims = (N_DEV, M, N // N_DEV) if out_shards else (M, N)
    n_ex = len(extras)
    n_out = len(out_dtypes)

    def finish(acc, ex, outs):
        res = (acc,) if epilogue is None else epilogue(acc, *[e[...] for e in ex])
        for o, r in zip(outs, res):
            o[...] = r.astype(o.dtype)

    def body(*refs):
        a_ref, b_ref = refs[:2]
        ex = refs[2:2 + n_ex]
        outs = refs[2 + n_ex:2 + n_ex + n_out]
        if nk == 1:
            finish(_dot(a_ref[...], b_ref[...], dims), ex, outs)
            return
        acc = refs[-1]
        k = pl.program_id(2)

        @pl.when(k == 0)
        def _():
            acc[...] = _dot(a_ref[...], b_ref[...], dims)

        @pl.when(jnp.logical_and(k > 0, k < nk - 1))
        def _():
            acc[...] += _dot(a_ref[...], b_ref[...], dims)

        @pl.when(k == nk - 1)
        def _():
            finish(acc[...] + _dot(a_ref[...], b_ref[...], dims), ex, outs)

    out = _call(
        body, name, (M // tm, N // tn, nk), [a_spec, b_spec] + [mn_spec] * n_ex, [out_spec] * n_out,
        [jax.ShapeDtypeStruct(out_dims, dt) for dt in out_dtypes], (a, b, *extras),
        scratch=[] if nk == 1 else [pltpu.VMEM((tm, tn), F32)], sem=("parallel", "parallel", "arbitrary"),
        comm=comm)
    return out[0] if n_out == 1 else out


def _rms_fwd(x, gain, out_dtype, name, width=None, col_block=0, residual=None, tb=256, comm=None):
    T = x.shape[0]
    W = x.shape[1] if width is None else width
    tb = _tile(T, tb)
    has_res = residual is not None

    def body(*refs):
        x_ref, g_ref = refs[:2]
        o_ref = refs[-1]
        xf = x_ref[...]
        y = xf * lax.rsqrt(jnp.mean(xf * xf, axis=-1, keepdims=True) + NORM_EPS) * g_ref[...]
        if has_res:
            y = refs[2][...] + y
        o_ref[...] = y.astype(o_ref.dtype)

    row = pl.BlockSpec((tb, W), lambda i: (i, 0))
    ins = [x, gain] + ([residual] if has_res else [])
    return _call(
        body, name, (T // tb,),
        [pl.BlockSpec((tb, W), lambda i: (i, col_block)),
         pl.BlockSpec((1, W), lambda i: (0, 0))] + ([row] if has_res else []),
        [row], [jax.ShapeDtypeStruct((T, W), out_dtype)], ins, sem=("parallel",), comm=comm)[0]


def _rms_bwd(dy, x, gain, out_dtype, name, width=None, col_block=0, residual=None, tb=256, comm=None):
    T = dy.shape[0]
    W = x.shape[1] if width is None else width
    tb = _tile(T, tb)
    has_res = residual is not None

    def body(*refs):
        dy_ref, x_ref, g_ref = refs[:3]
        dx_ref, dg_ref = refs[-2:]
        i = pl.program_id(0)
        xf = x_ref[...]
        r = lax.rsqrt(jnp.mean(xf * xf, axis=-1, keepdims=True) + NORM_EPS)
        xn = xf * r
        dyf = dy_ref[...].astype(F32)
        dyg = dyf * g_ref[...]
        dx = r * (dyg - xn * jnp.mean(dyg * xn, axis=-1, keepdims=True))
        if has_res:
            dx = refs[3][...] + dx
        dx_ref[...] = dx.astype(dx_ref.dtype)

        @pl.when(i == 0)
        def _():
            dg_ref[...] = jnp.zeros_like(dg_ref)

        dg_ref[...] += jnp.sum(dyf * xn, axis=0, keepdims=True)

    row = pl.BlockSpec((tb, W), lambda i: (i, 0))
    vec = pl.BlockSpec((1, W), lambda i: (0, 0))
    ins = [dy, x, gain] + ([residual] if has_res else [])
    return _call(
        body, name, (T // tb,),
        [row, pl.BlockSpec((tb, W), lambda i: (i, col_block)), vec] + ([row] if has_res else []), [row, vec],
        [jax.ShapeDtypeStruct((T, W), out_dtype), jax.ShapeDtypeStruct((1, W), F32)], ins,
        sem=("arbitrary",), comm=comm)


def _loss_head(x2, y2, gain, target, name, tb=256):
    T, D = x2.shape
    tb = _tile(T, tb)

    def body(x2_ref, y2_ref, g_ref, t_ref, dx3_ref, loss_ref):
        i = pl.program_id(0)
        yf = y2_ref[...]
        x3 = x2_ref[...] + yf * lax.rsqrt(jnp.mean(yf * yf, axis=-1, keepdims=True) + NORM_EPS) * g_ref[...]
        e = x3 - t_ref[...]
        dx3_ref[...] = e * (1.0 / D)

        @pl.when(i == 0)
        def _():
            loss_ref[...] = jnp.zeros_like(loss_ref)

        per_row = jnp.mean(e * e, axis=-1, keepdims=True)
        loss_ref[...] += 0.5 * jnp.sum(per_row, axis=0, keepdims=True)

    row = pl.BlockSpec((tb, D), lambda i: (i, 0))
    return pl.pallas_call(
        body, name=name, grid=(T // tb,),
        in_specs=[row, row, pl.BlockSpec((1, D), lambda i: (0, 0)), row],
        out_specs=[row, pl.BlockSpec((1, 1), lambda i: (0, 0))],
        out_shape=[jax.ShapeDtypeStruct((T, D), F32), jax.ShapeDtypeStruct((1, 1), F32)],
        compiler_params=_params("arbitrary"),
    )(x2, y2, gain, target)


def _rope_tables(positions, rot_dim):
    half = rot_dim // 2
    inv_freq = ROPE_THETA ** (-jnp.arange(0, rot_dim, 2, dtype=F32) / rot_dim)
    ang = positions.astype(F32)[:, None] * inv_freq[None, :]
    cos, sin = jnp.cos(ang), jnp.sin(ang)
    T = positions.shape[0]
    ones = jnp.ones((T, HD - rot_dim), F32)
    cos_t = jnp.concatenate([cos, cos, ones], axis=1)
    sin_t = jnp.concatenate([-sin, sin, jnp.zeros_like(ones)], axis=1)
    return cos_t, sin_t


def _rotate(x, cos_t, sin_t, half):
    lane = lax.broadcasted_iota(jnp.int32, x.shape, 1)
    swapped = jnp.where(lane < half, pltpu.roll(x, HD - half, 1), pltpu.roll(x, half, 1))
    return x * cos_t + swapped * sin_t


def _rope_apply(x, cos_t, sin_t, half, n_blocks, is_rope, out_dtype, name, window=0, tb=256):
    T = x.shape[0]
    tb = _tile(T, tb)
    W = n_blocks * HD

    def body(x_ref, c_ref, s_ref, o_ref):
        for j in range(n_blocks):
            sl = slice(j * HD, (j + 1) * HD)
            xj = x_ref[:, sl]
            if is_rope(j):
                xj = _rotate(xj.astype(F32), c_ref[...], s_ref[...], half)
            o_ref[:, sl] = xj.astype(o_ref.dtype)

    tab = pl.BlockSpec((tb, HD), lambda i: (i, 0))
    return pl.pallas_call(
        body, name=name, grid=(T // tb,),
        in_specs=[pl.BlockSpec((tb, W), lambda i: (i, window)), tab, tab],
        out_specs=pl.BlockSpec((tb, W), lambda i: (i, 0)),
        out_shape=jax.ShapeDtypeStruct((T, W), out_dtype),
        compiler_params=_params("parallel"),
    )(x, cos_t, sin_t)


DSWA_TB = 2048


def _deinterleave(src, dst_ref, d, dtype):
    rows = src.shape[0] // d
    for r in range(d):
        dst_ref[r] = src[pl.ds(r, rows, stride=d), :].astype(dtype)


def _rope_dswa(proj, cos_t, sin_t, name):
    T = proj.shape[0]
    tb = _tile(T, DSWA_TB)
    half = ROT_A // 2

    def body(x_ref, c_ref, s_ref, *rest):
        outs, scr = rest[:-1], rest[-1]
        j = pl.program_id(1)

        @pl.when(j < 2 * HEADS)
        def _():
            scr[...] = _rotate(x_ref[...], c_ref[...], s_ref[...], half)

        @pl.when(j >= 2 * HEADS)
        def _():
            scr[...] = x_ref[...]

        for o_ref, d in zip(outs, DSWA_DILATIONS):
            _deinterleave(scr, o_ref, d, BF16)

    blk = pl.BlockSpec((tb, HD), lambda i, j: (i, j))
    tab = pl.BlockSpec((tb, HD), lambda i, j: (i, 0))
    return pl.pallas_call(
        body, name=name, grid=(T // tb, 3 * HEADS),
        in_specs=[blk, tab, tab],
        out_specs=[pl.BlockSpec((d, tb // d, HD), lambda i, j: (0, i, j)) for d in DSWA_DILATIONS],
        out_shape=[jax.ShapeDtypeStruct((d, T // d, 3 * AW), BF16) for d in DSWA_DILATIONS],
        scratch_shapes=[pltpu.VMEM((tb, HD), F32)],
        compiler_params=_params("parallel", "parallel"),
    )(proj, cos_t, sin_t)


def _shared_key_grad(dk, cos_t, sin_t_neg, half, name, tb=512):
    T = dk.shape[0]
    tb = _tile(T, tb)

    def body(d_ref, c_ref, s_ref, o_ref):
        tot = d_ref[:, HD:2 * HD]
        for h in range(1, HEADS):
            tot = tot + d_ref[:, h * QK + HD:(h + 1) * QK]
        o_ref[...] = _rotate(tot, c_ref[...], s_ref[...], half).astype(o_ref.dtype)

    tab = pl.BlockSpec((tb, HD), lambda i: (i, 0))
    return pl.pallas_call(
        body, name=name, grid=(T // tb,),
        in_specs=[pl.BlockSpec((tb, HEADS * QK), lambda i: (i, 0)), tab, tab],
        out_specs=tab, out_shape=jax.ShapeDtypeStruct((T, HD), BF16),
        compiler_params=_params("parallel"),
    )(dk, cos_t, sin_t_neg)


def _band_mask(n):
    row = lax.broadcasted_iota(jnp.int32, (QBLK, 2 * QBLK), 0)
    col = lax.broadcasted_iota(jnp.int32, (QBLK, 2 * QBLK), 1)
    in_prev = jnp.logical_and(jnp.logical_and(col < QBLK, col >= row), n > 0)
    in_cur = jnp.logical_and(col >= QBLK, col - QBLK <= row)
    return jnp.logical_or(in_prev, in_cur)


def _dswa_specs(nb, reverse=False):
    pos = (lambda n: nb - 1 - n) if reverse else (lambda n: n)
    cur = lambda c: pl.BlockSpec((None, QBLK, AW), lambda r, n: (r, pos(n), c))
    prev = lambda c: pl.BlockSpec((None, QBLK, AW), lambda r, n: (r, jnp.maximum(pos(n) - 1, 0), c))
    return cur, prev


def _relayout_spec(d, tb):
    return pl.BlockSpec((d, tb // d, HD), lambda i, h: (0, i, h))


def _dswa_fwd(qkv, name, comm=None):
    d, sd = qkv.shape[:2]
    nb = sd // QBLK

    def body(q_ref, kc_ref, kp_ref, vc_ref, vp_ref, o_ref, l_ref):
        mask = _band_mask(pl.program_id(1))
        for h in range(HEADS):
            sl = slice(h * HD, (h + 1) * HD)
            keys = jnp.concatenate([kp_ref[:, sl], kc_ref[:, sl]], axis=0)
            vals = jnp.concatenate([vp_ref[:, sl], vc_ref[:, sl]], axis=0)
            s = jnp.where(mask, _dot(q_ref[:, sl], keys, NT) * SCALE_A, NEG_INF)
            m = jnp.max(s, axis=-1, keepdims=True)
            p = jnp.exp(s - m)
            den = jnp.sum(p, axis=-1, keepdims=True)
            o_ref[:, sl] = _dot((p / den).astype(BF16), vals, NN)
            l_ref[:, sl] = jnp.broadcast_to(m + jnp.log(den), (QBLK, HD))

    cur, prev = _dswa_specs(nb)
    return _call(
        body, name, (d, nb), [cur(0), cur(1), prev(1), cur(2), prev(2)], [cur(0), cur(0)],
        [jax.ShapeDtypeStruct((d, sd, AW), F32)] * 2, (qkv, qkv, qkv, qkv, qkv),
        sem=("parallel", "parallel"), comm=comm)


def _dswa_merge(outs, lses, name):
    nc = len(DSWA_DILATIONS)
    T = outs[0].shape[0] * outs[0].shape[1]
    tb = _tile(T, DSWA_TB)

    def body(*refs):
        o_refs, l_refs = refs[:nc], refs[nc:2 * nc]
        out_ref, outb_ref = refs[2 * nc:2 * nc + 2]
        lt_refs = refs[2 * nc + 2:3 * nc + 2]
        o_nat, l_nat, lt_nat = refs[3 * nc + 2:4 * nc + 2], refs[4 * nc + 2:5 * nc + 2], refs[-1]
        for c, d in enumerate(DSWA_DILATIONS):
            for r in range(d):
                o_nat[c][pl.ds(r, tb // d, stride=d), :] = o_refs[c][r]
                l_nat[c][pl.ds(r, tb // d, stride=d), :] = l_refs[c][r]
        ls = [l[...] for l in l_nat]
        m = functools.reduce(jnp.maximum, ls)
        es = [jnp.exp(l - m) for l in ls]
        tot = functools.reduce(lambda a, b: a + b, es)
        acc = (es[0] / tot) * o_nat[0][...]
        for c in range(1, nc):
            acc = acc + (es[c] / tot) * o_nat[c][...]
        out_ref[...] = acc
        outb_ref[...] = acc.astype(BF16)
        lt_nat[...] = m + jnp.log(tot)
        for c, d in enumerate(DSWA_DILATIONS):
            _deinterleave(lt_nat, lt_refs[c], d, F32)

    nat = pl.BlockSpec((tb, HD), lambda i, h: (i, h))
    by_d = [_relayout_spec(d, tb) for d in DSWA_DILATIONS]
    res = pl.pallas_call(
        body, name=name, grid=(T // tb, HEADS),
        in_specs=by_d + by_d, out_specs=[nat, nat] + by_d,
        out_shape=[jax.ShapeDtypeStruct((T, AW), F32), jax.ShapeDtypeStruct((T, AW), BF16)]
        + [jax.ShapeDtypeStruct((d, T // d, AW), F32) for d in DSWA_DILATIONS],
        scratch_shapes=[pltpu.VMEM((tb, HD), F32)] * (2 * nc + 1),
        compiler_params=_params("parallel", "parallel"),
    )(*outs, *lses)
    return res[0], res[1], res[2:]


def _dswa_delta(dout, out, name):
    nc = len(DSWA_DILATIONS)
    T = out.shape[0]
    tb = _tile(T, DSWA_TB)

    def body(do_ref, o_ref, *rest):
        dl_refs, dob_refs, dl_nat = rest[:nc], rest[nc:2 * nc], rest[-1]
        dl_nat[...] = jnp.broadcast_to(jnp.sum(do_ref[...] * o_ref[...], axis=-1, keepdims=True), (tb, HD))
        for c, d in enumerate(DSWA_DILATIONS):
            _deinterleave(dl_nat, dl_refs[c], d, F32)
            _deinterleave(do_ref, dob_refs[c], d, BF16)

    nat = pl.BlockSpec((tb, HD), lambda i, h: (i, h))
    by_d = [_relayout_spec(d, tb) for d in DSWA_DILATIONS]
    res = pl.pallas_call(
        body, name=name, grid=(T // tb, HEADS),
        in_specs=[nat, nat], out_specs=by_d + by_d,
        out_shape=[jax.ShapeDtypeStruct((d, T // d, AW), F32) for d in DSWA_DILATIONS]
        + [jax.ShapeDtypeStruct((d, T // d, AW), BF16) for d in DSWA_DILATIONS],
        scratch_shapes=[pltpu.VMEM((tb, HD), F32)],
        compiler_params=_params("parallel", "parallel"),
    )(dout, out)
    return res[:nc], res[nc:]


def _delta_prep(dout, col_block, out, name, tb=256):
    T = out.shape[0]
    tb = _tile(T, tb)

    def body(do_ref, o_ref, delta_ref, dob_ref):
        for h in range(HEADS):
            sl = slice(h * HD, (h + 1) * HD)
            doh = do_ref[:, sl]
            delta_ref[:, sl] = jnp.broadcast_to(jnp.sum(doh * o_ref[:, sl], axis=-1, keepdims=True), (tb, HD))
            dob_ref[:, sl] = doh.astype(BF16)

    row = pl.BlockSpec((tb, AW), lambda i: (i, 0))
    return pl.pallas_call(
        body, name=name, grid=(T // tb,),
        in_specs=[pl.BlockSpec((tb, AW), lambda i: (i, col_block)), row], out_specs=[row, row],
        out_shape=[jax.ShapeDtypeStruct((T, AW), F32), jax.ShapeDtypeStruct((T, AW), BF16)],
        compiler_params=_params("parallel"),
    )(dout, out)


def _dswa_bwd(qkv, dout_b, lse_tot, delta, name, comm=None):
    d, sd = qkv.shape[:2]
    nb = sd // QBLK

    def body(q_ref, kc_ref, kp_ref, vc_ref, vp_ref, do_ref, l_ref, dl_ref, dq_ref, dk_ref, dv_ref, carry_k, carry_v):
        mask = _band_mask(nb - 1 - pl.program_id(1))

        @pl.when(pl.program_id(1) == 0)
        def _():
            carry_k[...] = jnp.zeros_like(carry_k)
            carry_v[...] = jnp.zeros_like(carry_v)

        for h in range(HEADS):
            sl = slice(h * HD, (h + 1) * HD)
            qh, doh = q_ref[:, sl], do_ref[:, sl]
            keys = jnp.concatenate([kp_ref[:, sl], kc_ref[:, sl]], axis=0)
            vals = jnp.concatenate([vp_ref[:, sl], vc_ref[:, sl]], axis=0)
            s = jnp.where(mask, _dot(qh, keys, NT) * SCALE_A, NEG_INF)
            p = jnp.exp(s - l_ref[:, sl][:, 0:1])
            ds = (p * (_dot(doh, vals, NT) - dl_ref[:, sl][:, 0:1]) * SCALE_A).astype(BF16)
            dq_ref[:, sl] = _dot(ds, keys, NN)
            dk = _dot(ds, qh, TN)
            dv = _dot(p.astype(BF16), doh, TN)
            dk_ref[:, sl] = dk[QBLK:] + carry_k[:, sl]
            dv_ref[:, sl] = dv[QBLK:] + carry_v[:, sl]
            carry_k[:, sl] = dk[:QBLK]
            carry_v[:, sl] = dv[:QBLK]

    cur, prev = _dswa_specs(nb, reverse=True)
    return _call(
        body, name, (d, nb), [cur(0), cur(1), prev(1), cur(2), prev(2), cur(0), cur(0), cur(0)], [cur(0)] * 3,
        [jax.ShapeDtypeStruct((d, sd, AW), F32)] * 3, (qkv, qkv, qkv, qkv, qkv, dout_b, lse_tot, delta),
        scratch=[pltpu.VMEM((QBLK, AW), F32)] * 2, sem=("parallel", "arbitrary"), comm=comm)


def _dswa_combine(grads, cos_t, sin_t_neg, name):
    nc = len(DSWA_DILATIONS)
    T = grads[0][0].shape[0] * grads[0][0].shape[1]
    tb = _tile(T, DSWA_TB)
    half = ROT_A // 2

    def body(*refs):
        g_refs = refs[:3 * nc]
        c_ref, s_ref = refs[3 * nc:3 * nc + 2]
        outs, accs = refs[3 * nc + 2:3 * nc + 5], refs[3 * nc + 5:]
        for which in range(3):
            acc = accs[which]
            for c, d in enumerate(DSWA_DILATIONS):
                g = g_refs[3 * c + which]
                for r in range(d):
                    if c == 0:
                        acc[...] = g[r]
                    else:
                        acc[pl.ds(r, tb // d, stride=d), :] += g[r]
            val = acc[...]
            if which < 2:
                val = _rotate(val, c_ref[...], s_ref[...], half)
            outs[which][...] = val.astype(BF16)

    nat = pl.BlockSpec((tb, HD), lambda i, h: (i, h))
    tab = pl.BlockSpec((tb, HD), lambda i, h: (i, 0))
    in_specs, ins = [], []
    for d, g in zip(DSWA_DILATIONS, grads):
        in_specs += [_relayout_spec(d, tb)] * 3
        ins += list(g)
    return pl.pallas_call(
        body, name=name, grid=(T // tb, HEADS),
        in_specs=in_specs + [tab, tab], out_specs=[nat] * 3,
        out_shape=[jax.ShapeDtypeStruct((T, AW), BF16)] * 3,
        scratch_shapes=[pltpu.VMEM((tb, HD), F32)] * 3,
        compiler_params=_params("parallel", "parallel"),
    )(*ins, cos_t, sin_t_neg)


MLA_TQ = 512
QK = 2 * HD


def _triangle(nq, key_major):
    pairs = [(q, k) for q in range(nq) for k in range(q + 1)]
    if key_major:
        pairs.sort(key=lambda p: (p[1], p[0]))
    return (jnp.array([p[0] for p in pairs], jnp.int32), jnp.array([p[1] for p in pairs], jnp.int32))


def _mla_specs(tq):
    q_spec = pl.BlockSpec((tq, HEADS * QK), lambda t, qi, ki: (qi[t], 0))
    k_spec = pl.BlockSpec((tq, HEADS * QK), lambda t, qi, ki: (ki[t], 0))
    v_spec = pl.BlockSpec((tq, AW), lambda t, qi, ki: (ki[t], 1))
    qrow = pl.BlockSpec((tq, AW), lambda t, qi, ki: (qi[t], 0))
    krow = pl.BlockSpec((tq, AW), lambda t, qi, ki: (ki[t], 0))
    return q_spec, k_spec, v_spec, qrow, krow


def _mla_scores(q_ref, k_ref, h, qi, ki, tq):
    s = _dot(q_ref[:, h * QK:(h + 1) * QK], k_ref[:, h * QK:(h + 1) * QK], NT) * SCALE_B
    row = lax.broadcasted_iota(jnp.int32, s.shape, 0) + qi * tq
    col = lax.broadcasted_iota(jnp.int32, s.shape, 1) + ki * tq
    return jnp.where(col <= row, s, NEG_INF)


def _mla_fwd(q, k, v1, name, comm=None):
    T = q.shape[0]
    tq = _tile(T, MLA_TQ)
    tables = _triangle(T // tq, False)

    def body(qi_ref, ki_ref, q_ref, k_ref, v_ref, o_ref, ob_ref, l_ref, m_s, acc):
        t = pl.program_id(0)
        qi, ki = qi_ref[t], ki_ref[t]

        @pl.when(ki == 0)
        def _():
            m_s[...] = jnp.full_like(m_s, NEG_INF)
            acc[...] = jnp.zeros_like(acc)

        updates = []
        for h in range(HEADS):
            s = _mla_scores(q_ref, k_ref, h, qi, ki, tq)
            m_new = jnp.maximum(m_s[h], jnp.max(s, axis=-1, keepdims=True))
            p = jnp.exp(s - m_new).astype(BF16)
            updates.append((m_new, jnp.exp(m_s[h] - m_new), _dot(p, v_ref[:, h * QK:(h + 1) * QK], NN)))
        for h, (m_new, alpha, pv) in enumerate(updates):
            acc[:, h * QK:(h + 1) * QK] = alpha * acc[:, h * QK:(h + 1) * QK] + pv
            m_s[h] = m_new

        @pl.when(ki == qi)
        def _():
            for h in range(HEADS):
                sl = slice(h * HD, (h + 1) * HD)
                den = acc[:, h * QK + HD:h * QK + HD + 1]
                out = acc[:, h * QK:h * QK + HD] / den
                o_ref[:, sl] = out
                ob_ref[:, sl] = out.astype(BF16)
                l_ref[:, sl] = jnp.broadcast_to(m_s[h] + jnp.log(den), (tq, HD))

    q_spec, k_spec, _, qrow, _ = _mla_specs(tq)
    return _call(
        body, name, (tables[0].shape[0],), [q_spec, k_spec, k_spec], [qrow, qrow, qrow],
        [jax.ShapeDtypeStruct((T, AW), F32), jax.ShapeDtypeStruct((T, AW), BF16), jax.ShapeDtypeStruct((T, AW), F32)],
        (q, k, v1),
        scratch=[pltpu.VMEM((HEADS, tq, 1), F32), pltpu.VMEM((tq, HEADS * QK), F32)],
        sem=("arbitrary",), comm=comm, prefetch=tables)


def _mla_ds(q_ref, k_ref, v_ref, do_ref, l_ref, dl_ref, h, qi, ki, tq):
    sl = slice(h * HD, (h + 1) * HD)
    p = jnp.exp(_mla_scores(q_ref, k_ref, h, qi, ki, tq) - l_ref[:, sl][:, 0:1])
    ds = (p * (_dot(do_ref[:, sl], v_ref[:, sl], NT) - dl_ref[:, sl][:, 0:1]) * SCALE_B).astype(BF16)
    return p, ds


def _mla_bwd_q(q, k, kv, dout_b, lse, delta, name, comm=None):
    T = q.shape[0]
    tq = _tile(T, MLA_TQ)
    tables = _triangle(T // tq, False)

    def body(qi_ref, ki_ref, q_ref, k_ref, v_ref, do_ref, l_ref, dl_ref, dq_ref):
        t = pl.program_id(0)
        qi, ki = qi_ref[t], ki_ref[t]

        @pl.when(ki == 0)
        def _():
            dq_ref[...] = jnp.zeros_like(dq_ref)

        for h in range(HEADS):
            _, ds = _mla_ds(q_ref, k_ref, v_ref, do_ref, l_ref, dl_ref, h, qi, ki, tq)
            dq_ref[:, h * QK:(h + 1) * QK] += _dot(ds, k_ref[:, h * QK:(h + 1) * QK], NN)

    q_spec, k_spec, v_spec, qrow, _ = _mla_specs(tq)
    return _call(
        body, name, (tables[0].shape[0],), [q_spec, k_spec, v_spec, qrow, qrow, qrow], [q_spec],
        [jax.ShapeDtypeStruct((T, HEADS * QK), F32)], (q, k, kv, dout_b, lse, delta),
        sem=("arbitrary",), comm=comm, prefetch=tables)[0]


def _mla_bwd_kv(q, k, kv, dout_b, lse, delta, name, comm=None):
    T = q.shape[0]
    tq = _tile(T, MLA_TQ)
    tables = _triangle(T // tq, True)

    def body(qi_ref, ki_ref, q_ref, k_ref, v_ref, do_ref, l_ref, dl_ref, dk_ref, dv_ref):
        t = pl.program_id(0)
        qi, ki = qi_ref[t], ki_ref[t]

        @pl.when(qi == ki)
        def _():
            dk_ref[...] = jnp.zeros_like(dk_ref)
            dv_ref[...] = jnp.zeros_like(dv_ref)

        for h in range(HEADS):
            sl = slice(h * HD, (h + 1) * HD)
            p, ds = _mla_ds(q_ref, k_ref, v_ref, do_ref, l_ref, dl_ref, h, qi, ki, tq)
            dv_ref[:, sl] += _dot(p.astype(BF16), do_ref[:, sl], TN)
            dk_ref[:, h * QK:(h + 1) * QK] += _dot(ds, q_ref[:, h * QK:(h + 1) * QK], TN)

    q_spec, k_spec, v_spec, qrow, krow = _mla_specs(tq)
    return _call(
        body, name, (tables[0].shape[0],), [q_spec, k_spec, v_spec, qrow, qrow, qrow], [k_spec, krow],
        [jax.ShapeDtypeStruct((T, HEADS * QK), F32), jax.ShapeDtypeStruct((T, AW), F32)],
        (q, k, kv, dout_b, lse, delta), sem=("arbitrary",), comm=comm, prefetch=tables)


def _pair_sum(by_device, from_sibling, name, tb=256):
    n_chip, R, C = from_sibling.shape
    tb = _tile(R, tb)
    core = jnp.reshape(lax.axis_index("c"), (1,)).astype(jnp.int32)

    def body(core_ref, mine_ref, theirs_ref, o_ref):
        o_ref[...] = (mine_ref[...].astype(F32) + theirs_ref[...].astype(F32)).astype(o_ref.dtype)

    blk = pl.BlockSpec((None, tb, C), lambda p, i, core_ref: (p, i, 0))
    return _call(
        body, name, (n_chip, R // tb),
        [pl.BlockSpec((None, tb, C), lambda p, i, core_ref: (2 * p + core_ref[0], i, 0)), blk], [blk],
        [jax.ShapeDtypeStruct((n_chip, R, C), BF16)], (by_device, from_sibling),
        sem=("parallel", "parallel"), prefetch=(core,))[0]


def _adamw(parts, w, m, v, name, tb=128, comm=None):
    R, C = w.shape
    n_parts = parts.shape[0]
    tb = _tile(R, tb)
    c1 = 1.0 - ADAM_B1
    c2 = 1.0 - ADAM_B2
    bc1 = 1.0 - ADAM_B1 ** ADAM_STEP
    bc2 = 1.0 - ADAM_B2 ** ADAM_STEP

    def body(p_ref, w_ref, m_ref, v_ref, g_ref, d_ref, nm_ref, nv_ref):
        g = p_ref[0].astype(F32)
        for j in range(1, n_parts):
            g = g + p_ref[j].astype(F32)
        nm = ADAM_B1 * m_ref[...] + c1 * g
        nv = ADAM_B2 * v_ref[...] + c2 * (g * g)
        g_ref[...] = g
        nm_ref[...] = nm
        nv_ref[...] = nv
        d_ref[...] = -ADAM_LR * ((nm / bc1) / (jnp.sqrt(nv / bc2) + ADAM_EPS) + ADAM_WD * w_ref[...])

    row = pl.BlockSpec((tb, C), lambda i: (i, 0))
    return _call(
        body, name, (R // tb,), [pl.BlockSpec((n_parts, tb, C), lambda i: (0, i, 0)), row, row, row], [row] * 4,
        [jax.ShapeDtypeStruct((R, C), F32)] * 4, (parts, w, m, v), sem=("parallel",), comm=comm)


def _cols_from_shards(g):
    return jnp.transpose(g, (1, 0, 2)).reshape(g.shape[1], N_DEV * g.shape[2])


def _cols_to_shards(w):
    return jnp.transpose(w.reshape(w.shape[0], N_DEV, w.shape[1] // N_DEV), (1, 0, 2))


def _split_heads(w, first):
    w3 = w.reshape(w.shape[0], HEADS, -1)
    return w3[:, :, :first].reshape(w.shape[0], -1), w3[:, :, first:].reshape(w.shape[0], -1)


def _join_heads(a, b):
    R = a.shape[0]
    return jnp.concatenate([a.reshape(R, HEADS, -1), b.reshape(R, HEADS, -1)], axis=2).reshape(R, -1)


def _pad_heads(w, width):
    w3 = w.reshape(w.shape[0], HEADS, -1)
    return jnp.pad(w3, ((0, 0), (0, 0), (0, width - w3.shape[2]))).reshape(w.shape[0], HEADS * width)


def _unpad_heads(w, k):
    return w.reshape(w.shape[0], HEADS, -1)[:, :, :k].reshape(w.shape[0], HEADS * k)


def kernel(x, positions, norm_attn_pre, norm_attn_post, w_in, q_latent_norm, kv_latent_norm, w_uq, w_ukv, w_out, norm_mlp_pre, norm_mlp_post, w_up, w_down, loss_target, m_norm_attn_pre, m_norm_attn_post, m_w_in, m_q_latent_norm, m_kv_latent_norm, m_w_uq, m_w_ukv, m_w_out, m_norm_mlp_pre, m_norm_mlp_post, m_w_up, m_w_down, v_norm_attn_pre, v_norm_attn_post, v_w_in, v_q_latent_norm, v_kv_latent_norm, v_w_uq, v_w_ukv, v_w_out, v_norm_mlp_pre, v_norm_mlp_post, v_w_up, v_w_down):
    xs = x[0]
    tgt = loss_target[0]
    pos = positions[0]
    T, D = xs.shape
    big = dict(w_in=(w_in, m_w_in, v_w_in), w_uq=(w_uq, m_w_uq, v_w_uq), w_ukv=(w_ukv, m_w_ukv, v_w_ukv),
               w_out=(w_out, m_w_out, v_w_out), w_up=(w_up, m_w_up, v_w_up), w_down=(w_down, m_w_down, v_w_down))
    big = {n: tuple(t[0] for t in ts) for n, ts in big.items()}
    big_names = ["w_in", "w_uq", "w_ukv", "w_out", "w_up", "w_down"]
    col_sharded = {"w_in", "w_uq", "w_ukv", "w_up"}

    wb = {n: big[n][0].astype(BF16) for n in big_names}

    def gathered(ex, i, n):
        g = ex.results[i]
        return _cols_from_shards(g) if n in col_sharded else g.reshape(-1, g.shape[2])

    def by_device(g, n):
        return _cols_to_shards(g) if n in col_sharded else g.reshape(N_DEV, g.shape[0] // N_DEV, g.shape[1])

    def scatter_of(g, n):
        return _Exchange([by_device(g, n)], "scatter")

    cos_a, sin_a = _rope_tables(pos, ROT_A)
    cos_b, sin_b = _rope_tables(pos, ROPE_MLA)

    ex_in = _Exchange([wb["w_in"]], "gather")
    h1 = _rms_fwd(xs, norm_attn_pre, BF16, "norm_attn_pre_fwd", comm=ex_in)
    Wi = jnp.pad(gathered(ex_in, 0, "w_in"), ((0, 0), (0, IN_PAD - IN_COLS)))
    ex_mid = _Exchange([wb["w_uq"], wb["w_ukv"], wb["w_out"]], "gather")
    proj = _mm(h1, Wi, "nn", [F32], "proj_in", tn=1408, comm=ex_mid)
    Wuq = _pad_heads(gathered(ex_mid, 0, "w_uq"), QK)
    Wukv = jnp.concatenate(_split_heads(gathered(ex_mid, 1, "w_ukv"), HD), axis=1)
    Wo = gathered(ex_mid, 2, "w_out")
    qkv_by_d = _rope_dswa(proj, cos_a, sin_a, "rope_dswa")
    n_quarter = wb["w_down"].shape[0] // 4
    ex_down = [_Exchange([wb["w_down"][i * n_quarter:(i + 1) * n_quarter]], "gather") for i in range(4)]
    outs, lses = [], []
    for d, qkv, ex in zip(DSWA_DILATIONS, qkv_by_d, ex_down):
        o, l = _dswa_fwd(qkv, f"dswa_fwd_d{d}", comm=ex)
        outs.append(o)
        lses.append(l)
    a_out, a_out_b, a_lse_by_d = _dswa_merge(outs, lses, "dswa_merge")

    cqn = _rms_fwd(proj, q_latent_norm, BF16, "q_latent_norm_fwd", width=Q_LORA, col_block=3 * AW // Q_LORA)
    ckvn = _rms_fwd(proj, kv_latent_norm, BF16, "kv_latent_norm_fwd", width=KV_LORA, col_block=3 * AW // KV_LORA + 1)
    qb = _mm(cqn, Wuq, "nn", [F32], "q_up")
    kvb = _mm(ckvn, Wukv, "nn", [BF16], "kv_up")
    odd = lambda j: j % 2 == 1
    q_mla = _rope_apply(qb, cos_b, sin_b, ROPE_MLA // 2, 2 * HEADS, odd, BF16, "rope_mla_q")
    kr = _rope_apply(proj, cos_b, sin_b, ROPE_MLA // 2, 1, lambda j: True, BF16, "rope_mla_k",
                     window=(IN_PAD - HD) // HD)
    k_mla = jnp.concatenate([kvb[:, :AW].reshape(T, HEADS, HD), jnp.broadcast_to(kr[:, None, :], (T, HEADS, HD))],
                            axis=2).reshape(T, HEADS * QK)
    v1_mla = jnp.concatenate([kvb[:, AW:].reshape(T, HEADS, HD), jnp.ones((T, HEADS, 1), BF16),
                              jnp.zeros((T, HEADS, HD - 1), BF16)], axis=2).reshape(T, HEADS * QK)
    ex_up = _Exchange([wb["w_up"]], "gather")
    b_out, b_out_b, b_lse = _mla_fwd(q_mla, k_mla, v1_mla, "mla_fwd", comm=ex_up)
    Wup_shards = ex_up.results[0]

    mixed = jnp.concatenate([a_out_b, b_out_b], axis=1)
    y1 = _mm(mixed, Wo, "nn", [F32], "attn_out")
    x2 = _rms_fwd(y1, norm_attn_post, F32, "norm_attn_post_fwd", residual=xs)

    h2 = _rms_fwd(x2, norm_mlp_pre, BF16, "norm_mlp_pre_fwd")

    def relu2(z):
        r = jnp.maximum(z, 0.0)
        return r * r, r

    u, zr = _mm(h2, Wup_shards, "nn", [BF16, BF16], "mlp_up", epilogue=relu2, comm=ex_down[3],
                b_shards=True)
    Wdn = jnp.concatenate([ex.results[0] for ex in ex_down], axis=1).reshape(-1, D)
    y2 = _mm(u, Wdn, "nn", [F32], "mlp_down")
    dx3, loss_part = _loss_head(x2, y2, norm_mlp_post, tgt, "loss_head")

    dy2, dg_mlp_post = _rms_bwd(dx3, y2, norm_mlp_post, BF16, "norm_mlp_post_bwd")
    dz = _mm(dy2, Wdn, "nt", [BF16], "mlp_down_dx", epilogue=lambda du, r: (du * (2.0 * r.astype(F32)),), extras=(zr,))
    g_down = _mm(u, dy2, "tn", [BF16], "mlp_down_dw")
    down_dev = by_device(g_down, "w_down")
    pair_down = _Exchange([down_dev], "pair")
    up_dev = _mm(h2, dz, "tn", [BF16], "mlp_up_dw", comm=pair_down, out_shards=True)
    down_chip = _pair_sum(down_dev, pair_down.results[0], "pair_sum_w_down")
    pair_up = _Exchange([up_dev], "pair")
    dh2 = _mm(dz, Wup_shards, "nt", [F32], "mlp_up_dx", comm=pair_up, b_shards=True)
    up_chip = _pair_sum(up_dev, pair_up.results[0], "pair_sum_w_up")
    dx2, dg_mlp_pre = _rms_bwd(dh2, x2, norm_mlp_pre, F32, "norm_mlp_pre_bwd", residual=dx3)

    dy1, dg_attn_post = _rms_bwd(dx2, y1, norm_attn_post, BF16, "norm_attn_post_bwd")
    dmixed = _mm(dy1, Wo, "nt", [F32], "attn_out_dx")
    g_out = _mm(mixed, dy1, "tn", [BF16], "attn_out_dw")

    b_delta, b_dout = _delta_prep(dmixed, 1, b_out, "mla_delta")
    sc_down = _Exchange([down_chip], "chips")
    dq_mla = _mla_bwd_q(q_mla, k_mla, kvb, b_dout, b_lse, b_delta, "mla_bwd_q", comm=sc_down)
    sc_up = _Exchange([up_chip], "chips")
    dk_mla, dvb = _mla_bwd_kv(q_mla, k_mla, kvb, b_dout, b_lse, b_delta, "mla_bwd_kv", comm=sc_up)
    dqb = _rope_apply(dq_mla, cos_b, -sin_b, ROPE_MLA // 2, 2 * HEADS, odd, BF16, "rope_mla_q_bwd")
    dkn = dk_mla.reshape(T, HEADS, QK)[:, :, :HD].reshape(T, AW)
    dkvb = jnp.concatenate([dkn, dvb], axis=1).astype(BF16)
    d_kr = _shared_key_grad(dk_mla, cos_b, -sin_b, ROPE_MLA // 2, "rope_mla_k_bwd")
    g_uq_pad = _mm(cqn, dqb, "tn", [BF16], "q_up_dw")
    g_ukv_perm = _mm(ckvn, dkvb, "tn", [BF16], "kv_up_dw")
    dcqn = _mm(dqb, Wuq, "nt", [F32], "q_up_dx")
    dckvn = _mm(dkvb, Wukv, "nt", [F32], "kv_up_dx")
    d_cq, dg_q = _rms_bwd(dcqn, proj, q_latent_norm, BF16, "q_latent_norm_bwd", width=Q_LORA, col_block=3 * AW // Q_LORA)
    d_ckv, dg_kv = _rms_bwd(dckvn, proj, kv_latent_norm, BF16, "kv_latent_norm_bwd", width=KV_LORA,
                            col_block=3 * AW // KV_LORA + 1)

    g_uq = _unpad_heads(g_uq_pad, HD + ROPE_MLA)
    g_ukv = _join_heads(g_ukv_perm[:, :AW], g_ukv_perm[:, AW:])
    sc_out = scatter_of(g_out, "w_out")
    sc_uq = _Exchange([_cols_to_shards(g_uq), _cols_to_shards(g_ukv)], "scatter")
    a_delta_by_d, a_dout_by_d = _dswa_delta(dmixed, a_out, "dswa_delta")
    a_grads = [_dswa_bwd(qkv_by_d[c], a_dout_by_d[c], a_lse_by_d[c], a_delta_by_d[c], f"dswa_bwd_d{d}", comm=cm)
               for c, (d, cm) in enumerate(zip(DSWA_DILATIONS, (sc_out, sc_uq, None)))]
    d_aq, d_ak, d_av = _dswa_combine(a_grads, cos_a, -sin_a, "dswa_combine")

    dproj = jnp.concatenate([d_aq, d_ak, d_av, d_cq, d_ckv, d_kr], axis=1)
    g_in_pad = _mm(h1, dproj, "tn", [BF16], "proj_in_dw", tn=1408)
    in_dev = by_device(g_in_pad[:, :IN_COLS], "w_in")
    pair_in = _Exchange([in_dev], "pair")
    dh1 = _mm(dproj, Wi, "nt", [F32], "proj_in_dx", comm=pair_in)
    in_chip = _pair_sum(in_dev, pair_in.results[0], "pair_sum_w_in")
    cuts = [0, 3 * D // 8, 11 * D // 16, D]
    sc_in = [_Exchange([in_chip[:, a:b]], "chips") for a, b in zip(cuts[:-1], cuts[1:])]
    grad_x, dg_attn_pre = _rms_bwd(dh1, xs, norm_attn_pre, F32, "norm_attn_pre_bwd", residual=dx2, comm=sc_in[0])

    big_out = dict(w_down=_adamw(sc_down.results[0], *big["w_down"], "adamw_w_down", comm=sc_in[1]),
                   w_up=_adamw(sc_up.results[0], *big["w_up"], "adamw_w_up", comm=sc_in[2]))
    parts = dict(w_in=jnp.concatenate([sc.results[0] for sc in sc_in], axis=1), w_uq=sc_uq.results[0],
                 w_ukv=sc_uq.results[1], w_out=sc_out.results[0])
    big_out.update({n: _adamw(parts[n], *big[n], f"adamw_{n}") for n in parts})

    gain_names = ["norm_attn_pre", "norm_attn_post", "q_latent_norm", "kv_latent_norm", "norm_mlp_pre", "norm_mlp_post"]
    gain_args = dict(norm_attn_pre=(norm_attn_pre, m_norm_attn_pre, v_norm_attn_pre),
                     norm_attn_post=(norm_attn_post, m_norm_attn_post, v_norm_attn_post),
                     q_latent_norm=(q_latent_norm, m_q_latent_norm, v_q_latent_norm),
                     kv_latent_norm=(kv_latent_norm, m_kv_latent_norm, v_kv_latent_norm),
                     norm_mlp_pre=(norm_mlp_pre, m_norm_mlp_pre, v_norm_mlp_pre),
                     norm_mlp_post=(norm_mlp_post, m_norm_mlp_post, v_norm_mlp_post))
    gain_grads = dict(norm_attn_pre=dg_attn_pre, norm_attn_post=dg_attn_post, q_latent_norm=dg_q,
                      kv_latent_norm=dg_kv, norm_mlp_pre=dg_mlp_pre, norm_mlp_post=dg_mlp_post)
    packed = jnp.concatenate([gain_grads[n] for n in gain_names], axis=1)
    gain_parts = _Exchange([packed], "gather").standalone("gather_gain_grads")[0]
    pack3 = lambda i: jnp.concatenate([gain_args[n][i] for n in gain_names], axis=1)
    gain_out = _adamw(gain_parts, pack3(0), pack3(1), pack3(2), "adamw_gains", tb=1)
    offs = [0]
    for n in gain_names:
        offs.append(offs[-1] + gain_args[n][0].shape[1])
    small_out = {n: tuple(o[:, offs[i]:offs[i + 1]] for o in gain_out) for i, n in enumerate(gain_names)}

    loss = lax.psum(loss_part[0, 0], ("x", "y", "c"))

    order = ["norm_attn_pre", "norm_attn_post", "w_in", "q_latent_norm", "kv_latent_norm", "w_uq", "w_ukv", "w_out",
             "norm_mlp_pre", "norm_mlp_post", "w_up", "w_down"]
    res = {n: (small_out[n] if n in small_out else tuple(o[None] for o in big_out[n])) for n in order}
    return (loss, grad_x[None], *[res[n][0] for n in order], *[res[n][1] for n in order],
            *[res[n][2] for n in order], *[res[n][3] for n in order])
```

```python
import functools
import math

import jax
import jax.numpy as jnp
from jax import lax
from jax.experimental import pallas as pl
from jax.experimental.pallas import tpu as pltpu

F32 = jnp.float32
BF16 = jnp.bfloat16

N_DEV = 8
HEADS = 8
HD = 128
AW = HEADS * HD
Q_LORA = 512
KV_LORA = 512
ROPE_MLA = 64
ROT_A = 32
IN_COLS = 3 * AW + Q_LORA + KV_LORA + ROPE_MLA
IN_PAD = 3 * AW + Q_LORA + KV_LORA + HD
QBLK = 128
DSWA_DILATIONS = (1, 4, 16)
ROPE_THETA = 500000.0
NORM_EPS = 1e-6
NEG_INF = -1e30
SCALE_A = HD ** -0.5
SCALE_B = (HD + ROPE_MLA) ** -0.5

ADAM_LR = 0.001
ADAM_B1 = 0.9
ADAM_B2 = 0.999
ADAM_EPS = 1e-08
ADAM_WD = 0.01
ADAM_STEP = 10

VMEM_LIMIT = 48 * 1024 * 1024

NT = (((1,), (1,)), ((), ()))
NN = (((1,), (0,)), ((), ()))
TN = (((0,), (0,)), ((), ()))


def _dot(a, b, dims):
    return lax.dot_general(a, b, dims, preferred_element_type=F32)


def _params(*sem):
    return pltpu.CompilerParams(dimension_semantics=sem, vmem_limit_bytes=VMEM_LIMIT)


def _tile(n, want):
    t = min(n, want)
    while n % t:
        t //= 2
    return t


def _tile128(n, want):
    if n % 128:
        return n
    units = n // 128
    return 128 * max(u for u in range(1, max(want // 128, 1) + 1) if units % u == 0)


class _Exchange:
    def __init__(self, arrs, mode):
        self.arrs = list(arrs)
        self.mode = mode
        self.n = len(self.arrs)
        self.results = None
        hbm = pl.BlockSpec(memory_space=pltpu.HBM)
        self.specs = [hbm] * self.n
        shape = {"gather": lambda a: (N_DEV,) + a.shape, "scatter": lambda a: a.shape,
                 "pair": lambda a: (4,) + a.shape[1:], "chips": lambda a: a.shape}[mode]
        self.out_shape = [jax.ShapeDtypeStruct(shape(a), a.dtype) for a in self.arrs]
        n_sem = self.n * (N_DEV - 1)
        self.scratch = [pltpu.SemaphoreType.DMA((n_sem,)), pltpu.SemaphoreType.DMA((n_sem,)),
                        pltpu.SemaphoreType.DMA((self.n,))]

    def hooks(self, ins, outs, send_sems, recv_sems, local_sems):
        x, y, c = lax.axis_index("x"), lax.axis_index("y"), lax.axis_index("c")
        me = (x, y, c)
        sib = (x, y, 1 - c)
        chips = [(1 - x, y), (x, 1 - y), (1 - x, 1 - y)]
        slot = lambda p: 4 * p[0] + 2 * p[1] + p[2]
        chip_of = lambda p: 2 * p[0] + p[1]

        def rcopy(a, k, src, dst, to):
            i = a * (N_DEV - 1) + k
            return pltpu.make_async_remote_copy(src_ref=src, dst_ref=dst, send_sem=send_sems.at[i],
                                                recv_sem=recv_sems.at[i], device_id=to,
                                                device_id_type=pl.DeviceIdType.MESH)

        def local(a):
            if self.mode == "chips":
                return pltpu.make_async_copy(ins[a].at[chip_of(me)], outs[a].at[chip_of(me)], local_sems.at[a])
            src = ins[a].at[slot(me)] if self.mode == "scatter" else ins[a]
            return pltpu.make_async_copy(src, outs[a].at[slot(me)], local_sems.at[a])

        def peer(rel):
            return (1 - x if rel & 4 else x, 1 - y if rel & 2 else y, 1 - c if rel & 1 else c)

        if self.mode == "pair":
            def start():
                for a in range(self.n):
                    for p in range(4):
                        rcopy(a, p, ins[a].at[2 * p + 1 - c], outs[a].at[p], sib).start()

            def middle():
                pass

            def finish():
                for a in range(self.n):
                    for p in range(4):
                        cp = rcopy(a, p, ins[a].at[2 * p + 1 - c], outs[a].at[p], sib)
                        cp.wait_send()
                        cp.wait_recv()
        elif self.mode == "chips":
            def start():
                for a in range(self.n):
                    local(a).start()
                    for j, chip in enumerate(chips):
                        rcopy(a, j, ins[a].at[chip_of(chip)], outs[a].at[chip_of(me)], (*chip, c)).start()

            def middle():
                pass

            def finish():
                for a in range(self.n):
                    for j, chip in enumerate(chips):
                        cp = rcopy(a, j, ins[a].at[chip_of(chip)], outs[a].at[chip_of(chip)], (*chip, c))
                        cp.wait_send()
                        cp.wait_recv()
                    local(a).wait()
        elif self.mode == "scatter":
            def start():
                for a in range(self.n):
                    local(a).start()
                    for rel in range(1, N_DEV):
                        rcopy(a, rel - 1, ins[a].at[slot(peer(rel))], outs[a].at[slot(me)], peer(rel)).start()

            def middle():
                pass

            def finish():
                for a in range(self.n):
                    for rel in range(1, N_DEV):
                        cp = rcopy(a, rel - 1, ins[a].at[slot(peer(rel))], outs[a].at[slot(peer(rel))], peer(rel))
                        cp.wait_send()
                        cp.wait_recv()
                    local(a).wait()
        else:
            def start():
                for a in range(self.n):
                    local(a).start()
                    rcopy(a, 0, ins[a], outs[a].at[slot(me)], sib).start()
                    for j, chip in enumerate(chips):
                        rcopy(a, 1 + j, ins[a], outs[a].at[slot(me)], (*chip, c)).start()

            def middle():
                for a in range(self.n):
                    for j, chip in enumerate(chips):
                        landed = outs[a].at[slot((*chip, c))]
                        rcopy(a, 1 + j, ins[a], landed, me).wait_recv()
                        rcopy(a, 4 + j, landed, landed, sib).start()

            def finish():
                for a in range(self.n):
                    rcopy(a, 0, ins[a], outs[a].at[slot(sib)], me).wait_recv()
                    for j, chip in enumerate(chips):
                        rcopy(a, 4 + j, ins[a], outs[a].at[slot((*chip, 1 - c))], me).wait_recv()
                    for k in range(N_DEV - 1):
                        rcopy(a, k, ins[a], outs[a].at[slot(me)], me).wait_send()
                    local(a).wait()

        return start, middle, finish

    def set_results(self, res):
        self.results = list(res)

    def standalone(self, name):
        n = self.n

        def body(*refs):
            start, middle, finish = self.hooks(refs[:n], refs[n:2 * n], *refs[2 * n:])
            start()
            middle()
            finish()

        self.results = pl.pallas_call(
            body, name=name, in_specs=self.specs, out_specs=self.specs, out_shape=self.out_shape,
            scratch_shapes=self.scratch, compiler_params=pltpu.CompilerParams(has_side_effects=True),
        )(*self.arrs)
        return self.results


class _Carried:
    def __init__(self, parts):
        self.parts = list(parts)
        self.n = sum(p.n for p in self.parts)
        self.arrs = [a for p in self.parts for a in p.arrs]
        self.specs = [s for p in self.parts for s in p.specs]
        self.out_shape = [s for p in self.parts for s in p.out_shape]
        self.scratch = [s for p in self.parts for s in p.scratch]

    def hooks(self, ins, outs, *sems):
        hooks, i = [], 0
        for j, p in enumerate(self.parts):
            hooks.append(p.hooks(ins[i:i + p.n], outs[i:i + p.n], *sems[3 * j:3 * j + 3]))
            i += p.n
        def phase(k):
            def run():
                for h in hooks:
                    h[k]()
            return run

        return phase(0), phase(1), phase(2)

    def set_results(self, res):
        i = 0
        for p in self.parts:
            p.set_results(res[i:i + p.n])
            i += p.n


def _call(body, name, grid, in_specs, out_specs, out_shape, args, scratch=(), sem=(), comm=None, prefetch=()):
    npf = len(prefetch)
    if isinstance(comm, (list, tuple)):
        comm = _Carried(comm)
    if comm is None:
        spec = pltpu.PrefetchScalarGridSpec(num_scalar_prefetch=npf, grid=grid, in_specs=list(in_specs),
                                            out_specs=list(out_specs), scratch_shapes=list(scratch))
        return pl.pallas_call(body, name=name, grid_spec=spec, out_shape=list(out_shape),
                              compiler_params=_params(*sem))(*prefetch, *args)
    ni, no, ns, n = len(in_specs), len(out_specs), len(scratch), comm.n
    steps = math.prod(grid)

    def wrapped(*refs):
        pf, refs = refs[:npf], refs[npf:]
        ins, c_ins = refs[:ni], refs[ni:ni + n]
        outs, c_outs = refs[ni + n:ni + n + no], refs[ni + n + no:ni + 2 * n + no]
        scr, c_scr = refs[ni + 2 * n + no:ni + 2 * n + no + ns], refs[ni + 2 * n + no + ns:]
        start, middle, finish = comm.hooks(c_ins, c_outs, *c_scr)
        step = pl.program_id(0)
        for ax in range(1, len(grid)):
            step = step * grid[ax] + pl.program_id(ax)
        pl.when(step == 0)(start)
        pl.when(step == steps // 2)(middle)
        body(*pf, *ins, *outs, *scr)
        pl.when(step == steps - 1)(finish)

    spec = pltpu.PrefetchScalarGridSpec(num_scalar_prefetch=npf, grid=grid, in_specs=list(in_specs) + comm.specs,
                                        out_specs=list(out_specs) + comm.specs,
                                        scratch_shapes=list(scratch) + comm.scratch)
    res = pl.pallas_call(
        wrapped, name=name, grid_spec=spec, out_shape=list(out_shape) + comm.out_shape,
        compiler_params=pltpu.CompilerParams(dimension_semantics=("arbitrary",) * len(grid),
                                             vmem_limit_bytes=VMEM_LIMIT, has_side_effects=True),
    )(*prefetch, *args, *comm.arrs)
    comm.set_results(res[no:])
    return res[:no]


def _mm(a, b, mode, out_dtypes, name, epilogue=None, extras=(), tm=1024, tn=1024, tk=2048, comm=None,
        b_shards=False, out_shards=False):
    if mode == "tn":
        K, M = a.shape
    else:
        M, K = a.shape
    if b_shards:
        N = b.shape[1] if mode == "nt" else N_DEV * b.shape[2]
    else:
        N = b.shape[0] if mode == "nt" else b.shape[1]
    tm, tn, tk = _tile128(M, tm), _tile128(N, tn), _tile128(K, tk)
    pair_k = b_shards and mode == "nt"
    if pair_k:
        tk = 2 * K // N_DEV
        b = b.reshape(N_DEV // 2, 2, *b.shape[1:])
    elif b_shards or out_shards:
        tn = N // N_DEV
    nk = K // tk
    dims = {"nn": NN, "nt": NT, "tn": TN}[mode]
    a_spec = (pl.BlockSpec((tk, tm), lambda i, j, k: (k, i)) if mode == "tn"
              else pl.BlockSpec((tm, tk), lambda i, j, k: (i, k)))
    if b_shards:
        b_spec = (pl.BlockSpec((None, 2, tn, tk // 2), lambda i, j, k: (k, 0, j, 0)) if mode == "nt"
                  else pl.BlockSpec((None, tk, tn), lambda i, j, k: (j, k, 0)))
    else:
        b_spec = (pl.BlockSpec((tn, tk), lambda i, j, k: (j, k)) if mode == "nt"
                  else pl.BlockSpec((tk, tn), lambda i, j, k: (k, j)))
    mn_spec = pl.BlockSpec((tm, tn), lambda i, j, k: (i, j))
    out_spec = pl.BlockSpec((None, tm, tn), lambda i, j, k: (j, i, 0)) if out_shards else mn_spec
    out_dims = (N_DEV, M, N // N_DEV) if out_shards else (M, N)
    n_ex = len(extras)
    n_out = len(out_dtypes)

    def finish(acc, ex, outs):
        res = (acc,) if epilogue is None else epilogue(acc, *[e[...] for e in ex])
        for o, r in zip(outs, res):
            o[...] = r.astype(o.dtype)

    def product(a_ref, b_ref):
        if pair_k:
            return _dot(a_ref[:, :tk // 2], b_ref[0], NT) + _dot(a_ref[:, tk // 2:], b_ref[1], NT)
        return _dot(a_ref[...], b_ref[...], dims)

    def body(*refs):
        a_ref, b_ref = refs[:2]
        ex = refs[2:2 + n_ex]
        outs = refs[2 + n_ex:2 + n_ex + n_out]
        if nk == 1:
            finish(product(a_ref, b_ref), ex, outs)
            return
        acc = refs[-1]
        k = pl.program_id(2)

        @pl.when(k == 0)
        def _():
            acc[...] = product(a_ref, b_ref)

        @pl.when(jnp.logical_and(k > 0, k < nk - 1))
        def _():
            acc[...] += product(a_ref, b_ref)

        @pl.when(k == nk - 1)
        def _():
            finish(acc[...] + product(a_ref, b_ref), ex, outs)

    out = _call(
        body, name, (M // tm, N // tn, nk), [a_spec, b_spec] + [mn_spec] * n_ex, [out_spec] * n_out,
        [jax.ShapeDtypeStruct(out_dims, dt) for dt in out_dtypes], (a, b, *extras),
        scratch=[] if nk == 1 else [pltpu.VMEM((tm, tn), F32)], sem=("parallel", "parallel", "arbitrary"),
        comm=comm)
    return out[0] if n_out == 1 else out


def _rms_fwd(x, gain, out_dtype, name, width=None, col_block=0, residual=None, tb=256, comm=None):
    T = x.shape[0]
    W = x.shape[1] if width is None else width
    tb = _tile(T, tb)
    has_res = residual is not None

    def body(*refs):
        x_ref, g_ref = refs[:2]
        o_ref = refs[-1]
        xf = x_ref[...]
        y = xf * lax.rsqrt(jnp.mean(xf * xf, axis=-1, keepdims=True) + NORM_EPS) * g_ref[...]
        if has_res:
            y = refs[2][...] + y
        o_ref[...] = y.astype(o_ref.dtype)

    row = pl.BlockSpec((tb, W), lambda i: (i, 0))
    ins = [x, gain] + ([residual] if has_res else [])
    return _call(
        body, name, (T // tb,),
        [pl.BlockSpec((tb, W), lambda i: (i, col_block)),
         pl.BlockSpec((1, W), lambda i: (0, 0))] + ([row] if has_res else []),
        [row], [jax.ShapeDtypeStruct((T, W), out_dtype)], ins, sem=("parallel",), comm=comm)[0]


def _rms_bwd(dy, x, gain, out_dtype, name, width=None, col_block=0, residual=None, tb=256, comm=None):
    T = dy.shape[0]
    W = x.shape[1] if width is None else width
    tb = _tile(T, tb)
    has_res = residual is not None

    def body(*refs):
        dy_ref, x_ref, g_ref = refs[:3]
        dx_ref, dg_ref = refs[-2:]
        i = pl.program_id(0)
        xf = x_ref[...]
        r = lax.rsqrt(jnp.mean(xf * xf, axis=-1, keepdims=True) + NORM_EPS)
        xn = xf * r
        dyf = dy_ref[...].astype(F32)
        dyg = dyf * g_ref[...]
        dx = r * (dyg - xn * jnp.mean(dyg * xn, axis=-1, keepdims=True))
        if has_res:
            dx = refs[3][...] + dx
        dx_ref[...] = dx.astype(dx_ref.dtype)

        @pl.when(i == 0)
        def _():
            dg_ref[...] = jnp.zeros_like(dg_ref)

        dg_ref[...] += jnp.sum(dyf * xn, axis=0, keepdims=True)

    row = pl.BlockSpec((tb, W), lambda i: (i, 0))
    vec = pl.BlockSpec((1, W), lambda i: (0, 0))
    ins = [dy, x, gain] + ([residual] if has_res else [])
    return _call(
        body, name, (T // tb,),
        [row, pl.BlockSpec((tb, W), lambda i: (i, col_block)), vec] + ([row] if has_res else []), [row, vec],
        [jax.ShapeDtypeStruct((T, W), out_dtype), jax.ShapeDtypeStruct((1, W), F32)], ins,
        sem=("arbitrary",), comm=comm)


def _loss_head(x2, y2, gain, target, name, tb=256):
    T, D = x2.shape
    tb = _tile(T, tb)

    def body(x2_ref, y2_ref, g_ref, t_ref, dx3_ref, loss_ref):
        i = pl.program_id(0)
        yf = y2_ref[...]
        x3 = x2_ref[...] + yf * lax.rsqrt(jnp.mean(yf * yf, axis=-1, keepdims=True) + NORM_EPS) * g_ref[...]
        e = x3 - t_ref[...]
        dx3_ref[...] = e * (1.0 / D)

        @pl.when(i == 0)
        def _():
            loss_ref[...] = jnp.zeros_like(loss_ref)

        per_row = jnp.mean(e * e, axis=-1, keepdims=True)
        loss_ref[...] += 0.5 * jnp.sum(per_row, axis=0, keepdims=True)

    row = pl.BlockSpec((tb, D), lambda i: (i, 0))
    return pl.pallas_call(
        body, name=name, grid=(T // tb,),
        in_specs=[row, row, pl.BlockSpec((1, D), lambda i: (0, 0)), row],
        out_specs=[row, pl.BlockSpec((1, 1), lambda i: (0, 0))],
        out_shape=[jax.ShapeDtypeStruct((T, D), F32), jax.ShapeDtypeStruct((1, 1), F32)],
        compiler_params=_params("arbitrary"),
    )(x2, y2, gain, target)


def _rope_tables(positions, rot_dim):
    half = rot_dim // 2
    inv_freq = ROPE_THETA ** (-jnp.arange(0, rot_dim, 2, dtype=F32) / rot_dim)
    ang = positions.astype(F32)[:, None] * inv_freq[None, :]
    cos, sin = jnp.cos(ang), jnp.sin(ang)
    T = positions.shape[0]
    ones = jnp.ones((T, HD - rot_dim), F32)
    cos_t = jnp.concatenate([cos, cos, ones], axis=1)
    sin_t = jnp.concatenate([-sin, sin, jnp.zeros_like(ones)], axis=1)
    return cos_t, sin_t


def _rotate(x, cos_t, sin_t, half):
    lane = lax.broadcasted_iota(jnp.int32, x.shape, 1)
    swapped = jnp.where(lane < half, pltpu.roll(x, HD - half, 1), pltpu.roll(x, half, 1))
    return x * cos_t + swapped * sin_t


def _rope_apply(x, cos_t, sin_t, half, n_blocks, is_rope, out_dtype, name, window=0, tb=256):
    T = x.shape[0]
    tb = _tile(T, tb)
    W = n_blocks * HD

    def body(x_ref, c_ref, s_ref, o_ref):
        for j in range(n_blocks):
            sl = slice(j * HD, (j + 1) * HD)
            xj = x_ref[:, sl]
            if is_rope(j):
                xj = _rotate(xj.astype(F32), c_ref[...], s_ref[...], half)
            o_ref[:, sl] = xj.astype(o_ref.dtype)

    tab = pl.BlockSpec((tb, HD), lambda i: (i, 0))
    return pl.pallas_call(
        body, name=name, grid=(T // tb,),
        in_specs=[pl.BlockSpec((tb, W), lambda i: (i, window)), tab, tab],
        out_specs=pl.BlockSpec((tb, W), lambda i: (i, 0)),
        out_shape=jax.ShapeDtypeStruct((T, W), out_dtype),
        compiler_params=_params("parallel"),
    )(x, cos_t, sin_t)


DSWA_TB = 2048


def _deinterleave(src, dst_ref, d, dtype):
    rows = src.shape[0] // d
    for r in range(d):
        dst_ref[r] = src[pl.ds(r, rows, stride=d), :].astype(dtype)


def _rope_dswa(proj, cos_t, sin_t, name, comm=None):
    T = proj.shape[0]
    tb = _tile(T, DSWA_TB)
    half = ROT_A // 2

    def body(x_ref, c_ref, s_ref, *rest):
        outs, scr = rest[:-1], rest[-1]
        j = pl.program_id(1)

        @pl.when(j < 2 * HEADS)
        def _():
            scr[...] = _rotate(x_ref[...], c_ref[...], s_ref[...], half)

        @pl.when(j >= 2 * HEADS)
        def _():
            scr[...] = x_ref[...]

        for o_ref, d in zip(outs, DSWA_DILATIONS):
            _deinterleave(scr, o_ref, d, BF16)

    blk = pl.BlockSpec((tb, HD), lambda i, j: (i, j))
    tab = pl.BlockSpec((tb, HD), lambda i, j: (i, 0))
    return _call(
        body, name, (T // tb, 3 * HEADS), [blk, tab, tab],
        [pl.BlockSpec((d, tb // d, HD), lambda i, j: (0, i, j)) for d in DSWA_DILATIONS],
        [jax.ShapeDtypeStruct((d, T // d, 3 * AW), BF16) for d in DSWA_DILATIONS], (proj, cos_t, sin_t),
        scratch=[pltpu.VMEM((tb, HD), F32)], sem=("parallel", "parallel"), comm=comm)


def _shared_key_grad(dk, cos_t, sin_t_neg, half, name, tb=512):
    T = dk.shape[0]
    tb = _tile(T, tb)

    def body(d_ref, c_ref, s_ref, o_ref):
        tot = d_ref[:, HD:2 * HD]
        for h in range(1, HEADS):
            tot = tot + d_ref[:, h * QK + HD:(h + 1) * QK]
        o_ref[...] = _rotate(tot, c_ref[...], s_ref[...], half).astype(o_ref.dtype)

    tab = pl.BlockSpec((tb, HD), lambda i: (i, 0))
    return pl.pallas_call(
        body, name=name, grid=(T // tb,),
        in_specs=[pl.BlockSpec((tb, HEADS * QK), lambda i: (i, 0)), tab, tab],
        out_specs=tab, out_shape=jax.ShapeDtypeStruct((T, HD), BF16),
        compiler_params=_params("parallel"),
    )(dk, cos_t, sin_t_neg)


def _band_mask(n):
    row = lax.broadcasted_iota(jnp.int32, (QBLK, 2 * QBLK), 0)
    col = lax.broadcasted_iota(jnp.int32, (QBLK, 2 * QBLK), 1)
    in_prev = jnp.logical_and(jnp.logical_and(col < QBLK, col >= row), n > 0)
    in_cur = jnp.logical_and(col >= QBLK, col - QBLK <= row)
    return jnp.logical_or(in_prev, in_cur)


def _dswa_specs(nb, reverse=False):
    pos = (lambda n: nb - 1 - n) if reverse else (lambda n: n)
    cur = lambda c: pl.BlockSpec((None, QBLK, AW), lambda r, n: (r, pos(n), c))
    prev = lambda c: pl.BlockSpec((None, QBLK, AW), lambda r, n: (r, jnp.maximum(pos(n) - 1, 0), c))
    return cur, prev


def _relayout_spec(d, tb):
    return pl.BlockSpec((d, tb // d, HD), lambda i, h: (0, i, h))


def _dswa_fwd(qkv, name, comm=None):
    d, sd = qkv.shape[:2]
    nb = sd // QBLK

    def body(q_ref, kc_ref, kp_ref, vc_ref, vp_ref, o_ref, l_ref):
        mask = _band_mask(pl.program_id(1))
        for h in range(HEADS):
            sl = slice(h * HD, (h + 1) * HD)
            keys = jnp.concatenate([kp_ref[:, sl], kc_ref[:, sl]], axis=0)
            vals = jnp.concatenate([vp_ref[:, sl], vc_ref[:, sl]], axis=0)
            s = jnp.where(mask, _dot(q_ref[:, sl], keys, NT) * SCALE_A, NEG_INF)
            m = jnp.max(s, axis=-1, keepdims=True)
            p = jnp.exp(s - m)
            den = jnp.sum(p, axis=-1, keepdims=True)
            o_ref[:, sl] = _dot((p / den).astype(BF16), vals, NN)
            l_ref[:, sl] = jnp.broadcast_to(m + jnp.log(den), (QBLK, HD))

    cur, prev = _dswa_specs(nb)
    return _call(
        body, name, (d, nb), [cur(0), cur(1), prev(1), cur(2), prev(2)], [cur(0), cur(0)],
        [jax.ShapeDtypeStruct((d, sd, AW), F32)] * 2, (qkv, qkv, qkv, qkv, qkv),
        sem=("parallel", "parallel"), comm=comm)


def _dswa_merge(outs, lses, name, comm=None):
    nc = len(DSWA_DILATIONS)
    T = outs[0].shape[0] * outs[0].shape[1]
    tb = _tile(T, DSWA_TB)

    def body(*refs):
        o_refs, l_refs = refs[:nc], refs[nc:2 * nc]
        out_ref, outb_ref = refs[2 * nc:2 * nc + 2]
        lt_refs = refs[2 * nc + 2:3 * nc + 2]
        o_nat, l_nat, lt_nat = refs[3 * nc + 2:4 * nc + 2], refs[4 * nc + 2:5 * nc + 2], refs[-1]
        for c, d in enumerate(DSWA_DILATIONS):
            for r in range(d):
                o_nat[c][pl.ds(r, tb // d, stride=d), :] = o_refs[c][r]
                l_nat[c][pl.ds(r, tb // d, stride=d), :] = l_refs[c][r]
        ls = [l[...] for l in l_nat]
        m = functools.reduce(jnp.maximum, ls)
        es = [jnp.exp(l - m) for l in ls]
        tot = functools.reduce(lambda a, b: a + b, es)
        acc = (es[0] / tot) * o_nat[0][...]
        for c in range(1, nc):
            acc = acc + (es[c] / tot) * o_nat[c][...]
        out_ref[...] = acc
        outb_ref[...] = acc.astype(BF16)
        lt_nat[...] = m + jnp.log(tot)
        for c, d in enumerate(DSWA_DILATIONS):
            _deinterleave(lt_nat, lt_refs[c], d, F32)

    nat = pl.BlockSpec((tb, HD), lambda i, h: (i, h))
    by_d = [_relayout_spec(d, tb) for d in DSWA_DILATIONS]
    res = _call(
        body, name, (T // tb, HEADS), by_d + by_d, [nat, nat] + by_d,
        [jax.ShapeDtypeStruct((T, AW), F32), jax.ShapeDtypeStruct((T, AW), BF16)]
        + [jax.ShapeDtypeStruct((d, T // d, AW), F32) for d in DSWA_DILATIONS], (*outs, *lses),
        scratch=[pltpu.VMEM((tb, HD), F32)] * (2 * nc + 1), sem=("parallel", "parallel"), comm=comm)
    return res[0], res[1], res[2:]


def _dswa_delta(dout, out, name):
    nc = len(DSWA_DILATIONS)
    T = out.shape[0]
    tb = _tile(T, DSWA_TB)

    def body(do_ref, o_ref, *rest):
        dl_refs, dob_refs, dl_nat = rest[:nc], rest[nc:2 * nc], rest[-1]
        dl_nat[...] = jnp.broadcast_to(jnp.sum(do_ref[...] * o_ref[...], axis=-1, keepdims=True), (tb, HD))
        for c, d in enumerate(DSWA_DILATIONS):
            _deinterleave(dl_nat, dl_refs[c], d, F32)
            _deinterleave(do_ref, dob_refs[c], d, BF16)

    nat = pl.BlockSpec((tb, HD), lambda i, h: (i, h))
    by_d = [_relayout_spec(d, tb) for d in DSWA_DILATIONS]
    res = pl.pallas_call(
        body, name=name, grid=(T // tb, HEADS),
        in_specs=[nat, nat], out_specs=by_d + by_d,
        out_shape=[jax.ShapeDtypeStruct((d, T // d, AW), F32) for d in DSWA_DILATIONS]
        + [jax.ShapeDtypeStruct((d, T // d, AW), BF16) for d in DSWA_DILATIONS],
        scratch_shapes=[pltpu.VMEM((tb, HD), F32)],
        compiler_params=_params("parallel", "parallel"),
    )(dout, out)
    return res[:nc], res[nc:]


def _delta_prep(dout, col_block, out, name, tb=256):
    T = out.shape[0]
    tb = _tile(T, tb)

    def body(do_ref, o_ref, delta_ref, dob_ref):
        for h in range(HEADS):
            sl = slice(h * HD, (h + 1) * HD)
            doh = do_ref[:, sl]
            delta_ref[:, sl] = jnp.broadcast_to(jnp.sum(doh * o_ref[:, sl], axis=-1, keepdims=True), (tb, HD))
            dob_ref[:, sl] = doh.astype(BF16)

    row = pl.BlockSpec((tb, AW), lambda i: (i, 0))
    return pl.pallas_call(
        body, name=name, grid=(T // tb,),
        in_specs=[pl.BlockSpec((tb, AW), lambda i: (i, col_block)), row], out_specs=[row, row],
        out_shape=[jax.ShapeDtypeStruct((T, AW), F32), jax.ShapeDtypeStruct((T, AW), BF16)],
        compiler_params=_params("parallel"),
    )(dout, out)


def _dswa_bwd(qkv, dout_b, lse_tot, delta, name, comm=None):
    d, sd = qkv.shape[:2]
    nb = sd // QBLK

    def body(q_ref, kc_ref, kp_ref, vc_ref, vp_ref, do_ref, l_ref, dl_ref, dq_ref, dk_ref, dv_ref, carry_k, carry_v):
        mask = _band_mask(nb - 1 - pl.program_id(1))

        @pl.when(pl.program_id(1) == 0)
        def _():
            carry_k[...] = jnp.zeros_like(carry_k)
            carry_v[...] = jnp.zeros_like(carry_v)

        for h in range(HEADS):
            sl = slice(h * HD, (h + 1) * HD)
            qh, doh = q_ref[:, sl], do_ref[:, sl]
            keys = jnp.concatenate([kp_ref[:, sl], kc_ref[:, sl]], axis=0)
            vals = jnp.concatenate([vp_ref[:, sl], vc_ref[:, sl]], axis=0)
            s = jnp.where(mask, _dot(qh, keys, NT) * SCALE_A, NEG_INF)
            p = jnp.exp(s - l_ref[:, sl][:, 0:1])
            ds = (p * (_dot(doh, vals, NT) - dl_ref[:, sl][:, 0:1]) * SCALE_A).astype(BF16)
            dq_ref[:, sl] = _dot(ds, keys, NN)
            dk = _dot(ds, qh, TN)
            dv = _dot(p.astype(BF16), doh, TN)
            dk_ref[:, sl] = dk[QBLK:] + carry_k[:, sl]
            dv_ref[:, sl] = dv[QBLK:] + carry_v[:, sl]
            carry_k[:, sl] = dk[:QBLK]
            carry_v[:, sl] = dv[:QBLK]

    cur, prev = _dswa_specs(nb, reverse=True)
    return _call(
        body, name, (d, nb), [cur(0), cur(1), prev(1), cur(2), prev(2), cur(0), cur(0), cur(0)], [cur(0)] * 3,
        [jax.ShapeDtypeStruct((d, sd, AW), F32)] * 3, (qkv, qkv, qkv, qkv, qkv, dout_b, lse_tot, delta),
        scratch=[pltpu.VMEM((QBLK, AW), F32)] * 2, sem=("parallel", "arbitrary"), comm=comm)


def _dswa_combine(grads, cos_t, sin_t_neg, name):
    nc = len(DSWA_DILATIONS)
    T = grads[0][0].shape[0] * grads[0][0].shape[1]
    tb = _tile(T, DSWA_TB)
    half = ROT_A // 2

    def body(*refs):
        g_refs = refs[:3 * nc]
        c_ref, s_ref = refs[3 * nc:3 * nc + 2]
        outs, accs = refs[3 * nc + 2:3 * nc + 5], refs[3 * nc + 5:]
        for which in range(3):
            acc = accs[which]
            for c, d in enumerate(DSWA_DILATIONS):
                g = g_refs[3 * c + which]
                for r in range(d):
                    if c == 0:
                        acc[...] = g[r]
                    else:
                        acc[pl.ds(r, tb // d, stride=d), :] += g[r]
            val = acc[...]
            if which < 2:
                val = _rotate(val, c_ref[...], s_ref[...], half)
            outs[which][...] = val.astype(BF16)

    nat = pl.BlockSpec((tb, HD), lambda i, h: (i, h))
    tab = pl.BlockSpec((tb, HD), lambda i, h: (i, 0))
    in_specs, ins = [], []
    for d, g in zip(DSWA_DILATIONS, grads):
        in_specs += [_relayout_spec(d, tb)] * 3
        ins += list(g)
    return pl.pallas_call(
        body, name=name, grid=(T // tb, HEADS),
        in_specs=in_specs + [tab, tab], out_specs=[nat] * 3,
        out_shape=[jax.ShapeDtypeStruct((T, AW), BF16)] * 3,
        scratch_shapes=[pltpu.VMEM((tb, HD), F32)] * 3,
        compiler_params=_params("parallel", "parallel"),
    )(*ins, cos_t, sin_t_neg)


MLA_TQ = 512
QK = 2 * HD
LOG2E = 1.4426950408889634


def _triangle(nq, key_major):
    pairs = [(q, k) for q in range(nq) for k in range(q + 1)]
    if key_major:
        pairs.sort(key=lambda p: (p[1], p[0]))
    return (jnp.array([p[0] for p in pairs], jnp.int32), jnp.array([p[1] for p in pairs], jnp.int32))


def _mla_specs(tq):
    q_spec = pl.BlockSpec((tq, HEADS * QK), lambda t, qi, ki: (qi[t], 0))
    k_spec = pl.BlockSpec((tq, HEADS * QK), lambda t, qi, ki: (ki[t], 0))
    v_spec = pl.BlockSpec((tq, AW), lambda t, qi, ki: (ki[t], 1))
    qrow = pl.BlockSpec((tq, AW), lambda t, qi, ki: (qi[t], 0))
    krow = pl.BlockSpec((tq, AW), lambda t, qi, ki: (ki[t], 0))
    return q_spec, k_spec, v_spec, qrow, krow


def _mla_scores(q_ref, k_ref, h, qi, ki, tq):
    s = _dot(q_ref[:, h * QK:(h + 1) * QK], k_ref[:, h * QK:(h + 1) * QK], NT) * SCALE_B
    row = lax.broadcasted_iota(jnp.int32, s.shape, 0) + qi * tq
    col = lax.broadcasted_iota(jnp.int32, s.shape, 1) + ki * tq
    return jnp.where(col <= row, s, NEG_INF)


def _mla_fwd(q, k, v1, name, comm=None):
    T = q.shape[0]
    tq = _tile(T, MLA_TQ)
    tables = _triangle(T // tq, False)

    def body(qi_ref, ki_ref, q_ref, k_ref, v_ref, o_ref, ob_ref, l_ref, m_s, acc):
        t = pl.program_id(0)
        qi, ki = qi_ref[t], ki_ref[t]

        @pl.when(ki == 0)
        def _():
            m_s[...] = jnp.full_like(m_s, NEG_INF)
            acc[...] = jnp.zeros_like(acc)

        row = lax.broadcasted_iota(jnp.int32, (tq, tq), 0) + qi * tq
        col = lax.broadcasted_iota(jnp.int32, (tq, tq), 1) + ki * tq
        bias = jnp.where(col <= row, 0.0, NEG_INF)
        updates = []
        for h in range(HEADS):
            s = _dot(q_ref[:, h * QK:(h + 1) * QK], k_ref[:, h * QK:(h + 1) * QK], NT) + bias
            m_new = jnp.maximum(m_s[h], jnp.max(s, axis=-1, keepdims=True))
            p = jnp.exp2((s - m_new) * (SCALE_B * LOG2E)).astype(BF16)
            alpha = jnp.exp2((m_s[h] - m_new) * (SCALE_B * LOG2E))
            updates.append((m_new, alpha, _dot(p, v_ref[:, h * QK:(h + 1) * QK], NN)))
        for h, (m_new, alpha, pv) in enumerate(updates):
            acc[:, h * QK:(h + 1) * QK] = alpha * acc[:, h * QK:(h + 1) * QK] + pv
            m_s[h] = m_new

        @pl.when(ki == qi)
        def _():
            for h in range(HEADS):
                sl = slice(h * HD, (h + 1) * HD)
                den = acc[:, h * QK + HD:h * QK + HD + 1]
                out = acc[:, h * QK:h * QK + HD] / den
                o_ref[:, sl] = out
                ob_ref[:, sl] = out.astype(BF16)
                l_ref[:, sl] = jnp.broadcast_to(m_s[h] * SCALE_B + jnp.log(den), (tq, HD))

    q_spec, k_spec, _, qrow, _ = _mla_specs(tq)
    return _call(
        body, name, (tables[0].shape[0],), [q_spec, k_spec, k_spec], [qrow, qrow, qrow],
        [jax.ShapeDtypeStruct((T, AW), F32), jax.ShapeDtypeStruct((T, AW), BF16), jax.ShapeDtypeStruct((T, AW), F32)],
        (q, k, v1),
        scratch=[pltpu.VMEM((HEADS, tq, 1), F32), pltpu.VMEM((tq, HEADS * QK), F32)],
        sem=("arbitrary",), comm=comm, prefetch=tables)


def _mla_ds(q_ref, k_ref, v_ref, do_ref, l_ref, dl_ref, h, qi, ki, tq):
    sl = slice(h * HD, (h + 1) * HD)
    p = jnp.exp(_mla_scores(q_ref, k_ref, h, qi, ki, tq) - l_ref[:, sl][:, 0:1])
    ds = (p * (_dot(do_ref[:, sl], v_ref[:, sl], NT) - dl_ref[:, sl][:, 0:1]) * SCALE_B).astype(BF16)
    return p, ds


def _mla_bwd_q(q, k, kv, dout_b, lse, delta, name, comm=None):
    T = q.shape[0]
    tq = _tile(T, MLA_TQ)
    tables = _triangle(T // tq, False)

    def body(qi_ref, ki_ref, q_ref, k_ref, v_ref, do_ref, l_ref, dl_ref, dq_ref):
        t = pl.program_id(0)
        qi, ki = qi_ref[t], ki_ref[t]

        @pl.when(ki == 0)
        def _():
            dq_ref[...] = jnp.zeros_like(dq_ref)

        for h in range(HEADS):
            _, ds = _mla_ds(q_ref, k_ref, v_ref, do_ref, l_ref, dl_ref, h, qi, ki, tq)
            dq_ref[:, h * QK:(h + 1) * QK] += _dot(ds, k_ref[:, h * QK:(h + 1) * QK], NN)

    q_spec, k_spec, v_spec, qrow, _ = _mla_specs(tq)
    return _call(
        body, name, (tables[0].shape[0],), [q_spec, k_spec, v_spec, qrow, qrow, qrow], [q_spec],
        [jax.ShapeDtypeStruct((T, HEADS * QK), F32)], (q, k, kv, dout_b, lse, delta),
        sem=("arbitrary",), comm=comm, prefetch=tables)[0]


def _mla_bwd_kv(q, k, kv, dout_b, lse, delta, name, comm=None):
    T = q.shape[0]
    tq = _tile(T, MLA_TQ)
    tables = _triangle(T // tq, True)

    def body(qi_ref, ki_ref, q_ref, k_ref, v_ref, do_ref, l_ref, dl_ref, dk_ref, dv_ref):
        t = pl.program_id(0)
        qi, ki = qi_ref[t], ki_ref[t]

        @pl.when(qi == ki)
        def _():
            dk_ref[...] = jnp.zeros_like(dk_ref)
            dv_ref[...] = jnp.zeros_like(dv_ref)

        for h in range(HEADS):
            sl = slice(h * HD, (h + 1) * HD)
            p, ds = _mla_ds(q_ref, k_ref, v_ref, do_ref, l_ref, dl_ref, h, qi, ki, tq)
            dv_ref[:, sl] += _dot(p.astype(BF16), do_ref[:, sl], TN)
            dk_ref[:, h * QK:(h + 1) * QK] += _dot(ds, q_ref[:, h * QK:(h + 1) * QK], TN)

    q_spec, k_spec, v_spec, qrow, krow = _mla_specs(tq)
    return _call(
        body, name, (tables[0].shape[0],), [q_spec, k_spec, v_spec, qrow, qrow, qrow], [k_spec, krow],
        [jax.ShapeDtypeStruct((T, HEADS * QK), F32), jax.ShapeDtypeStruct((T, AW), F32)],
        (q, k, kv, dout_b, lse, delta), sem=("arbitrary",), comm=comm, prefetch=tables)


def _pair_sum(by_device, from_sibling, name, tb=256):
    n_chip, R, C = from_sibling.shape
    tb = _tile(R, tb)
    core = jnp.reshape(lax.axis_index("c"), (1,)).astype(jnp.int32)

    def body(core_ref, mine_ref, theirs_ref, o_ref):
        o_ref[...] = (mine_ref[...].astype(F32) + theirs_ref[...].astype(F32)).astype(o_ref.dtype)

    blk = pl.BlockSpec((None, tb, C), lambda p, i, core_ref: (p, i, 0))
    return _call(
        body, name, (n_chip, R // tb),
        [pl.BlockSpec((None, tb, C), lambda p, i, core_ref: (2 * p + core_ref[0], i, 0)), blk], [blk],
        [jax.ShapeDtypeStruct((n_chip, R, C), BF16)], (by_device, from_sibling),
        sem=("parallel", "parallel"), prefetch=(core,))[0]


def _adamw(parts, w, m, v, name, tb=128, comm=None):
    R, C = w.shape
    n_parts = parts.shape[0]
    tb = _tile(R, tb)
    c1 = 1.0 - ADAM_B1
    c2 = 1.0 - ADAM_B2
    bc1 = 1.0 - ADAM_B1 ** ADAM_STEP
    bc2 = 1.0 - ADAM_B2 ** ADAM_STEP

    def body(p_ref, w_ref, m_ref, v_ref, g_ref, d_ref, nm_ref, nv_ref):
        g = p_ref[0].astype(F32)
        for j in range(1, n_parts):
            g = g + p_ref[j].astype(F32)
        nm = ADAM_B1 * m_ref[...] + c1 * g
        nv = ADAM_B2 * v_ref[...] + c2 * (g * g)
        g_ref[...] = g
        nm_ref[...] = nm
        nv_ref[...] = nv
        d_ref[...] = -ADAM_LR * ((nm / bc1) / (jnp.sqrt(nv / bc2) + ADAM_EPS) + ADAM_WD * w_ref[...])

    row = pl.BlockSpec((tb, C), lambda i: (i, 0))
    return _call(
        body, name, (R // tb,), [pl.BlockSpec((n_parts, tb, C), lambda i: (0, i, 0)), row, row, row], [row] * 4,
        [jax.ShapeDtypeStruct((R, C), F32)] * 4, (parts, w, m, v), sem=("parallel",), comm=comm)


def _cols_from_shards(g):
    return jnp.transpose(g, (1, 0, 2)).reshape(g.shape[1], N_DEV * g.shape[2])


def _cols_to_shards(w):
    return jnp.transpose(w.reshape(w.shape[0], N_DEV, w.shape[1] // N_DEV), (1, 0, 2))


def _split_heads(w, first):
    w3 = w.reshape(w.shape[0], HEADS, -1)
    return w3[:, :, :first].reshape(w.shape[0], -1), w3[:, :, first:].reshape(w.shape[0], -1)


def _join_heads(a, b):
    R = a.shape[0]
    return jnp.concatenate([a.reshape(R, HEADS, -1), b.reshape(R, HEADS, -1)], axis=2).reshape(R, -1)


def _pad_heads(w, width):
    w3 = w.reshape(w.shape[0], HEADS, -1)
    return jnp.pad(w3, ((0, 0), (0, 0), (0, width - w3.shape[2]))).reshape(w.shape[0], HEADS * width)


def _unpad_heads(w, k):
    return w.reshape(w.shape[0], HEADS, -1)[:, :, :k].reshape(w.shape[0], HEADS * k)


def kernel(x, positions, norm_attn_pre, norm_attn_post, w_in, q_latent_norm, kv_latent_norm, w_uq, w_ukv, w_out, norm_mlp_pre, norm_mlp_post, w_up, w_down, loss_target, m_norm_attn_pre, m_norm_attn_post, m_w_in, m_q_latent_norm, m_kv_latent_norm, m_w_uq, m_w_ukv, m_w_out, m_norm_mlp_pre, m_norm_mlp_post, m_w_up, m_w_down, v_norm_attn_pre, v_norm_attn_post, v_w_in, v_q_latent_norm, v_kv_latent_norm, v_w_uq, v_w_ukv, v_w_out, v_norm_mlp_pre, v_norm_mlp_post, v_w_up, v_w_down):
    xs = x[0]
    tgt = loss_target[0]
    pos = positions[0]
    T, D = xs.shape
    big = dict(w_in=(w_in, m_w_in, v_w_in), w_uq=(w_uq, m_w_uq, v_w_uq), w_ukv=(w_ukv, m_w_ukv, v_w_ukv),
               w_out=(w_out, m_w_out, v_w_out), w_up=(w_up, m_w_up, v_w_up), w_down=(w_down, m_w_down, v_w_down))
    big = {n: tuple(t[0] for t in ts) for n, ts in big.items()}
    big_names = ["w_in", "w_uq", "w_ukv", "w_out", "w_up", "w_down"]
    col_sharded = {"w_in", "w_uq", "w_ukv", "w_up"}

    wb = {n: big[n][0].astype(BF16) for n in big_names}

    def gathered(ex, i, n):
        g = ex.results[i]
        return _cols_from_shards(g) if n in col_sharded else g.reshape(-1, g.shape[2])

    def by_device(g, n):
        return _cols_to_shards(g) if n in col_sharded else g.reshape(N_DEV, g.shape[0] // N_DEV, g.shape[1])

    def scatter_of(g, n):
        return _Exchange([by_device(g, n)], "scatter")

    cos_a, sin_a = _rope_tables(pos, ROT_A)
    cos_b, sin_b = _rope_tables(pos, ROPE_MLA)

    ex_in = _Exchange([wb["w_in"]], "gather")
    h1 = _rms_fwd(xs, norm_attn_pre, BF16, "norm_attn_pre_fwd", comm=ex_in)
    Wi = jnp.pad(gathered(ex_in, 0, "w_in"), ((0, 0), (0, IN_PAD - IN_COLS)))
    ex_mid = _Exchange([wb["w_uq"], wb["w_ukv"], wb["w_out"]], "gather")
    proj = _mm(h1, Wi, "nn", [F32], "proj_in", tn=1408, comm=ex_mid)
    Wuq = _pad_heads(gathered(ex_mid, 0, "w_uq"), QK)
    Wukv = jnp.concatenate(_split_heads(gathered(ex_mid, 1, "w_ukv"), HD), axis=1)
    Wo = gathered(ex_mid, 2, "w_out")
    n_piece = wb["w_down"].shape[0] // 8
    ex_down = [_Exchange([wb["w_down"][i * n_piece:(i + 1) * n_piece]], "gather") for i in range(8)]
    qkv_by_d = _rope_dswa(proj, cos_a, sin_a, "rope_dswa", comm=ex_down[0])
    outs, lses = [], []
    for d, qkv, ex in zip(DSWA_DILATIONS, qkv_by_d, ex_down[1:4]):
        o, l = _dswa_fwd(qkv, f"dswa_fwd_d{d}", comm=ex)
        outs.append(o)
        lses.append(l)
    a_out, a_out_b, a_lse_by_d = _dswa_merge(outs, lses, "dswa_merge", comm=ex_down[4:6])

    cqn = _rms_fwd(proj, q_latent_norm, BF16, "q_latent_norm_fwd", width=Q_LORA, col_block=3 * AW // Q_LORA)
    ckvn = _rms_fwd(proj, kv_latent_norm, BF16, "kv_latent_norm_fwd", width=KV_LORA, col_block=3 * AW // KV_LORA + 1)
    qb = _mm(cqn, Wuq, "nn", [F32], "q_up")
    kvb = _mm(ckvn, Wukv, "nn", [BF16], "kv_up")
    odd = lambda j: j % 2 == 1
    q_mla = _rope_apply(qb, cos_b, sin_b, ROPE_MLA // 2, 2 * HEADS, odd, BF16, "rope_mla_q")
    kr = _rope_apply(proj, cos_b, sin_b, ROPE_MLA // 2, 1, lambda j: True, BF16, "rope_mla_k",
                     window=(IN_PAD - HD) // HD)
    k_mla = jnp.concatenate([kvb[:, :AW].reshape(T, HEADS, HD), jnp.broadcast_to(kr[:, None, :], (T, HEADS, HD))],
                            axis=2).reshape(T, HEADS * QK)
    v1_mla = jnp.concatenate([kvb[:, AW:].reshape(T, HEADS, HD), jnp.ones((T, HEADS, 1), BF16),
                              jnp.zeros((T, HEADS, HD - 1), BF16)], axis=2).reshape(T, HEADS * QK)
    ex_up = _Exchange([wb["w_up"]], "gather")
    b_out, b_out_b, b_lse = _mla_fwd(q_mla, k_mla, v1_mla, "mla_fwd", comm=ex_up)
    Wup_shards = ex_up.results[0]

    mixed = jnp.concatenate([a_out_b, b_out_b], axis=1)
    y1 = _mm(mixed, Wo, "nn", [F32], "attn_out")
    x2 = _rms_fwd(y1, norm_attn_post, F32, "norm_attn_post_fwd", residual=xs)

    h2 = _rms_fwd(x2, norm_mlp_pre, BF16, "norm_mlp_pre_fwd")

    def relu2(z):
        r = jnp.maximum(z, 0.0)
        return r * r, r

    u, zr = _mm(h2, Wup_shards, "nn", [BF16, BF16], "mlp_up", epilogue=relu2, comm=ex_down[6:8],
                b_shards=True)
    Wdn = jnp.concatenate([ex.results[0] for ex in ex_down], axis=1).reshape(-1, D)
    y2 = _mm(u, Wdn, "nn", [F32], "mlp_down")
    dx3, loss_part = _loss_head(x2, y2, norm_mlp_post, tgt, "loss_head")

    dy2, dg_mlp_post = _rms_bwd(dx3, y2, norm_mlp_post, BF16, "norm_mlp_post_bwd")
    dz = _mm(dy2, Wdn, "nt", [BF16], "mlp_down_dx", epilogue=lambda du, r: (du * (2.0 * r.astype(F32)),), extras=(zr,))
    g_down = _mm(u, dy2, "tn", [BF16], "mlp_down_dw")
    down_dev = by_device(g_down, "w_down")
    pair_down = _Exchange([down_dev], "pair")
    up_dev = _mm(h2, dz, "tn", [BF16], "mlp_up_dw", comm=pair_down, out_shards=True)
    down_chip = _pair_sum(down_dev, pair_down.results[0], "pair_sum_w_down")
    pair_up = _Exchange([up_dev], "pair")
    sc_down = _Exchange([down_chip], "chips")
    dh2 = _mm(dz, Wup_shards, "nt", [F32], "mlp_up_dx", comm=[pair_up, sc_down], b_shards=True)
    up_chip = _pair_sum(up_dev, pair_up.results[0], "pair_sum_w_up")
    dx2, dg_mlp_pre = _rms_bwd(dh2, x2, norm_mlp_pre, F32, "norm_mlp_pre_bwd", residual=dx3)

    dy1, dg_attn_post = _rms_bwd(dx2, y1, norm_attn_post, BF16, "norm_attn_post_bwd")
    dmixed = _mm(dy1, Wo, "nt", [F32], "attn_out_dx")
    g_out = _mm(mixed, dy1, "tn", [BF16], "attn_out_dw")

    b_delta, b_dout = _delta_prep(dmixed, 1, b_out, "mla_delta")
    sc_up = _Exchange([up_chip], "chips")
    dq_mla = _mla_bwd_q(q_mla, k_mla, kvb, b_dout, b_lse, b_delta, "mla_bwd_q", comm=sc_up)
    sc_out = scatter_of(g_out, "w_out")
    dk_mla, dvb = _mla_bwd_kv(q_mla, k_mla, kvb, b_dout, b_lse, b_delta, "mla_bwd_kv", comm=sc_out)
    dqb = _rope_apply(dq_mla, cos_b, -sin_b, ROPE_MLA // 2, 2 * HEADS, odd, BF16, "rope_mla_q_bwd")
    dkn = dk_mla.reshape(T, HEADS, QK)[:, :, :HD].reshape(T, AW)
    dkvb = jnp.concatenate([dkn, dvb], axis=1).astype(BF16)
    d_kr = _shared_key_grad(dk_mla, cos_b, -sin_b, ROPE_MLA // 2, "rope_mla_k_bwd")
    g_uq_pad = _mm(cqn, dqb, "tn", [BF16], "q_up_dw")
    g_ukv_perm = _mm(ckvn, dkvb, "tn", [BF16], "kv_up_dw")
    dcqn = _mm(dqb, Wuq, "nt", [F32], "q_up_dx")
    dckvn = _mm(dkvb, Wukv, "nt", [F32], "kv_up_dx")
    d_cq, dg_q = _rms_bwd(dcqn, proj, q_latent_norm, BF16, "q_latent_norm_bwd", width=Q_LORA, col_block=3 * AW // Q_LORA)
    d_ckv, dg_kv = _rms_bwd(dckvn, proj, kv_latent_norm, BF16, "kv_latent_norm_bwd", width=KV_LORA,
                            col_block=3 * AW // KV_LORA + 1)

    g_uq = _unpad_heads(g_uq_pad, HD + ROPE_MLA)
    g_ukv = _join_heads(g_ukv_perm[:, :AW], g_ukv_perm[:, AW:])
    sc_uq = _Exchange([_cols_to_shards(g_uq), _cols_to_shards(g_ukv)], "scatter")
    a_delta_by_d, a_dout_by_d = _dswa_delta(dmixed, a_out, "dswa_delta")
    a_grads = [_dswa_bwd(qkv_by_d[c], a_dout_by_d[c], a_lse_by_d[c], a_delta_by_d[c], f"dswa_bwd_d{d}")
               for c, d in enumerate(DSWA_DILATIONS)]
    d_aq, d_ak, d_av = _dswa_combine(a_grads, cos_a, -sin_a, "dswa_combine")

    dproj = jnp.concatenate([d_aq, d_ak, d_av, d_cq, d_ckv, d_kr], axis=1)
    g_in_pad = _mm(h1, dproj, "tn", [BF16], "proj_in_dw", tn=1408, comm=sc_uq)
    in_dev = by_device(g_in_pad[:, :IN_COLS], "w_in")
    pair_in = _Exchange([in_dev], "pair")
    dh1 = _mm(dproj, Wi, "nt", [F32], "proj_in_dx", comm=pair_in)
    in_chip = _pair_sum(in_dev, pair_in.results[0], "pair_sum_w_in")
    cuts = [0, 3 * D // 8, 11 * D // 16, D]
    sc_in = [_Exchange([in_chip[:, a:b]], "chips") for a, b in zip(cuts[:-1], cuts[1:])]
    grad_x, dg_attn_pre = _rms_bwd(dh1, xs, norm_attn_pre, F32, "norm_attn_pre_bwd", residual=dx2, comm=sc_in[0])

    big_out = dict(w_down=_adamw(sc_down.results[0], *big["w_down"], "adamw_w_down", comm=sc_in[1]),
                   w_up=_adamw(sc_up.results[0], *big["w_up"], "adamw_w_up", comm=sc_in[2]))
    parts = dict(w_in=jnp.concatenate([sc.results[0] for sc in sc_in], axis=1), w_uq=sc_uq.results[0],
                 w_ukv=sc_uq.results[1], w_out=sc_out.results[0])
    big_out.update({n: _adamw(parts[n], *big[n], f"adamw_{n}") for n in parts})

    gain_names = ["norm_attn_pre", "norm_attn_post", "q_latent_norm", "kv_latent_norm", "norm_mlp_pre", "norm_mlp_post"]
    gain_args = dict(norm_attn_pre=(norm_attn_pre, m_norm_attn_pre, v_norm_attn_pre),
                     norm_attn_post=(norm_attn_post, m_norm_attn_post, v_norm_attn_post),
                     q_latent_norm=(q_latent_norm, m_q_latent_norm, v_q_latent_norm),
                     kv_latent_norm=(kv_latent_norm, m_kv_latent_norm, v_kv_latent_norm),
                     norm_mlp_pre=(norm_mlp_pre, m_norm_mlp_pre, v_norm_mlp_pre),
                     norm_mlp_post=(norm_mlp_post, m_norm_mlp_post, v_norm_mlp_post))
    gain_grads = dict(norm_attn_pre=dg_attn_pre, norm_attn_post=dg_attn_post, q_latent_norm=dg_q,
                      kv_latent_norm=dg_kv, norm_mlp_pre=dg_mlp_pre, norm_mlp_post=dg_mlp_post)
    packed = jnp.concatenate([gain_grads[n] for n in gain_names], axis=1)
    gain_parts = _Exchange([packed], "gather").standalone("gather_gain_grads")[0]
    pack3 = lambda i: jnp.concatenate([gain_args[n][i] for n in gain_names], axis=1)
    gain_out = _adamw(gain_parts, pack3(0), pack3(1), pack3(2), "adamw_gains", tb=1)
    offs = [0]
    for n in gain_names:
        offs.append(offs[-1] + gain_args[n][0].shape[1])
    small_out = {n: tuple(o[:, offs[i]:offs[i + 1]] for o in gain_out) for i, n in enumerate(gain_names)}

    loss = lax.psum(loss_part[0, 0], ("x", "y", "c"))

    order = ["norm_attn_pre", "norm_attn_post", "w_in", "q_latent_norm", "kv_latent_norm", "w_uq", "w_ukv", "w_out",
             "norm_mlp_pre", "norm_mlp_post", "w_up", "w_down"]
    res = {n: (small_out[n] if n in small_out else tuple(o[None] for o in big_out[n])) for n in order}
    return (loss, grad_x[None], *[res[n][0] for n in order], *[res[n][1] for n in order],
            *[res[n][2] for n in order], *[res[n][3] for n in order])
```

```python
import functools
import math

import jax
import jax.numpy as jnp
from jax import lax
from jax.experimental import pallas as pl
from jax.experimental.pallas import tpu as pltpu

F32 = jnp.float32
BF16 = jnp.bfloat16

N_DEV = 8
HEADS = 8
HD = 128
AW = HEADS * HD
Q_LORA = 512
KV_LORA = 512
ROPE_MLA = 64
ROT_A = 32
IN_COLS = 3 * AW + Q_LORA + KV_LORA + ROPE_MLA
IN_PAD = 3 * AW + Q_LORA + KV_LORA + HD
QBLK = 128
DSWA_DILATIONS = (1, 4, 16)
ROPE_THETA = 500000.0
NORM_EPS = 1e-6
NEG_INF = -1e30
SCALE_A = HD ** -0.5
SCALE_B = (HD + ROPE_MLA) ** -0.5

ADAM_LR = 0.001
ADAM_B1 = 0.9
ADAM_B2 = 0.999
ADAM_EPS = 1e-08
ADAM_WD = 0.01
ADAM_STEP = 10

VMEM_LIMIT = 48 * 1024 * 1024

NT = (((1,), (1,)), ((), ()))
NN = (((1,), (0,)), ((), ()))
TN = (((0,), (0,)), ((), ()))


def _dot(a, b, dims):
    return lax.dot_general(a, b, dims, preferred_element_type=F32)


def _params(*sem):
    return pltpu.CompilerParams(dimension_semantics=sem, vmem_limit_bytes=VMEM_LIMIT)


def _tile(n, want):
    t = min(n, want)
    while n % t:
        t //= 2
    return t


def _tile128(n, want):
    if n % 128:
        return n
    units = n // 128
    return 128 * max(u for u in range(1, max(want // 128, 1) + 1) if units % u == 0)


class _Exchange:
    def __init__(self, arrs, mode):
        self.arrs = list(arrs)
        self.mode = mode
        self.n = len(self.arrs)
        self.results = None
        hbm = pl.BlockSpec(memory_space=pltpu.HBM)
        self.specs = [hbm] * self.n
        shape = {"gather": lambda a: (N_DEV,) + a.shape, "scatter": lambda a: a.shape,
                 "pair": lambda a: (4,) + a.shape[1:], "chips": lambda a: a.shape}[mode]
        self.out_shape = [jax.ShapeDtypeStruct(shape(a), a.dtype) for a in self.arrs]
        n_sem = self.n * (N_DEV - 1)
        self.scratch = [pltpu.SemaphoreType.DMA((n_sem,)), pltpu.SemaphoreType.DMA((n_sem,)),
                        pltpu.SemaphoreType.DMA((self.n,))]

    def hooks(self, ins, outs, send_sems, recv_sems, local_sems):
        x, y, c = lax.axis_index("x"), lax.axis_index("y"), lax.axis_index("c")
        me = (x, y, c)
        sib = (x, y, 1 - c)
        chips = [(1 - x, y), (x, 1 - y), (1 - x, 1 - y)]
        slot = lambda p: 4 * p[0] + 2 * p[1] + p[2]
        chip_of = lambda p: 2 * p[0] + p[1]

        def rcopy(a, k, src, dst, to):
            i = a * (N_DEV - 1) + k
            return pltpu.make_async_remote_copy(src_ref=src, dst_ref=dst, send_sem=send_sems.at[i],
                                                recv_sem=recv_sems.at[i], device_id=to,
                                                device_id_type=pl.DeviceIdType.MESH)

        def local(a):
            if self.mode == "chips":
                return pltpu.make_async_copy(ins[a].at[chip_of(me)], outs[a].at[chip_of(me)], local_sems.at[a])
            src = ins[a].at[slot(me)] if self.mode == "scatter" else ins[a]
            return pltpu.make_async_copy(src, outs[a].at[slot(me)], local_sems.at[a])

        def peer(rel):
            return (1 - x if rel & 4 else x, 1 - y if rel & 2 else y, 1 - c if rel & 1 else c)

        if self.mode == "pair":
            def start():
                for a in range(self.n):
                    for p in range(4):
                        rcopy(a, p, ins[a].at[2 * p + 1 - c], outs[a].at[p], sib).start()

            def middle():
                pass

            def finish():
                for a in range(self.n):
                    for p in range(4):
                        cp = rcopy(a, p, ins[a].at[2 * p + 1 - c], outs[a].at[p], sib)
                        cp.wait_send()
                        cp.wait_recv()
        elif self.mode == "chips":
            def start():
                for a in range(self.n):
                    local(a).start()
                    for j, chip in enumerate(chips):
                        rcopy(a, j, ins[a].at[chip_of(chip)], outs[a].at[chip_of(me)], (*chip, c)).start()

            def middle():
                pass

            def finish():
                for a in range(self.n):
                    for j, chip in enumerate(chips):
                        cp = rcopy(a, j, ins[a].at[chip_of(chip)], outs[a].at[chip_of(chip)], (*chip, c))
                        cp.wait_send()
                        cp.wait_recv()
                    local(a).wait()
        elif self.mode == "scatter":
            def start():
                for a in range(self.n):
                    local(a).start()
                    for rel in range(1, N_DEV):
                        rcopy(a, rel - 1, ins[a].at[slot(peer(rel))], outs[a].at[slot(me)], peer(rel)).start()

            def middle():
                pass

            def finish():
                for a in range(self.n):
                    for rel in range(1, N_DEV):
                        cp = rcopy(a, rel - 1, ins[a].at[slot(peer(rel))], outs[a].at[slot(peer(rel))], peer(rel))
                        cp.wait_send()
                        cp.wait_recv()
                    local(a).wait()
        else:
            def start():
                for a in range(self.n):
                    local(a).start()
                    rcopy(a, 0, ins[a], outs[a].at[slot(me)], sib).start()
                    for j, chip in enumerate(chips):
                        rcopy(a, 1 + j, ins[a], outs[a].at[slot(me)], (*chip, c)).start()

            def middle():
                for a in range(self.n):
                    for j, chip in enumerate(chips):
                        landed = outs[a].at[slot((*chip, c))]
                        rcopy(a, 1 + j, ins[a], landed, me).wait_recv()
                        rcopy(a, 4 + j, landed, landed, sib).start()

            def finish():
                for a in range(self.n):
                    rcopy(a, 0, ins[a], outs[a].at[slot(sib)], me).wait_recv()
                    for j, chip in enumerate(chips):
                        rcopy(a, 4 + j, ins[a], outs[a].at[slot((*chip, 1 - c))], me).wait_recv()
                    for k in range(N_DEV - 1):
                        rcopy(a, k, ins[a], outs[a].at[slot(me)], me).wait_send()
                    local(a).wait()

        return start, middle, finish

    def set_results(self, res):
        self.results = list(res)

    def standalone(self, name):
        n = self.n

        def body(*refs):
            start, middle, finish = self.hooks(refs[:n], refs[n:2 * n], *refs[2 * n:])
            start()
            middle()
            finish()

        self.results = pl.pallas_call(
            body, name=name, in_specs=self.specs, out_specs=self.specs, out_shape=self.out_shape,
            scratch_shapes=self.scratch, compiler_params=pltpu.CompilerParams(has_side_effects=True),
        )(*self.arrs)
        return self.results


class _Carried:
    def __init__(self, parts):
        self.parts = list(parts)
        self.n = sum(p.n for p in self.parts)
        self.arrs = [a for p in self.parts for a in p.arrs]
        self.specs = [s for p in self.parts for s in p.specs]
        self.out_shape = [s for p in self.parts for s in p.out_shape]
        self.scratch = [s for p in self.parts for s in p.scratch]

    def hooks(self, ins, outs, *sems):
        hooks, i = [], 0
        for j, p in enumerate(self.parts):
            hooks.append(p.hooks(ins[i:i + p.n], outs[i:i + p.n], *sems[3 * j:3 * j + 3]))
            i += p.n
        def phase(k):
            def run():
                for h in hooks:
                    h[k]()
            return run

        return phase(0), phase(1), phase(2)

    def set_results(self, res):
        i = 0
        for p in self.parts:
            p.set_results(res[i:i + p.n])
            i += p.n


def _call(body, name, grid, in_specs, out_specs, out_shape, args, scratch=(), sem=(), comm=None, prefetch=()):
    npf = len(prefetch)
    if isinstance(comm, (list, tuple)):
        comm = _Carried(comm)
    if comm is None:
        spec = pltpu.PrefetchScalarGridSpec(num_scalar_prefetch=npf, grid=grid, in_specs=list(in_specs),
                                            out_specs=list(out_specs), scratch_shapes=list(scratch))
        return pl.pallas_call(body, name=name, grid_spec=spec, out_shape=list(out_shape),
                              compiler_params=_params(*sem))(*prefetch, *args)
    ni, no, ns, n = len(in_specs), len(out_specs), len(scratch), comm.n
    steps = math.prod(grid)

    def wrapped(*refs):
        pf, refs = refs[:npf], refs[npf:]
        ins, c_ins = refs[:ni], refs[ni:ni + n]
        outs, c_outs = refs[ni + n:ni + n + no], refs[ni + n + no:ni + 2 * n + no]
        scr, c_scr = refs[ni + 2 * n + no:ni + 2 * n + no + ns], refs[ni + 2 * n + no + ns:]
        start, middle, finish = comm.hooks(c_ins, c_outs, *c_scr)
        step = pl.program_id(0)
        for ax in range(1, len(grid)):
            step = step * grid[ax] + pl.program_id(ax)
        pl.when(step == 0)(start)
        pl.when(step == steps // 2)(middle)
        body(*pf, *ins, *outs, *scr)
        pl.when(step == steps - 1)(finish)

    spec = pltpu.PrefetchScalarGridSpec(num_scalar_prefetch=npf, grid=grid, in_specs=list(in_specs) + comm.specs,
                                        out_specs=list(out_specs) + comm.specs,
                                        scratch_shapes=list(scratch) + comm.scratch)
    res = pl.pallas_call(
        wrapped, name=name, grid_spec=spec, out_shape=list(out_shape) + comm.out_shape,
        compiler_params=pltpu.CompilerParams(dimension_semantics=("arbitrary",) * len(grid),
                                             vmem_limit_bytes=VMEM_LIMIT, has_side_effects=True),
    )(*prefetch, *args, *comm.arrs)
    comm.set_results(res[no:])
    return res[:no]


def _mm(a, b, mode, out_dtypes, name, epilogue=None, extras=(), tm=1024, tn=1024, tk=2048, comm=None,
        b_shards=False, out_shards=False):
    if mode == "tn":
        K, M = a.shape
    else:
        M, K = a.shape
    if b_shards:
        N = b.shape[1] if mode == "nt" else N_DEV * b.shape[2]
    else:
        N = b.shape[0] if mode == "nt" else b.shape[1]
    tm, tn, tk = _tile128(M, tm), _tile128(N, tn), _tile128(K, tk)
    pair_k = b_shards and mode == "nt"
    if pair_k:
        tk = 2 * K // N_DEV
        b = b.reshape(N_DEV // 2, 2, *b.shape[1:])
    elif b_shards or out_shards:
        tn = N // N_DEV
    nk = K // tk
    dims = {"nn": NN, "nt": NT, "tn": TN}[mode]
    a_spec = (pl.BlockSpec((tk, tm), lambda i, j, k: (k, i)) if mode == "tn"
              else pl.BlockSpec((tm, tk), lambda i, j, k: (i, k)))
    if b_shards:
        b_spec = (pl.BlockSpec((None, 2, tn, tk // 2), lambda i, j, k: (k, 0, j, 0)) if mode == "nt"
                  else pl.BlockSpec((None, tk, tn), lambda i, j, k: (j, k, 0)))
    else:
        b_spec = (pl.BlockSpec((tn, tk), lambda i, j, k: (j, k)) if mode == "nt"
                  else pl.BlockSpec((tk, tn), lambda i, j, k: (k, j)))
    mn_spec = pl.BlockSpec((tm, tn), lambda i, j, k: (i, j))
    out_spec = pl.BlockSpec((None, tm, tn), lambda i, j, k: (j, i, 0)) if out_shards else mn_spec
    out_dims = (N_DEV, M, N // N_DEV) if out_shards else (M, N)
    n_ex = len(extras)
    n_out = len(out_dtypes)

    def finish(acc, ex, outs):
        res = (acc,) if epilogue is None else epilogue(acc, *[e[...] for e in ex])
        for o, r in zip(outs, res):
            o[...] = r.astype(o.dtype)

    def product(a_ref, b_ref):
        if pair_k:
            return _dot(a_ref[:, :tk // 2], b_ref[0], NT) + _dot(a_ref[:, tk // 2:], b_ref[1], NT)
        return _dot(a_ref[...], b_ref[...], dims)

    def body(*refs):
        a_ref, b_ref = refs[:2]
        ex = refs[2:2 + n_ex]
        outs = refs[2 + n_ex:2 + n_ex + n_out]
        if nk == 1:
            finish(product(a_ref, b_ref), ex, outs)
            return
        acc = refs[-1]
        k = pl.program_id(2)

        @pl.when(k == 0)
        def _():
            acc[...] = product(a_ref, b_ref)

        @pl.when(jnp.logical_and(k > 0, k < nk - 1))
        def _():
            acc[...] += product(a_ref, b_ref)

        @pl.when(k == nk - 1)
        def _():
            finish(acc[...] + product(a_ref, b_ref), ex, outs)

    out = _call(
        body, name, (M // tm, N // tn, nk), [a_spec, b_spec] + [mn_spec] * n_ex, [out_spec] * n_out,
        [jax.ShapeDtypeStruct(out_dims, dt) for dt in out_dtypes], (a, b, *extras),
        scratch=[] if nk == 1 else [pltpu.VMEM((tm, tn), F32)], sem=("parallel", "parallel", "arbitrary"),
        comm=comm)
    return out[0] if n_out == 1 else out


def _rms_fwd(x, gain, out_dtype, name, width=None, col_block=0, residual=None, tb=256, comm=None):
    T = x.shape[0]
    W = x.shape[1] if width is None else width
    tb = _tile(T, tb)
    has_res = residual is not None

    def body(*refs):
        x_ref, g_ref = refs[:2]
        o_ref = refs[-1]
        xf = x_ref[...]
        y = xf * lax.rsqrt(jnp.mean(xf * xf, axis=-1, keepdims=True) + NORM_EPS) * g_ref[...]
        if has_res:
            y = refs[2][...] + y
        o_ref[...] = y.astype(o_ref.dtype)

    row = pl.BlockSpec((tb, W), lambda i: (i, 0))
    ins = [x, gain] + ([residual] if has_res else [])
    return _call(
        body, name, (T // tb,),
        [pl.BlockSpec((tb, W), lambda i: (i, col_block)),
         pl.BlockSpec((1, W), lambda i: (0, 0))] + ([row] if has_res else []),
        [row], [jax.ShapeDtypeStruct((T, W), out_dtype)], ins, sem=("parallel",), comm=comm)[0]


def _rms_bwd(dy, x, gain, out_dtype, name, width=None, col_block=0, residual=None, tb=256, comm=None):
    T = dy.shape[0]
    W = x.shape[1] if width is None else width
    tb = _tile(T, tb)
    has_res = residual is not None

    def body(*refs):
        dy_ref, x_ref, g_ref = refs[:3]
        dx_ref, dg_ref = refs[-2:]
        i = pl.program_id(0)
        xf = x_ref[...]
        r = lax.rsqrt(jnp.mean(xf * xf, axis=-1, keepdims=True) + NORM_EPS)
        xn = xf * r
        dyf = dy_ref[...].astype(F32)
        dyg = dyf * g_ref[...]
        dx = r * (dyg - xn * jnp.mean(dyg * xn, axis=-1, keepdims=True))
        if has_res:
            dx = refs[3][...] + dx
        dx_ref[...] = dx.astype(dx_ref.dtype)

        @pl.when(i == 0)
        def _():
            dg_ref[...] = jnp.zeros_like(dg_ref)

        dg_ref[...] += jnp.sum(dyf * xn, axis=0, keepdims=True)

    row = pl.BlockSpec((tb, W), lambda i: (i, 0))
    vec = pl.BlockSpec((1, W), lambda i: (0, 0))
    ins = [dy, x, gain] + ([residual] if has_res else [])
    return _call(
        body, name, (T // tb,),
        [row, pl.BlockSpec((tb, W), lambda i: (i, col_block)), vec] + ([row] if has_res else []), [row, vec],
        [jax.ShapeDtypeStruct((T, W), out_dtype), jax.ShapeDtypeStruct((1, W), F32)], ins,
        sem=("arbitrary",), comm=comm)


def _loss_head(x2, y2, gain, target, name, tb=256):
    T, D = x2.shape
    tb = _tile(T, tb)

    def body(x2_ref, y2_ref, g_ref, t_ref, dx3_ref, loss_ref):
        i = pl.program_id(0)
        yf = y2_ref[...]
        x3 = x2_ref[...] + yf * lax.rsqrt(jnp.mean(yf * yf, axis=-1, keepdims=True) + NORM_EPS) * g_ref[...]
        e = x3 - t_ref[...]
        dx3_ref[...] = e * (1.0 / D)

        @pl.when(i == 0)
        def _():
            loss_ref[...] = jnp.zeros_like(loss_ref)

        per_row = jnp.mean(e * e, axis=-1, keepdims=True)
        loss_ref[...] += 0.5 * jnp.sum(per_row, axis=0, keepdims=True)

    row = pl.BlockSpec((tb, D), lambda i: (i, 0))
    return pl.pallas_call(
        body, name=name, grid=(T // tb,),
        in_specs=[row, row, pl.BlockSpec((1, D), lambda i: (0, 0)), row],
        out_specs=[row, pl.BlockSpec((1, 1), lambda i: (0, 0))],
        out_shape=[jax.ShapeDtypeStruct((T, D), F32), jax.ShapeDtypeStruct((1, 1), F32)],
        compiler_params=_params("arbitrary"),
    )(x2, y2, gain, target)


def _rope_tables(positions, rot_dim):
    half = rot_dim // 2
    inv_freq = ROPE_THETA ** (-jnp.arange(0, rot_dim, 2, dtype=F32) / rot_dim)
    ang = positions.astype(F32)[:, None] * inv_freq[None, :]
    cos, sin = jnp.cos(ang), jnp.sin(ang)
    T = positions.shape[0]
    ones = jnp.ones((T, HD - rot_dim), F32)
    cos_t = jnp.concatenate([cos, cos, ones], axis=1)
    sin_t = jnp.concatenate([-sin, sin, jnp.zeros_like(ones)], axis=1)
    return cos_t, sin_t


def _rotate(x, cos_t, sin_t, half):
    lane = lax.broadcasted_iota(jnp.int32, x.shape, 1)
    swapped = jnp.where(lane < half, pltpu.roll(x, HD - half, 1), pltpu.roll(x, half, 1))
    return x * cos_t + swapped * sin_t


def _rope_apply(x, cos_t, sin_t, half, n_blocks, is_rope, out_dtype, name, window=0, tb=256):
    T = x.shape[0]
    tb = _tile(T, tb)
    W = n_blocks * HD

    def body(x_ref, c_ref, s_ref, o_ref):
        for j in range(n_blocks):
            sl = slice(j * HD, (j + 1) * HD)
            xj = x_ref[:, sl]
            if is_rope(j):
                xj = _rotate(xj.astype(F32), c_ref[...], s_ref[...], half)
            o_ref[:, sl] = xj.astype(o_ref.dtype)

    tab = pl.BlockSpec((tb, HD), lambda i: (i, 0))
    return pl.pallas_call(
        body, name=name, grid=(T // tb,),
        in_specs=[pl.BlockSpec((tb, W), lambda i: (i, window)), tab, tab],
        out_specs=pl.BlockSpec((tb, W), lambda i: (i, 0)),
        out_shape=jax.ShapeDtypeStruct((T, W), out_dtype),
        compiler_params=_params("parallel"),
    )(x, cos_t, sin_t)


DSWA_TB = 2048


def _deinterleave(src, dst_ref, d, dtype):
    rows = src.shape[0] // d
    for r in range(d):
        dst_ref[r] = src[pl.ds(r, rows, stride=d), :].astype(dtype)


def _rope_dswa(proj, cos_t, sin_t, name, comm=None):
    T = proj.shape[0]
    tb = _tile(T, DSWA_TB)
    half = ROT_A // 2

    def body(x_ref, c_ref, s_ref, *rest):
        outs, scr = rest[:-1], rest[-1]
        j = pl.program_id(1)

        @pl.when(j < 2 * HEADS)
        def _():
            scr[...] = _rotate(x_ref[...], c_ref[...], s_ref[...], half)

        @pl.when(j >= 2 * HEADS)
        def _():
            scr[...] = x_ref[...]

        for o_ref, d in zip(outs, DSWA_DILATIONS):
            _deinterleave(scr, o_ref, d, BF16)

    blk = pl.BlockSpec((tb, HD), lambda i, j: (i, j))
    tab = pl.BlockSpec((tb, HD), lambda i, j: (i, 0))
    return _call(
        body, name, (T // tb, 3 * HEADS), [blk, tab, tab],
        [pl.BlockSpec((d, tb // d, HD), lambda i, j: (0, i, j)) for d in DSWA_DILATIONS],
        [jax.ShapeDtypeStruct((d, T // d, 3 * AW), BF16) for d in DSWA_DILATIONS], (proj, cos_t, sin_t),
        scratch=[pltpu.VMEM((tb, HD), F32)], sem=("parallel", "parallel"), comm=comm)


def _shared_key_grad(dk, cos_t, sin_t_neg, half, name, tb=512):
    T = dk.shape[0]
    tb = _tile(T, tb)

    def body(d_ref, c_ref, s_ref, o_ref):
        tot = d_ref[:, HD:2 * HD]
        for h in range(1, HEADS):
            tot = tot + d_ref[:, h * QK + HD:(h + 1) * QK]
        o_ref[...] = _rotate(tot, c_ref[...], s_ref[...], half).astype(o_ref.dtype)

    tab = pl.BlockSpec((tb, HD), lambda i: (i, 0))
    return pl.pallas_call(
        body, name=name, grid=(T // tb,),
        in_specs=[pl.BlockSpec((tb, HEADS * QK), lambda i: (i, 0)), tab, tab],
        out_specs=tab, out_shape=jax.ShapeDtypeStruct((T, HD), BF16),
        compiler_params=_params("parallel"),
    )(dk, cos_t, sin_t_neg)


def _band_mask(n):
    row = lax.broadcasted_iota(jnp.int32, (QBLK, 2 * QBLK), 0)
    col = lax.broadcasted_iota(jnp.int32, (QBLK, 2 * QBLK), 1)
    in_prev = jnp.logical_and(jnp.logical_and(col < QBLK, col >= row), n > 0)
    in_cur = jnp.logical_and(col >= QBLK, col - QBLK <= row)
    return jnp.logical_or(in_prev, in_cur)


def _dswa_specs(nb, reverse=False):
    pos = (lambda n: nb - 1 - n) if reverse else (lambda n: n)
    cur = lambda c: pl.BlockSpec((None, QBLK, AW), lambda r, n: (r, pos(n), c))
    prev = lambda c: pl.BlockSpec((None, QBLK, AW), lambda r, n: (r, jnp.maximum(pos(n) - 1, 0), c))
    stat = pl.BlockSpec((None, QBLK, HD), lambda r, n: (r, pos(n), 0))
    return cur, prev, stat


def _relayout_spec(d, tb, per_head=True):
    if per_head:
        return pl.BlockSpec((d, tb // d, HD), lambda i, h: (0, i, h))
    return pl.BlockSpec((d, tb // d, HD), lambda i, h: (0, i, 0))


def _head_lane(x, h):
    lane = lax.broadcasted_iota(jnp.int32, x.shape, 1)
    return jnp.sum(jnp.where(lane == h, x, 0.0), axis=-1, keepdims=True)


def _dswa_fwd(qkv, name, comm=None):
    d, sd = qkv.shape[:2]
    nb = sd // QBLK

    def body(q_ref, kc_ref, kp_ref, vc_ref, vp_ref, o_ref, l_ref):
        mask = _band_mask(pl.program_id(1))
        l_ref[...] = jnp.zeros_like(l_ref)
        for h in range(HEADS):
            sl = slice(h * HD, (h + 1) * HD)
            keys = jnp.concatenate([kp_ref[:, sl], kc_ref[:, sl]], axis=0)
            vals = jnp.concatenate([vp_ref[:, sl], vc_ref[:, sl]], axis=0)
            s = jnp.where(mask, _dot(q_ref[:, sl], keys, NT) * SCALE_A, NEG_INF)
            m = jnp.max(s, axis=-1, keepdims=True)
            p = jnp.exp(s - m)
            den = jnp.sum(p, axis=-1, keepdims=True)
            o_ref[:, sl] = _dot((p / den).astype(BF16), vals, NN)
            l_ref[:, h:h + 1] = m + jnp.log(den)

    cur, prev, stat = _dswa_specs(nb)
    return _call(
        body, name, (d, nb), [cur(0), cur(1), prev(1), cur(2), prev(2)], [cur(0), stat],
        [jax.ShapeDtypeStruct((d, sd, AW), F32), jax.ShapeDtypeStruct((d, sd, HD), F32)],
        (qkv, qkv, qkv, qkv, qkv), sem=("parallel", "parallel"), comm=comm)


def _dswa_merge(outs, lses, name, comm=None):
    nc = len(DSWA_DILATIONS)
    T = outs[0].shape[0] * outs[0].shape[1]
    tb = _tile(T, DSWA_TB)

    def body(*refs):
        o_refs, l_refs = refs[:nc], refs[nc:2 * nc]
        out_ref, outb_ref = refs[2 * nc:2 * nc + 2]
        lt_refs = refs[2 * nc + 2:3 * nc + 2]
        o_nat, l_nat, lt_nat = refs[3 * nc + 2:4 * nc + 2], refs[4 * nc + 2:5 * nc + 2], refs[-1]
        h = pl.program_id(1)
        for c, d in enumerate(DSWA_DILATIONS):
            for r in range(d):
                o_nat[c][pl.ds(r, tb // d, stride=d), :] = o_refs[c][r]
                l_nat[c][pl.ds(r, tb // d, stride=d), :] = l_refs[c][r]
        ls = [l[...] for l in l_nat]
        m = functools.reduce(jnp.maximum, ls)
        es = [jnp.exp(l - m) for l in ls]
        tot = functools.reduce(lambda a, b: a + b, es)
        acc = _head_lane(es[0] / tot, h) * o_nat[0][...]
        for c in range(1, nc):
            acc = acc + _head_lane(es[c] / tot, h) * o_nat[c][...]
        out_ref[...] = acc
        outb_ref[...] = acc.astype(BF16)
        lt_nat[...] = m + jnp.log(tot)
        for c, d in enumerate(DSWA_DILATIONS):
            _deinterleave(lt_nat, lt_refs[c], d, F32)

    nat = pl.BlockSpec((tb, HD), lambda i, h: (i, h))
    by_d = [_relayout_spec(d, tb) for d in DSWA_DILATIONS]
    stat_by_d = [_relayout_spec(d, tb, per_head=False) for d in DSWA_DILATIONS]
    res = _call(
        body, name, (T // tb, HEADS), by_d + stat_by_d, [nat, nat] + stat_by_d,
        [jax.ShapeDtypeStruct((T, AW), F32), jax.ShapeDtypeStruct((T, AW), BF16)]
        + [jax.ShapeDtypeStruct((d, T // d, HD), F32) for d in DSWA_DILATIONS], (*outs, *lses),
        scratch=[pltpu.VMEM((tb, HD), F32)] * (2 * nc + 1), sem=("parallel", "arbitrary"), comm=comm)
    return res[0], res[1], res[2:]


def _dswa_delta(dout, out, name):
    nc = len(DSWA_DILATIONS)
    T = out.shape[0]
    tb = _tile(T, DSWA_TB)

    def body(do_ref, o_ref, *rest):
        dl_refs, dob_refs, dl_nat = rest[:nc], rest[nc:2 * nc], rest[-1]
        h = pl.program_id(1)
        lane = lax.broadcasted_iota(jnp.int32, (tb, HD), 1)
        mine = jnp.where(lane == h, jnp.sum(do_ref[...] * o_ref[...], axis=-1, keepdims=True), 0.0)

        @pl.when(h == 0)
        def _():
            dl_nat[...] = mine

        @pl.when(h > 0)
        def _():
            dl_nat[...] += mine

        for c, d in enumerate(DSWA_DILATIONS):
            _deinterleave(dl_nat, dl_refs[c], d, F32)
            _deinterleave(do_ref, dob_refs[c], d, BF16)

    nat = pl.BlockSpec((tb, HD), lambda i, h: (i, h))
    by_d = [_relayout_spec(d, tb) for d in DSWA_DILATIONS]
    stat_by_d = [_relayout_spec(d, tb, per_head=False) for d in DSWA_DILATIONS]
    res = pl.pallas_call(
        body, name=name, grid=(T // tb, HEADS),
        in_specs=[nat, nat], out_specs=stat_by_d + by_d,
        out_shape=[jax.ShapeDtypeStruct((d, T // d, HD), F32) for d in DSWA_DILATIONS]
        + [jax.ShapeDtypeStruct((d, T // d, AW), BF16) for d in DSWA_DILATIONS],
        scratch_shapes=[pltpu.VMEM((tb, HD), F32)],
        compiler_params=_params("parallel", "arbitrary"),
    )(dout, out)
    return res[:nc], res[nc:]


def _delta_prep(dout, col_block, out, name, tb=256):
    T = out.shape[0]
    tb = _tile(T, tb)

    def body(do_ref, o_ref, delta_ref, dob_ref):
        delta_ref[...] = jnp.zeros_like(delta_ref)
        for h in range(HEADS):
            sl = slice(h * HD, (h + 1) * HD)
            doh = do_ref[:, sl]
            delta_ref[:, h:h + 1] = jnp.sum(doh * o_ref[:, sl], axis=-1, keepdims=True)
            dob_ref[:, sl] = doh.astype(BF16)

    row = pl.BlockSpec((tb, AW), lambda i: (i, 0))
    return pl.pallas_call(
        body, name=name, grid=(T // tb,),
        in_specs=[pl.BlockSpec((tb, AW), lambda i: (i, col_block)), row],
        out_specs=[pl.BlockSpec((tb, HD), lambda i: (i, 0)), row],
        out_shape=[jax.ShapeDtypeStruct((T, HD), F32), jax.ShapeDtypeStruct((T, AW), BF16)],
        compiler_params=_params("parallel"),
    )(dout, out)


def _dswa_bwd(qkv, dout_b, lse_tot, delta, name, comm=None):
    d, sd = qkv.shape[:2]
    nb = sd // QBLK

    def body(q_ref, kc_ref, kp_ref, vc_ref, vp_ref, do_ref, l_ref, dl_ref, dq_ref, dk_ref, dv_ref, carry_k, carry_v):
        mask = _band_mask(nb - 1 - pl.program_id(1))

        @pl.when(pl.program_id(1) == 0)
        def _():
            carry_k[...] = jnp.zeros_like(carry_k)
            carry_v[...] = jnp.zeros_like(carry_v)

        for h in range(HEADS):
            sl = slice(h * HD, (h + 1) * HD)
            qh, doh = q_ref[:, sl], do_ref[:, sl]
            keys = jnp.concatenate([kp_ref[:, sl], kc_ref[:, sl]], axis=0)
            vals = jnp.concatenate([vp_ref[:, sl], vc_ref[:, sl]], axis=0)
            s = jnp.where(mask, _dot(qh, keys, NT) * SCALE_A, NEG_INF)
            p = jnp.exp(s - l_ref[:, h:h + 1])
            ds = (p * (_dot(doh, vals, NT) - dl_ref[:, h:h + 1]) * SCALE_A).astype(BF16)
            dq_ref[:, sl] = _dot(ds, keys, NN)
            dk = _dot(ds, qh, TN)
            dv = _dot(p.astype(BF16), doh, TN)
            dk_ref[:, sl] = dk[QBLK:] + carry_k[:, sl]
            dv_ref[:, sl] = dv[QBLK:] + carry_v[:, sl]
            carry_k[:, sl] = dk[:QBLK]
            carry_v[:, sl] = dv[:QBLK]

    cur, prev, stat = _dswa_specs(nb, reverse=True)
    return _call(
        body, name, (d, nb), [cur(0), cur(1), prev(1), cur(2), prev(2), cur(0), stat, stat], [cur(0)] * 3,
        [jax.ShapeDtypeStruct((d, sd, AW), F32)] * 3, (qkv, qkv, qkv, qkv, qkv, dout_b, lse_tot, delta),
        scratch=[pltpu.VMEM((QBLK, AW), F32)] * 2, sem=("parallel", "arbitrary"), comm=comm)


def _dswa_combine(grads, cos_t, sin_t_neg, name):
    nc = len(DSWA_DILATIONS)
    T = grads[0][0].shape[0] * grads[0][0].shape[1]
    tb = _tile(T, DSWA_TB)
    half = ROT_A // 2

    def body(*refs):
        g_refs = refs[:3 * nc]
        c_ref, s_ref = refs[3 * nc:3 * nc + 2]
        outs, accs = refs[3 * nc + 2:3 * nc + 5], refs[3 * nc + 5:]
        for which in range(3):
            acc = accs[which]
            for c, d in enumerate(DSWA_DILATIONS):
                g = g_refs[3 * c + which]
                for r in range(d):
                    if c == 0:
                        acc[...] = g[r]
                    else:
                        acc[pl.ds(r, tb // d, stride=d), :] += g[r]
            val = acc[...]
            if which < 2:
                val = _rotate(val, c_ref[...], s_ref[...], half)
            outs[which][...] = val.astype(BF16)

    nat = pl.BlockSpec((tb, HD), lambda i, h: (i, h))
    tab = pl.BlockSpec((tb, HD), lambda i, h: (i, 0))
    in_specs, ins = [], []
    for d, g in zip(DSWA_DILATIONS, grads):
        in_specs += [_relayout_spec(d, tb)] * 3
        ins += list(g)
    return pl.pallas_call(
        body, name=name, grid=(T // tb, HEADS),
        in_specs=in_specs + [tab, tab], out_specs=[nat] * 3,
        out_shape=[jax.ShapeDtypeStruct((T, AW), BF16)] * 3,
        scratch_shapes=[pltpu.VMEM((tb, HD), F32)] * 3,
        compiler_params=_params("parallel", "parallel"),
    )(*ins, cos_t, sin_t_neg)


MLA_TQ = 512
QK = 2 * HD
LOG2E = 1.4426950408889634


def _triangle(nq, key_major):
    pairs = [(q, k) for q in range(nq) for k in range(q + 1)]
    if key_major:
        pairs.sort(key=lambda p: (p[1], p[0]))
    return (jnp.array([p[0] for p in pairs], jnp.int32), jnp.array([p[1] for p in pairs], jnp.int32))


def _mla_specs(tq):
    q_spec = pl.BlockSpec((tq, HEADS * QK), lambda t, qi, ki: (qi[t], 0))
    k_spec = pl.BlockSpec((tq, HEADS * QK), lambda t, qi, ki: (ki[t], 0))
    v_spec = pl.BlockSpec((tq, AW), lambda t, qi, ki: (ki[t], 1))
    qrow = pl.BlockSpec((tq, AW), lambda t, qi, ki: (qi[t], 0))
    krow = pl.BlockSpec((tq, AW), lambda t, qi, ki: (ki[t], 0))
    return q_spec, k_spec, v_spec, qrow, krow


def _mla_stat_spec(tq):
    return pl.BlockSpec((tq, HD), lambda t, qi, ki: (qi[t], 0))


def _mla_scores(q_ref, k_ref, h, qi, ki, tq):
    s = _dot(q_ref[:, h * QK:(h + 1) * QK], k_ref[:, h * QK:(h + 1) * QK], NT) * SCALE_B
    row = lax.broadcasted_iota(jnp.int32, s.shape, 0) + qi * tq
    col = lax.broadcasted_iota(jnp.int32, s.shape, 1) + ki * tq
    return jnp.where(col <= row, s, NEG_INF)


def _mla_fwd(q, k, v1, name, comm=None):
    T = q.shape[0]
    tq = _tile(T, MLA_TQ)
    tables = _triangle(T // tq, False)

    def body(qi_ref, ki_ref, q_ref, k_ref, v_ref, o_ref, ob_ref, l_ref, m_s, acc):
        t = pl.program_id(0)
        qi, ki = qi_ref[t], ki_ref[t]

        @pl.when(ki == 0)
        def _():
            m_s[...] = jnp.full_like(m_s, NEG_INF)
            acc[...] = jnp.zeros_like(acc)

        row = lax.broadcasted_iota(jnp.int32, (tq, tq), 0) + qi * tq
        col = lax.broadcasted_iota(jnp.int32, (tq, tq), 1) + ki * tq
        bias = jnp.where(col <= row, 0.0, NEG_INF)
        updates = []
        for h in range(HEADS):
            s = _dot(q_ref[:, h * QK:(h + 1) * QK], k_ref[:, h * QK:(h + 1) * QK], NT) + bias
            m_new = jnp.maximum(m_s[h], jnp.max(s, axis=-1, keepdims=True))
            p = jnp.exp2((s - m_new) * (SCALE_B * LOG2E)).astype(BF16)
            alpha = jnp.exp2((m_s[h] - m_new) * (SCALE_B * LOG2E))
            updates.append((m_new, alpha, _dot(p, v_ref[:, h * QK:(h + 1) * QK], NN)))
        for h, (m_new, alpha, pv) in enumerate(updates):
            acc[:, h * QK:(h + 1) * QK] = alpha * acc[:, h * QK:(h + 1) * QK] + pv
            m_s[h] = m_new

        @pl.when(ki == qi)
        def _():
            l_ref[...] = jnp.zeros_like(l_ref)
            for h in range(HEADS):
                sl = slice(h * HD, (h + 1) * HD)
                den = acc[:, h * QK + HD:h * QK + HD + 1]
                out = acc[:, h * QK:h * QK + HD] / den
                o_ref[:, sl] = out
                ob_ref[:, sl] = out.astype(BF16)
                l_ref[:, h:h + 1] = m_s[h] * SCALE_B + jnp.log(den)

    q_spec, k_spec, _, qrow, _ = _mla_specs(tq)
    return _call(
        body, name, (tables[0].shape[0],), [q_spec, k_spec, k_spec], [qrow, qrow, _mla_stat_spec(tq)],
        [jax.ShapeDtypeStruct((T, AW), F32), jax.ShapeDtypeStruct((T, AW), BF16), jax.ShapeDtypeStruct((T, HD), F32)],
        (q, k, v1),
        scratch=[pltpu.VMEM((HEADS, tq, 1), F32), pltpu.VMEM((tq, HEADS * QK), F32)],
        sem=("arbitrary",), comm=comm, prefetch=tables)


def _mla_ds(q_ref, k_ref, v_ref, do_ref, l_ref, dl_ref, h, qi, ki, tq):
    sl = slice(h * HD, (h + 1) * HD)
    p = jnp.exp(_mla_scores(q_ref, k_ref, h, qi, ki, tq) - l_ref[:, h:h + 1])
    ds = (p * (_dot(do_ref[:, sl], v_ref[:, sl], NT) - dl_ref[:, h:h + 1]) * SCALE_B).astype(BF16)
    return p, ds


def _mla_bwd_q(q, k, kv, dout_b, lse, delta, name, comm=None):
    T = q.shape[0]
    tq = _tile(T, MLA_TQ)
    tables = _triangle(T // tq, False)

    def body(qi_ref, ki_ref, q_ref, k_ref, v_ref, do_ref, l_ref, dl_ref, dq_ref):
        t = pl.program_id(0)
        qi, ki = qi_ref[t], ki_ref[t]

        @pl.when(ki == 0)
        def _():
            dq_ref[...] = jnp.zeros_like(dq_ref)

        for h in range(HEADS):
            _, ds = _mla_ds(q_ref, k_ref, v_ref, do_ref, l_ref, dl_ref, h, qi, ki, tq)
            dq_ref[:, h * QK:(h + 1) * QK] += _dot(ds, k_ref[:, h * QK:(h + 1) * QK], NN)

    q_spec, k_spec, v_spec, qrow, _ = _mla_specs(tq)
    return _call(
        body, name, (tables[0].shape[0],), [q_spec, k_spec, v_spec, qrow, _mla_stat_spec(tq), _mla_stat_spec(tq)], [q_spec],
        [jax.ShapeDtypeStruct((T, HEADS * QK), F32)], (q, k, kv, dout_b, lse, delta),
        sem=("arbitrary",), comm=comm, prefetch=tables)[0]


def _mla_bwd_kv(q, k, kv, dout_b, lse, delta, name, comm=None):
    T = q.shape[0]
    tq = _tile(T, MLA_TQ)
    tables = _triangle(T // tq, True)

    def body(qi_ref, ki_ref, q_ref, k_ref, v_ref, do_ref, l_ref, dl_ref, dk_ref, dv_ref):
        t = pl.program_id(0)
        qi, ki = qi_ref[t], ki_ref[t]

        @pl.when(qi == ki)
        def _():
            dk_ref[...] = jnp.zeros_like(dk_ref)
            dv_ref[...] = jnp.zeros_like(dv_ref)

        for h in range(HEADS):
            sl = slice(h * HD, (h + 1) * HD)
            p, ds = _mla_ds(q_ref, k_ref, v_ref, do_ref, l_ref, dl_ref, h, qi, ki, tq)
            dv_ref[:, sl] += _dot(p.astype(BF16), do_ref[:, sl], TN)
            dk_ref[:, h * QK:(h + 1) * QK] += _dot(ds, q_ref[:, h * QK:(h + 1) * QK], TN)

    q_spec, k_spec, v_spec, qrow, krow = _mla_specs(tq)
    return _call(
        body, name, (tables[0].shape[0],), [q_spec, k_spec, v_spec, qrow, _mla_stat_spec(tq), _mla_stat_spec(tq)], [k_spec, krow],
        [jax.ShapeDtypeStruct((T, HEADS * QK), F32), jax.ShapeDtypeStruct((T, AW), F32)],
        (q, k, kv, dout_b, lse, delta), sem=("arbitrary",), comm=comm, prefetch=tables)


def _pair_sum(by_device, from_sibling, name, tb=256):
    n_chip, R, C = from_sibling.shape
    tb = _tile(R, tb)
    core = jnp.reshape(lax.axis_index("c"), (1,)).astype(jnp.int32)

    def body(core_ref, mine_ref, theirs_ref, o_ref):
        o_ref[...] = (mine_ref[...].astype(F32) + theirs_ref[...].astype(F32)).astype(o_ref.dtype)

    blk = pl.BlockSpec((None, tb, C), lambda p, i, core_ref: (p, i, 0))
    return _call(
        body, name, (n_chip, R // tb),
        [pl.BlockSpec((None, tb, C), lambda p, i, core_ref: (2 * p + core_ref[0], i, 0)), blk], [blk],
        [jax.ShapeDtypeStruct((n_chip, R, C), BF16)], (by_device, from_sibling),
        sem=("parallel", "parallel"), prefetch=(core,))[0]


def _adamw(parts, w, m, v, name, tb=128, comm=None):
    R, C = w.shape
    n_parts = parts.shape[0]
    tb = _tile(R, tb)
    c1 = 1.0 - ADAM_B1
    c2 = 1.0 - ADAM_B2
    bc1 = 1.0 - ADAM_B1 ** ADAM_STEP
    bc2 = 1.0 - ADAM_B2 ** ADAM_STEP

    def body(p_ref, w_ref, m_ref, v_ref, g_ref, d_ref, nm_ref, nv_ref):
        g = p_ref[0].astype(F32)
        for j in range(1, n_parts):
            g = g + p_ref[j].astype(F32)
        nm = ADAM_B1 * m_ref[...] + c1 * g
        nv = ADAM_B2 * v_ref[...] + c2 * (g * g)
        g_ref[...] = g
        nm_ref[...] = nm
        nv_ref[...] = nv
        d_ref[...] = -ADAM_LR * ((nm / bc1) / (jnp.sqrt(nv / bc2) + ADAM_EPS) + ADAM_WD * w_ref[...])

    row = pl.BlockSpec((tb, C), lambda i: (i, 0))
    return _call(
        body, name, (R // tb,), [pl.BlockSpec((n_parts, tb, C), lambda i: (0, i, 0)), row, row, row], [row] * 4,
        [jax.ShapeDtypeStruct((R, C), F32)] * 4, (parts, w, m, v), sem=("parallel",), comm=comm)


def _cols_from_shards(g):
    return jnp.transpose(g, (1, 0, 2)).reshape(g.shape[1], N_DEV * g.shape[2])


def _cols_to_shards(w):
    return jnp.transpose(w.reshape(w.shape[0], N_DEV, w.shape[1] // N_DEV), (1, 0, 2))


def _split_heads(w, first):
    w3 = w.reshape(w.shape[0], HEADS, -1)
    return w3[:, :, :first].reshape(w.shape[0], -1), w3[:, :, first:].reshape(w.shape[0], -1)


def _join_heads(a, b):
    R = a.shape[0]
    return jnp.concatenate([a.reshape(R, HEADS, -1), b.reshape(R, HEADS, -1)], axis=2).reshape(R, -1)


def _pad_heads(w, width):
    w3 = w.reshape(w.shape[0], HEADS, -1)
    return jnp.pad(w3, ((0, 0), (0, 0), (0, width - w3.shape[2]))).reshape(w.shape[0], HEADS * width)


def _unpad_heads(w, k):
    return w.reshape(w.shape[0], HEADS, -1)[:, :, :k].reshape(w.shape[0], HEADS * k)


def kernel(x, positions, norm_attn_pre, norm_attn_post, w_in, q_latent_norm, kv_latent_norm, w_uq, w_ukv, w_out, norm_mlp_pre, norm_mlp_post, w_up, w_down, loss_target, m_norm_attn_pre, m_norm_attn_post, m_w_in, m_q_latent_norm, m_kv_latent_norm, m_w_uq, m_w_ukv, m_w_out, m_norm_mlp_pre, m_norm_mlp_post, m_w_up, m_w_down, v_norm_attn_pre, v_norm_attn_post, v_w_in, v_q_latent_norm, v_kv_latent_norm, v_w_uq, v_w_ukv, v_w_out, v_norm_mlp_pre, v_norm_mlp_post, v_w_up, v_w_down):
    xs = x[0]
    tgt = loss_target[0]
    pos = positions[0]
    T, D = xs.shape
    big = dict(w_in=(w_in, m_w_in, v_w_in), w_uq=(w_uq, m_w_uq, v_w_uq), w_ukv=(w_ukv, m_w_ukv, v_w_ukv),
               w_out=(w_out, m_w_out, v_w_out), w_up=(w_up, m_w_up, v_w_up), w_down=(w_down, m_w_down, v_w_down))
    big = {n: tuple(t[0] for t in ts) for n, ts in big.items()}
    big_names = ["w_in", "w_uq", "w_ukv", "w_out", "w_up", "w_down"]
    col_sharded = {"w_in", "w_uq", "w_ukv", "w_up"}

    wb = {n: big[n][0].astype(BF16) for n in big_names}

    def gathered(ex, i, n):
        g = ex.results[i]
        return _cols_from_shards(g) if n in col_sharded else g.reshape(-1, g.shape[2])

    def by_device(g, n):
        return _cols_to_shards(g) if n in col_sharded else g.reshape(N_DEV, g.shape[0] // N_DEV, g.shape[1])

    def scatter_of(g, n):
        return _Exchange([by_device(g, n)], "scatter")

    cos_a, sin_a = _rope_tables(pos, ROT_A)
    cos_b, sin_b = _rope_tables(pos, ROPE_MLA)

    ex_in = _Exchange([wb["w_in"]], "gather")
    h1 = _rms_fwd(xs, norm_attn_pre, BF16, "norm_attn_pre_fwd", comm=ex_in)
    Wi = jnp.pad(gathered(ex_in, 0, "w_in"), ((0, 0), (0, IN_PAD - IN_COLS)))
    ex_mid = _Exchange([wb["w_uq"], wb["w_ukv"], wb["w_out"]], "gather")
    proj = _mm(h1, Wi, "nn", [F32], "proj_in", tn=1408, comm=ex_mid)
    Wuq = _pad_heads(gathered(ex_mid, 0, "w_uq"), QK)
    Wukv = jnp.concatenate(_split_heads(gathered(ex_mid, 1, "w_ukv"), HD), axis=1)
    Wo = gathered(ex_mid, 2, "w_out")
    n_piece = wb["w_down"].shape[0] // 8
    ex_down = [_Exchange([wb["w_down"][i * n_piece:(i + 1) * n_piece]], "gather") for i in range(8)]
    qkv_by_d = _rope_dswa(proj, cos_a, sin_a, "rope_dswa", comm=ex_down[0])
    outs, lses = [], []
    for d, qkv, ex in zip(DSWA_DILATIONS, qkv_by_d, ex_down[1:4]):
        o, l = _dswa_fwd(qkv, f"dswa_fwd_d{d}", comm=ex)
        outs.append(o)
        lses.append(l)
    a_out, a_out_b, a_lse_by_d = _dswa_merge(outs, lses, "dswa_merge", comm=ex_down[4:6])

    cqn = _rms_fwd(proj, q_latent_norm, BF16, "q_latent_norm_fwd", width=Q_LORA, col_block=3 * AW // Q_LORA)
    ckvn = _rms_fwd(proj, kv_latent_norm, BF16, "kv_latent_norm_fwd", width=KV_LORA, col_block=3 * AW // KV_LORA + 1)
    qb = _mm(cqn, Wuq, "nn", [F32], "q_up")
    kvb = _mm(ckvn, Wukv, "nn", [BF16], "kv_up")
    odd = lambda j: j % 2 == 1
    q_mla = _rope_apply(qb, cos_b, sin_b, ROPE_MLA // 2, 2 * HEADS, odd, BF16, "rope_mla_q")
    kr = _rope_apply(proj, cos_b, sin_b, ROPE_MLA // 2, 1, lambda j: True, BF16, "rope_mla_k",
                     window=(IN_PAD - HD) // HD)
    k_mla = jnp.concatenate([kvb[:, :AW].reshape(T, HEADS, HD), jnp.broadcast_to(kr[:, None, :], (T, HEADS, HD))],
                            axis=2).reshape(T, HEADS * QK)
    v1_mla = jnp.concatenate([kvb[:, AW:].reshape(T, HEADS, HD), jnp.ones((T, HEADS, 1), BF16),
                              jnp.zeros((T, HEADS, HD - 1), BF16)], axis=2).reshape(T, HEADS * QK)
    ex_up = _Exchange([wb["w_up"]], "gather")
    b_out, b_out_b, b_lse = _mla_fwd(q_mla, k_mla, v1_mla, "mla_fwd", comm=ex_up)
    Wup_shards = ex_up.results[0]

    mixed = jnp.concatenate([a_out_b, b_out_b], axis=1)
    y1 = _mm(mixed, Wo, "nn", [F32], "attn_out")
    x2 = _rms_fwd(y1, norm_attn_post, F32, "norm_attn_post_fwd", residual=xs)

    h2 = _rms_fwd(x2, norm_mlp_pre, BF16, "norm_mlp_pre_fwd")

    def relu2(z):
        r = jnp.maximum(z, 0.0)
        return r * r, r

    u, zr = _mm(h2, Wup_shards, "nn", [BF16, BF16], "mlp_up", epilogue=relu2, comm=ex_down[6:8],
                b_shards=True)
    Wdn = jnp.concatenate([ex.results[0] for ex in ex_down], axis=1).reshape(-1, D)
    y2 = _mm(u, Wdn, "nn", [F32], "mlp_down")
    dx3, loss_part = _loss_head(x2, y2, norm_mlp_post, tgt, "loss_head")

    dy2, dg_mlp_post = _rms_bwd(dx3, y2, norm_mlp_post, BF16, "norm_mlp_post_bwd")
    dz = _mm(dy2, Wdn, "nt", [BF16], "mlp_down_dx", epilogue=lambda du, r: (du * (2.0 * r.astype(F32)),), extras=(zr,))
    g_down = _mm(u, dy2, "tn", [BF16], "mlp_down_dw")
    down_dev = by_device(g_down, "w_down")
    pair_down = _Exchange([down_dev], "pair")
    up_dev = _mm(h2, dz, "tn", [BF16], "mlp_up_dw", comm=pair_down, out_shards=True)
    down_chip = _pair_sum(down_dev, pair_down.results[0], "pair_sum_w_down")
    pair_up = _Exchange([up_dev], "pair")
    sc_down = _Exchange([down_chip], "chips")
    dh2 = _mm(dz, Wup_shards, "nt", [F32], "mlp_up_dx", comm=[pair_up, sc_down], b_shards=True)
    up_chip = _pair_sum(up_dev, pair_up.results[0], "pair_sum_w_up")
    dx2, dg_mlp_pre = _rms_bwd(dh2, x2, norm_mlp_pre, F32, "norm_mlp_pre_bwd", residual=dx3)

    dy1, dg_attn_post = _rms_bwd(dx2, y1, norm_attn_post, BF16, "norm_attn_post_bwd")
    dmixed = _mm(dy1, Wo, "nt", [F32], "attn_out_dx")
    g_out = _mm(mixed, dy1, "tn", [BF16], "attn_out_dw")

    b_delta, b_dout = _delta_prep(dmixed, 1, b_out, "mla_delta")
    sc_up = _Exchange([up_chip], "chips")
    dq_mla = _mla_bwd_q(q_mla, k_mla, kvb, b_dout, b_lse, b_delta, "mla_bwd_q", comm=sc_up)
    sc_out = scatter_of(g_out, "w_out")
    dk_mla, dvb = _mla_bwd_kv(q_mla, k_mla, kvb, b_dout, b_lse, b_delta, "mla_bwd_kv", comm=sc_out)
    dqb = _rope_apply(dq_mla, cos_b, -sin_b, ROPE_MLA // 2, 2 * HEADS, odd, BF16, "rope_mla_q_bwd")
    dkn = dk_mla.reshape(T, HEADS, QK)[:, :, :HD].reshape(T, AW)
    dkvb = jnp.concatenate([dkn, dvb], axis=1).astype(BF16)
    d_kr = _shared_key_grad(dk_mla, cos_b, -sin_b, ROPE_MLA // 2, "rope_mla_k_bwd")
    g_uq_pad = _mm(cqn, dqb, "tn", [BF16], "q_up_dw")
    g_ukv_perm = _mm(ckvn, dkvb, "tn", [BF16], "kv_up_dw")
    dcqn = _mm(dqb, Wuq, "nt", [F32], "q_up_dx")
    dckvn = _mm(dkvb, Wukv, "nt", [F32], "kv_up_dx")
    d_cq, dg_q = _rms_bwd(dcqn, proj, q_latent_norm, BF16, "q_latent_norm_bwd", width=Q_LORA, col_block=3 * AW // Q_LORA)
    d_ckv, dg_kv = _rms_bwd(dckvn, proj, kv_latent_norm, BF16, "kv_latent_norm_bwd", width=KV_LORA,
                            col_block=3 * AW // KV_LORA + 1)

    g_uq = _unpad_heads(g_uq_pad, HD + ROPE_MLA)
    g_ukv = _join_heads(g_ukv_perm[:, :AW], g_ukv_perm[:, AW:])
    sc_uq = _Exchange([_cols_to_shards(g_uq), _cols_to_shards(g_ukv)], "scatter")
    a_delta_by_d, a_dout_by_d = _dswa_delta(dmixed, a_out, "dswa_delta")
    a_grads = [_dswa_bwd(qkv_by_d[c], a_dout_by_d[c], a_lse_by_d[c], a_delta_by_d[c], f"dswa_bwd_d{d}")
               for c, d in enumerate(DSWA_DILATIONS)]
    d_aq, d_ak, d_av = _dswa_combine(a_grads, cos_a, -sin_a, "dswa_combine")

    dproj = jnp.concatenate([d_aq, d_ak, d_av, d_cq, d_ckv, d_kr], axis=1)
    g_in_pad = _mm(h1, dproj, "tn", [BF16], "proj_in_dw", tn=1408, comm=sc_uq)
    in_dev = by_device(g_in_pad[:, :IN_COLS], "w_in")
    pair_in = _Exchange([in_dev], "pair")
    dh1 = _mm(dproj, Wi, "nt", [F32], "proj_in_dx", comm=pair_in)
    in_chip = _pair_sum(in_dev, pair_in.results[0], "pair_sum_w_in")
    cuts = [0, 3 * D // 8, 11 * D // 16, D]
    sc_in = [_Exchange([in_chip[:, a:b]], "chips") for a, b in zip(cuts[:-1], cuts[1:])]
    grad_x, dg_attn_pre = _rms_bwd(dh1, xs, norm_attn_pre, F32, "norm_attn_pre_bwd", residual=dx2, comm=sc_in[0])

    big_out = dict(w_down=_adamw(sc_down.results[0], *big["w_down"], "adamw_w_down", comm=sc_in[1]),
                   w_up=_adamw(sc_up.results[0], *big["w_up"], "adamw_w_up", comm=sc_in[2]))
    parts = dict(w_in=jnp.concatenate([sc.results[0] for sc in sc_in], axis=1), w_uq=sc_uq.results[0],
                 w_ukv=sc_uq.results[1], w_out=sc_out.results[0])
    big_out.update({n: _adamw(parts[n], *big[n], f"adamw_{n}") for n in parts})

    gain_names = ["norm_attn_pre", "norm_attn_post", "q_latent_norm", "kv_latent_norm", "norm_mlp_pre", "norm_mlp_post"]
    gain_args = dict(norm_attn_pre=(norm_attn_pre, m_norm_attn_pre, v_norm_attn_pre),
                     norm_attn_post=(norm_attn_post, m_norm_attn_post, v_norm_attn_post),
                     q_latent_norm=(q_latent_norm, m_q_latent_norm, v_q_latent_norm),
                     kv_latent_norm=(kv_latent_norm, m_kv_latent_norm, v_kv_latent_norm),
                     norm_mlp_pre=(norm_mlp_pre, m_norm_mlp_pre, v_norm_mlp_pre),
                     norm_mlp_post=(norm_mlp_post, m_norm_mlp_post, v_norm_mlp_post))
    gain_grads = dict(norm_attn_pre=dg_attn_pre, norm_attn_post=dg_attn_post, q_latent_norm=dg_q,
                      kv_latent_norm=dg_kv, norm_mlp_pre=dg_mlp_pre, norm_mlp_post=dg_mlp_post)
    packed = jnp.concatenate([gain_grads[n] for n in gain_names], axis=1)
    gain_parts = _Exchange([packed], "gather").standalone("gather_gain_grads")[0]
    pack3 = lambda i: jnp.concatenate([gain_args[n][i] for n in gain_names], axis=1)
    gain_out = _adamw(gain_parts, pack3(0), pack3(1), pack3(2), "adamw_gains", tb=1)
    offs = [0]
    for n in gain_names:
        offs.append(offs[-1] + gain_args[n][0].shape[1])
    small_out = {n: tuple(o[:, offs[i]:offs[i + 1]] for o in gain_out) for i, n in enumerate(gain_names)}

    loss = lax.psum(loss_part[0, 0], ("x", "y", "c"))

    order = ["norm_attn_pre", "norm_attn_post", "w_in", "q_latent_norm", "kv_latent_norm", "w_uq", "w_ukv", "w_out",
             "norm_mlp_pre", "norm_mlp_post", "w_up", "w_down"]
    res = {n: (small_out[n] if n in small_out else tuple(o[None] for o in big_out[n])) for n in order}
    return (loss, grad_x[None], *[res[n][0] for n in order], *[res[n][1] for n in order],
            *[res[n][2] for n in order], *[res[n][3] for n in order])
```

```python
import functools
import math

import jax
import jax.numpy as jnp
from jax import lax
from jax.experimental import pallas as pl
from jax.experimental.pallas import tpu as pltpu

F32 = jnp.float32
BF16 = jnp.bfloat16

N_DEV = 8
HEADS = 8
HD = 128
AW = HEADS * HD
Q_LORA = 512
KV_LORA = 512
ROPE_MLA = 64
ROT_A = 32
IN_COLS = 3 * AW + Q_LORA + KV_LORA + ROPE_MLA
IN_PAD = 3 * AW + Q_LORA + KV_LORA + HD
QBLK = 128
DSWA_DILATIONS = (1, 4, 16)
ROPE_THETA = 500000.0
NORM_EPS = 1e-6
NEG_INF = -1e30
SCALE_A = HD ** -0.5
SCALE_B = (HD + ROPE_MLA) ** -0.5

ADAM_LR = 0.001
ADAM_B1 = 0.9
ADAM_B2 = 0.999
ADAM_EPS = 1e-08
ADAM_WD = 0.01
ADAM_STEP = 10

VMEM_LIMIT = 48 * 1024 * 1024

NT = (((1,), (1,)), ((), ()))
NN = (((1,), (0,)), ((), ()))
TN = (((0,), (0,)), ((), ()))


def _dot(a, b, dims):
    return lax.dot_general(a, b, dims, preferred_element_type=F32)


def _params(*sem):
    return pltpu.CompilerParams(dimension_semantics=sem, vmem_limit_bytes=VMEM_LIMIT)


def _tile(n, want):
    t = min(n, want)
    while n % t:
        t //= 2
    return t


def _tile128(n, want):
    if n % 128:
        return n
    units = n // 128
    return 128 * max(u for u in range(1, max(want // 128, 1) + 1) if units % u == 0)


class _Exchange:
    def __init__(self, arrs, mode):
        self.arrs = list(arrs)
        self.mode = mode
        self.n = len(self.arrs)
        self.results = None
        hbm = pl.BlockSpec(memory_space=pltpu.HBM)
        self.specs = [hbm] * self.n
        shape = {"gather": lambda a: (N_DEV,) + a.shape, "scatter": lambda a: a.shape,
                 "pair": lambda a: (4,) + a.shape[1:], "chips": lambda a: a.shape}[mode]
        self.out_shape = [jax.ShapeDtypeStruct(shape(a), a.dtype) for a in self.arrs]
        n_sem = self.n * (N_DEV - 1)
        self.scratch = [pltpu.SemaphoreType.DMA((n_sem,)), pltpu.SemaphoreType.DMA((n_sem,)),
                        pltpu.SemaphoreType.DMA((self.n,))]

    def hooks(self, ins, outs, send_sems, recv_sems, local_sems):
        x, y, c = lax.axis_index("x"), lax.axis_index("y"), lax.axis_index("c")
        me = (x, y, c)
        sib = (x, y, 1 - c)
        chips = [(1 - x, y), (x, 1 - y), (1 - x, 1 - y)]
        slot = lambda p: 4 * p[0] + 2 * p[1] + p[2]
        chip_of = lambda p: 2 * p[0] + p[1]

        def rcopy(a, k, src, dst, to):
            i = a * (N_DEV - 1) + k
            return pltpu.make_async_remote_copy(src_ref=src, dst_ref=dst, send_sem=send_sems.at[i],
                                                recv_sem=recv_sems.at[i], device_id=to,
                                                device_id_type=pl.DeviceIdType.MESH)

        def local(a):
            if self.mode == "chips":
                return pltpu.make_async_copy(ins[a].at[chip_of(me)], outs[a].at[chip_of(me)], local_sems.at[a])
            src = ins[a].at[slot(me)] if self.mode == "scatter" else ins[a]
            return pltpu.make_async_copy(src, outs[a].at[slot(me)], local_sems.at[a])

        def peer(rel):
            return (1 - x if rel & 4 else x, 1 - y if rel & 2 else y, 1 - c if rel & 1 else c)

        if self.mode == "pair":
            def start():
                for a in range(self.n):
                    for p in range(4):
                        rcopy(a, p, ins[a].at[2 * p + 1 - c], outs[a].at[p], sib).start()

            def middle():
                pass

            def finish():
                for a in range(self.n):
                    for p in range(4):
                        cp = rcopy(a, p, ins[a].at[2 * p + 1 - c], outs[a].at[p], sib)
                        cp.wait_send()
                        cp.wait_recv()
        elif self.mode == "chips":
            def start():
                for a in range(self.n):
                    local(a).start()
                    for j, chip in enumerate(chips):
                        rcopy(a, j, ins[a].at[chip_of(chip)], outs[a].at[chip_of(me)], (*chip, c)).start()

            def middle():
                pass

            def finish():
                for a in range(self.n):
                    for j, chip in enumerate(chips):
                        cp = rcopy(a, j, ins[a].at[chip_of(chip)], outs[a].at[chip_of(chip)], (*chip, c))
                        cp.wait_send()
                        cp.wait_recv()
                    local(a).wait()
        elif self.mode == "scatter":
            def start():
                for a in range(self.n):
                    local(a).start()
                    for rel in range(1, N_DEV):
                        rcopy(a, rel - 1, ins[a].at[slot(peer(rel))], outs[a].at[slot(me)], peer(rel)).start()

            def middle():
                pass

            def finish():
                for a in range(self.n):
                    for rel in range(1, N_DEV):
                        cp = rcopy(a, rel - 1, ins[a].at[slot(peer(rel))], outs[a].at[slot(peer(rel))], peer(rel))
                        cp.wait_send()
                        cp.wait_recv()
                    local(a).wait()
        else:
            def start():
                for a in range(self.n):
                    local(a).start()
                    rcopy(a, 0, ins[a], outs[a].at[slot(me)], sib).start()
                    for j, chip in enumerate(chips):
                        rcopy(a, 1 + j, ins[a], outs[a].at[slot(me)], (*chip, c)).start()

            def middle():
                for a in range(self.n):
                    for j, chip in enumerate(chips):
                        landed = outs[a].at[slot((*chip, c))]
                        rcopy(a, 1 + j, ins[a], landed, me).wait_recv()
                        rcopy(a, 4 + j, landed, landed, sib).start()

            def finish():
                for a in range(self.n):
                    rcopy(a, 0, ins[a], outs[a].at[slot(sib)], me).wait_recv()
                    for j, chip in enumerate(chips):
                        rcopy(a, 4 + j, ins[a], outs[a].at[slot((*chip, 1 - c))], me).wait_recv()
                    for k in range(N_DEV - 1):
                        rcopy(a, k, ins[a], outs[a].at[slot(me)], me).wait_send()
                    local(a).wait()

        return start, middle, finish

    def set_results(self, res):
        self.results = list(res)

    def standalone(self, name):
        n = self.n

        def body(*refs):
            start, middle, finish = self.hooks(refs[:n], refs[n:2 * n], *refs[2 * n:])
            start()
            middle()
            finish()

        self.results = pl.pallas_call(
            body, name=name, in_specs=self.specs, out_specs=self.specs, out_shape=self.out_shape,
            scratch_shapes=self.scratch, compiler_params=pltpu.CompilerParams(has_side_effects=True),
        )(*self.arrs)
        return self.results


class _Carried:
    def __init__(self, parts):
        self.parts = list(parts)
        self.n = sum(p.n for p in self.parts)
        self.arrs = [a for p in self.parts for a in p.arrs]
        self.specs = [s for p in self.parts for s in p.specs]
        self.out_shape = [s for p in self.parts for s in p.out_shape]
        self.scratch = [s for p in self.parts for s in p.scratch]

    def hooks(self, ins, outs, *sems):
        hooks, i = [], 0
        for j, p in enumerate(self.parts):
            hooks.append(p.hooks(ins[i:i + p.n], outs[i:i + p.n], *sems[3 * j:3 * j + 3]))
            i += p.n
        def phase(k):
            def run():
                for h in hooks:
                    h[k]()
            return run

        return phase(0), phase(1), phase(2)

    def set_results(self, res):
        i = 0
        for p in self.parts:
            p.set_results(res[i:i + p.n])
            i += p.n


def _call(body, name, grid, in_specs, out_specs, out_shape, args, scratch=(), sem=(), comm=None, prefetch=()):
    npf = len(prefetch)
    if isinstance(comm, (list, tuple)):
        comm = _Carried(comm)
    if comm is None:
        spec = pltpu.PrefetchScalarGridSpec(num_scalar_prefetch=npf, grid=grid, in_specs=list(in_specs),
                                            out_specs=list(out_specs), scratch_shapes=list(scratch))
        return pl.pallas_call(body, name=name, grid_spec=spec, out_shape=list(out_shape),
                              compiler_params=_params(*sem))(*prefetch, *args)
    ni, no, ns, n = len(in_specs), len(out_specs), len(scratch), comm.n
    steps = math.prod(grid)

    def wrapped(*refs):
        pf, refs = refs[:npf], refs[npf:]
        ins, c_ins = refs[:ni], refs[ni:ni + n]
        outs, c_outs = refs[ni + n:ni + n + no], refs[ni + n + no:ni + 2 * n + no]
        scr, c_scr = refs[ni + 2 * n + no:ni + 2 * n + no + ns], refs[ni + 2 * n + no + ns:]
        start, middle, finish = comm.hooks(c_ins, c_outs, *c_scr)
        step = pl.program_id(0)
        for ax in range(1, len(grid)):
            step = step * grid[ax] + pl.program_id(ax)
        pl.when(step == 0)(start)
        pl.when(step == steps // 2)(middle)
        body(*pf, *ins, *outs, *scr)
        pl.when(step == steps - 1)(finish)

    spec = pltpu.PrefetchScalarGridSpec(num_scalar_prefetch=npf, grid=grid, in_specs=list(in_specs) + comm.specs,
                                        out_specs=list(out_specs) + comm.specs,
                                        scratch_shapes=list(scratch) + comm.scratch)
    res = pl.pallas_call(
        wrapped, name=name, grid_spec=spec, out_shape=list(out_shape) + comm.out_shape,
        compiler_params=pltpu.CompilerParams(dimension_semantics=("arbitrary",) * len(grid),
                                             vmem_limit_bytes=VMEM_LIMIT, has_side_effects=True),
    )(*prefetch, *args, *comm.arrs)
    comm.set_results(res[no:])
    return res[:no]


def _mm(a, b, mode, out_dtypes, name, epilogue=None, extras=(), tm=1024, tn=1024, tk=2048, comm=None,
        b_shards=False, out_shards=False):
    if mode == "tn":
        K, M = a.shape
    else:
        M, K = a.shape
    if b_shards:
        N = b.shape[1] if mode == "nt" else N_DEV * b.shape[2]
    else:
        N = b.shape[0] if mode == "nt" else b.shape[1]
    tm, tn, tk = _tile128(M, tm), _tile128(N, tn), _tile128(K, tk)
    pair_k = b_shards and mode == "nt"
    if pair_k:
        tk = 2 * K // N_DEV
        b = b.reshape(N_DEV // 2, 2, *b.shape[1:])
    elif b_shards or out_shards:
        tn = N // N_DEV
    nk = K // tk
    dims = {"nn": NN, "nt": NT, "tn": TN}[mode]
    a_spec = (pl.BlockSpec((tk, tm), lambda i, j, k: (k, i)) if mode == "tn"
              else pl.BlockSpec((tm, tk), lambda i, j, k: (i, k)))
    if b_shards:
        b_spec = (pl.BlockSpec((None, 2, tn, tk // 2), lambda i, j, k: (k, 0, j, 0)) if mode == "nt"
                  else pl.BlockSpec((None, tk, tn), lambda i, j, k: (j, k, 0)))
    else:
        b_spec = (pl.BlockSpec((tn, tk), lambda i, j, k: (j, k)) if mode == "nt"
                  else pl.BlockSpec((tk, tn), lambda i, j, k: (k, j)))
    mn_spec = pl.BlockSpec((tm, tn), lambda i, j, k: (i, j))
    out_spec = pl.BlockSpec((None, tm, tn), lambda i, j, k: (j, i, 0)) if out_shards else mn_spec
    out_dims = (N_DEV, M, N // N_DEV) if out_shards else (M, N)
    n_ex = len(extras)
    n_out = len(out_dtypes)

    def finish(acc, ex, outs):
        res = (acc,) if epilogue is None else epilogue(acc, *[e[...] for e in ex])
        for o, r in zip(outs, res):
            o[...] = r.astype(o.dtype)

    def product(a_ref, b_ref):
        if pair_k:
            return _dot(a_ref[:, :tk // 2], b_ref[0], NT) + _dot(a_ref[:, tk // 2:], b_ref[1], NT)
        return _dot(a_ref[...], b_ref[...], dims)

    def body(*refs):
        a_ref, b_ref = refs[:2]
        ex = refs[2:2 + n_ex]
        outs = refs[2 + n_ex:2 + n_ex + n_out]
        if nk == 1:
            finish(product(a_ref, b_ref), ex, outs)
            return
        acc = refs[-1]
        k = pl.program_id(2)

        @pl.when(k == 0)
        def _():
            acc[...] = product(a_ref, b_ref)

        @pl.when(jnp.logical_and(k > 0, k < nk - 1))
        def _():
            acc[...] += product(a_ref, b_ref)

        @pl.when(k == nk - 1)
        def _():
            finish(acc[...] + product(a_ref, b_ref), ex, outs)

    out = _call(
        body, name, (M // tm, N // tn, nk), [a_spec, b_spec] + [mn_spec] * n_ex, [out_spec] * n_out,
        [jax.ShapeDtypeStruct(out_dims, dt) for dt in out_dtypes], (a, b, *extras),
        scratch=[] if nk == 1 else [pltpu.VMEM((tm, tn), F32)], sem=("parallel", "parallel", "arbitrary"),
        comm=comm)
    return out[0] if n_out == 1 else out


def _rms_fwd(x, gain, out_dtype, name, width=None, col_block=0, residual=None, tb=256, comm=None):
    T = x.shape[0]
    W = x.shape[1] if width is None else width
    tb = _tile(T, tb)
    has_res = residual is not None

    def body(*refs):
        x_ref, g_ref = refs[:2]
        o_ref = refs[-1]
        xf = x_ref[...]
        y = xf * lax.rsqrt(jnp.mean(xf * xf, axis=-1, keepdims=True) + NORM_EPS) * g_ref[...]
        if has_res:
            y = refs[2][...] + y
        o_ref[...] = y.astype(o_ref.dtype)

    row = pl.BlockSpec((tb, W), lambda i: (i, 0))
    ins = [x, gain] + ([residual] if has_res else [])
    return _call(
        body, name, (T // tb,),
        [pl.BlockSpec((tb, W), lambda i: (i, col_block)),
         pl.BlockSpec((1, W), lambda i: (0, 0))] + ([row] if has_res else []),
        [row], [jax.ShapeDtypeStruct((T, W), out_dtype)], ins, sem=("parallel",), comm=comm)[0]


def _rms_bwd(dy, x, gain, out_dtype, name, width=None, col_block=0, residual=None, tb=256, comm=None):
    T = dy.shape[0]
    W = x.shape[1] if width is None else width
    tb = _tile(T, tb)
    has_res = residual is not None

    def body(*refs):
        dy_ref, x_ref, g_ref = refs[:3]
        dx_ref, dg_ref = refs[-2:]
        i = pl.program_id(0)
        xf = x_ref[...]
        r = lax.rsqrt(jnp.mean(xf * xf, axis=-1, keepdims=True) + NORM_EPS)
        xn = xf * r
        dyf = dy_ref[...].astype(F32)
        dyg = dyf * g_ref[...]
        dx = r * (dyg - xn * jnp.mean(dyg * xn, axis=-1, keepdims=True))
        if has_res:
            dx = refs[3][...] + dx
        dx_ref[...] = dx.astype(dx_ref.dtype)

        @pl.when(i == 0)
        def _():
            dg_ref[...] = jnp.zeros_like(dg_ref)

        dg_ref[...] += jnp.sum(dyf * xn, axis=0, keepdims=True)

    row = pl.BlockSpec((tb, W), lambda i: (i, 0))
    vec = pl.BlockSpec((1, W), lambda i: (0, 0))
    ins = [dy, x, gain] + ([residual] if has_res else [])
    return _call(
        body, name, (T // tb,),
        [row, pl.BlockSpec((tb, W), lambda i: (i, col_block)), vec] + ([row] if has_res else []), [row, vec],
        [jax.ShapeDtypeStruct((T, W), out_dtype), jax.ShapeDtypeStruct((1, W), F32)], ins,
        sem=("arbitrary",), comm=comm)


def _rms(xf):
    r = lax.rsqrt(jnp.mean(xf * xf, axis=-1, keepdims=True) + NORM_EPS)
    return r, xf * r


def _rms_grad(dyf, xn, r, gain):
    dyg = dyf * gain
    return r * (dyg - xn * jnp.mean(dyg * xn, axis=-1, keepdims=True)), dyf * xn


def _accumulate_rows(i, ref, rows):
    @pl.when(i == 0)
    def _():
        ref[...] = jnp.zeros_like(ref)

    ref[...] += jnp.sum(rows, axis=0, keepdims=True)


def _norm_pair_fwd(y1, xs, gain_post, gain_pre, name, tb=256):
    T, D = xs.shape
    tb = _tile(T, tb)

    def body(y1_ref, xs_ref, gp_ref, gq_ref, x2_ref, h2_ref):
        x2 = xs_ref[...] + _rms(y1_ref[...])[1] * gp_ref[...]
        x2_ref[...] = x2
        h2_ref[...] = (_rms(x2)[1] * gq_ref[...]).astype(BF16)

    row = pl.BlockSpec((tb, D), lambda i: (i, 0))
    vec = pl.BlockSpec((1, D), lambda i: (0, 0))
    return pl.pallas_call(
        body, name=name, grid=(T // tb,), in_specs=[row, row, vec, vec], out_specs=[row, row],
        out_shape=[jax.ShapeDtypeStruct((T, D), F32), jax.ShapeDtypeStruct((T, D), BF16)],
        compiler_params=_params("parallel"),
    )(y1, xs, gain_post, gain_pre)


def _norm_pair_bwd(dh2, x2, gain_pre, dx3, y1, gain_post, name, tb=256):
    T, D = x2.shape
    tb = _tile(T, tb)

    def body(dh2_ref, x2_ref, gq_ref, dx3_ref, y1_ref, gp_ref, dx2_ref, dy1_ref, dgq_ref, dgp_ref):
        i = pl.program_id(0)
        r2, xn2 = _rms(x2_ref[...])
        d2, rows_q = _rms_grad(dh2_ref[...], xn2, r2, gq_ref[...])
        dx2 = dx3_ref[...] + d2
        dx2_ref[...] = dx2
        r1, yn1 = _rms(y1_ref[...])
        d1, rows_p = _rms_grad(dx2, yn1, r1, gp_ref[...])
        dy1_ref[...] = d1.astype(BF16)
        _accumulate_rows(i, dgq_ref, rows_q)
        _accumulate_rows(i, dgp_ref, rows_p)

    row = pl.BlockSpec((tb, D), lambda i: (i, 0))
    vec = pl.BlockSpec((1, D), lambda i: (0, 0))
    return pl.pallas_call(
        body, name=name, grid=(T // tb,), in_specs=[row, row, vec, row, row, vec], out_specs=[row, row, vec, vec],
        out_shape=[jax.ShapeDtypeStruct((T, D), F32), jax.ShapeDtypeStruct((T, D), BF16),
                   jax.ShapeDtypeStruct((1, D), F32), jax.ShapeDtypeStruct((1, D), F32)],
        compiler_params=_params("arbitrary"),
    )(dh2, x2, gain_pre, dx3, y1, gain_post)


def _loss_head(x2, y2, gain, target, name, tb=256):
    T, D = x2.shape
    tb = _tile(T, tb)

    def body(x2_ref, y2_ref, g_ref, t_ref, dx3_ref, dy2_ref, loss_ref, dg_ref):
        i = pl.program_id(0)
        r, yn = _rms(y2_ref[...])
        e = x2_ref[...] + yn * g_ref[...] - t_ref[...]
        dx3 = e * (1.0 / D)
        dx3_ref[...] = dx3
        dy2, rows = _rms_grad(dx3, yn, r, g_ref[...])
        dy2_ref[...] = dy2.astype(BF16)
        _accumulate_rows(i, dg_ref, rows)
        _accumulate_rows(i, loss_ref, 0.5 * jnp.mean(e * e, axis=-1, keepdims=True))

    row = pl.BlockSpec((tb, D), lambda i: (i, 0))
    vec = pl.BlockSpec((1, D), lambda i: (0, 0))
    return pl.pallas_call(
        body, name=name, grid=(T // tb,),
        in_specs=[row, row, vec, row],
        out_specs=[row, row, pl.BlockSpec((1, 1), lambda i: (0, 0)), vec],
        out_shape=[jax.ShapeDtypeStruct((T, D), F32), jax.ShapeDtypeStruct((T, D), BF16),
                   jax.ShapeDtypeStruct((1, 1), F32), jax.ShapeDtypeStruct((1, D), F32)],
        compiler_params=_params("arbitrary"),
    )(x2, y2, gain, target)


def _rope_tables(positions, rot_dim):
    half = rot_dim // 2
    inv_freq = ROPE_THETA ** (-jnp.arange(0, rot_dim, 2, dtype=F32) / rot_dim)
    ang = positions.astype(F32)[:, None] * inv_freq[None, :]
    cos, sin = jnp.cos(ang), jnp.sin(ang)
    T = positions.shape[0]
    ones = jnp.ones((T, HD - rot_dim), F32)
    cos_t = jnp.concatenate([cos, cos, ones], axis=1)
    sin_t = jnp.concatenate([-sin, sin, jnp.zeros_like(ones)], axis=1)
    return cos_t, sin_t


def _rotate(x, cos_t, sin_t, half):
    lane = lax.broadcasted_iota(jnp.int32, x.shape, 1)
    swapped = jnp.where(lane < half, pltpu.roll(x, HD - half, 1), pltpu.roll(x, half, 1))
    return x * cos_t + swapped * sin_t


def _rope_apply(x, cos_t, sin_t, half, n_blocks, is_rope, out_dtype, name, window=0, tb=256):
    T = x.shape[0]
    tb = _tile(T, tb)
    W = n_blocks * HD

    def body(x_ref, c_ref, s_ref, o_ref):
        for j in range(n_blocks):
            sl = slice(j * HD, (j + 1) * HD)
            xj = x_ref[:, sl]
            if is_rope(j):
                xj = _rotate(xj.astype(F32), c_ref[...], s_ref[...], half)
            o_ref[:, sl] = xj.astype(o_ref.dtype)

    tab = pl.BlockSpec((tb, HD), lambda i: (i, 0))
    return pl.pallas_call(
        body, name=name, grid=(T // tb,),
        in_specs=[pl.BlockSpec((tb, W), lambda i: (i, window)), tab, tab],
        out_specs=pl.BlockSpec((tb, W), lambda i: (i, 0)),
        out_shape=jax.ShapeDtypeStruct((T, W), out_dtype),
        compiler_params=_params("parallel"),
    )(x, cos_t, sin_t)


DSWA_TB = 2048


def _deinterleave(src, dst_ref, d, dtype):
    rows = src.shape[0] // d
    for r in range(d):
        dst_ref[r] = src[pl.ds(r, rows, stride=d), :].astype(dtype)


def _rope_dswa(proj, cos_t, sin_t, name, comm=None):
    T = proj.shape[0]
    tb = _tile(T, DSWA_TB)
    half = ROT_A // 2

    def body(x_ref, c_ref, s_ref, *rest):
        outs, scr = rest[:-1], rest[-1]
        j = pl.program_id(1)

        @pl.when(j < 2 * HEADS)
        def _():
            scr[...] = _rotate(x_ref[...], c_ref[...], s_ref[...], half)

        @pl.when(j >= 2 * HEADS)
        def _():
            scr[...] = x_ref[...]

        for o_ref, d in zip(outs, DSWA_DILATIONS):
            _deinterleave(scr, o_ref, d, BF16)

    blk = pl.BlockSpec((tb, HD), lambda i, j: (i, j))
    tab = pl.BlockSpec((tb, HD), lambda i, j: (i, 0))
    return _call(
        body, name, (T // tb, 3 * HEADS), [blk, tab, tab],
        [pl.BlockSpec((d, tb // d, HD), lambda i, j: (0, i, j)) for d in DSWA_DILATIONS],
        [jax.ShapeDtypeStruct((d, T // d, 3 * AW), BF16) for d in DSWA_DILATIONS], (proj, cos_t, sin_t),
        scratch=[pltpu.VMEM((tb, HD), F32)], sem=("parallel", "parallel"), comm=comm)


def _shared_key_grad(dk, cos_t, sin_t_neg, half, name, tb=512):
    T = dk.shape[0]
    tb = _tile(T, tb)

    def body(d_ref, c_ref, s_ref, o_ref):
        tot = d_ref[:, HD:2 * HD]
        for h in range(1, HEADS):
            tot = tot + d_ref[:, h * QK + HD:(h + 1) * QK]
        o_ref[...] = _rotate(tot, c_ref[...], s_ref[...], half).astype(o_ref.dtype)

    tab = pl.BlockSpec((tb, HD), lambda i: (i, 0))
    return pl.pallas_call(
        body, name=name, grid=(T // tb,),
        in_specs=[pl.BlockSpec((tb, HEADS * QK), lambda i: (i, 0)), tab, tab],
        out_specs=tab, out_shape=jax.ShapeDtypeStruct((T, HD), BF16),
        compiler_params=_params("parallel"),
    )(dk, cos_t, sin_t_neg)


def _band_mask(n):
    row = lax.broadcasted_iota(jnp.int32, (QBLK, 2 * QBLK), 0)
    col = lax.broadcasted_iota(jnp.int32, (QBLK, 2 * QBLK), 1)
    in_prev = jnp.logical_and(jnp.logical_and(col < QBLK, col >= row), n > 0)
    in_cur = jnp.logical_and(col >= QBLK, col - QBLK <= row)
    return jnp.logical_or(in_prev, in_cur)


def _dswa_specs(nb, reverse=False):
    pos = (lambda n: nb - 1 - n) if reverse else (lambda n: n)
    cur = lambda c: pl.BlockSpec((None, QBLK, AW), lambda r, n: (r, pos(n), c))
    prev = lambda c: pl.BlockSpec((None, QBLK, AW), lambda r, n: (r, jnp.maximum(pos(n) - 1, 0), c))
    stat = pl.BlockSpec((None, QBLK, HD), lambda r, n: (r, pos(n), 0))
    return cur, prev, stat


def _relayout_spec(d, tb, per_head=True):
    if per_head:
        return pl.BlockSpec((d, tb // d, HD), lambda i, h: (0, i, h))
    return pl.BlockSpec((d, tb // d, HD), lambda i, h: (0, i, 0))


def _head_lane(x, h):
    lane = lax.broadcasted_iota(jnp.int32, x.shape, 1)
    return jnp.sum(jnp.where(lane == h, x, 0.0), axis=-1, keepdims=True)


def _dswa_fwd(qkv, name, comm=None):
    d, sd = qkv.shape[:2]
    nb = sd // QBLK

    def body(q_ref, kc_ref, kp_ref, vc_ref, vp_ref, o_ref, l_ref):
        mask = _band_mask(pl.program_id(1))
        l_ref[...] = jnp.zeros_like(l_ref)
        for h in range(HEADS):
            sl = slice(h * HD, (h + 1) * HD)
            keys = jnp.concatenate([kp_ref[:, sl], kc_ref[:, sl]], axis=0)
            vals = jnp.concatenate([vp_ref[:, sl], vc_ref[:, sl]], axis=0)
            s = jnp.where(mask, _dot(q_ref[:, sl], keys, NT) * SCALE_A, NEG_INF)
            m = jnp.max(s, axis=-1, keepdims=True)
            p = jnp.exp(s - m)
            den = jnp.sum(p, axis=-1, keepdims=True)
            o_ref[:, sl] = _dot((p / den).astype(BF16), vals, NN)
            l_ref[:, h:h + 1] = m + jnp.log(den)

    cur, prev, stat = _dswa_specs(nb)
    return _call(
        body, name, (d, nb), [cur(0), cur(1), prev(1), cur(2), prev(2)], [cur(0), stat],
        [jax.ShapeDtypeStruct((d, sd, AW), F32), jax.ShapeDtypeStruct((d, sd, HD), F32)],
        (qkv, qkv, qkv, qkv, qkv), sem=("parallel", "parallel"), comm=comm)


def _dswa_merge(outs, lses, name, comm=None):
    nc = len(DSWA_DILATIONS)
    T = outs[0].shape[0] * outs[0].shape[1]
    tb = _tile(T, DSWA_TB)

    def body(*refs):
        o_refs, l_refs = refs[:nc], refs[nc:2 * nc]
        out_ref, outb_ref = refs[2 * nc:2 * nc + 2]
        lt_refs = refs[2 * nc + 2:3 * nc + 2]
        o_nat, l_nat, lt_nat = refs[3 * nc + 2:4 * nc + 2], refs[4 * nc + 2:5 * nc + 2], refs[-1]
        h = pl.program_id(1)
        for c, d in enumerate(DSWA_DILATIONS):
            for r in range(d):
                o_nat[c][pl.ds(r, tb // d, stride=d), :] = o_refs[c][r]
                l_nat[c][pl.ds(r, tb // d, stride=d), :] = l_refs[c][r]
        ls = [l[...] for l in l_nat]
        m = functools.reduce(jnp.maximum, ls)
        es = [jnp.exp(l - m) for l in ls]
        tot = functools.reduce(lambda a, b: a + b, es)
        acc = _head_lane(es[0] / tot, h) * o_nat[0][...]
        for c in range(1, nc):
            acc = acc + _head_lane(es[c] / tot, h) * o_nat[c][...]
        out_ref[...] = acc
        outb_ref[...] = acc.astype(BF16)
        lt_nat[...] = m + jnp.log(tot)
        for c, d in enumerate(DSWA_DILATIONS):
            _deinterleave(lt_nat, lt_refs[c], d, F32)

    nat = pl.BlockSpec((tb, HD), lambda i, h: (i, h))
    by_d = [_relayout_spec(d, tb) for d in DSWA_DILATIONS]
    stat_by_d = [_relayout_spec(d, tb, per_head=False) for d in DSWA_DILATIONS]
    res = _call(
        body, name, (T // tb, HEADS), by_d + stat_by_d, [nat, nat] + stat_by_d,
        [jax.ShapeDtypeStruct((T, AW), F32), jax.ShapeDtypeStruct((T, AW), BF16)]
        + [jax.ShapeDtypeStruct((d, T // d, HD), F32) for d in DSWA_DILATIONS], (*outs, *lses),
        scratch=[pltpu.VMEM((tb, HD), F32)] * (2 * nc + 1), sem=("parallel", "arbitrary"), comm=comm)
    return res[0], res[1], res[2:]


def _dswa_delta(dout, out, name):
    nc = len(DSWA_DILATIONS)
    T = out.shape[0]
    tb = _tile(T, DSWA_TB)

    def body(do_ref, o_ref, *rest):
        dl_refs, dob_refs, dl_nat = rest[:nc], rest[nc:2 * nc], rest[-1]
        h = pl.program_id(1)
        lane = lax.broadcasted_iota(jnp.int32, (tb, HD), 1)
        mine = jnp.where(lane == h, jnp.sum(do_ref[...] * o_ref[...], axis=-1, keepdims=True), 0.0)

        @pl.when(h == 0)
        def _():
            dl_nat[...] = mine

        @pl.when(h > 0)
        def _():
            dl_nat[...] += mine

        for c, d in enumerate(DSWA_DILATIONS):
            _deinterleave(dl_nat, dl_refs[c], d, F32)
            _deinterleave(do_ref, dob_refs[c], d, BF16)

    nat = pl.BlockSpec((tb, HD), lambda i, h: (i, h))
    by_d = [_relayout_spec(d, tb) for d in DSWA_DILATIONS]
    stat_by_d = [_relayout_spec(d, tb, per_head=False) for d in DSWA_DILATIONS]
    res = pl.pallas_call(
        body, name=name, grid=(T // tb, HEADS),
        in_specs=[nat, nat], out_specs=stat_by_d + by_d,
        out_shape=[jax.ShapeDtypeStruct((d, T // d, HD), F32) for d in DSWA_DILATIONS]
        + [jax.ShapeDtypeStruct((d, T // d, AW), BF16) for d in DSWA_DILATIONS],
        scratch_shapes=[pltpu.VMEM((tb, HD), F32)],
        compiler_params=_params("parallel", "arbitrary"),
    )(dout, out)
    return res[:nc], res[nc:]


def _delta_prep(dout, col_block, out, name, tb=256):
    T = out.shape[0]
    tb = _tile(T, tb)

    def body(do_ref, o_ref, delta_ref, dob_ref):
        delta_ref[...] = jnp.zeros_like(delta_ref)
        for h in range(HEADS):
            sl = slice(h * HD, (h + 1) * HD)
            doh = do_ref[:, sl]
            delta_ref[:, h:h + 1] = jnp.sum(doh * o_ref[:, sl], axis=-1, keepdims=True)
            dob_ref[:, sl] = doh.astype(BF16)

    row = pl.BlockSpec((tb, AW), lambda i: (i, 0))
    return pl.pallas_call(
        body, name=name, grid=(T // tb,),
        in_specs=[pl.BlockSpec((tb, AW), lambda i: (i, col_block)), row],
        out_specs=[pl.BlockSpec((tb, HD), lambda i: (i, 0)), row],
        out_shape=[jax.ShapeDtypeStruct((T, HD), F32), jax.ShapeDtypeStruct((T, AW), BF16)],
        compiler_params=_params("parallel"),
    )(dout, out)


def _dswa_bwd(qkv, dout_b, lse_tot, delta, name, comm=None):
    d, sd = qkv.shape[:2]
    nb = sd // QBLK

    def body(q_ref, kc_ref, kp_ref, vc_ref, vp_ref, do_ref, l_ref, dl_ref, dq_ref, dk_ref, dv_ref, carry_k, carry_v):
        mask = _band_mask(nb - 1 - pl.program_id(1))

        @pl.when(pl.program_id(1) == 0)
        def _():
            carry_k[...] = jnp.zeros_like(carry_k)
            carry_v[...] = jnp.zeros_like(carry_v)

        for h in range(HEADS):
            sl = slice(h * HD, (h + 1) * HD)
            qh, doh = q_ref[:, sl], do_ref[:, sl]
            keys = jnp.concatenate([kp_ref[:, sl], kc_ref[:, sl]], axis=0)
            vals = jnp.concatenate([vp_ref[:, sl], vc_ref[:, sl]], axis=0)
            s = jnp.where(mask, _dot(qh, keys, NT) * SCALE_A, NEG_INF)
            p = jnp.exp(s - l_ref[:, h:h + 1])
            ds = (p * (_dot(doh, vals, NT) - dl_ref[:, h:h + 1]) * SCALE_A).astype(BF16)
            dq_ref[:, sl] = _dot(ds, keys, NN)
            dk = _dot(ds, qh, TN)
            dv = _dot(p.astype(BF16), doh, TN)
            dk_ref[:, sl] = dk[QBLK:] + carry_k[:, sl]
            dv_ref[:, sl] = dv[QBLK:] + carry_v[:, sl]
            carry_k[:, sl] = dk[:QBLK]
            carry_v[:, sl] = dv[:QBLK]

    cur, prev, stat = _dswa_specs(nb, reverse=True)
    return _call(
        body, name, (d, nb), [cur(0), cur(1), prev(1), cur(2), prev(2), cur(0), stat, stat], [cur(0)] * 3,
        [jax.ShapeDtypeStruct((d, sd, AW), F32)] * 3, (qkv, qkv, qkv, qkv, qkv, dout_b, lse_tot, delta),
        scratch=[pltpu.VMEM((QBLK, AW), F32)] * 2, sem=("parallel", "arbitrary"), comm=comm)


def _dswa_combine(grads, cos_t, sin_t_neg, name):
    nc = len(DSWA_DILATIONS)
    T = grads[0][0].shape[0] * grads[0][0].shape[1]
    tb = _tile(T, DSWA_TB)
    half = ROT_A // 2

    def body(*refs):
        g_refs = refs[:3 * nc]
        c_ref, s_ref = refs[3 * nc:3 * nc + 2]
        outs, accs = refs[3 * nc + 2:3 * nc + 5], refs[3 * nc + 5:]
        for which in range(3):
            acc = accs[which]
            for c, d in enumerate(DSWA_DILATIONS):
                g = g_refs[3 * c + which]
                for r in range(d):
                    if c == 0:
                        acc[...] = g[r]
                    else:
                        acc[pl.ds(r, tb // d, stride=d), :] += g[r]
            val = acc[...]
            if which < 2:
                val = _rotate(val, c_ref[...], s_ref[...], half)
            outs[which][...] = val.astype(BF16)

    nat = pl.BlockSpec((tb, HD), lambda i, h: (i, h))
    tab = pl.BlockSpec((tb, HD), lambda i, h: (i, 0))
    in_specs, ins = [], []
    for d, g in zip(DSWA_DILATIONS, grads):
        in_specs += [_relayout_spec(d, tb)] * 3
        ins += list(g)
    return pl.pallas_call(
        body, name=name, grid=(T // tb, HEADS),
        in_specs=in_specs + [tab, tab], out_specs=[nat] * 3,
        out_shape=[jax.ShapeDtypeStruct((T, AW), BF16)] * 3,
        scratch_shapes=[pltpu.VMEM((tb, HD), F32)] * 3,
        compiler_params=_params("parallel", "parallel"),
    )(*ins, cos_t, sin_t_neg)


MLA_TQ = 512
QK = 2 * HD
LOG2E = 1.4426950408889634


def _triangle(nq, key_major):
    pairs = [(q, k) for q in range(nq) for k in range(q + 1)]
    if key_major:
        pairs.sort(key=lambda p: (p[1], p[0]))
    return (jnp.array([p[0] for p in pairs], jnp.int32), jnp.array([p[1] for p in pairs], jnp.int32))


def _mla_specs(tq):
    q_spec = pl.BlockSpec((tq, HEADS * QK), lambda t, qi, ki: (qi[t], 0))
    k_spec = pl.BlockSpec((tq, HEADS * QK), lambda t, qi, ki: (ki[t], 0))
    v_spec = pl.BlockSpec((tq, AW), lambda t, qi, ki: (ki[t], 1))
    qrow = pl.BlockSpec((tq, AW), lambda t, qi, ki: (qi[t], 0))
    krow = pl.BlockSpec((tq, AW), lambda t, qi, ki: (ki[t], 0))
    return q_spec, k_spec, v_spec, qrow, krow


def _mla_stat_spec(tq):
    return pl.BlockSpec((tq, HD), lambda t, qi, ki: (qi[t], 0))


def _mla_scores(q_ref, k_ref, h, qi, ki, tq):
    s = _dot(q_ref[:, h * QK:(h + 1) * QK], k_ref[:, h * QK:(h + 1) * QK], NT) * SCALE_B
    row = lax.broadcasted_iota(jnp.int32, s.shape, 0) + qi * tq
    col = lax.broadcasted_iota(jnp.int32, s.shape, 1) + ki * tq
    return jnp.where(col <= row, s, NEG_INF)


def _mla_fwd(q, k, v1, name, comm=None):
    T = q.shape[0]
    tq = _tile(T, MLA_TQ)
    tables = _triangle(T // tq, False)

    def body(qi_ref, ki_ref, q_ref, k_ref, v_ref, o_ref, ob_ref, l_ref, m_s, acc):
        t = pl.program_id(0)
        qi, ki = qi_ref[t], ki_ref[t]

        @pl.when(ki == 0)
        def _():
            m_s[...] = jnp.full_like(m_s, NEG_INF)
            acc[...] = jnp.zeros_like(acc)

        row = lax.broadcasted_iota(jnp.int32, (tq, tq), 0) + qi * tq
        col = lax.broadcasted_iota(jnp.int32, (tq, tq), 1) + ki * tq
        bias = jnp.where(col <= row, 0.0, NEG_INF)
        updates = []
        for h in range(HEADS):
            s = _dot(q_ref[:, h * QK:(h + 1) * QK], k_ref[:, h * QK:(h + 1) * QK], NT) + bias
            m_new = jnp.maximum(m_s[h], jnp.max(s, axis=-1, keepdims=True))
            p = jnp.exp2((s - m_new) * (SCALE_B * LOG2E)).astype(BF16)
            alpha = jnp.exp2((m_s[h] - m_new) * (SCALE_B * LOG2E))
            updates.append((m_new, alpha, _dot(p, v_ref[:, h * QK:(h + 1) * QK], NN)))
        for h, (m_new, alpha, pv) in enumerate(updates):
            acc[:, h * QK:(h + 1) * QK] = alpha * acc[:, h * QK:(h + 1) * QK] + pv
            m_s[h] = m_new

        @pl.when(ki == qi)
        def _():
            l_ref[...] = jnp.zeros_like(l_ref)
            for h in range(HEADS):
                sl = slice(h * HD, (h + 1) * HD)
                den = acc[:, h * QK + HD:h * QK + HD + 1]
                out = acc[:, h * QK:h * QK + HD] / den
                o_ref[:, sl] = out
                ob_ref[:, sl] = out.astype(BF16)
                l_ref[:, h:h + 1] = m_s[h] * SCALE_B + jnp.log(den)

    q_spec, k_spec, _, qrow, _ = _mla_specs(tq)
    return _call(
        body, name, (tables[0].shape[0],), [q_spec, k_spec, k_spec], [qrow, qrow, _mla_stat_spec(tq)],
        [jax.ShapeDtypeStruct((T, AW), F32), jax.ShapeDtypeStruct((T, AW), BF16), jax.ShapeDtypeStruct((T, HD), F32)],
        (q, k, v1),
        scratch=[pltpu.VMEM((HEADS, tq, 1), F32), pltpu.VMEM((tq, HEADS * QK), F32)],
        sem=("arbitrary",), comm=comm, prefetch=tables)


def _mla_ds(q_ref, k_ref, v_ref, do_ref, l_ref, dl_ref, h, qi, ki, tq):
    sl = slice(h * HD, (h + 1) * HD)
    p = jnp.exp(_mla_scores(q_ref, k_ref, h, qi, ki, tq) - l_ref[:, h:h + 1])
    ds = (p * (_dot(do_ref[:, sl], v_ref[:, sl], NT) - dl_ref[:, h:h + 1]) * SCALE_B).astype(BF16)
    return p, ds


def _mla_bwd_q(q, k, kv, dout_b, lse, delta, name, comm=None):
    T = q.shape[0]
    tq = _tile(T, MLA_TQ)
    tables = _triangle(T // tq, False)

    def body(qi_ref, ki_ref, q_ref, k_ref, v_ref, do_ref, l_ref, dl_ref, dq_ref):
        t = pl.program_id(0)
        qi, ki = qi_ref[t], ki_ref[t]

        @pl.when(ki == 0)
        def _():
            dq_ref[...] = jnp.zeros_like(dq_ref)

        for h in range(HEADS):
            _, ds = _mla_ds(q_ref, k_ref, v_ref, do_ref, l_ref, dl_ref, h, qi, ki, tq)
            dq_ref[:, h * QK:(h + 1) * QK] += _dot(ds, k_ref[:, h * QK:(h + 1) * QK], NN)

    q_spec, k_spec, v_spec, qrow, _ = _mla_specs(tq)
    return _call(
        body, name, (tables[0].shape[0],), [q_spec, k_spec, v_spec, qrow, _mla_stat_spec(tq), _mla_stat_spec(tq)], [q_spec],
        [jax.ShapeDtypeStruct((T, HEADS * QK), F32)], (q, k, kv, dout_b, lse, delta),
        sem=("arbitrary",), comm=comm, prefetch=tables)[0]


def _mla_bwd_kv(q, k, kv, dout_b, lse, delta, name, comm=None):
    T = q.shape[0]
    tq = _tile(T, MLA_TQ)
    tables = _triangle(T // tq, True)

    def body(qi_ref, ki_ref, q_ref, k_ref, v_ref, do_ref, l_ref, dl_ref, dk_ref, dv_ref):
        t = pl.program_id(0)
        qi, ki = qi_ref[t], ki_ref[t]

        @pl.when(qi == ki)
        def _():
            dk_ref[...] = jnp.zeros_like(dk_ref)
            dv_ref[...] = jnp.zeros_like(dv_ref)

        for h in range(HEADS):
            sl = slice(h * HD, (h + 1) * HD)
            p, ds = _mla_ds(q_ref, k_ref, v_ref, do_ref, l_ref, dl_ref, h, qi, ki, tq)
            dv_ref[:, sl] += _dot(p.astype(BF16), do_ref[:, sl], TN)
            dk_ref[:, h * QK:(h + 1) * QK] += _dot(ds, q_ref[:, h * QK:(h + 1) * QK], TN)

    q_spec, k_spec, v_spec, qrow, krow = _mla_specs(tq)
    return _call(
        body, name, (tables[0].shape[0],), [q_spec, k_spec, v_spec, qrow, _mla_stat_spec(tq), _mla_stat_spec(tq)], [k_spec, krow],
        [jax.ShapeDtypeStruct((T, HEADS * QK), F32), jax.ShapeDtypeStruct((T, AW), F32)],
        (q, k, kv, dout_b, lse, delta), sem=("arbitrary",), comm=comm, prefetch=tables)


def _pair_sum(by_device, from_sibling, name, tb=256):
    n_chip, R, C = from_sibling.shape
    tb = _tile(R, tb)
    core = jnp.reshape(lax.axis_index("c"), (1,)).astype(jnp.int32)

    def body(core_ref, mine_ref, theirs_ref, o_ref):
        o_ref[...] = (mine_ref[...].astype(F32) + theirs_ref[...].astype(F32)).astype(o_ref.dtype)

    blk = pl.BlockSpec((None, tb, C), lambda p, i, core_ref: (p, i, 0))
    return _call(
        body, name, (n_chip, R // tb),
        [pl.BlockSpec((None, tb, C), lambda p, i, core_ref: (2 * p + core_ref[0], i, 0)), blk], [blk],
        [jax.ShapeDtypeStruct((n_chip, R, C), BF16)], (by_device, from_sibling),
        sem=("parallel", "parallel"), prefetch=(core,))[0]


def _adamw(parts, w, m, v, name, tb=128, comm=None):
    R, C = w.shape
    n_parts = parts.shape[0]
    tb = _tile(R, tb)
    c1 = 1.0 - ADAM_B1
    c2 = 1.0 - ADAM_B2
    bc1 = 1.0 - ADAM_B1 ** ADAM_STEP
    bc2 = 1.0 - ADAM_B2 ** ADAM_STEP

    def body(p_ref, w_ref, m_ref, v_ref, g_ref, d_ref, nm_ref, nv_ref):
        g = p_ref[0].astype(F32)
        for j in range(1, n_parts):
            g = g + p_ref[j].astype(F32)
        nm = ADAM_B1 * m_ref[...] + c1 * g
        nv = ADAM_B2 * v_ref[...] + c2 * (g * g)
        g_ref[...] = g
        nm_ref[...] = nm
        nv_ref[...] = nv
        d_ref[...] = -ADAM_LR * ((nm / bc1) / (jnp.sqrt(nv / bc2) + ADAM_EPS) + ADAM_WD * w_ref[...])

    row = pl.BlockSpec((tb, C), lambda i: (i, 0))
    return _call(
        body, name, (R // tb,), [pl.BlockSpec((n_parts, tb, C), lambda i: (0, i, 0)), row, row, row], [row] * 4,
        [jax.ShapeDtypeStruct((R, C), F32)] * 4, (parts, w, m, v), sem=("parallel",), comm=comm)


def _cols_from_shards(g):
    return jnp.transpose(g, (1, 0, 2)).reshape(g.shape[1], N_DEV * g.shape[2])


def _cols_to_shards(w):
    return jnp.transpose(w.reshape(w.shape[0], N_DEV, w.shape[1] // N_DEV), (1, 0, 2))


def _split_heads(w, first):
    w3 = w.reshape(w.shape[0], HEADS, -1)
    return w3[:, :, :first].reshape(w.shape[0], -1), w3[:, :, first:].reshape(w.shape[0], -1)


def _join_heads(a, b):
    R = a.shape[0]
    return jnp.concatenate([a.reshape(R, HEADS, -1), b.reshape(R, HEADS, -1)], axis=2).reshape(R, -1)


def _pad_heads(w, width):
    w3 = w.reshape(w.shape[0], HEADS, -1)
    return jnp.pad(w3, ((0, 0), (0, 0), (0, width - w3.shape[2]))).reshape(w.shape[0], HEADS * width)


def _unpad_heads(w, k):
    return w.reshape(w.shape[0], HEADS, -1)[:, :, :k].reshape(w.shape[0], HEADS * k)


def kernel(x, positions, norm_attn_pre, norm_attn_post, w_in, q_latent_norm, kv_latent_norm, w_uq, w_ukv, w_out, norm_mlp_pre, norm_mlp_post, w_up, w_down, loss_target, m_norm_attn_pre, m_norm_attn_post, m_w_in, m_q_latent_norm, m_kv_latent_norm, m_w_uq, m_w_ukv, m_w_out, m_norm_mlp_pre, m_norm_mlp_post, m_w_up, m_w_down, v_norm_attn_pre, v_norm_attn_post, v_w_in, v_q_latent_norm, v_kv_latent_norm, v_w_uq, v_w_ukv, v_w_out, v_norm_mlp_pre, v_norm_mlp_post, v_w_up, v_w_down):
    xs = x[0]
    tgt = loss_target[0]
    pos = positions[0]
    T, D = xs.shape
    big = dict(w_in=(w_in, m_w_in, v_w_in), w_uq=(w_uq, m_w_uq, v_w_uq), w_ukv=(w_ukv, m_w_ukv, v_w_ukv),
               w_out=(w_out, m_w_out, v_w_out), w_up=(w_up, m_w_up, v_w_up), w_down=(w_down, m_w_down, v_w_down))
    big = {n: tuple(t[0] for t in ts) for n, ts in big.items()}
    big_names = ["w_in", "w_uq", "w_ukv", "w_out", "w_up", "w_down"]
    col_sharded = {"w_in", "w_uq", "w_ukv", "w_up"}

    wb = {n: big[n][0].astype(BF16) for n in big_names}

    def gathered(ex, i, n):
        g = ex.results[i]
        return _cols_from_shards(g) if n in col_sharded else g.reshape(-1, g.shape[2])

    def by_device(g, n):
        return _cols_to_shards(g) if n in col_sharded else g.reshape(N_DEV, g.shape[0] // N_DEV, g.shape[1])

    def scatter_of(g, n):
        return _Exchange([by_device(g, n)], "scatter")

    cos_a, sin_a = _rope_tables(pos, ROT_A)
    cos_b, sin_b = _rope_tables(pos, ROPE_MLA)

    ex_in = _Exchange([wb["w_in"]], "gather")
    h1 = _rms_fwd(xs, norm_attn_pre, BF16, "norm_attn_pre_fwd", comm=ex_in)
    Wi = jnp.pad(gathered(ex_in, 0, "w_in"), ((0, 0), (0, IN_PAD - IN_COLS)))
    ex_mid = _Exchange([wb["w_uq"], wb["w_ukv"], wb["w_out"]], "gather")
    proj = _mm(h1, Wi, "nn", [F32], "proj_in", tn=1408, comm=ex_mid)
    Wuq = _pad_heads(gathered(ex_mid, 0, "w_uq"), QK)
    Wukv = jnp.concatenate(_split_heads(gathered(ex_mid, 1, "w_ukv"), HD), axis=1)
    Wo = gathered(ex_mid, 2, "w_out")
    n_piece = wb["w_down"].shape[0] // 8
    ex_down = [_Exchange([wb["w_down"][i * n_piece:(i + 1) * n_piece]], "gather") for i in range(8)]
    qkv_by_d = _rope_dswa(proj, cos_a, sin_a, "rope_dswa", comm=ex_down[0])
    outs, lses = [], []
    for d, qkv, ex in zip(DSWA_DILATIONS, qkv_by_d, ex_down[1:4]):
        o, l = _dswa_fwd(qkv, f"dswa_fwd_d{d}", comm=ex)
        outs.append(o)
        lses.append(l)
    a_out, a_out_b, a_lse_by_d = _dswa_merge(outs, lses, "dswa_merge", comm=ex_down[4:6])

    cqn = _rms_fwd(proj, q_latent_norm, BF16, "q_latent_norm_fwd", width=Q_LORA, col_block=3 * AW // Q_LORA)
    ckvn = _rms_fwd(proj, kv_latent_norm, BF16, "kv_latent_norm_fwd", width=KV_LORA, col_block=3 * AW // KV_LORA + 1)
    qb = _mm(cqn, Wuq, "nn", [F32], "q_up")
    kvb = _mm(ckvn, Wukv, "nn", [BF16], "kv_up")
    odd = lambda j: j % 2 == 1
    q_mla = _rope_apply(qb, cos_b, sin_b, ROPE_MLA // 2, 2 * HEADS, odd, BF16, "rope_mla_q")
    kr = _rope_apply(proj, cos_b, sin_b, ROPE_MLA // 2, 1, lambda j: True, BF16, "rope_mla_k",
                     window=(IN_PAD - HD) // HD)
    k_mla = jnp.concatenate([kvb[:, :AW].reshape(T, HEADS, HD), jnp.broadcast_to(kr[:, None, :], (T, HEADS, HD))],
                            axis=2).reshape(T, HEADS * QK)
    v1_mla = jnp.concatenate([kvb[:, AW:].reshape(T, HEADS, HD), jnp.ones((T, HEADS, 1), BF16),
                              jnp.zeros((T, HEADS, HD - 1), BF16)], axis=2).reshape(T, HEADS * QK)
    ex_up = _Exchange([wb["w_up"]], "gather")
    b_out, b_out_b, b_lse = _mla_fwd(q_mla, k_mla, v1_mla, "mla_fwd", comm=ex_up)
    Wup_shards = ex_up.results[0]

    mixed = jnp.concatenate([a_out_b, b_out_b], axis=1)
    y1 = _mm(mixed, Wo, "nn", [F32], "attn_out")

    x2, h2 = _norm_pair_fwd(y1, xs, norm_attn_post, norm_mlp_pre, "norm_attn_post_mlp_pre_fwd")

    def relu2(z):
        r = jnp.maximum(z, 0.0)
        return r * r, r

    u, zr = _mm(h2, Wup_shards, "nn", [BF16, BF16], "mlp_up", epilogue=relu2, comm=ex_down[6:8],
                b_shards=True)
    Wdn = jnp.concatenate([ex.results[0] for ex in ex_down], axis=1).reshape(-1, D)
    y2 = _mm(u, Wdn, "nn", [F32], "mlp_down")
    dx3, dy2, loss_part, dg_mlp_post = _loss_head(x2, y2, norm_mlp_post, tgt, "loss_head")

    dz =_mm(dy2, Wdn, "nt", [BF16], "mlp_down_dx", epilogue=lambda du, r: (du * (2.0 * r.astype(F32)),), extras=(zr,))
    g_down = _mm(u, dy2, "tn", [BF16], "mlp_down_dw")
    down_dev = by_device(g_down, "w_down")
    pair_down = _Exchange([down_dev], "pair")
    up_dev = _mm(h2, dz, "tn", [BF16], "mlp_up_dw", comm=pair_down, out_shards=True)
    down_chip = _pair_sum(down_dev, pair_down.results[0], "pair_sum_w_down")
    pair_up = _Exchange([up_dev], "pair")
    sc_down = _Exchange([down_chip], "chips")
    dh2 = _mm(dz, Wup_shards, "nt", [F32], "mlp_up_dx", comm=[pair_up, sc_down], b_shards=True)
    up_chip = _pair_sum(up_dev, pair_up.results[0], "pair_sum_w_up")

    dx2, dy1, dg_mlp_pre, dg_attn_post = _norm_pair_bwd(dh2, x2, norm_mlp_pre, dx3, y1, norm_attn_post,
                                                        "norm_mlp_pre_attn_post_bwd")
    dmixed = _mm(dy1, Wo, "nt", [F32], "attn_out_dx")
    g_out = _mm(mixed, dy1, "tn", [BF16], "attn_out_dw")

    b_delta, b_dout = _delta_prep(dmixed, 1, b_out, "mla_delta")
    sc_up = _Exchange([up_chip], "chips")
    dq_mla = _mla_bwd_q(q_mla, k_mla, kvb, b_dout, b_lse, b_delta, "mla_bwd_q", comm=sc_up)
    sc_out = scatter_of(g_out, "w_out")
    dk_mla, dvb = _mla_bwd_kv(q_mla, k_mla, kvb, b_dout, b_lse, b_delta, "mla_bwd_kv", comm=sc_out)
    dqb = _rope_apply(dq_mla, cos_b, -sin_b, ROPE_MLA // 2, 2 * HEADS, odd, BF16, "rope_mla_q_bwd")
    dkn = dk_mla.reshape(T, HEADS, QK)[:, :, :HD].reshape(T, AW)
    dkvb = jnp.concatenate([dkn, dvb], axis=1).astype(BF16)
    d_kr = _shared_key_grad(dk_mla, cos_b, -sin_b, ROPE_MLA // 2, "rope_mla_k_bwd")
    g_uq_pad = _mm(cqn, dqb, "tn", [BF16], "q_up_dw")
    g_ukv_perm = _mm(ckvn, dkvb, "tn", [BF16], "kv_up_dw")
    dcqn = _mm(dqb, Wuq, "nt", [F32], "q_up_dx")
    dckvn = _mm(dkvb, Wukv, "nt", [F32], "kv_up_dx")
    d_cq, dg_q = _rms_bwd(dcqn, proj, q_latent_norm, BF16, "q_latent_norm_bwd", width=Q_LORA, col_block=3 * AW // Q_LORA)
    d_ckv, dg_kv = _rms_bwd(dckvn, proj, kv_latent_norm, BF16, "kv_latent_norm_bwd", width=KV_LORA,
                            col_block=3 * AW // KV_LORA + 1)

    g_uq = _unpad_heads(g_uq_pad, HD + ROPE_MLA)
    g_ukv = _join_heads(g_ukv_perm[:, :AW], g_ukv_perm[:, AW:])
    sc_uq = _Exchange([_cols_to_shards(g_uq), _cols_to_shards(g_ukv)], "scatter")
    a_delta_by_d, a_dout_by_d = _dswa_delta(dmixed, a_out, "dswa_delta")
    a_grads = [_dswa_bwd(qkv_by_d[c], a_dout_by_d[c], a_lse_by_d[c], a_delta_by_d[c], f"dswa_bwd_d{d}")
               for c, d in enumerate(DSWA_DILATIONS)]
    d_aq, d_ak, d_av = _dswa_combine(a_grads, cos_a, -sin_a, "dswa_combine")

    dproj = jnp.concatenate([d_aq, d_ak, d_av, d_cq, d_ckv, d_kr], axis=1)
    g_in_pad = _mm(h1, dproj, "tn", [BF16], "proj_in_dw", tn=1408, comm=sc_uq)
    in_dev = by_device(g_in_pad[:, :IN_COLS], "w_in")
    pair_in = _Exchange([in_dev], "pair")
    dh1 = _mm(dproj, Wi, "nt", [F32], "proj_in_dx", comm=pair_in)
    in_chip = _pair_sum(in_dev, pair_in.results[0], "pair_sum_w_in")
    cuts = [0, 3 * D // 8, 11 * D // 16, D]
    sc_in = [_Exchange([in_chip[:, a:b]], "chips") for a, b in zip(cuts[:-1], cuts[1:])]
    grad_x, dg_attn_pre = _rms_bwd(dh1, xs, norm_attn_pre, F32, "norm_attn_pre_bwd", residual=dx2, comm=sc_in[0])

    big_out = dict(w_down=_adamw(sc_down.results[0], *big["w_down"], "adamw_w_down", comm=sc_in[1]),
                   w_up=_adamw(sc_up.results[0], *big["w_up"], "adamw_w_up", comm=sc_in[2]))
    parts = dict(w_in=jnp.concatenate([sc.results[0] for sc in sc_in], axis=1), w_uq=sc_uq.results[0],
                 w_ukv=sc_uq.results[1], w_out=sc_out.results[0])
    big_out.update({n: _adamw(parts[n], *big[n], f"adamw_{n}") for n in parts})

    gain_names = ["norm_attn_pre", "norm_attn_post", "q_latent_norm", "kv_latent_norm", "norm_mlp_pre", "norm_mlp_post"]
    gain_args = dict(norm_attn_pre=(norm_attn_pre, m_norm_attn_pre, v_norm_attn_pre),
                     norm_attn_post=(norm_attn_post, m_norm_attn_post, v_norm_attn_post),
                     q_latent_norm=(q_latent_norm, m_q_latent_norm, v_q_latent_norm),
                     kv_latent_norm=(kv_latent_norm, m_kv_latent_norm, v_kv_latent_norm),
                     norm_mlp_pre=(norm_mlp_pre, m_norm_mlp_pre, v_norm_mlp_pre),
                     norm_mlp_post=(norm_mlp_post, m_norm_mlp_post, v_norm_mlp_post))
    gain_grads = dict(norm_attn_pre=dg_attn_pre, norm_attn_post=dg_attn_post, q_latent_norm=dg_q,
                      kv_latent_norm=dg_kv, norm_mlp_pre=dg_mlp_pre, norm_mlp_post=dg_mlp_post)
    packed = jnp.concatenate([gain_grads[n] for n in gain_names], axis=1)
    gain_parts = _Exchange([packed], "gather").standalone("gather_gain_grads")[0]
    pack3 = lambda i: jnp.concatenate([gain_args[n][i] for n in gain_names], axis=1)
    gain_out = _adamw(gain_parts, pack3(0), pack3(1), pack3(2), "adamw_gains", tb=1)
    offs = [0]
    for n in gain_names:
        offs.append(offs[-1] + gain_args[n][0].shape[1])
    small_out = {n: tuple(o[:, offs[i]:offs[i + 1]] for o in gain_out) for i, n in enumerate(gain_names)}

    loss = lax.psum(loss_part[0, 0], ("x", "y", "c"))

    order = ["norm_attn_pre", "norm_attn_post", "w_in", "q_latent_norm", "kv_latent_norm", "w_uq", "w_ukv", "w_out",
             "norm_mlp_pre", "norm_mlp_post", "w_up", "w_down"]
    res = {n: (small_out[n] if n in small_out else tuple(o[None] for o in big_out[n])) for n in order}
    return (loss, grad_x[None], *[res[n][0] for n in order], *[res[n][1] for n in order],
            *[res[n][2] for n in order], *[res[n][3] for n in order])
```

```python
import functools
import math

import jax
import jax.numpy as jnp
from jax import lax
from jax.experimental import pallas as pl
from jax.experimental.pallas import tpu as pltpu

F32 = jnp.float32
BF16 = jnp.bfloat16

N_DEV = 8
HEADS = 8
HD = 128
AW = HEADS * HD
Q_LORA = 512
KV_LORA = 512
ROPE_MLA = 64
ROT_A = 32
IN_COLS = 3 * AW + Q_LORA + KV_LORA + ROPE_MLA
IN_PAD = 3 * AW + Q_LORA + KV_LORA + HD
QBLK = 128
DSWA_DILATIONS = (1, 4, 16)
ROPE_THETA = 500000.0
NORM_EPS = 1e-6
NEG_INF = -1e30
SCALE_A = HD ** -0.5
SCALE_B = (HD + ROPE_MLA) ** -0.5

ADAM_LR = 0.001
ADAM_B1 = 0.9
ADAM_B2 = 0.999
ADAM_EPS = 1e-08
ADAM_WD = 0.01
ADAM_STEP = 10

VMEM_LIMIT = 48 * 1024 * 1024

NT = (((1,), (1,)), ((), ()))
NN = (((1,), (0,)), ((), ()))
TN = (((0,), (0,)), ((), ()))


def _dot(a, b, dims):
    return lax.dot_general(a, b, dims, preferred_element_type=F32)


def _params(*sem):
    return pltpu.CompilerParams(dimension_semantics=sem, vmem_limit_bytes=VMEM_LIMIT)


def _tile(n, want):
    t = min(n, want)
    while n % t:
        t //= 2
    return t


def _tile128(n, want):
    if n % 128:
        return n
    units = n // 128
    return 128 * max(u for u in range(1, max(want // 128, 1) + 1) if units % u == 0)


class _Exchange:
    def __init__(self, arrs, mode):
        self.arrs = list(arrs)
        self.mode = mode
        self.n = len(self.arrs)
        self.results = None
        hbm = pl.BlockSpec(memory_space=pltpu.HBM)
        self.specs = [hbm] * self.n
        shape = {"gather": lambda a: (N_DEV,) + a.shape, "scatter": lambda a: a.shape,
                 "pair": lambda a: (4,) + a.shape[1:], "chips": lambda a: a.shape}[mode]
        self.out_shape = [jax.ShapeDtypeStruct(shape(a), a.dtype) for a in self.arrs]
        n_sem = self.n * (N_DEV - 1)
        self.scratch = [pltpu.SemaphoreType.DMA((n_sem,)), pltpu.SemaphoreType.DMA((n_sem,)),
                        pltpu.SemaphoreType.DMA((self.n,))]

    def hooks(self, ins, outs, send_sems, recv_sems, local_sems):
        x, y, c = lax.axis_index("x"), lax.axis_index("y"), lax.axis_index("c")
        me = (x, y, c)
        sib = (x, y, 1 - c)
        chips = [(1 - x, y), (x, 1 - y), (1 - x, 1 - y)]
        slot = lambda p: 4 * p[0] + 2 * p[1] + p[2]
        chip_of = lambda p: 2 * p[0] + p[1]

        def rcopy(a, k, src, dst, to):
            i = a * (N_DEV - 1) + k
            return pltpu.make_async_remote_copy(src_ref=src, dst_ref=dst, send_sem=send_sems.at[i],
                                                recv_sem=recv_sems.at[i], device_id=to,
                                                device_id_type=pl.DeviceIdType.MESH)

        def local(a):
            if self.mode == "chips":
                return pltpu.make_async_copy(ins[a].at[chip_of(me)], outs[a].at[chip_of(me)], local_sems.at[a])
            src = ins[a].at[slot(me)] if self.mode == "scatter" else ins[a]
            return pltpu.make_async_copy(src, outs[a].at[slot(me)], local_sems.at[a])

        def peer(rel):
            return (1 - x if rel & 4 else x, 1 - y if rel & 2 else y, 1 - c if rel & 1 else c)

        if self.mode == "pair":
            def start():
                for a in range(self.n):
                    for p in range(4):
                        rcopy(a, p, ins[a].at[2 * p + 1 - c], outs[a].at[p], sib).start()

            def middle():
                pass

            def finish():
                for a in range(self.n):
                    for p in range(4):
                        cp = rcopy(a, p, ins[a].at[2 * p + 1 - c], outs[a].at[p], sib)
                        cp.wait_send()
                        cp.wait_recv()
        elif self.mode == "chips":
            def start():
                for a in range(self.n):
                    local(a).start()
                    for j, chip in enumerate(chips):
                        rcopy(a, j, ins[a].at[chip_of(chip)], outs[a].at[chip_of(me)], (*chip, c)).start()

            def middle():
                pass

            def finish():
                for a in range(self.n):
                    for j, chip in enumerate(chips):
                        cp = rcopy(a, j, ins[a].at[chip_of(chip)], outs[a].at[chip_of(chip)], (*chip, c))
                        cp.wait_send()
                        cp.wait_recv()
                    local(a).wait()
        elif self.mode == "scatter":
            def start():
                for a in range(self.n):
                    local(a).start()
                    for rel in range(1, N_DEV):
                        rcopy(a, rel - 1, ins[a].at[slot(peer(rel))], outs[a].at[slot(me)], peer(rel)).start()

            def middle():
                pass

            def finish():
                for a in range(self.n):
                    for rel in range(1, N_DEV):
                        cp = rcopy(a, rel - 1, ins[a].at[slot(peer(rel))], outs[a].at[slot(peer(rel))], peer(rel))
                        cp.wait_send()
                        cp.wait_recv()
                    local(a).wait()
        else:
            def start():
                for a in range(self.n):
                    local(a).start()
                    rcopy(a, 0, ins[a], outs[a].at[slot(me)], sib).start()
                    for j, chip in enumerate(chips):
                        rcopy(a, 1 + j, ins[a], outs[a].at[slot(me)], (*chip, c)).start()

            def middle():
                for a in range(self.n):
                    for j, chip in enumerate(chips):
                        landed = outs[a].at[slot((*chip, c))]
                        rcopy(a, 1 + j, ins[a], landed, me).wait_recv()
                        rcopy(a, 4 + j, landed, landed, sib).start()

            def finish():
                for a in range(self.n):
                    rcopy(a, 0, ins[a], outs[a].at[slot(sib)], me).wait_recv()
                    for j, chip in enumerate(chips):
                        rcopy(a, 4 + j, ins[a], outs[a].at[slot((*chip, 1 - c))], me).wait_recv()
                    for k in range(N_DEV - 1):
                        rcopy(a, k, ins[a], outs[a].at[slot(me)], me).wait_send()
                    local(a).wait()

        return start, middle, finish

    def set_results(self, res):
        self.results = list(res)

    def standalone(self, name):
        n = self.n

        def body(*refs):
            start, middle, finish = self.hooks(refs[:n], refs[n:2 * n], *refs[2 * n:])
            start()
            middle()
            finish()

        self.results = pl.pallas_call(
            body, name=name, in_specs=self.specs, out_specs=self.specs, out_shape=self.out_shape,
            scratch_shapes=self.scratch, compiler_params=pltpu.CompilerParams(has_side_effects=True),
        )(*self.arrs)
        return self.results


class _Carried:
    def __init__(self, parts):
        self.parts = list(parts)
        self.n = sum(p.n for p in self.parts)
        self.arrs = [a for p in self.parts for a in p.arrs]
        self.specs = [s for p in self.parts for s in p.specs]
        self.out_shape = [s for p in self.parts for s in p.out_shape]
        self.scratch = [s for p in self.parts for s in p.scratch]

    def hooks(self, ins, outs, *sems):
        hooks, i = [], 0
        for j, p in enumerate(self.parts):
            hooks.append(p.hooks(ins[i:i + p.n], outs[i:i + p.n], *sems[3 * j:3 * j + 3]))
            i += p.n
        def phase(k):
            def run():
                for h in hooks:
                    h[k]()
            return run

        return phase(0), phase(1), phase(2)

    def set_results(self, res):
        i = 0
        for p in self.parts:
            p.set_results(res[i:i + p.n])
            i += p.n


def _call(body, name, grid, in_specs, out_specs, out_shape, args, scratch=(), sem=(), comm=None, prefetch=()):
    npf = len(prefetch)
    if isinstance(comm, (list, tuple)):
        comm = _Carried(comm)
    if comm is None:
        spec = pltpu.PrefetchScalarGridSpec(num_scalar_prefetch=npf, grid=grid, in_specs=list(in_specs),
                                            out_specs=list(out_specs), scratch_shapes=list(scratch))
        return pl.pallas_call(body, name=name, grid_spec=spec, out_shape=list(out_shape),
                              compiler_params=_params(*sem))(*prefetch, *args)
    ni, no, ns, n = len(in_specs), len(out_specs), len(scratch), comm.n
    steps = math.prod(grid)

    def wrapped(*refs):
        pf, refs = refs[:npf], refs[npf:]
        ins, c_ins = refs[:ni], refs[ni:ni + n]
        outs, c_outs = refs[ni + n:ni + n + no], refs[ni + n + no:ni + 2 * n + no]
        scr, c_scr = refs[ni + 2 * n + no:ni + 2 * n + no + ns], refs[ni + 2 * n + no + ns:]
        start, middle, finish = comm.hooks(c_ins, c_outs, *c_scr)
        step = pl.program_id(0)
        for ax in range(1, len(grid)):
            step = step * grid[ax] + pl.program_id(ax)
        pl.when(step == 0)(start)
        pl.when(step == steps // 2)(middle)
        body(*pf, *ins, *outs, *scr)
        pl.when(step == steps - 1)(finish)

    spec = pltpu.PrefetchScalarGridSpec(num_scalar_prefetch=npf, grid=grid, in_specs=list(in_specs) + comm.specs,
                                        out_specs=list(out_specs) + comm.specs,
                                        scratch_shapes=list(scratch) + comm.scratch)
    res = pl.pallas_call(
        wrapped, name=name, grid_spec=spec, out_shape=list(out_shape) + comm.out_shape,
        compiler_params=pltpu.CompilerParams(dimension_semantics=("arbitrary",) * len(grid),
                                             vmem_limit_bytes=VMEM_LIMIT, has_side_effects=True),
    )(*prefetch, *args, *comm.arrs)
    comm.set_results(res[no:])
    return res[:no]


def _mm(a, b, mode, out_dtypes, name, epilogue=None, extras=(), tm=1024, tn=1024, tk=2048, comm=None,
        b_shards=False, out_shards=False):
    if mode == "tn":
        K, M = a.shape
    else:
        M, K = a.shape
    if b_shards:
        N = b.shape[1] if mode == "nt" else N_DEV * b.shape[2]
    else:
        N = b.shape[0] if mode == "nt" else b.shape[1]
    tm, tn, tk = _tile128(M, tm), _tile128(N, tn), _tile128(K, tk)
    pair_k = b_shards and mode == "nt"
    if pair_k:
        tk = 2 * K // N_DEV
        b = b.reshape(N_DEV // 2, 2, *b.shape[1:])
    elif b_shards or out_shards:
        tn = N // N_DEV
    nk = K // tk
    dims = {"nn": NN, "nt": NT, "tn": TN}[mode]
    a_spec = (pl.BlockSpec((tk, tm), lambda i, j, k: (k, i)) if mode == "tn"
              else pl.BlockSpec((tm, tk), lambda i, j, k: (i, k)))
    if b_shards:
        b_spec = (pl.BlockSpec((None, 2, tn, tk // 2), lambda i, j, k: (k, 0, j, 0)) if mode == "nt"
                  else pl.BlockSpec((None, tk, tn), lambda i, j, k: (j, k, 0)))
    else:
        b_spec = (pl.BlockSpec((tn, tk), lambda i, j, k: (j, k)) if mode == "nt"
                  else pl.BlockSpec((tk, tn), lambda i, j, k: (k, j)))
    mn_spec = pl.BlockSpec((tm, tn), lambda i, j, k: (i, j))
    out_spec = pl.BlockSpec((None, tm, tn), lambda i, j, k: (j, i, 0)) if out_shards else mn_spec
    out_dims = (N_DEV, M, N // N_DEV) if out_shards else (M, N)
    n_ex = len(extras)
    n_out = len(out_dtypes)

    def finish(acc, ex, outs):
        res = (acc,) if epilogue is None else epilogue(acc, *[e[...] for e in ex])
        for o, r in zip(outs, res):
            o[...] = r.astype(o.dtype)

    def product(a_ref, b_ref):
        if pair_k:
            return _dot(a_ref[:, :tk // 2], b_ref[0], NT) + _dot(a_ref[:, tk // 2:], b_ref[1], NT)
        return _dot(a_ref[...], b_ref[...], dims)

    def body(*refs):
        a_ref, b_ref = refs[:2]
        ex = refs[2:2 + n_ex]
        outs = refs[2 + n_ex:2 + n_ex + n_out]
        if nk == 1:
            finish(product(a_ref, b_ref), ex, outs)
            return
        acc = refs[-1]
        k = pl.program_id(2)

        @pl.when(k == 0)
        def _():
            acc[...] = product(a_ref, b_ref)

        @pl.when(jnp.logical_and(k > 0, k < nk - 1))
        def _():
            acc[...] += product(a_ref, b_ref)

        @pl.when(k == nk - 1)
        def _():
            finish(acc[...] + product(a_ref, b_ref), ex, outs)

    out = _call(
        body, name, (M // tm, N // tn, nk), [a_spec, b_spec] + [mn_spec] * n_ex, [out_spec] * n_out,
        [jax.ShapeDtypeStruct(out_dims, dt) for dt in out_dtypes], (a, b, *extras),
        scratch=[] if nk == 1 else [pltpu.VMEM((tm, tn), F32)], sem=("parallel", "parallel", "arbitrary"),
        comm=comm)
    return out[0] if n_out == 1 else out


def _rms_fwd(x, gain, out_dtype, name, width=None, col_block=0, residual=None, tb=256, comm=None):
    T = x.shape[0]
    W = x.shape[1] if width is None else width
    tb = _tile(T, tb)
    has_res = residual is not None

    def body(*refs):
        x_ref, g_ref = refs[:2]
        o_ref = refs[-1]
        xf = x_ref[...]
        y = xf * lax.rsqrt(jnp.mean(xf * xf, axis=-1, keepdims=True) + NORM_EPS) * g_ref[...]
        if has_res:
            y = refs[2][...] + y
        o_ref[...] = y.astype(o_ref.dtype)

    row = pl.BlockSpec((tb, W), lambda i: (i, 0))
    ins = [x, gain] + ([residual] if has_res else [])
    return _call(
        body, name, (T // tb,),
        [pl.BlockSpec((tb, W), lambda i: (i, col_block)),
         pl.BlockSpec((1, W), lambda i: (0, 0))] + ([row] if has_res else []),
        [row], [jax.ShapeDtypeStruct((T, W), out_dtype)], ins, sem=("parallel",), comm=comm)[0]


def _rms_bwd(dy, x, gain, out_dtype, name, width=None, col_block=0, residual=None, tb=256, comm=None):
    T = dy.shape[0]
    W = x.shape[1] if width is None else width
    tb = _tile(T, tb)
    has_res = residual is not None

    def body(*refs):
        dy_ref, x_ref, g_ref = refs[:3]
        dx_ref, dg_ref = refs[-2:]
        i = pl.program_id(0)
        xf = x_ref[...]
        r = lax.rsqrt(jnp.mean(xf * xf, axis=-1, keepdims=True) + NORM_EPS)
        xn = xf * r
        dyf = dy_ref[...].astype(F32)
        dyg = dyf * g_ref[...]
        dx = r * (dyg - xn * jnp.mean(dyg * xn, axis=-1, keepdims=True))
        if has_res:
            dx = refs[3][...] + dx
        dx_ref[...] = dx.astype(dx_ref.dtype)

        @pl.when(i == 0)
        def _():
            dg_ref[...] = jnp.zeros_like(dg_ref)

        dg_ref[...] += jnp.sum(dyf * xn, axis=0, keepdims=True)

    row = pl.BlockSpec((tb, W), lambda i: (i, 0))
    vec = pl.BlockSpec((1, W), lambda i: (0, 0))
    ins = [dy, x, gain] + ([residual] if has_res else [])
    return _call(
        body, name, (T // tb,),
        [row, pl.BlockSpec((tb, W), lambda i: (i, col_block)), vec] + ([row] if has_res else []), [row, vec],
        [jax.ShapeDtypeStruct((T, W), out_dtype), jax.ShapeDtypeStruct((1, W), F32)], ins,
        sem=("arbitrary",), comm=comm)


def _rms(xf):
    r = lax.rsqrt(jnp.mean(xf * xf, axis=-1, keepdims=True) + NORM_EPS)
    return r, xf * r


def _rms_grad(dyf, xn, r, gain):
    dyg = dyf * gain
    return r * (dyg - xn * jnp.mean(dyg * xn, axis=-1, keepdims=True)), dyf * xn


def _accumulate_rows(i, ref, rows):
    @pl.when(i == 0)
    def _():
        ref[...] = jnp.zeros_like(ref)

    ref[...] += jnp.sum(rows, axis=0, keepdims=True)


def _norm_pair_fwd(y1, xs, gain_post, gain_pre, name, tb=256):
    T, D = xs.shape
    tb = _tile(T, tb)

    def body(y1_ref, xs_ref, gp_ref, gq_ref, x2_ref, h2_ref):
        x2 = xs_ref[...] + _rms(y1_ref[...])[1] * gp_ref[...]
        x2_ref[...] = x2
        h2_ref[...] = (_rms(x2)[1] * gq_ref[...]).astype(BF16)

    row = pl.BlockSpec((tb, D), lambda i: (i, 0))
    vec = pl.BlockSpec((1, D), lambda i: (0, 0))
    return pl.pallas_call(
        body, name=name, grid=(T // tb,), in_specs=[row, row, vec, vec], out_specs=[row, row],
        out_shape=[jax.ShapeDtypeStruct((T, D), F32), jax.ShapeDtypeStruct((T, D), BF16)],
        compiler_params=_params("parallel"),
    )(y1, xs, gain_post, gain_pre)


def _norm_pair_bwd(dh2, x2, gain_pre, dx3, y1, gain_post, name, tb=256):
    T, D = x2.shape
    tb = _tile(T, tb)

    def body(dh2_ref, x2_ref, gq_ref, dx3_ref, y1_ref, gp_ref, dx2_ref, dy1_ref, dgq_ref, dgp_ref):
        i = pl.program_id(0)
        r2, xn2 = _rms(x2_ref[...])
        d2, rows_q = _rms_grad(dh2_ref[...], xn2, r2, gq_ref[...])
        dx2 = dx3_ref[...] + d2
        dx2_ref[...] = dx2
        r1, yn1 = _rms(y1_ref[...])
        d1, rows_p = _rms_grad(dx2, yn1, r1, gp_ref[...])
        dy1_ref[...] = d1.astype(BF16)
        _accumulate_rows(i, dgq_ref, rows_q)
        _accumulate_rows(i, dgp_ref, rows_p)

    row = pl.BlockSpec((tb, D), lambda i: (i, 0))
    vec = pl.BlockSpec((1, D), lambda i: (0, 0))
    return pl.pallas_call(
        body, name=name, grid=(T // tb,), in_specs=[row, row, vec, row, row, vec], out_specs=[row, row, vec, vec],
        out_shape=[jax.ShapeDtypeStruct((T, D), F32), jax.ShapeDtypeStruct((T, D), BF16),
                   jax.ShapeDtypeStruct((1, D), F32), jax.ShapeDtypeStruct((1, D), F32)],
        compiler_params=_params("arbitrary"),
    )(dh2, x2, gain_pre, dx3, y1, gain_post)


def _loss_head(x2, y2, gain, target, name, tb=256):
    T, D = x2.shape
    tb = _tile(T, tb)

    def body(x2_ref, y2_ref, g_ref, t_ref, dx3_ref, dy2_ref, loss_ref, dg_ref):
        i = pl.program_id(0)
        r, yn = _rms(y2_ref[...])
        e = x2_ref[...] + yn * g_ref[...] - t_ref[...]
        dx3 = e * (1.0 / D)
        dx3_ref[...] = dx3
        dy2, rows = _rms_grad(dx3, yn, r, g_ref[...])
        dy2_ref[...] = dy2.astype(BF16)
        _accumulate_rows(i, dg_ref, rows)
        _accumulate_rows(i, loss_ref, 0.5 * jnp.mean(e * e, axis=-1, keepdims=True))

    row = pl.BlockSpec((tb, D), lambda i: (i, 0))
    vec = pl.BlockSpec((1, D), lambda i: (0, 0))
    return pl.pallas_call(
        body, name=name, grid=(T // tb,),
        in_specs=[row, row, vec, row],
        out_specs=[row, row, pl.BlockSpec((1, 1), lambda i: (0, 0)), vec],
        out_shape=[jax.ShapeDtypeStruct((T, D), F32), jax.ShapeDtypeStruct((T, D), BF16),
                   jax.ShapeDtypeStruct((1, 1), F32), jax.ShapeDtypeStruct((1, D), F32)],
        compiler_params=_params("arbitrary"),
    )(x2, y2, gain, target)


def _rope_tables(positions, rot_dim):
    half = rot_dim // 2
    inv_freq = ROPE_THETA ** (-jnp.arange(0, rot_dim, 2, dtype=F32) / rot_dim)
    ang = positions.astype(F32)[:, None] * inv_freq[None, :]
    cos, sin = jnp.cos(ang), jnp.sin(ang)
    T = positions.shape[0]
    ones = jnp.ones((T, HD - rot_dim), F32)
    cos_t = jnp.concatenate([cos, cos, ones], axis=1)
    sin_t = jnp.concatenate([-sin, sin, jnp.zeros_like(ones)], axis=1)
    return cos_t, sin_t


def _rotate(x, cos_t, sin_t, half):
    lane = lax.broadcasted_iota(jnp.int32, x.shape, 1)
    swapped = jnp.where(lane < half, pltpu.roll(x, HD - half, 1), pltpu.roll(x, half, 1))
    return x * cos_t + swapped * sin_t


def _rope_apply(x, cos_t, sin_t, half, n_blocks, is_rope, out_dtype, name, window=0, tb=256):
    T = x.shape[0]
    tb = _tile(T, tb)
    W = n_blocks * HD

    def body(x_ref, c_ref, s_ref, o_ref):
        for j in range(n_blocks):
            sl = slice(j * HD, (j + 1) * HD)
            xj = x_ref[:, sl]
            if is_rope(j):
                xj = _rotate(xj.astype(F32), c_ref[...], s_ref[...], half)
            o_ref[:, sl] = xj.astype(o_ref.dtype)

    tab = pl.BlockSpec((tb, HD), lambda i: (i, 0))
    return pl.pallas_call(
        body, name=name, grid=(T // tb,),
        in_specs=[pl.BlockSpec((tb, W), lambda i: (i, window)), tab, tab],
        out_specs=pl.BlockSpec((tb, W), lambda i: (i, 0)),
        out_shape=jax.ShapeDtypeStruct((T, W), out_dtype),
        compiler_params=_params("parallel"),
    )(x, cos_t, sin_t)


DSWA_TB = 2048


def _deinterleave(src, dst_ref, d, dtype):
    rows = src.shape[0] // d
    for r in range(d):
        dst_ref[r] = src[pl.ds(r, rows, stride=d), :].astype(dtype)


def _rope_dswa(proj, cos_t, sin_t, name, comm=None):
    T = proj.shape[0]
    tb = _tile(T, DSWA_TB)
    half = ROT_A // 2

    def body(x_ref, c_ref, s_ref, *rest):
        outs, scr = rest[:-1], rest[-1]
        j = pl.program_id(1)

        @pl.when(j < 2 * HEADS)
        def _():
            scr[...] = _rotate(x_ref[...], c_ref[...], s_ref[...], half)

        @pl.when(j >= 2 * HEADS)
        def _():
            scr[...] = x_ref[...]

        for o_ref, d in zip(outs, DSWA_DILATIONS):
            _deinterleave(scr, o_ref, d, BF16)

    blk = pl.BlockSpec((tb, HD), lambda i, j: (i, j))
    tab = pl.BlockSpec((tb, HD), lambda i, j: (i, 0))
    return _call(
        body, name, (T // tb, 3 * HEADS), [blk, tab, tab],
        [pl.BlockSpec((d, tb // d, HD), lambda i, j: (0, i, j)) for d in DSWA_DILATIONS],
        [jax.ShapeDtypeStruct((d, T // d, 3 * AW), BF16) for d in DSWA_DILATIONS], (proj, cos_t, sin_t),
        scratch=[pltpu.VMEM((tb, HD), F32)], sem=("parallel", "parallel"), comm=comm)


def _shared_key_grad(dk, cos_t, sin_t_neg, half, name, tb=512):
    T = dk.shape[0]
    tb = _tile(T, tb)

    def body(d_ref, c_ref, s_ref, o_ref):
        tot = d_ref[:, HD:2 * HD]
        for h in range(1, HEADS):
            tot = tot + d_ref[:, h * QK + HD:(h + 1) * QK]
        o_ref[...] = _rotate(tot, c_ref[...], s_ref[...], half).astype(o_ref.dtype)

    tab = pl.BlockSpec((tb, HD), lambda i: (i, 0))
    return pl.pallas_call(
        body, name=name, grid=(T // tb,),
        in_specs=[pl.BlockSpec((tb, HEADS * QK), lambda i: (i, 0)), tab, tab],
        out_specs=tab, out_shape=jax.ShapeDtypeStruct((T, HD), BF16),
        compiler_params=_params("parallel"),
    )(dk, cos_t, sin_t_neg)


def _band_mask(n):
    row = lax.broadcasted_iota(jnp.int32, (QBLK, 2 * QBLK), 0)
    col = lax.broadcasted_iota(jnp.int32, (QBLK, 2 * QBLK), 1)
    in_prev = jnp.logical_and(jnp.logical_and(col < QBLK, col >= row), n > 0)
    in_cur = jnp.logical_and(col >= QBLK, col - QBLK <= row)
    return jnp.logical_or(in_prev, in_cur)


def _dswa_specs(nb, reverse=False):
    pos = (lambda n: nb - 1 - n) if reverse else (lambda n: n)
    cur = lambda c: pl.BlockSpec((None, QBLK, AW), lambda r, n: (r, pos(n), c))
    prev = lambda c: pl.BlockSpec((None, QBLK, AW), lambda r, n: (r, jnp.maximum(pos(n) - 1, 0), c))
    stat = pl.BlockSpec((None, QBLK, HD), lambda r, n: (r, pos(n), 0))
    return cur, prev, stat


def _relayout_spec(d, tb, per_head=True):
    if per_head:
        return pl.BlockSpec((d, tb // d, HD), lambda i, h: (0, i, h))
    return pl.BlockSpec((d, tb // d, HD), lambda i, h: (0, i, 0))


def _head_lane(x, h):
    lane = lax.broadcasted_iota(jnp.int32, x.shape, 1)
    return jnp.sum(jnp.where(lane == h, x, 0.0), axis=-1, keepdims=True)


def _dswa_fwd(qkv, name, comm=None):
    d, sd = qkv.shape[:2]
    nb = sd // QBLK

    def body(q_ref, kc_ref, kp_ref, vc_ref, vp_ref, o_ref, l_ref):
        mask = _band_mask(pl.program_id(1))
        l_ref[...] = jnp.zeros_like(l_ref)
        for h in range(HEADS):
            sl = slice(h * HD, (h + 1) * HD)
            keys = jnp.concatenate([kp_ref[:, sl], kc_ref[:, sl]], axis=0)
            vals = jnp.concatenate([vp_ref[:, sl], vc_ref[:, sl]], axis=0)
            s = jnp.where(mask, _dot(q_ref[:, sl], keys, NT) * SCALE_A, NEG_INF)
            m = jnp.max(s, axis=-1, keepdims=True)
            p = jnp.exp(s - m)
            den = jnp.sum(p, axis=-1, keepdims=True)
            o_ref[:, sl] = _dot((p / den).astype(BF16), vals, NN)
            l_ref[:, h:h + 1] = m + jnp.log(den)

    cur, prev, stat = _dswa_specs(nb)
    return _call(
        body, name, (d, nb), [cur(0), cur(1), prev(1), cur(2), prev(2)], [cur(0), stat],
        [jax.ShapeDtypeStruct((d, sd, AW), F32), jax.ShapeDtypeStruct((d, sd, HD), F32)],
        (qkv, qkv, qkv, qkv, qkv), sem=("parallel", "parallel"), comm=comm)


def _dswa_merge(outs, lses, name, comm=None):
    nc = len(DSWA_DILATIONS)
    T = outs[0].shape[0] * outs[0].shape[1]
    tb = _tile(T, DSWA_TB)

    def body(*refs):
        o_refs, l_refs = refs[:nc], refs[nc:2 * nc]
        out_ref, outb_ref = refs[2 * nc:2 * nc + 2]
        lt_refs = refs[2 * nc + 2:3 * nc + 2]
        o_nat, l_nat, lt_nat = refs[3 * nc + 2:4 * nc + 2], refs[4 * nc + 2:5 * nc + 2], refs[-1]
        h = pl.program_id(1)
        for c, d in enumerate(DSWA_DILATIONS):
            for r in range(d):
                o_nat[c][pl.ds(r, tb // d, stride=d), :] = o_refs[c][r]
                l_nat[c][pl.ds(r, tb // d, stride=d), :] = l_refs[c][r]
        ls = [l[...] for l in l_nat]
        m = functools.reduce(jnp.maximum, ls)
        es = [jnp.exp(l - m) for l in ls]
        tot = functools.reduce(lambda a, b: a + b, es)
        acc = _head_lane(es[0] / tot, h) * o_nat[0][...]
        for c in range(1, nc):
            acc = acc + _head_lane(es[c] / tot, h) * o_nat[c][...]
        out_ref[...] = acc
        outb_ref[...] = acc.astype(BF16)
        lt_nat[...] = m + jnp.log(tot)
        for c, d in enumerate(DSWA_DILATIONS):
            _deinterleave(lt_nat, lt_refs[c], d, F32)

    nat = pl.BlockSpec((tb, HD), lambda i, h: (i, h))
    by_d = [_relayout_spec(d, tb) for d in DSWA_DILATIONS]
    stat_by_d = [_relayout_spec(d, tb, per_head=False) for d in DSWA_DILATIONS]
    res = _call(
        body, name, (T // tb, HEADS), by_d + stat_by_d, [nat, nat] + stat_by_d,
        [jax.ShapeDtypeStruct((T, AW), F32), jax.ShapeDtypeStruct((T, AW), BF16)]
        + [jax.ShapeDtypeStruct((d, T // d, HD), F32) for d in DSWA_DILATIONS], (*outs, *lses),
        scratch=[pltpu.VMEM((tb, HD), F32)] * (2 * nc + 1), sem=("parallel", "arbitrary"), comm=comm)
    return res[0], res[1], res[2:]


def _dswa_delta(dout, out, name):
    nc = len(DSWA_DILATIONS)
    T = out.shape[0]
    tb = _tile(T, DSWA_TB)

    def body(do_ref, o_ref, *rest):
        dl_refs, dob_refs, dl_nat = rest[:nc], rest[nc:2 * nc], rest[-1]
        h = pl.program_id(1)
        lane = lax.broadcasted_iota(jnp.int32, (tb, HD), 1)
        mine = jnp.where(lane == h, jnp.sum(do_ref[...] * o_ref[...], axis=-1, keepdims=True), 0.0)

        @pl.when(h == 0)
        def _():
            dl_nat[...] = mine

        @pl.when(h > 0)
        def _():
            dl_nat[...] += mine

        for c, d in enumerate(DSWA_DILATIONS):
            _deinterleave(dl_nat, dl_refs[c], d, F32)
            _deinterleave(do_ref, dob_refs[c], d, BF16)

    nat = pl.BlockSpec((tb, HD), lambda i, h: (i, h))
    by_d = [_relayout_spec(d, tb) for d in DSWA_DILATIONS]
    stat_by_d = [_relayout_spec(d, tb, per_head=False) for d in DSWA_DILATIONS]
    res = pl.pallas_call(
        body, name=name, grid=(T // tb, HEADS),
        in_specs=[nat, nat], out_specs=stat_by_d + by_d,
        out_shape=[jax.ShapeDtypeStruct((d, T // d, HD), F32) for d in DSWA_DILATIONS]
        + [jax.ShapeDtypeStruct((d, T // d, AW), BF16) for d in DSWA_DILATIONS],
        scratch_shapes=[pltpu.VMEM((tb, HD), F32)],
        compiler_params=_params("parallel", "arbitrary"),
    )(dout, out)
    return res[:nc], res[nc:]


def _delta_prep(dout, col_block, out, name, tb=256):
    T = out.shape[0]
    tb = _tile(T, tb)

    def body(do_ref, o_ref, delta_ref, dob_ref):
        delta_ref[...] = jnp.zeros_like(delta_ref)
        for h in range(HEADS):
            sl = slice(h * HD, (h + 1) * HD)
            doh = do_ref[:, sl]
            delta_ref[:, h:h + 1] = jnp.sum(doh * o_ref[:, sl], axis=-1, keepdims=True)
            dob_ref[:, sl] = doh.astype(BF16)

    row = pl.BlockSpec((tb, AW), lambda i: (i, 0))
    return pl.pallas_call(
        body, name=name, grid=(T // tb,),
        in_specs=[pl.BlockSpec((tb, AW), lambda i: (i, col_block)), row],
        out_specs=[pl.BlockSpec((tb, HD), lambda i: (i, 0)), row],
        out_shape=[jax.ShapeDtypeStruct((T, HD), F32), jax.ShapeDtypeStruct((T, AW), BF16)],
        compiler_params=_params("parallel"),
    )(dout, out)


def _dswa_bwd(qkv, dout_b, lse_tot, delta, name, comm=None):
    d, sd = qkv.shape[:2]
    nb = sd // QBLK

    def body(q_ref, kc_ref, kp_ref, vc_ref, vp_ref, do_ref, l_ref, dl_ref, dq_ref, dk_ref, dv_ref, carry_k, carry_v):
        mask = _band_mask(nb - 1 - pl.program_id(1))

        @pl.when(pl.program_id(1) == 0)
        def _():
            carry_k[...] = jnp.zeros_like(carry_k)
            carry_v[...] = jnp.zeros_like(carry_v)

        for h in range(HEADS):
            sl = slice(h * HD, (h + 1) * HD)
            qh, doh = q_ref[:, sl], do_ref[:, sl]
            keys = jnp.concatenate([kp_ref[:, sl], kc_ref[:, sl]], axis=0)
            vals = jnp.concatenate([vp_ref[:, sl], vc_ref[:, sl]], axis=0)
            s = jnp.where(mask, _dot(qh, keys, NT) * SCALE_A, NEG_INF)
            p = jnp.exp(s - l_ref[:, h:h + 1])
            ds = (p * (_dot(doh, vals, NT) - dl_ref[:, h:h + 1]) * SCALE_A).astype(BF16)
            dq_ref[:, sl] = _dot(ds, keys, NN)
            dk = _dot(ds, qh, TN)
            dv = _dot(p.astype(BF16), doh, TN)
            dk_ref[:, sl] = dk[QBLK:] + carry_k[:, sl]
            dv_ref[:, sl] = dv[QBLK:] + carry_v[:, sl]
            carry_k[:, sl] = dk[:QBLK]
            carry_v[:, sl] = dv[:QBLK]

    cur, prev, stat = _dswa_specs(nb, reverse=True)
    return _call(
        body, name, (d, nb), [cur(0), cur(1), prev(1), cur(2), prev(2), cur(0), stat, stat], [cur(0)] * 3,
        [jax.ShapeDtypeStruct((d, sd, AW), F32)] * 3, (qkv, qkv, qkv, qkv, qkv, dout_b, lse_tot, delta),
        scratch=[pltpu.VMEM((QBLK, AW), F32)] * 2, sem=("parallel", "arbitrary"), comm=comm)


def _dswa_combine(grads, cos_t, sin_t_neg, name):
    nc = len(DSWA_DILATIONS)
    T = grads[0][0].shape[0] * grads[0][0].shape[1]
    tb = _tile(T, DSWA_TB)
    half = ROT_A // 2

    def body(*refs):
        g_refs = refs[:3 * nc]
        c_ref, s_ref = refs[3 * nc:3 * nc + 2]
        outs, accs = refs[3 * nc + 2:3 * nc + 5], refs[3 * nc + 5:]
        for which in range(3):
            acc = accs[which]
            for c, d in enumerate(DSWA_DILATIONS):
                g = g_refs[3 * c + which]
                for r in range(d):
                    if c == 0:
                        acc[...] = g[r]
                    else:
                        acc[pl.ds(r, tb // d, stride=d), :] += g[r]
            val = acc[...]
            if which < 2:
                val = _rotate(val, c_ref[...], s_ref[...], half)
            outs[which][...] = val.astype(BF16)

    nat = pl.BlockSpec((tb, HD), lambda i, h: (i, h))
    tab = pl.BlockSpec((tb, HD), lambda i, h: (i, 0))
    in_specs, ins = [], []
    for d, g in zip(DSWA_DILATIONS, grads):
        in_specs += [_relayout_spec(d, tb)] * 3
        ins += list(g)
    return pl.pallas_call(
        body, name=name, grid=(T // tb, HEADS),
        in_specs=in_specs + [tab, tab], out_specs=[nat] * 3,
        out_shape=[jax.ShapeDtypeStruct((T, AW), BF16)] * 3,
        scratch_shapes=[pltpu.VMEM((tb, HD), F32)] * 3,
        compiler_params=_params("parallel", "parallel"),
    )(*ins, cos_t, sin_t_neg)


MLA_TQ = 512
QK = 2 * HD
LOG2E = 1.4426950408889634


def _triangle(nq, key_major):
    pairs = [(q, k) for q in range(nq) for k in range(q + 1)]
    if key_major:
        pairs.sort(key=lambda p: (p[1], p[0]))
    return (jnp.array([p[0] for p in pairs], jnp.int32), jnp.array([p[1] for p in pairs], jnp.int32))


def _mla_specs(tq):
    q_spec = pl.BlockSpec((tq, HEADS * QK), lambda t, qi, ki: (qi[t], 0))
    k_spec = pl.BlockSpec((tq, HEADS * QK), lambda t, qi, ki: (ki[t], 0))
    v_spec = pl.BlockSpec((tq, AW), lambda t, qi, ki: (ki[t], 1))
    qrow = pl.BlockSpec((tq, AW), lambda t, qi, ki: (qi[t], 0))
    krow = pl.BlockSpec((tq, AW), lambda t, qi, ki: (ki[t], 0))
    return q_spec, k_spec, v_spec, qrow, krow


def _mla_stat_spec(tq):
    return pl.BlockSpec((tq, HD), lambda t, qi, ki: (qi[t], 0))


def _mla_scores(q_ref, k_ref, h, qi, ki, tq):
    s = _dot(q_ref[:, h * QK:(h + 1) * QK], k_ref[:, h * QK:(h + 1) * QK], NT) * SCALE_B
    row = lax.broadcasted_iota(jnp.int32, s.shape, 0) + qi * tq
    col = lax.broadcasted_iota(jnp.int32, s.shape, 1) + ki * tq
    return jnp.where(col <= row, s, NEG_INF)


def _mla_fwd(q, k, v1, name, comm=None):
    T = q.shape[0]
    tq = _tile(T, MLA_TQ)
    tables = _triangle(T // tq, False)

    def body(qi_ref, ki_ref, q_ref, k_ref, v_ref, o_ref, ob_ref, l_ref, m_s, acc):
        t = pl.program_id(0)
        qi, ki = qi_ref[t], ki_ref[t]

        @pl.when(ki == 0)
        def _():
            m_s[...] = jnp.full_like(m_s, NEG_INF)
            acc[...] = jnp.zeros_like(acc)

        row = lax.broadcasted_iota(jnp.int32, (tq, tq), 0) + qi * tq
        col = lax.broadcasted_iota(jnp.int32, (tq, tq), 1) + ki * tq
        bias = jnp.where(col <= row, 0.0, NEG_INF)
        updates = []
        for h in range(HEADS):
            s = _dot(q_ref[:, h * QK:(h + 1) * QK], k_ref[:, h * QK:(h + 1) * QK], NT) + bias
            m_new = jnp.maximum(m_s[h], jnp.max(s, axis=-1, keepdims=True))
            p = jnp.exp2((s - m_new) * (SCALE_B * LOG2E)).astype(BF16)
            alpha = jnp.exp2((m_s[h] - m_new) * (SCALE_B * LOG2E))
            updates.append((m_new, alpha, _dot(p, v_ref[:, h * QK:(h + 1) * QK], NN)))
        for h, (m_new, alpha, pv) in enumerate(updates):
            acc[:, h * QK:(h + 1) * QK] = alpha * acc[:, h * QK:(h + 1) * QK] + pv
            m_s[h] = m_new

        @pl.when(ki == qi)
        def _():
            l_ref[...] = jnp.zeros_like(l_ref)
            for h in range(HEADS):
                sl = slice(h * HD, (h + 1) * HD)
                den = acc[:, h * QK + HD:h * QK + HD + 1]
                out = acc[:, h * QK:h * QK + HD] / den
                o_ref[:, sl] = out
                ob_ref[:, sl] = out.astype(BF16)
                l_ref[:, h:h + 1] = m_s[h] * SCALE_B + jnp.log(den)

    q_spec, k_spec, _, qrow, _ = _mla_specs(tq)
    return _call(
        body, name, (tables[0].shape[0],), [q_spec, k_spec, k_spec], [qrow, qrow, _mla_stat_spec(tq)],
        [jax.ShapeDtypeStruct((T, AW), F32), jax.ShapeDtypeStruct((T, AW), BF16), jax.ShapeDtypeStruct((T, HD), F32)],
        (q, k, v1),
        scratch=[pltpu.VMEM((HEADS, tq, 1), F32), pltpu.VMEM((tq, HEADS * QK), F32)],
        sem=("arbitrary",), comm=comm, prefetch=tables)


def _mla_ds(q_ref, k_ref, v_ref, do_ref, l_ref, dl_ref, h, qi, ki, tq):
    sl = slice(h * HD, (h + 1) * HD)
    p = jnp.exp(_mla_scores(q_ref, k_ref, h, qi, ki, tq) - l_ref[:, h:h + 1])
    ds = (p * (_dot(do_ref[:, sl], v_ref[:, sl], NT) - dl_ref[:, h:h + 1]) * SCALE_B).astype(BF16)
    return p, ds


def _mla_bwd_q(q, k, kv, dout_b, lse, delta, name, comm=None):
    T = q.shape[0]
    tq = _tile(T, MLA_TQ)
    tables = _triangle(T // tq, False)

    def body(qi_ref, ki_ref, q_ref, k_ref, v_ref, do_ref, l_ref, dl_ref, dq_ref):
        t = pl.program_id(0)
        qi, ki = qi_ref[t], ki_ref[t]

        @pl.when(ki == 0)
        def _():
            dq_ref[...] = jnp.zeros_like(dq_ref)

        for h in range(HEADS):
            _, ds = _mla_ds(q_ref, k_ref, v_ref, do_ref, l_ref, dl_ref, h, qi, ki, tq)
            dq_ref[:, h * QK:(h + 1) * QK] += _dot(ds, k_ref[:, h * QK:(h + 1) * QK], NN)

    q_spec, k_spec, v_spec, qrow, _ = _mla_specs(tq)
    return _call(
        body, name, (tables[0].shape[0],), [q_spec, k_spec, v_spec, qrow, _mla_stat_spec(tq), _mla_stat_spec(tq)], [q_spec],
        [jax.ShapeDtypeStruct((T, HEADS * QK), F32)], (q, k, kv, dout_b, lse, delta),
        sem=("arbitrary",), comm=comm, prefetch=tables)[0]


def _mla_bwd_kv(q, k, kv, dout_b, lse, delta, name, comm=None):
    T = q.shape[0]
    tq = _tile(T, MLA_TQ)
    nq = T // tq
    tables = _triangle(nq, True)

    def body(qi_ref, ki_ref, q_ref, k_ref, v_ref, do_ref, l_ref, dl_ref, dk_ref, dkv_ref, dv_acc):
        t = pl.program_id(0)
        qi, ki = qi_ref[t], ki_ref[t]

        @pl.when(qi == ki)
        def _():
            dk_ref[...] = jnp.zeros_like(dk_ref)
            dv_acc[...] = jnp.zeros_like(dv_acc)

        for h in range(HEADS):
            sl = slice(h * HD, (h + 1) * HD)
            p, ds = _mla_ds(q_ref, k_ref, v_ref, do_ref, l_ref, dl_ref, h, qi, ki, tq)
            dv_acc[:, sl] += _dot(p.astype(BF16), do_ref[:, sl], TN)
            dk_ref[:, h * QK:(h + 1) * QK] += _dot(ds, q_ref[:, h * QK:(h + 1) * QK], TN)

        @pl.when(qi == nq - 1)
        def _():
            for h in range(HEADS):
                dkv_ref[:, h * HD:(h + 1) * HD] = dk_ref[:, h * QK:h * QK + HD].astype(BF16)
                dkv_ref[:, AW + h * HD:AW + (h + 1) * HD] = dv_acc[:, h * HD:(h + 1) * HD].astype(BF16)

    q_spec, k_spec, v_spec, qrow, _ = _mla_specs(tq)
    return _call(
        body, name, (tables[0].shape[0],),
        [q_spec, k_spec, v_spec, qrow, _mla_stat_spec(tq), _mla_stat_spec(tq)],
        [k_spec, pl.BlockSpec((tq, 2 * AW), lambda t, qi, ki: (ki[t], 0))],
        [jax.ShapeDtypeStruct((T, HEADS * QK), F32), jax.ShapeDtypeStruct((T, 2 * AW), BF16)],
        (q, k, kv, dout_b, lse, delta), scratch=[pltpu.VMEM((tq, AW), F32)],
        sem=("arbitrary",), comm=comm, prefetch=tables)


def _pair_sum(by_device, from_sibling, name, tb=256):
    n_chip, R, C = from_sibling.shape
    tb = _tile(R, tb)
    core = jnp.reshape(lax.axis_index("c"), (1,)).astype(jnp.int32)

    def body(core_ref, mine_ref, theirs_ref, o_ref):
        o_ref[...] = (mine_ref[...].astype(F32) + theirs_ref[...].astype(F32)).astype(o_ref.dtype)

    blk = pl.BlockSpec((None, tb, C), lambda p, i, core_ref: (p, i, 0))
    return _call(
        body, name, (n_chip, R // tb),
        [pl.BlockSpec((None, tb, C), lambda p, i, core_ref: (2 * p + core_ref[0], i, 0)), blk], [blk],
        [jax.ShapeDtypeStruct((n_chip, R, C), BF16)], (by_device, from_sibling),
        sem=("parallel", "parallel"), prefetch=(core,))[0]


def _adamw(parts, w, m, v, name, tb=128, comm=None):
    R, C = w.shape
    n_parts = parts.shape[0]
    tb = _tile(R, tb)
    c1 = 1.0 - ADAM_B1
    c2 = 1.0 - ADAM_B2
    bc1 = 1.0 - ADAM_B1 ** ADAM_STEP
    bc2 = 1.0 - ADAM_B2 ** ADAM_STEP

    def body(p_ref, w_ref, m_ref, v_ref, g_ref, d_ref, nm_ref, nv_ref):
        g = p_ref[0].astype(F32)
        for j in range(1, n_parts):
            g = g + p_ref[j].astype(F32)
        nm = ADAM_B1 * m_ref[...] + c1 * g
        nv = ADAM_B2 * v_ref[...] + c2 * (g * g)
        g_ref[...] = g
        nm_ref[...] = nm
        nv_ref[...] = nv
        d_ref[...] = -ADAM_LR * ((nm / bc1) / (jnp.sqrt(nv / bc2) + ADAM_EPS) + ADAM_WD * w_ref[...])

    row = pl.BlockSpec((tb, C), lambda i: (i, 0))
    return _call(
        body, name, (R // tb,), [pl.BlockSpec((n_parts, tb, C), lambda i: (0, i, 0)), row, row, row], [row] * 4,
        [jax.ShapeDtypeStruct((R, C), F32)] * 4, (parts, w, m, v), sem=("parallel",), comm=comm)


def _cols_from_shards(g):
    return jnp.transpose(g, (1, 0, 2)).reshape(g.shape[1], N_DEV * g.shape[2])


def _cols_to_shards(w):
    return jnp.transpose(w.reshape(w.shape[0], N_DEV, w.shape[1] // N_DEV), (1, 0, 2))


def _split_heads(w, first):
    w3 = w.reshape(w.shape[0], HEADS, -1)
    return w3[:, :, :first].reshape(w.shape[0], -1), w3[:, :, first:].reshape(w.shape[0], -1)


def _join_heads(a, b):
    R = a.shape[0]
    return jnp.concatenate([a.reshape(R, HEADS, -1), b.reshape(R, HEADS, -1)], axis=2).reshape(R, -1)


def _pad_heads(w, width):
    w3 = w.reshape(w.shape[0], HEADS, -1)
    return jnp.pad(w3, ((0, 0), (0, 0), (0, width - w3.shape[2]))).reshape(w.shape[0], HEADS * width)


def _unpad_heads(w, k):
    return w.reshape(w.shape[0], HEADS, -1)[:, :, :k].reshape(w.shape[0], HEADS * k)


def kernel(x, positions, norm_attn_pre, norm_attn_post, w_in, q_latent_norm, kv_latent_norm, w_uq, w_ukv, w_out, norm_mlp_pre, norm_mlp_post, w_up, w_down, loss_target, m_norm_attn_pre, m_norm_attn_post, m_w_in, m_q_latent_norm, m_kv_latent_norm, m_w_uq, m_w_ukv, m_w_out, m_norm_mlp_pre, m_norm_mlp_post, m_w_up, m_w_down, v_norm_attn_pre, v_norm_attn_post, v_w_in, v_q_latent_norm, v_kv_latent_norm, v_w_uq, v_w_ukv, v_w_out, v_norm_mlp_pre, v_norm_mlp_post, v_w_up, v_w_down):
    xs = x[0]
    tgt = loss_target[0]
    pos = positions[0]
    T, D = xs.shape
    big = dict(w_in=(w_in, m_w_in, v_w_in), w_uq=(w_uq, m_w_uq, v_w_uq), w_ukv=(w_ukv, m_w_ukv, v_w_ukv),
               w_out=(w_out, m_w_out, v_w_out), w_up=(w_up, m_w_up, v_w_up), w_down=(w_down, m_w_down, v_w_down))
    big = {n: tuple(t[0] for t in ts) for n, ts in big.items()}
    big_names = ["w_in", "w_uq", "w_ukv", "w_out", "w_up", "w_down"]
    col_sharded = {"w_in", "w_uq", "w_ukv", "w_up"}

    wb = {n: big[n][0].astype(BF16) for n in big_names}

    def gathered(ex, i, n):
        g = ex.results[i]
        return _cols_from_shards(g) if n in col_sharded else g.reshape(-1, g.shape[2])

    def by_device(g, n):
        return _cols_to_shards(g) if n in col_sharded else g.reshape(N_DEV, g.shape[0] // N_DEV, g.shape[1])

    def scatter_of(g, n):
        return _Exchange([by_device(g, n)], "scatter")

    cos_a, sin_a = _rope_tables(pos, ROT_A)
    cos_b, sin_b = _rope_tables(pos, ROPE_MLA)

    ex_in = _Exchange([wb["w_in"]], "gather")
    h1 = _rms_fwd(xs, norm_attn_pre, BF16, "norm_attn_pre_fwd", comm=ex_in)
    Wi = jnp.pad(gathered(ex_in, 0, "w_in"), ((0, 0), (0, IN_PAD - IN_COLS)))
    ex_mid = _Exchange([wb["w_uq"], wb["w_ukv"], wb["w_out"]], "gather")
    proj = _mm(h1, Wi, "nn", [F32], "proj_in", tn=1408, comm=ex_mid)
    Wuq = _pad_heads(gathered(ex_mid, 0, "w_uq"), QK)
    Wukv = jnp.concatenate(_split_heads(gathered(ex_mid, 1, "w_ukv"), HD), axis=1)
    Wo = gathered(ex_mid, 2, "w_out")
    n_piece = wb["w_down"].shape[0] // 8
    ex_down = [_Exchange([wb["w_down"][i * n_piece:(i + 1) * n_piece]], "gather") for i in range(8)]
    qkv_by_d = _rope_dswa(proj, cos_a, sin_a, "rope_dswa", comm=ex_down[0])
    outs, lses = [], []
    for d, qkv, ex in zip(DSWA_DILATIONS, qkv_by_d, ex_down[1:4]):
        o, l = _dswa_fwd(qkv, f"dswa_fwd_d{d}", comm=ex)
        outs.append(o)
        lses.append(l)
    a_out, a_out_b, a_lse_by_d = _dswa_merge(outs, lses, "dswa_merge", comm=ex_down[4])

    cqn = _rms_fwd(proj, q_latent_norm, BF16, "q_latent_norm_fwd", width=Q_LORA, col_block=3 * AW // Q_LORA)
    ckvn = _rms_fwd(proj, kv_latent_norm, BF16, "kv_latent_norm_fwd", width=KV_LORA, col_block=3 * AW // KV_LORA + 1)
    qb = _mm(cqn, Wuq, "nn", [F32], "q_up")
    kvb = _mm(ckvn, Wukv, "nn", [BF16], "kv_up")
    odd = lambda j: j % 2 == 1
    q_mla = _rope_apply(qb, cos_b, sin_b, ROPE_MLA // 2, 2 * HEADS, odd, BF16, "rope_mla_q")
    kr = _rope_apply(proj, cos_b, sin_b, ROPE_MLA // 2, 1, lambda j: True, BF16, "rope_mla_k",
                     window=(IN_PAD - HD) // HD)
    k_mla = jnp.concatenate([kvb[:, :AW].reshape(T, HEADS, HD), jnp.broadcast_to(kr[:, None, :], (T, HEADS, HD))],
                            axis=2).reshape(T, HEADS * QK)
    v1_mla = jnp.concatenate([kvb[:, AW:].reshape(T, HEADS, HD), jnp.ones((T, HEADS, 1), BF16),
                              jnp.zeros((T, HEADS, HD - 1), BF16)], axis=2).reshape(T, HEADS * QK)
    ex_up = _Exchange([wb["w_up"]], "gather")
    b_out, b_out_b, b_lse = _mla_fwd(q_mla, k_mla, v1_mla, "mla_fwd", comm=ex_up)
    Wup_shards = ex_up.results[0]

    mixed = jnp.concatenate([a_out_b, b_out_b], axis=1)
    y1 = _mm(mixed, Wo, "nn", [F32], "attn_out", comm=ex_down[5])

    x2, h2 = _norm_pair_fwd(y1, xs, norm_attn_post, norm_mlp_pre, "norm_attn_post_mlp_pre_fwd")

    def relu2(z):
        r = jnp.maximum(z, 0.0)
        return r * r, r

    u, zr = _mm(h2, Wup_shards, "nn", [BF16, BF16], "mlp_up", epilogue=relu2, comm=ex_down[6:8],
                b_shards=True)
    Wdn = jnp.concatenate([ex.results[0] for ex in ex_down], axis=1).reshape(-1, D)
    y2 = _mm(u, Wdn, "nn", [F32], "mlp_down")
    dx3, dy2, loss_part, dg_mlp_post = _loss_head(x2, y2, norm_mlp_post, tgt, "loss_head")

    dz =_mm(dy2, Wdn, "nt", [BF16], "mlp_down_dx", epilogue=lambda du, r: (du * (2.0 * r.astype(F32)),), extras=(zr,))
    g_down = _mm(u, dy2, "tn", [BF16], "mlp_down_dw")
    down_dev = by_device(g_down, "w_down")
    pair_down = _Exchange([down_dev], "pair")
    up_dev = _mm(h2, dz, "tn", [BF16], "mlp_up_dw", comm=pair_down, out_shards=True)
    down_chip = _pair_sum(down_dev, pair_down.results[0], "pair_sum_w_down")
    pair_up = _Exchange([up_dev], "pair")
    cut = 5 * down_chip.shape[1] // 8
    sc_down = [_Exchange([down_chip[:, :cut]], "chips"), _Exchange([down_chip[:, cut:]], "chips")]
    dh2 = _mm(dz, Wup_shards, "nt", [F32], "mlp_up_dx", comm=[pair_up, sc_down[0]], b_shards=True)
    up_chip = _pair_sum(up_dev, pair_up.results[0], "pair_sum_w_up")

    dx2, dy1, dg_mlp_pre, dg_attn_post = _norm_pair_bwd(dh2, x2, norm_mlp_pre, dx3, y1, norm_attn_post,
                                                        "norm_mlp_pre_attn_post_bwd")
    dmixed = _mm(dy1, Wo, "nt", [F32], "attn_out_dx")
    g_out = _mm(mixed, dy1, "tn", [BF16], "attn_out_dw")

    b_delta, b_dout = _delta_prep(dmixed, 1, b_out, "mla_delta")
    sc_up = _Exchange([up_chip], "chips")
    dq_mla = _mla_bwd_q(q_mla, k_mla, kvb, b_dout, b_lse, b_delta, "mla_bwd_q", comm=sc_up)
    sc_out = scatter_of(g_out, "w_out")
    dk_mla, dkvb = _mla_bwd_kv(q_mla, k_mla, kvb, b_dout, b_lse, b_delta, "mla_bwd_kv", comm=[sc_out, sc_down[1]])
    dqb = _rope_apply(dq_mla, cos_b, -sin_b, ROPE_MLA // 2, 2 * HEADS, odd, BF16, "rope_mla_q_bwd")
    d_kr = _shared_key_grad(dk_mla, cos_b, -sin_b, ROPE_MLA // 2, "rope_mla_k_bwd")
    g_uq_pad = _mm(cqn, dqb, "tn", [BF16], "q_up_dw")
    g_ukv_perm = _mm(ckvn, dkvb, "tn", [BF16], "kv_up_dw")
    dcqn = _mm(dqb, Wuq, "nt", [F32], "q_up_dx")
    dckvn = _mm(dkvb, Wukv, "nt", [F32], "kv_up_dx")
    d_cq, dg_q = _rms_bwd(dcqn, proj, q_latent_norm, BF16, "q_latent_norm_bwd", width=Q_LORA, col_block=3 * AW // Q_LORA)
    d_ckv, dg_kv = _rms_bwd(dckvn, proj, kv_latent_norm, BF16, "kv_latent_norm_bwd", width=KV_LORA,
                            col_block=3 * AW // KV_LORA + 1)

    g_uq = _unpad_heads(g_uq_pad, HD + ROPE_MLA)
    g_ukv = _join_heads(g_ukv_perm[:, :AW], g_ukv_perm[:, AW:])
    sc_uq = _Exchange([_cols_to_shards(g_uq), _cols_to_shards(g_ukv)], "scatter")
    a_delta_by_d, a_dout_by_d = _dswa_delta(dmixed, a_out, "dswa_delta")
    a_grads = [_dswa_bwd(qkv_by_d[c], a_dout_by_d[c], a_lse_by_d[c], a_delta_by_d[c], f"dswa_bwd_d{d}")
               for c, d in enumerate(DSWA_DILATIONS)]
    d_aq, d_ak, d_av = _dswa_combine(a_grads, cos_a, -sin_a, "dswa_combine")

    dproj = jnp.concatenate([d_aq, d_ak, d_av, d_cq, d_ckv, d_kr], axis=1)
    g_in_pad = _mm(h1, dproj, "tn", [BF16], "proj_in_dw", tn=1408, comm=sc_uq)
    in_dev = by_device(g_in_pad[:, :IN_COLS], "w_in")
    pair_in = _Exchange([in_dev], "pair")
    dh1 = _mm(dproj, Wi, "nt", [F32], "proj_in_dx", comm=pair_in)
    in_chip = _pair_sum(in_dev, pair_in.results[0], "pair_sum_w_in")
    cuts = [0, 3 * D // 8, 11 * D // 16, D]
    sc_in = [_Exchange([in_chip[:, a:b]], "chips") for a, b in zip(cuts[:-1], cuts[1:])]
    grad_x, dg_attn_pre = _rms_bwd(dh1, xs, norm_attn_pre, F32, "norm_attn_pre_bwd", residual=dx2, comm=sc_in[0])

    down_parts = jnp.concatenate([sc.results[0] for sc in sc_down], axis=1)
    big_out = dict(w_down=_adamw(down_parts, *big["w_down"], "adamw_w_down", comm=sc_in[1]),
                   w_up=_adamw(sc_up.results[0], *big["w_up"], "adamw_w_up", comm=sc_in[2]))
    parts = dict(w_in=jnp.concatenate([sc.results[0] for sc in sc_in], axis=1), w_uq=sc_uq.results[0],
                 w_ukv=sc_uq.results[1], w_out=sc_out.results[0])
    big_out.update({n: _adamw(parts[n], *big[n], f"adamw_{n}") for n in parts})

    gain_names = ["norm_attn_pre", "norm_attn_post", "q_latent_norm", "kv_latent_norm", "norm_mlp_pre", "norm_mlp_post"]
    gain_args = dict(norm_attn_pre=(norm_attn_pre, m_norm_attn_pre, v_norm_attn_pre),
                     norm_attn_post=(norm_attn_post, m_norm_attn_post, v_norm_attn_post),
                     q_latent_norm=(q_latent_norm, m_q_latent_norm, v_q_latent_norm),
                     kv_latent_norm=(kv_latent_norm, m_kv_latent_norm, v_kv_latent_norm),
                     norm_mlp_pre=(norm_mlp_pre, m_norm_mlp_pre, v_norm_mlp_pre),
                     norm_mlp_post=(norm_mlp_post, m_norm_mlp_post, v_norm_mlp_post))
    gain_grads = dict(norm_attn_pre=dg_attn_pre, norm_attn_post=dg_attn_post, q_latent_norm=dg_q,
                      kv_latent_norm=dg_kv, norm_mlp_pre=dg_mlp_pre, norm_mlp_post=dg_mlp_post)
    packed = jnp.concatenate([gain_grads[n] for n in gain_names], axis=1)
    gain_parts = _Exchange([packed], "gather").standalone("gather_gain_grads")[0]
    pack3 = lambda i: jnp.concatenate([gain_args[n][i] for n in gain_names], axis=1)
    gain_out = _adamw(gain_parts, pack3(0), pack3(1), pack3(2), "adamw_gains", tb=1)
    offs = [0]
    for n in gain_names:
        offs.append(offs[-1] + gain_args[n][0].shape[1])
    small_out = {n: tuple(o[:, offs[i]:offs[i + 1]] for o in gain_out) for i, n in enumerate(gain_names)}

    loss = lax.psum(loss_part[0, 0], ("x", "y", "c"))

    order = ["norm_attn_pre", "norm_attn_post", "w_in", "q_latent_norm", "kv_latent_norm", "w_uq", "w_ukv", "w_out",
             "norm_mlp_pre", "norm_mlp_post", "w_up", "w_down"]
    res = {n: (small_out[n] if n in small_out else tuple(o[None] for o in big_out[n])) for n in order}
    return (loss, grad_x[None], *[res[n][0] for n in order], *[res[n][1] for n in order],
            *[res[n][2] for n in order], *[res[n][3] for n in order])
```

```python
import functools
import math

import jax
import jax.numpy as jnp
from jax import lax
from jax.experimental import pallas as pl
from jax.experimental.pallas import tpu as pltpu

F32 = jnp.float32
BF16 = jnp.bfloat16

N_DEV = 8
HEADS = 8
HD = 128
AW = HEADS * HD
Q_LORA = 512
KV_LORA = 512
ROPE_MLA = 64
ROT_A = 32
IN_COLS = 3 * AW + Q_LORA + KV_LORA + ROPE_MLA
IN_PAD = 3 * AW + Q_LORA + KV_LORA + HD
QBLK = 128
DSWA_DILATIONS = (1, 4, 16)
ROPE_THETA = 500000.0
NORM_EPS = 1e-6
NEG_INF = -1e30
SCALE_A = HD ** -0.5
SCALE_B = (HD + ROPE_MLA) ** -0.5

ADAM_LR = 0.001
ADAM_B1 = 0.9
ADAM_B2 = 0.999
ADAM_EPS = 1e-08
ADAM_WD = 0.01
ADAM_STEP = 10

VMEM_LIMIT = 48 * 1024 * 1024

NT = (((1,), (1,)), ((), ()))
NN = (((1,), (0,)), ((), ()))
TN = (((0,), (0,)), ((), ()))


def _dot(a, b, dims):
    return lax.dot_general(a, b, dims, preferred_element_type=F32)


def _params(*sem):
    return pltpu.CompilerParams(dimension_semantics=sem, vmem_limit_bytes=VMEM_LIMIT)


def _tile(n, want):
    t = min(n, want)
    while n % t:
        t //= 2
    return t


def _tile128(n, want):
    if n % 128:
        return n
    units = n // 128
    return 128 * max(u for u in range(1, max(want // 128, 1) + 1) if units % u == 0)


class _Exchange:
    def __init__(self, arrs, mode):
        self.arrs = list(arrs)
        self.mode = mode
        self.n = len(self.arrs)
        self.results = None
        hbm = pl.BlockSpec(memory_space=pltpu.HBM)
        self.specs = [hbm] * self.n
        shape = {"gather": lambda a: (N_DEV,) + a.shape, "scatter": lambda a: a.shape,
                 "pair": lambda a: (4,) + a.shape[1:], "chips": lambda a: a.shape}[mode]
        self.out_shape = [jax.ShapeDtypeStruct(shape(a), a.dtype) for a in self.arrs]
        n_sem = self.n * (N_DEV - 1)
        self.scratch = [pltpu.SemaphoreType.DMA((n_sem,)), pltpu.SemaphoreType.DMA((n_sem,)),
                        pltpu.SemaphoreType.DMA((self.n,))]

    def hooks(self, ins, outs, send_sems, recv_sems, local_sems):
        x, y, c = lax.axis_index("x"), lax.axis_index("y"), lax.axis_index("c")
        me = (x, y, c)
        sib = (x, y, 1 - c)
        chips = [(1 - x, y), (x, 1 - y), (1 - x, 1 - y)]
        slot = lambda p: 4 * p[0] + 2 * p[1] + p[2]
        chip_of = lambda p: 2 * p[0] + p[1]

        def rcopy(a, k, src, dst, to):
            i = a * (N_DEV - 1) + k
            return pltpu.make_async_remote_copy(src_ref=src, dst_ref=dst, send_sem=send_sems.at[i],
                                                recv_sem=recv_sems.at[i], device_id=to,
                                                device_id_type=pl.DeviceIdType.MESH)

        def local(a):
            if self.mode == "chips":
                return pltpu.make_async_copy(ins[a].at[chip_of(me)], outs[a].at[chip_of(me)], local_sems.at[a])
            src = ins[a].at[slot(me)] if self.mode == "scatter" else ins[a]
            return pltpu.make_async_copy(src, outs[a].at[slot(me)], local_sems.at[a])

        def peer(rel):
            return (1 - x if rel & 4 else x, 1 - y if rel & 2 else y, 1 - c if rel & 1 else c)

        if self.mode == "pair":
            def start():
                for a in range(self.n):
                    for p in range(4):
                        rcopy(a, p, ins[a].at[2 * p + 1 - c], outs[a].at[p], sib).start()

            def middle():
                pass

            def finish():
                for a in range(self.n):
                    for p in range(4):
                        cp = rcopy(a, p, ins[a].at[2 * p + 1 - c], outs[a].at[p], sib)
                        cp.wait_send()
                        cp.wait_recv()
        elif self.mode == "chips":
            def start():
                for a in range(self.n):
                    local(a).start()
                    for j, chip in enumerate(chips):
                        rcopy(a, j, ins[a].at[chip_of(chip)], outs[a].at[chip_of(me)], (*chip, c)).start()

            def middle():
                pass

            def finish():
                for a in range(self.n):
                    for j, chip in enumerate(chips):
                        cp = rcopy(a, j, ins[a].at[chip_of(chip)], outs[a].at[chip_of(chip)], (*chip, c))
                        cp.wait_send()
                        cp.wait_recv()
                    local(a).wait()
        elif self.mode == "scatter":
            def start():
                for a in range(self.n):
                    local(a).start()
                    for rel in range(1, N_DEV):
                        rcopy(a, rel - 1, ins[a].at[slot(peer(rel))], outs[a].at[slot(me)], peer(rel)).start()

            def middle():
                pass

            def finish():
                for a in range(self.n):
                    for rel in range(1, N_DEV):
                        cp = rcopy(a, rel - 1, ins[a].at[slot(peer(rel))], outs[a].at[slot(peer(rel))], peer(rel))
                        cp.wait_send()
                        cp.wait_recv()
                    local(a).wait()
        else:
            def start():
                for a in range(self.n):
                    local(a).start()
                    rcopy(a, 0, ins[a], outs[a].at[slot(me)], sib).start()
                    for j, chip in enumerate(chips):
                        rcopy(a, 1 + j, ins[a], outs[a].at[slot(me)], (*chip, c)).start()

            def middle():
                for a in range(self.n):
                    for j, chip in enumerate(chips):
                        landed = outs[a].at[slot((*chip, c))]
                        rcopy(a, 1 + j, ins[a], landed, me).wait_recv()
                        rcopy(a, 4 + j, landed, landed, sib).start()

            def finish():
                for a in range(self.n):
                    rcopy(a, 0, ins[a], outs[a].at[slot(sib)], me).wait_recv()
                    for j, chip in enumerate(chips):
                        rcopy(a, 4 + j, ins[a], outs[a].at[slot((*chip, 1 - c))], me).wait_recv()
                    for k in range(N_DEV - 1):
                        rcopy(a, k, ins[a], outs[a].at[slot(me)], me).wait_send()
                    local(a).wait()

        return start, middle, finish

    def set_results(self, res):
        self.results = list(res)

    def standalone(self, name):
        n = self.n

        def body(*refs):
            start, middle, finish = self.hooks(refs[:n], refs[n:2 * n], *refs[2 * n:])
            start()
            middle()
            finish()

        self.results = pl.pallas_call(
            body, name=name, in_specs=self.specs, out_specs=self.specs, out_shape=self.out_shape,
            scratch_shapes=self.scratch, compiler_params=pltpu.CompilerParams(has_side_effects=True),
        )(*self.arrs)
        return self.results


class _Carried:
    def __init__(self, parts):
        self.parts = list(parts)
        self.n = sum(p.n for p in self.parts)
        self.arrs = [a for p in self.parts for a in p.arrs]
        self.specs = [s for p in self.parts for s in p.specs]
        self.out_shape = [s for p in self.parts for s in p.out_shape]
        self.scratch = [s for p in self.parts for s in p.scratch]

    def hooks(self, ins, outs, *sems):
        hooks, i = [], 0
        for j, p in enumerate(self.parts):
            hooks.append(p.hooks(ins[i:i + p.n], outs[i:i + p.n], *sems[3 * j:3 * j + 3]))
            i += p.n
        def phase(k):
            def run():
                for h in hooks:
                    h[k]()
            return run

        return phase(0), phase(1), phase(2)

    def set_results(self, res):
        i = 0
        for p in self.parts:
            p.set_results(res[i:i + p.n])
            i += p.n


def _call(body, name, grid, in_specs, out_specs, out_shape, args, scratch=(), sem=(), comm=None, prefetch=()):
    npf = len(prefetch)
    if isinstance(comm, (list, tuple)):
        comm = _Carried(comm)
    if comm is None:
        spec = pltpu.PrefetchScalarGridSpec(num_scalar_prefetch=npf, grid=grid, in_specs=list(in_specs),
                                            out_specs=list(out_specs), scratch_shapes=list(scratch))
        return pl.pallas_call(body, name=name, grid_spec=spec, out_shape=list(out_shape),
                              compiler_params=_params(*sem))(*prefetch, *args)
    ni, no, ns, n = len(in_specs), len(out_specs), len(scratch), comm.n
    steps = math.prod(grid)

    def wrapped(*refs):
        pf, refs = refs[:npf], refs[npf:]
        ins, c_ins = refs[:ni], refs[ni:ni + n]
        outs, c_outs = refs[ni + n:ni + n + no], refs[ni + n + no:ni + 2 * n + no]
        scr, c_scr = refs[ni + 2 * n + no:ni + 2 * n + no + ns], refs[ni + 2 * n + no + ns:]
        start, middle, finish = comm.hooks(c_ins, c_outs, *c_scr)
        step = pl.program_id(0)
        for ax in range(1, len(grid)):
            step = step * grid[ax] + pl.program_id(ax)
        pl.when(step == 0)(start)
        pl.when(step == steps // 2)(middle)
        body(*pf, *ins, *outs, *scr)
        pl.when(step == steps - 1)(finish)

    spec = pltpu.PrefetchScalarGridSpec(num_scalar_prefetch=npf, grid=grid, in_specs=list(in_specs) + comm.specs,
                                        out_specs=list(out_specs) + comm.specs,
                                        scratch_shapes=list(scratch) + comm.scratch)
    res = pl.pallas_call(
        wrapped, name=name, grid_spec=spec, out_shape=list(out_shape) + comm.out_shape,
        compiler_params=pltpu.CompilerParams(dimension_semantics=("arbitrary",) * len(grid),
                                             vmem_limit_bytes=VMEM_LIMIT, has_side_effects=True),
    )(*prefetch, *args, *comm.arrs)
    comm.set_results(res[no:])
    return res[:no]


def _mm(a, b, mode, out_dtypes, name, epilogue=None, extras=(), tm=1024, tn=1024, tk=2048, comm=None,
        b_shards=False, out_shards=False):
    if mode == "tn":
        K, M = a.shape
    else:
        M, K = a.shape
    if b_shards:
        N = b.shape[1] if mode == "nt" else N_DEV * b.shape[2]
    else:
        N = b.shape[0] if mode == "nt" else b.shape[1]
    tm, tn, tk = _tile128(M, tm), _tile128(N, tn), _tile128(K, tk)
    pair_k = b_shards and mode == "nt"
    if pair_k:
        tk = 2 * K // N_DEV
        b = b.reshape(N_DEV // 2, 2, *b.shape[1:])
    elif b_shards or out_shards:
        tn = N // N_DEV
    nk = K // tk
    dims = {"nn": NN, "nt": NT, "tn": TN}[mode]
    a_spec = (pl.BlockSpec((tk, tm), lambda i, j, k: (k, i)) if mode == "tn"
              else pl.BlockSpec((tm, tk), lambda i, j, k: (i, k)))
    if b_shards:
        b_spec = (pl.BlockSpec((None, 2, tn, tk // 2), lambda i, j, k: (k, 0, j, 0)) if mode == "nt"
                  else pl.BlockSpec((None, tk, tn), lambda i, j, k: (j, k, 0)))
    else:
        b_spec = (pl.BlockSpec((tn, tk), lambda i, j, k: (j, k)) if mode == "nt"
                  else pl.BlockSpec((tk, tn), lambda i, j, k: (k, j)))
    mn_spec = pl.BlockSpec((tm, tn), lambda i, j, k: (i, j))
    out_spec = pl.BlockSpec((None, tm, tn), lambda i, j, k: (j, i, 0)) if out_shards else mn_spec
    out_dims = (N_DEV, M, N // N_DEV) if out_shards else (M, N)
    n_ex = len(extras)
    n_out = len(out_dtypes)

    def finish(acc, ex, outs):
        res = (acc,) if epilogue is None else epilogue(acc, *[e[...] for e in ex])
        for o, r in zip(outs, res):
            o[...] = r.astype(o.dtype)

    def product(a_ref, b_ref):
        if pair_k:
            return _dot(a_ref[:, :tk // 2], b_ref[0], NT) + _dot(a_ref[:, tk // 2:], b_ref[1], NT)
        return _dot(a_ref[...], b_ref[...], dims)

    def body(*refs):
        a_ref, b_ref = refs[:2]
        ex = refs[2:2 + n_ex]
        outs = refs[2 + n_ex:2 + n_ex + n_out]
        if nk == 1:
            finish(product(a_ref, b_ref), ex, outs)
            return
        acc = refs[-1]
        k = pl.program_id(2)

        @pl.when(k == 0)
        def _():
            acc[...] = product(a_ref, b_ref)

        @pl.when(jnp.logical_and(k > 0, k < nk - 1))
        def _():
            acc[...] += product(a_ref, b_ref)

        @pl.when(k == nk - 1)
        def _():
            finish(acc[...] + product(a_ref, b_ref), ex, outs)

    out = _call(
        body, name, (M // tm, N // tn, nk), [a_spec, b_spec] + [mn_spec] * n_ex, [out_spec] * n_out,
        [jax.ShapeDtypeStruct(out_dims, dt) for dt in out_dtypes], (a, b, *extras),
        scratch=[] if nk == 1 else [pltpu.VMEM((tm, tn), F32)], sem=("parallel", "parallel", "arbitrary"),
        comm=comm)
    return out[0] if n_out == 1 else out


def _rms_fwd(x, gain, out_dtype, name, width=None, col_block=0, residual=None, tb=256, comm=None):
    T = x.shape[0]
    W = x.shape[1] if width is None else width
    tb = _tile(T, tb)
    has_res = residual is not None

    def body(*refs):
        x_ref, g_ref = refs[:2]
        o_ref = refs[-1]
        xf = x_ref[...]
        y = xf * lax.rsqrt(jnp.mean(xf * xf, axis=-1, keepdims=True) + NORM_EPS) * g_ref[...]
        if has_res:
            y = refs[2][...] + y
        o_ref[...] = y.astype(o_ref.dtype)

    row = pl.BlockSpec((tb, W), lambda i: (i, 0))
    ins = [x, gain] + ([residual] if has_res else [])
    return _call(
        body, name, (T // tb,),
        [pl.BlockSpec((tb, W), lambda i: (i, col_block)),
         pl.BlockSpec((1, W), lambda i: (0, 0))] + ([row] if has_res else []),
        [row], [jax.ShapeDtypeStruct((T, W), out_dtype)], ins, sem=("parallel",), comm=comm)[0]


def _rms_bwd(dy, x, gain, out_dtype, name, width=None, col_block=0, residual=None, tb=256, comm=None):
    T = dy.shape[0]
    W = x.shape[1] if width is None else width
    tb = _tile(T, tb)
    has_res = residual is not None

    def body(*refs):
        dy_ref, x_ref, g_ref = refs[:3]
        dx_ref, dg_ref = refs[-2:]
        i = pl.program_id(0)
        xf = x_ref[...]
        r = lax.rsqrt(jnp.mean(xf * xf, axis=-1, keepdims=True) + NORM_EPS)
        xn = xf * r
        dyf = dy_ref[...].astype(F32)
        dyg = dyf * g_ref[...]
        dx = r * (dyg - xn * jnp.mean(dyg * xn, axis=-1, keepdims=True))
        if has_res:
            dx = refs[3][...] + dx
        dx_ref[...] = dx.astype(dx_ref.dtype)

        @pl.when(i == 0)
        def _():
            dg_ref[...] = jnp.zeros_like(dg_ref)

        dg_ref[...] += jnp.sum(dyf * xn, axis=0, keepdims=True)

    row = pl.BlockSpec((tb, W), lambda i: (i, 0))
    vec = pl.BlockSpec((1, W), lambda i: (0, 0))
    ins = [dy, x, gain] + ([residual] if has_res else [])
    return _call(
        body, name, (T // tb,),
        [row, pl.BlockSpec((tb, W), lambda i: (i, col_block)), vec] + ([row] if has_res else []), [row, vec],
        [jax.ShapeDtypeStruct((T, W), out_dtype), jax.ShapeDtypeStruct((1, W), F32)], ins,
        sem=("arbitrary",), comm=comm)


def _rms(xf):
    r = lax.rsqrt(jnp.mean(xf * xf, axis=-1, keepdims=True) + NORM_EPS)
    return r, xf * r


def _rms_grad(dyf, xn, r, gain):
    dyg = dyf * gain
    return r * (dyg - xn * jnp.mean(dyg * xn, axis=-1, keepdims=True)), dyf * xn


def _accumulate_rows(i, ref, rows):
    @pl.when(i == 0)
    def _():
        ref[...] = jnp.zeros_like(ref)

    ref[...] += jnp.sum(rows, axis=0, keepdims=True)


def _norm_pair_fwd(y1, xs, gain_post, gain_pre, name, tb=256):
    T, D = xs.shape
    tb = _tile(T, tb)

    def body(y1_ref, xs_ref, gp_ref, gq_ref, x2_ref, h2_ref):
        x2 = xs_ref[...] + _rms(y1_ref[...])[1] * gp_ref[...]
        x2_ref[...] = x2
        h2_ref[...] = (_rms(x2)[1] * gq_ref[...]).astype(BF16)

    row = pl.BlockSpec((tb, D), lambda i: (i, 0))
    vec = pl.BlockSpec((1, D), lambda i: (0, 0))
    return pl.pallas_call(
        body, name=name, grid=(T // tb,), in_specs=[row, row, vec, vec], out_specs=[row, row],
        out_shape=[jax.ShapeDtypeStruct((T, D), F32), jax.ShapeDtypeStruct((T, D), BF16)],
        compiler_params=_params("parallel"),
    )(y1, xs, gain_post, gain_pre)


def _norm_pair_bwd(dh2, x2, gain_pre, dx3, y1, gain_post, name, tb=256):
    T, D = x2.shape
    tb = _tile(T, tb)

    def body(dh2_ref, x2_ref, gq_ref, dx3_ref, y1_ref, gp_ref, dx2_ref, dy1_ref, dgq_ref, dgp_ref):
        i = pl.program_id(0)
        r2, xn2 = _rms(x2_ref[...])
        d2, rows_q = _rms_grad(dh2_ref[...], xn2, r2, gq_ref[...])
        dx2 = dx3_ref[...] + d2
        dx2_ref[...] = dx2
        r1, yn1 = _rms(y1_ref[...])
        d1, rows_p = _rms_grad(dx2, yn1, r1, gp_ref[...])
        dy1_ref[...] = d1.astype(BF16)
        _accumulate_rows(i, dgq_ref, rows_q)
        _accumulate_rows(i, dgp_ref, rows_p)

    row = pl.BlockSpec((tb, D), lambda i: (i, 0))
    vec = pl.BlockSpec((1, D), lambda i: (0, 0))
    return pl.pallas_call(
        body, name=name, grid=(T // tb,), in_specs=[row, row, vec, row, row, vec], out_specs=[row, row, vec, vec],
        out_shape=[jax.ShapeDtypeStruct((T, D), F32), jax.ShapeDtypeStruct((T, D), BF16),
                   jax.ShapeDtypeStruct((1, D), F32), jax.ShapeDtypeStruct((1, D), F32)],
        compiler_params=_params("arbitrary"),
    )(dh2, x2, gain_pre, dx3, y1, gain_post)


def _loss_head(x2, y2, gain, target, name, tb=256):
    T, D = x2.shape
    tb = _tile(T, tb)

    def body(x2_ref, y2_ref, g_ref, t_ref, dx3_ref, dy2_ref, loss_ref, dg_ref):
        i = pl.program_id(0)
        r, yn = _rms(y2_ref[...])
        e = x2_ref[...] + yn * g_ref[...] - t_ref[...]
        dx3 = e * (1.0 / D)
        dx3_ref[...] = dx3
        dy2, rows = _rms_grad(dx3, yn, r, g_ref[...])
        dy2_ref[...] = dy2.astype(BF16)
        _accumulate_rows(i, dg_ref, rows)
        _accumulate_rows(i, loss_ref, 0.5 * jnp.mean(e * e, axis=-1, keepdims=True))

    row = pl.BlockSpec((tb, D), lambda i: (i, 0))
    vec = pl.BlockSpec((1, D), lambda i: (0, 0))
    return pl.pallas_call(
        body, name=name, grid=(T // tb,),
        in_specs=[row, row, vec, row],
        out_specs=[row, row, pl.BlockSpec((1, 1), lambda i: (0, 0)), vec],
        out_shape=[jax.ShapeDtypeStruct((T, D), F32), jax.ShapeDtypeStruct((T, D), BF16),
                   jax.ShapeDtypeStruct((1, 1), F32), jax.ShapeDtypeStruct((1, D), F32)],
        compiler_params=_params("arbitrary"),
    )(x2, y2, gain, target)


def _rope_tables(positions, rot_dim):
    half = rot_dim // 2
    inv_freq = ROPE_THETA ** (-jnp.arange(0, rot_dim, 2, dtype=F32) / rot_dim)
    ang = positions.astype(F32)[:, None] * inv_freq[None, :]
    cos, sin = jnp.cos(ang), jnp.sin(ang)
    T = positions.shape[0]
    ones = jnp.ones((T, HD - rot_dim), F32)
    cos_t = jnp.concatenate([cos, cos, ones], axis=1)
    sin_t = jnp.concatenate([-sin, sin, jnp.zeros_like(ones)], axis=1)
    return cos_t, sin_t


def _rotate(x, cos_t, sin_t, half):
    lane = lax.broadcasted_iota(jnp.int32, x.shape, 1)
    swapped = jnp.where(lane < half, pltpu.roll(x, HD - half, 1), pltpu.roll(x, half, 1))
    return x * cos_t + swapped * sin_t


def _rope_apply(x, cos_t, sin_t, half, n_blocks, is_rope, out_dtype, name, window=0, tb=256):
    T = x.shape[0]
    tb = _tile(T, tb)
    W = n_blocks * HD

    def body(x_ref, c_ref, s_ref, o_ref):
        for j in range(n_blocks):
            sl = slice(j * HD, (j + 1) * HD)
            xj = x_ref[:, sl]
            if is_rope(j):
                xj = _rotate(xj.astype(F32), c_ref[...], s_ref[...], half)
            o_ref[:, sl] = xj.astype(o_ref.dtype)

    tab = pl.BlockSpec((tb, HD), lambda i: (i, 0))
    return pl.pallas_call(
        body, name=name, grid=(T // tb,),
        in_specs=[pl.BlockSpec((tb, W), lambda i: (i, window)), tab, tab],
        out_specs=pl.BlockSpec((tb, W), lambda i: (i, 0)),
        out_shape=jax.ShapeDtypeStruct((T, W), out_dtype),
        compiler_params=_params("parallel"),
    )(x, cos_t, sin_t)


DSWA_TB = 2048


def _deinterleave(src, dst_ref, d, dtype):
    rows = src.shape[0] // d
    for r in range(d):
        dst_ref[r] = src[pl.ds(r, rows, stride=d), :].astype(dtype)


def _rope_dswa(proj, cos_t, sin_t, name, comm=None):
    T = proj.shape[0]
    tb = _tile(T, DSWA_TB)
    half = ROT_A // 2

    def body(x_ref, c_ref, s_ref, *rest):
        outs, scr = rest[:-1], rest[-1]
        j = pl.program_id(1)

        @pl.when(j < 2 * HEADS)
        def _():
            scr[...] = _rotate(x_ref[...], c_ref[...], s_ref[...], half)

        @pl.when(j >= 2 * HEADS)
        def _():
            scr[...] = x_ref[...]

        for o_ref, d in zip(outs, DSWA_DILATIONS):
            _deinterleave(scr, o_ref, d, BF16)

    blk = pl.BlockSpec((tb, HD), lambda i, j: (i, j))
    tab = pl.BlockSpec((tb, HD), lambda i, j: (i, 0))
    return _call(
        body, name, (T // tb, 3 * HEADS), [blk, tab, tab],
        [pl.BlockSpec((d, tb // d, HD), lambda i, j: (0, i, j)) for d in DSWA_DILATIONS],
        [jax.ShapeDtypeStruct((d, T // d, 3 * AW), BF16) for d in DSWA_DILATIONS], (proj, cos_t, sin_t),
        scratch=[pltpu.VMEM((tb, HD), F32)], sem=("parallel", "parallel"), comm=comm)


def _shared_key_grad(dk, cos_t, sin_t_neg, half, name, tb=512):
    T = dk.shape[0]
    tb = _tile(T, tb)

    def body(d_ref, c_ref, s_ref, o_ref):
        tot = d_ref[:, HD:2 * HD]
        for h in range(1, HEADS):
            tot = tot + d_ref[:, h * QK + HD:(h + 1) * QK]
        o_ref[...] = _rotate(tot, c_ref[...], s_ref[...], half).astype(o_ref.dtype)

    tab = pl.BlockSpec((tb, HD), lambda i: (i, 0))
    return pl.pallas_call(
        body, name=name, grid=(T // tb,),
        in_specs=[pl.BlockSpec((tb, HEADS * QK), lambda i: (i, 0)), tab, tab],
        out_specs=tab, out_shape=jax.ShapeDtypeStruct((T, HD), BF16),
        compiler_params=_params("parallel"),
    )(dk, cos_t, sin_t_neg)


def _band_mask(n):
    row = lax.broadcasted_iota(jnp.int32, (QBLK, 2 * QBLK), 0)
    col = lax.broadcasted_iota(jnp.int32, (QBLK, 2 * QBLK), 1)
    in_prev = jnp.logical_and(jnp.logical_and(col < QBLK, col >= row), n > 0)
    in_cur = jnp.logical_and(col >= QBLK, col - QBLK <= row)
    return jnp.logical_or(in_prev, in_cur)


def _dswa_specs(nb, reverse=False):
    pos = (lambda n: nb - 1 - n) if reverse else (lambda n: n)
    cur = lambda c: pl.BlockSpec((None, QBLK, AW), lambda r, n: (r, pos(n), c))
    prev = lambda c: pl.BlockSpec((None, QBLK, AW), lambda r, n: (r, jnp.maximum(pos(n) - 1, 0), c))
    stat = pl.BlockSpec((None, QBLK, HD), lambda r, n: (r, pos(n), 0))
    return cur, prev, stat


def _relayout_spec(d, tb, per_head=True):
    if per_head:
        return pl.BlockSpec((d, tb // d, HD), lambda i, h: (0, i, h))
    return pl.BlockSpec((d, tb // d, HD), lambda i, h: (0, i, 0))


def _head_lane(x, h):
    lane = lax.broadcasted_iota(jnp.int32, x.shape, 1)
    return jnp.sum(jnp.where(lane == h, x, 0.0), axis=-1, keepdims=True)


def _dswa_fwd(qkv, name, comm=None):
    d, sd = qkv.shape[:2]
    nb = sd // QBLK

    def body(q_ref, kc_ref, kp_ref, vc_ref, vp_ref, o_ref, l_ref):
        mask = _band_mask(pl.program_id(1))
        l_ref[...] = jnp.zeros_like(l_ref)
        for h in range(HEADS):
            sl = slice(h * HD, (h + 1) * HD)
            keys = jnp.concatenate([kp_ref[:, sl], kc_ref[:, sl]], axis=0)
            vals = jnp.concatenate([vp_ref[:, sl], vc_ref[:, sl]], axis=0)
            s = jnp.where(mask, _dot(q_ref[:, sl], keys, NT) * SCALE_A, NEG_INF)
            m = jnp.max(s, axis=-1, keepdims=True)
            p = jnp.exp(s - m)
            den = jnp.sum(p, axis=-1, keepdims=True)
            o_ref[:, sl] = _dot((p / den).astype(BF16), vals, NN)
            l_ref[:, h:h + 1] = m + jnp.log(den)

    cur, prev, stat = _dswa_specs(nb)
    return _call(
        body, name, (d, nb), [cur(0), cur(1), prev(1), cur(2), prev(2)], [cur(0), stat],
        [jax.ShapeDtypeStruct((d, sd, AW), F32), jax.ShapeDtypeStruct((d, sd, HD), F32)],
        (qkv, qkv, qkv, qkv, qkv), sem=("parallel", "parallel"), comm=comm)


def _dswa_merge(outs, lses, name, comm=None):
    nc = len(DSWA_DILATIONS)
    T = outs[0].shape[0] * outs[0].shape[1]
    tb = _tile(T, DSWA_TB)

    def body(*refs):
        o_refs, l_refs = refs[:nc], refs[nc:2 * nc]
        out_ref, outb_ref = refs[2 * nc:2 * nc + 2]
        lt_refs = refs[2 * nc + 2:3 * nc + 2]
        o_nat, l_nat, lt_nat = refs[3 * nc + 2:4 * nc + 2], refs[4 * nc + 2:5 * nc + 2], refs[-1]
        h = pl.program_id(1)
        for c, d in enumerate(DSWA_DILATIONS):
            for r in range(d):
                o_nat[c][pl.ds(r, tb // d, stride=d), :] = o_refs[c][r]
                l_nat[c][pl.ds(r, tb // d, stride=d), :] = l_refs[c][r]
        ls = [l[...] for l in l_nat]
        m = functools.reduce(jnp.maximum, ls)
        es = [jnp.exp(l - m) for l in ls]
        tot = functools.reduce(lambda a, b: a + b, es)
        acc = _head_lane(es[0] / tot, h) * o_nat[0][...]
        for c in range(1, nc):
            acc = acc + _head_lane(es[c] / tot, h) * o_nat[c][...]
        out_ref[...] = acc
        outb_ref[...] = acc.astype(BF16)
        lt_nat[...] = m + jnp.log(tot)
        for c, d in enumerate(DSWA_DILATIONS):
            _deinterleave(lt_nat, lt_refs[c], d, F32)

    nat = pl.BlockSpec((tb, HD), lambda i, h: (i, h))
    by_d = [_relayout_spec(d, tb) for d in DSWA_DILATIONS]
    stat_by_d = [_relayout_spec(d, tb, per_head=False) for d in DSWA_DILATIONS]
    res = _call(
        body, name, (T // tb, HEADS), by_d + stat_by_d, [nat, nat] + stat_by_d,
        [jax.ShapeDtypeStruct((T, AW), F32), jax.ShapeDtypeStruct((T, AW), BF16)]
        + [jax.ShapeDtypeStruct((d, T // d, HD), F32) for d in DSWA_DILATIONS], (*outs, *lses),
        scratch=[pltpu.VMEM((tb, HD), F32)] * (2 * nc + 1), sem=("parallel", "arbitrary"), comm=comm)
    return res[0], res[1], res[2:]


def _dswa_delta(dout, out, name):
    nc = len(DSWA_DILATIONS)
    T = out.shape[0]
    tb = _tile(T, DSWA_TB)

    def body(do_ref, o_ref, *rest):
        dl_refs, dob_refs, dl_nat = rest[:nc], rest[nc:2 * nc], rest[-1]
        h = pl.program_id(1)
        lane = lax.broadcasted_iota(jnp.int32, (tb, HD), 1)
        mine = jnp.where(lane == h, jnp.sum(do_ref[...] * o_ref[...], axis=-1, keepdims=True), 0.0)

        @pl.when(h == 0)
        def _():
            dl_nat[...] = mine

        @pl.when(h > 0)
        def _():
            dl_nat[...] += mine

        for c, d in enumerate(DSWA_DILATIONS):
            _deinterleave(dl_nat, dl_refs[c], d, F32)
            _deinterleave(do_ref, dob_refs[c], d, BF16)

    nat = pl.BlockSpec((tb, HD), lambda i, h: (i, h))
    by_d = [_relayout_spec(d, tb) for d in DSWA_DILATIONS]
    stat_by_d = [_relayout_spec(d, tb, per_head=False) for d in DSWA_DILATIONS]
    res = pl.pallas_call(
        body, name=name, grid=(T // tb, HEADS),
        in_specs=[nat, nat], out_specs=stat_by_d + by_d,
        out_shape=[jax.ShapeDtypeStruct((d, T // d, HD), F32) for d in DSWA_DILATIONS]
        + [jax.ShapeDtypeStruct((d, T // d, AW), BF16) for d in DSWA_DILATIONS],
        scratch_shapes=[pltpu.VMEM((tb, HD), F32)],
        compiler_params=_params("parallel", "arbitrary"),
    )(dout, out)
    return res[:nc], res[nc:]


def _delta_prep(dout, col_block, out, name, tb=256):
    T = out.shape[0]
    tb = _tile(T, tb)

    def body(do_ref, o_ref, delta_ref, dob_ref):
        delta_ref[...] = jnp.zeros_like(delta_ref)
        for h in range(HEADS):
            sl = slice(h * HD, (h + 1) * HD)
            doh = do_ref[:, sl]
            delta_ref[:, h:h + 1] = jnp.sum(doh * o_ref[:, sl], axis=-1, keepdims=True)
            dob_ref[:, sl] = doh.astype(BF16)

    row = pl.BlockSpec((tb, AW), lambda i: (i, 0))
    return pl.pallas_call(
        body, name=name, grid=(T // tb,),
        in_specs=[pl.BlockSpec((tb, AW), lambda i: (i, col_block)), row],
        out_specs=[pl.BlockSpec((tb, HD), lambda i: (i, 0)), row],
        out_shape=[jax.ShapeDtypeStruct((T, HD), F32), jax.ShapeDtypeStruct((T, AW), BF16)],
        compiler_params=_params("parallel"),
    )(dout, out)


def _dswa_bwd(qkv, dout_b, lse_tot, delta, name, comm=None):
    d, sd = qkv.shape[:2]
    nb = sd // QBLK

    def body(q_ref, kc_ref, kp_ref, vc_ref, vp_ref, do_ref, l_ref, dl_ref, dq_ref, dk_ref, dv_ref, carry_k, carry_v):
        mask = _band_mask(nb - 1 - pl.program_id(1))

        @pl.when(pl.program_id(1) == 0)
        def _():
            carry_k[...] = jnp.zeros_like(carry_k)
            carry_v[...] = jnp.zeros_like(carry_v)

        for h in range(HEADS):
            sl = slice(h * HD, (h + 1) * HD)
            qh, doh = q_ref[:, sl], do_ref[:, sl]
            keys = jnp.concatenate([kp_ref[:, sl], kc_ref[:, sl]], axis=0)
            vals = jnp.concatenate([vp_ref[:, sl], vc_ref[:, sl]], axis=0)
            s = jnp.where(mask, _dot(qh, keys, NT) * SCALE_A, NEG_INF)
            p = jnp.exp(s - l_ref[:, h:h + 1])
            ds = (p * (_dot(doh, vals, NT) - dl_ref[:, h:h + 1]) * SCALE_A).astype(BF16)
            dq_ref[:, sl] = _dot(ds, keys, NN)
            dk = _dot(ds, qh, TN)
            dv = _dot(p.astype(BF16), doh, TN)
            dk_ref[:, sl] = dk[QBLK:] + carry_k[:, sl]
            dv_ref[:, sl] = dv[QBLK:] + carry_v[:, sl]
            carry_k[:, sl] = dk[:QBLK]
            carry_v[:, sl] = dv[:QBLK]

    cur, prev, stat = _dswa_specs(nb, reverse=True)
    return _call(
        body, name, (d, nb), [cur(0), cur(1), prev(1), cur(2), prev(2), cur(0), stat, stat], [cur(0)] * 3,
        [jax.ShapeDtypeStruct((d, sd, AW), F32)] * 3, (qkv, qkv, qkv, qkv, qkv, dout_b, lse_tot, delta),
        scratch=[pltpu.VMEM((QBLK, AW), F32)] * 2, sem=("parallel", "arbitrary"), comm=comm)


def _dswa_combine(grads, cos_t, sin_t_neg, name):
    nc = len(DSWA_DILATIONS)
    T = grads[0][0].shape[0] * grads[0][0].shape[1]
    tb = _tile(T, DSWA_TB)
    half = ROT_A // 2

    def body(*refs):
        g_refs = refs[:3 * nc]
        c_ref, s_ref = refs[3 * nc:3 * nc + 2]
        outs, accs = refs[3 * nc + 2:3 * nc + 5], refs[3 * nc + 5:]
        for which in range(3):
            acc = accs[which]
            for c, d in enumerate(DSWA_DILATIONS):
                g = g_refs[3 * c + which]
                for r in range(d):
                    if c == 0:
                        acc[...] = g[r]
                    else:
                        acc[pl.ds(r, tb // d, stride=d), :] += g[r]
            val = acc[...]
            if which < 2:
                val = _rotate(val, c_ref[...], s_ref[...], half)
            outs[which][...] = val.astype(BF16)

    nat = pl.BlockSpec((tb, HD), lambda i, h: (i, h))
    tab = pl.BlockSpec((tb, HD), lambda i, h: (i, 0))
    in_specs, ins = [], []
    for d, g in zip(DSWA_DILATIONS, grads):
        in_specs += [_relayout_spec(d, tb)] * 3
        ins += list(g)
    return pl.pallas_call(
        body, name=name, grid=(T // tb, HEADS),
        in_specs=in_specs + [tab, tab], out_specs=[nat] * 3,
        out_shape=[jax.ShapeDtypeStruct((T, AW), BF16)] * 3,
        scratch_shapes=[pltpu.VMEM((tb, HD), F32)] * 3,
        compiler_params=_params("parallel", "parallel"),
    )(*ins, cos_t, sin_t_neg)


MLA_TQ = 512
QK = 2 * HD
LOG2E = 1.4426950408889634


def _triangle(nq, key_major):
    pairs = [(q, k) for q in range(nq) for k in range(q + 1)]
    if key_major:
        pairs.sort(key=lambda p: (p[1], p[0]))
    return (jnp.array([p[0] for p in pairs], jnp.int32), jnp.array([p[1] for p in pairs], jnp.int32))


def _mla_specs(tq):
    q_spec = pl.BlockSpec((tq, HEADS * QK), lambda t, qi, ki: (qi[t], 0))
    k_spec = pl.BlockSpec((tq, HEADS * QK), lambda t, qi, ki: (ki[t], 0))
    v_spec = pl.BlockSpec((tq, AW), lambda t, qi, ki: (ki[t], 1))
    qrow = pl.BlockSpec((tq, AW), lambda t, qi, ki: (qi[t], 0))
    krow = pl.BlockSpec((tq, AW), lambda t, qi, ki: (ki[t], 0))
    return q_spec, k_spec, v_spec, qrow, krow


def _mla_stat_spec(tq):
    return pl.BlockSpec((tq, HD), lambda t, qi, ki: (qi[t], 0))


def _mla_scores(q_ref, k_ref, h, qi, ki, tq):
    s = _dot(q_ref[:, h * QK:(h + 1) * QK], k_ref[:, h * QK:(h + 1) * QK], NT) * SCALE_B
    row = lax.broadcasted_iota(jnp.int32, s.shape, 0) + qi * tq
    col = lax.broadcasted_iota(jnp.int32, s.shape, 1) + ki * tq
    return jnp.where(col <= row, s, NEG_INF)


def _mla_fwd(q, k, v1, name, comm=None):
    T = q.shape[0]
    tq = _tile(T, MLA_TQ)
    tables = _triangle(T // tq, False)

    def body(qi_ref, ki_ref, q_ref, k_ref, v_ref, o_ref, ob_ref, l_ref, m_s, acc):
        t = pl.program_id(0)
        qi, ki = qi_ref[t], ki_ref[t]

        @pl.when(ki == 0)
        def _():
            m_s[...] = jnp.full_like(m_s, NEG_INF)
            acc[...] = jnp.zeros_like(acc)

        row = lax.broadcasted_iota(jnp.int32, (tq, tq), 0) + qi * tq
        col = lax.broadcasted_iota(jnp.int32, (tq, tq), 1) + ki * tq
        bias = jnp.where(col <= row, 0.0, NEG_INF)
        updates = []
        for h in range(HEADS):
            s = _dot(q_ref[:, h * QK:(h + 1) * QK], k_ref[:, h * QK:(h + 1) * QK], NT) + bias
            m_new = jnp.maximum(m_s[h], jnp.max(s, axis=-1, keepdims=True))
            p = jnp.exp2((s - m_new) * (SCALE_B * LOG2E)).astype(BF16)
            alpha = jnp.exp2((m_s[h] - m_new) * (SCALE_B * LOG2E))
            updates.append((m_new, alpha, _dot(p, v_ref[:, h * QK:(h + 1) * QK], NN)))
        for h, (m_new, alpha, pv) in enumerate(updates):
            acc[:, h * QK:(h + 1) * QK] = alpha * acc[:, h * QK:(h + 1) * QK] + pv
            m_s[h] = m_new

        @pl.when(ki == qi)
        def _():
            l_ref[...] = jnp.zeros_like(l_ref)
            for h in range(HEADS):
                sl = slice(h * HD, (h + 1) * HD)
                den = acc[:, h * QK + HD:h * QK + HD + 1]
                out = acc[:, h * QK:h * QK + HD] / den
                o_ref[:, sl] = out
                ob_ref[:, sl] = out.astype(BF16)
                l_ref[:, h:h + 1] = m_s[h] * SCALE_B + jnp.log(den)

    q_spec, k_spec, _, qrow, _ = _mla_specs(tq)
    return _call(
        body, name, (tables[0].shape[0],), [q_spec, k_spec, k_spec], [qrow, qrow, _mla_stat_spec(tq)],
        [jax.ShapeDtypeStruct((T, AW), F32), jax.ShapeDtypeStruct((T, AW), BF16), jax.ShapeDtypeStruct((T, HD), F32)],
        (q, k, v1),
        scratch=[pltpu.VMEM((HEADS, tq, 1), F32), pltpu.VMEM((tq, HEADS * QK), F32)],
        sem=("arbitrary",), comm=comm, prefetch=tables)


def _mla_ds(q_ref, k_ref, v_ref, do_ref, l_ref, dl_ref, h, qi, ki, tq):
    sl = slice(h * HD, (h + 1) * HD)
    p = jnp.exp(_mla_scores(q_ref, k_ref, h, qi, ki, tq) - l_ref[:, h:h + 1])
    ds = (p * (_dot(do_ref[:, sl], v_ref[:, sl], NT) - dl_ref[:, h:h + 1]) * SCALE_B).astype(BF16)
    return p, ds


def _mla_bwd_q(q, k, kv, dout_b, lse, delta, name, comm=None):
    T = q.shape[0]
    tq = _tile(T, MLA_TQ)
    tables = _triangle(T // tq, False)

    def body(qi_ref, ki_ref, q_ref, k_ref, v_ref, do_ref, l_ref, dl_ref, dq_ref):
        t = pl.program_id(0)
        qi, ki = qi_ref[t], ki_ref[t]

        @pl.when(ki == 0)
        def _():
            dq_ref[...] = jnp.zeros_like(dq_ref)

        for h in range(HEADS):
            _, ds = _mla_ds(q_ref, k_ref, v_ref, do_ref, l_ref, dl_ref, h, qi, ki, tq)
            dq_ref[:, h * QK:(h + 1) * QK] += _dot(ds, k_ref[:, h * QK:(h + 1) * QK], NN)

    q_spec, k_spec, v_spec, qrow, _ = _mla_specs(tq)
    return _call(
        body, name, (tables[0].shape[0],), [q_spec, k_spec, v_spec, qrow, _mla_stat_spec(tq), _mla_stat_spec(tq)], [q_spec],
        [jax.ShapeDtypeStruct((T, HEADS * QK), F32)], (q, k, kv, dout_b, lse, delta),
        sem=("arbitrary",), comm=comm, prefetch=tables)[0]


def _mla_bwd_kv(q, k, kv, dout_b, lse, delta, name, comm=None):
    T = q.shape[0]
    tq = _tile(T, MLA_TQ)
    nq = T // tq
    tables = _triangle(nq, True)

    def body(qi_ref, ki_ref, q_ref, k_ref, v_ref, do_ref, l_ref, dl_ref, dk_ref, dkv_ref, dv_acc):
        t = pl.program_id(0)
        qi, ki = qi_ref[t], ki_ref[t]

        @pl.when(qi == ki)
        def _():
            dk_ref[...] = jnp.zeros_like(dk_ref)
            dv_acc[...] = jnp.zeros_like(dv_acc)

        for h in range(HEADS):
            sl = slice(h * HD, (h + 1) * HD)
            p, ds = _mla_ds(q_ref, k_ref, v_ref, do_ref, l_ref, dl_ref, h, qi, ki, tq)
            dv_acc[:, sl] += _dot(p.astype(BF16), do_ref[:, sl], TN)
            dk_ref[:, h * QK:(h + 1) * QK] += _dot(ds, q_ref[:, h * QK:(h + 1) * QK], TN)

        @pl.when(qi == nq - 1)
        def _():
            for h in range(HEADS):
                dkv_ref[:, h * HD:(h + 1) * HD] = dk_ref[:, h * QK:h * QK + HD].astype(BF16)
                dkv_ref[:, AW + h * HD:AW + (h + 1) * HD] = dv_acc[:, h * HD:(h + 1) * HD].astype(BF16)

    q_spec, k_spec, v_spec, qrow, _ = _mla_specs(tq)
    return _call(
        body, name, (tables[0].shape[0],),
        [q_spec, k_spec, v_spec, qrow, _mla_stat_spec(tq), _mla_stat_spec(tq)],
        [k_spec, pl.BlockSpec((tq, 2 * AW), lambda t, qi, ki: (ki[t], 0))],
        [jax.ShapeDtypeStruct((T, HEADS * QK), F32), jax.ShapeDtypeStruct((T, 2 * AW), BF16)],
        (q, k, kv, dout_b, lse, delta), scratch=[pltpu.VMEM((tq, AW), F32)],
        sem=("arbitrary",), comm=comm, prefetch=tables)


def _pair_sum(by_device, from_sibling, name, tb=256):
    n_chip, R, C = from_sibling.shape
    tb = _tile(R, tb)
    core = jnp.reshape(lax.axis_index("c"), (1,)).astype(jnp.int32)

    def body(core_ref, mine_ref, theirs_ref, o_ref):
        o_ref[...] = (mine_ref[...].astype(F32) + theirs_ref[...].astype(F32)).astype(o_ref.dtype)

    blk = pl.BlockSpec((None, tb, C), lambda p, i, core_ref: (p, i, 0))
    return _call(
        body, name, (n_chip, R // tb),
        [pl.BlockSpec((None, tb, C), lambda p, i, core_ref: (2 * p + core_ref[0], i, 0)), blk], [blk],
        [jax.ShapeDtypeStruct((n_chip, R, C), BF16)], (by_device, from_sibling),
        sem=("parallel", "parallel"), prefetch=(core,))[0]


def _adamw(parts, w, m, v, name, tb=128, comm=None):
    R, C = w.shape
    n_parts = parts.shape[0]
    tb = _tile(R, tb)
    c1 = 1.0 - ADAM_B1
    c2 = 1.0 - ADAM_B2
    bc1 = 1.0 - ADAM_B1 ** ADAM_STEP
    bc2 = 1.0 - ADAM_B2 ** ADAM_STEP

    def body(p_ref, w_ref, m_ref, v_ref, g_ref, d_ref, nm_ref, nv_ref):
        g = p_ref[0].astype(F32)
        for j in range(1, n_parts):
            g = g + p_ref[j].astype(F32)
        nm = ADAM_B1 * m_ref[...] + c1 * g
        nv = ADAM_B2 * v_ref[...] + c2 * (g * g)
        g_ref[...] = g
        nm_ref[...] = nm
        nv_ref[...] = nv
        d_ref[...] = -ADAM_LR * ((nm / bc1) / (jnp.sqrt(nv / bc2) + ADAM_EPS) + ADAM_WD * w_ref[...])

    row = pl.BlockSpec((tb, C), lambda i: (i, 0))
    return _call(
        body, name, (R // tb,), [pl.BlockSpec((n_parts, tb, C), lambda i: (0, i, 0)), row, row, row], [row] * 4,
        [jax.ShapeDtypeStruct((R, C), F32)] * 4, (parts, w, m, v), sem=("parallel",), comm=comm)


def _cols_from_shards(g):
    return jnp.transpose(g, (1, 0, 2)).reshape(g.shape[1], N_DEV * g.shape[2])


def _cols_to_shards(w):
    return jnp.transpose(w.reshape(w.shape[0], N_DEV, w.shape[1] // N_DEV), (1, 0, 2))


def _split_heads(w, first):
    w3 = w.reshape(w.shape[0], HEADS, -1)
    return w3[:, :, :first].reshape(w.shape[0], -1), w3[:, :, first:].reshape(w.shape[0], -1)


def _join_heads(a, b):
    R = a.shape[0]
    return jnp.concatenate([a.reshape(R, HEADS, -1), b.reshape(R, HEADS, -1)], axis=2).reshape(R, -1)


def _pad_heads(w, width):
    w3 = w.reshape(w.shape[0], HEADS, -1)
    return jnp.pad(w3, ((0, 0), (0, 0), (0, width - w3.shape[2]))).reshape(w.shape[0], HEADS * width)


def _unpad_heads(w, k):
    return w.reshape(w.shape[0], HEADS, -1)[:, :, :k].reshape(w.shape[0], HEADS * k)


def kernel(x, positions, norm_attn_pre, norm_attn_post, w_in, q_latent_norm, kv_latent_norm, w_uq, w_ukv, w_out, norm_mlp_pre, norm_mlp_post, w_up, w_down, loss_target, m_norm_attn_pre, m_norm_attn_post, m_w_in, m_q_latent_norm, m_kv_latent_norm, m_w_uq, m_w_ukv, m_w_out, m_norm_mlp_pre, m_norm_mlp_post, m_w_up, m_w_down, v_norm_attn_pre, v_norm_attn_post, v_w_in, v_q_latent_norm, v_kv_latent_norm, v_w_uq, v_w_ukv, v_w_out, v_norm_mlp_pre, v_norm_mlp_post, v_w_up, v_w_down):
    xs = x[0]
    tgt = loss_target[0]
    pos = positions[0]
    T, D = xs.shape
    big = dict(w_in=(w_in, m_w_in, v_w_in), w_uq=(w_uq, m_w_uq, v_w_uq), w_ukv=(w_ukv, m_w_ukv, v_w_ukv),
               w_out=(w_out, m_w_out, v_w_out), w_up=(w_up, m_w_up, v_w_up), w_down=(w_down, m_w_down, v_w_down))
    big = {n: tuple(t[0] for t in ts) for n, ts in big.items()}
    big_names = ["w_in", "w_uq", "w_ukv", "w_out", "w_up", "w_down"]
    col_sharded = {"w_in", "w_uq", "w_ukv", "w_up"}

    wb = {n: big[n][0].astype(BF16) for n in big_names}

    def gathered(ex, i, n):
        g = ex.results[i]
        return _cols_from_shards(g) if n in col_sharded else g.reshape(-1, g.shape[2])

    def by_device(g, n):
        return _cols_to_shards(g) if n in col_sharded else g.reshape(N_DEV, g.shape[0] // N_DEV, g.shape[1])

    def scatter_of(g, n):
        return _Exchange([by_device(g, n)], "scatter")

    cos_a, sin_a = _rope_tables(pos, ROT_A)
    cos_b, sin_b = _rope_tables(pos, ROPE_MLA)

    ex_in = _Exchange([wb["w_in"]], "gather")
    h1 = _rms_fwd(xs, norm_attn_pre, BF16, "norm_attn_pre_fwd", comm=ex_in)
    Wi = jnp.pad(gathered(ex_in, 0, "w_in"), ((0, 0), (0, IN_PAD - IN_COLS)))
    ex_mid = _Exchange([wb["w_uq"], wb["w_ukv"], wb["w_out"]], "gather")
    proj = _mm(h1, Wi, "nn", [F32], "proj_in", tn=1408, comm=ex_mid)
    Wuq = _pad_heads(gathered(ex_mid, 0, "w_uq"), QK)
    Wukv = jnp.concatenate(_split_heads(gathered(ex_mid, 1, "w_ukv"), HD), axis=1)
    Wo = gathered(ex_mid, 2, "w_out")
    n_piece = wb["w_down"].shape[0] // 8
    ex_down = [_Exchange([wb["w_down"][i * n_piece:(i + 1) * n_piece]], "gather") for i in range(8)]
    qkv_by_d = _rope_dswa(proj, cos_a, sin_a, "rope_dswa", comm=ex_down[0])
    outs, lses = [], []
    for d, qkv, ex in zip(DSWA_DILATIONS, qkv_by_d, ex_down[1:4]):
        o, l = _dswa_fwd(qkv, f"dswa_fwd_d{d}", comm=ex)
        outs.append(o)
        lses.append(l)
    a_out, a_out_b, a_lse_by_d = _dswa_merge(outs, lses, "dswa_merge", comm=ex_down[4])

    cqn = _rms_fwd(proj, q_latent_norm, BF16, "q_latent_norm_fwd", width=Q_LORA, col_block=3 * AW // Q_LORA)
    ckvn = _rms_fwd(proj, kv_latent_norm, BF16, "kv_latent_norm_fwd", width=KV_LORA, col_block=3 * AW // KV_LORA + 1)
    qb = _mm(cqn, Wuq, "nn", [F32], "q_up")
    kvb = _mm(ckvn, Wukv, "nn", [BF16], "kv_up")
    odd = lambda j: j % 2 == 1
    q_mla = _rope_apply(qb, cos_b, sin_b, ROPE_MLA // 2, 2 * HEADS, odd, BF16, "rope_mla_q")
    kr = _rope_apply(proj, cos_b, sin_b, ROPE_MLA // 2, 1, lambda j: True, BF16, "rope_mla_k",
                     window=(IN_PAD - HD) // HD)
    k_mla = jnp.concatenate([kvb[:, :AW].reshape(T, HEADS, HD), jnp.broadcast_to(kr[:, None, :], (T, HEADS, HD))],
                            axis=2).reshape(T, HEADS * QK)
    v1_mla = jnp.concatenate([kvb[:, AW:].reshape(T, HEADS, HD), jnp.ones((T, HEADS, 1), BF16),
                              jnp.zeros((T, HEADS, HD - 1), BF16)], axis=2).reshape(T, HEADS * QK)
    ex_up = _Exchange([wb["w_up"]], "gather")
    b_out, b_out_b, b_lse = _mla_fwd(q_mla, k_mla, v1_mla, "mla_fwd", comm=ex_up)
    Wup_shards = ex_up.results[0]

    mixed = jnp.concatenate([a_out_b, b_out_b], axis=1)
    y1 = _mm(mixed, Wo, "nn", [F32], "attn_out", comm=ex_down[5])

    x2, h2 = _norm_pair_fwd(y1, xs, norm_attn_post, norm_mlp_pre, "norm_attn_post_mlp_pre_fwd")

    def relu2(z):
        r = jnp.maximum(z, 0.0)
        return r * r, r

    u, zr = _mm(h2, Wup_shards, "nn", [BF16, BF16], "mlp_up", epilogue=relu2, comm=ex_down[6:8],
                b_shards=True)
    Wdn = jnp.concatenate([ex.results[0] for ex in ex_down], axis=1).reshape(-1, D)
    y2 = _mm(u, Wdn, "nn", [F32], "mlp_down")
    dx3, dy2, loss_part, dg_mlp_post = _loss_head(x2, y2, norm_mlp_post, tgt, "loss_head")

    dz =_mm(dy2, Wdn, "nt", [BF16], "mlp_down_dx", epilogue=lambda du, r: (du * (2.0 * r.astype(F32)),), extras=(zr,))
    g_down = _mm(u, dy2, "tn", [BF16], "mlp_down_dw")
    down_dev = by_device(g_down, "w_down")
    pair_down = _Exchange([down_dev], "pair")
    up_dev = _mm(h2, dz, "tn", [BF16], "mlp_up_dw", comm=pair_down, out_shards=True)
    down_chip = _pair_sum(down_dev, pair_down.results[0], "pair_sum_w_down")
    pair_up = _Exchange([up_dev], "pair")
    cut = 5 * down_chip.shape[1] // 8
    sc_down = [_Exchange([down_chip[:, :cut]], "chips"), _Exchange([down_chip[:, cut:]], "chips")]
    dh2 = _mm(dz, Wup_shards, "nt", [F32], "mlp_up_dx", comm=[pair_up, sc_down[0]], b_shards=True)
    up_chip = _pair_sum(up_dev, pair_up.results[0], "pair_sum_w_up")

    dx2, dy1, dg_mlp_pre, dg_attn_post = _norm_pair_bwd(dh2, x2, norm_mlp_pre, dx3, y1, norm_attn_post,
                                                        "norm_mlp_pre_attn_post_bwd")
    dmixed = _mm(dy1, Wo, "nt", [F32], "attn_out_dx")
    g_out = _mm(mixed, dy1, "tn", [BF16], "attn_out_dw")

    b_delta, b_dout = _delta_prep(dmixed, 1, b_out, "mla_delta")
    sc_up = _Exchange([up_chip], "chips")
    dq_mla = _mla_bwd_q(q_mla, k_mla, kvb, b_dout, b_lse, b_delta, "mla_bwd_q", comm=sc_up)
    sc_out = scatter_of(g_out, "w_out")
    dk_mla, dkvb = _mla_bwd_kv(q_mla, k_mla, kvb, b_dout, b_lse, b_delta, "mla_bwd_kv", comm=[sc_out, sc_down[1]])
    dqb = _rope_apply(dq_mla, cos_b, -sin_b, ROPE_MLA // 2, 2 * HEADS, odd, BF16, "rope_mla_q_bwd")
    d_kr = _shared_key_grad(dk_mla, cos_b, -sin_b, ROPE_MLA // 2, "rope_mla_k_bwd")
    g_uq_pad = _mm(cqn, dqb, "tn", [BF16], "q_up_dw")
    g_ukv_perm = _mm(ckvn, dkvb, "tn", [BF16], "kv_up_dw")
    dcqn = _mm(dqb, Wuq, "nt", [F32], "q_up_dx")
    dckvn = _mm(dkvb, Wukv, "nt", [F32], "kv_up_dx")
    d_cq, dg_q = _rms_bwd(dcqn, proj, q_latent_norm, BF16, "q_latent_norm_bwd", width=Q_LORA, col_block=3 * AW // Q_LORA)
    d_ckv, dg_kv = _rms_bwd(dckvn, proj, kv_latent_norm, BF16, "kv_latent_norm_bwd", width=KV_LORA,
                            col_block=3 * AW // KV_LORA + 1)

    g_uq = _unpad_heads(g_uq_pad, HD + ROPE_MLA)
    g_ukv = _join_heads(g_ukv_perm[:, :AW], g_ukv_perm[:, AW:])
    sc_uq = _Exchange([_cols_to_shards(g_uq), _cols_to_shards(g_ukv)], "scatter")
    a_delta_by_d, a_dout_by_d = _dswa_delta(dmixed, a_out, "dswa_delta")
    a_grads = [_dswa_bwd(qkv_by_d[c], a_dout_by_d[c], a_lse_by_d[c], a_delta_by_d[c], f"dswa_bwd_d{d}")
               for c, d in enumerate(DSWA_DILATIONS)]
    d_aq, d_ak, d_av = _dswa_combine(a_grads, cos_a, -sin_a, "dswa_combine")

    dproj = jnp.concatenate([d_aq, d_ak, d_av, d_cq, d_ckv, d_kr], axis=1)
    g_in_pad = _mm(h1, dproj, "tn", [BF16], "proj_in_dw", tn=1408, comm=sc_uq)
    in_dev = by_device(g_in_pad[:, :IN_COLS], "w_in")
    pair_in = _Exchange([in_dev], "pair").standalone("pair_w_in")
    sc_in = _Exchange([_pair_sum(in_dev, pair_in[0], "pair_sum_w_in")], "chips")
    dh1 = _mm(dproj, Wi, "nt", [F32], "proj_in_dx", comm=sc_in)
    grad_x, dg_attn_pre = _rms_bwd(dh1, xs, norm_attn_pre, F32, "norm_attn_pre_bwd", residual=dx2)

    parts = dict(w_in=sc_in.results[0], w_uq=sc_uq.results[0], w_ukv=sc_uq.results[1], w_out=sc_out.results[0],
                 w_up=sc_up.results[0], w_down=jnp.concatenate([sc.results[0] for sc in sc_down], axis=1))
    big_out = {n: _adamw(parts[n], *big[n], f"adamw_{n}") for n in big_names}

    gain_names = ["norm_attn_pre", "norm_attn_post", "q_latent_norm", "kv_latent_norm", "norm_mlp_pre", "norm_mlp_post"]
    gain_args = dict(norm_attn_pre=(norm_attn_pre, m_norm_attn_pre, v_norm_attn_pre),
                     norm_attn_post=(norm_attn_post, m_norm_attn_post, v_norm_attn_post),
                     q_latent_norm=(q_latent_norm, m_q_latent_norm, v_q_latent_norm),
                     kv_latent_norm=(kv_latent_norm, m_kv_latent_norm, v_kv_latent_norm),
                     norm_mlp_pre=(norm_mlp_pre, m_norm_mlp_pre, v_norm_mlp_pre),
                     norm_mlp_post=(norm_mlp_post, m_norm_mlp_post, v_norm_mlp_post))
    gain_grads = dict(norm_attn_pre=dg_attn_pre, norm_attn_post=dg_attn_post, q_latent_norm=dg_q,
                      kv_latent_norm=dg_kv, norm_mlp_pre=dg_mlp_pre, norm_mlp_post=dg_mlp_post)
    packed = jnp.concatenate([gain_grads[n] for n in gain_names], axis=1)
    gain_parts = _Exchange([packed], "gather").standalone("gather_gain_grads")[0]
    pack3 = lambda i: jnp.concatenate([gain_args[n][i] for n in gain_names], axis=1)
    gain_out = _adamw(gain_parts, pack3(0), pack3(1), pack3(2), "adamw_gains", tb=1)
    offs = [0]
    for n in gain_names:
        offs.append(offs[-1] + gain_args[n][0].shape[1])
    small_out = {n: tuple(o[:, offs[i]:offs[i + 1]] for o in gain_out) for i, n in enumerate(gain_names)}

    loss = lax.psum(loss_part[0, 0], ("x", "y", "c"))

    order = ["norm_attn_pre", "norm_attn_post", "w_in", "q_latent_norm", "kv_latent_norm", "w_uq", "w_ukv", "w_out",
             "norm_mlp_pre", "norm_mlp_post", "w_up", "w_down"]
    res = {n: (small_out[n] if n in small_out else tuple(o[None] for o in big_out[n])) for n in order}
    return (loss, grad_x[None], *[res[n][0] for n in order], *[res[n][1] for n in order],
            *[res[n][2] for n in order], *[res[n][3] for n in order])
```

```python
import functools
import math

import jax
import jax.numpy as jnp
from jax import lax
from jax.experimental import pallas as pl
from jax.experimental.pallas import tpu as pltpu

F32 = jnp.float32
BF16 = jnp.bfloat16

N_DEV = 8
HEADS = 8
HD = 128
AW = HEADS * HD
Q_LORA = 512
KV_LORA = 512
ROPE_MLA = 64
ROT_A = 32
IN_COLS = 3 * AW + Q_LORA + KV_LORA + ROPE_MLA
IN_PAD = 3 * AW + Q_LORA + KV_LORA + HD
QBLK = 128
DSWA_DILATIONS = (1, 4, 16)
ROPE_THETA = 500000.0
NORM_EPS = 1e-6
NEG_INF = -1e30
SCALE_A = HD ** -0.5
SCALE_B = (HD + ROPE_MLA) ** -0.5

ADAM_LR = 0.001
ADAM_B1 = 0.9
ADAM_B2 = 0.999
ADAM_EPS = 1e-08
ADAM_WD = 0.01
ADAM_STEP = 10

VMEM_LIMIT = 48 * 1024 * 1024

NT = (((1,), (1,)), ((), ()))
NN = (((1,), (0,)), ((), ()))
TN = (((0,), (0,)), ((), ()))


def _dot(a, b, dims):
    return lax.dot_general(a, b, dims, preferred_element_type=F32)


def _params(*sem):
    return pltpu.CompilerParams(dimension_semantics=sem, vmem_limit_bytes=VMEM_LIMIT)


def _tile(n, want):
    t = min(n, want)
    while n % t:
        t //= 2
    return t


def _tile128(n, want):
    if n % 128:
        return n
    units = n // 128
    return 128 * max(u for u in range(1, max(want // 128, 1) + 1) if units % u == 0)


class _Exchange:
    def __init__(self, arrs, mode):
        self.arrs = list(arrs)
        self.mode = mode
        self.n = len(self.arrs)
        self.results = None
        hbm = pl.BlockSpec(memory_space=pltpu.HBM)
        self.specs = [hbm] * self.n
        shape = {"gather": lambda a: (N_DEV,) + a.shape, "scatter": lambda a: a.shape,
                 "pair": lambda a: (4,) + a.shape[1:], "chips": lambda a: a.shape}[mode]
        self.out_shape = [jax.ShapeDtypeStruct(shape(a), a.dtype) for a in self.arrs]
        n_sem = self.n * (N_DEV - 1)
        self.scratch = [pltpu.SemaphoreType.DMA((n_sem,)), pltpu.SemaphoreType.DMA((n_sem,)),
                        pltpu.SemaphoreType.DMA((self.n,))]

    def hooks(self, ins, outs, send_sems, recv_sems, local_sems):
        x, y, c = lax.axis_index("x"), lax.axis_index("y"), lax.axis_index("c")
        me = (x, y, c)
        sib = (x, y, 1 - c)
        chips = [(1 - x, y), (x, 1 - y), (1 - x, 1 - y)]
        slot = lambda p: 4 * p[0] + 2 * p[1] + p[2]
        chip_of = lambda p: 2 * p[0] + p[1]

        def rcopy(a, k, src, dst, to):
            i = a * (N_DEV - 1) + k
            return pltpu.make_async_remote_copy(src_ref=src, dst_ref=dst, send_sem=send_sems.at[i],
                                                recv_sem=recv_sems.at[i], device_id=to,
                                                device_id_type=pl.DeviceIdType.MESH)

        def local(a):
            if self.mode == "chips":
                return pltpu.make_async_copy(ins[a].at[chip_of(me)], outs[a].at[chip_of(me)], local_sems.at[a])
            src = ins[a].at[slot(me)] if self.mode == "scatter" else ins[a]
            return pltpu.make_async_copy(src, outs[a].at[slot(me)], local_sems.at[a])

        def peer(rel):
            return (1 - x if rel & 4 else x, 1 - y if rel & 2 else y, 1 - c if rel & 1 else c)

        if self.mode == "pair":
            def start():
                for a in range(self.n):
                    for p in range(4):
                        rcopy(a, p, ins[a].at[2 * p + 1 - c], outs[a].at[p], sib).start()

            def middle():
                pass

            def finish():
                for a in range(self.n):
                    for p in range(4):
                        cp = rcopy(a, p, ins[a].at[2 * p + 1 - c], outs[a].at[p], sib)
                        cp.wait_send()
                        cp.wait_recv()
        elif self.mode == "chips":
            def start():
                for a in range(self.n):
                    local(a).start()
                    for j, chip in enumerate(chips):
                        rcopy(a, j, ins[a].at[chip_of(chip)], outs[a].at[chip_of(me)], (*chip, c)).start()

            def middle():
                pass

            def finish():
                for a in range(self.n):
                    for j, chip in enumerate(chips):
                        cp = rcopy(a, j, ins[a].at[chip_of(chip)], outs[a].at[chip_of(chip)], (*chip, c))
                        cp.wait_send()
                        cp.wait_recv()
                    local(a).wait()
        elif self.mode == "scatter":
            def start():
                for a in range(self.n):
                    local(a).start()
                    for rel in range(1, N_DEV):
                        rcopy(a, rel - 1, ins[a].at[slot(peer(rel))], outs[a].at[slot(me)], peer(rel)).start()

            def middle():
                pass

            def finish():
                for a in range(self.n):
                    for rel in range(1, N_DEV):
                        cp = rcopy(a, rel - 1, ins[a].at[slot(peer(rel))], outs[a].at[slot(peer(rel))], peer(rel))
                        cp.wait_send()
                        cp.wait_recv()
                    local(a).wait()
        else:
            def start():
                for a in range(self.n):
                    local(a).start()
                    rcopy(a, 0, ins[a], outs[a].at[slot(me)], sib).start()
                    for j, chip in enumerate(chips):
                        rcopy(a, 1 + j, ins[a], outs[a].at[slot(me)], (*chip, c)).start()

            def middle():
                for a in range(self.n):
                    for j, chip in enumerate(chips):
                        landed = outs[a].at[slot((*chip, c))]
                        rcopy(a, 1 + j, ins[a], landed, me).wait_recv()
                        rcopy(a, 4 + j, landed, landed, sib).start()

            def finish():
                for a in range(self.n):
                    rcopy(a, 0, ins[a], outs[a].at[slot(sib)], me).wait_recv()
                    for j, chip in enumerate(chips):
                        rcopy(a, 4 + j, ins[a], outs[a].at[slot((*chip, 1 - c))], me).wait_recv()
                    for k in range(N_DEV - 1):
                        rcopy(a, k, ins[a], outs[a].at[slot(me)], me).wait_send()
                    local(a).wait()

        return start, middle, finish

    def set_results(self, res):
        self.results = list(res)

    def standalone(self, name):
        n = self.n

        def body(*refs):
            start, middle, finish = self.hooks(refs[:n], refs[n:2 * n], *refs[2 * n:])
            start()
            middle()
            finish()

        self.results = pl.pallas_call(
            body, name=name, in_specs=self.specs, out_specs=self.specs, out_shape=self.out_shape,
            scratch_shapes=self.scratch, compiler_params=pltpu.CompilerParams(has_side_effects=True),
        )(*self.arrs)
        return self.results


class _Carried:
    def __init__(self, parts):
        self.parts = list(parts)
        self.n = sum(p.n for p in self.parts)
        self.arrs = [a for p in self.parts for a in p.arrs]
        self.specs = [s for p in self.parts for s in p.specs]
        self.out_shape = [s for p in self.parts for s in p.out_shape]
        self.scratch = [s for p in self.parts for s in p.scratch]

    def hooks(self, ins, outs, *sems):
        hooks, i = [], 0
        for j, p in enumerate(self.parts):
            hooks.append(p.hooks(ins[i:i + p.n], outs[i:i + p.n], *sems[3 * j:3 * j + 3]))
            i += p.n
        def phase(k):
            def run():
                for h in hooks:
                    h[k]()
            return run

        return phase(0), phase(1), phase(2)

    def set_results(self, res):
        i = 0
        for p in self.parts:
            p.set_results(res[i:i + p.n])
            i += p.n


def _call(body, name, grid, in_specs, out_specs, out_shape, args, scratch=(), sem=(), comm=None, prefetch=(),
          aliases=None):
    npf = len(prefetch)
    if isinstance(comm, (list, tuple)):
        comm = _Carried(comm)
    if comm is None:
        spec = pltpu.PrefetchScalarGridSpec(num_scalar_prefetch=npf, grid=grid, in_specs=list(in_specs),
                                            out_specs=list(out_specs), scratch_shapes=list(scratch))
        return pl.pallas_call(body, name=name, grid_spec=spec, out_shape=list(out_shape),
                              input_output_aliases=aliases or {},
                              compiler_params=_params(*sem))(*prefetch, *args)
    assert aliases is None
    ni, no, ns, n = len(in_specs), len(out_specs), len(scratch), comm.n
    steps = math.prod(grid)

    def wrapped(*refs):
        pf, refs = refs[:npf], refs[npf:]
        ins, c_ins = refs[:ni], refs[ni:ni + n]
        outs, c_outs = refs[ni + n:ni + n + no], refs[ni + n + no:ni + 2 * n + no]
        scr, c_scr = refs[ni + 2 * n + no:ni + 2 * n + no + ns], refs[ni + 2 * n + no + ns:]
        start, middle, finish = comm.hooks(c_ins, c_outs, *c_scr)
        step = pl.program_id(0)
        for ax in range(1, len(grid)):
            step = step * grid[ax] + pl.program_id(ax)
        pl.when(step == 0)(start)
        pl.when(step == steps // 2)(middle)
        body(*pf, *ins, *outs, *scr)
        pl.when(step == steps - 1)(finish)

    spec = pltpu.PrefetchScalarGridSpec(num_scalar_prefetch=npf, grid=grid, in_specs=list(in_specs) + comm.specs,
                                        out_specs=list(out_specs) + comm.specs,
                                        scratch_shapes=list(scratch) + comm.scratch)
    res = pl.pallas_call(
        wrapped, name=name, grid_spec=spec, out_shape=list(out_shape) + comm.out_shape,
        compiler_params=pltpu.CompilerParams(dimension_semantics=("arbitrary",) * len(grid),
                                             vmem_limit_bytes=VMEM_LIMIT, has_side_effects=True),
    )(*prefetch, *args, *comm.arrs)
    comm.set_results(res[no:])
    return res[:no]


def _mm(a, b, mode, out_dtypes, name, epilogue=None, extras=(), tm=1024, tn=1024, tk=2048, comm=None,
        b_shards=False, out_shards=False):
    if mode == "tn":
        K, M = a.shape
    else:
        M, K = a.shape
    if b_shards:
        N = b.shape[1] if mode == "nt" else N_DEV * b.shape[2]
    else:
        N = b.shape[0] if mode == "nt" else b.shape[1]
    tm, tn, tk = _tile128(M, tm), _tile128(N, tn), _tile128(K, tk)
    pair_k = b_shards and mode == "nt"
    if pair_k:
        tk = 2 * K // N_DEV
        b = b.reshape(N_DEV // 2, 2, *b.shape[1:])
    elif b_shards or out_shards:
        tn = N // N_DEV
    nk = K // tk
    dims = {"nn": NN, "nt": NT, "tn": TN}[mode]
    a_spec = (pl.BlockSpec((tk, tm), lambda i, j, k: (k, i)) if mode == "tn"
              else pl.BlockSpec((tm, tk), lambda i, j, k: (i, k)))
    if b_shards:
        b_spec = (pl.BlockSpec((None, 2, tn, tk // 2), lambda i, j, k: (k, 0, j, 0)) if mode == "nt"
                  else pl.BlockSpec((None, tk, tn), lambda i, j, k: (j, k, 0)))
    else:
        b_spec = (pl.BlockSpec((tn, tk), lambda i, j, k: (j, k)) if mode == "nt"
                  else pl.BlockSpec((tk, tn), lambda i, j, k: (k, j)))
    mn_spec = pl.BlockSpec((tm, tn), lambda i, j, k: (i, j))
    out_spec = pl.BlockSpec((None, tm, tn), lambda i, j, k: (j, i, 0)) if out_shards else mn_spec
    out_dims = (N_DEV, M, N // N_DEV) if out_shards else (M, N)
    n_ex = len(extras)
    n_out = len(out_dtypes)

    def finish(acc, ex, outs):
        res = (acc,) if epilogue is None else epilogue(acc, *[e[...] for e in ex])
        for o, r in zip(outs, res):
            o[...] = r.astype(o.dtype)

    def product(a_ref, b_ref):
        if pair_k:
            return _dot(a_ref[:, :tk // 2], b_ref[0], NT) + _dot(a_ref[:, tk // 2:], b_ref[1], NT)
        return _dot(a_ref[...], b_ref[...], dims)

    def body(*refs):
        a_ref, b_ref = refs[:2]
        ex = refs[2:2 + n_ex]
        outs = refs[2 + n_ex:2 + n_ex + n_out]
        if nk == 1:
            finish(product(a_ref, b_ref), ex, outs)
            return
        acc = refs[-1]
        k = pl.program_id(2)

        @pl.when(k == 0)
        def _():
            acc[...] = product(a_ref, b_ref)

        @pl.when(jnp.logical_and(k > 0, k < nk - 1))
        def _():
            acc[...] += product(a_ref, b_ref)

        @pl.when(k == nk - 1)
        def _():
            finish(acc[...] + product(a_ref, b_ref), ex, outs)

    out = _call(
        body, name, (M // tm, N // tn, nk), [a_spec, b_spec] + [mn_spec] * n_ex, [out_spec] * n_out,
        [jax.ShapeDtypeStruct(out_dims, dt) for dt in out_dtypes], (a, b, *extras),
        scratch=[] if nk == 1 else [pltpu.VMEM((tm, tn), F32)], sem=("parallel", "parallel", "arbitrary"),
        comm=comm)
    return out[0] if n_out == 1 else out


def _rms_fwd(x, gain, out_dtype, name, width=None, col_block=0, residual=None, tb=256, comm=None):
    T = x.shape[0]
    W = x.shape[1] if width is None else width
    tb = _tile(T, tb)
    has_res = residual is not None

    def body(*refs):
        x_ref, g_ref = refs[:2]
        o_ref = refs[-1]
        xf = x_ref[...]
        y = xf * lax.rsqrt(jnp.mean(xf * xf, axis=-1, keepdims=True) + NORM_EPS) * g_ref[...]
        if has_res:
            y = refs[2][...] + y
        o_ref[...] = y.astype(o_ref.dtype)

    row = pl.BlockSpec((tb, W), lambda i: (i, 0))
    ins = [x, gain] + ([residual] if has_res else [])
    return _call(
        body, name, (T // tb,),
        [pl.BlockSpec((tb, W), lambda i: (i, col_block)),
         pl.BlockSpec((1, W), lambda i: (0, 0))] + ([row] if has_res else []),
        [row], [jax.ShapeDtypeStruct((T, W), out_dtype)], ins, sem=("parallel",), comm=comm)[0]


def _rms_bwd(dy, x, gain, out_dtype, name, width=None, col_block=0, residual=None, tb=256, comm=None, into=None):
    T = dy.shape[0]
    W = x.shape[1] if width is None else width
    tb = _tile(T, tb)
    has_res = residual is not None

    def body(*refs):
        dy_ref, x_ref, g_ref = refs[:3]
        dx_ref, dg_ref = refs[-2:]
        i = pl.program_id(0)
        xf = x_ref[...]
        r = lax.rsqrt(jnp.mean(xf * xf, axis=-1, keepdims=True) + NORM_EPS)
        xn = xf * r
        dyf = dy_ref[...].astype(F32)
        dyg = dyf * g_ref[...]
        dx = r * (dyg - xn * jnp.mean(dyg * xn, axis=-1, keepdims=True))
        if has_res:
            dx = refs[3][...] + dx
        dx_ref[...] = dx.astype(dx_ref.dtype)

        @pl.when(i == 0)
        def _():
            dg_ref[...] = jnp.zeros_like(dg_ref)

        dg_ref[...] += jnp.sum(dyf * xn, axis=0, keepdims=True)

    row = pl.BlockSpec((tb, W), lambda i: (i, 0))
    vec = pl.BlockSpec((1, W), lambda i: (0, 0))
    ins = [dy, x, gain] + ([residual] if has_res else [])
    in_specs = [row, pl.BlockSpec((tb, W), lambda i: (i, col_block)), vec] + ([row] if has_res else [])
    if into is None:
        out_spec, out_struct, aliases = row, jax.ShapeDtypeStruct((T, W), out_dtype), None
    else:
        buf, buf_block = into
        out_spec = pl.BlockSpec((tb, W), lambda i: (i, buf_block))
        out_struct, aliases = jax.ShapeDtypeStruct(buf.shape, buf.dtype), {len(ins): 0}
        ins, in_specs = ins + [buf], in_specs + [pl.BlockSpec(memory_space=pl.ANY)]
    return _call(
        body, name, (T // tb,), in_specs, [out_spec, vec], [out_struct, jax.ShapeDtypeStruct((1, W), F32)], ins,
        sem=("arbitrary",), comm=comm, aliases=aliases)


def _rms(xf):
    r = lax.rsqrt(jnp.mean(xf * xf, axis=-1, keepdims=True) + NORM_EPS)
    return r, xf * r


def _rms_grad(dyf, xn, r, gain):
    dyg = dyf * gain
    return r * (dyg - xn * jnp.mean(dyg * xn, axis=-1, keepdims=True)), dyf * xn


def _accumulate_rows(i, ref, rows):
    @pl.when(i == 0)
    def _():
        ref[...] = jnp.zeros_like(ref)

    ref[...] += jnp.sum(rows, axis=0, keepdims=True)


def _norm_pair_fwd(y1, xs, gain_post, gain_pre, name, tb=256):
    T, D = xs.shape
    tb = _tile(T, tb)

    def body(y1_ref, xs_ref, gp_ref, gq_ref, x2_ref, h2_ref):
        x2 = xs_ref[...] + _rms(y1_ref[...])[1] * gp_ref[...]
        x2_ref[...] = x2
        h2_ref[...] = (_rms(x2)[1] * gq_ref[...]).astype(BF16)

    row = pl.BlockSpec((tb, D), lambda i: (i, 0))
    vec = pl.BlockSpec((1, D), lambda i: (0, 0))
    return pl.pallas_call(
        body, name=name, grid=(T // tb,), in_specs=[row, row, vec, vec], out_specs=[row, row],
        out_shape=[jax.ShapeDtypeStruct((T, D), F32), jax.ShapeDtypeStruct((T, D), BF16)],
        compiler_params=_params("parallel"),
    )(y1, xs, gain_post, gain_pre)


def _norm_pair_bwd(dh2, x2, gain_pre, dx3, y1, gain_post, name, tb=256):
    T, D = x2.shape
    tb = _tile(T, tb)

    def body(dh2_ref, x2_ref, gq_ref, dx3_ref, y1_ref, gp_ref, dx2_ref, dy1_ref, dgq_ref, dgp_ref):
        i = pl.program_id(0)
        r2, xn2 = _rms(x2_ref[...])
        d2, rows_q = _rms_grad(dh2_ref[...], xn2, r2, gq_ref[...])
        dx2 = dx3_ref[...] + d2
        dx2_ref[...] = dx2
        r1, yn1 = _rms(y1_ref[...])
        d1, rows_p = _rms_grad(dx2, yn1, r1, gp_ref[...])
        dy1_ref[...] = d1.astype(BF16)
        _accumulate_rows(i, dgq_ref, rows_q)
        _accumulate_rows(i, dgp_ref, rows_p)

    row = pl.BlockSpec((tb, D), lambda i: (i, 0))
    vec = pl.BlockSpec((1, D), lambda i: (0, 0))
    return pl.pallas_call(
        body, name=name, grid=(T // tb,), in_specs=[row, row, vec, row, row, vec], out_specs=[row, row, vec, vec],
        out_shape=[jax.ShapeDtypeStruct((T, D), F32), jax.ShapeDtypeStruct((T, D), BF16),
                   jax.ShapeDtypeStruct((1, D), F32), jax.ShapeDtypeStruct((1, D), F32)],
        compiler_params=_params("arbitrary"),
    )(dh2, x2, gain_pre, dx3, y1, gain_post)


def _loss_head(x2, y2, gain, target, name, tb=256):
    T, D = x2.shape
    tb = _tile(T, tb)

    def body(x2_ref, y2_ref, g_ref, t_ref, dx3_ref, dy2_ref, loss_ref, dg_ref):
        i = pl.program_id(0)
        r, yn = _rms(y2_ref[...])
        e = x2_ref[...] + yn * g_ref[...] - t_ref[...]
        dx3 = e * (1.0 / D)
        dx3_ref[...] = dx3
        dy2, rows = _rms_grad(dx3, yn, r, g_ref[...])
        dy2_ref[...] = dy2.astype(BF16)
        _accumulate_rows(i, dg_ref, rows)
        _accumulate_rows(i, loss_ref, 0.5 * jnp.mean(e * e, axis=-1, keepdims=True))

    row = pl.BlockSpec((tb, D), lambda i: (i, 0))
    vec = pl.BlockSpec((1, D), lambda i: (0, 0))
    return pl.pallas_call(
        body, name=name, grid=(T // tb,),
        in_specs=[row, row, vec, row],
        out_specs=[row, row, pl.BlockSpec((1, 1), lambda i: (0, 0)), vec],
        out_shape=[jax.ShapeDtypeStruct((T, D), F32), jax.ShapeDtypeStruct((T, D), BF16),
                   jax.ShapeDtypeStruct((1, 1), F32), jax.ShapeDtypeStruct((1, D), F32)],
        compiler_params=_params("arbitrary"),
    )(x2, y2, gain, target)


def _rope_tables(positions, rot_dim):
    half = rot_dim // 2
    inv_freq = ROPE_THETA ** (-jnp.arange(0, rot_dim, 2, dtype=F32) / rot_dim)
    ang = positions.astype(F32)[:, None] * inv_freq[None, :]
    cos, sin = jnp.cos(ang), jnp.sin(ang)
    T = positions.shape[0]
    ones = jnp.ones((T, HD - rot_dim), F32)
    cos_t = jnp.concatenate([cos, cos, ones], axis=1)
    sin_t = jnp.concatenate([-sin, sin, jnp.zeros_like(ones)], axis=1)
    return cos_t, sin_t


def _rotate(x, cos_t, sin_t, half):
    lane = lax.broadcasted_iota(jnp.int32, x.shape, 1)
    swapped = jnp.where(lane < half, pltpu.roll(x, HD - half, 1), pltpu.roll(x, half, 1))
    return x * cos_t + swapped * sin_t


def _rope_apply(x, cos_t, sin_t, half, n_blocks, is_rope, out_dtype, name, window=0, tb=256):
    T = x.shape[0]
    tb = _tile(T, tb)
    W = n_blocks * HD

    def body(x_ref, c_ref, s_ref, o_ref):
        for j in range(n_blocks):
            sl = slice(j * HD, (j + 1) * HD)
            xj = x_ref[:, sl]
            if is_rope(j):
                xj = _rotate(xj.astype(F32), c_ref[...], s_ref[...], half)
            o_ref[:, sl] = xj.astype(o_ref.dtype)

    tab = pl.BlockSpec((tb, HD), lambda i: (i, 0))
    return pl.pallas_call(
        body, name=name, grid=(T // tb,),
        in_specs=[pl.BlockSpec((tb, W), lambda i: (i, window)), tab, tab],
        out_specs=pl.BlockSpec((tb, W), lambda i: (i, 0)),
        out_shape=jax.ShapeDtypeStruct((T, W), out_dtype),
        compiler_params=_params("parallel"),
    )(x, cos_t, sin_t)


DSWA_TB = 2048


def _deinterleave(src, dst_ref, d, dtype):
    rows = src.shape[0] // d
    for r in range(d):
        dst_ref[r] = src[pl.ds(r, rows, stride=d), :].astype(dtype)


def _rope_dswa(proj, cos_t, sin_t, name, comm=None):
    T = proj.shape[0]
    tb = _tile(T, DSWA_TB)
    half = ROT_A // 2

    def body(x_ref, c_ref, s_ref, *rest):
        outs, scr = rest[:-1], rest[-1]
        j = pl.program_id(1)

        @pl.when(j < 2 * HEADS)
        def _():
            scr[...] = _rotate(x_ref[...], c_ref[...], s_ref[...], half)

        @pl.when(j >= 2 * HEADS)
        def _():
            scr[...] = x_ref[...]

        for o_ref, d in zip(outs, DSWA_DILATIONS):
            _deinterleave(scr, o_ref, d, BF16)

    blk = pl.BlockSpec((tb, HD), lambda i, j: (i, j))
    tab = pl.BlockSpec((tb, HD), lambda i, j: (i, 0))
    return _call(
        body, name, (T // tb, 3 * HEADS), [blk, tab, tab],
        [pl.BlockSpec((d, tb // d, HD), lambda i, j: (0, i, j)) for d in DSWA_DILATIONS],
        [jax.ShapeDtypeStruct((d, T // d, 3 * AW), BF16) for d in DSWA_DILATIONS], (proj, cos_t, sin_t),
        scratch=[pltpu.VMEM((tb, HD), F32)], sem=("parallel", "parallel"), comm=comm)


def _shared_key_grad(dk, cos_t, sin_t_neg, half, name, into, tb=512):
    T = dk.shape[0]
    tb = _tile(T, tb)
    buf, buf_block = into

    def body(d_ref, c_ref, s_ref, buf_ref, o_ref):
        tot = d_ref[:, HD:2 * HD]
        for h in range(1, HEADS):
            tot = tot + d_ref[:, h * QK + HD:(h + 1) * QK]
        o_ref[...] = _rotate(tot, c_ref[...], s_ref[...], half).astype(o_ref.dtype)

    tab = pl.BlockSpec((tb, HD), lambda i: (i, 0))
    return pl.pallas_call(
        body, name=name, grid=(T // tb,),
        in_specs=[pl.BlockSpec((tb, HEADS * QK), lambda i: (i, 0)), tab, tab, pl.BlockSpec(memory_space=pl.ANY)],
        out_specs=pl.BlockSpec((tb, HD), lambda i: (i, buf_block)),
        out_shape=jax.ShapeDtypeStruct(buf.shape, buf.dtype), input_output_aliases={3: 0},
        compiler_params=_params("parallel"),
    )(dk, cos_t, sin_t_neg, buf)


def _band_mask(n):
    row = lax.broadcasted_iota(jnp.int32, (QBLK, 2 * QBLK), 0)
    col = lax.broadcasted_iota(jnp.int32, (QBLK, 2 * QBLK), 1)
    in_prev = jnp.logical_and(jnp.logical_and(col < QBLK, col >= row), n > 0)
    in_cur = jnp.logical_and(col >= QBLK, col - QBLK <= row)
    return jnp.logical_or(in_prev, in_cur)


def _dswa_specs(nb, reverse=False):
    pos = (lambda n: nb - 1 - n) if reverse else (lambda n: n)
    cur = lambda c: pl.BlockSpec((None, QBLK, AW), lambda r, n: (r, pos(n), c))
    prev = lambda c: pl.BlockSpec((None, QBLK, AW), lambda r, n: (r, jnp.maximum(pos(n) - 1, 0), c))
    stat = pl.BlockSpec((None, QBLK, HD), lambda r, n: (r, pos(n), 0))
    return cur, prev, stat


def _relayout_spec(d, tb, per_head=True):
    if per_head:
        return pl.BlockSpec((d, tb // d, HD), lambda i, h: (0, i, h))
    return pl.BlockSpec((d, tb // d, HD), lambda i, h: (0, i, 0))


def _head_lane(x, h):
    lane = lax.broadcasted_iota(jnp.int32, x.shape, 1)
    return jnp.sum(jnp.where(lane == h, x, 0.0), axis=-1, keepdims=True)


def _dswa_fwd(qkv, name, comm=None):
    d, sd = qkv.shape[:2]
    nb = sd // QBLK

    def body(q_ref, kc_ref, kp_ref, vc_ref, vp_ref, o_ref, l_ref):
        mask = _band_mask(pl.program_id(1))
        l_ref[...] = jnp.zeros_like(l_ref)
        for h in range(HEADS):
            sl = slice(h * HD, (h + 1) * HD)
            keys = jnp.concatenate([kp_ref[:, sl], kc_ref[:, sl]], axis=0)
            vals = jnp.concatenate([vp_ref[:, sl], vc_ref[:, sl]], axis=0)
            s = jnp.where(mask, _dot(q_ref[:, sl], keys, NT) * SCALE_A, NEG_INF)
            m = jnp.max(s, axis=-1, keepdims=True)
            p = jnp.exp(s - m)
            den = jnp.sum(p, axis=-1, keepdims=True)
            o_ref[:, sl] = _dot((p / den).astype(BF16), vals, NN)
            l_ref[:, h:h + 1] = m + jnp.log(den)

    cur, prev, stat = _dswa_specs(nb)
    return _call(
        body, name, (d, nb), [cur(0), cur(1), prev(1), cur(2), prev(2)], [cur(0), stat],
        [jax.ShapeDtypeStruct((d, sd, AW), F32), jax.ShapeDtypeStruct((d, sd, HD), F32)],
        (qkv, qkv, qkv, qkv, qkv), sem=("parallel", "parallel"), comm=comm)


def _dswa_merge(outs, lses, name, comm=None):
    nc = len(DSWA_DILATIONS)
    T = outs[0].shape[0] * outs[0].shape[1]
    tb = _tile(T, DSWA_TB)

    def body(*refs):
        o_refs, l_refs = refs[:nc], refs[nc:2 * nc]
        out_ref, outb_ref = refs[2 * nc:2 * nc + 2]
        lt_refs = refs[2 * nc + 2:3 * nc + 2]
        o_nat, l_nat, lt_nat = refs[3 * nc + 2:4 * nc + 2], refs[4 * nc + 2:5 * nc + 2], refs[-1]
        h = pl.program_id(1)
        for c, d in enumerate(DSWA_DILATIONS):
            for r in range(d):
                o_nat[c][pl.ds(r, tb // d, stride=d), :] = o_refs[c][r]
                l_nat[c][pl.ds(r, tb // d, stride=d), :] = l_refs[c][r]
        ls = [l[...] for l in l_nat]
        m = functools.reduce(jnp.maximum, ls)
        es = [jnp.exp(l - m) for l in ls]
        tot = functools.reduce(lambda a, b: a + b, es)
        acc = _head_lane(es[0] / tot, h) * o_nat[0][...]
        for c in range(1, nc):
            acc = acc + _head_lane(es[c] / tot, h) * o_nat[c][...]
        out_ref[...] = acc
        outb_ref[...] = acc.astype(BF16)
        lt_nat[...] = m + jnp.log(tot)
        for c, d in enumerate(DSWA_DILATIONS):
            _deinterleave(lt_nat, lt_refs[c], d, F32)

    nat = pl.BlockSpec((tb, HD), lambda i, h: (i, h))
    by_d = [_relayout_spec(d, tb) for d in DSWA_DILATIONS]
    stat_by_d = [_relayout_spec(d, tb, per_head=False) for d in DSWA_DILATIONS]
    res = _call(
        body, name, (T // tb, HEADS), by_d + stat_by_d, [nat, nat] + stat_by_d,
        [jax.ShapeDtypeStruct((T, AW), F32), jax.ShapeDtypeStruct((T, AW), BF16)]
        + [jax.ShapeDtypeStruct((d, T // d, HD), F32) for d in DSWA_DILATIONS], (*outs, *lses),
        scratch=[pltpu.VMEM((tb, HD), F32)] * (2 * nc + 1), sem=("parallel", "arbitrary"), comm=comm)
    return res[0], res[1], res[2:]


def _dswa_delta(dout, out, name):
    nc = len(DSWA_DILATIONS)
    T = out.shape[0]
    tb = _tile(T, DSWA_TB)

    def body(do_ref, o_ref, *rest):
        dl_refs, dob_refs, dl_nat = rest[:nc], rest[nc:2 * nc], rest[-1]
        h = pl.program_id(1)
        lane = lax.broadcasted_iota(jnp.int32, (tb, HD), 1)
        mine = jnp.where(lane == h, jnp.sum(do_ref[...] * o_ref[...], axis=-1, keepdims=True), 0.0)

        @pl.when(h == 0)
        def _():
            dl_nat[...] = mine

        @pl.when(h > 0)
        def _():
            dl_nat[...] += mine

        for c, d in enumerate(DSWA_DILATIONS):
            _deinterleave(dl_nat, dl_refs[c], d, F32)
            _deinterleave(do_ref, dob_refs[c], d, BF16)

    nat = pl.BlockSpec((tb, HD), lambda i, h: (i, h))
    by_d = [_relayout_spec(d, tb) for d in DSWA_DILATIONS]
    stat_by_d = [_relayout_spec(d, tb, per_head=False) for d in DSWA_DILATIONS]
    res = pl.pallas_call(
        body, name=name, grid=(T // tb, HEADS),
        in_specs=[nat, nat], out_specs=stat_by_d + by_d,
        out_shape=[jax.ShapeDtypeStruct((d, T // d, HD), F32) for d in DSWA_DILATIONS]
        + [jax.ShapeDtypeStruct((d, T // d, AW), BF16) for d in DSWA_DILATIONS],
        scratch_shapes=[pltpu.VMEM((tb, HD), F32)],
        compiler_params=_params("parallel", "arbitrary"),
    )(dout, out)
    return res[:nc], res[nc:]


def _delta_prep(dout, col_block, out, name, tb=256):
    T = out.shape[0]
    tb = _tile(T, tb)

    def body(do_ref, o_ref, delta_ref, dob_ref):
        delta_ref[...] = jnp.zeros_like(delta_ref)
        for h in range(HEADS):
            sl = slice(h * HD, (h + 1) * HD)
            doh = do_ref[:, sl]
            delta_ref[:, h:h + 1] = jnp.sum(doh * o_ref[:, sl], axis=-1, keepdims=True)
            dob_ref[:, sl] = doh.astype(BF16)

    row = pl.BlockSpec((tb, AW), lambda i: (i, 0))
    return pl.pallas_call(
        body, name=name, grid=(T // tb,),
        in_specs=[pl.BlockSpec((tb, AW), lambda i: (i, col_block)), row],
        out_specs=[pl.BlockSpec((tb, HD), lambda i: (i, 0)), row],
        out_shape=[jax.ShapeDtypeStruct((T, HD), F32), jax.ShapeDtypeStruct((T, AW), BF16)],
        compiler_params=_params("parallel"),
    )(dout, out)


def _dswa_bwd(qkv, dout_b, lse_tot, delta, name, comm=None):
    d, sd = qkv.shape[:2]
    nb = sd // QBLK

    def body(q_ref, kc_ref, kp_ref, vc_ref, vp_ref, do_ref, l_ref, dl_ref, g_ref, carry_k, carry_v):
        mask = _band_mask(nb - 1 - pl.program_id(1))

        @pl.when(pl.program_id(1) == 0)
        def _():
            carry_k[...] = jnp.zeros_like(carry_k)
            carry_v[...] = jnp.zeros_like(carry_v)

        for h in range(HEADS):
            sl = slice(h * HD, (h + 1) * HD)
            qh, doh = q_ref[:, sl], do_ref[:, sl]
            keys = jnp.concatenate([kp_ref[:, sl], kc_ref[:, sl]], axis=0)
            vals = jnp.concatenate([vp_ref[:, sl], vc_ref[:, sl]], axis=0)
            s = jnp.where(mask, _dot(qh, keys, NT) * SCALE_A, NEG_INF)
            p = jnp.exp(s - l_ref[:, h:h + 1])
            ds = (p * (_dot(doh, vals, NT) - dl_ref[:, h:h + 1]) * SCALE_A).astype(BF16)
            g_ref[:, sl] = _dot(ds, keys, NN).astype(BF16)
            dk = _dot(ds, qh, TN)
            dv = _dot(p.astype(BF16), doh, TN)
            g_ref[:, AW + h * HD:AW + (h + 1) * HD] = (dk[QBLK:] + carry_k[:, sl]).astype(BF16)
            g_ref[:, 2 * AW + h * HD:2 * AW + (h + 1) * HD] = (dv[QBLK:] + carry_v[:, sl]).astype(BF16)
            carry_k[:, sl] = dk[:QBLK]
            carry_v[:, sl] = dv[:QBLK]

    cur, prev, stat = _dswa_specs(nb, reverse=True)
    out_spec = pl.BlockSpec((None, QBLK, 3 * AW), lambda r, n: (r, nb - 1 - n, 0))
    return _call(
        body, name, (d, nb), [cur(0), cur(1), prev(1), cur(2), prev(2), cur(0), stat, stat], [out_spec],
        [jax.ShapeDtypeStruct((d, sd, 3 * AW), BF16)], (qkv, qkv, qkv, qkv, qkv, dout_b, lse_tot, delta),
        scratch=[pltpu.VMEM((QBLK, AW), F32)] * 2, sem=("parallel", "arbitrary"), comm=comm)[0]


def _dswa_combine(grads, cos_t, sin_t_neg, name, width):
    T = grads[0].shape[0] * grads[0].shape[1]
    tb = _tile(T, DSWA_TB)
    half = ROT_A // 2

    def body(*refs):
        g_refs = refs[:len(grads)]
        c_ref, s_ref, out_ref, acc = refs[len(grads):]
        j = pl.program_id(1)
        for c, d in enumerate(DSWA_DILATIONS):
            for r in range(d):
                if c == 0:
                    acc[...] = g_refs[c][r].astype(F32)
                else:
                    acc[pl.ds(r, tb // d, stride=d), :] += g_refs[c][r].astype(F32)
        val = acc[...]
        out_ref[...] = jnp.where(j < 2 * HEADS, _rotate(val, c_ref[...], s_ref[...], half), val).astype(BF16)

    tab = pl.BlockSpec((tb, HD), lambda i, j: (i, 0))
    return pl.pallas_call(
        body, name=name, grid=(T // tb, 3 * HEADS),
        in_specs=[pl.BlockSpec((d, tb // d, HD), lambda i, j: (0, i, j)) for d in DSWA_DILATIONS] + [tab, tab],
        out_specs=pl.BlockSpec((tb, HD), lambda i, j: (i, j)),
        out_shape=jax.ShapeDtypeStruct((T, width), BF16),
        scratch_shapes=[pltpu.VMEM((tb, HD), F32)],
        compiler_params=_params("parallel", "parallel"),
    )(*grads, cos_t, sin_t_neg)


MLA_TQ = 512
QK = 2 * HD
LOG2E = 1.4426950408889634


def _triangle(nq, key_major):
    pairs = [(q, k) for q in range(nq) for k in range(q + 1)]
    if key_major:
        pairs.sort(key=lambda p: (p[1], p[0]))
    return (jnp.array([p[0] for p in pairs], jnp.int32), jnp.array([p[1] for p in pairs], jnp.int32))


def _mla_specs(tq):
    q_spec = pl.BlockSpec((tq, HEADS * QK), lambda t, qi, ki: (qi[t], 0))
    k_spec = pl.BlockSpec((tq, HEADS * QK), lambda t, qi, ki: (ki[t], 0))
    v_spec = pl.BlockSpec((tq, AW), lambda t, qi, ki: (ki[t], 1))
    qrow = pl.BlockSpec((tq, AW), lambda t, qi, ki: (qi[t], 0))
    krow = pl.BlockSpec((tq, AW), lambda t, qi, ki: (ki[t], 0))
    return q_spec, k_spec, v_spec, qrow, krow


def _mla_stat_spec(tq):
    return pl.BlockSpec((tq, HD), lambda t, qi, ki: (qi[t], 0))


def _mla_scores(q_ref, k_ref, h, qi, ki, tq):
    s = _dot(q_ref[:, h * QK:(h + 1) * QK], k_ref[:, h * QK:(h + 1) * QK], NT) * SCALE_B
    row = lax.broadcasted_iota(jnp.int32, s.shape, 0) + qi * tq
    col = lax.broadcasted_iota(jnp.int32, s.shape, 1) + ki * tq
    return jnp.where(col <= row, s, NEG_INF)


def _mla_fwd(q, k, v1, name, comm=None):
    T = q.shape[0]
    tq = _tile(T, MLA_TQ)
    tables = _triangle(T // tq, False)

    def body(qi_ref, ki_ref, q_ref, k_ref, v_ref, o_ref, ob_ref, l_ref, m_s, acc):
        t = pl.program_id(0)
        qi, ki = qi_ref[t], ki_ref[t]

        @pl.when(ki == 0)
        def _():
            m_s[...] = jnp.full_like(m_s, NEG_INF)
            acc[...] = jnp.zeros_like(acc)

        row = lax.broadcasted_iota(jnp.int32, (tq, tq), 0) + qi * tq
        col = lax.broadcasted_iota(jnp.int32, (tq, tq), 1) + ki * tq
        bias = jnp.where(col <= row, 0.0, NEG_INF)
        updates = []
        for h in range(HEADS):
            s = _dot(q_ref[:, h * QK:(h + 1) * QK], k_ref[:, h * QK:(h + 1) * QK], NT) + bias
            m_new = jnp.maximum(m_s[h], jnp.max(s, axis=-1, keepdims=True))
            p = jnp.exp2((s - m_new) * (SCALE_B * LOG2E)).astype(BF16)
            alpha = jnp.exp2((m_s[h] - m_new) * (SCALE_B * LOG2E))
            updates.append((m_new, alpha, _dot(p, v_ref[:, h * QK:(h + 1) * QK], NN)))
        for h, (m_new, alpha, pv) in enumerate(updates):
            acc[:, h * QK:(h + 1) * QK] = alpha * acc[:, h * QK:(h + 1) * QK] + pv
            m_s[h] = m_new

        @pl.when(ki == qi)
        def _():
            l_ref[...] = jnp.zeros_like(l_ref)
            for h in range(HEADS):
                sl = slice(h * HD, (h + 1) * HD)
                den = acc[:, h * QK + HD:h * QK + HD + 1]
                out = acc[:, h * QK:h * QK + HD] / den
                o_ref[:, sl] = out
                ob_ref[:, sl] = out.astype(BF16)
                l_ref[:, h:h + 1] = m_s[h] * SCALE_B + jnp.log(den)

    q_spec, k_spec, _, qrow, _ = _mla_specs(tq)
    return _call(
        body, name, (tables[0].shape[0],), [q_spec, k_spec, k_spec], [qrow, qrow, _mla_stat_spec(tq)],
        [jax.ShapeDtypeStruct((T, AW), F32), jax.ShapeDtypeStruct((T, AW), BF16), jax.ShapeDtypeStruct((T, HD), F32)],
        (q, k, v1),
        scratch=[pltpu.VMEM((HEADS, tq, 1), F32), pltpu.VMEM((tq, HEADS * QK), F32)],
        sem=("arbitrary",), comm=comm, prefetch=tables)


def _mla_ds(q_ref, k_ref, v_ref, do_ref, l_ref, dl_ref, h, qi, ki, tq):
    sl = slice(h * HD, (h + 1) * HD)
    p = jnp.exp(_mla_scores(q_ref, k_ref, h, qi, ki, tq) - l_ref[:, h:h + 1])
    ds = (p * (_dot(do_ref[:, sl], v_ref[:, sl], NT) - dl_ref[:, h:h + 1]) * SCALE_B).astype(BF16)
    return p, ds


def _mla_bwd_q(q, k, kv, dout_b, lse, delta, name, comm=None):
    T = q.shape[0]
    tq = _tile(T, MLA_TQ)
    tables = _triangle(T // tq, False)

    def body(qi_ref, ki_ref, q_ref, k_ref, v_ref, do_ref, l_ref, dl_ref, dq_ref):
        t = pl.program_id(0)
        qi, ki = qi_ref[t], ki_ref[t]

        @pl.when(ki == 0)
        def _():
            dq_ref[...] = jnp.zeros_like(dq_ref)

        for h in range(HEADS):
            _, ds = _mla_ds(q_ref, k_ref, v_ref, do_ref, l_ref, dl_ref, h, qi, ki, tq)
            dq_ref[:, h * QK:(h + 1) * QK] += _dot(ds, k_ref[:, h * QK:(h + 1) * QK], NN)

    q_spec, k_spec, v_spec, qrow, _ = _mla_specs(tq)
    return _call(
        body, name, (tables[0].shape[0],), [q_spec, k_spec, v_spec, qrow, _mla_stat_spec(tq), _mla_stat_spec(tq)], [q_spec],
        [jax.ShapeDtypeStruct((T, HEADS * QK), F32)], (q, k, kv, dout_b, lse, delta),
        sem=("arbitrary",), comm=comm, prefetch=tables)[0]


def _mla_bwd_kv(q, k, kv, dout_b, lse, delta, name, comm=None):
    T = q.shape[0]
    tq = _tile(T, MLA_TQ)
    nq = T // tq
    tables = _triangle(nq, True)

    def body(qi_ref, ki_ref, q_ref, k_ref, v_ref, do_ref, l_ref, dl_ref, dk_ref, dkv_ref, dv_acc):
        t = pl.program_id(0)
        qi, ki = qi_ref[t], ki_ref[t]

        @pl.when(qi == ki)
        def _():
            dk_ref[...] = jnp.zeros_like(dk_ref)
            dv_acc[...] = jnp.zeros_like(dv_acc)

        for h in range(HEADS):
            sl = slice(h * HD, (h + 1) * HD)
            p, ds = _mla_ds(q_ref, k_ref, v_ref, do_ref, l_ref, dl_ref, h, qi, ki, tq)
            dv_acc[:, sl] += _dot(p.astype(BF16), do_ref[:, sl], TN)
            dk_ref[:, h * QK:(h + 1) * QK] += _dot(ds, q_ref[:, h * QK:(h + 1) * QK], TN)

        @pl.when(qi == nq - 1)
        def _():
            for h in range(HEADS):
                dkv_ref[:, h * HD:(h + 1) * HD] = dk_ref[:, h * QK:h * QK + HD].astype(BF16)
                dkv_ref[:, AW + h * HD:AW + (h + 1) * HD] = dv_acc[:, h * HD:(h + 1) * HD].astype(BF16)

    q_spec, k_spec, v_spec, qrow, _ = _mla_specs(tq)
    return _call(
        body, name, (tables[0].shape[0],),
        [q_spec, k_spec, v_spec, qrow, _mla_stat_spec(tq), _mla_stat_spec(tq)],
        [k_spec, pl.BlockSpec((tq, 2 * AW), lambda t, qi, ki: (ki[t], 0))],
        [jax.ShapeDtypeStruct((T, HEADS * QK), F32), jax.ShapeDtypeStruct((T, 2 * AW), BF16)],
        (q, k, kv, dout_b, lse, delta), scratch=[pltpu.VMEM((tq, AW), F32)],
        sem=("arbitrary",), comm=comm, prefetch=tables)


def _pair_sum(by_device, from_sibling, name, tb=256):
    n_chip, R, C = from_sibling.shape
    tb = _tile(R, tb)
    core = jnp.reshape(lax.axis_index("c"), (1,)).astype(jnp.int32)

    def body(core_ref, mine_ref, theirs_ref, o_ref):
        o_ref[...] = (mine_ref[...].astype(F32) + theirs_ref[...].astype(F32)).astype(o_ref.dtype)

    blk = pl.BlockSpec((None, tb, C), lambda p, i, core_ref: (p, i, 0))
    return _call(
        body, name, (n_chip, R // tb),
        [pl.BlockSpec((None, tb, C), lambda p, i, core_ref: (2 * p + core_ref[0], i, 0)), blk], [blk],
        [jax.ShapeDtypeStruct((n_chip, R, C), BF16)], (by_device, from_sibling),
        sem=("parallel", "parallel"), prefetch=(core,))[0]


def _adamw(parts, w, m, v, name, tb=128, comm=None):
    R, C = w.shape
    n_parts = parts.shape[0]
    tb = _tile(R, tb)
    c1 = 1.0 - ADAM_B1
    c2 = 1.0 - ADAM_B2
    bc1 = 1.0 - ADAM_B1 ** ADAM_STEP
    bc2 = 1.0 - ADAM_B2 ** ADAM_STEP

    def body(p_ref, w_ref, m_ref, v_ref, g_ref, d_ref, nm_ref, nv_ref):
        g = p_ref[0].astype(F32)
        for j in range(1, n_parts):
            g = g + p_ref[j].astype(F32)
        nm = ADAM_B1 * m_ref[...] + c1 * g
        nv = ADAM_B2 * v_ref[...] + c2 * (g * g)
        g_ref[...] = g
        nm_ref[...] = nm
        nv_ref[...] = nv
        d_ref[...] = -ADAM_LR * ((nm / bc1) / (jnp.sqrt(nv / bc2) + ADAM_EPS) + ADAM_WD * w_ref[...])

    row = pl.BlockSpec((tb, C), lambda i: (i, 0))
    return _call(
        body, name, (R // tb,), [pl.BlockSpec((n_parts, tb, C), lambda i: (0, i, 0)), row, row, row], [row] * 4,
        [jax.ShapeDtypeStruct((R, C), F32)] * 4, (parts, w, m, v), sem=("parallel",), comm=comm)


def _cols_from_shards(g):
    return jnp.transpose(g, (1, 0, 2)).reshape(g.shape[1], N_DEV * g.shape[2])


def _cols_to_shards(w):
    return jnp.transpose(w.reshape(w.shape[0], N_DEV, w.shape[1] // N_DEV), (1, 0, 2))


def _split_heads(w, first):
    w3 = w.reshape(w.shape[0], HEADS, -1)
    return w3[:, :, :first].reshape(w.shape[0], -1), w3[:, :, first:].reshape(w.shape[0], -1)


def _join_heads(a, b):
    R = a.shape[0]
    return jnp.concatenate([a.reshape(R, HEADS, -1), b.reshape(R, HEADS, -1)], axis=2).reshape(R, -1)


def _pad_heads(w, width):
    w3 = w.reshape(w.shape[0], HEADS, -1)
    return jnp.pad(w3, ((0, 0), (0, 0), (0, width - w3.shape[2]))).reshape(w.shape[0], HEADS * width)


def _unpad_heads(w, k):
    return w.reshape(w.shape[0], HEADS, -1)[:, :, :k].reshape(w.shape[0], HEADS * k)


def kernel(x, positions, norm_attn_pre, norm_attn_post, w_in, q_latent_norm, kv_latent_norm, w_uq, w_ukv, w_out, norm_mlp_pre, norm_mlp_post, w_up, w_down, loss_target, m_norm_attn_pre, m_norm_attn_post, m_w_in, m_q_latent_norm, m_kv_latent_norm, m_w_uq, m_w_ukv, m_w_out, m_norm_mlp_pre, m_norm_mlp_post, m_w_up, m_w_down, v_norm_attn_pre, v_norm_attn_post, v_w_in, v_q_latent_norm, v_kv_latent_norm, v_w_uq, v_w_ukv, v_w_out, v_norm_mlp_pre, v_norm_mlp_post, v_w_up, v_w_down):
    xs = x[0]
    tgt = loss_target[0]
    pos = positions[0]
    T, D = xs.shape
    big = dict(w_in=(w_in, m_w_in, v_w_in), w_uq=(w_uq, m_w_uq, v_w_uq), w_ukv=(w_ukv, m_w_ukv, v_w_ukv),
               w_out=(w_out, m_w_out, v_w_out), w_up=(w_up, m_w_up, v_w_up), w_down=(w_down, m_w_down, v_w_down))
    big = {n: tuple(t[0] for t in ts) for n, ts in big.items()}
    big_names = ["w_in", "w_uq", "w_ukv", "w_out", "w_up", "w_down"]
    col_sharded = {"w_in", "w_uq", "w_ukv", "w_up"}

    wb = {n: big[n][0].astype(BF16) for n in big_names}

    def gathered(ex, i, n):
        g = ex.results[i]
        return _cols_from_shards(g) if n in col_sharded else g.reshape(-1, g.shape[2])

    def by_device(g, n):
        return _cols_to_shards(g) if n in col_sharded else g.reshape(N_DEV, g.shape[0] // N_DEV, g.shape[1])

    def scatter_of(g, n):
        return _Exchange([by_device(g, n)], "scatter")

    cos_a, sin_a = _rope_tables(pos, ROT_A)
    cos_b, sin_b = _rope_tables(pos, ROPE_MLA)

    ex_in = _Exchange([wb["w_in"]], "gather")
    h1 = _rms_fwd(xs, norm_attn_pre, BF16, "norm_attn_pre_fwd", comm=ex_in)
    Wi = jnp.pad(gathered(ex_in, 0, "w_in"), ((0, 0), (0, IN_PAD - IN_COLS)))
    ex_mid = _Exchange([wb["w_uq"], wb["w_ukv"], wb["w_out"]], "gather")
    proj = _mm(h1, Wi, "nn", [F32], "proj_in", tn=1408, comm=ex_mid)
    Wuq = _pad_heads(gathered(ex_mid, 0, "w_uq"), QK)
    Wukv = jnp.concatenate(_split_heads(gathered(ex_mid, 1, "w_ukv"), HD), axis=1)
    Wo = gathered(ex_mid, 2, "w_out")
    n_piece = wb["w_down"].shape[0] // 8
    ex_down = [_Exchange([wb["w_down"][i * n_piece:(i + 1) * n_piece]], "gather") for i in range(8)]
    qkv_by_d = _rope_dswa(proj, cos_a, sin_a, "rope_dswa", comm=ex_down[0])
    outs, lses = [], []
    for d, qkv, ex in zip(DSWA_DILATIONS, qkv_by_d, ex_down[1:4]):
        o, l = _dswa_fwd(qkv, f"dswa_fwd_d{d}", comm=ex)
        outs.append(o)
        lses.append(l)
    a_out, a_out_b, a_lse_by_d = _dswa_merge(outs, lses, "dswa_merge", comm=ex_down[4])

    cqn = _rms_fwd(proj, q_latent_norm, BF16, "q_latent_norm_fwd", width=Q_LORA, col_block=3 * AW // Q_LORA)
    ckvn = _rms_fwd(proj, kv_latent_norm, BF16, "kv_latent_norm_fwd", width=KV_LORA, col_block=3 * AW // KV_LORA + 1)
    qb = _mm(cqn, Wuq, "nn", [F32], "q_up")
    kvb = _mm(ckvn, Wukv, "nn", [BF16], "kv_up")
    odd = lambda j: j % 2 == 1
    q_mla = _rope_apply(qb, cos_b, sin_b, ROPE_MLA // 2, 2 * HEADS, odd, BF16, "rope_mla_q")
    kr = _rope_apply(proj, cos_b, sin_b, ROPE_MLA // 2, 1, lambda j: True, BF16, "rope_mla_k",
                     window=(IN_PAD - HD) // HD)
    k_mla = jnp.concatenate([kvb[:, :AW].reshape(T, HEADS, HD), jnp.broadcast_to(kr[:, None, :], (T, HEADS, HD))],
                            axis=2).reshape(T, HEADS * QK)
    v1_mla = jnp.concatenate([kvb[:, AW:].reshape(T, HEADS, HD), jnp.ones((T, HEADS, 1), BF16),
                              jnp.zeros((T, HEADS, HD - 1), BF16)], axis=2).reshape(T, HEADS * QK)
    ex_up = _Exchange([wb["w_up"]], "gather")
    b_out, b_out_b, b_lse = _mla_fwd(q_mla, k_mla, v1_mla, "mla_fwd", comm=ex_up)
    Wup_shards = ex_up.results[0]

    mixed = jnp.concatenate([a_out_b, b_out_b], axis=1)
    y1 = _mm(mixed, Wo, "nn", [F32], "attn_out", comm=ex_down[5])

    x2, h2 = _norm_pair_fwd(y1, xs, norm_attn_post, norm_mlp_pre, "norm_attn_post_mlp_pre_fwd")

    def relu2(z):
        r = jnp.maximum(z, 0.0)
        return r * r, r

    u, zr = _mm(h2, Wup_shards, "nn", [BF16, BF16], "mlp_up", epilogue=relu2, comm=ex_down[6:8],
                b_shards=True)
    Wdn = jnp.concatenate([ex.results[0] for ex in ex_down], axis=1).reshape(-1, D)
    y2 = _mm(u, Wdn, "nn", [F32], "mlp_down")
    dx3, dy2, loss_part, dg_mlp_post = _loss_head(x2, y2, norm_mlp_post, tgt, "loss_head")

    dz =_mm(dy2, Wdn, "nt", [BF16], "mlp_down_dx", epilogue=lambda du, r: (du * (2.0 * r.astype(F32)),), extras=(zr,))
    g_down = _mm(u, dy2, "tn", [BF16], "mlp_down_dw")
    down_dev = by_device(g_down, "w_down")
    pair_down = _Exchange([down_dev], "pair")
    up_dev = _mm(h2, dz, "tn", [BF16], "mlp_up_dw", comm=pair_down, out_shards=True)
    down_chip = _pair_sum(down_dev, pair_down.results[0], "pair_sum_w_down")
    pair_up = _Exchange([up_dev], "pair")
    cut = 5 * down_chip.shape[1] // 8
    sc_down = [_Exchange([down_chip[:, :cut]], "chips"), _Exchange([down_chip[:, cut:]], "chips")]
    dh2 = _mm(dz, Wup_shards, "nt", [F32], "mlp_up_dx", comm=[pair_up, sc_down[0]], b_shards=True)
    up_chip = _pair_sum(up_dev, pair_up.results[0], "pair_sum_w_up")

    dx2, dy1, dg_mlp_pre, dg_attn_post = _norm_pair_bwd(dh2, x2, norm_mlp_pre, dx3, y1, norm_attn_post,
                                                        "norm_mlp_pre_attn_post_bwd")
    dmixed = _mm(dy1, Wo, "nt", [F32], "attn_out_dx")
    g_out = _mm(mixed, dy1, "tn", [BF16], "attn_out_dw")

    b_delta, b_dout = _delta_prep(dmixed, 1, b_out, "mla_delta")
    sc_up = _Exchange([up_chip], "chips")
    dq_mla = _mla_bwd_q(q_mla, k_mla, kvb, b_dout, b_lse, b_delta, "mla_bwd_q", comm=sc_up)
    sc_out = scatter_of(g_out, "w_out")
    dk_mla, dkvb = _mla_bwd_kv(q_mla, k_mla, kvb, b_dout, b_lse, b_delta, "mla_bwd_kv", comm=[sc_out, sc_down[1]])
    dqb = _rope_apply(dq_mla, cos_b, -sin_b, ROPE_MLA // 2, 2 * HEADS, odd, BF16, "rope_mla_q_bwd")
    g_uq_pad = _mm(cqn, dqb, "tn", [BF16], "q_up_dw")
    g_ukv_perm = _mm(ckvn, dkvb, "tn", [BF16], "kv_up_dw")
    dcqn = _mm(dqb, Wuq, "nt", [F32], "q_up_dx")
    dckvn = _mm(dkvb, Wukv, "nt", [F32], "kv_up_dx")

    g_uq = _unpad_heads(g_uq_pad, HD + ROPE_MLA)
    g_ukv = _join_heads(g_ukv_perm[:, :AW], g_ukv_perm[:, AW:])
    sc_uq = _Exchange([_cols_to_shards(g_uq), _cols_to_shards(g_ukv)], "scatter")
    a_delta_by_d, a_dout_by_d = _dswa_delta(dmixed, a_out, "dswa_delta")
    a_grads = [_dswa_bwd(qkv_by_d[c], a_dout_by_d[c], a_lse_by_d[c], a_delta_by_d[c], f"dswa_bwd_d{d}")
               for c, d in enumerate(DSWA_DILATIONS)]
    dproj = _dswa_combine(a_grads, cos_a, -sin_a, "dswa_combine", IN_PAD)
    dproj, dg_q = _rms_bwd(dcqn, proj, q_latent_norm, BF16, "q_latent_norm_bwd", width=Q_LORA,
                           col_block=3 * AW // Q_LORA, into=(dproj, 3 * AW // Q_LORA))
    dproj, dg_kv = _rms_bwd(dckvn, proj, kv_latent_norm, BF16, "kv_latent_norm_bwd", width=KV_LORA,
                            col_block=3 * AW // KV_LORA + 1, into=(dproj, 3 * AW // KV_LORA + 1))
    dproj = _shared_key_grad(dk_mla, cos_b, -sin_b, ROPE_MLA // 2, "rope_mla_k_bwd", (dproj, (IN_PAD - HD) // HD))
    g_in_pad = _mm(h1, dproj, "tn", [BF16], "proj_in_dw", tn=1408, comm=sc_uq)
    in_dev = by_device(g_in_pad[:, :IN_COLS], "w_in")
    pair_in = _Exchange([in_dev], "pair").standalone("pair_w_in")
    sc_in = _Exchange([_pair_sum(in_dev, pair_in[0], "pair_sum_w_in")], "chips")
    dh1 = _mm(dproj, Wi, "nt", [F32], "proj_in_dx", comm=sc_in)
    grad_x, dg_attn_pre = _rms_bwd(dh1, xs, norm_attn_pre, F32, "norm_attn_pre_bwd", residual=dx2)

    parts = dict(w_in=sc_in.results[0], w_uq=sc_uq.results[0], w_ukv=sc_uq.results[1], w_out=sc_out.results[0],
                 w_up=sc_up.results[0], w_down=jnp.concatenate([sc.results[0] for sc in sc_down], axis=1))
    big_out = {n: _adamw(parts[n], *big[n], f"adamw_{n}") for n in big_names}

    gain_names = ["norm_attn_pre", "norm_attn_post", "q_latent_norm", "kv_latent_norm", "norm_mlp_pre", "norm_mlp_post"]
    gain_args = dict(norm_attn_pre=(norm_attn_pre, m_norm_attn_pre, v_norm_attn_pre),
                     norm_attn_post=(norm_attn_post, m_norm_attn_post, v_norm_attn_post),
                     q_latent_norm=(q_latent_norm, m_q_latent_norm, v_q_latent_norm),
                     kv_latent_norm=(kv_latent_norm, m_kv_latent_norm, v_kv_latent_norm),
                     norm_mlp_pre=(norm_mlp_pre, m_norm_mlp_pre, v_norm_mlp_pre),
                     norm_mlp_post=(norm_mlp_post, m_norm_mlp_post, v_norm_mlp_post))
    gain_grads = dict(norm_attn_pre=dg_attn_pre, norm_attn_post=dg_attn_post, q_latent_norm=dg_q,
                      kv_latent_norm=dg_kv, norm_mlp_pre=dg_mlp_pre, norm_mlp_post=dg_mlp_post)
    packed = jnp.concatenate([gain_grads[n] for n in gain_names], axis=1)
    gain_parts = _Exchange([packed], "gather").standalone("gather_gain_grads")[0]
    pack3 = lambda i: jnp.concatenate([gain_args[n][i] for n in gain_names], axis=1)
    gain_out = _adamw(gain_parts, pack3(0), pack3(1), pack3(2), "adamw_gains", tb=1)
    offs = [0]
    for n in gain_names:
        offs.append(offs[-1] + gain_args[n][0].shape[1])
    small_out = {n: tuple(o[:, offs[i]:offs[i + 1]] for o in gain_out) for i, n in enumerate(gain_names)}

    loss = lax.psum(loss_part[0, 0], ("x", "y", "c"))

    order = ["norm_attn_pre", "norm_attn_post", "w_in", "q_latent_norm", "kv_latent_norm", "w_uq", "w_ukv", "w_out",
             "norm_mlp_pre", "norm_mlp_post", "w_up", "w_down"]
    res = {n: (small_out[n] if n in small_out else tuple(o[None] for o in big_out[n])) for n in order}
    return (loss, grad_x[None], *[res[n][0] for n in order], *[res[n][1] for n in order],
            *[res[n][2] for n in order], *[res[n][3] for n in order])
```

```python
import functools
import math

import jax
import jax.numpy as jnp
from jax import lax
from jax.experimental import pallas as pl
from jax.experimental.pallas import tpu as pltpu

F32 = jnp.float32
BF16 = jnp.bfloat16

N_DEV = 8
HEADS = 8
HD = 128
AW = HEADS * HD
Q_LORA = 512
KV_LORA = 512
ROPE_MLA = 64
ROT_A = 32
IN_COLS = 3 * AW + Q_LORA + KV_LORA + ROPE_MLA
IN_PAD = 3 * AW + Q_LORA + KV_LORA + HD
QBLK = 128
DSWA_DILATIONS = (1, 4, 16)
ROPE_THETA = 500000.0
NORM_EPS = 1e-6
NEG_INF = -1e30
SCALE_A = HD ** -0.5
SCALE_B = (HD + ROPE_MLA) ** -0.5

ADAM_LR = 0.001
ADAM_B1 = 0.9
ADAM_B2 = 0.999
ADAM_EPS = 1e-08
ADAM_WD = 0.01
ADAM_STEP = 10

VMEM_LIMIT = 48 * 1024 * 1024

NT = (((1,), (1,)), ((), ()))
NN = (((1,), (0,)), ((), ()))
TN = (((0,), (0,)), ((), ()))


def _dot(a, b, dims):
    return lax.dot_general(a, b, dims, preferred_element_type=F32)


def _params(*sem):
    return pltpu.CompilerParams(dimension_semantics=sem, vmem_limit_bytes=VMEM_LIMIT)


def _tile(n, want):
    t = min(n, want)
    while n % t:
        t //= 2
    return t


def _tile128(n, want):
    if n % 128:
        return n
    units = n // 128
    return 128 * max(u for u in range(1, max(want // 128, 1) + 1) if units % u == 0)


class _Exchange:
    def __init__(self, arrs, mode):
        self.arrs = list(arrs)
        self.mode = mode
        self.n = len(self.arrs)
        self.results = None
        hbm = pl.BlockSpec(memory_space=pltpu.HBM)
        self.specs = [hbm] * self.n
        shape = {"gather": lambda a: (N_DEV,) + a.shape, "scatter": lambda a: a.shape,
                 "pair": lambda a: (4,) + a.shape[1:], "chips": lambda a: a.shape}[mode]
        self.out_shape = [jax.ShapeDtypeStruct(shape(a), a.dtype) for a in self.arrs]
        n_sem = self.n * (N_DEV - 1)
        self.scratch = [pltpu.SemaphoreType.DMA((n_sem,)), pltpu.SemaphoreType.DMA((n_sem,)),
                        pltpu.SemaphoreType.DMA((self.n,))]

    def hooks(self, ins, outs, send_sems, recv_sems, local_sems):
        x, y, c = lax.axis_index("x"), lax.axis_index("y"), lax.axis_index("c")
        me = (x, y, c)
        sib = (x, y, 1 - c)
        chips = [(1 - x, y), (x, 1 - y), (1 - x, 1 - y)]
        slot = lambda p: 4 * p[0] + 2 * p[1] + p[2]
        chip_of = lambda p: 2 * p[0] + p[1]

        def rcopy(a, k, src, dst, to):
            i = a * (N_DEV - 1) + k
            return pltpu.make_async_remote_copy(src_ref=src, dst_ref=dst, send_sem=send_sems.at[i],
                                                recv_sem=recv_sems.at[i], device_id=to,
                                                device_id_type=pl.DeviceIdType.MESH)

        def local(a):
            if self.mode == "chips":
                return pltpu.make_async_copy(ins[a].at[chip_of(me)], outs[a].at[chip_of(me)], local_sems.at[a])
            src = ins[a].at[slot(me)] if self.mode == "scatter" else ins[a]
            return pltpu.make_async_copy(src, outs[a].at[slot(me)], local_sems.at[a])

        def peer(rel):
            return (1 - x if rel & 4 else x, 1 - y if rel & 2 else y, 1 - c if rel & 1 else c)

        if self.mode == "pair":
            def start():
                for a in range(self.n):
                    for p in range(4):
                        rcopy(a, p, ins[a].at[2 * p + 1 - c], outs[a].at[p], sib).start()

            def middle():
                pass

            def finish():
                for a in range(self.n):
                    for p in range(4):
                        cp = rcopy(a, p, ins[a].at[2 * p + 1 - c], outs[a].at[p], sib)
                        cp.wait_send()
                        cp.wait_recv()
        elif self.mode == "chips":
            def start():
                for a in range(self.n):
                    local(a).start()
                    for j, chip in enumerate(chips):
                        rcopy(a, j, ins[a].at[chip_of(chip)], outs[a].at[chip_of(me)], (*chip, c)).start()

            def middle():
                pass

            def finish():
                for a in range(self.n):
                    for j, chip in enumerate(chips):
                        cp = rcopy(a, j, ins[a].at[chip_of(chip)], outs[a].at[chip_of(chip)], (*chip, c))
                        cp.wait_send()
                        cp.wait_recv()
                    local(a).wait()
        elif self.mode == "scatter":
            def start():
                for a in range(self.n):
                    local(a).start()
                    for rel in range(1, N_DEV):
                        rcopy(a, rel - 1, ins[a].at[slot(peer(rel))], outs[a].at[slot(me)], peer(rel)).start()

            def middle():
                pass

            def finish():
                for a in range(self.n):
                    for rel in range(1, N_DEV):
                        cp = rcopy(a, rel - 1, ins[a].at[slot(peer(rel))], outs[a].at[slot(peer(rel))], peer(rel))
                        cp.wait_send()
                        cp.wait_recv()
                    local(a).wait()
        else:
            def start():
                for a in range(self.n):
                    local(a).start()
                    rcopy(a, 0, ins[a], outs[a].at[slot(me)], sib).start()
                    for j, chip in enumerate(chips):
                        rcopy(a, 1 + j, ins[a], outs[a].at[slot(me)], (*chip, c)).start()

            def middle():
                for a in range(self.n):
                    for j, chip in enumerate(chips):
                        landed = outs[a].at[slot((*chip, c))]
                        rcopy(a, 1 + j, ins[a], landed, me).wait_recv()
                        rcopy(a, 4 + j, landed, landed, sib).start()

            def finish():
                for a in range(self.n):
                    rcopy(a, 0, ins[a], outs[a].at[slot(sib)], me).wait_recv()
                    for j, chip in enumerate(chips):
                        rcopy(a, 4 + j, ins[a], outs[a].at[slot((*chip, 1 - c))], me).wait_recv()
                    for k in range(N_DEV - 1):
                        rcopy(a, k, ins[a], outs[a].at[slot(me)], me).wait_send()
                    local(a).wait()

        return start, middle, finish

    def set_results(self, res):
        self.results = list(res)

    def standalone(self, name):
        n = self.n

        def body(*refs):
            start, middle, finish = self.hooks(refs[:n], refs[n:2 * n], *refs[2 * n:])
            start()
            middle()
            finish()

        self.results = pl.pallas_call(
            body, name=name, in_specs=self.specs, out_specs=self.specs, out_shape=self.out_shape,
            scratch_shapes=self.scratch, compiler_params=pltpu.CompilerParams(has_side_effects=True),
        )(*self.arrs)
        return self.results


class _Carried:
    def __init__(self, parts):
        self.parts = list(parts)
        self.n = sum(p.n for p in self.parts)
        self.arrs = [a for p in self.parts for a in p.arrs]
        self.specs = [s for p in self.parts for s in p.specs]
        self.out_shape = [s for p in self.parts for s in p.out_shape]
        self.scratch = [s for p in self.parts for s in p.scratch]

    def hooks(self, ins, outs, *sems):
        hooks, i = [], 0
        for j, p in enumerate(self.parts):
            hooks.append(p.hooks(ins[i:i + p.n], outs[i:i + p.n], *sems[3 * j:3 * j + 3]))
            i += p.n
        def phase(k):
            def run():
                for h in hooks:
                    h[k]()
            return run

        return phase(0), phase(1), phase(2)

    def set_results(self, res):
        i = 0
        for p in self.parts:
            p.set_results(res[i:i + p.n])
            i += p.n


def _call(body, name, grid, in_specs, out_specs, out_shape, args, scratch=(), sem=(), comm=None, prefetch=(),
          aliases=None):
    npf = len(prefetch)
    if isinstance(comm, (list, tuple)):
        comm = _Carried(comm)
    if comm is None:
        spec = pltpu.PrefetchScalarGridSpec(num_scalar_prefetch=npf, grid=grid, in_specs=list(in_specs),
                                            out_specs=list(out_specs), scratch_shapes=list(scratch))
        return pl.pallas_call(body, name=name, grid_spec=spec, out_shape=list(out_shape),
                              input_output_aliases=aliases or {},
                              compiler_params=_params(*sem))(*prefetch, *args)
    assert aliases is None
    ni, no, ns, n = len(in_specs), len(out_specs), len(scratch), comm.n
    steps = math.prod(grid)

    def wrapped(*refs):
        pf, refs = refs[:npf], refs[npf:]
        ins, c_ins = refs[:ni], refs[ni:ni + n]
        outs, c_outs = refs[ni + n:ni + n + no], refs[ni + n + no:ni + 2 * n + no]
        scr, c_scr = refs[ni + 2 * n + no:ni + 2 * n + no + ns], refs[ni + 2 * n + no + ns:]
        start, middle, finish = comm.hooks(c_ins, c_outs, *c_scr)
        step = pl.program_id(0)
        for ax in range(1, len(grid)):
            step = step * grid[ax] + pl.program_id(ax)
        pl.when(step == 0)(start)
        pl.when(step == steps // 2)(middle)
        body(*pf, *ins, *outs, *scr)
        pl.when(step == steps - 1)(finish)

    spec = pltpu.PrefetchScalarGridSpec(num_scalar_prefetch=npf, grid=grid, in_specs=list(in_specs) + comm.specs,
                                        out_specs=list(out_specs) + comm.specs,
                                        scratch_shapes=list(scratch) + comm.scratch)
    res = pl.pallas_call(
        wrapped, name=name, grid_spec=spec, out_shape=list(out_shape) + comm.out_shape,
        compiler_params=pltpu.CompilerParams(dimension_semantics=("arbitrary",) * len(grid),
                                             vmem_limit_bytes=VMEM_LIMIT, has_side_effects=True),
    )(*prefetch, *args, *comm.arrs)
    comm.set_results(res[no:])
    return res[:no]


def _mm(a, b, mode, out_dtypes, name, epilogue=None, extras=(), tm=1024, tn=1024, tk=2048, comm=None,
        b_shards=False, out_shards=False):
    if mode == "tn":
        K, M = a.shape
    else:
        M, K = a.shape
    if b_shards:
        N = b.shape[1] if mode == "nt" else N_DEV * b.shape[2]
    else:
        N = b.shape[0] if mode == "nt" else b.shape[1]
    tm, tn, tk = _tile128(M, tm), _tile128(N, tn), _tile128(K, tk)
    pair_k = b_shards and mode == "nt"
    if pair_k:
        tk = 2 * K // N_DEV
        b = b.reshape(N_DEV // 2, 2, *b.shape[1:])
    elif b_shards or out_shards:
        tn = N // N_DEV
    nk = K // tk
    dims = {"nn": NN, "nt": NT, "tn": TN}[mode]
    a_spec = (pl.BlockSpec((tk, tm), lambda i, j, k: (k, i)) if mode == "tn"
              else pl.BlockSpec((tm, tk), lambda i, j, k: (i, k)))
    if b_shards:
        b_spec = (pl.BlockSpec((None, 2, tn, tk // 2), lambda i, j, k: (k, 0, j, 0)) if mode == "nt"
                  else pl.BlockSpec((None, tk, tn), lambda i, j, k: (j, k, 0)))
    else:
        b_spec = (pl.BlockSpec((tn, tk), lambda i, j, k: (j, k)) if mode == "nt"
                  else pl.BlockSpec((tk, tn), lambda i, j, k: (k, j)))
    mn_spec = pl.BlockSpec((tm, tn), lambda i, j, k: (i, j))
    out_spec = pl.BlockSpec((None, tm, tn), lambda i, j, k: (j, i, 0)) if out_shards else mn_spec
    out_dims = (N_DEV, M, N // N_DEV) if out_shards else (M, N)
    n_ex = len(extras)
    n_out = len(out_dtypes)

    def finish(acc, ex, outs):
        res = (acc,) if epilogue is None else epilogue(acc, *[e[...] for e in ex])
        for o, r in zip(outs, res):
            o[...] = r.astype(o.dtype)

    def product(a_ref, b_ref):
        if pair_k:
            return _dot(a_ref[:, :tk // 2], b_ref[0], NT) + _dot(a_ref[:, tk // 2:], b_ref[1], NT)
        return _dot(a_ref[...], b_ref[...], dims)

    def body(*refs):
        a_ref, b_ref = refs[:2]
        ex = refs[2:2 + n_ex]
        outs = refs[2 + n_ex:2 + n_ex + n_out]
        if nk == 1:
            finish(product(a_ref, b_ref), ex, outs)
            return
        acc = refs[-1]
        k = pl.program_id(2)

        @pl.when(k == 0)
        def _():
            acc[...] = product(a_ref, b_ref)

        @pl.when(jnp.logical_and(k > 0, k < nk - 1))
        def _():
            acc[...] += product(a_ref, b_ref)

        @pl.when(k == nk - 1)
        def _():
            finish(acc[...] + product(a_ref, b_ref), ex, outs)

    out = _call(
        body, name, (M // tm, N // tn, nk), [a_spec, b_spec] + [mn_spec] * n_ex, [out_spec] * n_out,
        [jax.ShapeDtypeStruct(out_dims, dt) for dt in out_dtypes], (a, b, *extras),
        scratch=[] if nk == 1 else [pltpu.VMEM((tm, tn), F32)], sem=("parallel", "parallel", "arbitrary"),
        comm=comm)
    return out[0] if n_out == 1 else out


def _rms_fwd(x, gain, out_dtype, name, width=None, col_block=0, residual=None, tb=256, comm=None):
    T = x.shape[0]
    W = x.shape[1] if width is None else width
    tb = _tile(T, tb)
    has_res = residual is not None

    def body(*refs):
        x_ref, g_ref = refs[:2]
        o_ref = refs[-1]
        xf = x_ref[...]
        y = xf * lax.rsqrt(jnp.mean(xf * xf, axis=-1, keepdims=True) + NORM_EPS) * g_ref[...]
        if has_res:
            y = refs[2][...] + y
        o_ref[...] = y.astype(o_ref.dtype)

    row = pl.BlockSpec((tb, W), lambda i: (i, 0))
    ins = [x, gain] + ([residual] if has_res else [])
    return _call(
        body, name, (T // tb,),
        [pl.BlockSpec((tb, W), lambda i: (i, col_block)),
         pl.BlockSpec((1, W), lambda i: (0, 0))] + ([row] if has_res else []),
        [row], [jax.ShapeDtypeStruct((T, W), out_dtype)], ins, sem=("parallel",), comm=comm)[0]


def _rms_bwd(dy, x, gain, out_dtype, name, width=None, col_block=0, residual=None, tb=256, comm=None, into=None):
    T = dy.shape[0]
    W = x.shape[1] if width is None else width
    tb = _tile(T, tb)
    has_res = residual is not None

    def body(*refs):
        dy_ref, x_ref, g_ref = refs[:3]
        dx_ref, dg_ref = refs[-2:]
        i = pl.program_id(0)
        xf = x_ref[...]
        r = lax.rsqrt(jnp.mean(xf * xf, axis=-1, keepdims=True) + NORM_EPS)
        xn = xf * r
        dyf = dy_ref[...].astype(F32)
        dyg = dyf * g_ref[...]
        dx = r * (dyg - xn * jnp.mean(dyg * xn, axis=-1, keepdims=True))
        if has_res:
            dx = refs[3][...] + dx
        dx_ref[...] = dx.astype(dx_ref.dtype)

        @pl.when(i == 0)
        def _():
            dg_ref[...] = jnp.zeros_like(dg_ref)

        dg_ref[...] += jnp.sum(dyf * xn, axis=0, keepdims=True)

    row = pl.BlockSpec((tb, W), lambda i: (i, 0))
    vec = pl.BlockSpec((1, W), lambda i: (0, 0))
    ins = [dy, x, gain] + ([residual] if has_res else [])
    in_specs = [row, pl.BlockSpec((tb, W), lambda i: (i, col_block)), vec] + ([row] if has_res else [])
    if into is None:
        out_spec, out_struct, aliases = row, jax.ShapeDtypeStruct((T, W), out_dtype), None
    else:
        buf, buf_block = into
        out_spec = pl.BlockSpec((tb, W), lambda i: (i, buf_block))
        out_struct, aliases = jax.ShapeDtypeStruct(buf.shape, buf.dtype), {len(ins): 0}
        ins, in_specs = ins + [buf], in_specs + [pl.BlockSpec(memory_space=pl.ANY)]
    return _call(
        body, name, (T // tb,), in_specs, [out_spec, vec], [out_struct, jax.ShapeDtypeStruct((1, W), F32)], ins,
        sem=("arbitrary",), comm=comm, aliases=aliases)


def _rms(xf):
    r = lax.rsqrt(jnp.mean(xf * xf, axis=-1, keepdims=True) + NORM_EPS)
    return r, xf * r


def _rms_grad(dyf, xn, r, gain):
    dyg = dyf * gain
    return r * (dyg - xn * jnp.mean(dyg * xn, axis=-1, keepdims=True)), dyf * xn


def _accumulate_rows(i, ref, rows):
    @pl.when(i == 0)
    def _():
        ref[...] = jnp.zeros_like(ref)

    ref[...] += jnp.sum(rows, axis=0, keepdims=True)


def _norm_pair_fwd(y1, xs, gain_post, gain_pre, name, tb=256):
    T, D = xs.shape
    tb = _tile(T, tb)

    def body(y1_ref, xs_ref, gp_ref, gq_ref, x2_ref, h2_ref):
        x2 = xs_ref[...] + _rms(y1_ref[...])[1] * gp_ref[...]
        x2_ref[...] = x2
        h2_ref[...] = (_rms(x2)[1] * gq_ref[...]).astype(BF16)

    row = pl.BlockSpec((tb, D), lambda i: (i, 0))
    vec = pl.BlockSpec((1, D), lambda i: (0, 0))
    return pl.pallas_call(
        body, name=name, grid=(T // tb,), in_specs=[row, row, vec, vec], out_specs=[row, row],
        out_shape=[jax.ShapeDtypeStruct((T, D), F32), jax.ShapeDtypeStruct((T, D), BF16)],
        compiler_params=_params("parallel"),
    )(y1, xs, gain_post, gain_pre)


def _norm_pair_bwd(dh2, x2, gain_pre, dx3, y1, gain_post, name, tb=256):
    T, D = x2.shape
    tb = _tile(T, tb)

    def body(dh2_ref, x2_ref, gq_ref, dx3_ref, y1_ref, gp_ref, dx2_ref, dy1_ref, dgq_ref, dgp_ref):
        i = pl.program_id(0)
        r2, xn2 = _rms(x2_ref[...])
        d2, rows_q = _rms_grad(dh2_ref[...], xn2, r2, gq_ref[...])
        dx2 = dx3_ref[...] + d2
        dx2_ref[...] = dx2
        r1, yn1 = _rms(y1_ref[...])
        d1, rows_p = _rms_grad(dx2, yn1, r1, gp_ref[...])
        dy1_ref[...] = d1.astype(BF16)
        _accumulate_rows(i, dgq_ref, rows_q)
        _accumulate_rows(i, dgp_ref, rows_p)

    row = pl.BlockSpec((tb, D), lambda i: (i, 0))
    vec = pl.BlockSpec((1, D), lambda i: (0, 0))
    return pl.pallas_call(
        body, name=name, grid=(T // tb,), in_specs=[row, row, vec, row, row, vec], out_specs=[row, row, vec, vec],
        out_shape=[jax.ShapeDtypeStruct((T, D), F32), jax.ShapeDtypeStruct((T, D), BF16),
                   jax.ShapeDtypeStruct((1, D), F32), jax.ShapeDtypeStruct((1, D), F32)],
        compiler_params=_params("arbitrary"),
    )(dh2, x2, gain_pre, dx3, y1, gain_post)


def _loss_head(x2, y2, gain, target, name, tb=256):
    T, D = x2.shape
    tb = _tile(T, tb)

    def body(x2_ref, y2_ref, g_ref, t_ref, dx3_ref, dy2_ref, loss_ref, dg_ref):
        i = pl.program_id(0)
        r, yn = _rms(y2_ref[...])
        e = x2_ref[...] + yn * g_ref[...] - t_ref[...]
        dx3 = e * (1.0 / D)
        dx3_ref[...] = dx3
        dy2, rows = _rms_grad(dx3, yn, r, g_ref[...])
        dy2_ref[...] = dy2.astype(BF16)
        _accumulate_rows(i, dg_ref, rows)
        _accumulate_rows(i, loss_ref, 0.5 * jnp.mean(e * e, axis=-1, keepdims=True))

    row = pl.BlockSpec((tb, D), lambda i: (i, 0))
    vec = pl.BlockSpec((1, D), lambda i: (0, 0))
    return pl.pallas_call(
        body, name=name, grid=(T // tb,),
        in_specs=[row, row, vec, row],
        out_specs=[row, row, pl.BlockSpec((1, 1), lambda i: (0, 0)), vec],
        out_shape=[jax.ShapeDtypeStruct((T, D), F32), jax.ShapeDtypeStruct((T, D), BF16),
                   jax.ShapeDtypeStruct((1, 1), F32), jax.ShapeDtypeStruct((1, D), F32)],
        compiler_params=_params("arbitrary"),
    )(x2, y2, gain, target)


def _rope_tables(positions, rot_dim):
    half = rot_dim // 2
    inv_freq = ROPE_THETA ** (-jnp.arange(0, rot_dim, 2, dtype=F32) / rot_dim)
    ang = positions.astype(F32)[:, None] * inv_freq[None, :]
    cos, sin = jnp.cos(ang), jnp.sin(ang)
    T = positions.shape[0]
    ones = jnp.ones((T, HD - rot_dim), F32)
    cos_t = jnp.concatenate([cos, cos, ones], axis=1)
    sin_t = jnp.concatenate([-sin, sin, jnp.zeros_like(ones)], axis=1)
    return cos_t, sin_t


def _rotate(x, cos_t, sin_t, half):
    lane = lax.broadcasted_iota(jnp.int32, x.shape, 1)
    swapped = jnp.where(lane < half, pltpu.roll(x, HD - half, 1), pltpu.roll(x, half, 1))
    return x * cos_t + swapped * sin_t


def _rope_apply(x, cos_t, sin_t, half, n_blocks, is_rope, out_dtype, name, window=0, tb=256):
    T = x.shape[0]
    tb = _tile(T, tb)
    W = n_blocks * HD

    def body(x_ref, c_ref, s_ref, o_ref):
        for j in range(n_blocks):
            sl = slice(j * HD, (j + 1) * HD)
            xj = x_ref[:, sl]
            if is_rope(j):
                xj = _rotate(xj.astype(F32), c_ref[...], s_ref[...], half)
            o_ref[:, sl] = xj.astype(o_ref.dtype)

    tab = pl.BlockSpec((tb, HD), lambda i: (i, 0))
    return pl.pallas_call(
        body, name=name, grid=(T // tb,),
        in_specs=[pl.BlockSpec((tb, W), lambda i: (i, window)), tab, tab],
        out_specs=pl.BlockSpec((tb, W), lambda i: (i, 0)),
        out_shape=jax.ShapeDtypeStruct((T, W), out_dtype),
        compiler_params=_params("parallel"),
    )(x, cos_t, sin_t)


DSWA_TB = 2048


def _deinterleave(src, dst_ref, d, dtype):
    rows = src.shape[0] // d
    for r in range(d):
        dst_ref[r] = src[pl.ds(r, rows, stride=d), :].astype(dtype)


def _rope_dswa(proj, cos_t, sin_t, name, comm=None):
    T = proj.shape[0]
    tb = _tile(T, 2 * DSWA_TB)
    half = ROT_A // 2

    def body(x_ref, c_ref, s_ref, *rest):
        outs, scr = rest[:-1], rest[-1]
        j = pl.program_id(1)

        @pl.when(j < 2 * HEADS)
        def _():
            scr[...] = _rotate(x_ref[...], c_ref[...], s_ref[...], half)

        @pl.when(j >= 2 * HEADS)
        def _():
            scr[...] = x_ref[...]

        for o_ref, d in zip(outs, DSWA_DILATIONS):
            _deinterleave(scr, o_ref, d, BF16)

    blk = pl.BlockSpec((tb, HD), lambda i, j: (i, j))
    tab = pl.BlockSpec((tb, HD), lambda i, j: (i, 0))
    return _call(
        body, name, (T // tb, 3 * HEADS), [blk, tab, tab],
        [pl.BlockSpec((d, tb // d, HD), lambda i, j: (0, i, j)) for d in DSWA_DILATIONS],
        [jax.ShapeDtypeStruct((d, T // d, 3 * AW), BF16) for d in DSWA_DILATIONS], (proj, cos_t, sin_t),
        scratch=[pltpu.VMEM((tb, HD), F32)], sem=("parallel", "parallel"), comm=comm)


def _shared_key_grad(dk, cos_t, sin_t_neg, half, name, into, tb=512):
    T = dk.shape[0]
    tb = _tile(T, tb)
    buf, buf_block = into

    def body(d_ref, c_ref, s_ref, buf_ref, o_ref):
        tot = d_ref[:, HD:2 * HD]
        for h in range(1, HEADS):
            tot = tot + d_ref[:, h * QK + HD:(h + 1) * QK]
        o_ref[...] = _rotate(tot, c_ref[...], s_ref[...], half).astype(o_ref.dtype)

    tab = pl.BlockSpec((tb, HD), lambda i: (i, 0))
    return pl.pallas_call(
        body, name=name, grid=(T // tb,),
        in_specs=[pl.BlockSpec((tb, HEADS * QK), lambda i: (i, 0)), tab, tab, pl.BlockSpec(memory_space=pl.ANY)],
        out_specs=pl.BlockSpec((tb, HD), lambda i: (i, buf_block)),
        out_shape=jax.ShapeDtypeStruct(buf.shape, buf.dtype), input_output_aliases={3: 0},
        compiler_params=_params("parallel"),
    )(dk, cos_t, sin_t_neg, buf)


def _band_mask(n):
    row = lax.broadcasted_iota(jnp.int32, (QBLK, 2 * QBLK), 0)
    col = lax.broadcasted_iota(jnp.int32, (QBLK, 2 * QBLK), 1)
    in_prev = jnp.logical_and(jnp.logical_and(col < QBLK, col >= row), n > 0)
    in_cur = jnp.logical_and(col >= QBLK, col - QBLK <= row)
    return jnp.logical_or(in_prev, in_cur)


def _dswa_specs(nb, reverse=False):
    pos = (lambda n: nb - 1 - n) if reverse else (lambda n: n)
    cur = lambda c: pl.BlockSpec((None, QBLK, AW), lambda r, n: (r, pos(n), c))
    prev = lambda c: pl.BlockSpec((None, QBLK, AW), lambda r, n: (r, jnp.maximum(pos(n) - 1, 0), c))
    stat = pl.BlockSpec((None, QBLK, HD), lambda r, n: (r, pos(n), 0))
    return cur, prev, stat


def _relayout_spec(d, tb, per_head=True):
    if per_head:
        return pl.BlockSpec((d, tb // d, HD), lambda i, h: (0, i, h))
    return pl.BlockSpec((d, tb // d, HD), lambda i, h: (0, i, 0))


def _head_lane(x, h):
    lane = lax.broadcasted_iota(jnp.int32, x.shape, 1)
    return jnp.sum(jnp.where(lane == h, x, 0.0), axis=-1, keepdims=True)


def _dswa_fwd(qkv, name, comm=None):
    d, sd = qkv.shape[:2]
    nb = sd // QBLK

    def body(q_ref, kc_ref, kp_ref, vc_ref, vp_ref, o_ref, l_ref):
        mask = _band_mask(pl.program_id(1))
        l_ref[...] = jnp.zeros_like(l_ref)
        for h in range(HEADS):
            sl = slice(h * HD, (h + 1) * HD)
            keys = jnp.concatenate([kp_ref[:, sl], kc_ref[:, sl]], axis=0)
            vals = jnp.concatenate([vp_ref[:, sl], vc_ref[:, sl]], axis=0)
            s = jnp.where(mask, _dot(q_ref[:, sl], keys, NT) * SCALE_A, NEG_INF)
            m = jnp.max(s, axis=-1, keepdims=True)
            p = jnp.exp(s - m)
            den = jnp.sum(p, axis=-1, keepdims=True)
            o_ref[:, sl] = _dot((p / den).astype(BF16), vals, NN)
            l_ref[:, h:h + 1] = m + jnp.log(den)

    cur, prev, stat = _dswa_specs(nb)
    return _call(
        body, name, (d, nb), [cur(0), cur(1), prev(1), cur(2), prev(2)], [cur(0), stat],
        [jax.ShapeDtypeStruct((d, sd, AW), F32), jax.ShapeDtypeStruct((d, sd, HD), F32)],
        (qkv, qkv, qkv, qkv, qkv), sem=("parallel", "parallel"), comm=comm)


def _dswa_merge(outs, lses, name, comm=None):
    nc = len(DSWA_DILATIONS)
    T = outs[0].shape[0] * outs[0].shape[1]
    tb = _tile(T, DSWA_TB)

    def body(*refs):
        o_refs, l_refs = refs[:nc], refs[nc:2 * nc]
        out_ref, outb_ref = refs[2 * nc:2 * nc + 2]
        lt_refs = refs[2 * nc + 2:3 * nc + 2]
        o_nat, l_nat, lt_nat = refs[3 * nc + 2:4 * nc + 2], refs[4 * nc + 2:5 * nc + 2], refs[-1]
        h = pl.program_id(1)
        for c, d in enumerate(DSWA_DILATIONS):
            for r in range(d):
                o_nat[c][pl.ds(r, tb // d, stride=d), :] = o_refs[c][r]
                l_nat[c][pl.ds(r, tb // d, stride=d), :] = l_refs[c][r]
        ls = [l[...] for l in l_nat]
        m = functools.reduce(jnp.maximum, ls)
        es = [jnp.exp(l - m) for l in ls]
        tot = functools.reduce(lambda a, b: a + b, es)
        acc = _head_lane(es[0] / tot, h) * o_nat[0][...]
        for c in range(1, nc):
            acc = acc + _head_lane(es[c] / tot, h) * o_nat[c][...]
        out_ref[...] = acc
        outb_ref[...] = acc.astype(BF16)
        lt_nat[...] = m + jnp.log(tot)
        for c, d in enumerate(DSWA_DILATIONS):
            _deinterleave(lt_nat, lt_refs[c], d, F32)

    nat = pl.BlockSpec((tb, HD), lambda i, h: (i, h))
    by_d = [_relayout_spec(d, tb) for d in DSWA_DILATIONS]
    stat_by_d = [_relayout_spec(d, tb, per_head=False) for d in DSWA_DILATIONS]
    res = _call(
        body, name, (T // tb, HEADS), by_d + stat_by_d, [nat, nat] + stat_by_d,
        [jax.ShapeDtypeStruct((T, AW), F32), jax.ShapeDtypeStruct((T, AW), BF16)]
        + [jax.ShapeDtypeStruct((d, T // d, HD), F32) for d in DSWA_DILATIONS], (*outs, *lses),
        scratch=[pltpu.VMEM((tb, HD), F32)] * (2 * nc + 1), sem=("parallel", "arbitrary"), comm=comm)
    return res[0], res[1], res[2:]


def _dswa_delta(dout, out, name):
    nc = len(DSWA_DILATIONS)
    T = out.shape[0]
    tb = _tile(T, DSWA_TB)

    def body(do_ref, o_ref, *rest):
        dl_refs, dob_refs, dl_nat = rest[:nc], rest[nc:2 * nc], rest[-1]
        h = pl.program_id(1)
        lane = lax.broadcasted_iota(jnp.int32, (tb, HD), 1)
        mine = jnp.where(lane == h, jnp.sum(do_ref[...] * o_ref[...], axis=-1, keepdims=True), 0.0)

        @pl.when(h == 0)
        def _():
            dl_nat[...] = mine

        @pl.when(h > 0)
        def _():
            dl_nat[...] += mine

        for c, d in enumerate(DSWA_DILATIONS):
            _deinterleave(dl_nat, dl_refs[c], d, F32)
            _deinterleave(do_ref, dob_refs[c], d, BF16)

    nat = pl.BlockSpec((tb, HD), lambda i, h: (i, h))
    by_d = [_relayout_spec(d, tb) for d in DSWA_DILATIONS]
    stat_by_d = [_relayout_spec(d, tb, per_head=False) for d in DSWA_DILATIONS]
    res = pl.pallas_call(
        body, name=name, grid=(T // tb, HEADS),
        in_specs=[nat, nat], out_specs=stat_by_d + by_d,
        out_shape=[jax.ShapeDtypeStruct((d, T // d, HD), F32) for d in DSWA_DILATIONS]
        + [jax.ShapeDtypeStruct((d, T // d, AW), BF16) for d in DSWA_DILATIONS],
        scratch_shapes=[pltpu.VMEM((tb, HD), F32)],
        compiler_params=_params("parallel", "arbitrary"),
    )(dout, out)
    return res[:nc], res[nc:]


def _delta_prep(dout, col_block, out, name, tb=256):
    T = out.shape[0]
    tb = _tile(T, tb)

    def body(do_ref, o_ref, delta_ref, dob_ref):
        delta_ref[...] = jnp.zeros_like(delta_ref)
        for h in range(HEADS):
            sl = slice(h * HD, (h + 1) * HD)
            doh = do_ref[:, sl]
            delta_ref[:, h:h + 1] = jnp.sum(doh * o_ref[:, sl], axis=-1, keepdims=True)
            dob_ref[:, sl] = doh.astype(BF16)

    row = pl.BlockSpec((tb, AW), lambda i: (i, 0))
    return pl.pallas_call(
        body, name=name, grid=(T // tb,),
        in_specs=[pl.BlockSpec((tb, AW), lambda i: (i, col_block)), row],
        out_specs=[pl.BlockSpec((tb, HD), lambda i: (i, 0)), row],
        out_shape=[jax.ShapeDtypeStruct((T, HD), F32), jax.ShapeDtypeStruct((T, AW), BF16)],
        compiler_params=_params("parallel"),
    )(dout, out)


def _dswa_bwd(qkv, dout_b, lse_tot, delta, name, comm=None):
    d, sd = qkv.shape[:2]
    nb = sd // QBLK

    def body(q_ref, kc_ref, kp_ref, vc_ref, vp_ref, do_ref, l_ref, dl_ref, g_ref, carry_k, carry_v):
        mask = _band_mask(nb - 1 - pl.program_id(1))

        @pl.when(pl.program_id(1) == 0)
        def _():
            carry_k[...] = jnp.zeros_like(carry_k)
            carry_v[...] = jnp.zeros_like(carry_v)

        for h in range(HEADS):
            sl = slice(h * HD, (h + 1) * HD)
            qh, doh = q_ref[:, sl], do_ref[:, sl]
            keys = jnp.concatenate([kp_ref[:, sl], kc_ref[:, sl]], axis=0)
            vals = jnp.concatenate([vp_ref[:, sl], vc_ref[:, sl]], axis=0)
            s = jnp.where(mask, _dot(qh, keys, NT) * SCALE_A, NEG_INF)
            p = jnp.exp(s - l_ref[:, h:h + 1])
            ds = (p * (_dot(doh, vals, NT) - dl_ref[:, h:h + 1]) * SCALE_A).astype(BF16)
            g_ref[:, sl] = _dot(ds, keys, NN).astype(BF16)
            dk = _dot(ds, qh, TN)
            dv = _dot(p.astype(BF16), doh, TN)
            g_ref[:, AW + h * HD:AW + (h + 1) * HD] = (dk[QBLK:] + carry_k[:, sl]).astype(BF16)
            g_ref[:, 2 * AW + h * HD:2 * AW + (h + 1) * HD] = (dv[QBLK:] + carry_v[:, sl]).astype(BF16)
            carry_k[:, sl] = dk[:QBLK]
            carry_v[:, sl] = dv[:QBLK]

    cur, prev, stat = _dswa_specs(nb, reverse=True)
    out_spec = pl.BlockSpec((None, QBLK, 3 * AW), lambda r, n: (r, nb - 1 - n, 0))
    return _call(
        body, name, (d, nb), [cur(0), cur(1), prev(1), cur(2), prev(2), cur(0), stat, stat], [out_spec],
        [jax.ShapeDtypeStruct((d, sd, 3 * AW), BF16)], (qkv, qkv, qkv, qkv, qkv, dout_b, lse_tot, delta),
        scratch=[pltpu.VMEM((QBLK, AW), F32)] * 2, sem=("parallel", "arbitrary"), comm=comm)[0]


def _dswa_combine(grads, cos_t, sin_t_neg, name, width):
    T = grads[0].shape[0] * grads[0].shape[1]
    tb = _tile(T, DSWA_TB)
    half = ROT_A // 2

    def body(*refs):
        g_refs = refs[:len(grads)]
        c_ref, s_ref, out_ref, acc = refs[len(grads):]
        j = pl.program_id(1)
        for c, d in enumerate(DSWA_DILATIONS):
            for r in range(d):
                if c == 0:
                    acc[...] = g_refs[c][r].astype(F32)
                else:
                    acc[pl.ds(r, tb // d, stride=d), :] += g_refs[c][r].astype(F32)
        val = acc[...]
        out_ref[...] = jnp.where(j < 2 * HEADS, _rotate(val, c_ref[...], s_ref[...], half), val).astype(BF16)

    tab = pl.BlockSpec((tb, HD), lambda i, j: (i, 0))
    return pl.pallas_call(
        body, name=name, grid=(T // tb, 3 * HEADS),
        in_specs=[pl.BlockSpec((d, tb // d, HD), lambda i, j: (0, i, j)) for d in DSWA_DILATIONS] + [tab, tab],
        out_specs=pl.BlockSpec((tb, HD), lambda i, j: (i, j)),
        out_shape=jax.ShapeDtypeStruct((T, width), BF16),
        scratch_shapes=[pltpu.VMEM((tb, HD), F32)],
        compiler_params=_params("parallel", "parallel"),
    )(*grads, cos_t, sin_t_neg)


MLA_TQ = 512
QK = 2 * HD
LOG2E = 1.4426950408889634


def _triangle(nq, key_major):
    pairs = [(q, k) for q in range(nq) for k in range(q + 1)]
    if key_major:
        pairs.sort(key=lambda p: (p[1], p[0]))
    return (jnp.array([p[0] for p in pairs], jnp.int32), jnp.array([p[1] for p in pairs], jnp.int32))


def _mla_specs(tq):
    q_spec = pl.BlockSpec((tq, HEADS * QK), lambda t, qi, ki: (qi[t], 0))
    k_spec = pl.BlockSpec((tq, HEADS * QK), lambda t, qi, ki: (ki[t], 0))
    v_spec = pl.BlockSpec((tq, AW), lambda t, qi, ki: (ki[t], 1))
    qrow = pl.BlockSpec((tq, AW), lambda t, qi, ki: (qi[t], 0))
    krow = pl.BlockSpec((tq, AW), lambda t, qi, ki: (ki[t], 0))
    return q_spec, k_spec, v_spec, qrow, krow


def _mla_pack(kv, kr, name, tb=512):
    T = kv.shape[0]
    tb = _tile(T, tb)

    def body(kv_ref, kr_ref, k_ref, v1_ref):
        lane = lax.broadcasted_iota(jnp.int32, (tb, HD), 1)
        one_hot = jnp.where(lane == 0, 1.0, 0.0).astype(BF16)
        for h in range(HEADS):
            k_ref[:, h * QK:h * QK + HD] = kv_ref[:, h * HD:(h + 1) * HD]
            k_ref[:, h * QK + HD:(h + 1) * QK] = kr_ref[...]
            v1_ref[:, h * QK:h * QK + HD] = kv_ref[:, AW + h * HD:AW + (h + 1) * HD]
            v1_ref[:, h * QK + HD:(h + 1) * QK] = one_hot

    wide = pl.BlockSpec((tb, HEADS * QK), lambda i: (i, 0))
    return pl.pallas_call(
        body, name=name, grid=(T // tb,),
        in_specs=[pl.BlockSpec((tb, 2 * AW), lambda i: (i, 0)), pl.BlockSpec((tb, HD), lambda i: (i, 0))],
        out_specs=[wide, wide], out_shape=[jax.ShapeDtypeStruct((T, HEADS * QK), BF16)] * 2,
        compiler_params=_params("parallel"),
    )(kv, kr)


def _mla_stat_spec(tq):
    return pl.BlockSpec((tq, HD), lambda t, qi, ki: (qi[t], 0))


def _mla_scores(q_ref, k_ref, h, qi, ki, tq):
    s = _dot(q_ref[:, h * QK:(h + 1) * QK], k_ref[:, h * QK:(h + 1) * QK], NT) * SCALE_B
    row = lax.broadcasted_iota(jnp.int32, s.shape, 0) + qi * tq
    col = lax.broadcasted_iota(jnp.int32, s.shape, 1) + ki * tq
    return jnp.where(col <= row, s, NEG_INF)


def _mla_fwd(q, k, v1, name, comm=None):
    T = q.shape[0]
    tq = _tile(T, MLA_TQ)
    tables = _triangle(T // tq, False)

    def body(qi_ref, ki_ref, q_ref, k_ref, v_ref, o_ref, ob_ref, l_ref, m_s, acc):
        t = pl.program_id(0)
        qi, ki = qi_ref[t], ki_ref[t]

        @pl.when(ki == 0)
        def _():
            m_s[...] = jnp.full_like(m_s, NEG_INF)
            acc[...] = jnp.zeros_like(acc)

        row = lax.broadcasted_iota(jnp.int32, (tq, tq), 0) + qi * tq
        col = lax.broadcasted_iota(jnp.int32, (tq, tq), 1) + ki * tq
        bias = jnp.where(col <= row, 0.0, NEG_INF)
        updates = []
        for h in range(HEADS):
            s = _dot(q_ref[:, h * QK:(h + 1) * QK], k_ref[:, h * QK:(h + 1) * QK], NT) + bias
            m_new = jnp.maximum(m_s[h], jnp.max(s, axis=-1, keepdims=True))
            p = jnp.exp2((s - m_new) * (SCALE_B * LOG2E)).astype(BF16)
            alpha = jnp.exp2((m_s[h] - m_new) * (SCALE_B * LOG2E))
            updates.append((m_new, alpha, _dot(p, v_ref[:, h * QK:(h + 1) * QK], NN)))
        for h, (m_new, alpha, pv) in enumerate(updates):
            acc[:, h * QK:(h + 1) * QK] = alpha * acc[:, h * QK:(h + 1) * QK] + pv
            m_s[h] = m_new

        @pl.when(ki == qi)
        def _():
            l_ref[...] = jnp.zeros_like(l_ref)
            for h in range(HEADS):
                sl = slice(h * HD, (h + 1) * HD)
                den = acc[:, h * QK + HD:h * QK + HD + 1]
                out = acc[:, h * QK:h * QK + HD] / den
                o_ref[:, sl] = out
                ob_ref[:, sl] = out.astype(BF16)
                l_ref[:, h:h + 1] = m_s[h] * SCALE_B + jnp.log(den)

    q_spec, k_spec, _, qrow, _ = _mla_specs(tq)
    return _call(
        body, name, (tables[0].shape[0],), [q_spec, k_spec, k_spec], [qrow, qrow, _mla_stat_spec(tq)],
        [jax.ShapeDtypeStruct((T, AW), F32), jax.ShapeDtypeStruct((T, AW), BF16), jax.ShapeDtypeStruct((T, HD), F32)],
        (q, k, v1),
        scratch=[pltpu.VMEM((HEADS, tq, 1), F32), pltpu.VMEM((tq, HEADS * QK), F32)],
        sem=("arbitrary",), comm=comm, prefetch=tables)


def _mla_ds(q_ref, k_ref, v_ref, do_ref, l_ref, dl_ref, h, qi, ki, tq):
    sl = slice(h * HD, (h + 1) * HD)
    p = jnp.exp(_mla_scores(q_ref, k_ref, h, qi, ki, tq) - l_ref[:, h:h + 1])
    ds = (p * (_dot(do_ref[:, sl], v_ref[:, sl], NT) - dl_ref[:, h:h + 1]) * SCALE_B).astype(BF16)
    return p, ds


def _mla_bwd_q(q, k, kv, dout_b, lse, delta, name, comm=None):
    T = q.shape[0]
    tq = _tile(T, MLA_TQ)
    tables = _triangle(T // tq, False)

    def body(qi_ref, ki_ref, q_ref, k_ref, v_ref, do_ref, l_ref, dl_ref, dq_ref):
        t = pl.program_id(0)
        qi, ki = qi_ref[t], ki_ref[t]

        @pl.when(ki == 0)
        def _():
            dq_ref[...] = jnp.zeros_like(dq_ref)

        for h in range(HEADS):
            _, ds = _mla_ds(q_ref, k_ref, v_ref, do_ref, l_ref, dl_ref, h, qi, ki, tq)
            dq_ref[:, h * QK:(h + 1) * QK] += _dot(ds, k_ref[:, h * QK:(h + 1) * QK], NN)

    q_spec, k_spec, v_spec, qrow, _ = _mla_specs(tq)
    return _call(
        body, name, (tables[0].shape[0],), [q_spec, k_spec, v_spec, qrow, _mla_stat_spec(tq), _mla_stat_spec(tq)], [q_spec],
        [jax.ShapeDtypeStruct((T, HEADS * QK), F32)], (q, k, kv, dout_b, lse, delta),
        sem=("arbitrary",), comm=comm, prefetch=tables)[0]


def _mla_bwd_kv(q, k, kv, dout_b, lse, delta, name, comm=None):
    T = q.shape[0]
    tq = _tile(T, MLA_TQ)
    nq = T // tq
    tables = _triangle(nq, True)

    def body(qi_ref, ki_ref, q_ref, k_ref, v_ref, do_ref, l_ref, dl_ref, dk_ref, dkv_ref, dv_acc):
        t = pl.program_id(0)
        qi, ki = qi_ref[t], ki_ref[t]

        @pl.when(qi == ki)
        def _():
            dk_ref[...] = jnp.zeros_like(dk_ref)
            dv_acc[...] = jnp.zeros_like(dv_acc)

        for h in range(HEADS):
            sl = slice(h * HD, (h + 1) * HD)
            p, ds = _mla_ds(q_ref, k_ref, v_ref, do_ref, l_ref, dl_ref, h, qi, ki, tq)
            dv_acc[:, sl] += _dot(p.astype(BF16), do_ref[:, sl], TN)
            dk_ref[:, h * QK:(h + 1) * QK] += _dot(ds, q_ref[:, h * QK:(h + 1) * QK], TN)

        @pl.when(qi == nq - 1)
        def _():
            for h in range(HEADS):
                dkv_ref[:, h * HD:(h + 1) * HD] = dk_ref[:, h * QK:h * QK + HD].astype(BF16)
                dkv_ref[:, AW + h * HD:AW + (h + 1) * HD] = dv_acc[:, h * HD:(h + 1) * HD].astype(BF16)

    q_spec, k_spec, v_spec, qrow, _ = _mla_specs(tq)
    return _call(
        body, name, (tables[0].shape[0],),
        [q_spec, k_spec, v_spec, qrow, _mla_stat_spec(tq), _mla_stat_spec(tq)],
        [k_spec, pl.BlockSpec((tq, 2 * AW), lambda t, qi, ki: (ki[t], 0))],
        [jax.ShapeDtypeStruct((T, HEADS * QK), F32), jax.ShapeDtypeStruct((T, 2 * AW), BF16)],
        (q, k, kv, dout_b, lse, delta), scratch=[pltpu.VMEM((tq, AW), F32)],
        sem=("arbitrary",), comm=comm, prefetch=tables)


def _pair_sum(by_device, from_sibling, name, tb=256):
    n_chip, R, C = from_sibling.shape
    tb = _tile(R, tb)
    core = jnp.reshape(lax.axis_index("c"), (1,)).astype(jnp.int32)

    def body(core_ref, mine_ref, theirs_ref, o_ref):
        o_ref[...] = (mine_ref[...].astype(F32) + theirs_ref[...].astype(F32)).astype(o_ref.dtype)

    blk = pl.BlockSpec((None, tb, C), lambda p, i, core_ref: (p, i, 0))
    return _call(
        body, name, (n_chip, R // tb),
        [pl.BlockSpec((None, tb, C), lambda p, i, core_ref: (2 * p + core_ref[0], i, 0)), blk], [blk],
        [jax.ShapeDtypeStruct((n_chip, R, C), BF16)], (by_device, from_sibling),
        sem=("parallel", "parallel"), prefetch=(core,))[0]


def _adamw(parts, w, m, v, name, tb=128, comm=None):
    R, C = w.shape
    n_parts = parts.shape[0]
    tb = _tile(R, tb)
    c1 = 1.0 - ADAM_B1
    c2 = 1.0 - ADAM_B2
    bc1 = 1.0 - ADAM_B1 ** ADAM_STEP
    bc2 = 1.0 - ADAM_B2 ** ADAM_STEP

    def body(p_ref, w_ref, m_ref, v_ref, g_ref, d_ref, nm_ref, nv_ref):
        g = p_ref[0].astype(F32)
        for j in range(1, n_parts):
            g = g + p_ref[j].astype(F32)
        nm = ADAM_B1 * m_ref[...] + c1 * g
        nv = ADAM_B2 * v_ref[...] + c2 * (g * g)
        g_ref[...] = g
        nm_ref[...] = nm
        nv_ref[...] = nv
        d_ref[...] = -ADAM_LR * ((nm / bc1) / (jnp.sqrt(nv / bc2) + ADAM_EPS) + ADAM_WD * w_ref[...])

    row = pl.BlockSpec((tb, C), lambda i: (i, 0))
    return _call(
        body, name, (R // tb,), [pl.BlockSpec((n_parts, tb, C), lambda i: (0, i, 0)), row, row, row], [row] * 4,
        [jax.ShapeDtypeStruct((R, C), F32)] * 4, (parts, w, m, v), sem=("parallel",), comm=comm)


def _cols_from_shards(g):
    return jnp.transpose(g, (1, 0, 2)).reshape(g.shape[1], N_DEV * g.shape[2])


def _cols_to_shards(w):
    return jnp.transpose(w.reshape(w.shape[0], N_DEV, w.shape[1] // N_DEV), (1, 0, 2))


def _split_heads(w, first):
    w3 = w.reshape(w.shape[0], HEADS, -1)
    return w3[:, :, :first].reshape(w.shape[0], -1), w3[:, :, first:].reshape(w.shape[0], -1)


def _join_heads(a, b):
    R = a.shape[0]
    return jnp.concatenate([a.reshape(R, HEADS, -1), b.reshape(R, HEADS, -1)], axis=2).reshape(R, -1)


def _pad_heads(w, width):
    w3 = w.reshape(w.shape[0], HEADS, -1)
    return jnp.pad(w3, ((0, 0), (0, 0), (0, width - w3.shape[2]))).reshape(w.shape[0], HEADS * width)


def _unpad_heads(w, k):
    return w.reshape(w.shape[0], HEADS, -1)[:, :, :k].reshape(w.shape[0], HEADS * k)


def kernel(x, positions, norm_attn_pre, norm_attn_post, w_in, q_latent_norm, kv_latent_norm, w_uq, w_ukv, w_out, norm_mlp_pre, norm_mlp_post, w_up, w_down, loss_target, m_norm_attn_pre, m_norm_attn_post, m_w_in, m_q_latent_norm, m_kv_latent_norm, m_w_uq, m_w_ukv, m_w_out, m_norm_mlp_pre, m_norm_mlp_post, m_w_up, m_w_down, v_norm_attn_pre, v_norm_attn_post, v_w_in, v_q_latent_norm, v_kv_latent_norm, v_w_uq, v_w_ukv, v_w_out, v_norm_mlp_pre, v_norm_mlp_post, v_w_up, v_w_down):
    xs = x[0]
    tgt = loss_target[0]
    pos = positions[0]
    T, D = xs.shape
    big = dict(w_in=(w_in, m_w_in, v_w_in), w_uq=(w_uq, m_w_uq, v_w_uq), w_ukv=(w_ukv, m_w_ukv, v_w_ukv),
               w_out=(w_out, m_w_out, v_w_out), w_up=(w_up, m_w_up, v_w_up), w_down=(w_down, m_w_down, v_w_down))
    big = {n: tuple(t[0] for t in ts) for n, ts in big.items()}
    big_names = ["w_in", "w_uq", "w_ukv", "w_out", "w_up", "w_down"]
    col_sharded = {"w_in", "w_uq", "w_ukv", "w_up"}

    wb = {n: big[n][0].astype(BF16) for n in big_names}

    def gathered(ex, i, n):
        g = ex.results[i]
        return _cols_from_shards(g) if n in col_sharded else g.reshape(-1, g.shape[2])

    def by_device(g, n):
        return _cols_to_shards(g) if n in col_sharded else g.reshape(N_DEV, g.shape[0] // N_DEV, g.shape[1])

    def scatter_of(g, n):
        return _Exchange([by_device(g, n)], "scatter")

    cos_a, sin_a = _rope_tables(pos, ROT_A)
    cos_b, sin_b = _rope_tables(pos, ROPE_MLA)

    ex_in = _Exchange([wb["w_in"]], "gather")
    h1 = _rms_fwd(xs, norm_attn_pre, BF16, "norm_attn_pre_fwd", comm=ex_in)
    Wi = jnp.pad(gathered(ex_in, 0, "w_in"), ((0, 0), (0, IN_PAD - IN_COLS)))
    ex_mid = _Exchange([wb["w_uq"], wb["w_ukv"], wb["w_out"]], "gather")
    proj = _mm(h1, Wi, "nn", [F32], "proj_in", tn=1408, comm=ex_mid)
    Wuq = _pad_heads(gathered(ex_mid, 0, "w_uq"), QK)
    Wukv = jnp.concatenate(_split_heads(gathered(ex_mid, 1, "w_ukv"), HD), axis=1)
    Wo = gathered(ex_mid, 2, "w_out")
    n_piece = wb["w_down"].shape[0] // 8
    ex_down = [_Exchange([wb["w_down"][i * n_piece:(i + 1) * n_piece]], "gather") for i in range(8)]
    qkv_by_d = _rope_dswa(proj, cos_a, sin_a, "rope_dswa", comm=ex_down[0])
    outs, lses = [], []
    for d, qkv, ex in zip(DSWA_DILATIONS, qkv_by_d, ex_down[1:4]):
        o, l = _dswa_fwd(qkv, f"dswa_fwd_d{d}", comm=ex)
        outs.append(o)
        lses.append(l)
    a_out, a_out_b, a_lse_by_d = _dswa_merge(outs, lses, "dswa_merge", comm=ex_down[4])

    cqn = _rms_fwd(proj, q_latent_norm, BF16, "q_latent_norm_fwd", width=Q_LORA, col_block=3 * AW // Q_LORA)
    ckvn = _rms_fwd(proj, kv_latent_norm, BF16, "kv_latent_norm_fwd", width=KV_LORA, col_block=3 * AW // KV_LORA + 1)
    qb = _mm(cqn, Wuq, "nn", [F32], "q_up")
    kvb = _mm(ckvn, Wukv, "nn", [BF16], "kv_up")
    odd = lambda j: j % 2 == 1
    q_mla = _rope_apply(qb, cos_b, sin_b, ROPE_MLA // 2, 2 * HEADS, odd, BF16, "rope_mla_q")
    kr = _rope_apply(proj, cos_b, sin_b, ROPE_MLA // 2, 1, lambda j: True, BF16, "rope_mla_k",
                     window=(IN_PAD - HD) // HD)
    k_mla, v1_mla = _mla_pack(kvb, kr, "mla_pack")
    ex_up = _Exchange([wb["w_up"]], "gather")
    b_out, b_out_b, b_lse = _mla_fwd(q_mla, k_mla, v1_mla, "mla_fwd", comm=ex_up)
    Wup_shards = ex_up.results[0]

    mixed = jnp.concatenate([a_out_b, b_out_b], axis=1)
    y1 = _mm(mixed, Wo, "nn", [F32], "attn_out", comm=ex_down[5])

    x2, h2 = _norm_pair_fwd(y1, xs, norm_attn_post, norm_mlp_pre, "norm_attn_post_mlp_pre_fwd")

    def relu2(z):
        r = jnp.maximum(z, 0.0)
        return r * r, r

    u, zr = _mm(h2, Wup_shards, "nn", [BF16, BF16], "mlp_up", epilogue=relu2, comm=ex_down[6:8],
                b_shards=True)
    Wdn = jnp.concatenate([ex.results[0] for ex in ex_down], axis=1).reshape(-1, D)
    y2 = _mm(u, Wdn, "nn", [F32], "mlp_down")
    dx3, dy2, loss_part, dg_mlp_post = _loss_head(x2, y2, norm_mlp_post, tgt, "loss_head")

    dz =_mm(dy2, Wdn, "nt", [BF16], "mlp_down_dx", epilogue=lambda du, r: (du * (2.0 * r.astype(F32)),), extras=(zr,))
    g_down = _mm(u, dy2, "tn", [BF16], "mlp_down_dw")
    down_dev = by_device(g_down, "w_down")
    pair_down = _Exchange([down_dev], "pair")
    up_dev = _mm(h2, dz, "tn", [BF16], "mlp_up_dw", comm=pair_down, out_shards=True)
    down_chip = _pair_sum(down_dev, pair_down.results[0], "pair_sum_w_down")
    pair_up = _Exchange([up_dev], "pair")
    cut = 5 * down_chip.shape[1] // 8
    sc_down = [_Exchange([down_chip[:, :cut]], "chips"), _Exchange([down_chip[:, cut:]], "chips")]
    dh2 = _mm(dz, Wup_shards, "nt", [F32], "mlp_up_dx", comm=[pair_up, sc_down[0]], b_shards=True)
    up_chip = _pair_sum(up_dev, pair_up.results[0], "pair_sum_w_up")

    dx2, dy1, dg_mlp_pre, dg_attn_post = _norm_pair_bwd(dh2, x2, norm_mlp_pre, dx3, y1, norm_attn_post,
                                                        "norm_mlp_pre_attn_post_bwd")
    dmixed = _mm(dy1, Wo, "nt", [F32], "attn_out_dx")
    g_out = _mm(mixed, dy1, "tn", [BF16], "attn_out_dw")

    b_delta, b_dout = _delta_prep(dmixed, 1, b_out, "mla_delta")
    sc_up = _Exchange([up_chip], "chips")
    dq_mla = _mla_bwd_q(q_mla, k_mla, kvb, b_dout, b_lse, b_delta, "mla_bwd_q", comm=sc_up)
    sc_out = scatter_of(g_out, "w_out")
    dk_mla, dkvb = _mla_bwd_kv(q_mla, k_mla, kvb, b_dout, b_lse, b_delta, "mla_bwd_kv", comm=[sc_out, sc_down[1]])
    dqb = _rope_apply(dq_mla, cos_b, -sin_b, ROPE_MLA // 2, 2 * HEADS, odd, BF16, "rope_mla_q_bwd")
    g_uq_pad = _mm(cqn, dqb, "tn", [BF16], "q_up_dw")
    g_ukv_perm = _mm(ckvn, dkvb, "tn", [BF16], "kv_up_dw")
    dcqn = _mm(dqb, Wuq, "nt", [F32], "q_up_dx")
    dckvn = _mm(dkvb, Wukv, "nt", [F32], "kv_up_dx")

    g_uq = _unpad_heads(g_uq_pad, HD + ROPE_MLA)
    g_ukv = _join_heads(g_ukv_perm[:, :AW], g_ukv_perm[:, AW:])
    sc_uq = _Exchange([_cols_to_shards(g_uq), _cols_to_shards(g_ukv)], "scatter")
    a_delta_by_d, a_dout_by_d = _dswa_delta(dmixed, a_out, "dswa_delta")
    a_grads = [_dswa_bwd(qkv_by_d[c], a_dout_by_d[c], a_lse_by_d[c], a_delta_by_d[c], f"dswa_bwd_d{d}")
               for c, d in enumerate(DSWA_DILATIONS)]
    dproj = _dswa_combine(a_grads, cos_a, -sin_a, "dswa_combine", IN_PAD)
    dproj, dg_q = _rms_bwd(dcqn, proj, q_latent_norm, BF16, "q_latent_norm_bwd", width=Q_LORA,
                           col_block=3 * AW // Q_LORA, into=(dproj, 3 * AW // Q_LORA))
    dproj, dg_kv = _rms_bwd(dckvn, proj, kv_latent_norm, BF16, "kv_latent_norm_bwd", width=KV_LORA,
                            col_block=3 * AW // KV_LORA + 1, into=(dproj, 3 * AW // KV_LORA + 1))
    dproj = _shared_key_grad(dk_mla, cos_b, -sin_b, ROPE_MLA // 2, "rope_mla_k_bwd", (dproj, (IN_PAD - HD) // HD))
    g_in_pad = _mm(h1, dproj, "tn", [BF16], "proj_in_dw", tn=1408, comm=sc_uq)
    in_dev = by_device(g_in_pad[:, :IN_COLS], "w_in")
    pair_in = _Exchange([in_dev], "pair").standalone("pair_w_in")
    sc_in = _Exchange([_pair_sum(in_dev, pair_in[0], "pair_sum_w_in")], "chips")
    dh1 = _mm(dproj, Wi, "nt", [F32], "proj_in_dx", comm=sc_in)
    grad_x, dg_attn_pre = _rms_bwd(dh1, xs, norm_attn_pre, F32, "norm_attn_pre_bwd", residual=dx2)

    parts = dict(w_in=sc_in.results[0], w_uq=sc_uq.results[0], w_ukv=sc_uq.results[1], w_out=sc_out.results[0],
                 w_up=sc_up.results[0], w_down=jnp.concatenate([sc.results[0] for sc in sc_down], axis=1))
    big_out = {n: _adamw(parts[n], *big[n], f"adamw_{n}") for n in big_names}

    gain_names = ["norm_attn_pre", "norm_attn_post", "q_latent_norm", "kv_latent_norm", "norm_mlp_pre", "norm_mlp_post"]
    gain_args = dict(norm_attn_pre=(norm_attn_pre, m_norm_attn_pre, v_norm_attn_pre),
                     norm_attn_post=(norm_attn_post, m_norm_attn_post, v_norm_attn_post),
                     q_latent_norm=(q_latent_norm, m_q_latent_norm, v_q_latent_norm),
                     kv_latent_norm=(kv_latent_norm, m_kv_latent_norm, v_kv_latent_norm),
                     norm_mlp_pre=(norm_mlp_pre, m_norm_mlp_pre, v_norm_mlp_pre),
                     norm_mlp_post=(norm_mlp_post, m_norm_mlp_post, v_norm_mlp_post))
    gain_grads = dict(norm_attn_pre=dg_attn_pre, norm_attn_post=dg_attn_post, q_latent_norm=dg_q,
                      kv_latent_norm=dg_kv, norm_mlp_pre=dg_mlp_pre, norm_mlp_post=dg_mlp_post)
    packed = jnp.concatenate([gain_grads[n] for n in gain_names], axis=1)
    gain_parts = _Exchange([packed], "gather").standalone("gather_gain_grads")[0]
    pack3 = lambda i: jnp.concatenate([gain_args[n][i] for n in gain_names], axis=1)
    gain_out = _adamw(gain_parts, pack3(0), pack3(1), pack3(2), "adamw_gains", tb=1)
    offs = [0]
    for n in gain_names:
        offs.append(offs[-1] + gain_args[n][0].shape[1])
    small_out = {n: tuple(o[:, offs[i]:offs[i + 1]] for o in gain_out) for i, n in enumerate(gain_names)}

    loss = lax.psum(loss_part[0, 0], ("x", "y", "c"))

    order = ["norm_attn_pre", "norm_attn_post", "w_in", "q_latent_norm", "kv_latent_norm", "w_uq", "w_ukv", "w_out",
             "norm_mlp_pre", "norm_mlp_post", "w_up", "w_down"]
    res = {n: (small_out[n] if n in small_out else tuple(o[None] for o in big_out[n])) for n in order}
    return (loss, grad_x[None], *[res[n][0] for n in order], *[res[n][1] for n in order],
            *[res[n][2] for n in order], *[res[n][3] for n in order])
```

```python
import functools
import math

import jax
import jax.numpy as jnp
from jax import lax
from jax.experimental import pallas as pl
from jax.experimental.pallas import tpu as pltpu

F32 = jnp.float32
BF16 = jnp.bfloat16

N_DEV = 8
HEADS = 8
HD = 128
AW = HEADS * HD
Q_LORA = 512
KV_LORA = 512
ROPE_MLA = 64
ROT_A = 32
IN_COLS = 3 * AW + Q_LORA + KV_LORA + ROPE_MLA
IN_PAD = 3 * AW + Q_LORA + KV_LORA + HD
QBLK = 128
DSWA_DILATIONS = (1, 4, 16)
ROPE_THETA = 500000.0
NORM_EPS = 1e-6
NEG_INF = -1e30
SCALE_A = HD ** -0.5
SCALE_B = (HD + ROPE_MLA) ** -0.5

ADAM_LR = 0.001
ADAM_B1 = 0.9
ADAM_B2 = 0.999
ADAM_EPS = 1e-08
ADAM_WD = 0.01
ADAM_STEP = 10

VMEM_LIMIT = 48 * 1024 * 1024

NT = (((1,), (1,)), ((), ()))
NN = (((1,), (0,)), ((), ()))
TN = (((0,), (0,)), ((), ()))


def _dot(a, b, dims):
    return lax.dot_general(a, b, dims, preferred_element_type=F32)


def _params(*sem):
    return pltpu.CompilerParams(dimension_semantics=sem, vmem_limit_bytes=VMEM_LIMIT)


def _tile(n, want):
    t = min(n, want)
    while n % t:
        t //= 2
    return t


def _tile128(n, want):
    if n % 128:
        return n
    units = n // 128
    return 128 * max(u for u in range(1, max(want // 128, 1) + 1) if units % u == 0)


class _Exchange:
    def __init__(self, arrs, mode):
        self.arrs = list(arrs)
        self.mode = mode
        self.n = len(self.arrs)
        self.results = None
        hbm = pl.BlockSpec(memory_space=pltpu.HBM)
        self.specs = [hbm] * self.n
        shape = {"gather": lambda a: (N_DEV,) + a.shape, "scatter": lambda a: a.shape,
                 "pair": lambda a: (4,) + a.shape[1:], "chips": lambda a: a.shape}[mode]
        self.out_shape = [jax.ShapeDtypeStruct(shape(a), a.dtype) for a in self.arrs]
        n_sem = self.n * (N_DEV - 1)
        self.scratch = [pltpu.SemaphoreType.DMA((n_sem,)), pltpu.SemaphoreType.DMA((n_sem,)),
                        pltpu.SemaphoreType.DMA((self.n,))]

    def hooks(self, ins, outs, send_sems, recv_sems, local_sems):
        x, y, c = lax.axis_index("x"), lax.axis_index("y"), lax.axis_index("c")
        me = (x, y, c)
        sib = (x, y, 1 - c)
        chips = [(1 - x, y), (x, 1 - y), (1 - x, 1 - y)]
        slot = lambda p: 4 * p[0] + 2 * p[1] + p[2]
        chip_of = lambda p: 2 * p[0] + p[1]

        def rcopy(a, k, src, dst, to):
            i = a * (N_DEV - 1) + k
            return pltpu.make_async_remote_copy(src_ref=src, dst_ref=dst, send_sem=send_sems.at[i],
                                                recv_sem=recv_sems.at[i], device_id=to,
                                                device_id_type=pl.DeviceIdType.MESH)

        def local(a):
            if self.mode == "chips":
                return pltpu.make_async_copy(ins[a].at[chip_of(me)], outs[a].at[chip_of(me)], local_sems.at[a])
            src = ins[a].at[slot(me)] if self.mode == "scatter" else ins[a]
            return pltpu.make_async_copy(src, outs[a].at[slot(me)], local_sems.at[a])

        def peer(rel):
            return (1 - x if rel & 4 else x, 1 - y if rel & 2 else y, 1 - c if rel & 1 else c)

        if self.mode == "pair":
            def start():
                for a in range(self.n):
                    for p in range(4):
                        rcopy(a, p, ins[a].at[2 * p + 1 - c], outs[a].at[p], sib).start()

            def middle():
                pass

            def finish():
                for a in range(self.n):
                    for p in range(4):
                        cp = rcopy(a, p, ins[a].at[2 * p + 1 - c], outs[a].at[p], sib)
                        cp.wait_send()
                        cp.wait_recv()
        elif self.mode == "chips":
            def start():
                for a in range(self.n):
                    local(a).start()
                    for j, chip in enumerate(chips):
                        rcopy(a, j, ins[a].at[chip_of(chip)], outs[a].at[chip_of(me)], (*chip, c)).start()

            def middle():
                pass

            def finish():
                for a in range(self.n):
                    for j, chip in enumerate(chips):
                        cp = rcopy(a, j, ins[a].at[chip_of(chip)], outs[a].at[chip_of(chip)], (*chip, c))
                        cp.wait_send()
                        cp.wait_recv()
                    local(a).wait()
        elif self.mode == "scatter":
            def start():
                for a in range(self.n):
                    local(a).start()
                    for rel in range(1, N_DEV):
                        rcopy(a, rel - 1, ins[a].at[slot(peer(rel))], outs[a].at[slot(me)], peer(rel)).start()

            def middle():
                pass

            def finish():
                for a in range(self.n):
                    for rel in range(1, N_DEV):
                        cp = rcopy(a, rel - 1, ins[a].at[slot(peer(rel))], outs[a].at[slot(peer(rel))], peer(rel))
                        cp.wait_send()
                        cp.wait_recv()
                    local(a).wait()
        else:
            def start():
                for a in range(self.n):
                    local(a).start()
                    rcopy(a, 0, ins[a], outs[a].at[slot(me)], sib).start()
                    for j, chip in enumerate(chips):
                        rcopy(a, 1 + j, ins[a], outs[a].at[slot(me)], (*chip, c)).start()

            def middle():
                for a in range(self.n):
                    for j, chip in enumerate(chips):
                        landed = outs[a].at[slot((*chip, c))]
                        rcopy(a, 1 + j, ins[a], landed, me).wait_recv()
                        rcopy(a, 4 + j, landed, landed, sib).start()

            def finish():
                for a in range(self.n):
                    rcopy(a, 0, ins[a], outs[a].at[slot(sib)], me).wait_recv()
                    for j, chip in enumerate(chips):
                        rcopy(a, 4 + j, ins[a], outs[a].at[slot((*chip, 1 - c))], me).wait_recv()
                    for k in range(N_DEV - 1):
                        rcopy(a, k, ins[a], outs[a].at[slot(me)], me).wait_send()
                    local(a).wait()

        return start, middle, finish

    def set_results(self, res):
        self.results = list(res)

    def standalone(self, name):
        n = self.n

        def body(*refs):
            start, middle, finish = self.hooks(refs[:n], refs[n:2 * n], *refs[2 * n:])
            start()
            middle()
            finish()

        self.results = pl.pallas_call(
            body, name=name, in_specs=self.specs, out_specs=self.specs, out_shape=self.out_shape,
            scratch_shapes=self.scratch, compiler_params=pltpu.CompilerParams(has_side_effects=True),
        )(*self.arrs)
        return self.results


class _Carried:
    def __init__(self, parts):
        self.parts = list(parts)
        self.n = sum(p.n for p in self.parts)
        self.arrs = [a for p in self.parts for a in p.arrs]
        self.specs = [s for p in self.parts for s in p.specs]
        self.out_shape = [s for p in self.parts for s in p.out_shape]
        self.scratch = [s for p in self.parts for s in p.scratch]

    def hooks(self, ins, outs, *sems):
        hooks, i = [], 0
        for j, p in enumerate(self.parts):
            hooks.append(p.hooks(ins[i:i + p.n], outs[i:i + p.n], *sems[3 * j:3 * j + 3]))
            i += p.n
        def phase(k):
            def run():
                for h in hooks:
                    h[k]()
            return run

        return phase(0), phase(1), phase(2)

    def set_results(self, res):
        i = 0
        for p in self.parts:
            p.set_results(res[i:i + p.n])
            i += p.n


def _call(body, name, grid, in_specs, out_specs, out_shape, args, scratch=(), sem=(), comm=None, prefetch=(),
          aliases=None):
    npf = len(prefetch)
    if isinstance(comm, (list, tuple)):
        comm = _Carried(comm)
    if comm is None:
        spec = pltpu.PrefetchScalarGridSpec(num_scalar_prefetch=npf, grid=grid, in_specs=list(in_specs),
                                            out_specs=list(out_specs), scratch_shapes=list(scratch))
        return pl.pallas_call(body, name=name, grid_spec=spec, out_shape=list(out_shape),
                              input_output_aliases=aliases or {},
                              compiler_params=_params(*sem))(*prefetch, *args)
    assert aliases is None
    ni, no, ns, n = len(in_specs), len(out_specs), len(scratch), comm.n
    steps = math.prod(grid)

    def wrapped(*refs):
        pf, refs = refs[:npf], refs[npf:]
        ins, c_ins = refs[:ni], refs[ni:ni + n]
        outs, c_outs = refs[ni + n:ni + n + no], refs[ni + n + no:ni + 2 * n + no]
        scr, c_scr = refs[ni + 2 * n + no:ni + 2 * n + no + ns], refs[ni + 2 * n + no + ns:]
        start, middle, finish = comm.hooks(c_ins, c_outs, *c_scr)
        step = pl.program_id(0)
        for ax in range(1, len(grid)):
            step = step * grid[ax] + pl.program_id(ax)
        pl.when(step == 0)(start)
        pl.when(step == steps // 2)(middle)
        body(*pf, *ins, *outs, *scr)
        pl.when(step == steps - 1)(finish)

    spec = pltpu.PrefetchScalarGridSpec(num_scalar_prefetch=npf, grid=grid, in_specs=list(in_specs) + comm.specs,
                                        out_specs=list(out_specs) + comm.specs,
                                        scratch_shapes=list(scratch) + comm.scratch)
    res = pl.pallas_call(
        wrapped, name=name, grid_spec=spec, out_shape=list(out_shape) + comm.out_shape,
        compiler_params=pltpu.CompilerParams(dimension_semantics=("arbitrary",) * len(grid),
                                             vmem_limit_bytes=VMEM_LIMIT, has_side_effects=True),
    )(*prefetch, *args, *comm.arrs)
    comm.set_results(res[no:])
    return res[:no]


def _mm(a, b, mode, out_dtypes, name, epilogue=None, extras=(), tm=1024, tn=1024, tk=2048, comm=None,
        b_shards=False, out_shards=False):
    if mode == "tn":
        K, M = a.shape
    else:
        M, K = a.shape
    if b_shards:
        N = b.shape[1] if mode == "nt" else N_DEV * b.shape[2]
    else:
        N = b.shape[0] if mode == "nt" else b.shape[1]
    tm, tn, tk = _tile128(M, tm), _tile128(N, tn), _tile128(K, tk)
    pair_k = b_shards and mode == "nt"
    if pair_k:
        tk = 2 * K // N_DEV
        b = b.reshape(N_DEV // 2, 2, *b.shape[1:])
    elif b_shards or out_shards:
        tn = N // N_DEV
    nk = K // tk
    dims = {"nn": NN, "nt": NT, "tn": TN}[mode]
    a_spec = (pl.BlockSpec((tk, tm), lambda i, j, k: (k, i)) if mode == "tn"
              else pl.BlockSpec((tm, tk), lambda i, j, k: (i, k)))
    if b_shards:
        b_spec = (pl.BlockSpec((None, 2, tn, tk // 2), lambda i, j, k: (k, 0, j, 0)) if mode == "nt"
                  else pl.BlockSpec((None, tk, tn), lambda i, j, k: (j, k, 0)))
    else:
        b_spec = (pl.BlockSpec((tn, tk), lambda i, j, k: (j, k)) if mode == "nt"
                  else pl.BlockSpec((tk, tn), lambda i, j, k: (k, j)))
    mn_spec = pl.BlockSpec((tm, tn), lambda i, j, k: (i, j))
    out_spec = pl.BlockSpec((None, tm, tn), lambda i, j, k: (j, i, 0)) if out_shards else mn_spec
    out_dims = (N_DEV, M, N // N_DEV) if out_shards else (M, N)
    n_ex = len(extras)
    n_out = len(out_dtypes)

    def finish(acc, ex, outs):
        res = (acc,) if epilogue is None else epilogue(acc, *[e[...] for e in ex])
        for o, r in zip(outs, res):
            o[...] = r.astype(o.dtype)

    def product(a_ref, b_ref):
        if pair_k:
            return _dot(a_ref[:, :tk // 2], b_ref[0], NT) + _dot(a_ref[:, tk // 2:], b_ref[1], NT)
        return _dot(a_ref[...], b_ref[...], dims)

    def body(*refs):
        a_ref, b_ref = refs[:2]
        ex = refs[2:2 + n_ex]
        outs = refs[2 + n_ex:2 + n_ex + n_out]
        if nk == 1:
            finish(product(a_ref, b_ref), ex, outs)
            return
        acc = refs[-1]
        k = pl.program_id(2)

        @pl.when(k == 0)
        def _():
            acc[...] = product(a_ref, b_ref)

        @pl.when(jnp.logical_and(k > 0, k < nk - 1))
        def _():
            acc[...] += product(a_ref, b_ref)

        @pl.when(k == nk - 1)
        def _():
            finish(acc[...] + product(a_ref, b_ref), ex, outs)

    out = _call(
        body, name, (M // tm, N // tn, nk), [a_spec, b_spec] + [mn_spec] * n_ex, [out_spec] * n_out,
        [jax.ShapeDtypeStruct(out_dims, dt) for dt in out_dtypes], (a, b, *extras),
        scratch=[] if nk == 1 else [pltpu.VMEM((tm, tn), F32)], sem=("parallel", "parallel", "arbitrary"),
        comm=comm)
    return out[0] if n_out == 1 else out


def _rms_fwd(x, gain, out_dtype, name, width=None, col_block=0, residual=None, tb=256, comm=None):
    T = x.shape[0]
    W = x.shape[1] if width is None else width
    tb = _tile(T, tb)
    has_res = residual is not None

    def body(*refs):
        x_ref, g_ref = refs[:2]
        o_ref = refs[-1]
        xf = x_ref[...]
        y = xf * lax.rsqrt(jnp.mean(xf * xf, axis=-1, keepdims=True) + NORM_EPS) * g_ref[...]
        if has_res:
            y = refs[2][...] + y
        o_ref[...] = y.astype(o_ref.dtype)

    row = pl.BlockSpec((tb, W), lambda i: (i, 0))
    ins = [x, gain] + ([residual] if has_res else [])
    return _call(
        body, name, (T // tb,),
        [pl.BlockSpec((tb, W), lambda i: (i, col_block)),
         pl.BlockSpec((1, W), lambda i: (0, 0))] + ([row] if has_res else []),
        [row], [jax.ShapeDtypeStruct((T, W), out_dtype)], ins, sem=("parallel",), comm=comm)[0]


def _rms_bwd(dy, x, gain, out_dtype, name, width=None, col_block=0, residual=None, tb=256, comm=None, into=None):
    T = dy.shape[0]
    W = x.shape[1] if width is None else width
    tb = _tile(T, tb)
    has_res = residual is not None

    def body(*refs):
        dy_ref, x_ref, g_ref = refs[:3]
        dx_ref, dg_ref = refs[-2:]
        i = pl.program_id(0)
        xf = x_ref[...]
        r = lax.rsqrt(jnp.mean(xf * xf, axis=-1, keepdims=True) + NORM_EPS)
        xn = xf * r
        dyf = dy_ref[...].astype(F32)
        dyg = dyf * g_ref[...]
        dx = r * (dyg - xn * jnp.mean(dyg * xn, axis=-1, keepdims=True))
        if has_res:
            dx = refs[3][...] + dx
        dx_ref[...] = dx.astype(dx_ref.dtype)

        @pl.when(i == 0)
        def _():
            dg_ref[...] = jnp.zeros_like(dg_ref)

        dg_ref[...] += jnp.sum(dyf * xn, axis=0, keepdims=True)

    row = pl.BlockSpec((tb, W), lambda i: (i, 0))
    vec = pl.BlockSpec((1, W), lambda i: (0, 0))
    ins = [dy, x, gain] + ([residual] if has_res else [])
    in_specs = [row, pl.BlockSpec((tb, W), lambda i: (i, col_block)), vec] + ([row] if has_res else [])
    if into is None:
        out_spec, out_struct, aliases = row, jax.ShapeDtypeStruct((T, W), out_dtype), None
    else:
        buf, buf_block = into
        out_spec = pl.BlockSpec((tb, W), lambda i: (i, buf_block))
        out_struct, aliases = jax.ShapeDtypeStruct(buf.shape, buf.dtype), {len(ins): 0}
        ins, in_specs = ins + [buf], in_specs + [pl.BlockSpec(memory_space=pl.ANY)]
    return _call(
        body, name, (T // tb,), in_specs, [out_spec, vec], [out_struct, jax.ShapeDtypeStruct((1, W), F32)], ins,
        sem=("arbitrary",), comm=comm, aliases=aliases)


def _rms(xf):
    r = lax.rsqrt(jnp.mean(xf * xf, axis=-1, keepdims=True) + NORM_EPS)
    return r, xf * r


def _rms_grad(dyf, xn, r, gain):
    dyg = dyf * gain
    return r * (dyg - xn * jnp.mean(dyg * xn, axis=-1, keepdims=True)), dyf * xn


def _accumulate_rows(i, ref, rows):
    @pl.when(i == 0)
    def _():
        ref[...] = jnp.zeros_like(ref)

    ref[...] += jnp.sum(rows, axis=0, keepdims=True)


def _norm_pair_fwd(y1, xs, gain_post, gain_pre, name, tb=256):
    T, D = xs.shape
    tb = _tile(T, tb)

    def body(y1_ref, xs_ref, gp_ref, gq_ref, x2_ref, h2_ref):
        x2 = xs_ref[...] + _rms(y1_ref[...])[1] * gp_ref[...]
        x2_ref[...] = x2
        h2_ref[...] = (_rms(x2)[1] * gq_ref[...]).astype(BF16)

    row = pl.BlockSpec((tb, D), lambda i: (i, 0))
    vec = pl.BlockSpec((1, D), lambda i: (0, 0))
    return pl.pallas_call(
        body, name=name, grid=(T // tb,), in_specs=[row, row, vec, vec], out_specs=[row, row],
        out_shape=[jax.ShapeDtypeStruct((T, D), F32), jax.ShapeDtypeStruct((T, D), BF16)],
        compiler_params=_params("parallel"),
    )(y1, xs, gain_post, gain_pre)


def _norm_pair_bwd(dh2, x2, gain_pre, dx3, y1, gain_post, name, tb=256):
    T, D = x2.shape
    tb = _tile(T, tb)

    def body(dh2_ref, x2_ref, gq_ref, dx3_ref, y1_ref, gp_ref, dx2_ref, dy1_ref, dgq_ref, dgp_ref):
        i = pl.program_id(0)
        r2, xn2 = _rms(x2_ref[...])
        d2, rows_q = _rms_grad(dh2_ref[...], xn2, r2, gq_ref[...])
        dx2 = dx3_ref[...] + d2
        dx2_ref[...] = dx2
        r1, yn1 = _rms(y1_ref[...])
        d1, rows_p = _rms_grad(dx2, yn1, r1, gp_ref[...])
        dy1_ref[...] = d1.astype(BF16)
        _accumulate_rows(i, dgq_ref, rows_q)
        _accumulate_rows(i, dgp_ref, rows_p)

    row = pl.BlockSpec((tb, D), lambda i: (i, 0))
    vec = pl.BlockSpec((1, D), lambda i: (0, 0))
    return pl.pallas_call(
        body, name=name, grid=(T // tb,), in_specs=[row, row, vec, row, row, vec], out_specs=[row, row, vec, vec],
        out_shape=[jax.ShapeDtypeStruct((T, D), F32), jax.ShapeDtypeStruct((T, D), BF16),
                   jax.ShapeDtypeStruct((1, D), F32), jax.ShapeDtypeStruct((1, D), F32)],
        compiler_params=_params("arbitrary"),
    )(dh2, x2, gain_pre, dx3, y1, gain_post)


def _loss_head(x2, y2, gain, target, name, tb=256):
    T, D = x2.shape
    tb = _tile(T, tb)

    def body(x2_ref, y2_ref, g_ref, t_ref, dx3_ref, dy2_ref, loss_ref, dg_ref):
        i = pl.program_id(0)
        r, yn = _rms(y2_ref[...])
        e = x2_ref[...] + yn * g_ref[...] - t_ref[...]
        dx3 = e * (1.0 / D)
        dx3_ref[...] = dx3
        dy2, rows = _rms_grad(dx3, yn, r, g_ref[...])
        dy2_ref[...] = dy2.astype(BF16)
        _accumulate_rows(i, dg_ref, rows)
        _accumulate_rows(i, loss_ref, 0.5 * jnp.mean(e * e, axis=-1, keepdims=True))

    row = pl.BlockSpec((tb, D), lambda i: (i, 0))
    vec = pl.BlockSpec((1, D), lambda i: (0, 0))
    return pl.pallas_call(
        body, name=name, grid=(T // tb,),
        in_specs=[row, row, vec, row],
        out_specs=[row, row, pl.BlockSpec((1, 1), lambda i: (0, 0)), vec],
        out_shape=[jax.ShapeDtypeStruct((T, D), F32), jax.ShapeDtypeStruct((T, D), BF16),
                   jax.ShapeDtypeStruct((1, 1), F32), jax.ShapeDtypeStruct((1, D), F32)],
        compiler_params=_params("arbitrary"),
    )(x2, y2, gain, target)


def _rope_tables(positions, rot_dim):
    half = rot_dim // 2
    inv_freq = ROPE_THETA ** (-jnp.arange(0, rot_dim, 2, dtype=F32) / rot_dim)
    ang = positions.astype(F32)[:, None] * inv_freq[None, :]
    cos, sin = jnp.cos(ang), jnp.sin(ang)
    T = positions.shape[0]
    ones = jnp.ones((T, HD - rot_dim), F32)
    cos_t = jnp.concatenate([cos, cos, ones], axis=1)
    sin_t = jnp.concatenate([-sin, sin, jnp.zeros_like(ones)], axis=1)
    return cos_t, sin_t


def _rotate(x, cos_t, sin_t, half):
    lane = lax.broadcasted_iota(jnp.int32, x.shape, 1)
    swapped = jnp.where(lane < half, pltpu.roll(x, HD - half, 1), pltpu.roll(x, half, 1))
    return x * cos_t + swapped * sin_t


def _rope_apply(x, cos_t, sin_t, half, n_blocks, is_rope, out_dtype, name, window=0, tb=256):
    T = x.shape[0]
    tb = _tile(T, tb)
    W = n_blocks * HD

    def body(x_ref, c_ref, s_ref, o_ref):
        for j in range(n_blocks):
            sl = slice(j * HD, (j + 1) * HD)
            xj = x_ref[:, sl]
            if is_rope(j):
                xj = _rotate(xj.astype(F32), c_ref[...], s_ref[...], half)
            o_ref[:, sl] = xj.astype(o_ref.dtype)

    tab = pl.BlockSpec((tb, HD), lambda i: (i, 0))
    return pl.pallas_call(
        body, name=name, grid=(T // tb,),
        in_specs=[pl.BlockSpec((tb, W), lambda i: (i, window)), tab, tab],
        out_specs=pl.BlockSpec((tb, W), lambda i: (i, 0)),
        out_shape=jax.ShapeDtypeStruct((T, W), out_dtype),
        compiler_params=_params("parallel"),
    )(x, cos_t, sin_t)


DSWA_TB = 2048


def _deinterleave(src, dst_ref, d, dtype):
    rows = src.shape[0] // d
    for r in range(d):
        dst_ref[r] = src[pl.ds(r, rows, stride=d), :].astype(dtype)


def _rope_dswa(proj, cos_t, sin_t, name, comm=None):
    T = proj.shape[0]
    tb = _tile(T, 2 * DSWA_TB)
    half = ROT_A // 2

    def body(x_ref, c_ref, s_ref, *rest):
        outs, scr = rest[:-1], rest[-1]
        j = pl.program_id(1)

        @pl.when(j < 2 * HEADS)
        def _():
            scr[...] = _rotate(x_ref[...], c_ref[...], s_ref[...], half)

        @pl.when(j >= 2 * HEADS)
        def _():
            scr[...] = x_ref[...]

        for o_ref, d in zip(outs, DSWA_DILATIONS):
            _deinterleave(scr, o_ref, d, BF16)

    blk = pl.BlockSpec((tb, HD), lambda i, j: (i, j))
    tab = pl.BlockSpec((tb, HD), lambda i, j: (i, 0))
    return _call(
        body, name, (T // tb, 3 * HEADS), [blk, tab, tab],
        [pl.BlockSpec((d, tb // d, HD), lambda i, j: (0, i, j)) for d in DSWA_DILATIONS],
        [jax.ShapeDtypeStruct((d, T // d, 3 * AW), BF16) for d in DSWA_DILATIONS], (proj, cos_t, sin_t),
        scratch=[pltpu.VMEM((tb, HD), F32)], sem=("parallel", "parallel"), comm=comm)


def _shared_key_grad(dk, cos_t, sin_t_neg, half, name, into, tb=512):
    T = dk.shape[0]
    tb = _tile(T, tb)
    buf, buf_block = into

    def body(d_ref, c_ref, s_ref, buf_ref, o_ref):
        tot = d_ref[:, HD:2 * HD]
        for h in range(1, HEADS):
            tot = tot + d_ref[:, h * QK + HD:(h + 1) * QK]
        o_ref[...] = _rotate(tot, c_ref[...], s_ref[...], half).astype(o_ref.dtype)

    tab = pl.BlockSpec((tb, HD), lambda i: (i, 0))
    return pl.pallas_call(
        body, name=name, grid=(T // tb,),
        in_specs=[pl.BlockSpec((tb, HEADS * QK), lambda i: (i, 0)), tab, tab, pl.BlockSpec(memory_space=pl.ANY)],
        out_specs=pl.BlockSpec((tb, HD), lambda i: (i, buf_block)),
        out_shape=jax.ShapeDtypeStruct(buf.shape, buf.dtype), input_output_aliases={3: 0},
        compiler_params=_params("parallel"),
    )(dk, cos_t, sin_t_neg, buf)


def _band_mask(n):
    row = lax.broadcasted_iota(jnp.int32, (QBLK, 2 * QBLK), 0)
    col = lax.broadcasted_iota(jnp.int32, (QBLK, 2 * QBLK), 1)
    in_prev = jnp.logical_and(jnp.logical_and(col < QBLK, col >= row), n > 0)
    in_cur = jnp.logical_and(col >= QBLK, col - QBLK <= row)
    return jnp.logical_or(in_prev, in_cur)


def _dswa_specs(nb, reverse=False):
    pos = (lambda n: nb - 1 - n) if reverse else (lambda n: n)
    cur = lambda c: pl.BlockSpec((None, QBLK, AW), lambda r, n: (r, pos(n), c))
    prev = lambda c: pl.BlockSpec((None, QBLK, AW), lambda r, n: (r, jnp.maximum(pos(n) - 1, 0), c))
    stat = pl.BlockSpec((None, QBLK, HD), lambda r, n: (r, pos(n), 0))
    return cur, prev, stat


def _relayout_spec(d, tb, per_head=True):
    if per_head:
        return pl.BlockSpec((d, tb // d, HD), lambda i, h: (0, i, h))
    return pl.BlockSpec((d, tb // d, HD), lambda i, h: (0, i, 0))


def _head_lane(x, h):
    lane = lax.broadcasted_iota(jnp.int32, x.shape, 1)
    return jnp.sum(jnp.where(lane == h, x, 0.0), axis=-1, keepdims=True)


def _dswa_fwd(qkv, name, comm=None):
    d, sd = qkv.shape[:2]
    nb = sd // QBLK

    def body(q_ref, kc_ref, kp_ref, vc_ref, vp_ref, o_ref, l_ref):
        mask = _band_mask(pl.program_id(1))
        l_ref[...] = jnp.zeros_like(l_ref)
        for h in range(HEADS):
            sl = slice(h * HD, (h + 1) * HD)
            keys = jnp.concatenate([kp_ref[:, sl], kc_ref[:, sl]], axis=0)
            vals = jnp.concatenate([vp_ref[:, sl], vc_ref[:, sl]], axis=0)
            s = jnp.where(mask, _dot(q_ref[:, sl], keys, NT) * SCALE_A, NEG_INF)
            m = jnp.max(s, axis=-1, keepdims=True)
            p = jnp.exp(s - m)
            den = jnp.sum(p, axis=-1, keepdims=True)
            o_ref[:, sl] = _dot((p / den).astype(BF16), vals, NN)
            l_ref[:, h:h + 1] = m + jnp.log(den)

    cur, prev, stat = _dswa_specs(nb)
    return _call(
        body, name, (d, nb), [cur(0), cur(1), prev(1), cur(2), prev(2)], [cur(0), stat],
        [jax.ShapeDtypeStruct((d, sd, AW), F32), jax.ShapeDtypeStruct((d, sd, HD), F32)],
        (qkv, qkv, qkv, qkv, qkv), sem=("parallel", "parallel"), comm=comm)


def _dswa_merge(outs, lses, name, comm=None):
    nc = len(DSWA_DILATIONS)
    T = outs[0].shape[0] * outs[0].shape[1]
    tb = _tile(T, DSWA_TB)

    def body(*refs):
        o_refs, l_refs = refs[:nc], refs[nc:2 * nc]
        out_ref, outb_ref = refs[2 * nc:2 * nc + 2]
        lt_refs = refs[2 * nc + 2:3 * nc + 2]
        o_nat, l_nat, lt_nat = refs[3 * nc + 2:4 * nc + 2], refs[4 * nc + 2:5 * nc + 2], refs[-1]
        h = pl.program_id(1)
        for c, d in enumerate(DSWA_DILATIONS):
            for r in range(d):
                o_nat[c][pl.ds(r, tb // d, stride=d), :] = o_refs[c][r]
                l_nat[c][pl.ds(r, tb // d, stride=d), :] = l_refs[c][r]
        ls = [l[...] for l in l_nat]
        m = functools.reduce(jnp.maximum, ls)
        es = [jnp.exp(l - m) for l in ls]
        tot = functools.reduce(lambda a, b: a + b, es)
        acc = _head_lane(es[0] / tot, h) * o_nat[0][...]
        for c in range(1, nc):
            acc = acc + _head_lane(es[c] / tot, h) * o_nat[c][...]
        out_ref[...] = acc
        outb_ref[...] = acc.astype(BF16)
        lt_nat[...] = m + jnp.log(tot)
        for c, d in enumerate(DSWA_DILATIONS):
            _deinterleave(lt_nat, lt_refs[c], d, F32)

    nat = pl.BlockSpec((tb, HD), lambda i, h: (i, h))
    by_d = [_relayout_spec(d, tb) for d in DSWA_DILATIONS]
    stat_by_d = [_relayout_spec(d, tb, per_head=False) for d in DSWA_DILATIONS]
    res = _call(
        body, name, (T // tb, HEADS), by_d + stat_by_d, [nat, nat] + stat_by_d,
        [jax.ShapeDtypeStruct((T, AW), F32), jax.ShapeDtypeStruct((T, AW), BF16)]
        + [jax.ShapeDtypeStruct((d, T // d, HD), F32) for d in DSWA_DILATIONS], (*outs, *lses),
        scratch=[pltpu.VMEM((tb, HD), F32)] * (2 * nc + 1), sem=("parallel", "arbitrary"), comm=comm)
    return res[0], res[1], res[2:]


def _dswa_delta(dout, out, name):
    nc = len(DSWA_DILATIONS)
    T = out.shape[0]
    tb = _tile(T, DSWA_TB)

    def body(do_ref, o_ref, *rest):
        dl_refs, dob_refs, dl_nat = rest[:nc], rest[nc:2 * nc], rest[-1]
        h = pl.program_id(1)
        lane = lax.broadcasted_iota(jnp.int32, (tb, HD), 1)
        mine = jnp.where(lane == h, jnp.sum(do_ref[...] * o_ref[...], axis=-1, keepdims=True), 0.0)

        @pl.when(h == 0)
        def _():
            dl_nat[...] = mine

        @pl.when(h > 0)
        def _():
            dl_nat[...] += mine

        for c, d in enumerate(DSWA_DILATIONS):
            _deinterleave(dl_nat, dl_refs[c], d, F32)
            _deinterleave(do_ref, dob_refs[c], d, BF16)

    nat = pl.BlockSpec((tb, HD), lambda i, h: (i, h))
    by_d = [_relayout_spec(d, tb) for d in DSWA_DILATIONS]
    stat_by_d = [_relayout_spec(d, tb, per_head=False) for d in DSWA_DILATIONS]
    res = pl.pallas_call(
        body, name=name, grid=(T // tb, HEADS),
        in_specs=[nat, nat], out_specs=stat_by_d + by_d,
        out_shape=[jax.ShapeDtypeStruct((d, T // d, HD), F32) for d in DSWA_DILATIONS]
        + [jax.ShapeDtypeStruct((d, T // d, AW), BF16) for d in DSWA_DILATIONS],
        scratch_shapes=[pltpu.VMEM((tb, HD), F32)],
        compiler_params=_params("parallel", "arbitrary"),
    )(dout, out)
    return res[:nc], res[nc:]


def _delta_prep(dout, col_block, out, name, tb=256):
    T = out.shape[0]
    tb = _tile(T, tb)

    def body(do_ref, o_ref, delta_ref, dob_ref):
        delta_ref[...] = jnp.zeros_like(delta_ref)
        for h in range(HEADS):
            sl = slice(h * HD, (h + 1) * HD)
            doh = do_ref[:, sl]
            delta_ref[:, h:h + 1] = jnp.sum(doh * o_ref[:, sl], axis=-1, keepdims=True)
            dob_ref[:, sl] = doh.astype(BF16)

    row = pl.BlockSpec((tb, AW), lambda i: (i, 0))
    return pl.pallas_call(
        body, name=name, grid=(T // tb,),
        in_specs=[pl.BlockSpec((tb, AW), lambda i: (i, col_block)), row],
        out_specs=[pl.BlockSpec((tb, HD), lambda i: (i, 0)), row],
        out_shape=[jax.ShapeDtypeStruct((T, HD), F32), jax.ShapeDtypeStruct((T, AW), BF16)],
        compiler_params=_params("parallel"),
    )(dout, out)


def _dswa_bwd(qkv, dout_b, lse_tot, delta, name, comm=None):
    d, sd = qkv.shape[:2]
    nb = sd // QBLK

    def body(q_ref, kc_ref, kp_ref, vc_ref, vp_ref, do_ref, l_ref, dl_ref, g_ref, carry_k, carry_v):
        mask = _band_mask(nb - 1 - pl.program_id(1))

        @pl.when(pl.program_id(1) == 0)
        def _():
            carry_k[...] = jnp.zeros_like(carry_k)
            carry_v[...] = jnp.zeros_like(carry_v)

        for h in range(HEADS):
            sl = slice(h * HD, (h + 1) * HD)
            qh, doh = q_ref[:, sl], do_ref[:, sl]
            keys = jnp.concatenate([kp_ref[:, sl], kc_ref[:, sl]], axis=0)
            vals = jnp.concatenate([vp_ref[:, sl], vc_ref[:, sl]], axis=0)
            s = jnp.where(mask, _dot(qh, keys, NT) * SCALE_A, NEG_INF)
            p = jnp.exp(s - l_ref[:, h:h + 1])
            ds = (p * (_dot(doh, vals, NT) - dl_ref[:, h:h + 1]) * SCALE_A).astype(BF16)
            g_ref[:, sl] = _dot(ds, keys, NN).astype(BF16)
            dk = _dot(ds, qh, TN)
            dv = _dot(p.astype(BF16), doh, TN)
            g_ref[:, AW + h * HD:AW + (h + 1) * HD] = (dk[QBLK:] + carry_k[:, sl]).astype(BF16)
            g_ref[:, 2 * AW + h * HD:2 * AW + (h + 1) * HD] = (dv[QBLK:] + carry_v[:, sl]).astype(BF16)
            carry_k[:, sl] = dk[:QBLK]
            carry_v[:, sl] = dv[:QBLK]

    cur, prev, stat = _dswa_specs(nb, reverse=True)
    out_spec = pl.BlockSpec((None, QBLK, 3 * AW), lambda r, n: (r, nb - 1 - n, 0))
    return _call(
        body, name, (d, nb), [cur(0), cur(1), prev(1), cur(2), prev(2), cur(0), stat, stat], [out_spec],
        [jax.ShapeDtypeStruct((d, sd, 3 * AW), BF16)], (qkv, qkv, qkv, qkv, qkv, dout_b, lse_tot, delta),
        scratch=[pltpu.VMEM((QBLK, AW), F32)] * 2, sem=("parallel", "arbitrary"), comm=comm)[0]


def _dswa_combine(grads, cos_t, sin_t_neg, name, width):
    T = grads[0].shape[0] * grads[0].shape[1]
    tb = _tile(T, DSWA_TB)
    half = ROT_A // 2

    def body(*refs):
        g_refs = refs[:len(grads)]
        c_ref, s_ref, out_ref, acc = refs[len(grads):]
        j = pl.program_id(1)
        for c, d in enumerate(DSWA_DILATIONS):
            for r in range(d):
                if c == 0:
                    acc[...] = g_refs[c][r].astype(F32)
                else:
                    acc[pl.ds(r, tb // d, stride=d), :] += g_refs[c][r].astype(F32)
        val = acc[...]
        out_ref[...] = jnp.where(j < 2 * HEADS, _rotate(val, c_ref[...], s_ref[...], half), val).astype(BF16)

    tab = pl.BlockSpec((tb, HD), lambda i, j: (i, 0))
    return pl.pallas_call(
        body, name=name, grid=(T // tb, 3 * HEADS),
        in_specs=[pl.BlockSpec((d, tb // d, HD), lambda i, j: (0, i, j)) for d in DSWA_DILATIONS] + [tab, tab],
        out_specs=pl.BlockSpec((tb, HD), lambda i, j: (i, j)),
        out_shape=jax.ShapeDtypeStruct((T, width), BF16),
        scratch_shapes=[pltpu.VMEM((tb, HD), F32)],
        compiler_params=_params("parallel", "parallel"),
    )(*grads, cos_t, sin_t_neg)


MLA_TQ = 512
QK = 2 * HD
LOG2E = 1.4426950408889634


def _triangle(nq, key_major):
    pairs = [(q, k) for q in range(nq) for k in range(q + 1)]
    if key_major:
        pairs.sort(key=lambda p: (p[1], p[0]))
    return (jnp.array([p[0] for p in pairs], jnp.int32), jnp.array([p[1] for p in pairs], jnp.int32))


def _mla_specs(tq):
    q_spec = pl.BlockSpec((tq, HEADS * QK), lambda t, qi, ki: (qi[t], 0))
    k_spec = pl.BlockSpec((tq, HEADS * QK), lambda t, qi, ki: (ki[t], 0))
    v_spec = pl.BlockSpec((tq, AW), lambda t, qi, ki: (ki[t], 1))
    qrow = pl.BlockSpec((tq, AW), lambda t, qi, ki: (qi[t], 0))
    krow = pl.BlockSpec((tq, AW), lambda t, qi, ki: (ki[t], 0))
    return q_spec, k_spec, v_spec, qrow, krow


def _mla_pack(kv, kr, name, tb=512):
    T = kv.shape[0]
    tb = _tile(T, tb)

    def body(kv_ref, kr_ref, k_ref, v1_ref):
        lane = lax.broadcasted_iota(jnp.int32, (tb, HD), 1)
        one_hot = jnp.where(lane == 0, 1.0, 0.0).astype(BF16)
        for h in range(HEADS):
            k_ref[:, h * QK:h * QK + HD] = kv_ref[:, h * HD:(h + 1) * HD]
            k_ref[:, h * QK + HD:(h + 1) * QK] = kr_ref[...]
            v1_ref[:, h * QK:h * QK + HD] = kv_ref[:, AW + h * HD:AW + (h + 1) * HD]
            v1_ref[:, h * QK + HD:(h + 1) * QK] = one_hot

    wide = pl.BlockSpec((tb, HEADS * QK), lambda i: (i, 0))
    return pl.pallas_call(
        body, name=name, grid=(T // tb,),
        in_specs=[pl.BlockSpec((tb, 2 * AW), lambda i: (i, 0)), pl.BlockSpec((tb, HD), lambda i: (i, 0))],
        out_specs=[wide, wide], out_shape=[jax.ShapeDtypeStruct((T, HEADS * QK), BF16)] * 2,
        compiler_params=_params("parallel"),
    )(kv, kr)


def _mla_stat_spec(tq):
    return pl.BlockSpec((tq, HD), lambda t, qi, ki: (qi[t], 0))


def _mla_scores(q_ref, k_ref, h, qi, ki, tq):
    s = _dot(q_ref[:, h * QK:(h + 1) * QK], k_ref[:, h * QK:(h + 1) * QK], NT) * SCALE_B
    row = lax.broadcasted_iota(jnp.int32, s.shape, 0) + qi * tq
    col = lax.broadcasted_iota(jnp.int32, s.shape, 1) + ki * tq
    return jnp.where(col <= row, s, NEG_INF)


def _mla_fwd(q, k, v1, name, comm=None):
    T = q.shape[0]
    tq = _tile(T, MLA_TQ)
    tables = _triangle(T // tq, False)

    def body(qi_ref, ki_ref, q_ref, k_ref, v_ref, o_ref, ob_ref, l_ref, m_s, acc):
        t = pl.program_id(0)
        qi, ki = qi_ref[t], ki_ref[t]

        @pl.when(ki == 0)
        def _():
            m_s[...] = jnp.full_like(m_s, NEG_INF)
            acc[...] = jnp.zeros_like(acc)

        row = lax.broadcasted_iota(jnp.int32, (tq, tq), 0) + qi * tq
        col = lax.broadcasted_iota(jnp.int32, (tq, tq), 1) + ki * tq
        bias = jnp.where(col <= row, 0.0, NEG_INF)
        updates = []
        for h in range(HEADS):
            s = _dot(q_ref[:, h * QK:(h + 1) * QK], k_ref[:, h * QK:(h + 1) * QK], NT) + bias
            m_new = jnp.maximum(m_s[h], jnp.max(s, axis=-1, keepdims=True))
            p = jnp.exp2((s - m_new) * (SCALE_B * LOG2E)).astype(BF16)
            alpha = jnp.exp2((m_s[h] - m_new) * (SCALE_B * LOG2E))
            updates.append((m_new, alpha, _dot(p, v_ref[:, h * QK:(h + 1) * QK], NN)))
        for h, (m_new, alpha, pv) in enumerate(updates):
            acc[:, h * QK:(h + 1) * QK] = alpha * acc[:, h * QK:(h + 1) * QK] + pv
            m_s[h] = m_new

        @pl.when(ki == qi)
        def _():
            l_ref[...] = jnp.zeros_like(l_ref)
            for h in range(HEADS):
                sl = slice(h * HD, (h + 1) * HD)
                den = acc[:, h * QK + HD:h * QK + HD + 1]
                out = acc[:, h * QK:h * QK + HD] / den
                o_ref[:, sl] = out
                ob_ref[:, sl] = out.astype(BF16)
                l_ref[:, h:h + 1] = m_s[h] * SCALE_B + jnp.log(den)

    q_spec, k_spec, _, qrow, _ = _mla_specs(tq)
    return _call(
        body, name, (tables[0].shape[0],), [q_spec, k_spec, k_spec], [qrow, qrow, _mla_stat_spec(tq)],
        [jax.ShapeDtypeStruct((T, AW), F32), jax.ShapeDtypeStruct((T, AW), BF16), jax.ShapeDtypeStruct((T, HD), F32)],
        (q, k, v1),
        scratch=[pltpu.VMEM((HEADS, tq, 1), F32), pltpu.VMEM((tq, HEADS * QK), F32)],
        sem=("arbitrary",), comm=comm, prefetch=tables)


def _mla_ds(q_ref, k_ref, v_ref, do_ref, l_ref, dl_ref, h, qi, ki, tq):
    sl = slice(h * HD, (h + 1) * HD)
    p = jnp.exp(_mla_scores(q_ref, k_ref, h, qi, ki, tq) - l_ref[:, h:h + 1])
    ds = (p * (_dot(do_ref[:, sl], v_ref[:, sl], NT) - dl_ref[:, h:h + 1]) * SCALE_B).astype(BF16)
    return p, ds


def _mla_bwd_q(q, k, kv, dout_b, lse, delta, name, comm=None):
    T = q.shape[0]
    tq = _tile(T, MLA_TQ)
    tables = _triangle(T // tq, False)

    def body(qi_ref, ki_ref, q_ref, k_ref, v_ref, do_ref, l_ref, dl_ref, dq_ref):
        t = pl.program_id(0)
        qi, ki = qi_ref[t], ki_ref[t]

        @pl.when(ki == 0)
        def _():
            dq_ref[...] = jnp.zeros_like(dq_ref)

        for h in range(HEADS):
            _, ds = _mla_ds(q_ref, k_ref, v_ref, do_ref, l_ref, dl_ref, h, qi, ki, tq)
            dq_ref[:, h * QK:(h + 1) * QK] += _dot(ds, k_ref[:, h * QK:(h + 1) * QK], NN)

    q_spec, k_spec, v_spec, qrow, _ = _mla_specs(tq)
    return _call(
        body, name, (tables[0].shape[0],), [q_spec, k_spec, v_spec, qrow, _mla_stat_spec(tq), _mla_stat_spec(tq)], [q_spec],
        [jax.ShapeDtypeStruct((T, HEADS * QK), F32)], (q, k, kv, dout_b, lse, delta),
        sem=("arbitrary",), comm=comm, prefetch=tables)[0]


def _mla_bwd_kv(q, k, kv, dout_b, lse, delta, name, comm=None):
    T = q.shape[0]
    tq = _tile(T, MLA_TQ)
    nq = T // tq
    tables = _triangle(nq, True)

    def body(qi_ref, ki_ref, q_ref, k_ref, v_ref, do_ref, l_ref, dl_ref, dk_ref, dkv_ref, dv_acc):
        t = pl.program_id(0)
        qi, ki = qi_ref[t], ki_ref[t]

        @pl.when(qi == ki)
        def _():
            dk_ref[...] = jnp.zeros_like(dk_ref)
            dv_acc[...] = jnp.zeros_like(dv_acc)

        for h in range(HEADS):
            sl = slice(h * HD, (h + 1) * HD)
            p, ds = _mla_ds(q_ref, k_ref, v_ref, do_ref, l_ref, dl_ref, h, qi, ki, tq)
            dv_acc[:, sl] += _dot(p.astype(BF16), do_ref[:, sl], TN)
            dk_ref[:, h * QK:(h + 1) * QK] += _dot(ds, q_ref[:, h * QK:(h + 1) * QK], TN)

        @pl.when(qi == nq - 1)
        def _():
            for h in range(HEADS):
                dkv_ref[:, h * HD:(h + 1) * HD] = dk_ref[:, h * QK:h * QK + HD].astype(BF16)
                dkv_ref[:, AW + h * HD:AW + (h + 1) * HD] = dv_acc[:, h * HD:(h + 1) * HD].astype(BF16)

    q_spec, k_spec, v_spec, qrow, _ = _mla_specs(tq)
    return _call(
        body, name, (tables[0].shape[0],),
        [q_spec, k_spec, v_spec, qrow, _mla_stat_spec(tq), _mla_stat_spec(tq)],
        [k_spec, pl.BlockSpec((tq, 2 * AW), lambda t, qi, ki: (ki[t], 0))],
        [jax.ShapeDtypeStruct((T, HEADS * QK), F32), jax.ShapeDtypeStruct((T, 2 * AW), BF16)],
        (q, k, kv, dout_b, lse, delta), scratch=[pltpu.VMEM((tq, AW), F32)],
        sem=("arbitrary",), comm=comm, prefetch=tables)


def _pair_sum(by_device, from_sibling, name, tb=256):
    n_chip, R, C = from_sibling.shape
    tb = _tile(R, tb)
    core = jnp.reshape(lax.axis_index("c"), (1,)).astype(jnp.int32)

    def body(core_ref, mine_ref, theirs_ref, o_ref):
        o_ref[...] = (mine_ref[...].astype(F32) + theirs_ref[...].astype(F32)).astype(o_ref.dtype)

    blk = pl.BlockSpec((None, tb, C), lambda p, i, core_ref: (p, i, 0))
    return _call(
        body, name, (n_chip, R // tb),
        [pl.BlockSpec((None, tb, C), lambda p, i, core_ref: (2 * p + core_ref[0], i, 0)), blk], [blk],
        [jax.ShapeDtypeStruct((n_chip, R, C), BF16)], (by_device, from_sibling),
        sem=("parallel", "parallel"), prefetch=(core,))[0]


def _adamw(parts, w, m, v, name, tb=128, comm=None):
    R, C = w.shape
    pieces = list(parts) if isinstance(parts, (list, tuple)) else [parts]
    n_parts = pieces[0].shape[0]
    tb = _tile(R, tb)
    starts = [0]
    for p in pieces:
        assert p.shape[1] % tb == 0
        starts.append(starts[-1] + p.shape[1] // tb)
    c1 = 1.0 - ADAM_B1
    c2 = 1.0 - ADAM_B2
    bc1 = 1.0 - ADAM_B1 ** ADAM_STEP
    bc2 = 1.0 - ADAM_B2 ** ADAM_STEP

    def body(*refs):
        p_refs = refs[:len(pieces)]
        w_ref, m_ref, v_ref, g_ref, d_ref, nm_ref, nv_ref = refs[len(pieces):]
        i = pl.program_id(0)
        g = None
        for k, p_ref in enumerate(p_refs):
            gk = p_ref[0].astype(F32)
            for j in range(1, n_parts):
                gk = gk + p_ref[j].astype(F32)
            g = gk if g is None else jnp.where(i >= starts[k], gk, g)
        nm = ADAM_B1 * m_ref[...] + c1 * g
        nv = ADAM_B2 * v_ref[...] + c2 * (g * g)
        g_ref[...] = g
        nm_ref[...] = nm
        nv_ref[...] = nv
        d_ref[...] = -ADAM_LR * ((nm / bc1) / (jnp.sqrt(nv / bc2) + ADAM_EPS) + ADAM_WD * w_ref[...])

    row = pl.BlockSpec((tb, C), lambda i: (i, 0))
    piece_specs = [pl.BlockSpec((n_parts, tb, C),
                                functools.partial(lambda i, lo, hi: (0, jnp.clip(i - lo, 0, hi - lo - 1), 0),
                                                  lo=starts[k], hi=starts[k + 1]))
                   for k in range(len(pieces))]
    return _call(
        body, name, (R // tb,), piece_specs + [row, row, row], [row] * 4,
        [jax.ShapeDtypeStruct((R, C), F32)] * 4, (*pieces, w, m, v), sem=("parallel",), comm=comm)


def _cols_from_shards(g):
    return jnp.transpose(g, (1, 0, 2)).reshape(g.shape[1], N_DEV * g.shape[2])


def _cols_to_shards(w):
    return jnp.transpose(w.reshape(w.shape[0], N_DEV, w.shape[1] // N_DEV), (1, 0, 2))


def _split_heads(w, first):
    w3 = w.reshape(w.shape[0], HEADS, -1)
    return w3[:, :, :first].reshape(w.shape[0], -1), w3[:, :, first:].reshape(w.shape[0], -1)


def _join_heads(a, b):
    R = a.shape[0]
    return jnp.concatenate([a.reshape(R, HEADS, -1), b.reshape(R, HEADS, -1)], axis=2).reshape(R, -1)


def _pad_heads(w, width):
    w3 = w.reshape(w.shape[0], HEADS, -1)
    return jnp.pad(w3, ((0, 0), (0, 0), (0, width - w3.shape[2]))).reshape(w.shape[0], HEADS * width)


def _unpad_heads(w, k):
    return w.reshape(w.shape[0], HEADS, -1)[:, :, :k].reshape(w.shape[0], HEADS * k)


def kernel(x, positions, norm_attn_pre, norm_attn_post, w_in, q_latent_norm, kv_latent_norm, w_uq, w_ukv, w_out, norm_mlp_pre, norm_mlp_post, w_up, w_down, loss_target, m_norm_attn_pre, m_norm_attn_post, m_w_in, m_q_latent_norm, m_kv_latent_norm, m_w_uq, m_w_ukv, m_w_out, m_norm_mlp_pre, m_norm_mlp_post, m_w_up, m_w_down, v_norm_attn_pre, v_norm_attn_post, v_w_in, v_q_latent_norm, v_kv_latent_norm, v_w_uq, v_w_ukv, v_w_out, v_norm_mlp_pre, v_norm_mlp_post, v_w_up, v_w_down):
    xs = x[0]
    tgt = loss_target[0]
    pos = positions[0]
    T, D = xs.shape
    big = dict(w_in=(w_in, m_w_in, v_w_in), w_uq=(w_uq, m_w_uq, v_w_uq), w_ukv=(w_ukv, m_w_ukv, v_w_ukv),
               w_out=(w_out, m_w_out, v_w_out), w_up=(w_up, m_w_up, v_w_up), w_down=(w_down, m_w_down, v_w_down))
    big = {n: tuple(t[0] for t in ts) for n, ts in big.items()}
    big_names = ["w_in", "w_uq", "w_ukv", "w_out", "w_up", "w_down"]
    col_sharded = {"w_in", "w_uq", "w_ukv", "w_up"}

    wb = {n: big[n][0].astype(BF16) for n in big_names}

    def gathered(ex, i, n):
        g = ex.results[i]
        return _cols_from_shards(g) if n in col_sharded else g.reshape(-1, g.shape[2])

    def by_device(g, n):
        return _cols_to_shards(g) if n in col_sharded else g.reshape(N_DEV, g.shape[0] // N_DEV, g.shape[1])

    def scatter_of(g, n):
        return _Exchange([by_device(g, n)], "scatter")

    cos_a, sin_a = _rope_tables(pos, ROT_A)
    cos_b, sin_b = _rope_tables(pos, ROPE_MLA)

    ex_in = _Exchange([wb["w_in"]], "gather")
    h1 = _rms_fwd(xs, norm_attn_pre, BF16, "norm_attn_pre_fwd", comm=ex_in)
    Wi = jnp.pad(gathered(ex_in, 0, "w_in"), ((0, 0), (0, IN_PAD - IN_COLS)))
    ex_mid = _Exchange([wb["w_uq"], wb["w_ukv"], wb["w_out"]], "gather")
    proj = _mm(h1, Wi, "nn", [F32], "proj_in", tn=1408, comm=ex_mid)
    Wuq = _pad_heads(gathered(ex_mid, 0, "w_uq"), QK)
    Wukv = jnp.concatenate(_split_heads(gathered(ex_mid, 1, "w_ukv"), HD), axis=1)
    Wo = gathered(ex_mid, 2, "w_out")
    n_piece = wb["w_down"].shape[0] // 8
    ex_down = [_Exchange([wb["w_down"][i * n_piece:(i + 1) * n_piece]], "gather") for i in range(8)]
    qkv_by_d = _rope_dswa(proj, cos_a, sin_a, "rope_dswa", comm=ex_down[0])
    outs, lses = [], []
    for d, qkv, ex in zip(DSWA_DILATIONS, qkv_by_d, ex_down[1:4]):
        o, l = _dswa_fwd(qkv, f"dswa_fwd_d{d}", comm=ex)
        outs.append(o)
        lses.append(l)
    a_out, a_out_b, a_lse_by_d = _dswa_merge(outs, lses, "dswa_merge", comm=ex_down[4])

    cqn = _rms_fwd(proj, q_latent_norm, BF16, "q_latent_norm_fwd", width=Q_LORA, col_block=3 * AW // Q_LORA)
    ckvn = _rms_fwd(proj, kv_latent_norm, BF16, "kv_latent_norm_fwd", width=KV_LORA, col_block=3 * AW // KV_LORA + 1)
    qb = _mm(cqn, Wuq, "nn", [F32], "q_up")
    kvb = _mm(ckvn, Wukv, "nn", [BF16], "kv_up")
    odd = lambda j: j % 2 == 1
    q_mla = _rope_apply(qb, cos_b, sin_b, ROPE_MLA // 2, 2 * HEADS, odd, BF16, "rope_mla_q")
    kr = _rope_apply(proj, cos_b, sin_b, ROPE_MLA // 2, 1, lambda j: True, BF16, "rope_mla_k",
                     window=(IN_PAD - HD) // HD)
    k_mla, v1_mla = _mla_pack(kvb, kr, "mla_pack")
    ex_up = _Exchange([wb["w_up"]], "gather")
    b_out, b_out_b, b_lse = _mla_fwd(q_mla, k_mla, v1_mla, "mla_fwd", comm=ex_up)
    Wup_shards = ex_up.results[0]

    mixed = jnp.concatenate([a_out_b, b_out_b], axis=1)
    y1 = _mm(mixed, Wo, "nn", [F32], "attn_out", comm=ex_down[5])

    x2, h2 = _norm_pair_fwd(y1, xs, norm_attn_post, norm_mlp_pre, "norm_attn_post_mlp_pre_fwd")

    def relu2(z):
        r = jnp.maximum(z, 0.0)
        return r * r, r

    u, zr = _mm(h2, Wup_shards, "nn", [BF16, BF16], "mlp_up", epilogue=relu2, comm=ex_down[6:8],
                b_shards=True)
    Wdn = jnp.concatenate([ex.results[0] for ex in ex_down], axis=1).reshape(-1, D)
    y2 = _mm(u, Wdn, "nn", [F32], "mlp_down")
    dx3, dy2, loss_part, dg_mlp_post = _loss_head(x2, y2, norm_mlp_post, tgt, "loss_head")

    dz =_mm(dy2, Wdn, "nt", [BF16], "mlp_down_dx", epilogue=lambda du, r: (du * (2.0 * r.astype(F32)),), extras=(zr,))
    g_down = _mm(u, dy2, "tn", [BF16], "mlp_down_dw")
    down_dev = by_device(g_down, "w_down")
    pair_down = _Exchange([down_dev], "pair")
    up_dev = _mm(h2, dz, "tn", [BF16], "mlp_up_dw", comm=pair_down, out_shards=True)
    down_chip = _pair_sum(down_dev, pair_down.results[0], "pair_sum_w_down")
    pair_up = _Exchange([up_dev], "pair")
    cut = 5 * down_chip.shape[1] // 8
    sc_down = [_Exchange([down_chip[:, :cut]], "chips"), _Exchange([down_chip[:, cut:]], "chips")]
    dh2 = _mm(dz, Wup_shards, "nt", [F32], "mlp_up_dx", comm=[pair_up, sc_down[0]], b_shards=True)
    up_chip = _pair_sum(up_dev, pair_up.results[0], "pair_sum_w_up")

    dx2, dy1, dg_mlp_pre, dg_attn_post = _norm_pair_bwd(dh2, x2, norm_mlp_pre, dx3, y1, norm_attn_post,
                                                        "norm_mlp_pre_attn_post_bwd")
    dmixed = _mm(dy1, Wo, "nt", [F32], "attn_out_dx")
    g_out = _mm(mixed, dy1, "tn", [BF16], "attn_out_dw")

    b_delta, b_dout = _delta_prep(dmixed, 1, b_out, "mla_delta")
    sc_up = _Exchange([up_chip], "chips")
    dq_mla = _mla_bwd_q(q_mla, k_mla, kvb, b_dout, b_lse, b_delta, "mla_bwd_q", comm=sc_up)
    sc_out = scatter_of(g_out, "w_out")
    dk_mla, dkvb = _mla_bwd_kv(q_mla, k_mla, kvb, b_dout, b_lse, b_delta, "mla_bwd_kv", comm=[sc_out, sc_down[1]])
    dqb = _rope_apply(dq_mla, cos_b, -sin_b, ROPE_MLA // 2, 2 * HEADS, odd, BF16, "rope_mla_q_bwd")
    g_uq_pad = _mm(cqn, dqb, "tn", [BF16], "q_up_dw")
    g_ukv_perm = _mm(ckvn, dkvb, "tn", [BF16], "kv_up_dw")
    dcqn = _mm(dqb, Wuq, "nt", [F32], "q_up_dx")
    dckvn = _mm(dkvb, Wukv, "nt", [F32], "kv_up_dx")

    g_uq = _unpad_heads(g_uq_pad, HD + ROPE_MLA)
    g_ukv = _join_heads(g_ukv_perm[:, :AW], g_ukv_perm[:, AW:])
    sc_uq = _Exchange([_cols_to_shards(g_uq), _cols_to_shards(g_ukv)], "scatter")
    a_delta_by_d, a_dout_by_d = _dswa_delta(dmixed, a_out, "dswa_delta")
    a_grads = [_dswa_bwd(qkv_by_d[c], a_dout_by_d[c], a_lse_by_d[c], a_delta_by_d[c], f"dswa_bwd_d{d}")
               for c, d in enumerate(DSWA_DILATIONS)]
    dproj = _dswa_combine(a_grads, cos_a, -sin_a, "dswa_combine", IN_PAD)
    dproj, dg_q = _rms_bwd(dcqn, proj, q_latent_norm, BF16, "q_latent_norm_bwd", width=Q_LORA,
                           col_block=3 * AW // Q_LORA, into=(dproj, 3 * AW // Q_LORA))
    dproj, dg_kv = _rms_bwd(dckvn, proj, kv_latent_norm, BF16, "kv_latent_norm_bwd", width=KV_LORA,
                            col_block=3 * AW // KV_LORA + 1, into=(dproj, 3 * AW // KV_LORA + 1))
    dproj = _shared_key_grad(dk_mla, cos_b, -sin_b, ROPE_MLA // 2, "rope_mla_k_bwd", (dproj, (IN_PAD - HD) // HD))
    g_in_pad = _mm(h1, dproj, "tn", [BF16], "proj_in_dw", tn=1408, comm=sc_uq)
    in_dev = by_device(g_in_pad[:, :IN_COLS], "w_in")
    pair_in = _Exchange([in_dev], "pair").standalone("pair_w_in")
    sc_in = _Exchange([_pair_sum(in_dev, pair_in[0], "pair_sum_w_in")], "chips")
    dh1 = _mm(dproj, Wi, "nt", [F32], "proj_in_dx", comm=sc_in)
    grad_x, dg_attn_pre = _rms_bwd(dh1, xs, norm_attn_pre, F32, "norm_attn_pre_bwd", residual=dx2)

    parts = dict(w_in=sc_in.results[0], w_uq=sc_uq.results[0], w_ukv=sc_uq.results[1], w_out=sc_out.results[0],
                 w_up=sc_up.results[0], w_down=[sc.results[0] for sc in sc_down])
    big_out = {n: _adamw(parts[n], *big[n], f"adamw_{n}") for n in big_names}

    gain_names = ["norm_attn_pre", "norm_attn_post", "q_latent_norm", "kv_latent_norm", "norm_mlp_pre", "norm_mlp_post"]
    gain_args = dict(norm_attn_pre=(norm_attn_pre, m_norm_attn_pre, v_norm_attn_pre),
                     norm_attn_post=(norm_attn_post, m_norm_attn_post, v_norm_attn_post),
                     q_latent_norm=(q_latent_norm, m_q_latent_norm, v_q_latent_norm),
                     kv_latent_norm=(kv_latent_norm, m_kv_latent_norm, v_kv_latent_norm),
                     norm_mlp_pre=(norm_mlp_pre, m_norm_mlp_pre, v_norm_mlp_pre),
                     norm_mlp_post=(norm_mlp_post, m_norm_mlp_post, v_norm_mlp_post))
    gain_grads = dict(norm_attn_pre=dg_attn_pre, norm_attn_post=dg_attn_post, q_latent_norm=dg_q,
                      kv_latent_norm=dg_kv, norm_mlp_pre=dg_mlp_pre, norm_mlp_post=dg_mlp_post)
    packed = jnp.concatenate([gain_grads[n] for n in gain_names], axis=1)
    gain_parts = _Exchange([packed], "gather").standalone("gather_gain_grads")[0]
    pack3 = lambda i: jnp.concatenate([gain_args[n][i] for n in gain_names], axis=1)
    gain_out = _adamw(gain_parts, pack3(0), pack3(1), pack3(2), "adamw_gains", tb=1)
    offs = [0]
    for n in gain_names:
        offs.append(offs[-1] + gain_args[n][0].shape[1])
    small_out = {n: tuple(o[:, offs[i]:offs[i + 1]] for o in gain_out) for i, n in enumerate(gain_names)}

    loss = lax.psum(loss_part[0, 0], ("x", "y", "c"))

    order = ["norm_attn_pre", "norm_attn_post", "w_in", "q_latent_norm", "kv_latent_norm", "w_uq", "w_ukv", "w_out",
             "norm_mlp_pre", "norm_mlp_post", "w_up", "w_down"]
    res = {n: (small_out[n] if n in small_out else tuple(o[None] for o in big_out[n])) for n in order}
    return (loss, grad_x[None], *[res[n][0] for n in order], *[res[n][1] for n in order],
            *[res[n][2] for n in order], *[res[n][3] for n in order])
```

```python
import functools
import math

import jax
import jax.numpy as jnp
from jax import lax
from jax.experimental import pallas as pl
from jax.experimental.pallas import tpu as pltpu

F32 = jnp.float32
BF16 = jnp.bfloat16

N_DEV = 8
HEADS = 8
HD = 128
AW = HEADS * HD
Q_LORA = 512
KV_LORA = 512
ROPE_MLA = 64
ROT_A = 32
IN_COLS = 3 * AW + Q_LORA + KV_LORA + ROPE_MLA
IN_PAD = 3 * AW + Q_LORA + KV_LORA + HD
QBLK = 128
DSWA_DILATIONS = (1, 4, 16)
ROPE_THETA = 500000.0
NORM_EPS = 1e-6
NEG_INF = -1e30
SCALE_A = HD ** -0.5
SCALE_B = (HD + ROPE_MLA) ** -0.5

ADAM_LR = 0.001
ADAM_B1 = 0.9
ADAM_B2 = 0.999
ADAM_EPS = 1e-08
ADAM_WD = 0.01
ADAM_STEP = 10

VMEM_LIMIT = 48 * 1024 * 1024

NT = (((1,), (1,)), ((), ()))
NN = (((1,), (0,)), ((), ()))
TN = (((0,), (0,)), ((), ()))


def _dot(a, b, dims):
    return lax.dot_general(a, b, dims, preferred_element_type=F32)


def _params(*sem):
    return pltpu.CompilerParams(dimension_semantics=sem, vmem_limit_bytes=VMEM_LIMIT)


def _tile(n, want):
    t = min(n, want)
    while n % t:
        t //= 2
    return t


def _tile128(n, want):
    if n % 128:
        return n
    units = n // 128
    return 128 * max(u for u in range(1, max(want // 128, 1) + 1) if units % u == 0)


class _Exchange:
    def __init__(self, arrs, mode):
        self.arrs = list(arrs)
        self.mode = mode
        self.n = len(self.arrs)
        self.results = None
        hbm = pl.BlockSpec(memory_space=pltpu.HBM)
        self.specs = [hbm] * self.n
        shape = {"gather": lambda a: (N_DEV,) + a.shape, "scatter": lambda a: a.shape,
                 "pair": lambda a: (4,) + a.shape[1:], "chips": lambda a: a.shape}[mode]
        self.out_shape = [jax.ShapeDtypeStruct(shape(a), a.dtype) for a in self.arrs]
        n_sem = self.n * (N_DEV - 1)
        self.scratch = [pltpu.SemaphoreType.DMA((n_sem,)), pltpu.SemaphoreType.DMA((n_sem,)),
                        pltpu.SemaphoreType.DMA((self.n,))]

    def hooks(self, ins, outs, send_sems, recv_sems, local_sems):
        x, y, c = lax.axis_index("x"), lax.axis_index("y"), lax.axis_index("c")
        me = (x, y, c)
        sib = (x, y, 1 - c)
        chips = [(1 - x, y), (x, 1 - y), (1 - x, 1 - y)]
        slot = lambda p: 4 * p[0] + 2 * p[1] + p[2]
        chip_of = lambda p: 2 * p[0] + p[1]

        def rcopy(a, k, src, dst, to):
            i = a * (N_DEV - 1) + k
            return pltpu.make_async_remote_copy(src_ref=src, dst_ref=dst, send_sem=send_sems.at[i],
                                                recv_sem=recv_sems.at[i], device_id=to,
                                                device_id_type=pl.DeviceIdType.MESH)

        def local(a):
            if self.mode == "chips":
                return pltpu.make_async_copy(ins[a].at[chip_of(me)], outs[a].at[chip_of(me)], local_sems.at[a])
            src = ins[a].at[slot(me)] if self.mode == "scatter" else ins[a]
            return pltpu.make_async_copy(src, outs[a].at[slot(me)], local_sems.at[a])

        def peer(rel):
            return (1 - x if rel & 4 else x, 1 - y if rel & 2 else y, 1 - c if rel & 1 else c)

        if self.mode == "pair":
            def start():
                for a in range(self.n):
                    for p in range(4):
                        rcopy(a, p, ins[a].at[2 * p + 1 - c], outs[a].at[p], sib).start()

            def middle():
                pass

            def finish():
                for a in range(self.n):
                    for p in range(4):
                        cp = rcopy(a, p, ins[a].at[2 * p + 1 - c], outs[a].at[p], sib)
                        cp.wait_send()
                        cp.wait_recv()
        elif self.mode == "chips":
            def start():
                for a in range(self.n):
                    local(a).start()
                    for j, chip in enumerate(chips):
                        rcopy(a, j, ins[a].at[chip_of(chip)], outs[a].at[chip_of(me)], (*chip, c)).start()

            def middle():
                pass

            def finish():
                for a in range(self.n):
                    for j, chip in enumerate(chips):
                        cp = rcopy(a, j, ins[a].at[chip_of(chip)], outs[a].at[chip_of(chip)], (*chip, c))
                        cp.wait_send()
                        cp.wait_recv()
                    local(a).wait()
        elif self.mode == "scatter":
            def start():
                for a in range(self.n):
                    local(a).start()
                    for rel in range(1, N_DEV):
                        rcopy(a, rel - 1, ins[a].at[slot(peer(rel))], outs[a].at[slot(me)], peer(rel)).start()

            def middle():
                pass

            def finish():
                for a in range(self.n):
                    for rel in range(1, N_DEV):
                        cp = rcopy(a, rel - 1, ins[a].at[slot(peer(rel))], outs[a].at[slot(peer(rel))], peer(rel))
                        cp.wait_send()
                        cp.wait_recv()
                    local(a).wait()
        else:
            def start():
                for a in range(self.n):
                    local(a).start()
                    rcopy(a, 0, ins[a], outs[a].at[slot(me)], sib).start()
                    for j, chip in enumerate(chips):
                        rcopy(a, 1 + j, ins[a], outs[a].at[slot(me)], (*chip, c)).start()

            def middle():
                for a in range(self.n):
                    for j, chip in enumerate(chips):
                        landed = outs[a].at[slot((*chip, c))]
                        rcopy(a, 1 + j, ins[a], landed, me).wait_recv()
                        rcopy(a, 4 + j, landed, landed, sib).start()

            def finish():
                for a in range(self.n):
                    rcopy(a, 0, ins[a], outs[a].at[slot(sib)], me).wait_recv()
                    for j, chip in enumerate(chips):
                        rcopy(a, 4 + j, ins[a], outs[a].at[slot((*chip, 1 - c))], me).wait_recv()
                    for k in range(N_DEV - 1):
                        rcopy(a, k, ins[a], outs[a].at[slot(me)], me).wait_send()
                    local(a).wait()

        return start, middle, finish

    def set_results(self, res):
        self.results = list(res)

    def standalone(self, name):
        n = self.n

        def body(*refs):
            start, middle, finish = self.hooks(refs[:n], refs[n:2 * n], *refs[2 * n:])
            start()
            middle()
            finish()

        self.results = pl.pallas_call(
            body, name=name, in_specs=self.specs, out_specs=self.specs, out_shape=self.out_shape,
            scratch_shapes=self.scratch, compiler_params=pltpu.CompilerParams(has_side_effects=True),
        )(*self.arrs)
        return self.results


class _Carried:
    def __init__(self, parts):
        self.parts = list(parts)
        self.n = sum(p.n for p in self.parts)
        self.arrs = [a for p in self.parts for a in p.arrs]
        self.specs = [s for p in self.parts for s in p.specs]
        self.out_shape = [s for p in self.parts for s in p.out_shape]
        self.scratch = [s for p in self.parts for s in p.scratch]

    def hooks(self, ins, outs, *sems):
        hooks, i = [], 0
        for j, p in enumerate(self.parts):
            hooks.append(p.hooks(ins[i:i + p.n], outs[i:i + p.n], *sems[3 * j:3 * j + 3]))
            i += p.n
        def phase(k):
            def run():
                for h in hooks:
                    h[k]()
            return run

        return phase(0), phase(1), phase(2)

    def set_results(self, res):
        i = 0
        for p in self.parts:
            p.set_results(res[i:i + p.n])
            i += p.n


def _call(body, name, grid, in_specs, out_specs, out_shape, args, scratch=(), sem=(), comm=None, prefetch=(),
          aliases=None):
    npf = len(prefetch)
    if isinstance(comm, (list, tuple)):
        comm = _Carried(comm)
    if comm is None:
        spec = pltpu.PrefetchScalarGridSpec(num_scalar_prefetch=npf, grid=grid, in_specs=list(in_specs),
                                            out_specs=list(out_specs), scratch_shapes=list(scratch))
        return pl.pallas_call(body, name=name, grid_spec=spec, out_shape=list(out_shape),
                              input_output_aliases=aliases or {},
                              compiler_params=_params(*sem))(*prefetch, *args)
    assert aliases is None
    ni, no, ns, n = len(in_specs), len(out_specs), len(scratch), comm.n
    steps = math.prod(grid)

    def wrapped(*refs):
        pf, refs = refs[:npf], refs[npf:]
        ins, c_ins = refs[:ni], refs[ni:ni + n]
        outs, c_outs = refs[ni + n:ni + n + no], refs[ni + n + no:ni + 2 * n + no]
        scr, c_scr = refs[ni + 2 * n + no:ni + 2 * n + no + ns], refs[ni + 2 * n + no + ns:]
        start, middle, finish = comm.hooks(c_ins, c_outs, *c_scr)
        step = pl.program_id(0)
        for ax in range(1, len(grid)):
            step = step * grid[ax] + pl.program_id(ax)
        pl.when(step == 0)(start)
        pl.when(step == steps // 2)(middle)
        body(*pf, *ins, *outs, *scr)
        pl.when(step == steps - 1)(finish)

    spec = pltpu.PrefetchScalarGridSpec(num_scalar_prefetch=npf, grid=grid, in_specs=list(in_specs) + comm.specs,
                                        out_specs=list(out_specs) + comm.specs,
                                        scratch_shapes=list(scratch) + comm.scratch)
    res = pl.pallas_call(
        wrapped, name=name, grid_spec=spec, out_shape=list(out_shape) + comm.out_shape,
        compiler_params=pltpu.CompilerParams(dimension_semantics=("arbitrary",) * len(grid),
                                             vmem_limit_bytes=VMEM_LIMIT, has_side_effects=True),
    )(*prefetch, *args, *comm.arrs)
    comm.set_results(res[no:])
    return res[:no]


def _mm(a, b, mode, out_dtypes, name, epilogue=None, extras=(), tm=1024, tn=1024, tk=2048, comm=None,
        b_shards=False, out_shards=False):
    if mode == "tn":
        K, M = a.shape
    else:
        M, K = a.shape
    if b_shards:
        N = b.shape[1] if mode == "nt" else N_DEV * b.shape[2]
    else:
        N = b.shape[0] if mode == "nt" else b.shape[1]
    tm, tn, tk = _tile128(M, tm), _tile128(N, tn), _tile128(K, tk)
    pair_k = b_shards and mode == "nt"
    if pair_k:
        tk = 2 * K // N_DEV
        b = b.reshape(N_DEV // 2, 2, *b.shape[1:])
    elif b_shards or out_shards:
        tn = N // N_DEV
    nk = K // tk
    dims = {"nn": NN, "nt": NT, "tn": TN}[mode]
    a_spec = (pl.BlockSpec((tk, tm), lambda i, j, k: (k, i)) if mode == "tn"
              else pl.BlockSpec((tm, tk), lambda i, j, k: (i, k)))
    if b_shards:
        b_spec = (pl.BlockSpec((None, 2, tn, tk // 2), lambda i, j, k: (k, 0, j, 0)) if mode == "nt"
                  else pl.BlockSpec((None, tk, tn), lambda i, j, k: (j, k, 0)))
    else:
        b_spec = (pl.BlockSpec((tn, tk), lambda i, j, k: (j, k)) if mode == "nt"
                  else pl.BlockSpec((tk, tn), lambda i, j, k: (k, j)))
    mn_spec = pl.BlockSpec((tm, tn), lambda i, j, k: (i, j))
    out_spec = pl.BlockSpec((None, tm, tn), lambda i, j, k: (j, i, 0)) if out_shards else mn_spec
    out_dims = (N_DEV, M, N // N_DEV) if out_shards else (M, N)
    n_ex = len(extras)
    n_out = len(out_dtypes)

    def finish(acc, ex, outs):
        res = (acc,) if epilogue is None else epilogue(acc, *[e[...] for e in ex])
        for o, r in zip(outs, res):
            o[...] = r.astype(o.dtype)

    def product(a_ref, b_ref):
        if pair_k:
            return _dot(a_ref[:, :tk // 2], b_ref[0], NT) + _dot(a_ref[:, tk // 2:], b_ref[1], NT)
        return _dot(a_ref[...], b_ref[...], dims)

    def body(*refs):
        a_ref, b_ref = refs[:2]
        ex = refs[2:2 + n_ex]
        outs = refs[2 + n_ex:2 + n_ex + n_out]
        if nk == 1:
            finish(product(a_ref, b_ref), ex, outs)
            return
        acc = refs[-1]
        k = pl.program_id(2)

        @pl.when(k == 0)
        def _():
            acc[...] = product(a_ref, b_ref)

        @pl.when(jnp.logical_and(k > 0, k < nk - 1))
        def _():
            acc[...] += product(a_ref, b_ref)

        @pl.when(k == nk - 1)
        def _():
            finish(acc[...] + product(a_ref, b_ref), ex, outs)

    out = _call(
        body, name, (M // tm, N // tn, nk), [a_spec, b_spec] + [mn_spec] * n_ex, [out_spec] * n_out,
        [jax.ShapeDtypeStruct(out_dims, dt) for dt in out_dtypes], (a, b, *extras),
        scratch=[] if nk == 1 else [pltpu.VMEM((tm, tn), F32)], sem=("parallel", "parallel", "arbitrary"),
        comm=comm)
    return out[0] if n_out == 1 else out


def _rms_fwd(x, gain, out_dtype, name, width=None, col_block=0, residual=None, tb=256, comm=None):
    T = x.shape[0]
    W = x.shape[1] if width is None else width
    tb = _tile(T, tb)
    has_res = residual is not None

    def body(*refs):
        x_ref, g_ref = refs[:2]
        o_ref = refs[-1]
        xf = x_ref[...]
        y = xf * lax.rsqrt(jnp.mean(xf * xf, axis=-1, keepdims=True) + NORM_EPS) * g_ref[...]
        if has_res:
            y = refs[2][...] + y
        o_ref[...] = y.astype(o_ref.dtype)

    row = pl.BlockSpec((tb, W), lambda i: (i, 0))
    ins = [x, gain] + ([residual] if has_res else [])
    return _call(
        body, name, (T // tb,),
        [pl.BlockSpec((tb, W), lambda i: (i, col_block)),
         pl.BlockSpec((1, W), lambda i: (0, 0))] + ([row] if has_res else []),
        [row], [jax.ShapeDtypeStruct((T, W), out_dtype)], ins, sem=("parallel",), comm=comm)[0]


def _rms_bwd(dy, x, gain, out_dtype, name, width=None, col_block=0, residual=None, tb=256, comm=None, into=None):
    T = dy.shape[0]
    W = x.shape[1] if width is None else width
    tb = _tile(T, tb)
    has_res = residual is not None

    def body(*refs):
        dy_ref, x_ref, g_ref = refs[:3]
        dx_ref, dg_ref = refs[-2:]
        i = pl.program_id(0)
        xf = x_ref[...]
        r = lax.rsqrt(jnp.mean(xf * xf, axis=-1, keepdims=True) + NORM_EPS)
        xn = xf * r
        dyf = dy_ref[...].astype(F32)
        dyg = dyf * g_ref[...]
        dx = r * (dyg - xn * jnp.mean(dyg * xn, axis=-1, keepdims=True))
        if has_res:
            dx = refs[3][...] + dx
        dx_ref[...] = dx.astype(dx_ref.dtype)

        @pl.when(i == 0)
        def _():
            dg_ref[...] = jnp.zeros_like(dg_ref)

        dg_ref[...] += jnp.sum(dyf * xn, axis=0, keepdims=True)

    row = pl.BlockSpec((tb, W), lambda i: (i, 0))
    vec = pl.BlockSpec((1, W), lambda i: (0, 0))
    ins = [dy, x, gain] + ([residual] if has_res else [])
    in_specs = [row, pl.BlockSpec((tb, W), lambda i: (i, col_block)), vec] + ([row] if has_res else [])
    if into is None:
        out_spec, out_struct, aliases = row, jax.ShapeDtypeStruct((T, W), out_dtype), None
    else:
        buf, buf_block = into
        out_spec = pl.BlockSpec((tb, W), lambda i: (i, buf_block))
        out_struct, aliases = jax.ShapeDtypeStruct(buf.shape, buf.dtype), {len(ins): 0}
        ins, in_specs = ins + [buf], in_specs + [pl.BlockSpec(memory_space=pl.ANY)]
    return _call(
        body, name, (T // tb,), in_specs, [out_spec, vec], [out_struct, jax.ShapeDtypeStruct((1, W), F32)], ins,
        sem=("arbitrary",), comm=comm, aliases=aliases)


def _rms(xf):
    r = lax.rsqrt(jnp.mean(xf * xf, axis=-1, keepdims=True) + NORM_EPS)
    return r, xf * r


def _rms_grad(dyf, xn, r, gain):
    dyg = dyf * gain
    return r * (dyg - xn * jnp.mean(dyg * xn, axis=-1, keepdims=True)), dyf * xn


def _accumulate_rows(i, ref, rows):
    @pl.when(i == 0)
    def _():
        ref[...] = jnp.zeros_like(ref)

    ref[...] += jnp.sum(rows, axis=0, keepdims=True)


def _norm_pair_fwd(y1, xs, gain_post, gain_pre, name, tb=256):
    T, D = xs.shape
    tb = _tile(T, tb)

    def body(y1_ref, xs_ref, gp_ref, gq_ref, x2_ref, h2_ref):
        x2 = xs_ref[...] + _rms(y1_ref[...])[1] * gp_ref[...]
        x2_ref[...] = x2
        h2_ref[...] = (_rms(x2)[1] * gq_ref[...]).astype(BF16)

    row = pl.BlockSpec((tb, D), lambda i: (i, 0))
    vec = pl.BlockSpec((1, D), lambda i: (0, 0))
    return pl.pallas_call(
        body, name=name, grid=(T // tb,), in_specs=[row, row, vec, vec], out_specs=[row, row],
        out_shape=[jax.ShapeDtypeStruct((T, D), F32), jax.ShapeDtypeStruct((T, D), BF16)],
        compiler_params=_params("parallel"),
    )(y1, xs, gain_post, gain_pre)


def _norm_pair_bwd(dh2, x2, gain_pre, dx3, y1, gain_post, name, tb=256):
    T, D = x2.shape
    tb = _tile(T, tb)

    def body(dh2_ref, x2_ref, gq_ref, dx3_ref, y1_ref, gp_ref, dx2_ref, dy1_ref, dgq_ref, dgp_ref):
        i = pl.program_id(0)
        r2, xn2 = _rms(x2_ref[...])
        d2, rows_q = _rms_grad(dh2_ref[...], xn2, r2, gq_ref[...])
        dx2 = dx3_ref[...] + d2
        dx2_ref[...] = dx2
        r1, yn1 = _rms(y1_ref[...])
        d1, rows_p = _rms_grad(dx2, yn1, r1, gp_ref[...])
        dy1_ref[...] = d1.astype(BF16)
        _accumulate_rows(i, dgq_ref, rows_q)
        _accumulate_rows(i, dgp_ref, rows_p)

    row = pl.BlockSpec((tb, D), lambda i: (i, 0))
    vec = pl.BlockSpec((1, D), lambda i: (0, 0))
    return pl.pallas_call(
        body, name=name, grid=(T // tb,), in_specs=[row, row, vec, row, row, vec], out_specs=[row, row, vec, vec],
        out_shape=[jax.ShapeDtypeStruct((T, D), F32), jax.ShapeDtypeStruct((T, D), BF16),
                   jax.ShapeDtypeStruct((1, D), F32), jax.ShapeDtypeStruct((1, D), F32)],
        compiler_params=_params("arbitrary"),
    )(dh2, x2, gain_pre, dx3, y1, gain_post)


def _loss_head(x2, y2, gain, target, name, tb=256):
    T, D = x2.shape
    tb = _tile(T, tb)

    def body(x2_ref, y2_ref, g_ref, t_ref, dx3_ref, dy2_ref, loss_ref, dg_ref):
        i = pl.program_id(0)
        r, yn = _rms(y2_ref[...])
        e = x2_ref[...] + yn * g_ref[...] - t_ref[...]
        dx3 = e * (1.0 / D)
        dx3_ref[...] = dx3
        dy2, rows = _rms_grad(dx3, yn, r, g_ref[...])
        dy2_ref[...] = dy2.astype(BF16)
        _accumulate_rows(i, dg_ref, rows)
        _accumulate_rows(i, loss_ref, 0.5 * jnp.mean(e * e, axis=-1, keepdims=True))

    row = pl.BlockSpec((tb, D), lambda i: (i, 0))
    vec = pl.BlockSpec((1, D), lambda i: (0, 0))
    return pl.pallas_call(
        body, name=name, grid=(T // tb,),
        in_specs=[row, row, vec, row],
        out_specs=[row, row, pl.BlockSpec((1, 1), lambda i: (0, 0)), vec],
        out_shape=[jax.ShapeDtypeStruct((T, D), F32), jax.ShapeDtypeStruct((T, D), BF16),
                   jax.ShapeDtypeStruct((1, 1), F32), jax.ShapeDtypeStruct((1, D), F32)],
        compiler_params=_params("arbitrary"),
    )(x2, y2, gain, target)


def _rope_tables(positions, rot_dim):
    half = rot_dim // 2
    inv_freq = ROPE_THETA ** (-jnp.arange(0, rot_dim, 2, dtype=F32) / rot_dim)
    ang = positions.astype(F32)[:, None] * inv_freq[None, :]
    cos, sin = jnp.cos(ang), jnp.sin(ang)
    T = positions.shape[0]
    ones = jnp.ones((T, HD - rot_dim), F32)
    cos_t = jnp.concatenate([cos, cos, ones], axis=1)
    sin_t = jnp.concatenate([-sin, sin, jnp.zeros_like(ones)], axis=1)
    return cos_t, sin_t


def _rotate(x, cos_t, sin_t, half):
    lane = lax.broadcasted_iota(jnp.int32, x.shape, 1)
    swapped = jnp.where(lane < half, pltpu.roll(x, HD - half, 1), pltpu.roll(x, half, 1))
    return x * cos_t + swapped * sin_t


def _rope_apply(x, cos_t, sin_t, half, n_blocks, is_rope, out_dtype, name, window=0, tb=256):
    T = x.shape[0]
    tb = _tile(T, tb)
    W = n_blocks * HD

    def body(x_ref, c_ref, s_ref, o_ref):
        for j in range(n_blocks):
            sl = slice(j * HD, (j + 1) * HD)
            xj = x_ref[:, sl]
            if is_rope(j):
                xj = _rotate(xj.astype(F32), c_ref[...], s_ref[...], half)
            o_ref[:, sl] = xj.astype(o_ref.dtype)

    tab = pl.BlockSpec((tb, HD), lambda i: (i, 0))
    return pl.pallas_call(
        body, name=name, grid=(T // tb,),
        in_specs=[pl.BlockSpec((tb, W), lambda i: (i, window)), tab, tab],
        out_specs=pl.BlockSpec((tb, W), lambda i: (i, 0)),
        out_shape=jax.ShapeDtypeStruct((T, W), out_dtype),
        compiler_params=_params("parallel"),
    )(x, cos_t, sin_t)


DSWA_TB = 2048


def _deinterleave(src, dst_ref, d, dtype):
    rows = src.shape[0] // d
    for r in range(d):
        dst_ref[r] = src[pl.ds(r, rows, stride=d), :].astype(dtype)


def _rope_dswa(proj, cos_t, sin_t, name, comm=None):
    T = proj.shape[0]
    tb = _tile(T, 2 * DSWA_TB)
    half = ROT_A // 2

    def body(x_ref, c_ref, s_ref, *rest):
        outs, scr = rest[:-1], rest[-1]
        j = pl.program_id(1)

        @pl.when(j < 2 * HEADS)
        def _():
            scr[...] = _rotate(x_ref[...], c_ref[...], s_ref[...], half)

        @pl.when(j >= 2 * HEADS)
        def _():
            scr[...] = x_ref[...]

        for o_ref, d in zip(outs, DSWA_DILATIONS):
            _deinterleave(scr, o_ref, d, BF16)

    blk = pl.BlockSpec((tb, HD), lambda i, j: (i, j))
    tab = pl.BlockSpec((tb, HD), lambda i, j: (i, 0))
    return _call(
        body, name, (T // tb, 3 * HEADS), [blk, tab, tab],
        [pl.BlockSpec((d, tb // d, HD), lambda i, j: (0, i, j)) for d in DSWA_DILATIONS],
        [jax.ShapeDtypeStruct((d, T // d, 3 * AW), BF16) for d in DSWA_DILATIONS], (proj, cos_t, sin_t),
        scratch=[pltpu.VMEM((tb, HD), F32)], sem=("parallel", "parallel"), comm=comm)


def _shared_key_grad(dk, cos_t, sin_t_neg, half, name, into, tb=512):
    T = dk.shape[0]
    tb = _tile(T, tb)
    buf, buf_block = into

    def body(d_ref, c_ref, s_ref, buf_ref, o_ref):
        tot = d_ref[:, HD:2 * HD]
        for h in range(1, HEADS):
            tot = tot + d_ref[:, h * QK + HD:(h + 1) * QK]
        o_ref[...] = _rotate(tot, c_ref[...], s_ref[...], half).astype(o_ref.dtype)

    tab = pl.BlockSpec((tb, HD), lambda i: (i, 0))
    return pl.pallas_call(
        body, name=name, grid=(T // tb,),
        in_specs=[pl.BlockSpec((tb, HEADS * QK), lambda i: (i, 0)), tab, tab, pl.BlockSpec(memory_space=pl.ANY)],
        out_specs=pl.BlockSpec((tb, HD), lambda i: (i, buf_block)),
        out_shape=jax.ShapeDtypeStruct(buf.shape, buf.dtype), input_output_aliases={3: 0},
        compiler_params=_params("parallel"),
    )(dk, cos_t, sin_t_neg, buf)


def _band_mask(n):
    row = lax.broadcasted_iota(jnp.int32, (QBLK, 2 * QBLK), 0)
    col = lax.broadcasted_iota(jnp.int32, (QBLK, 2 * QBLK), 1)
    in_prev = jnp.logical_and(jnp.logical_and(col < QBLK, col >= row), n > 0)
    in_cur = jnp.logical_and(col >= QBLK, col - QBLK <= row)
    return jnp.logical_or(in_prev, in_cur)


def _dswa_specs(nb, reverse=False):
    pos = (lambda n: nb - 1 - n) if reverse else (lambda n: n)
    cur = lambda c: pl.BlockSpec((None, QBLK, AW), lambda r, n: (r, pos(n), c))
    prev = lambda c: pl.BlockSpec((None, QBLK, AW), lambda r, n: (r, jnp.maximum(pos(n) - 1, 0), c))
    stat = pl.BlockSpec((None, QBLK, HD), lambda r, n: (r, pos(n), 0))
    return cur, prev, stat


def _relayout_spec(d, tb, per_head=True):
    if per_head:
        return pl.BlockSpec((d, tb // d, HD), lambda i, h: (0, i, h))
    return pl.BlockSpec((d, tb // d, HD), lambda i, h: (0, i, 0))


def _head_lane(x, h):
    lane = lax.broadcasted_iota(jnp.int32, x.shape, 1)
    return jnp.sum(jnp.where(lane == h, x, 0.0), axis=-1, keepdims=True)


def _dswa_fwd(qkv, name, comm=None):
    d, sd = qkv.shape[:2]
    nb = sd // QBLK

    def body(q_ref, kc_ref, kp_ref, vc_ref, vp_ref, o_ref, l_ref):
        mask = _band_mask(pl.program_id(1))
        l_ref[...] = jnp.zeros_like(l_ref)
        for h in range(HEADS):
            sl = slice(h * HD, (h + 1) * HD)
            keys = jnp.concatenate([kp_ref[:, sl], kc_ref[:, sl]], axis=0)
            vals = jnp.concatenate([vp_ref[:, sl], vc_ref[:, sl]], axis=0)
            s = jnp.where(mask, _dot(q_ref[:, sl], keys, NT) * SCALE_A, NEG_INF)
            m = jnp.max(s, axis=-1, keepdims=True)
            p = jnp.exp(s - m)
            den = jnp.sum(p, axis=-1, keepdims=True)
            o_ref[:, sl] = _dot((p / den).astype(BF16), vals, NN)
            l_ref[:, h:h + 1] = m + jnp.log(den)

    cur, prev, stat = _dswa_specs(nb)
    return _call(
        body, name, (d, nb), [cur(0), cur(1), prev(1), cur(2), prev(2)], [cur(0), stat],
        [jax.ShapeDtypeStruct((d, sd, AW), F32), jax.ShapeDtypeStruct((d, sd, HD), F32)],
        (qkv, qkv, qkv, qkv, qkv), sem=("parallel", "parallel"), comm=comm)


def _dswa_merge(outs, lses, name, comm=None):
    nc = len(DSWA_DILATIONS)
    T = outs[0].shape[0] * outs[0].shape[1]
    tb = _tile(T, DSWA_TB)

    def body(*refs):
        o_refs, l_refs = refs[:nc], refs[nc:2 * nc]
        out_ref, outb_ref = refs[2 * nc:2 * nc + 2]
        lt_refs = refs[2 * nc + 2:3 * nc + 2]
        o_nat, l_nat, lt_nat = refs[3 * nc + 2:4 * nc + 2], refs[4 * nc + 2:5 * nc + 2], refs[-1]
        h = pl.program_id(1)
        for c, d in enumerate(DSWA_DILATIONS):
            for r in range(d):
                o_nat[c][pl.ds(r, tb // d, stride=d), :] = o_refs[c][r]
                l_nat[c][pl.ds(r, tb // d, stride=d), :] = l_refs[c][r]
        ls = [l[...] for l in l_nat]
        m = functools.reduce(jnp.maximum, ls)
        es = [jnp.exp(l - m) for l in ls]
        tot = functools.reduce(lambda a, b: a + b, es)
        acc = _head_lane(es[0] / tot, h) * o_nat[0][...]
        for c in range(1, nc):
            acc = acc + _head_lane(es[c] / tot, h) * o_nat[c][...]
        out_ref[...] = acc
        outb_ref[...] = acc.astype(BF16)
        lt_nat[...] = m + jnp.log(tot)
        for c, d in enumerate(DSWA_DILATIONS):
            _deinterleave(lt_nat, lt_refs[c], d, F32)

    nat = pl.BlockSpec((tb, HD), lambda i, h: (i, h))
    by_d = [_relayout_spec(d, tb) for d in DSWA_DILATIONS]
    stat_by_d = [_relayout_spec(d, tb, per_head=False) for d in DSWA_DILATIONS]
    res = _call(
        body, name, (T // tb, HEADS), by_d + stat_by_d, [nat, nat] + stat_by_d,
        [jax.ShapeDtypeStruct((T, AW), F32), jax.ShapeDtypeStruct((T, AW), BF16)]
        + [jax.ShapeDtypeStruct((d, T // d, HD), F32) for d in DSWA_DILATIONS], (*outs, *lses),
        scratch=[pltpu.VMEM((tb, HD), F32)] * (2 * nc + 1), sem=("parallel", "arbitrary"), comm=comm)
    return res[0], res[1], res[2:]


def _dswa_delta(dout, out, name):
    nc = len(DSWA_DILATIONS)
    T = out.shape[0]
    tb = _tile(T, DSWA_TB)

    def body(do_ref, o_ref, *rest):
        dl_refs, dob_refs, dl_nat = rest[:nc], rest[nc:2 * nc], rest[-1]
        h = pl.program_id(1)
        lane = lax.broadcasted_iota(jnp.int32, (tb, HD), 1)
        mine = jnp.where(lane == h, jnp.sum(do_ref[...] * o_ref[...], axis=-1, keepdims=True), 0.0)

        @pl.when(h == 0)
        def _():
            dl_nat[...] = mine

        @pl.when(h > 0)
        def _():
            dl_nat[...] += mine

        for c, d in enumerate(DSWA_DILATIONS):
            _deinterleave(dl_nat, dl_refs[c], d, F32)
            _deinterleave(do_ref, dob_refs[c], d, BF16)

    nat = pl.BlockSpec((tb, HD), lambda i, h: (i, h))
    by_d = [_relayout_spec(d, tb) for d in DSWA_DILATIONS]
    stat_by_d = [_relayout_spec(d, tb, per_head=False) for d in DSWA_DILATIONS]
    res = pl.pallas_call(
        body, name=name, grid=(T // tb, HEADS),
        in_specs=[nat, nat], out_specs=stat_by_d + by_d,
        out_shape=[jax.ShapeDtypeStruct((d, T // d, HD), F32) for d in DSWA_DILATIONS]
        + [jax.ShapeDtypeStruct((d, T // d, AW), BF16) for d in DSWA_DILATIONS],
        scratch_shapes=[pltpu.VMEM((tb, HD), F32)],
        compiler_params=_params("parallel", "arbitrary"),
    )(dout, out)
    return res[:nc], res[nc:]


def _delta_prep(dout, col_block, out, name, tb=256):
    T = out.shape[0]
    tb = _tile(T, tb)

    def body(do_ref, o_ref, delta_ref, dob_ref):
        delta_ref[...] = jnp.zeros_like(delta_ref)
        for h in range(HEADS):
            sl = slice(h * HD, (h + 1) * HD)
            doh = do_ref[:, sl]
            delta_ref[:, h:h + 1] = jnp.sum(doh * o_ref[:, sl], axis=-1, keepdims=True)
            dob_ref[:, sl] = doh.astype(BF16)

    row = pl.BlockSpec((tb, AW), lambda i: (i, 0))
    return pl.pallas_call(
        body, name=name, grid=(T // tb,),
        in_specs=[pl.BlockSpec((tb, AW), lambda i: (i, col_block)), row],
        out_specs=[pl.BlockSpec((tb, HD), lambda i: (i, 0)), row],
        out_shape=[jax.ShapeDtypeStruct((T, HD), F32), jax.ShapeDtypeStruct((T, AW), BF16)],
        compiler_params=_params("parallel"),
    )(dout, out)


def _dswa_bwd(qkv, dout_b, lse_tot, delta, name, comm=None):
    d, sd = qkv.shape[:2]
    nb = sd // QBLK

    def body(q_ref, kc_ref, kp_ref, vc_ref, vp_ref, do_ref, l_ref, dl_ref, g_ref, carry_k, carry_v):
        mask = _band_mask(nb - 1 - pl.program_id(1))

        @pl.when(pl.program_id(1) == 0)
        def _():
            carry_k[...] = jnp.zeros_like(carry_k)
            carry_v[...] = jnp.zeros_like(carry_v)

        for h in range(HEADS):
            sl = slice(h * HD, (h + 1) * HD)
            qh, doh = q_ref[:, sl], do_ref[:, sl]
            keys = jnp.concatenate([kp_ref[:, sl], kc_ref[:, sl]], axis=0)
            vals = jnp.concatenate([vp_ref[:, sl], vc_ref[:, sl]], axis=0)
            s = jnp.where(mask, _dot(qh, keys, NT) * SCALE_A, NEG_INF)
            p = jnp.exp(s - l_ref[:, h:h + 1])
            ds = (p * (_dot(doh, vals, NT) - dl_ref[:, h:h + 1]) * SCALE_A).astype(BF16)
            g_ref[:, sl] = _dot(ds, keys, NN).astype(BF16)
            dk = _dot(ds, qh, TN)
            dv = _dot(p.astype(BF16), doh, TN)
            g_ref[:, AW + h * HD:AW + (h + 1) * HD] = (dk[QBLK:] + carry_k[:, sl]).astype(BF16)
            g_ref[:, 2 * AW + h * HD:2 * AW + (h + 1) * HD] = (dv[QBLK:] + carry_v[:, sl]).astype(BF16)
            carry_k[:, sl] = dk[:QBLK]
            carry_v[:, sl] = dv[:QBLK]

    cur, prev, stat = _dswa_specs(nb, reverse=True)
    out_spec = pl.BlockSpec((None, QBLK, 3 * AW), lambda r, n: (r, nb - 1 - n, 0))
    return _call(
        body, name, (d, nb), [cur(0), cur(1), prev(1), cur(2), prev(2), cur(0), stat, stat], [out_spec],
        [jax.ShapeDtypeStruct((d, sd, 3 * AW), BF16)], (qkv, qkv, qkv, qkv, qkv, dout_b, lse_tot, delta),
        scratch=[pltpu.VMEM((QBLK, AW), F32)] * 2, sem=("parallel", "arbitrary"), comm=comm)[0]


def _dswa_combine(grads, cos_t, sin_t_neg, name, width):
    T = grads[0].shape[0] * grads[0].shape[1]
    tb = _tile(T, DSWA_TB)
    half = ROT_A // 2

    def body(*refs):
        g_refs = refs[:len(grads)]
        c_ref, s_ref, out_ref, acc = refs[len(grads):]
        j = pl.program_id(1)
        for c, d in enumerate(DSWA_DILATIONS):
            for r in range(d):
                if c == 0:
                    acc[...] = g_refs[c][r].astype(F32)
                else:
                    acc[pl.ds(r, tb // d, stride=d), :] += g_refs[c][r].astype(F32)
        val = acc[...]
        out_ref[...] = jnp.where(j < 2 * HEADS, _rotate(val, c_ref[...], s_ref[...], half), val).astype(BF16)

    tab = pl.BlockSpec((tb, HD), lambda i, j: (i, 0))
    return pl.pallas_call(
        body, name=name, grid=(T // tb, 3 * HEADS),
        in_specs=[pl.BlockSpec((d, tb // d, HD), lambda i, j: (0, i, j)) for d in DSWA_DILATIONS] + [tab, tab],
        out_specs=pl.BlockSpec((tb, HD), lambda i, j: (i, j)),
        out_shape=jax.ShapeDtypeStruct((T, width), BF16),
        scratch_shapes=[pltpu.VMEM((tb, HD), F32)],
        compiler_params=_params("parallel", "parallel"),
    )(*grads, cos_t, sin_t_neg)


MLA_TQ = 512
QK = 2 * HD
LOG2E = 1.4426950408889634


def _triangle(nq, key_major):
    pairs = [(q, k) for q in range(nq) for k in range(q + 1)]
    if key_major:
        pairs.sort(key=lambda p: (p[1], p[0]))
    return (jnp.array([p[0] for p in pairs], jnp.int32), jnp.array([p[1] for p in pairs], jnp.int32))


def _mla_specs(tq):
    q_spec = pl.BlockSpec((tq, HEADS * QK), lambda t, qi, ki: (qi[t], 0))
    k_spec = pl.BlockSpec((tq, HEADS * QK), lambda t, qi, ki: (ki[t], 0))
    v_spec = pl.BlockSpec((tq, AW), lambda t, qi, ki: (ki[t], 1))
    qrow = pl.BlockSpec((tq, AW), lambda t, qi, ki: (qi[t], 0))
    krow = pl.BlockSpec((tq, AW), lambda t, qi, ki: (ki[t], 0))
    return q_spec, k_spec, v_spec, qrow, krow


def _mla_pack(kv, kr, name, tb=512):
    T = kv.shape[0]
    tb = _tile(T, tb)

    def body(kv_ref, kr_ref, k_ref, v1_ref):
        lane = lax.broadcasted_iota(jnp.int32, (tb, HD), 1)
        one_hot = jnp.where(lane == 0, 1.0, 0.0).astype(BF16)
        for h in range(HEADS):
            k_ref[:, h * QK:h * QK + HD] = kv_ref[:, h * HD:(h + 1) * HD]
            k_ref[:, h * QK + HD:(h + 1) * QK] = kr_ref[...]
            v1_ref[:, h * QK:h * QK + HD] = kv_ref[:, AW + h * HD:AW + (h + 1) * HD]
            v1_ref[:, h * QK + HD:(h + 1) * QK] = one_hot

    wide = pl.BlockSpec((tb, HEADS * QK), lambda i: (i, 0))
    return pl.pallas_call(
        body, name=name, grid=(T // tb,),
        in_specs=[pl.BlockSpec((tb, 2 * AW), lambda i: (i, 0)), pl.BlockSpec((tb, HD), lambda i: (i, 0))],
        out_specs=[wide, wide], out_shape=[jax.ShapeDtypeStruct((T, HEADS * QK), BF16)] * 2,
        compiler_params=_params("parallel"),
    )(kv, kr)


def _mla_stat_spec(tq):
    return pl.BlockSpec((tq, HD), lambda t, qi, ki: (qi[t], 0))


def _mla_scores(q_ref, k_ref, h, qi, ki, tq):
    s = _dot(q_ref[:, h * QK:(h + 1) * QK], k_ref[:, h * QK:(h + 1) * QK], NT) * SCALE_B
    row = lax.broadcasted_iota(jnp.int32, s.shape, 0) + qi * tq
    col = lax.broadcasted_iota(jnp.int32, s.shape, 1) + ki * tq
    return jnp.where(col <= row, s, NEG_INF)


def _mla_fwd(q, k, v1, name, comm=None):
    T = q.shape[0]
    tq = _tile(T, MLA_TQ)
    tables = _triangle(T // tq, False)

    def body(qi_ref, ki_ref, q_ref, k_ref, v_ref, o_ref, ob_ref, l_ref, m_s, acc):
        t = pl.program_id(0)
        qi, ki = qi_ref[t], ki_ref[t]

        @pl.when(ki == 0)
        def _():
            m_s[...] = jnp.full_like(m_s, NEG_INF)
            acc[...] = jnp.zeros_like(acc)

        row = lax.broadcasted_iota(jnp.int32, (tq, tq), 0) + qi * tq
        col = lax.broadcasted_iota(jnp.int32, (tq, tq), 1) + ki * tq
        bias = jnp.where(col <= row, 0.0, NEG_INF)
        updates = []
        for h in range(HEADS):
            s = _dot(q_ref[:, h * QK:(h + 1) * QK], k_ref[:, h * QK:(h + 1) * QK], NT) + bias
            m_new = jnp.maximum(m_s[h], jnp.max(s, axis=-1, keepdims=True))
            p = jnp.exp2((s - m_new) * (SCALE_B * LOG2E)).astype(BF16)
            alpha = jnp.exp2((m_s[h] - m_new) * (SCALE_B * LOG2E))
            updates.append((m_new, alpha, _dot(p, v_ref[:, h * QK:(h + 1) * QK], NN)))
        for h, (m_new, alpha, pv) in enumerate(updates):
            acc[:, h * QK:(h + 1) * QK] = alpha * acc[:, h * QK:(h + 1) * QK] + pv
            m_s[h] = m_new

        @pl.when(ki == qi)
        def _():
            l_ref[...] = jnp.zeros_like(l_ref)
            for h in range(HEADS):
                sl = slice(h * HD, (h + 1) * HD)
                den = acc[:, h * QK + HD:h * QK + HD + 1]
                out = acc[:, h * QK:h * QK + HD] / den
                o_ref[:, sl] = out
                ob_ref[:, sl] = out.astype(BF16)
                l_ref[:, h:h + 1] = m_s[h] * SCALE_B + jnp.log(den)

    q_spec, k_spec, _, qrow, _ = _mla_specs(tq)
    return _call(
        body, name, (tables[0].shape[0],), [q_spec, k_spec, k_spec], [qrow, qrow, _mla_stat_spec(tq)],
        [jax.ShapeDtypeStruct((T, AW), F32), jax.ShapeDtypeStruct((T, AW), BF16), jax.ShapeDtypeStruct((T, HD), F32)],
        (q, k, v1),
        scratch=[pltpu.VMEM((HEADS, tq, 1), F32), pltpu.VMEM((tq, HEADS * QK), F32)],
        sem=("arbitrary",), comm=comm, prefetch=tables)


def _mla_ds(q_ref, k_ref, v_ref, do_ref, l_ref, dl_ref, h, qi, ki, tq):
    sl = slice(h * HD, (h + 1) * HD)
    p = jnp.exp(_mla_scores(q_ref, k_ref, h, qi, ki, tq) - l_ref[:, h:h + 1])
    ds = (p * (_dot(do_ref[:, sl], v_ref[:, sl], NT) - dl_ref[:, h:h + 1]) * SCALE_B).astype(BF16)
    return p, ds


def _mla_bwd_q(q, k, kv, dout_b, lse, delta, name, comm=None):
    T = q.shape[0]
    tq = _tile(T, MLA_TQ)
    tables = _triangle(T // tq, False)

    def body(qi_ref, ki_ref, q_ref, k_ref, v_ref, do_ref, l_ref, dl_ref, dq_ref):
        t = pl.program_id(0)
        qi, ki = qi_ref[t], ki_ref[t]

        @pl.when(ki == 0)
        def _():
            dq_ref[...] = jnp.zeros_like(dq_ref)

        for h in range(HEADS):
            _, ds = _mla_ds(q_ref, k_ref, v_ref, do_ref, l_ref, dl_ref, h, qi, ki, tq)
            dq_ref[:, h * QK:(h + 1) * QK] += _dot(ds, k_ref[:, h * QK:(h + 1) * QK], NN)

    q_spec, k_spec, v_spec, qrow, _ = _mla_specs(tq)
    return _call(
        body, name, (tables[0].shape[0],), [q_spec, k_spec, v_spec, qrow, _mla_stat_spec(tq), _mla_stat_spec(tq)], [q_spec],
        [jax.ShapeDtypeStruct((T, HEADS * QK), F32)], (q, k, kv, dout_b, lse, delta),
        sem=("arbitrary",), comm=comm, prefetch=tables)[0]


def _mla_bwd_kv(q, k, kv, dout_b, lse, delta, name, comm=None):
    T = q.shape[0]
    tq = _tile(T, MLA_TQ)
    nq = T // tq
    tables = _triangle(nq, True)

    def body(qi_ref, ki_ref, q_ref, k_ref, v_ref, do_ref, l_ref, dl_ref, dk_ref, dkv_ref, dv_acc):
        t = pl.program_id(0)
        qi, ki = qi_ref[t], ki_ref[t]

        @pl.when(qi == ki)
        def _():
            dk_ref[...] = jnp.zeros_like(dk_ref)
            dv_acc[...] = jnp.zeros_like(dv_acc)

        for h in range(HEADS):
            sl = slice(h * HD, (h + 1) * HD)
            p, ds = _mla_ds(q_ref, k_ref, v_ref, do_ref, l_ref, dl_ref, h, qi, ki, tq)
            dv_acc[:, sl] += _dot(p.astype(BF16), do_ref[:, sl], TN)
            dk_ref[:, h * QK:(h + 1) * QK] += _dot(ds, q_ref[:, h * QK:(h + 1) * QK], TN)

        @pl.when(qi == nq - 1)
        def _():
            for h in range(HEADS):
                dkv_ref[:, h * HD:(h + 1) * HD] = dk_ref[:, h * QK:h * QK + HD].astype(BF16)
                dkv_ref[:, AW + h * HD:AW + (h + 1) * HD] = dv_acc[:, h * HD:(h + 1) * HD].astype(BF16)

    q_spec, k_spec, v_spec, qrow, _ = _mla_specs(tq)
    return _call(
        body, name, (tables[0].shape[0],),
        [q_spec, k_spec, v_spec, qrow, _mla_stat_spec(tq), _mla_stat_spec(tq)],
        [k_spec, pl.BlockSpec((tq, 2 * AW), lambda t, qi, ki: (ki[t], 0))],
        [jax.ShapeDtypeStruct((T, HEADS * QK), F32), jax.ShapeDtypeStruct((T, 2 * AW), BF16)],
        (q, k, kv, dout_b, lse, delta), scratch=[pltpu.VMEM((tq, AW), F32)],
        sem=("arbitrary",), comm=comm, prefetch=tables)


def _pair_sum(by_device, from_sibling, name, tb=256):
    n_chip, R, C = from_sibling.shape
    tb = _tile(R, tb)
    core = jnp.reshape(lax.axis_index("c"), (1,)).astype(jnp.int32)

    def body(core_ref, mine_ref, theirs_ref, o_ref):
        o_ref[...] = (mine_ref[...].astype(F32) + theirs_ref[...].astype(F32)).astype(o_ref.dtype)

    blk = pl.BlockSpec((None, tb, C), lambda p, i, core_ref: (p, i, 0))
    return _call(
        body, name, (n_chip, R // tb),
        [pl.BlockSpec((None, tb, C), lambda p, i, core_ref: (2 * p + core_ref[0], i, 0)), blk], [blk],
        [jax.ShapeDtypeStruct((n_chip, R, C), BF16)], (by_device, from_sibling),
        sem=("parallel", "parallel"), prefetch=(core,))[0]


def _adamw(parts, w, m, v, name, tb=128, comm=None):
    R, C = w.shape
    n_parts = parts.shape[0]
    tb = _tile(R, tb)
    c1 = 1.0 - ADAM_B1
    c2 = 1.0 - ADAM_B2
    bc1 = 1.0 - ADAM_B1 ** ADAM_STEP
    bc2 = 1.0 - ADAM_B2 ** ADAM_STEP

    def body(p_ref, w_ref, m_ref, v_ref, g_ref, d_ref, nm_ref, nv_ref):
        g = p_ref[0].astype(F32)
        for j in range(1, n_parts):
            g = g + p_ref[j].astype(F32)
        nm = ADAM_B1 * m_ref[...] + c1 * g
        nv = ADAM_B2 * v_ref[...] + c2 * (g * g)
        g_ref[...] = g
        nm_ref[...] = nm
        nv_ref[...] = nv
        d_ref[...] = -ADAM_LR * ((nm / bc1) / (jnp.sqrt(nv / bc2) + ADAM_EPS) + ADAM_WD * w_ref[...])

    row = pl.BlockSpec((tb, C), lambda i: (i, 0))
    return _call(
        body, name, (R // tb,), [pl.BlockSpec((n_parts, tb, C), lambda i: (0, i, 0)), row, row, row], [row] * 4,
        [jax.ShapeDtypeStruct((R, C), F32)] * 4, (parts, w, m, v), sem=("parallel",), comm=comm)


def _cols_from_shards(g):
    return jnp.transpose(g, (1, 0, 2)).reshape(g.shape[1], N_DEV * g.shape[2])


def _cols_to_shards(w):
    return jnp.transpose(w.reshape(w.shape[0], N_DEV, w.shape[1] // N_DEV), (1, 0, 2))


def _split_heads(w, first):
    w3 = w.reshape(w.shape[0], HEADS, -1)
    return w3[:, :, :first].reshape(w.shape[0], -1), w3[:, :, first:].reshape(w.shape[0], -1)


def _join_heads(a, b):
    R = a.shape[0]
    return jnp.concatenate([a.reshape(R, HEADS, -1), b.reshape(R, HEADS, -1)], axis=2).reshape(R, -1)


def _join_shards_padded(g, width, name, tb=256):
    n, R, cs = g.shape
    tb = _tile(R, tb)

    def body(g_ref, o_ref):
        for j in range(n):
            o_ref[:, j * cs:(j + 1) * cs] = g_ref[j]
        o_ref[:, n * cs:] = jnp.zeros((tb, width - n * cs), o_ref.dtype)

    return pl.pallas_call(
        body, name=name, grid=(R // tb,),
        in_specs=[pl.BlockSpec((n, tb, cs), lambda i: (0, i, 0))],
        out_specs=pl.BlockSpec((tb, width), lambda i: (i, 0)),
        out_shape=jax.ShapeDtypeStruct((R, width), g.dtype),
        compiler_params=_params("parallel"),
    )(g)


def _split_shards(w, cs, name, tb=256):
    R, width = w.shape
    tb = _tile(R, tb)

    def body(w_ref, o_ref):
        for j in range(N_DEV):
            o_ref[j] = w_ref[:, j * cs:(j + 1) * cs]

    return pl.pallas_call(
        body, name=name, grid=(R // tb,),
        in_specs=[pl.BlockSpec((tb, width), lambda i: (i, 0))],
        out_specs=pl.BlockSpec((N_DEV, tb, cs), lambda i: (0, i, 0)),
        out_shape=jax.ShapeDtypeStruct((N_DEV, R, cs), w.dtype),
        compiler_params=_params("parallel"),
    )(w)


def _pad_heads(w, width):
    w3 = w.reshape(w.shape[0], HEADS, -1)
    return jnp.pad(w3, ((0, 0), (0, 0), (0, width - w3.shape[2]))).reshape(w.shape[0], HEADS * width)


def _unpad_heads(w, k):
    return w.reshape(w.shape[0], HEADS, -1)[:, :, :k].reshape(w.shape[0], HEADS * k)


def kernel(x, positions, norm_attn_pre, norm_attn_post, w_in, q_latent_norm, kv_latent_norm, w_uq, w_ukv, w_out, norm_mlp_pre, norm_mlp_post, w_up, w_down, loss_target, m_norm_attn_pre, m_norm_attn_post, m_w_in, m_q_latent_norm, m_kv_latent_norm, m_w_uq, m_w_ukv, m_w_out, m_norm_mlp_pre, m_norm_mlp_post, m_w_up, m_w_down, v_norm_attn_pre, v_norm_attn_post, v_w_in, v_q_latent_norm, v_kv_latent_norm, v_w_uq, v_w_ukv, v_w_out, v_norm_mlp_pre, v_norm_mlp_post, v_w_up, v_w_down):
    xs = x[0]
    tgt = loss_target[0]
    pos = positions[0]
    T, D = xs.shape
    big = dict(w_in=(w_in, m_w_in, v_w_in), w_uq=(w_uq, m_w_uq, v_w_uq), w_ukv=(w_ukv, m_w_ukv, v_w_ukv),
               w_out=(w_out, m_w_out, v_w_out), w_up=(w_up, m_w_up, v_w_up), w_down=(w_down, m_w_down, v_w_down))
    big = {n: tuple(t[0] for t in ts) for n, ts in big.items()}
    big_names = ["w_in", "w_uq", "w_ukv", "w_out", "w_up", "w_down"]
    col_sharded = {"w_in", "w_uq", "w_ukv", "w_up"}

    wb = {n: big[n][0].astype(BF16) for n in big_names}

    def gathered(ex, i, n):
        g = ex.results[i]
        return _cols_from_shards(g) if n in col_sharded else g.reshape(-1, g.shape[2])

    def by_device(g, n):
        return _cols_to_shards(g) if n in col_sharded else g.reshape(N_DEV, g.shape[0] // N_DEV, g.shape[1])

    def scatter_of(g, n):
        return _Exchange([by_device(g, n)], "scatter")

    cos_a, sin_a = _rope_tables(pos, ROT_A)
    cos_b, sin_b = _rope_tables(pos, ROPE_MLA)

    ex_in = _Exchange([wb["w_in"]], "gather")
    h1 = _rms_fwd(xs, norm_attn_pre, BF16, "norm_attn_pre_fwd", comm=ex_in)
    Wi = _join_shards_padded(ex_in.results[0], IN_PAD, "w_in_join")
    ex_mid = _Exchange([wb["w_uq"], wb["w_ukv"], wb["w_out"]], "gather")
    proj = _mm(h1, Wi, "nn", [F32], "proj_in", tn=1408, comm=ex_mid)
    Wuq = _pad_heads(gathered(ex_mid, 0, "w_uq"), QK)
    Wukv = jnp.concatenate(_split_heads(gathered(ex_mid, 1, "w_ukv"), HD), axis=1)
    Wo = gathered(ex_mid, 2, "w_out")
    n_piece = wb["w_down"].shape[0] // 8
    ex_down = [_Exchange([wb["w_down"][i * n_piece:(i + 1) * n_piece]], "gather") for i in range(8)]
    qkv_by_d = _rope_dswa(proj, cos_a, sin_a, "rope_dswa", comm=ex_down[0])
    outs, lses = [], []
    for d, qkv, ex in zip(DSWA_DILATIONS, qkv_by_d, ex_down[1:4]):
        o, l = _dswa_fwd(qkv, f"dswa_fwd_d{d}", comm=ex)
        outs.append(o)
        lses.append(l)
    a_out, a_out_b, a_lse_by_d = _dswa_merge(outs, lses, "dswa_merge", comm=ex_down[4])

    cqn = _rms_fwd(proj, q_latent_norm, BF16, "q_latent_norm_fwd", width=Q_LORA, col_block=3 * AW // Q_LORA)
    ckvn = _rms_fwd(proj, kv_latent_norm, BF16, "kv_latent_norm_fwd", width=KV_LORA, col_block=3 * AW // KV_LORA + 1)
    qb = _mm(cqn, Wuq, "nn", [F32], "q_up")
    kvb = _mm(ckvn, Wukv, "nn", [BF16], "kv_up")
    odd = lambda j: j % 2 == 1
    q_mla = _rope_apply(qb, cos_b, sin_b, ROPE_MLA // 2, 2 * HEADS, odd, BF16, "rope_mla_q")
    kr = _rope_apply(proj, cos_b, sin_b, ROPE_MLA // 2, 1, lambda j: True, BF16, "rope_mla_k",
                     window=(IN_PAD - HD) // HD)
    k_mla, v1_mla = _mla_pack(kvb, kr, "mla_pack")
    ex_up = _Exchange([wb["w_up"]], "gather")
    b_out, b_out_b, b_lse = _mla_fwd(q_mla, k_mla, v1_mla, "mla_fwd", comm=ex_up)
    Wup_shards = ex_up.results[0]

    mixed = jnp.concatenate([a_out_b, b_out_b], axis=1)
    y1 = _mm(mixed, Wo, "nn", [F32], "attn_out", comm=ex_down[5])

    x2, h2 = _norm_pair_fwd(y1, xs, norm_attn_post, norm_mlp_pre, "norm_attn_post_mlp_pre_fwd")

    def relu2(z):
        r = jnp.maximum(z, 0.0)
        return r * r, r

    u, zr = _mm(h2, Wup_shards, "nn", [BF16, BF16], "mlp_up", epilogue=relu2, comm=ex_down[6:8],
                b_shards=True)
    Wdn = jnp.concatenate([ex.results[0] for ex in ex_down], axis=1).reshape(-1, D)
    y2 = _mm(u, Wdn, "nn", [F32], "mlp_down")
    dx3, dy2, loss_part, dg_mlp_post = _loss_head(x2, y2, norm_mlp_post, tgt, "loss_head")

    dz =_mm(dy2, Wdn, "nt", [BF16], "mlp_down_dx", epilogue=lambda du, r: (du * (2.0 * r.astype(F32)),), extras=(zr,))
    g_down = _mm(u, dy2, "tn", [BF16], "mlp_down_dw")
    down_dev = by_device(g_down, "w_down")
    pair_down = _Exchange([down_dev], "pair")
    up_dev = _mm(h2, dz, "tn", [BF16], "mlp_up_dw", comm=pair_down, out_shards=True)
    down_chip = _pair_sum(down_dev, pair_down.results[0], "pair_sum_w_down")
    pair_up = _Exchange([up_dev], "pair")
    cut = 5 * down_chip.shape[1] // 8
    sc_down = [_Exchange([down_chip[:, :cut]], "chips"), _Exchange([down_chip[:, cut:]], "chips")]
    dh2 = _mm(dz, Wup_shards, "nt", [F32], "mlp_up_dx", comm=[pair_up, sc_down[0]], b_shards=True)
    up_chip = _pair_sum(up_dev, pair_up.results[0], "pair_sum_w_up")

    dx2, dy1, dg_mlp_pre, dg_attn_post = _norm_pair_bwd(dh2, x2, norm_mlp_pre, dx3, y1, norm_attn_post,
                                                        "norm_mlp_pre_attn_post_bwd")
    dmixed = _mm(dy1, Wo, "nt", [F32], "attn_out_dx")
    g_out = _mm(mixed, dy1, "tn", [BF16], "attn_out_dw")

    b_delta, b_dout = _delta_prep(dmixed, 1, b_out, "mla_delta")
    sc_up = _Exchange([up_chip], "chips")
    dq_mla = _mla_bwd_q(q_mla, k_mla, kvb, b_dout, b_lse, b_delta, "mla_bwd_q", comm=sc_up)
    sc_out = scatter_of(g_out, "w_out")
    dk_mla, dkvb = _mla_bwd_kv(q_mla, k_mla, kvb, b_dout, b_lse, b_delta, "mla_bwd_kv", comm=[sc_out, sc_down[1]])
    dqb = _rope_apply(dq_mla, cos_b, -sin_b, ROPE_MLA // 2, 2 * HEADS, odd, BF16, "rope_mla_q_bwd")
    g_uq_pad = _mm(cqn, dqb, "tn", [BF16], "q_up_dw")
    g_ukv_perm = _mm(ckvn, dkvb, "tn", [BF16], "kv_up_dw")
    dcqn = _mm(dqb, Wuq, "nt", [F32], "q_up_dx")
    dckvn = _mm(dkvb, Wukv, "nt", [F32], "kv_up_dx")

    g_uq = _unpad_heads(g_uq_pad, HD + ROPE_MLA)
    g_ukv = _join_heads(g_ukv_perm[:, :AW], g_ukv_perm[:, AW:])
    sc_uq = _Exchange([_cols_to_shards(g_uq), _cols_to_shards(g_ukv)], "scatter")
    a_delta_by_d, a_dout_by_d = _dswa_delta(dmixed, a_out, "dswa_delta")
    a_grads = [_dswa_bwd(qkv_by_d[c], a_dout_by_d[c], a_lse_by_d[c], a_delta_by_d[c], f"dswa_bwd_d{d}")
               for c, d in enumerate(DSWA_DILATIONS)]
    dproj = _dswa_combine(a_grads, cos_a, -sin_a, "dswa_combine", IN_PAD)
    dproj, dg_q = _rms_bwd(dcqn, proj, q_latent_norm, BF16, "q_latent_norm_bwd", width=Q_LORA,
                           col_block=3 * AW // Q_LORA, into=(dproj, 3 * AW // Q_LORA))
    dproj, dg_kv = _rms_bwd(dckvn, proj, kv_latent_norm, BF16, "kv_latent_norm_bwd", width=KV_LORA,
                            col_block=3 * AW // KV_LORA + 1, into=(dproj, 3 * AW // KV_LORA + 1))
    dproj = _shared_key_grad(dk_mla, cos_b, -sin_b, ROPE_MLA // 2, "rope_mla_k_bwd", (dproj, (IN_PAD - HD) // HD))
    g_in_pad = _mm(h1, dproj, "tn", [BF16], "proj_in_dw", tn=1408, comm=sc_uq)
    in_dev = _split_shards(g_in_pad, IN_COLS // N_DEV, "w_in_grad_split")
    pair_in = _Exchange([in_dev], "pair").standalone("pair_w_in")
    sc_in = _Exchange([_pair_sum(in_dev, pair_in[0], "pair_sum_w_in")], "chips")
    dh1 = _mm(dproj, Wi, "nt", [F32], "proj_in_dx", comm=sc_in)
    grad_x, dg_attn_pre = _rms_bwd(dh1, xs, norm_attn_pre, F32, "norm_attn_pre_bwd", residual=dx2)

    parts = dict(w_in=sc_in.results[0], w_uq=sc_uq.results[0], w_ukv=sc_uq.results[1], w_out=sc_out.results[0],
                 w_up=sc_up.results[0], w_down=jnp.concatenate([sc.results[0] for sc in sc_down], axis=1))
    big_out = {n: _adamw(parts[n], *big[n], f"adamw_{n}") for n in big_names}

    gain_names = ["norm_attn_pre", "norm_attn_post", "q_latent_norm", "kv_latent_norm", "norm_mlp_pre", "norm_mlp_post"]
    gain_args = dict(norm_attn_pre=(norm_attn_pre, m_norm_attn_pre, v_norm_attn_pre),
                     norm_attn_post=(norm_attn_post, m_norm_attn_post, v_norm_attn_post),
                     q_latent_norm=(q_latent_norm, m_q_latent_norm, v_q_latent_norm),
                     kv_latent_norm=(kv_latent_norm, m_kv_latent_norm, v_kv_latent_norm),
                     norm_mlp_pre=(norm_mlp_pre, m_norm_mlp_pre, v_norm_mlp_pre),
                     norm_mlp_post=(norm_mlp_post, m_norm_mlp_post, v_norm_mlp_post))
    gain_grads = dict(norm_attn_pre=dg_attn_pre, norm_attn_post=dg_attn_post, q_latent_norm=dg_q,
                      kv_latent_norm=dg_kv, norm_mlp_pre=dg_mlp_pre, norm_mlp_post=dg_mlp_post)
    packed = jnp.concatenate([gain_grads[n] for n in gain_names], axis=1)
    gain_parts = _Exchange([packed], "gather").standalone("gather_gain_grads")[0]
    pack3 = lambda i: jnp.concatenate([gain_args[n][i] for n in gain_names], axis=1)
    gain_out = _adamw(gain_parts, pack3(0), pack3(1), pack3(2), "adamw_gains", tb=1)
    offs = [0]
    for n in gain_names:
        offs.append(offs[-1] + gain_args[n][0].shape[1])
    small_out = {n: tuple(o[:, offs[i]:offs[i + 1]] for o in gain_out) for i, n in enumerate(gain_names)}

    loss = lax.psum(loss_part[0, 0], ("x", "y", "c"))

    order = ["norm_attn_pre", "norm_attn_post", "w_in", "q_latent_norm", "kv_latent_norm", "w_uq", "w_ukv", "w_out",
             "norm_mlp_pre", "norm_mlp_post", "w_up", "w_down"]
    res = {n: (small_out[n] if n in small_out else tuple(o[None] for o in big_out[n])) for n in order}
    return (loss, grad_x[None], *[res[n][0] for n in order], *[res[n][1] for n in order],
            *[res[n][2] for n in order], *[res[n][3] for n in order])
```

```python
import functools
import math

import jax
import jax.numpy as jnp
from jax import lax
from jax.experimental import pallas as pl
from jax.experimental.pallas import tpu as pltpu

F32 = jnp.float32
BF16 = jnp.bfloat16

N_DEV = 8
HEADS = 8
HD = 128
AW = HEADS * HD
Q_LORA = 512
KV_LORA = 512
ROPE_MLA = 64
ROT_A = 32
IN_COLS = 3 * AW + Q_LORA + KV_LORA + ROPE_MLA
IN_PAD = 3 * AW + Q_LORA + KV_LORA + HD
QBLK = 128
DSWA_DILATIONS = (1, 4, 16)
ROPE_THETA = 500000.0
NORM_EPS = 1e-6
NEG_INF = -1e30
SCALE_A = HD ** -0.5
SCALE_B = (HD + ROPE_MLA) ** -0.5

ADAM_LR = 0.001
ADAM_B1 = 0.9
ADAM_B2 = 0.999
ADAM_EPS = 1e-08
ADAM_WD = 0.01
ADAM_STEP = 10

VMEM_LIMIT = 48 * 1024 * 1024

NT = (((1,), (1,)), ((), ()))
NN = (((1,), (0,)), ((), ()))
TN = (((0,), (0,)), ((), ()))


def _dot(a, b, dims):
    return lax.dot_general(a, b, dims, preferred_element_type=F32)


def _params(*sem):
    return pltpu.CompilerParams(dimension_semantics=sem, vmem_limit_bytes=VMEM_LIMIT)


def _tile(n, want):
    t = min(n, want)
    while n % t:
        t //= 2
    return t


def _tile128(n, want):
    if n % 128:
        return n
    units = n // 128
    return 128 * max(u for u in range(1, max(want // 128, 1) + 1) if units % u == 0)


class _Exchange:
    def __init__(self, arrs, mode):
        self.arrs = list(arrs)
        self.mode = mode
        self.n = len(self.arrs)
        self.results = None
        hbm = pl.BlockSpec(memory_space=pltpu.HBM)
        self.specs = [hbm] * self.n
        shape = {"gather": lambda a: (N_DEV,) + a.shape, "scatter": lambda a: a.shape,
                 "pair": lambda a: (4,) + a.shape[1:], "chips": lambda a: a.shape}[mode]
        self.out_shape = [jax.ShapeDtypeStruct(shape(a), a.dtype) for a in self.arrs]
        n_sem = self.n * (N_DEV - 1)
        self.scratch = [pltpu.SemaphoreType.DMA((n_sem,)), pltpu.SemaphoreType.DMA((n_sem,)),
                        pltpu.SemaphoreType.DMA((self.n,))]

    def hooks(self, ins, outs, send_sems, recv_sems, local_sems):
        x, y, c = lax.axis_index("x"), lax.axis_index("y"), lax.axis_index("c")
        me = (x, y, c)
        sib = (x, y, 1 - c)
        chips = [(1 - x, y), (x, 1 - y), (1 - x, 1 - y)]
        slot = lambda p: 4 * p[0] + 2 * p[1] + p[2]
        chip_of = lambda p: 2 * p[0] + p[1]

        def rcopy(a, k, src, dst, to):
            i = a * (N_DEV - 1) + k
            return pltpu.make_async_remote_copy(src_ref=src, dst_ref=dst, send_sem=send_sems.at[i],
                                                recv_sem=recv_sems.at[i], device_id=to,
                                                device_id_type=pl.DeviceIdType.MESH)

        def local(a):
            if self.mode == "chips":
                return pltpu.make_async_copy(ins[a].at[chip_of(me)], outs[a].at[chip_of(me)], local_sems.at[a])
            src = ins[a].at[slot(me)] if self.mode == "scatter" else ins[a]
            return pltpu.make_async_copy(src, outs[a].at[slot(me)], local_sems.at[a])

        def peer(rel):
            return (1 - x if rel & 4 else x, 1 - y if rel & 2 else y, 1 - c if rel & 1 else c)

        if self.mode == "pair":
            def start():
                for a in range(self.n):
                    for p in range(4):
                        rcopy(a, p, ins[a].at[2 * p + 1 - c], outs[a].at[p], sib).start()

            def middle():
                pass

            def finish():
                for a in range(self.n):
                    for p in range(4):
                        cp = rcopy(a, p, ins[a].at[2 * p + 1 - c], outs[a].at[p], sib)
                        cp.wait_send()
                        cp.wait_recv()
        elif self.mode == "chips":
            def start():
                for a in range(self.n):
                    local(a).start()
                    for j, chip in enumerate(chips):
                        rcopy(a, j, ins[a].at[chip_of(chip)], outs[a].at[chip_of(me)], (*chip, c)).start()

            def middle():
                pass

            def finish():
                for a in range(self.n):
                    for j, chip in enumerate(chips):
                        cp = rcopy(a, j, ins[a].at[chip_of(chip)], outs[a].at[chip_of(chip)], (*chip, c))
                        cp.wait_send()
                        cp.wait_recv()
                    local(a).wait()
        elif self.mode == "scatter":
            def start():
                for a in range(self.n):
                    local(a).start()
                    for rel in range(1, N_DEV):
                        rcopy(a, rel - 1, ins[a].at[slot(peer(rel))], outs[a].at[slot(me)], peer(rel)).start()

            def middle():
                pass

            def finish():
                for a in range(self.n):
                    for rel in range(1, N_DEV):
                        cp = rcopy(a, rel - 1, ins[a].at[slot(peer(rel))], outs[a].at[slot(peer(rel))], peer(rel))
                        cp.wait_send()
                        cp.wait_recv()
                    local(a).wait()
        else:
            def start():
                for a in range(self.n):
                    local(a).start()
                    rcopy(a, 0, ins[a], outs[a].at[slot(me)], sib).start()
                    for j, chip in enumerate(chips):
                        rcopy(a, 1 + j, ins[a], outs[a].at[slot(me)], (*chip, c)).start()

            def middle():
                for a in range(self.n):
                    for j, chip in enumerate(chips):
                        landed = outs[a].at[slot((*chip, c))]
                        rcopy(a, 1 + j, ins[a], landed, me).wait_recv()
                        rcopy(a, 4 + j, landed, landed, sib).start()

            def finish():
                for a in range(self.n):
                    rcopy(a, 0, ins[a], outs[a].at[slot(sib)], me).wait_recv()
                    for j, chip in enumerate(chips):
                        rcopy(a, 4 + j, ins[a], outs[a].at[slot((*chip, 1 - c))], me).wait_recv()
                    for k in range(N_DEV - 1):
                        rcopy(a, k, ins[a], outs[a].at[slot(me)], me).wait_send()
                    local(a).wait()

        return start, middle, finish

    def set_results(self, res):
        self.results = list(res)

    def standalone(self, name):
        n = self.n

        def body(*refs):
            start, middle, finish = self.hooks(refs[:n], refs[n:2 * n], *refs[2 * n:])
            start()
            middle()
            finish()

        self.results = pl.pallas_call(
            body, name=name, in_specs=self.specs, out_specs=self.specs, out_shape=self.out_shape,
            scratch_shapes=self.scratch, compiler_params=pltpu.CompilerParams(has_side_effects=True),
        )(*self.arrs)
        return self.results


class _Carried:
    def __init__(self, parts):
        self.parts = list(parts)
        self.n = sum(p.n for p in self.parts)
        self.arrs = [a for p in self.parts for a in p.arrs]
        self.specs = [s for p in self.parts for s in p.specs]
        self.out_shape = [s for p in self.parts for s in p.out_shape]
        self.scratch = [s for p in self.parts for s in p.scratch]

    def hooks(self, ins, outs, *sems):
        hooks, i = [], 0
        for j, p in enumerate(self.parts):
            hooks.append(p.hooks(ins[i:i + p.n], outs[i:i + p.n], *sems[3 * j:3 * j + 3]))
            i += p.n
        def phase(k):
            def run():
                for h in hooks:
                    h[k]()
            return run

        return phase(0), phase(1), phase(2)

    def set_results(self, res):
        i = 0
        for p in self.parts:
            p.set_results(res[i:i + p.n])
            i += p.n


def _call(body, name, grid, in_specs, out_specs, out_shape, args, scratch=(), sem=(), comm=None, prefetch=(),
          aliases=None):
    npf = len(prefetch)
    if isinstance(comm, (list, tuple)):
        comm = _Carried(comm)
    if comm is None:
        spec = pltpu.PrefetchScalarGridSpec(num_scalar_prefetch=npf, grid=grid, in_specs=list(in_specs),
                                            out_specs=list(out_specs), scratch_shapes=list(scratch))
        return pl.pallas_call(body, name=name, grid_spec=spec, out_shape=list(out_shape),
                              input_output_aliases=aliases or {},
                              compiler_params=_params(*sem))(*prefetch, *args)
    assert aliases is None
    ni, no, ns, n = len(in_specs), len(out_specs), len(scratch), comm.n
    steps = math.prod(grid)

    def wrapped(*refs):
        pf, refs = refs[:npf], refs[npf:]
        ins, c_ins = refs[:ni], refs[ni:ni + n]
        outs, c_outs = refs[ni + n:ni + n + no], refs[ni + n + no:ni + 2 * n + no]
        scr, c_scr = refs[ni + 2 * n + no:ni + 2 * n + no + ns], refs[ni + 2 * n + no + ns:]
        start, middle, finish = comm.hooks(c_ins, c_outs, *c_scr)
        step = pl.program_id(0)
        for ax in range(1, len(grid)):
            step = step * grid[ax] + pl.program_id(ax)
        pl.when(step == 0)(start)
        pl.when(step == steps // 2)(middle)
        body(*pf, *ins, *outs, *scr)
        pl.when(step == steps - 1)(finish)

    spec = pltpu.PrefetchScalarGridSpec(num_scalar_prefetch=npf, grid=grid, in_specs=list(in_specs) + comm.specs,
                                        out_specs=list(out_specs) + comm.specs,
                                        scratch_shapes=list(scratch) + comm.scratch)
    res = pl.pallas_call(
        wrapped, name=name, grid_spec=spec, out_shape=list(out_shape) + comm.out_shape,
        compiler_params=pltpu.CompilerParams(dimension_semantics=("arbitrary",) * len(grid),
                                             vmem_limit_bytes=VMEM_LIMIT, has_side_effects=True),
    )(*prefetch, *args, *comm.arrs)
    comm.set_results(res[no:])
    return res[:no]


def _mm(a, b, mode, out_dtypes, name, epilogue=None, extras=(), tm=1024, tn=1024, tk=2048, comm=None,
        b_shards=False, out_shards=False):
    if mode == "tn":
        K, M = a.shape
    else:
        M, K = a.shape
    if b_shards:
        N = b.shape[1] if mode == "nt" else N_DEV * b.shape[2]
    else:
        N = b.shape[0] if mode == "nt" else b.shape[1]
    tm, tn, tk = _tile128(M, tm), _tile128(N, tn), _tile128(K, tk)
    pair_k = b_shards and mode == "nt"
    if pair_k:
        tk = 2 * K // N_DEV
        b = b.reshape(N_DEV // 2, 2, *b.shape[1:])
    elif b_shards or out_shards:
        tn = N // N_DEV
    nk = K // tk
    dims = {"nn": NN, "nt": NT, "tn": TN}[mode]
    a_spec = (pl.BlockSpec((tk, tm), lambda i, j, k: (k, i)) if mode == "tn"
              else pl.BlockSpec((tm, tk), lambda i, j, k: (i, k)))
    if b_shards:
        b_spec = (pl.BlockSpec((None, 2, tn, tk // 2), lambda i, j, k: (k, 0, j, 0)) if mode == "nt"
                  else pl.BlockSpec((None, tk, tn), lambda i, j, k: (j, k, 0)))
    else:
        b_spec = (pl.BlockSpec((tn, tk), lambda i, j, k: (j, k)) if mode == "nt"
                  else pl.BlockSpec((tk, tn), lambda i, j, k: (k, j)))
    mn_spec = pl.BlockSpec((tm, tn), lambda i, j, k: (i, j))
    out_spec = pl.BlockSpec((None, tm, tn), lambda i, j, k: (j, i, 0)) if out_shards else mn_spec
    out_dims = (N_DEV, M, N // N_DEV) if out_shards else (M, N)
    n_ex = len(extras)
    n_out = len(out_dtypes)

    def finish(acc, ex, outs):
        res = (acc,) if epilogue is None else epilogue(acc, *[e[...] for e in ex])
        for o, r in zip(outs, res):
            o[...] = r.astype(o.dtype)

    def product(a_ref, b_ref):
        if pair_k:
            return _dot(a_ref[:, :tk // 2], b_ref[0], NT) + _dot(a_ref[:, tk // 2:], b_ref[1], NT)
        return _dot(a_ref[...], b_ref[...], dims)

    def body(*refs):
        a_ref, b_ref = refs[:2]
        ex = refs[2:2 + n_ex]
        outs = refs[2 + n_ex:2 + n_ex + n_out]
        if nk == 1:
            finish(product(a_ref, b_ref), ex, outs)
            return
        acc = refs[-1]
        k = pl.program_id(2)

        @pl.when(k == 0)
        def _():
            acc[...] = product(a_ref, b_ref)

        @pl.when(jnp.logical_and(k > 0, k < nk - 1))
        def _():
            acc[...] += product(a_ref, b_ref)

        @pl.when(k == nk - 1)
        def _():
            finish(acc[...] + product(a_ref, b_ref), ex, outs)

    out = _call(
        body, name, (M // tm, N // tn, nk), [a_spec, b_spec] + [mn_spec] * n_ex, [out_spec] * n_out,
        [jax.ShapeDtypeStruct(out_dims, dt) for dt in out_dtypes], (a, b, *extras),
        scratch=[] if nk == 1 else [pltpu.VMEM((tm, tn), F32)], sem=("parallel", "parallel", "arbitrary"),
        comm=comm)
    return out[0] if n_out == 1 else out


def _rms_fwd(x, gain, out_dtype, name, width=None, col_block=0, residual=None, tb=256, comm=None):
    T = x.shape[0]
    W = x.shape[1] if width is None else width
    tb = _tile(T, tb)
    has_res = residual is not None

    def body(*refs):
        x_ref, g_ref = refs[:2]
        o_ref = refs[-1]
        xf = x_ref[...]
        y = xf * lax.rsqrt(jnp.mean(xf * xf, axis=-1, keepdims=True) + NORM_EPS) * g_ref[...]
        if has_res:
            y = refs[2][...] + y
        o_ref[...] = y.astype(o_ref.dtype)

    row = pl.BlockSpec((tb, W), lambda i: (i, 0))
    ins = [x, gain] + ([residual] if has_res else [])
    return _call(
        body, name, (T // tb,),
        [pl.BlockSpec((tb, W), lambda i: (i, col_block)),
         pl.BlockSpec((1, W), lambda i: (0, 0))] + ([row] if has_res else []),
        [row], [jax.ShapeDtypeStruct((T, W), out_dtype)], ins, sem=("parallel",), comm=comm)[0]


def _rms_bwd(dy, x, gain, out_dtype, name, width=None, col_block=0, residual=None, tb=256, comm=None, into=None):
    T = dy.shape[0]
    W = x.shape[1] if width is None else width
    tb = _tile(T, tb)
    has_res = residual is not None

    def body(*refs):
        dy_ref, x_ref, g_ref = refs[:3]
        dx_ref, dg_ref = refs[-2:]
        i = pl.program_id(0)
        xf = x_ref[...]
        r = lax.rsqrt(jnp.mean(xf * xf, axis=-1, keepdims=True) + NORM_EPS)
        xn = xf * r
        dyf = dy_ref[...].astype(F32)
        dyg = dyf * g_ref[...]
        dx = r * (dyg - xn * jnp.mean(dyg * xn, axis=-1, keepdims=True))
        if has_res:
            dx = refs[3][...] + dx
        dx_ref[...] = dx.astype(dx_ref.dtype)

        @pl.when(i == 0)
        def _():
            dg_ref[...] = jnp.zeros_like(dg_ref)

        dg_ref[...] += jnp.sum(dyf * xn, axis=0, keepdims=True)

    row = pl.BlockSpec((tb, W), lambda i: (i, 0))
    vec = pl.BlockSpec((1, W), lambda i: (0, 0))
    ins = [dy, x, gain] + ([residual] if has_res else [])
    in_specs = [row, pl.BlockSpec((tb, W), lambda i: (i, col_block)), vec] + ([row] if has_res else [])
    if into is None:
        out_spec, out_struct, aliases = row, jax.ShapeDtypeStruct((T, W), out_dtype), None
    else:
        buf, buf_block = into
        out_spec = pl.BlockSpec((tb, W), lambda i: (i, buf_block))
        out_struct, aliases = jax.ShapeDtypeStruct(buf.shape, buf.dtype), {len(ins): 0}
        ins, in_specs = ins + [buf], in_specs + [pl.BlockSpec(memory_space=pl.ANY)]
    return _call(
        body, name, (T // tb,), in_specs, [out_spec, vec], [out_struct, jax.ShapeDtypeStruct((1, W), F32)], ins,
        sem=("arbitrary",), comm=comm, aliases=aliases)


def _rms(xf):
    r = lax.rsqrt(jnp.mean(xf * xf, axis=-1, keepdims=True) + NORM_EPS)
    return r, xf * r


def _rms_grad(dyf, xn, r, gain):
    dyg = dyf * gain
    return r * (dyg - xn * jnp.mean(dyg * xn, axis=-1, keepdims=True)), dyf * xn


def _accumulate_rows(i, ref, rows):
    @pl.when(i == 0)
    def _():
        ref[...] = jnp.zeros_like(ref)

    ref[...] += jnp.sum(rows, axis=0, keepdims=True)


def _norm_pair_fwd(y1, xs, gain_post, gain_pre, name, tb=256):
    T, D = xs.shape
    tb = _tile(T, tb)

    def body(y1_ref, xs_ref, gp_ref, gq_ref, x2_ref, h2_ref):
        x2 = xs_ref[...] + _rms(y1_ref[...])[1] * gp_ref[...]
        x2_ref[...] = x2
        h2_ref[...] = (_rms(x2)[1] * gq_ref[...]).astype(BF16)

    row = pl.BlockSpec((tb, D), lambda i: (i, 0))
    vec = pl.BlockSpec((1, D), lambda i: (0, 0))
    return pl.pallas_call(
        body, name=name, grid=(T // tb,), in_specs=[row, row, vec, vec], out_specs=[row, row],
        out_shape=[jax.ShapeDtypeStruct((T, D), F32), jax.ShapeDtypeStruct((T, D), BF16)],
        compiler_params=_params("parallel"),
    )(y1, xs, gain_post, gain_pre)


def _norm_pair_bwd(dh2, x2, gain_pre, dx3, y1, gain_post, name, tb=256):
    T, D = x2.shape
    tb = _tile(T, tb)

    def body(dh2_ref, x2_ref, gq_ref, dx3_ref, y1_ref, gp_ref, dx2_ref, dy1_ref, dgq_ref, dgp_ref):
        i = pl.program_id(0)
        r2, xn2 = _rms(x2_ref[...])
        d2, rows_q = _rms_grad(dh2_ref[...], xn2, r2, gq_ref[...])
        dx2 = dx3_ref[...] + d2
        dx2_ref[...] = dx2
        r1, yn1 = _rms(y1_ref[...])
        d1, rows_p = _rms_grad(dx2, yn1, r1, gp_ref[...])
        dy1_ref[...] = d1.astype(BF16)
        _accumulate_rows(i, dgq_ref, rows_q)
        _accumulate_rows(i, dgp_ref, rows_p)

    row = pl.BlockSpec((tb, D), lambda i: (i, 0))
    vec = pl.BlockSpec((1, D), lambda i: (0, 0))
    return pl.pallas_call(
        body, name=name, grid=(T // tb,), in_specs=[row, row, vec, row, row, vec], out_specs=[row, row, vec, vec],
        out_shape=[jax.ShapeDtypeStruct((T, D), F32), jax.ShapeDtypeStruct((T, D), BF16),
                   jax.ShapeDtypeStruct((1, D), F32), jax.ShapeDtypeStruct((1, D), F32)],
        compiler_params=_params("arbitrary"),
    )(dh2, x2, gain_pre, dx3, y1, gain_post)


def _loss_head(x2, y2, gain, target, name, tb=256):
    T, D = x2.shape
    tb = _tile(T, tb)

    def body(x2_ref, y2_ref, g_ref, t_ref, dx3_ref, dy2_ref, loss_ref, dg_ref):
        i = pl.program_id(0)
        r, yn = _rms(y2_ref[...])
        e = x2_ref[...] + yn * g_ref[...] - t_ref[...]
        dx3 = e * (1.0 / D)
        dx3_ref[...] = dx3
        dy2, rows = _rms_grad(dx3, yn, r, g_ref[...])
        dy2_ref[...] = dy2.astype(BF16)
        _accumulate_rows(i, dg_ref, rows)
        _accumulate_rows(i, loss_ref, 0.5 * jnp.mean(e * e, axis=-1, keepdims=True))

    row = pl.BlockSpec((tb, D), lambda i: (i, 0))
    vec = pl.BlockSpec((1, D), lambda i: (0, 0))
    return pl.pallas_call(
        body, name=name, grid=(T // tb,),
        in_specs=[row, row, vec, row],
        out_specs=[row, row, pl.BlockSpec((1, 1), lambda i: (0, 0)), vec],
        out_shape=[jax.ShapeDtypeStruct((T, D), F32), jax.ShapeDtypeStruct((T, D), BF16),
                   jax.ShapeDtypeStruct((1, 1), F32), jax.ShapeDtypeStruct((1, D), F32)],
        compiler_params=_params("arbitrary"),
    )(x2, y2, gain, target)


def _rope_tables(positions, rot_dim):
    half = rot_dim // 2
    inv_freq = ROPE_THETA ** (-jnp.arange(0, rot_dim, 2, dtype=F32) / rot_dim)
    ang = positions.astype(F32)[:, None] * inv_freq[None, :]
    cos, sin = jnp.cos(ang), jnp.sin(ang)
    T = positions.shape[0]
    ones = jnp.ones((T, HD - rot_dim), F32)
    cos_t = jnp.concatenate([cos, cos, ones], axis=1)
    sin_t = jnp.concatenate([-sin, sin, jnp.zeros_like(ones)], axis=1)
    return cos_t, sin_t


def _rotate(x, cos_t, sin_t, half):
    lane = lax.broadcasted_iota(jnp.int32, x.shape, 1)
    swapped = jnp.where(lane < half, pltpu.roll(x, HD - half, 1), pltpu.roll(x, half, 1))
    return x * cos_t + swapped * sin_t


def _rope_apply(x, cos_t, sin_t, half, n_blocks, is_rope, out_dtype, name, window=0, tb=256):
    T = x.shape[0]
    tb = _tile(T, tb)
    W = n_blocks * HD

    def body(x_ref, c_ref, s_ref, o_ref):
        for j in range(n_blocks):
            sl = slice(j * HD, (j + 1) * HD)
            xj = x_ref[:, sl]
            if is_rope(j):
                xj = _rotate(xj.astype(F32), c_ref[...], s_ref[...], half)
            o_ref[:, sl] = xj.astype(o_ref.dtype)

    tab = pl.BlockSpec((tb, HD), lambda i: (i, 0))
    return pl.pallas_call(
        body, name=name, grid=(T // tb,),
        in_specs=[pl.BlockSpec((tb, W), lambda i: (i, window)), tab, tab],
        out_specs=pl.BlockSpec((tb, W), lambda i: (i, 0)),
        out_shape=jax.ShapeDtypeStruct((T, W), out_dtype),
        compiler_params=_params("parallel"),
    )(x, cos_t, sin_t)


DSWA_TB = 2048


def _deinterleave(src, dst_ref, d, dtype):
    rows = src.shape[0] // d
    for r in range(d):
        dst_ref[r] = src[pl.ds(r, rows, stride=d), :].astype(dtype)


def _rope_dswa(proj, cos_t, sin_t, name, comm=None):
    T = proj.shape[0]
    tb = _tile(T, 2 * DSWA_TB)
    half = ROT_A // 2

    def body(x_ref, c_ref, s_ref, *rest):
        outs, scr = rest[:-1], rest[-1]
        j = pl.program_id(1)

        @pl.when(j < 2 * HEADS)
        def _():
            scr[...] = _rotate(x_ref[...], c_ref[...], s_ref[...], half)

        @pl.when(j >= 2 * HEADS)
        def _():
            scr[...] = x_ref[...]

        for o_ref, d in zip(outs, DSWA_DILATIONS):
            _deinterleave(scr, o_ref, d, BF16)

    blk = pl.BlockSpec((tb, HD), lambda i, j: (i, j))
    tab = pl.BlockSpec((tb, HD), lambda i, j: (i, 0))
    return _call(
        body, name, (T // tb, 3 * HEADS), [blk, tab, tab],
        [pl.BlockSpec((d, tb // d, HD), lambda i, j: (0, i, j)) for d in DSWA_DILATIONS],
        [jax.ShapeDtypeStruct((d, T // d, 3 * AW), BF16) for d in DSWA_DILATIONS], (proj, cos_t, sin_t),
        scratch=[pltpu.VMEM((tb, HD), F32)], sem=("parallel", "parallel"), comm=comm)


def _shared_key_grad(dk, cos_t, sin_t_neg, half, name, into, tb=512):
    T = dk.shape[0]
    tb = _tile(T, tb)
    buf, buf_block = into

    def body(d_ref, c_ref, s_ref, buf_ref, o_ref):
        tot = d_ref[:, HD:2 * HD]
        for h in range(1, HEADS):
            tot = tot + d_ref[:, h * QK + HD:(h + 1) * QK]
        o_ref[...] = _rotate(tot, c_ref[...], s_ref[...], half).astype(o_ref.dtype)

    tab = pl.BlockSpec((tb, HD), lambda i: (i, 0))
    return pl.pallas_call(
        body, name=name, grid=(T // tb,),
        in_specs=[pl.BlockSpec((tb, HEADS * QK), lambda i: (i, 0)), tab, tab, pl.BlockSpec(memory_space=pl.ANY)],
        out_specs=pl.BlockSpec((tb, HD), lambda i: (i, buf_block)),
        out_shape=jax.ShapeDtypeStruct(buf.shape, buf.dtype), input_output_aliases={3: 0},
        compiler_params=_params("parallel"),
    )(dk, cos_t, sin_t_neg, buf)


def _band_mask(n):
    row = lax.broadcasted_iota(jnp.int32, (QBLK, 2 * QBLK), 0)
    col = lax.broadcasted_iota(jnp.int32, (QBLK, 2 * QBLK), 1)
    in_prev = jnp.logical_and(jnp.logical_and(col < QBLK, col >= row), n > 0)
    in_cur = jnp.logical_and(col >= QBLK, col - QBLK <= row)
    return jnp.logical_or(in_prev, in_cur)


def _dswa_specs(nb, reverse=False):
    pos = (lambda n: nb - 1 - n) if reverse else (lambda n: n)
    cur = lambda c: pl.BlockSpec((None, QBLK, AW), lambda r, n: (r, pos(n), c))
    prev = lambda c: pl.BlockSpec((None, QBLK, AW), lambda r, n: (r, jnp.maximum(pos(n) - 1, 0), c))
    stat = pl.BlockSpec((None, QBLK, HD), lambda r, n: (r, pos(n), 0))
    return cur, prev, stat


def _relayout_spec(d, tb, per_head=True):
    if per_head:
        return pl.BlockSpec((d, tb // d, HD), lambda i, h: (0, i, h))
    return pl.BlockSpec((d, tb // d, HD), lambda i, h: (0, i, 0))


def _head_lane(x, h):
    lane = lax.broadcasted_iota(jnp.int32, x.shape, 1)
    return jnp.sum(jnp.where(lane == h, x, 0.0), axis=-1, keepdims=True)


def _dswa_fwd(qkv, name, comm=None):
    d, sd = qkv.shape[:2]
    nb = sd // QBLK

    def body(q_ref, kc_ref, kp_ref, vc_ref, vp_ref, o_ref, l_ref):
        mask = _band_mask(pl.program_id(1))
        l_ref[...] = jnp.zeros_like(l_ref)
        for h in range(HEADS):
            sl = slice(h * HD, (h + 1) * HD)
            keys = jnp.concatenate([kp_ref[:, sl], kc_ref[:, sl]], axis=0)
            vals = jnp.concatenate([vp_ref[:, sl], vc_ref[:, sl]], axis=0)
            s = jnp.where(mask, _dot(q_ref[:, sl], keys, NT) * SCALE_A, NEG_INF)
            m = jnp.max(s, axis=-1, keepdims=True)
            p = jnp.exp(s - m)
            den = jnp.sum(p, axis=-1, keepdims=True)
            o_ref[:, sl] = _dot((p / den).astype(BF16), vals, NN)
            l_ref[:, h:h + 1] = m + jnp.log(den)

    cur, prev, stat = _dswa_specs(nb)
    return _call(
        body, name, (d, nb), [cur(0), cur(1), prev(1), cur(2), prev(2)], [cur(0), stat],
        [jax.ShapeDtypeStruct((d, sd, AW), F32), jax.ShapeDtypeStruct((d, sd, HD), F32)],
        (qkv, qkv, qkv, qkv, qkv), sem=("parallel", "parallel"), comm=comm)


def _dswa_merge(outs, lses, name, comm=None):
    nc = len(DSWA_DILATIONS)
    T = outs[0].shape[0] * outs[0].shape[1]
    tb = _tile(T, DSWA_TB)

    def body(*refs):
        o_refs, l_refs = refs[:nc], refs[nc:2 * nc]
        out_ref, outb_ref = refs[2 * nc:2 * nc + 2]
        lt_refs = refs[2 * nc + 2:3 * nc + 2]
        o_nat, l_nat, lt_nat = refs[3 * nc + 2:4 * nc + 2], refs[4 * nc + 2:5 * nc + 2], refs[-1]
        h = pl.program_id(1)
        for c, d in enumerate(DSWA_DILATIONS):
            for r in range(d):
                o_nat[c][pl.ds(r, tb // d, stride=d), :] = o_refs[c][r]
                l_nat[c][pl.ds(r, tb // d, stride=d), :] = l_refs[c][r]
        ls = [l[...] for l in l_nat]
        m = functools.reduce(jnp.maximum, ls)
        es = [jnp.exp(l - m) for l in ls]
        tot = functools.reduce(lambda a, b: a + b, es)
        acc = _head_lane(es[0] / tot, h) * o_nat[0][...]
        for c in range(1, nc):
            acc = acc + _head_lane(es[c] / tot, h) * o_nat[c][...]
        out_ref[...] = acc
        outb_ref[...] = acc.astype(BF16)
        lt_nat[...] = m + jnp.log(tot)
        for c, d in enumerate(DSWA_DILATIONS):
            _deinterleave(lt_nat, lt_refs[c], d, F32)

    nat = pl.BlockSpec((tb, HD), lambda i, h: (i, h))
    by_d = [_relayout_spec(d, tb) for d in DSWA_DILATIONS]
    stat_by_d = [_relayout_spec(d, tb, per_head=False) for d in DSWA_DILATIONS]
    res = _call(
        body, name, (T // tb, HEADS), by_d + stat_by_d, [nat, nat] + stat_by_d,
        [jax.ShapeDtypeStruct((T, AW), F32), jax.ShapeDtypeStruct((T, AW), BF16)]
        + [jax.ShapeDtypeStruct((d, T // d, HD), F32) for d in DSWA_DILATIONS], (*outs, *lses),
        scratch=[pltpu.VMEM((tb, HD), F32)] * (2 * nc + 1), sem=("parallel", "arbitrary"), comm=comm)
    return res[0], res[1], res[2:]


def _dswa_delta(dout, out, name):
    nc = len(DSWA_DILATIONS)
    T = out.shape[0]
    tb = _tile(T, DSWA_TB)

    def body(do_ref, o_ref, *rest):
        dl_refs, dob_refs, dl_nat = rest[:nc], rest[nc:2 * nc], rest[-1]
        h = pl.program_id(1)
        lane = lax.broadcasted_iota(jnp.int32, (tb, HD), 1)
        mine = jnp.where(lane == h, jnp.sum(do_ref[...] * o_ref[...], axis=-1, keepdims=True), 0.0)

        @pl.when(h == 0)
        def _():
            dl_nat[...] = mine

        @pl.when(h > 0)
        def _():
            dl_nat[...] += mine

        for c, d in enumerate(DSWA_DILATIONS):
            _deinterleave(dl_nat, dl_refs[c], d, F32)
            _deinterleave(do_ref, dob_refs[c], d, BF16)

    nat = pl.BlockSpec((tb, HD), lambda i, h: (i, h))
    by_d = [_relayout_spec(d, tb) for d in DSWA_DILATIONS]
    stat_by_d = [_relayout_spec(d, tb, per_head=False) for d in DSWA_DILATIONS]
    res = pl.pallas_call(
        body, name=name, grid=(T // tb, HEADS),
        in_specs=[nat, nat], out_specs=stat_by_d + by_d,
        out_shape=[jax.ShapeDtypeStruct((d, T // d, HD), F32) for d in DSWA_DILATIONS]
        + [jax.ShapeDtypeStruct((d, T // d, AW), BF16) for d in DSWA_DILATIONS],
        scratch_shapes=[pltpu.VMEM((tb, HD), F32)],
        compiler_params=_params("parallel", "arbitrary"),
    )(dout, out)
    return res[:nc], res[nc:]


def _delta_prep(dout, col_block, out, name, tb=256):
    T = out.shape[0]
    tb = _tile(T, tb)

    def body(do_ref, o_ref, delta_ref, dob_ref):
        delta_ref[...] = jnp.zeros_like(delta_ref)
        for h in range(HEADS):
            sl = slice(h * HD, (h + 1) * HD)
            doh = do_ref[:, sl]
            delta_ref[:, h:h + 1] = jnp.sum(doh * o_ref[:, sl], axis=-1, keepdims=True)
            dob_ref[:, sl] = doh.astype(BF16)

    row = pl.BlockSpec((tb, AW), lambda i: (i, 0))
    return pl.pallas_call(
        body, name=name, grid=(T // tb,),
        in_specs=[pl.BlockSpec((tb, AW), lambda i: (i, col_block)), row],
        out_specs=[pl.BlockSpec((tb, HD), lambda i: (i, 0)), row],
        out_shape=[jax.ShapeDtypeStruct((T, HD), F32), jax.ShapeDtypeStruct((T, AW), BF16)],
        compiler_params=_params("parallel"),
    )(dout, out)


def _dswa_bwd(qkv, dout_b, lse_tot, delta, name, comm=None):
    d, sd = qkv.shape[:2]
    nb = sd // QBLK

    def body(q_ref, kc_ref, kp_ref, vc_ref, vp_ref, do_ref, l_ref, dl_ref, g_ref, carry_k, carry_v):
        mask = _band_mask(nb - 1 - pl.program_id(1))

        @pl.when(pl.program_id(1) == 0)
        def _():
            carry_k[...] = jnp.zeros_like(carry_k)
            carry_v[...] = jnp.zeros_like(carry_v)

        for h in range(HEADS):
            sl = slice(h * HD, (h + 1) * HD)
            qh, doh = q_ref[:, sl], do_ref[:, sl]
            keys = jnp.concatenate([kp_ref[:, sl], kc_ref[:, sl]], axis=0)
            vals = jnp.concatenate([vp_ref[:, sl], vc_ref[:, sl]], axis=0)
            s = jnp.where(mask, _dot(qh, keys, NT) * SCALE_A, NEG_INF)
            p = jnp.exp(s - l_ref[:, h:h + 1])
            ds = (p * (_dot(doh, vals, NT) - dl_ref[:, h:h + 1]) * SCALE_A).astype(BF16)
            g_ref[:, sl] = _dot(ds, keys, NN).astype(BF16)
            dk = _dot(ds, qh, TN)
            dv = _dot(p.astype(BF16), doh, TN)
            g_ref[:, AW + h * HD:AW + (h + 1) * HD] = (dk[QBLK:] + carry_k[:, sl]).astype(BF16)
            g_ref[:, 2 * AW + h * HD:2 * AW + (h + 1) * HD] = (dv[QBLK:] + carry_v[:, sl]).astype(BF16)
            carry_k[:, sl] = dk[:QBLK]
            carry_v[:, sl] = dv[:QBLK]

    cur, prev, stat = _dswa_specs(nb, reverse=True)
    out_spec = pl.BlockSpec((None, QBLK, 3 * AW), lambda r, n: (r, nb - 1 - n, 0))
    return _call(
        body, name, (d, nb), [cur(0), cur(1), prev(1), cur(2), prev(2), cur(0), stat, stat], [out_spec],
        [jax.ShapeDtypeStruct((d, sd, 3 * AW), BF16)], (qkv, qkv, qkv, qkv, qkv, dout_b, lse_tot, delta),
        scratch=[pltpu.VMEM((QBLK, AW), F32)] * 2, sem=("parallel", "arbitrary"), comm=comm)[0]


def _dswa_combine(grads, cos_t, sin_t_neg, name, width):
    T = grads[0].shape[0] * grads[0].shape[1]
    tb = _tile(T, DSWA_TB)
    half = ROT_A // 2

    def body(*refs):
        g_refs = refs[:len(grads)]
        c_ref, s_ref, out_ref, acc = refs[len(grads):]
        j = pl.program_id(1)
        for c, d in enumerate(DSWA_DILATIONS):
            for r in range(d):
                if c == 0:
                    acc[...] = g_refs[c][r].astype(F32)
                else:
                    acc[pl.ds(r, tb // d, stride=d), :] += g_refs[c][r].astype(F32)
        val = acc[...]
        out_ref[...] = jnp.where(j < 2 * HEADS, _rotate(val, c_ref[...], s_ref[...], half), val).astype(BF16)

    tab = pl.BlockSpec((tb, HD), lambda i, j: (i, 0))
    return pl.pallas_call(
        body, name=name, grid=(T // tb, 3 * HEADS),
        in_specs=[pl.BlockSpec((d, tb // d, HD), lambda i, j: (0, i, j)) for d in DSWA_DILATIONS] + [tab, tab],
        out_specs=pl.BlockSpec((tb, HD), lambda i, j: (i, j)),
        out_shape=jax.ShapeDtypeStruct((T, width), BF16),
        scratch_shapes=[pltpu.VMEM((tb, HD), F32)],
        compiler_params=_params("parallel", "parallel"),
    )(*grads, cos_t, sin_t_neg)


MLA_TQ = 512
QK = 2 * HD
LOG2E = 1.4426950408889634


def _triangle(nq, key_major):
    pairs = [(q, k) for q in range(nq) for k in range(q + 1)]
    if key_major:
        pairs.sort(key=lambda p: (p[1], p[0]))
    return (jnp.array([p[0] for p in pairs], jnp.int32), jnp.array([p[1] for p in pairs], jnp.int32))


def _mla_specs(tq):
    q_spec = pl.BlockSpec((tq, HEADS * QK), lambda t, qi, ki: (qi[t], 0))
    k_spec = pl.BlockSpec((tq, HEADS * QK), lambda t, qi, ki: (ki[t], 0))
    v_spec = pl.BlockSpec((tq, AW), lambda t, qi, ki: (ki[t], 1))
    qrow = pl.BlockSpec((tq, AW), lambda t, qi, ki: (qi[t], 0))
    krow = pl.BlockSpec((tq, AW), lambda t, qi, ki: (ki[t], 0))
    return q_spec, k_spec, v_spec, qrow, krow


def _mla_pack(kv, kr, name, tb=512):
    T = kv.shape[0]
    tb = _tile(T, tb)

    def body(kv_ref, kr_ref, k_ref, v1_ref):
        lane = lax.broadcasted_iota(jnp.int32, (tb, HD), 1)
        one_hot = jnp.where(lane == 0, 1.0, 0.0).astype(BF16)
        for h in range(HEADS):
            k_ref[:, h * QK:h * QK + HD] = kv_ref[:, h * HD:(h + 1) * HD]
            k_ref[:, h * QK + HD:(h + 1) * QK] = kr_ref[...]
            v1_ref[:, h * QK:h * QK + HD] = kv_ref[:, AW + h * HD:AW + (h + 1) * HD]
            v1_ref[:, h * QK + HD:(h + 1) * QK] = one_hot

    wide = pl.BlockSpec((tb, HEADS * QK), lambda i: (i, 0))
    return pl.pallas_call(
        body, name=name, grid=(T // tb,),
        in_specs=[pl.BlockSpec((tb, 2 * AW), lambda i: (i, 0)), pl.BlockSpec((tb, HD), lambda i: (i, 0))],
        out_specs=[wide, wide], out_shape=[jax.ShapeDtypeStruct((T, HEADS * QK), BF16)] * 2,
        compiler_params=_params("parallel"),
    )(kv, kr)


def _mla_stat_spec(tq):
    return pl.BlockSpec((tq, HD), lambda t, qi, ki: (qi[t], 0))


def _mla_scores(q_ref, k_ref, h, qi, ki, tq):
    s = _dot(q_ref[:, h * QK:(h + 1) * QK], k_ref[:, h * QK:(h + 1) * QK], NT) * SCALE_B
    row = lax.broadcasted_iota(jnp.int32, s.shape, 0) + qi * tq
    col = lax.broadcasted_iota(jnp.int32, s.shape, 1) + ki * tq
    return jnp.where(col <= row, s, NEG_INF)


def _mla_fwd(q, k, v1, name, comm=None):
    T = q.shape[0]
    tq = _tile(T, MLA_TQ)
    tables = _triangle(T // tq, False)

    def body(qi_ref, ki_ref, q_ref, k_ref, v_ref, o_ref, ob_ref, l_ref, m_s, acc):
        t = pl.program_id(0)
        qi, ki = qi_ref[t], ki_ref[t]

        @pl.when(ki == 0)
        def _():
            m_s[...] = jnp.full_like(m_s, NEG_INF)
            acc[...] = jnp.zeros_like(acc)

        row = lax.broadcasted_iota(jnp.int32, (tq, tq), 0) + qi * tq
        col = lax.broadcasted_iota(jnp.int32, (tq, tq), 1) + ki * tq
        bias = jnp.where(col <= row, 0.0, NEG_INF)
        updates = []
        for h in range(HEADS):
            s = _dot(q_ref[:, h * QK:(h + 1) * QK], k_ref[:, h * QK:(h + 1) * QK], NT) + bias
            m_new = jnp.maximum(m_s[h], jnp.max(s, axis=-1, keepdims=True))
            p = jnp.exp2((s - m_new) * (SCALE_B * LOG2E)).astype(BF16)
            alpha = jnp.exp2((m_s[h] - m_new) * (SCALE_B * LOG2E))
            updates.append((m_new, alpha, _dot(p, v_ref[:, h * QK:(h + 1) * QK], NN)))
        for h, (m_new, alpha, pv) in enumerate(updates):
            acc[:, h * QK:(h + 1) * QK] = alpha * acc[:, h * QK:(h + 1) * QK] + pv
            m_s[h] = m_new

        @pl.when(ki == qi)
        def _():
            l_ref[...] = jnp.zeros_like(l_ref)
            for h in range(HEADS):
                sl = slice(h * HD, (h + 1) * HD)
                den = acc[:, h * QK + HD:h * QK + HD + 1]
                out = acc[:, h * QK:h * QK + HD] / den
                o_ref[:, sl] = out
                ob_ref[:, sl] = out.astype(BF16)
                l_ref[:, h:h + 1] = m_s[h] * SCALE_B + jnp.log(den)

    q_spec, k_spec, _, qrow, _ = _mla_specs(tq)
    return _call(
        body, name, (tables[0].shape[0],), [q_spec, k_spec, k_spec], [qrow, qrow, _mla_stat_spec(tq)],
        [jax.ShapeDtypeStruct((T, AW), F32), jax.ShapeDtypeStruct((T, AW), BF16), jax.ShapeDtypeStruct((T, HD), F32)],
        (q, k, v1),
        scratch=[pltpu.VMEM((HEADS, tq, 1), F32), pltpu.VMEM((tq, HEADS * QK), F32)],
        sem=("arbitrary",), comm=comm, prefetch=tables)


def _mla_ds(q_ref, k_ref, v_ref, do_ref, l_ref, dl_ref, h, qi, ki, tq):
    sl = slice(h * HD, (h + 1) * HD)
    p = jnp.exp(_mla_scores(q_ref, k_ref, h, qi, ki, tq) - l_ref[:, h:h + 1])
    ds = (p * (_dot(do_ref[:, sl], v_ref[:, sl], NT) - dl_ref[:, h:h + 1]) * SCALE_B).astype(BF16)
    return p, ds


def _mla_bwd_q(q, k, kv, dout_b, lse, delta, name, comm=None):
    T = q.shape[0]
    tq = _tile(T, MLA_TQ)
    tables = _triangle(T // tq, False)

    def body(qi_ref, ki_ref, q_ref, k_ref, v_ref, do_ref, l_ref, dl_ref, dq_ref):
        t = pl.program_id(0)
        qi, ki = qi_ref[t], ki_ref[t]

        @pl.when(ki == 0)
        def _():
            dq_ref[...] = jnp.zeros_like(dq_ref)

        for h in range(HEADS):
            _, ds = _mla_ds(q_ref, k_ref, v_ref, do_ref, l_ref, dl_ref, h, qi, ki, tq)
            dq_ref[:, h * QK:(h + 1) * QK] += _dot(ds, k_ref[:, h * QK:(h + 1) * QK], NN)

    q_spec, k_spec, v_spec, qrow, _ = _mla_specs(tq)
    return _call(
        body, name, (tables[0].shape[0],), [q_spec, k_spec, v_spec, qrow, _mla_stat_spec(tq), _mla_stat_spec(tq)], [q_spec],
        [jax.ShapeDtypeStruct((T, HEADS * QK), F32)], (q, k, kv, dout_b, lse, delta),
        sem=("arbitrary",), comm=comm, prefetch=tables)[0]


def _mla_bwd_kv(q, k, kv, dout_b, lse, delta, name, comm=None):
    T = q.shape[0]
    tq = _tile(T, MLA_TQ)
    nq = T // tq
    tables = _triangle(nq, True)

    def body(qi_ref, ki_ref, q_ref, k_ref, v_ref, do_ref, l_ref, dl_ref, dk_ref, dkv_ref, dv_acc):
        t = pl.program_id(0)
        qi, ki = qi_ref[t], ki_ref[t]

        @pl.when(qi == ki)
        def _():
            dk_ref[...] = jnp.zeros_like(dk_ref)
            dv_acc[...] = jnp.zeros_like(dv_acc)

        for h in range(HEADS):
            sl = slice(h * HD, (h + 1) * HD)
            p, ds = _mla_ds(q_ref, k_ref, v_ref, do_ref, l_ref, dl_ref, h, qi, ki, tq)
            dv_acc[:, sl] += _dot(p.astype(BF16), do_ref[:, sl], TN)
            dk_ref[:, h * QK:(h + 1) * QK] += _dot(ds, q_ref[:, h * QK:(h + 1) * QK], TN)

        @pl.when(qi == nq - 1)
        def _():
            for h in range(HEADS):
                dkv_ref[:, h * HD:(h + 1) * HD] = dk_ref[:, h * QK:h * QK + HD].astype(BF16)
                dkv_ref[:, AW + h * HD:AW + (h + 1) * HD] = dv_acc[:, h * HD:(h + 1) * HD].astype(BF16)

    q_spec, k_spec, v_spec, qrow, _ = _mla_specs(tq)
    return _call(
        body, name, (tables[0].shape[0],),
        [q_spec, k_spec, v_spec, qrow, _mla_stat_spec(tq), _mla_stat_spec(tq)],
        [k_spec, pl.BlockSpec((tq, 2 * AW), lambda t, qi, ki: (ki[t], 0))],
        [jax.ShapeDtypeStruct((T, HEADS * QK), F32), jax.ShapeDtypeStruct((T, 2 * AW), BF16)],
        (q, k, kv, dout_b, lse, delta), scratch=[pltpu.VMEM((tq, AW), F32)],
        sem=("arbitrary",), comm=comm, prefetch=tables)


def _pair_sum(by_device, from_sibling, name, tb=256):
    n_chip, R, C = from_sibling.shape
    tb = _tile(R, tb)
    core = jnp.reshape(lax.axis_index("c"), (1,)).astype(jnp.int32)

    def body(core_ref, mine_ref, theirs_ref, o_ref):
        o_ref[...] = (mine_ref[...].astype(F32) + theirs_ref[...].astype(F32)).astype(o_ref.dtype)

    blk = pl.BlockSpec((None, tb, C), lambda p, i, core_ref: (p, i, 0))
    return _call(
        body, name, (n_chip, R // tb),
        [pl.BlockSpec((None, tb, C), lambda p, i, core_ref: (2 * p + core_ref[0], i, 0)), blk], [blk],
        [jax.ShapeDtypeStruct((n_chip, R, C), BF16)], (by_device, from_sibling),
        sem=("parallel", "parallel"), prefetch=(core,))[0]


def _adamw(parts, w, m, v, name, tb=128, comm=None):
    R, C = w.shape
    pieces = list(parts) if isinstance(parts, (list, tuple)) else [parts]
    n_parts = pieces[0].shape[0]
    tb = _tile(R, tb)
    starts = [0]
    for p in pieces:
        assert p.shape[1] % tb == 0
        starts.append(starts[-1] + p.shape[1] // tb)
    c1 = 1.0 - ADAM_B1
    c2 = 1.0 - ADAM_B2
    bc1 = 1.0 - ADAM_B1 ** ADAM_STEP
    bc2 = 1.0 - ADAM_B2 ** ADAM_STEP

    def body(*refs):
        p_refs = refs[:len(pieces)]
        w_ref, m_ref, v_ref, g_ref, d_ref, nm_ref, nv_ref = refs[len(pieces):]
        i = pl.program_id(0)
        g = None
        for k, p_ref in enumerate(p_refs):
            gk = p_ref[0].astype(F32)
            for j in range(1, n_parts):
                gk = gk + p_ref[j].astype(F32)
            g = gk if g is None else jnp.where(i >= starts[k], gk, g)
        nm = ADAM_B1 * m_ref[...] + c1 * g
        nv = ADAM_B2 * v_ref[...] + c2 * (g * g)
        g_ref[...] = g
        nm_ref[...] = nm
        nv_ref[...] = nv
        d_ref[...] = -ADAM_LR * ((nm / bc1) / (jnp.sqrt(nv / bc2) + ADAM_EPS) + ADAM_WD * w_ref[...])

    row = pl.BlockSpec((tb, C), lambda i: (i, 0))
    piece_specs = [pl.BlockSpec((n_parts, tb, C),
                                functools.partial(lambda i, lo, hi: (0, jnp.clip(i - lo, 0, hi - lo - 1), 0),
                                                  lo=starts[k], hi=starts[k + 1]))
                   for k in range(len(pieces))]
    return _call(
        body, name, (R // tb,), piece_specs + [row, row, row], [row] * 4,
        [jax.ShapeDtypeStruct((R, C), F32)] * 4, (*pieces, w, m, v), sem=("parallel",), comm=comm)


def _cols_from_shards(g):
    return jnp.transpose(g, (1, 0, 2)).reshape(g.shape[1], N_DEV * g.shape[2])


def _cols_to_shards(w):
    return jnp.transpose(w.reshape(w.shape[0], N_DEV, w.shape[1] // N_DEV), (1, 0, 2))


def _split_heads(w, first):
    w3 = w.reshape(w.shape[0], HEADS, -1)
    return w3[:, :, :first].reshape(w.shape[0], -1), w3[:, :, first:].reshape(w.shape[0], -1)


def _join_heads(a, b):
    R = a.shape[0]
    return jnp.concatenate([a.reshape(R, HEADS, -1), b.reshape(R, HEADS, -1)], axis=2).reshape(R, -1)


def _join_shards_padded(g, width, name, tb=256):
    n, R, cs = g.shape
    tb = _tile(R, tb)

    def body(g_ref, o_ref):
        for j in range(n):
            o_ref[:, j * cs:(j + 1) * cs] = g_ref[j]
        o_ref[:, n * cs:] = jnp.zeros((tb, width - n * cs), o_ref.dtype)

    return pl.pallas_call(
        body, name=name, grid=(R // tb,),
        in_specs=[pl.BlockSpec((n, tb, cs), lambda i: (0, i, 0))],
        out_specs=pl.BlockSpec((tb, width), lambda i: (i, 0)),
        out_shape=jax.ShapeDtypeStruct((R, width), g.dtype),
        compiler_params=_params("parallel"),
    )(g)


def _split_shards(w, cs, name, tb=256):
    R, width = w.shape
    tb = _tile(R, tb)

    def body(w_ref, o_ref):
        for j in range(N_DEV):
            o_ref[j] = w_ref[:, j * cs:(j + 1) * cs]

    return pl.pallas_call(
        body, name=name, grid=(R // tb,),
        in_specs=[pl.BlockSpec((tb, width), lambda i: (i, 0))],
        out_specs=pl.BlockSpec((N_DEV, tb, cs), lambda i: (0, i, 0)),
        out_shape=jax.ShapeDtypeStruct((N_DEV, R, cs), w.dtype),
        compiler_params=_params("parallel"),
    )(w)


def _pad_heads(w, width):
    w3 = w.reshape(w.shape[0], HEADS, -1)
    return jnp.pad(w3, ((0, 0), (0, 0), (0, width - w3.shape[2]))).reshape(w.shape[0], HEADS * width)


def _unpad_heads(w, k):
    return w.reshape(w.shape[0], HEADS, -1)[:, :, :k].reshape(w.shape[0], HEADS * k)


def kernel(x, positions, norm_attn_pre, norm_attn_post, w_in, q_latent_norm, kv_latent_norm, w_uq, w_ukv, w_out, norm_mlp_pre, norm_mlp_post, w_up, w_down, loss_target, m_norm_attn_pre, m_norm_attn_post, m_w_in, m_q_latent_norm, m_kv_latent_norm, m_w_uq, m_w_ukv, m_w_out, m_norm_mlp_pre, m_norm_mlp_post, m_w_up, m_w_down, v_norm_attn_pre, v_norm_attn_post, v_w_in, v_q_latent_norm, v_kv_latent_norm, v_w_uq, v_w_ukv, v_w_out, v_norm_mlp_pre, v_norm_mlp_post, v_w_up, v_w_down):
    xs = x[0]
    tgt = loss_target[0]
    pos = positions[0]
    T, D = xs.shape
    big = dict(w_in=(w_in, m_w_in, v_w_in), w_uq=(w_uq, m_w_uq, v_w_uq), w_ukv=(w_ukv, m_w_ukv, v_w_ukv),
               w_out=(w_out, m_w_out, v_w_out), w_up=(w_up, m_w_up, v_w_up), w_down=(w_down, m_w_down, v_w_down))
    big = {n: tuple(t[0] for t in ts) for n, ts in big.items()}
    big_names = ["w_in", "w_uq", "w_ukv", "w_out", "w_up", "w_down"]
    col_sharded = {"w_in", "w_uq", "w_ukv", "w_up"}

    wb = {n: big[n][0].astype(BF16) for n in big_names}

    def gathered(ex, i, n):
        g = ex.results[i]
        return _cols_from_shards(g) if n in col_sharded else g.reshape(-1, g.shape[2])

    def by_device(g, n):
        return _cols_to_shards(g) if n in col_sharded else g.reshape(N_DEV, g.shape[0] // N_DEV, g.shape[1])

    def scatter_of(g, n):
        return _Exchange([by_device(g, n)], "scatter")

    cos_a, sin_a = _rope_tables(pos, ROT_A)
    cos_b, sin_b = _rope_tables(pos, ROPE_MLA)

    ex_in = _Exchange([wb["w_in"]], "gather")
    h1 = _rms_fwd(xs, norm_attn_pre, BF16, "norm_attn_pre_fwd", comm=ex_in)
    Wi = _join_shards_padded(ex_in.results[0], IN_PAD, "w_in_join")
    ex_mid = _Exchange([wb["w_uq"], wb["w_ukv"], wb["w_out"]], "gather")
    proj = _mm(h1, Wi, "nn", [F32], "proj_in", tn=1408, comm=ex_mid)
    Wuq = _pad_heads(gathered(ex_mid, 0, "w_uq"), QK)
    Wukv = jnp.concatenate(_split_heads(gathered(ex_mid, 1, "w_ukv"), HD), axis=1)
    Wo = gathered(ex_mid, 2, "w_out")
    n_piece = wb["w_down"].shape[0] // 8
    ex_down = [_Exchange([wb["w_down"][i * n_piece:(i + 1) * n_piece]], "gather") for i in range(8)]
    qkv_by_d = _rope_dswa(proj, cos_a, sin_a, "rope_dswa", comm=ex_down[0])
    outs, lses = [], []
    for d, qkv, ex in zip(DSWA_DILATIONS, qkv_by_d, ex_down[1:4]):
        o, l = _dswa_fwd(qkv, f"dswa_fwd_d{d}", comm=ex)
        outs.append(o)
        lses.append(l)
    a_out, a_out_b, a_lse_by_d = _dswa_merge(outs, lses, "dswa_merge", comm=ex_down[4])

    cqn = _rms_fwd(proj, q_latent_norm, BF16, "q_latent_norm_fwd", width=Q_LORA, col_block=3 * AW // Q_LORA)
    ckvn = _rms_fwd(proj, kv_latent_norm, BF16, "kv_latent_norm_fwd", width=KV_LORA, col_block=3 * AW // KV_LORA + 1)
    qb = _mm(cqn, Wuq, "nn", [F32], "q_up")
    kvb = _mm(ckvn, Wukv, "nn", [BF16], "kv_up")
    odd = lambda j: j % 2 == 1
    q_mla = _rope_apply(qb, cos_b, sin_b, ROPE_MLA // 2, 2 * HEADS, odd, BF16, "rope_mla_q")
    kr = _rope_apply(proj, cos_b, sin_b, ROPE_MLA // 2, 1, lambda j: True, BF16, "rope_mla_k",
                     window=(IN_PAD - HD) // HD)
    k_mla, v1_mla = _mla_pack(kvb, kr, "mla_pack")
    ex_up = _Exchange([wb["w_up"]], "gather")
    b_out, b_out_b, b_lse = _mla_fwd(q_mla, k_mla, v1_mla, "mla_fwd", comm=ex_up)
    Wup_shards = ex_up.results[0]

    mixed = jnp.concatenate([a_out_b, b_out_b], axis=1)
    y1 = _mm(mixed, Wo, "nn", [F32], "attn_out", comm=ex_down[5])

    x2, h2 = _norm_pair_fwd(y1, xs, norm_attn_post, norm_mlp_pre, "norm_attn_post_mlp_pre_fwd")

    def relu2(z):
        r = jnp.maximum(z, 0.0)
        return r * r, r

    u, zr = _mm(h2, Wup_shards, "nn", [BF16, BF16], "mlp_up", epilogue=relu2, comm=ex_down[6:8],
                b_shards=True)
    Wdn = jnp.concatenate([ex.results[0] for ex in ex_down], axis=1).reshape(-1, D)
    y2 = _mm(u, Wdn, "nn", [F32], "mlp_down")
    dx3, dy2, loss_part, dg_mlp_post = _loss_head(x2, y2, norm_mlp_post, tgt, "loss_head")

    dz =_mm(dy2, Wdn, "nt", [BF16], "mlp_down_dx", epilogue=lambda du, r: (du * (2.0 * r.astype(F32)),), extras=(zr,))
    g_down = _mm(u, dy2, "tn", [BF16], "mlp_down_dw")
    down_dev = by_device(g_down, "w_down")
    pair_down = _Exchange([down_dev], "pair")
    up_dev = _mm(h2, dz, "tn", [BF16], "mlp_up_dw", comm=pair_down, out_shards=True)
    down_chip = _pair_sum(down_dev, pair_down.results[0], "pair_sum_w_down")
    pair_up = _Exchange([up_dev], "pair")
    cut = 5 * down_chip.shape[1] // 8
    sc_down = [_Exchange([down_chip[:, :cut]], "chips"), _Exchange([down_chip[:, cut:]], "chips")]
    dh2 = _mm(dz, Wup_shards, "nt", [F32], "mlp_up_dx", comm=[pair_up, sc_down[0]], b_shards=True)
    up_chip = _pair_sum(up_dev, pair_up.results[0], "pair_sum_w_up")

    dx2, dy1, dg_mlp_pre, dg_attn_post = _norm_pair_bwd(dh2, x2, norm_mlp_pre, dx3, y1, norm_attn_post,
                                                        "norm_mlp_pre_attn_post_bwd")
    dmixed = _mm(dy1, Wo, "nt", [F32], "attn_out_dx")
    g_out = _mm(mixed, dy1, "tn", [BF16], "attn_out_dw")

    b_delta, b_dout = _delta_prep(dmixed, 1, b_out, "mla_delta")
    sc_up = _Exchange([up_chip], "chips")
    dq_mla = _mla_bwd_q(q_mla, k_mla, kvb, b_dout, b_lse, b_delta, "mla_bwd_q", comm=sc_up)
    sc_out = scatter_of(g_out, "w_out")
    dk_mla, dkvb = _mla_bwd_kv(q_mla, k_mla, kvb, b_dout, b_lse, b_delta, "mla_bwd_kv", comm=[sc_out, sc_down[1]])
    dqb = _rope_apply(dq_mla, cos_b, -sin_b, ROPE_MLA // 2, 2 * HEADS, odd, BF16, "rope_mla_q_bwd")
    g_uq_pad = _mm(cqn, dqb, "tn", [BF16], "q_up_dw")
    g_ukv_perm = _mm(ckvn, dkvb, "tn", [BF16], "kv_up_dw")
    dcqn = _mm(dqb, Wuq, "nt", [F32], "q_up_dx")
    dckvn = _mm(dkvb, Wukv, "nt", [F32], "kv_up_dx")

    g_uq = _unpad_heads(g_uq_pad, HD + ROPE_MLA)
    g_ukv = _join_heads(g_ukv_perm[:, :AW], g_ukv_perm[:, AW:])
    sc_uq = _Exchange([_cols_to_shards(g_uq), _cols_to_shards(g_ukv)], "scatter")
    a_delta_by_d, a_dout_by_d = _dswa_delta(dmixed, a_out, "dswa_delta")
    a_grads = [_dswa_bwd(qkv_by_d[c], a_dout_by_d[c], a_lse_by_d[c], a_delta_by_d[c], f"dswa_bwd_d{d}")
               for c, d in enumerate(DSWA_DILATIONS)]
    dproj = _dswa_combine(a_grads, cos_a, -sin_a, "dswa_combine", IN_PAD)
    dproj, dg_q = _rms_bwd(dcqn, proj, q_latent_norm, BF16, "q_latent_norm_bwd", width=Q_LORA,
                           col_block=3 * AW // Q_LORA, into=(dproj, 3 * AW // Q_LORA))
    dproj, dg_kv = _rms_bwd(dckvn, proj, kv_latent_norm, BF16, "kv_latent_norm_bwd", width=KV_LORA,
                            col_block=3 * AW // KV_LORA + 1, into=(dproj, 3 * AW // KV_LORA + 1))
    dproj = _shared_key_grad(dk_mla, cos_b, -sin_b, ROPE_MLA // 2, "rope_mla_k_bwd", (dproj, (IN_PAD - HD) // HD))
    g_in_pad = _mm(h1, dproj, "tn", [BF16], "proj_in_dw", tn=1408, comm=sc_uq)
    in_dev = _split_shards(g_in_pad, IN_COLS // N_DEV, "w_in_grad_split")
    pair_in = _Exchange([in_dev], "pair").standalone("pair_w_in")
    sc_in = _Exchange([_pair_sum(in_dev, pair_in[0], "pair_sum_w_in")], "chips")
    dh1 = _mm(dproj, Wi, "nt", [F32], "proj_in_dx", comm=sc_in)
    grad_x, dg_attn_pre = _rms_bwd(dh1, xs, norm_attn_pre, F32, "norm_attn_pre_bwd", residual=dx2)

    parts = dict(w_in=sc_in.results[0], w_uq=sc_uq.results[0], w_ukv=sc_uq.results[1], w_out=sc_out.results[0],
                 w_up=sc_up.results[0], w_down=[sc.results[0] for sc in sc_down])
    big_out = {n: _adamw(parts[n], *big[n], f"adamw_{n}") for n in big_names}

    gain_names = ["norm_attn_pre", "norm_attn_post", "q_latent_norm", "kv_latent_norm", "norm_mlp_pre", "norm_mlp_post"]
    gain_args = dict(norm_attn_pre=(norm_attn_pre, m_norm_attn_pre, v_norm_attn_pre),
                     norm_attn_post=(norm_attn_post, m_norm_attn_post, v_norm_attn_post),
                     q_latent_norm=(q_latent_norm, m_q_latent_norm, v_q_latent_norm),
                     kv_latent_norm=(kv_latent_norm, m_kv_latent_norm, v_kv_latent_norm),
                     norm_mlp_pre=(norm_mlp_pre, m_norm_mlp_pre, v_norm_mlp_pre),
                     norm_mlp_post=(norm_mlp_post, m_norm_mlp_post, v_norm_mlp_post))
    gain_grads = dict(norm_attn_pre=dg_attn_pre, norm_attn_post=dg_attn_post, q_latent_norm=dg_q,
                      kv_latent_norm=dg_kv, norm_mlp_pre=dg_mlp_pre, norm_mlp_post=dg_mlp_post)
    packed = jnp.concatenate([gain_grads[n] for n in gain_names], axis=1)
    gain_parts = _Exchange([packed], "gather").standalone("gather_gain_grads")[0]
    pack3 = lambda i: jnp.concatenate([gain_args[n][i] for n in gain_names], axis=1)
    gain_out = _adamw(gain_parts, pack3(0), pack3(1), pack3(2), "adamw_gains", tb=1)
    offs = [0]
    for n in gain_names:
        offs.append(offs[-1] + gain_args[n][0].shape[1])
    small_out = {n: tuple(o[:, offs[i]:offs[i + 1]] for o in gain_out) for i, n in enumerate(gain_names)}

    loss = lax.psum(loss_part[0, 0], ("x", "y", "c"))

    order = ["norm_attn_pre", "norm_attn_post", "w_in", "q_latent_norm", "kv_latent_norm", "w_uq", "w_ukv", "w_out",
             "norm_mlp_pre", "norm_mlp_post", "w_up", "w_down"]
    res = {n: (small_out[n] if n in small_out else tuple(o[None] for o in big_out[n])) for n in order}
    return (loss, grad_x[None], *[res[n][0] for n in order], *[res[n][1] for n in order],
            *[res[n][2] for n in order], *[res[n][3] for n in order])
```

```python
import functools
import math

import jax
import jax.numpy as jnp
from jax import lax
from jax.experimental import pallas as pl
from jax.experimental.pallas import tpu as pltpu

F32 = jnp.float32
BF16 = jnp.bfloat16

N_DEV = 8
HEADS = 8
HD = 128
AW = HEADS * HD
Q_LORA = 512
KV_LORA = 512
ROPE_MLA = 64
ROT_A = 32
IN_COLS = 3 * AW + Q_LORA + KV_LORA + ROPE_MLA
IN_PAD = 3 * AW + Q_LORA + KV_LORA + HD
QBLK = 128
DSWA_DILATIONS = (1, 4, 16)
ROPE_THETA = 500000.0
NORM_EPS = 1e-6
NEG_INF = -1e30
SCALE_A = HD ** -0.5
SCALE_B = (HD + ROPE_MLA) ** -0.5

ADAM_LR = 0.001
ADAM_B1 = 0.9
ADAM_B2 = 0.999
ADAM_EPS = 1e-08
ADAM_WD = 0.01
ADAM_STEP = 10

VMEM_LIMIT = 48 * 1024 * 1024

NT = (((1,), (1,)), ((), ()))
NN = (((1,), (0,)), ((), ()))
TN = (((0,), (0,)), ((), ()))


def _dot(a, b, dims):
    return lax.dot_general(a, b, dims, preferred_element_type=F32)


def _params(*sem):
    return pltpu.CompilerParams(dimension_semantics=sem, vmem_limit_bytes=VMEM_LIMIT)


def _tile(n, want):
    t = min(n, want)
    while n % t:
        t //= 2
    return t


def _tile128(n, want):
    if n % 128:
        return n
    units = n // 128
    return 128 * max(u for u in range(1, max(want // 128, 1) + 1) if units % u == 0)


class _Exchange:
    def __init__(self, arrs, mode, rows=None):
        self.arrs = list(arrs)
        self.mode = mode
        self.rows = rows
        self.n = len(self.arrs)
        self.results = None
        hbm = pl.BlockSpec(memory_space=pltpu.HBM)
        self.specs = [hbm] * self.n
        shape = {"gather": lambda a: (N_DEV,) + a.shape, "scatter": lambda a: a.shape,
                 "pair": lambda a: (4,) + a.shape[1:],
                 "chips": lambda a: a.shape if rows is None else (a.shape[0], rows[1]) + a.shape[2:]}[mode]
        self.out_shape = [jax.ShapeDtypeStruct(shape(a), a.dtype) for a in self.arrs]
        n_sem = self.n * (N_DEV - 1)
        self.scratch = [pltpu.SemaphoreType.DMA((n_sem,)), pltpu.SemaphoreType.DMA((n_sem,)),
                        pltpu.SemaphoreType.DMA((self.n,))]

    def hooks(self, ins, outs, send_sems, recv_sems, local_sems):
        x, y, c = lax.axis_index("x"), lax.axis_index("y"), lax.axis_index("c")
        me = (x, y, c)
        sib = (x, y, 1 - c)
        chips = [(1 - x, y), (x, 1 - y), (1 - x, 1 - y)]
        slot = lambda p: 4 * p[0] + 2 * p[1] + p[2]
        chip_of = lambda p: 2 * p[0] + p[1]

        def rcopy(a, k, src, dst, to):
            i = a * (N_DEV - 1) + k
            return pltpu.make_async_remote_copy(src_ref=src, dst_ref=dst, send_sem=send_sems.at[i],
                                                recv_sem=recv_sems.at[i], device_id=to,
                                                device_id_type=pl.DeviceIdType.MESH)

        def window(a, chip_slot):
            if self.rows is None:
                return ins[a].at[chip_slot]
            return ins[a].at[chip_slot, pl.ds(self.rows[0], self.rows[1])]

        def local(a):
            if self.mode == "chips":
                return pltpu.make_async_copy(window(a, chip_of(me)), outs[a].at[chip_of(me)], local_sems.at[a])
            src = ins[a].at[slot(me)] if self.mode == "scatter" else ins[a]
            return pltpu.make_async_copy(src, outs[a].at[slot(me)], local_sems.at[a])

        def peer(rel):
            return (1 - x if rel & 4 else x, 1 - y if rel & 2 else y, 1 - c if rel & 1 else c)

        if self.mode == "pair":
            def start():
                for a in range(self.n):
                    for p in range(4):
                        rcopy(a, p, ins[a].at[2 * p + 1 - c], outs[a].at[p], sib).start()

            def middle():
                pass

            def finish():
                for a in range(self.n):
                    for p in range(4):
                        cp = rcopy(a, p, ins[a].at[2 * p + 1 - c], outs[a].at[p], sib)
                        cp.wait_send()
                        cp.wait_recv()
        elif self.mode == "chips":
            def start():
                for a in range(self.n):
                    local(a).start()
                    for j, chip in enumerate(chips):
                        rcopy(a, j, window(a, chip_of(chip)), outs[a].at[chip_of(me)], (*chip, c)).start()

            def middle():
                pass

            def finish():
                for a in range(self.n):
                    for j, chip in enumerate(chips):
                        cp = rcopy(a, j, window(a, chip_of(chip)), outs[a].at[chip_of(chip)], (*chip, c))
                        cp.wait_send()
                        cp.wait_recv()
                    local(a).wait()
        elif self.mode == "scatter":
            def start():
                for a in range(self.n):
                    local(a).start()
                    for rel in range(1, N_DEV):
                        rcopy(a, rel - 1, ins[a].at[slot(peer(rel))], outs[a].at[slot(me)], peer(rel)).start()

            def middle():
                pass

            def finish():
                for a in range(self.n):
                    for rel in range(1, N_DEV):
                        cp = rcopy(a, rel - 1, ins[a].at[slot(peer(rel))], outs[a].at[slot(peer(rel))], peer(rel))
                        cp.wait_send()
                        cp.wait_recv()
                    local(a).wait()
        else:
            def start():
                for a in range(self.n):
                    local(a).start()
                    rcopy(a, 0, ins[a], outs[a].at[slot(me)], sib).start()
                    for j, chip in enumerate(chips):
                        rcopy(a, 1 + j, ins[a], outs[a].at[slot(me)], (*chip, c)).start()

            def middle():
                for a in range(self.n):
                    for j, chip in enumerate(chips):
                        landed = outs[a].at[slot((*chip, c))]
                        rcopy(a, 1 + j, ins[a], landed, me).wait_recv()
                        rcopy(a, 4 + j, landed, landed, sib).start()

            def finish():
                for a in range(self.n):
                    rcopy(a, 0, ins[a], outs[a].at[slot(sib)], me).wait_recv()
                    for j, chip in enumerate(chips):
                        rcopy(a, 4 + j, ins[a], outs[a].at[slot((*chip, 1 - c))], me).wait_recv()
                    for k in range(N_DEV - 1):
                        rcopy(a, k, ins[a], outs[a].at[slot(me)], me).wait_send()
                    local(a).wait()

        return start, middle, finish

    def set_results(self, res):
        self.results = list(res)

    def standalone(self, name):
        n = self.n

        def body(*refs):
            start, middle, finish = self.hooks(refs[:n], refs[n:2 * n], *refs[2 * n:])
            start()
            middle()
            finish()

        self.results = pl.pallas_call(
            body, name=name, in_specs=self.specs, out_specs=self.specs, out_shape=self.out_shape,
            scratch_shapes=self.scratch, compiler_params=pltpu.CompilerParams(has_side_effects=True),
        )(*self.arrs)
        return self.results


class _Carried:
    def __init__(self, parts):
        self.parts = list(parts)
        self.n = sum(p.n for p in self.parts)
        self.arrs = [a for p in self.parts for a in p.arrs]
        self.specs = [s for p in self.parts for s in p.specs]
        self.out_shape = [s for p in self.parts for s in p.out_shape]
        self.scratch = [s for p in self.parts for s in p.scratch]

    def hooks(self, ins, outs, *sems):
        hooks, i = [], 0
        for j, p in enumerate(self.parts):
            hooks.append(p.hooks(ins[i:i + p.n], outs[i:i + p.n], *sems[3 * j:3 * j + 3]))
            i += p.n
        def phase(k):
            def run():
                for h in hooks:
                    h[k]()
            return run

        return phase(0), phase(1), phase(2)

    def set_results(self, res):
        i = 0
        for p in self.parts:
            p.set_results(res[i:i + p.n])
            i += p.n


def _call(body, name, grid, in_specs, out_specs, out_shape, args, scratch=(), sem=(), comm=None, prefetch=(),
          aliases=None):
    npf = len(prefetch)
    if isinstance(comm, (list, tuple)):
        comm = _Carried(comm)
    if comm is None:
        spec = pltpu.PrefetchScalarGridSpec(num_scalar_prefetch=npf, grid=grid, in_specs=list(in_specs),
                                            out_specs=list(out_specs), scratch_shapes=list(scratch))
        return pl.pallas_call(body, name=name, grid_spec=spec, out_shape=list(out_shape),
                              input_output_aliases=aliases or {},
                              compiler_params=_params(*sem))(*prefetch, *args)
    assert aliases is None
    ni, no, ns, n = len(in_specs), len(out_specs), len(scratch), comm.n
    steps = math.prod(grid)

    def wrapped(*refs):
        pf, refs = refs[:npf], refs[npf:]
        ins, c_ins = refs[:ni], refs[ni:ni + n]
        outs, c_outs = refs[ni + n:ni + n + no], refs[ni + n + no:ni + 2 * n + no]
        scr, c_scr = refs[ni + 2 * n + no:ni + 2 * n + no + ns], refs[ni + 2 * n + no + ns:]
        start, middle, finish = comm.hooks(c_ins, c_outs, *c_scr)
        step = pl.program_id(0)
        for ax in range(1, len(grid)):
            step = step * grid[ax] + pl.program_id(ax)
        pl.when(step == 0)(start)
        pl.when(step == steps // 2)(middle)
        body(*pf, *ins, *outs, *scr)
        pl.when(step == steps - 1)(finish)

    spec = pltpu.PrefetchScalarGridSpec(num_scalar_prefetch=npf, grid=grid, in_specs=list(in_specs) + comm.specs,
                                        out_specs=list(out_specs) + comm.specs,
                                        scratch_shapes=list(scratch) + comm.scratch)
    res = pl.pallas_call(
        wrapped, name=name, grid_spec=spec, out_shape=list(out_shape) + comm.out_shape,
        compiler_params=pltpu.CompilerParams(dimension_semantics=("arbitrary",) * len(grid),
                                             vmem_limit_bytes=VMEM_LIMIT, has_side_effects=True),
    )(*prefetch, *args, *comm.arrs)
    comm.set_results(res[no:])
    return res[:no]


def _mm(a, b, mode, out_dtypes, name, epilogue=None, extras=(), tm=1024, tn=1024, tk=2048, comm=None,
        b_shards=False, out_shards=False):
    if mode == "tn":
        K, M = a.shape
    else:
        M, K = a.shape
    if b_shards:
        N = b.shape[1] if mode == "nt" else N_DEV * b.shape[2]
    else:
        N = b.shape[0] if mode == "nt" else b.shape[1]
    tm, tn, tk = _tile128(M, tm), _tile128(N, tn), _tile128(K, tk)
    pair_k = b_shards and mode == "nt"
    if pair_k:
        tk = 2 * K // N_DEV
        b = b.reshape(N_DEV // 2, 2, *b.shape[1:])
    elif b_shards or out_shards:
        tn = N // N_DEV
    nk = K // tk
    dims = {"nn": NN, "nt": NT, "tn": TN}[mode]
    a_spec = (pl.BlockSpec((tk, tm), lambda i, j, k: (k, i)) if mode == "tn"
              else pl.BlockSpec((tm, tk), lambda i, j, k: (i, k)))
    if b_shards:
        b_spec = (pl.BlockSpec((None, 2, tn, tk // 2), lambda i, j, k: (k, 0, j, 0)) if mode == "nt"
                  else pl.BlockSpec((None, tk, tn), lambda i, j, k: (j, k, 0)))
    else:
        b_spec = (pl.BlockSpec((tn, tk), lambda i, j, k: (j, k)) if mode == "nt"
                  else pl.BlockSpec((tk, tn), lambda i, j, k: (k, j)))
    mn_spec = pl.BlockSpec((tm, tn), lambda i, j, k: (i, j))
    out_spec = pl.BlockSpec((None, tm, tn), lambda i, j, k: (j, i, 0)) if out_shards else mn_spec
    out_dims = (N_DEV, M, N // N_DEV) if out_shards else (M, N)
    n_ex = len(extras)
    n_out = len(out_dtypes)

    def finish(acc, ex, outs):
        res = (acc,) if epilogue is None else epilogue(acc, *[e[...] for e in ex])
        for o, r in zip(outs, res):
            o[...] = r.astype(o.dtype)

    def product(a_ref, b_ref):
        if pair_k:
            return _dot(a_ref[:, :tk // 2], b_ref[0], NT) + _dot(a_ref[:, tk // 2:], b_ref[1], NT)
        return _dot(a_ref[...], b_ref[...], dims)

    def body(*refs):
        a_ref, b_ref = refs[:2]
        ex = refs[2:2 + n_ex]
        outs = refs[2 + n_ex:2 + n_ex + n_out]
        if nk == 1:
            finish(product(a_ref, b_ref), ex, outs)
            return
        acc = refs[-1]
        k = pl.program_id(2)

        @pl.when(k == 0)
        def _():
            acc[...] = product(a_ref, b_ref)

        @pl.when(jnp.logical_and(k > 0, k < nk - 1))
        def _():
            acc[...] += product(a_ref, b_ref)

        @pl.when(k == nk - 1)
        def _():
            finish(acc[...] + product(a_ref, b_ref), ex, outs)

    out = _call(
        body, name, (M // tm, N // tn, nk), [a_spec, b_spec] + [mn_spec] * n_ex, [out_spec] * n_out,
        [jax.ShapeDtypeStruct(out_dims, dt) for dt in out_dtypes], (a, b, *extras),
        scratch=[] if nk == 1 else [pltpu.VMEM((tm, tn), F32)], sem=("parallel", "parallel", "arbitrary"),
        comm=comm)
    return out[0] if n_out == 1 else out


def _rms_fwd(x, gain, out_dtype, name, width=None, col_block=0, residual=None, tb=256, comm=None):
    T = x.shape[0]
    W = x.shape[1] if width is None else width
    tb = _tile(T, tb)
    has_res = residual is not None

    def body(*refs):
        x_ref, g_ref = refs[:2]
        o_ref = refs[-1]
        xf = x_ref[...]
        y = xf * lax.rsqrt(jnp.mean(xf * xf, axis=-1, keepdims=True) + NORM_EPS) * g_ref[...]
        if has_res:
            y = refs[2][...] + y
        o_ref[...] = y.astype(o_ref.dtype)

    row = pl.BlockSpec((tb, W), lambda i: (i, 0))
    ins = [x, gain] + ([residual] if has_res else [])
    return _call(
        body, name, (T // tb,),
        [pl.BlockSpec((tb, W), lambda i: (i, col_block)),
         pl.BlockSpec((1, W), lambda i: (0, 0))] + ([row] if has_res else []),
        [row], [jax.ShapeDtypeStruct((T, W), out_dtype)], ins, sem=("parallel",), comm=comm)[0]


def _rms_bwd(dy, x, gain, out_dtype, name, width=None, col_block=0, residual=None, tb=256, comm=None, into=None):
    T = dy.shape[0]
    W = x.shape[1] if width is None else width
    tb = _tile(T, tb)
    has_res = residual is not None

    def body(*refs):
        dy_ref, x_ref, g_ref = refs[:3]
        dx_ref, dg_ref = refs[-2:]
        i = pl.program_id(0)
        xf = x_ref[...]
        r = lax.rsqrt(jnp.mean(xf * xf, axis=-1, keepdims=True) + NORM_EPS)
        xn = xf * r
        dyf = dy_ref[...].astype(F32)
        dyg = dyf * g_ref[...]
        dx = r * (dyg - xn * jnp.mean(dyg * xn, axis=-1, keepdims=True))
        if has_res:
            dx = refs[3][...] + dx
        dx_ref[...] = dx.astype(dx_ref.dtype)

        @pl.when(i == 0)
        def _():
            dg_ref[...] = jnp.zeros_like(dg_ref)

        dg_ref[...] += jnp.sum(dyf * xn, axis=0, keepdims=True)

    row = pl.BlockSpec((tb, W), lambda i: (i, 0))
    vec = pl.BlockSpec((1, W), lambda i: (0, 0))
    ins = [dy, x, gain] + ([residual] if has_res else [])
    in_specs = [row, pl.BlockSpec((tb, W), lambda i: (i, col_block)), vec] + ([row] if has_res else [])
    if into is None:
        out_spec, out_struct, aliases = row, jax.ShapeDtypeStruct((T, W), out_dtype), None
    else:
        buf, buf_block = into
        out_spec = pl.BlockSpec((tb, W), lambda i: (i, buf_block))
        out_struct, aliases = jax.ShapeDtypeStruct(buf.shape, buf.dtype), {len(ins): 0}
        ins, in_specs = ins + [buf], in_specs + [pl.BlockSpec(memory_space=pl.ANY)]
    return _call(
        body, name, (T // tb,), in_specs, [out_spec, vec], [out_struct, jax.ShapeDtypeStruct((1, W), F32)], ins,
        sem=("arbitrary",), comm=comm, aliases=aliases)


def _rms(xf):
    r = lax.rsqrt(jnp.mean(xf * xf, axis=-1, keepdims=True) + NORM_EPS)
    return r, xf * r


def _rms_grad(dyf, xn, r, gain):
    dyg = dyf * gain
    return r * (dyg - xn * jnp.mean(dyg * xn, axis=-1, keepdims=True)), dyf * xn


def _accumulate_rows(i, ref, rows):
    @pl.when(i == 0)
    def _():
        ref[...] = jnp.zeros_like(ref)

    ref[...] += jnp.sum(rows, axis=0, keepdims=True)


def _norm_pair_fwd(y1, xs, gain_post, gain_pre, name, tb=256):
    T, D = xs.shape
    tb = _tile(T, tb)

    def body(y1_ref, xs_ref, gp_ref, gq_ref, x2_ref, h2_ref):
        x2 = xs_ref[...] + _rms(y1_ref[...])[1] * gp_ref[...]
        x2_ref[...] = x2
        h2_ref[...] = (_rms(x2)[1] * gq_ref[...]).astype(BF16)

    row = pl.BlockSpec((tb, D), lambda i: (i, 0))
    vec = pl.BlockSpec((1, D), lambda i: (0, 0))
    return pl.pallas_call(
        body, name=name, grid=(T // tb,), in_specs=[row, row, vec, vec], out_specs=[row, row],
        out_shape=[jax.ShapeDtypeStruct((T, D), F32), jax.ShapeDtypeStruct((T, D), BF16)],
        compiler_params=_params("parallel"),
    )(y1, xs, gain_post, gain_pre)


def _norm_pair_bwd(dh2, x2, gain_pre, dx3, y1, gain_post, name, tb=256):
    T, D = x2.shape
    tb = _tile(T, tb)

    def body(dh2_ref, x2_ref, gq_ref, dx3_ref, y1_ref, gp_ref, dx2_ref, dy1_ref, dgq_ref, dgp_ref):
        i = pl.program_id(0)
        r2, xn2 = _rms(x2_ref[...])
        d2, rows_q = _rms_grad(dh2_ref[...], xn2, r2, gq_ref[...])
        dx2 = dx3_ref[...] + d2
        dx2_ref[...] = dx2
        r1, yn1 = _rms(y1_ref[...])
        d1, rows_p = _rms_grad(dx2, yn1, r1, gp_ref[...])
        dy1_ref[...] = d1.astype(BF16)
        _accumulate_rows(i, dgq_ref, rows_q)
        _accumulate_rows(i, dgp_ref, rows_p)

    row = pl.BlockSpec((tb, D), lambda i: (i, 0))
    vec = pl.BlockSpec((1, D), lambda i: (0, 0))
    return pl.pallas_call(
        body, name=name, grid=(T // tb,), in_specs=[row, row, vec, row, row, vec], out_specs=[row, row, vec, vec],
        out_shape=[jax.ShapeDtypeStruct((T, D), F32), jax.ShapeDtypeStruct((T, D), BF16),
                   jax.ShapeDtypeStruct((1, D), F32), jax.ShapeDtypeStruct((1, D), F32)],
        compiler_params=_params("arbitrary"),
    )(dh2, x2, gain_pre, dx3, y1, gain_post)


def _loss_head(x2, y2, gain, target, name, tb=256):
    T, D = x2.shape
    tb = _tile(T, tb)

    def body(x2_ref, y2_ref, g_ref, t_ref, dx3_ref, dy2_ref, loss_ref, dg_ref):
        i = pl.program_id(0)
        r, yn = _rms(y2_ref[...])
        e = x2_ref[...] + yn * g_ref[...] - t_ref[...]
        dx3 = e * (1.0 / D)
        dx3_ref[...] = dx3
        dy2, rows = _rms_grad(dx3, yn, r, g_ref[...])
        dy2_ref[...] = dy2.astype(BF16)
        _accumulate_rows(i, dg_ref, rows)
        _accumulate_rows(i, loss_ref, 0.5 * jnp.mean(e * e, axis=-1, keepdims=True))

    row = pl.BlockSpec((tb, D), lambda i: (i, 0))
    vec = pl.BlockSpec((1, D), lambda i: (0, 0))
    return pl.pallas_call(
        body, name=name, grid=(T // tb,),
        in_specs=[row, row, vec, row],
        out_specs=[row, row, pl.BlockSpec((1, 1), lambda i: (0, 0)), vec],
        out_shape=[jax.ShapeDtypeStruct((T, D), F32), jax.ShapeDtypeStruct((T, D), BF16),
                   jax.ShapeDtypeStruct((1, 1), F32), jax.ShapeDtypeStruct((1, D), F32)],
        compiler_params=_params("arbitrary"),
    )(x2, y2, gain, target)


def _rope_tables(positions, rot_dim):
    half = rot_dim // 2
    inv_freq = ROPE_THETA ** (-jnp.arange(0, rot_dim, 2, dtype=F32) / rot_dim)
    ang = positions.astype(F32)[:, None] * inv_freq[None, :]
    cos, sin = jnp.cos(ang), jnp.sin(ang)
    T = positions.shape[0]
    ones = jnp.ones((T, HD - rot_dim), F32)
    cos_t = jnp.concatenate([cos, cos, ones], axis=1)
    sin_t = jnp.concatenate([-sin, sin, jnp.zeros_like(ones)], axis=1)
    return cos_t, sin_t


def _rotate(x, cos_t, sin_t, half):
    lane = lax.broadcasted_iota(jnp.int32, x.shape, 1)
    swapped = jnp.where(lane < half, pltpu.roll(x, HD - half, 1), pltpu.roll(x, half, 1))
    return x * cos_t + swapped * sin_t


def _rope_apply(x, cos_t, sin_t, half, n_blocks, is_rope, out_dtype, name, window=0, tb=256):
    T = x.shape[0]
    tb = _tile(T, tb)
    W = n_blocks * HD

    def body(x_ref, c_ref, s_ref, o_ref):
        for j in range(n_blocks):
            sl = slice(j * HD, (j + 1) * HD)
            xj = x_ref[:, sl]
            if is_rope(j):
                xj = _rotate(xj.astype(F32), c_ref[...], s_ref[...], half)
            o_ref[:, sl] = xj.astype(o_ref.dtype)

    tab = pl.BlockSpec((tb, HD), lambda i: (i, 0))
    return pl.pallas_call(
        body, name=name, grid=(T // tb,),
        in_specs=[pl.BlockSpec((tb, W), lambda i: (i, window)), tab, tab],
        out_specs=pl.BlockSpec((tb, W), lambda i: (i, 0)),
        out_shape=jax.ShapeDtypeStruct((T, W), out_dtype),
        compiler_params=_params("parallel"),
    )(x, cos_t, sin_t)


DSWA_TB = 2048


def _deinterleave(src, dst_ref, d, dtype):
    rows = src.shape[0] // d
    for r in range(d):
        dst_ref[r] = src[pl.ds(r, rows, stride=d), :].astype(dtype)


def _rope_dswa(proj, cos_t, sin_t, name, comm=None):
    T = proj.shape[0]
    tb = _tile(T, 2 * DSWA_TB)
    half = ROT_A // 2

    def body(x_ref, c_ref, s_ref, *rest):
        outs, scr = rest[:-1], rest[-1]
        j = pl.program_id(1)

        @pl.when(j < 2 * HEADS)
        def _():
            scr[...] = _rotate(x_ref[...], c_ref[...], s_ref[...], half)

        @pl.when(j >= 2 * HEADS)
        def _():
            scr[...] = x_ref[...]

        for o_ref, d in zip(outs, DSWA_DILATIONS):
            _deinterleave(scr, o_ref, d, BF16)

    blk = pl.BlockSpec((tb, HD), lambda i, j: (i, j))
    tab = pl.BlockSpec((tb, HD), lambda i, j: (i, 0))
    return _call(
        body, name, (T // tb, 3 * HEADS), [blk, tab, tab],
        [pl.BlockSpec((d, tb // d, HD), lambda i, j: (0, i, j)) for d in DSWA_DILATIONS],
        [jax.ShapeDtypeStruct((d, T // d, 3 * AW), BF16) for d in DSWA_DILATIONS], (proj, cos_t, sin_t),
        scratch=[pltpu.VMEM((tb, HD), F32)], sem=("parallel", "parallel"), comm=comm)


def _shared_key_grad(dk, cos_t, sin_t_neg, half, name, into, tb=512):
    T = dk.shape[0]
    tb = _tile(T, tb)
    buf, buf_block = into

    def body(d_ref, c_ref, s_ref, buf_ref, o_ref):
        tot = d_ref[:, HD:2 * HD]
        for h in range(1, HEADS):
            tot = tot + d_ref[:, h * QK + HD:(h + 1) * QK]
        o_ref[...] = _rotate(tot, c_ref[...], s_ref[...], half).astype(o_ref.dtype)

    tab = pl.BlockSpec((tb, HD), lambda i: (i, 0))
    return pl.pallas_call(
        body, name=name, grid=(T // tb,),
        in_specs=[pl.BlockSpec((tb, HEADS * QK), lambda i: (i, 0)), tab, tab, pl.BlockSpec(memory_space=pl.ANY)],
        out_specs=pl.BlockSpec((tb, HD), lambda i: (i, buf_block)),
        out_shape=jax.ShapeDtypeStruct(buf.shape, buf.dtype), input_output_aliases={3: 0},
        compiler_params=_params("parallel"),
    )(dk, cos_t, sin_t_neg, buf)


def _band_mask(n):
    row = lax.broadcasted_iota(jnp.int32, (QBLK, 2 * QBLK), 0)
    col = lax.broadcasted_iota(jnp.int32, (QBLK, 2 * QBLK), 1)
    in_prev = jnp.logical_and(jnp.logical_and(col < QBLK, col >= row), n > 0)
    in_cur = jnp.logical_and(col >= QBLK, col - QBLK <= row)
    return jnp.logical_or(in_prev, in_cur)


def _dswa_specs(nb, reverse=False):
    pos = (lambda n: nb - 1 - n) if reverse else (lambda n: n)
    cur = lambda c: pl.BlockSpec((None, QBLK, AW), lambda r, n: (r, pos(n), c))
    prev = lambda c: pl.BlockSpec((None, QBLK, AW), lambda r, n: (r, jnp.maximum(pos(n) - 1, 0), c))
    stat = pl.BlockSpec((None, QBLK, HD), lambda r, n: (r, pos(n), 0))
    return cur, prev, stat


def _relayout_spec(d, tb, per_head=True):
    if per_head:
        return pl.BlockSpec((d, tb // d, HD), lambda i, h: (0, i, h))
    return pl.BlockSpec((d, tb // d, HD), lambda i, h: (0, i, 0))


def _head_lane(x, h):
    lane = lax.broadcasted_iota(jnp.int32, x.shape, 1)
    return jnp.sum(jnp.where(lane == h, x, 0.0), axis=-1, keepdims=True)


def _dswa_fwd(qkv, name, comm=None):
    d, sd = qkv.shape[:2]
    nb = sd // QBLK

    def body(q_ref, kc_ref, kp_ref, vc_ref, vp_ref, o_ref, l_ref):
        mask = _band_mask(pl.program_id(1))
        l_ref[...] = jnp.zeros_like(l_ref)
        for h in range(HEADS):
            sl = slice(h * HD, (h + 1) * HD)
            keys = jnp.concatenate([kp_ref[:, sl], kc_ref[:, sl]], axis=0)
            vals = jnp.concatenate([vp_ref[:, sl], vc_ref[:, sl]], axis=0)
            s = jnp.where(mask, _dot(q_ref[:, sl], keys, NT) * SCALE_A, NEG_INF)
            m = jnp.max(s, axis=-1, keepdims=True)
            p = jnp.exp(s - m)
            den = jnp.sum(p, axis=-1, keepdims=True)
            o_ref[:, sl] = _dot((p / den).astype(BF16), vals, NN)
            l_ref[:, h:h + 1] = m + jnp.log(den)

    cur, prev, stat = _dswa_specs(nb)
    return _call(
        body, name, (d, nb), [cur(0), cur(1), prev(1), cur(2), prev(2)], [cur(0), stat],
        [jax.ShapeDtypeStruct((d, sd, AW), F32), jax.ShapeDtypeStruct((d, sd, HD), F32)],
        (qkv, qkv, qkv, qkv, qkv), sem=("parallel", "parallel"), comm=comm)


def _dswa_merge(outs, lses, name, comm=None):
    nc = len(DSWA_DILATIONS)
    T = outs[0].shape[0] * outs[0].shape[1]
    tb = _tile(T, DSWA_TB)

    def body(*refs):
        o_refs, l_refs = refs[:nc], refs[nc:2 * nc]
        out_ref, outb_ref = refs[2 * nc:2 * nc + 2]
        lt_refs = refs[2 * nc + 2:3 * nc + 2]
        o_nat, l_nat, lt_nat = refs[3 * nc + 2:4 * nc + 2], refs[4 * nc + 2:5 * nc + 2], refs[-1]
        h = pl.program_id(1)
        for c, d in enumerate(DSWA_DILATIONS):
            for r in range(d):
                o_nat[c][pl.ds(r, tb // d, stride=d), :] = o_refs[c][r]
                l_nat[c][pl.ds(r, tb // d, stride=d), :] = l_refs[c][r]
        ls = [l[...] for l in l_nat]
        m = functools.reduce(jnp.maximum, ls)
        es = [jnp.exp(l - m) for l in ls]
        tot = functools.reduce(lambda a, b: a + b, es)
        acc = _head_lane(es[0] / tot, h) * o_nat[0][...]
        for c in range(1, nc):
            acc = acc + _head_lane(es[c] / tot, h) * o_nat[c][...]
        out_ref[...] = acc
        outb_ref[...] = acc.astype(BF16)
        lt_nat[...] = m + jnp.log(tot)
        for c, d in enumerate(DSWA_DILATIONS):
            _deinterleave(lt_nat, lt_refs[c], d, F32)

    nat = pl.BlockSpec((tb, HD), lambda i, h: (i, h))
    by_d = [_relayout_spec(d, tb) for d in DSWA_DILATIONS]
    stat_by_d = [_relayout_spec(d, tb, per_head=False) for d in DSWA_DILATIONS]
    res = _call(
        body, name, (T // tb, HEADS), by_d + stat_by_d, [nat, nat] + stat_by_d,
        [jax.ShapeDtypeStruct((T, AW), F32), jax.ShapeDtypeStruct((T, AW), BF16)]
        + [jax.ShapeDtypeStruct((d, T // d, HD), F32) for d in DSWA_DILATIONS], (*outs, *lses),
        scratch=[pltpu.VMEM((tb, HD), F32)] * (2 * nc + 1), sem=("parallel", "arbitrary"), comm=comm)
    return res[0], res[1], res[2:]


def _dswa_delta(dout, out, name):
    nc = len(DSWA_DILATIONS)
    T = out.shape[0]
    tb = _tile(T, DSWA_TB)

    def body(do_ref, o_ref, *rest):
        dl_refs, dob_refs, dl_nat = rest[:nc], rest[nc:2 * nc], rest[-1]
        h = pl.program_id(1)
        lane = lax.broadcasted_iota(jnp.int32, (tb, HD), 1)
        mine = jnp.where(lane == h, jnp.sum(do_ref[...] * o_ref[...], axis=-1, keepdims=True), 0.0)

        @pl.when(h == 0)
        def _():
            dl_nat[...] = mine

        @pl.when(h > 0)
        def _():
            dl_nat[...] += mine

        for c, d in enumerate(DSWA_DILATIONS):
            _deinterleave(dl_nat, dl_refs[c], d, F32)
            _deinterleave(do_ref, dob_refs[c], d, BF16)

    nat = pl.BlockSpec((tb, HD), lambda i, h: (i, h))
    by_d = [_relayout_spec(d, tb) for d in DSWA_DILATIONS]
    stat_by_d = [_relayout_spec(d, tb, per_head=False) for d in DSWA_DILATIONS]
    res = pl.pallas_call(
        body, name=name, grid=(T // tb, HEADS),
        in_specs=[nat, nat], out_specs=stat_by_d + by_d,
        out_shape=[jax.ShapeDtypeStruct((d, T // d, HD), F32) for d in DSWA_DILATIONS]
        + [jax.ShapeDtypeStruct((d, T // d, AW), BF16) for d in DSWA_DILATIONS],
        scratch_shapes=[pltpu.VMEM((tb, HD), F32)],
        compiler_params=_params("parallel", "arbitrary"),
    )(dout, out)
    return res[:nc], res[nc:]


def _delta_prep(dout, col_block, out, name, tb=256):
    T = out.shape[0]
    tb = _tile(T, tb)

    def body(do_ref, o_ref, delta_ref, dob_ref):
        delta_ref[...] = jnp.zeros_like(delta_ref)
        for h in range(HEADS):
            sl = slice(h * HD, (h + 1) * HD)
            doh = do_ref[:, sl]
            delta_ref[:, h:h + 1] = jnp.sum(doh * o_ref[:, sl], axis=-1, keepdims=True)
            dob_ref[:, sl] = doh.astype(BF16)

    row = pl.BlockSpec((tb, AW), lambda i: (i, 0))
    return pl.pallas_call(
        body, name=name, grid=(T // tb,),
        in_specs=[pl.BlockSpec((tb, AW), lambda i: (i, col_block)), row],
        out_specs=[pl.BlockSpec((tb, HD), lambda i: (i, 0)), row],
        out_shape=[jax.ShapeDtypeStruct((T, HD), F32), jax.ShapeDtypeStruct((T, AW), BF16)],
        compiler_params=_params("parallel"),
    )(dout, out)


def _dswa_bwd(qkv, dout_b, lse_tot, delta, name, comm=None):
    d, sd = qkv.shape[:2]
    nb = sd // QBLK

    def body(q_ref, kc_ref, kp_ref, vc_ref, vp_ref, do_ref, l_ref, dl_ref, g_ref, carry_k, carry_v):
        mask = _band_mask(nb - 1 - pl.program_id(1))

        @pl.when(pl.program_id(1) == 0)
        def _():
            carry_k[...] = jnp.zeros_like(carry_k)
            carry_v[...] = jnp.zeros_like(carry_v)

        for h in range(HEADS):
            sl = slice(h * HD, (h + 1) * HD)
            qh, doh = q_ref[:, sl], do_ref[:, sl]
            keys = jnp.concatenate([kp_ref[:, sl], kc_ref[:, sl]], axis=0)
            vals = jnp.concatenate([vp_ref[:, sl], vc_ref[:, sl]], axis=0)
            s = jnp.where(mask, _dot(qh, keys, NT) * SCALE_A, NEG_INF)
            p = jnp.exp(s - l_ref[:, h:h + 1])
            ds = (p * (_dot(doh, vals, NT) - dl_ref[:, h:h + 1]) * SCALE_A).astype(BF16)
            g_ref[:, sl] = _dot(ds, keys, NN).astype(BF16)
            dk = _dot(ds, qh, TN)
            dv = _dot(p.astype(BF16), doh, TN)
            g_ref[:, AW + h * HD:AW + (h + 1) * HD] = (dk[QBLK:] + carry_k[:, sl]).astype(BF16)
            g_ref[:, 2 * AW + h * HD:2 * AW + (h + 1) * HD] = (dv[QBLK:] + carry_v[:, sl]).astype(BF16)
            carry_k[:, sl] = dk[:QBLK]
            carry_v[:, sl] = dv[:QBLK]

    cur, prev, stat = _dswa_specs(nb, reverse=True)
    out_spec = pl.BlockSpec((None, QBLK, 3 * AW), lambda r, n: (r, nb - 1 - n, 0))
    return _call(
        body, name, (d, nb), [cur(0), cur(1), prev(1), cur(2), prev(2), cur(0), stat, stat], [out_spec],
        [jax.ShapeDtypeStruct((d, sd, 3 * AW), BF16)], (qkv, qkv, qkv, qkv, qkv, dout_b, lse_tot, delta),
        scratch=[pltpu.VMEM((QBLK, AW), F32)] * 2, sem=("parallel", "arbitrary"), comm=comm)[0]


def _dswa_combine(grads, cos_t, sin_t_neg, name, width):
    T = grads[0].shape[0] * grads[0].shape[1]
    tb = _tile(T, DSWA_TB)
    half = ROT_A // 2

    def body(*refs):
        g_refs = refs[:len(grads)]
        c_ref, s_ref, out_ref, acc = refs[len(grads):]
        j = pl.program_id(1)
        for c, d in enumerate(DSWA_DILATIONS):
            for r in range(d):
                if c == 0:
                    acc[...] = g_refs[c][r].astype(F32)
                else:
                    acc[pl.ds(r, tb // d, stride=d), :] += g_refs[c][r].astype(F32)
        val = acc[...]
        out_ref[...] = jnp.where(j < 2 * HEADS, _rotate(val, c_ref[...], s_ref[...], half), val).astype(BF16)

    tab = pl.BlockSpec((tb, HD), lambda i, j: (i, 0))
    return pl.pallas_call(
        body, name=name, grid=(T // tb, 3 * HEADS),
        in_specs=[pl.BlockSpec((d, tb // d, HD), lambda i, j: (0, i, j)) for d in DSWA_DILATIONS] + [tab, tab],
        out_specs=pl.BlockSpec((tb, HD), lambda i, j: (i, j)),
        out_shape=jax.ShapeDtypeStruct((T, width), BF16),
        scratch_shapes=[pltpu.VMEM((tb, HD), F32)],
        compiler_params=_params("parallel", "parallel"),
    )(*grads, cos_t, sin_t_neg)


MLA_TQ = 512
QK = 2 * HD
LOG2E = 1.4426950408889634


def _triangle(nq, key_major):
    pairs = [(q, k) for q in range(nq) for k in range(q + 1)]
    if key_major:
        pairs.sort(key=lambda p: (p[1], p[0]))
    return (jnp.array([p[0] for p in pairs], jnp.int32), jnp.array([p[1] for p in pairs], jnp.int32))


def _mla_specs(tq):
    q_spec = pl.BlockSpec((tq, HEADS * QK), lambda t, qi, ki: (qi[t], 0))
    k_spec = pl.BlockSpec((tq, HEADS * QK), lambda t, qi, ki: (ki[t], 0))
    v_spec = pl.BlockSpec((tq, AW), lambda t, qi, ki: (ki[t], 1))
    qrow = pl.BlockSpec((tq, AW), lambda t, qi, ki: (qi[t], 0))
    krow = pl.BlockSpec((tq, AW), lambda t, qi, ki: (ki[t], 0))
    return q_spec, k_spec, v_spec, qrow, krow


def _mla_pack(kv, kr, name, tb=512):
    T = kv.shape[0]
    tb = _tile(T, tb)

    def body(kv_ref, kr_ref, k_ref, v1_ref):
        lane = lax.broadcasted_iota(jnp.int32, (tb, HD), 1)
        one_hot = jnp.where(lane == 0, 1.0, 0.0).astype(BF16)
        for h in range(HEADS):
            k_ref[:, h * QK:h * QK + HD] = kv_ref[:, h * HD:(h + 1) * HD]
            k_ref[:, h * QK + HD:(h + 1) * QK] = kr_ref[...]
            v1_ref[:, h * QK:h * QK + HD] = kv_ref[:, AW + h * HD:AW + (h + 1) * HD]
            v1_ref[:, h * QK + HD:(h + 1) * QK] = one_hot

    wide = pl.BlockSpec((tb, HEADS * QK), lambda i: (i, 0))
    return pl.pallas_call(
        body, name=name, grid=(T // tb,),
        in_specs=[pl.BlockSpec((tb, 2 * AW), lambda i: (i, 0)), pl.BlockSpec((tb, HD), lambda i: (i, 0))],
        out_specs=[wide, wide], out_shape=[jax.ShapeDtypeStruct((T, HEADS * QK), BF16)] * 2,
        compiler_params=_params("parallel"),
    )(kv, kr)


def _mla_stat_spec(tq):
    return pl.BlockSpec((tq, HD), lambda t, qi, ki: (qi[t], 0))


def _mla_scores(q_ref, k_ref, h, qi, ki, tq):
    s = _dot(q_ref[:, h * QK:(h + 1) * QK], k_ref[:, h * QK:(h + 1) * QK], NT) * SCALE_B
    row = lax.broadcasted_iota(jnp.int32, s.shape, 0) + qi * tq
    col = lax.broadcasted_iota(jnp.int32, s.shape, 1) + ki * tq
    return jnp.where(col <= row, s, NEG_INF)


def _mla_fwd(q, k, v1, name, comm=None):
    T = q.shape[0]
    tq = _tile(T, MLA_TQ)
    tables = _triangle(T // tq, False)

    def body(qi_ref, ki_ref, q_ref, k_ref, v_ref, o_ref, ob_ref, l_ref, m_s, acc):
        t = pl.program_id(0)
        qi, ki = qi_ref[t], ki_ref[t]

        @pl.when(ki == 0)
        def _():
            m_s[...] = jnp.full_like(m_s, NEG_INF)
            acc[...] = jnp.zeros_like(acc)

        row = lax.broadcasted_iota(jnp.int32, (tq, tq), 0) + qi * tq
        col = lax.broadcasted_iota(jnp.int32, (tq, tq), 1) + ki * tq
        bias = jnp.where(col <= row, 0.0, NEG_INF)
        updates = []
        for h in range(HEADS):
            s = _dot(q_ref[:, h * QK:(h + 1) * QK], k_ref[:, h * QK:(h + 1) * QK], NT) + bias
            m_new = jnp.maximum(m_s[h], jnp.max(s, axis=-1, keepdims=True))
            p = jnp.exp2((s - m_new) * (SCALE_B * LOG2E)).astype(BF16)
            alpha = jnp.exp2((m_s[h] - m_new) * (SCALE_B * LOG2E))
            updates.append((m_new, alpha, _dot(p, v_ref[:, h * QK:(h + 1) * QK], NN)))
        for h, (m_new, alpha, pv) in enumerate(updates):
            acc[:, h * QK:(h + 1) * QK] = alpha * acc[:, h * QK:(h + 1) * QK] + pv
            m_s[h] = m_new

        @pl.when(ki == qi)
        def _():
            l_ref[...] = jnp.zeros_like(l_ref)
            for h in range(HEADS):
                sl = slice(h * HD, (h + 1) * HD)
                den = acc[:, h * QK + HD:h * QK + HD + 1]
                out = acc[:, h * QK:h * QK + HD] / den
                o_ref[:, sl] = out
                ob_ref[:, sl] = out.astype(BF16)
                l_ref[:, h:h + 1] = m_s[h] * SCALE_B + jnp.log(den)

    q_spec, k_spec, _, qrow, _ = _mla_specs(tq)
    return _call(
        body, name, (tables[0].shape[0],), [q_spec, k_spec, k_spec], [qrow, qrow, _mla_stat_spec(tq)],
        [jax.ShapeDtypeStruct((T, AW), F32), jax.ShapeDtypeStruct((T, AW), BF16), jax.ShapeDtypeStruct((T, HD), F32)],
        (q, k, v1),
        scratch=[pltpu.VMEM((HEADS, tq, 1), F32), pltpu.VMEM((tq, HEADS * QK), F32)],
        sem=("arbitrary",), comm=comm, prefetch=tables)


def _mla_ds(q_ref, k_ref, v_ref, do_ref, l_ref, dl_ref, h, qi, ki, tq):
    sl = slice(h * HD, (h + 1) * HD)
    p = jnp.exp(_mla_scores(q_ref, k_ref, h, qi, ki, tq) - l_ref[:, h:h + 1])
    ds = (p * (_dot(do_ref[:, sl], v_ref[:, sl], NT) - dl_ref[:, h:h + 1]) * SCALE_B).astype(BF16)
    return p, ds


def _mla_bwd_q(q, k, kv, dout_b, lse, delta, name, comm=None):
    T = q.shape[0]
    tq = _tile(T, MLA_TQ)
    tables = _triangle(T // tq, False)

    def body(qi_ref, ki_ref, q_ref, k_ref, v_ref, do_ref, l_ref, dl_ref, dq_ref):
        t = pl.program_id(0)
        qi, ki = qi_ref[t], ki_ref[t]

        @pl.when(ki == 0)
        def _():
            dq_ref[...] = jnp.zeros_like(dq_ref)

        for h in range(HEADS):
            _, ds = _mla_ds(q_ref, k_ref, v_ref, do_ref, l_ref, dl_ref, h, qi, ki, tq)
            dq_ref[:, h * QK:(h + 1) * QK] += _dot(ds, k_ref[:, h * QK:(h + 1) * QK], NN)

    q_spec, k_spec, v_spec, qrow, _ = _mla_specs(tq)
    return _call(
        body, name, (tables[0].shape[0],), [q_spec, k_spec, v_spec, qrow, _mla_stat_spec(tq), _mla_stat_spec(tq)], [q_spec],
        [jax.ShapeDtypeStruct((T, HEADS * QK), F32)], (q, k, kv, dout_b, lse, delta),
        sem=("arbitrary",), comm=comm, prefetch=tables)[0]


def _mla_bwd_kv(q, k, kv, dout_b, lse, delta, name, comm=None):
    T = q.shape[0]
    tq = _tile(T, MLA_TQ)
    nq = T // tq
    tables = _triangle(nq, True)

    def body(qi_ref, ki_ref, q_ref, k_ref, v_ref, do_ref, l_ref, dl_ref, dk_ref, dkv_ref, dv_acc):
        t = pl.program_id(0)
        qi, ki = qi_ref[t], ki_ref[t]

        @pl.when(qi == ki)
        def _():
            dk_ref[...] = jnp.zeros_like(dk_ref)
            dv_acc[...] = jnp.zeros_like(dv_acc)

        for h in range(HEADS):
            sl = slice(h * HD, (h + 1) * HD)
            p, ds = _mla_ds(q_ref, k_ref, v_ref, do_ref, l_ref, dl_ref, h, qi, ki, tq)
            dv_acc[:, sl] += _dot(p.astype(BF16), do_ref[:, sl], TN)
            dk_ref[:, h * QK:(h + 1) * QK] += _dot(ds, q_ref[:, h * QK:(h + 1) * QK], TN)

        @pl.when(qi == nq - 1)
        def _():
            for h in range(HEADS):
                dkv_ref[:, h * HD:(h + 1) * HD] = dk_ref[:, h * QK:h * QK + HD].astype(BF16)
                dkv_ref[:, AW + h * HD:AW + (h + 1) * HD] = dv_acc[:, h * HD:(h + 1) * HD].astype(BF16)

    q_spec, k_spec, v_spec, qrow, _ = _mla_specs(tq)
    return _call(
        body, name, (tables[0].shape[0],),
        [q_spec, k_spec, v_spec, qrow, _mla_stat_spec(tq), _mla_stat_spec(tq)],
        [k_spec, pl.BlockSpec((tq, 2 * AW), lambda t, qi, ki: (ki[t], 0))],
        [jax.ShapeDtypeStruct((T, HEADS * QK), F32), jax.ShapeDtypeStruct((T, 2 * AW), BF16)],
        (q, k, kv, dout_b, lse, delta), scratch=[pltpu.VMEM((tq, AW), F32)],
        sem=("arbitrary",), comm=comm, prefetch=tables)


def _pair_sum(by_device, from_sibling, name, tb=256):
    n_chip, R, C = from_sibling.shape
    tb = _tile(R, tb)
    core = jnp.reshape(lax.axis_index("c"), (1,)).astype(jnp.int32)

    def body(core_ref, mine_ref, theirs_ref, o_ref):
        o_ref[...] = (mine_ref[...].astype(F32) + theirs_ref[...].astype(F32)).astype(o_ref.dtype)

    blk = pl.BlockSpec((None, tb, C), lambda p, i, core_ref: (p, i, 0))
    return _call(
        body, name, (n_chip, R // tb),
        [pl.BlockSpec((None, tb, C), lambda p, i, core_ref: (2 * p + core_ref[0], i, 0)), blk], [blk],
        [jax.ShapeDtypeStruct((n_chip, R, C), BF16)], (by_device, from_sibling),
        sem=("parallel", "parallel"), prefetch=(core,))[0]


def _adamw(parts, w, m, v, name, tb=128, comm=None):
    R, C = w.shape
    pieces = list(parts) if isinstance(parts, (list, tuple)) else [parts]
    n_parts = pieces[0].shape[0]
    tb = _tile(R, tb)
    starts = [0]
    for p in pieces:
        assert p.shape[1] % tb == 0
        starts.append(starts[-1] + p.shape[1] // tb)
    c1 = 1.0 - ADAM_B1
    c2 = 1.0 - ADAM_B2
    bc1 = 1.0 - ADAM_B1 ** ADAM_STEP
    bc2 = 1.0 - ADAM_B2 ** ADAM_STEP

    def body(*refs):
        p_refs = refs[:len(pieces)]
        w_ref, m_ref, v_ref, g_ref, d_ref, nm_ref, nv_ref = refs[len(pieces):]
        i = pl.program_id(0)
        g = None
        for k, p_ref in enumerate(p_refs):
            gk = p_ref[0].astype(F32)
            for j in range(1, n_parts):
                gk = gk + p_ref[j].astype(F32)
            g = gk if g is None else jnp.where(i >= starts[k], gk, g)
        nm = ADAM_B1 * m_ref[...] + c1 * g
        nv = ADAM_B2 * v_ref[...] + c2 * (g * g)
        g_ref[...] = g
        nm_ref[...] = nm
        nv_ref[...] = nv
        d_ref[...] = -ADAM_LR * ((nm / bc1) / (jnp.sqrt(nv / bc2) + ADAM_EPS) + ADAM_WD * w_ref[...])

    row = pl.BlockSpec((tb, C), lambda i: (i, 0))
    piece_specs = [pl.BlockSpec((n_parts, tb, C),
                                functools.partial(lambda i, lo, hi: (0, jnp.clip(i - lo, 0, hi - lo - 1), 0),
                                                  lo=starts[k], hi=starts[k + 1]))
                   for k in range(len(pieces))]
    return _call(
        body, name, (R // tb,), piece_specs + [row, row, row], [row] * 4,
        [jax.ShapeDtypeStruct((R, C), F32)] * 4, (*pieces, w, m, v), sem=("parallel",), comm=comm)


def _cols_from_shards(g):
    return jnp.transpose(g, (1, 0, 2)).reshape(g.shape[1], N_DEV * g.shape[2])


def _cols_to_shards(w):
    return jnp.transpose(w.reshape(w.shape[0], N_DEV, w.shape[1] // N_DEV), (1, 0, 2))


def _split_heads(w, first):
    w3 = w.reshape(w.shape[0], HEADS, -1)
    return w3[:, :, :first].reshape(w.shape[0], -1), w3[:, :, first:].reshape(w.shape[0], -1)


def _join_heads(a, b):
    R = a.shape[0]
    return jnp.concatenate([a.reshape(R, HEADS, -1), b.reshape(R, HEADS, -1)], axis=2).reshape(R, -1)


def _join_shards_padded(g, width, name, tb=256):
    n, R, cs = g.shape
    tb = _tile(R, tb)

    def body(g_ref, o_ref):
        for j in range(n):
            o_ref[:, j * cs:(j + 1) * cs] = g_ref[j]
        o_ref[:, n * cs:] = jnp.zeros((tb, width - n * cs), o_ref.dtype)

    return pl.pallas_call(
        body, name=name, grid=(R // tb,),
        in_specs=[pl.BlockSpec((n, tb, cs), lambda i: (0, i, 0))],
        out_specs=pl.BlockSpec((tb, width), lambda i: (i, 0)),
        out_shape=jax.ShapeDtypeStruct((R, width), g.dtype),
        compiler_params=_params("parallel"),
    )(g)


def _split_shards(w, cs, name, tb=256):
    R, width = w.shape
    tb = _tile(R, tb)

    def body(w_ref, o_ref):
        for j in range(N_DEV):
            o_ref[j] = w_ref[:, j * cs:(j + 1) * cs]

    return pl.pallas_call(
        body, name=name, grid=(R // tb,),
        in_specs=[pl.BlockSpec((tb, width), lambda i: (i, 0))],
        out_specs=pl.BlockSpec((N_DEV, tb, cs), lambda i: (0, i, 0)),
        out_shape=jax.ShapeDtypeStruct((N_DEV, R, cs), w.dtype),
        compiler_params=_params("parallel"),
    )(w)


def _pad_heads(w, width):
    w3 = w.reshape(w.shape[0], HEADS, -1)
    return jnp.pad(w3, ((0, 0), (0, 0), (0, width - w3.shape[2]))).reshape(w.shape[0], HEADS * width)


def _unpad_heads(w, k):
    return w.reshape(w.shape[0], HEADS, -1)[:, :, :k].reshape(w.shape[0], HEADS * k)


def kernel(x, positions, norm_attn_pre, norm_attn_post, w_in, q_latent_norm, kv_latent_norm, w_uq, w_ukv, w_out, norm_mlp_pre, norm_mlp_post, w_up, w_down, loss_target, m_norm_attn_pre, m_norm_attn_post, m_w_in, m_q_latent_norm, m_kv_latent_norm, m_w_uq, m_w_ukv, m_w_out, m_norm_mlp_pre, m_norm_mlp_post, m_w_up, m_w_down, v_norm_attn_pre, v_norm_attn_post, v_w_in, v_q_latent_norm, v_kv_latent_norm, v_w_uq, v_w_ukv, v_w_out, v_norm_mlp_pre, v_norm_mlp_post, v_w_up, v_w_down):
    xs = x[0]
    tgt = loss_target[0]
    pos = positions[0]
    T, D = xs.shape
    big = dict(w_in=(w_in, m_w_in, v_w_in), w_uq=(w_uq, m_w_uq, v_w_uq), w_ukv=(w_ukv, m_w_ukv, v_w_ukv),
               w_out=(w_out, m_w_out, v_w_out), w_up=(w_up, m_w_up, v_w_up), w_down=(w_down, m_w_down, v_w_down))
    big = {n: tuple(t[0] for t in ts) for n, ts in big.items()}
    big_names = ["w_in", "w_uq", "w_ukv", "w_out", "w_up", "w_down"]
    col_sharded = {"w_in", "w_uq", "w_ukv", "w_up"}

    wb = {n: big[n][0].astype(BF16) for n in big_names}

    def gathered(ex, i, n):
        g = ex.results[i]
        return _cols_from_shards(g) if n in col_sharded else g.reshape(-1, g.shape[2])

    def by_device(g, n):
        return _cols_to_shards(g) if n in col_sharded else g.reshape(N_DEV, g.shape[0] // N_DEV, g.shape[1])

    def scatter_of(g, n):
        return _Exchange([by_device(g, n)], "scatter")

    cos_a, sin_a = _rope_tables(pos, ROT_A)
    cos_b, sin_b = _rope_tables(pos, ROPE_MLA)

    ex_in = _Exchange([wb["w_in"]], "gather")
    h1 = _rms_fwd(xs, norm_attn_pre, BF16, "norm_attn_pre_fwd", comm=ex_in)
    Wi = _join_shards_padded(ex_in.results[0], IN_PAD, "w_in_join")
    ex_mid = _Exchange([wb["w_uq"], wb["w_ukv"], wb["w_out"]], "gather")
    proj = _mm(h1, Wi, "nn", [F32], "proj_in", tn=1408, comm=ex_mid)
    Wuq = _pad_heads(gathered(ex_mid, 0, "w_uq"), QK)
    Wukv = jnp.concatenate(_split_heads(gathered(ex_mid, 1, "w_ukv"), HD), axis=1)
    Wo = gathered(ex_mid, 2, "w_out")
    n_piece = wb["w_down"].shape[0] // 8
    ex_down = [_Exchange([wb["w_down"][i * n_piece:(i + 1) * n_piece]], "gather") for i in range(8)]
    qkv_by_d = _rope_dswa(proj, cos_a, sin_a, "rope_dswa", comm=ex_down[0])
    outs, lses = [], []
    for d, qkv, ex in zip(DSWA_DILATIONS, qkv_by_d, ex_down[1:4]):
        o, l = _dswa_fwd(qkv, f"dswa_fwd_d{d}", comm=ex)
        outs.append(o)
        lses.append(l)
    a_out, a_out_b, a_lse_by_d = _dswa_merge(outs, lses, "dswa_merge", comm=ex_down[4])

    cqn = _rms_fwd(proj, q_latent_norm, BF16, "q_latent_norm_fwd", width=Q_LORA, col_block=3 * AW // Q_LORA)
    ckvn = _rms_fwd(proj, kv_latent_norm, BF16, "kv_latent_norm_fwd", width=KV_LORA, col_block=3 * AW // KV_LORA + 1)
    qb = _mm(cqn, Wuq, "nn", [F32], "q_up")
    kvb = _mm(ckvn, Wukv, "nn", [BF16], "kv_up")
    odd = lambda j: j % 2 == 1
    q_mla = _rope_apply(qb, cos_b, sin_b, ROPE_MLA // 2, 2 * HEADS, odd, BF16, "rope_mla_q")
    kr = _rope_apply(proj, cos_b, sin_b, ROPE_MLA // 2, 1, lambda j: True, BF16, "rope_mla_k",
                     window=(IN_PAD - HD) // HD)
    k_mla, v1_mla = _mla_pack(kvb, kr, "mla_pack")
    ex_up = _Exchange([wb["w_up"]], "gather")
    b_out, b_out_b, b_lse = _mla_fwd(q_mla, k_mla, v1_mla, "mla_fwd", comm=ex_up)
    Wup_shards = ex_up.results[0]

    mixed = jnp.concatenate([a_out_b, b_out_b], axis=1)
    y1 = _mm(mixed, Wo, "nn", [F32], "attn_out", comm=ex_down[5])

    x2, h2 = _norm_pair_fwd(y1, xs, norm_attn_post, norm_mlp_pre, "norm_attn_post_mlp_pre_fwd")

    def relu2(z):
        r = jnp.maximum(z, 0.0)
        return r * r, r

    u, zr = _mm(h2, Wup_shards, "nn", [BF16, BF16], "mlp_up", epilogue=relu2, comm=ex_down[6:8],
                b_shards=True)
    Wdn = jnp.concatenate([ex.results[0] for ex in ex_down], axis=1).reshape(-1, D)
    y2 = _mm(u, Wdn, "nn", [F32], "mlp_down")
    dx3, dy2, loss_part, dg_mlp_post = _loss_head(x2, y2, norm_mlp_post, tgt, "loss_head")

    dz =_mm(dy2, Wdn, "nt", [BF16], "mlp_down_dx", epilogue=lambda du, r: (du * (2.0 * r.astype(F32)),), extras=(zr,))
    g_down = _mm(u, dy2, "tn", [BF16], "mlp_down_dw")
    down_dev = by_device(g_down, "w_down")
    pair_down = _Exchange([down_dev], "pair")
    up_dev = _mm(h2, dz, "tn", [BF16], "mlp_up_dw", comm=pair_down, out_shards=True)
    down_chip = _pair_sum(down_dev, pair_down.results[0], "pair_sum_w_down")
    pair_up = _Exchange([up_dev], "pair")
    cut = 5 * down_chip.shape[1] // 8
    sc_down = [_Exchange([down_chip], "chips", rows=(0, cut)),
               _Exchange([down_chip], "chips", rows=(cut, down_chip.shape[1] - cut))]
    dh2 = _mm(dz, Wup_shards, "nt", [F32], "mlp_up_dx", comm=[pair_up, sc_down[0]], b_shards=True)
    up_chip = _pair_sum(up_dev, pair_up.results[0], "pair_sum_w_up")

    dx2, dy1, dg_mlp_pre, dg_attn_post = _norm_pair_bwd(dh2, x2, norm_mlp_pre, dx3, y1, norm_attn_post,
                                                        "norm_mlp_pre_attn_post_bwd")
    dmixed = _mm(dy1, Wo, "nt", [F32], "attn_out_dx")
    g_out = _mm(mixed, dy1, "tn", [BF16], "attn_out_dw")

    b_delta, b_dout = _delta_prep(dmixed, 1, b_out, "mla_delta")
    sc_up = _Exchange([up_chip], "chips")
    dq_mla = _mla_bwd_q(q_mla, k_mla, kvb, b_dout, b_lse, b_delta, "mla_bwd_q", comm=sc_up)
    sc_out = scatter_of(g_out, "w_out")
    dk_mla, dkvb = _mla_bwd_kv(q_mla, k_mla, kvb, b_dout, b_lse, b_delta, "mla_bwd_kv", comm=[sc_out, sc_down[1]])
    dqb = _rope_apply(dq_mla, cos_b, -sin_b, ROPE_MLA // 2, 2 * HEADS, odd, BF16, "rope_mla_q_bwd")
    g_uq_pad = _mm(cqn, dqb, "tn", [BF16], "q_up_dw")
    g_ukv_perm = _mm(ckvn, dkvb, "tn", [BF16], "kv_up_dw")
    dcqn = _mm(dqb, Wuq, "nt", [F32], "q_up_dx")
    dckvn = _mm(dkvb, Wukv, "nt", [F32], "kv_up_dx")

    g_uq = _unpad_heads(g_uq_pad, HD + ROPE_MLA)
    g_ukv = _join_heads(g_ukv_perm[:, :AW], g_ukv_perm[:, AW:])
    sc_uq = _Exchange([_cols_to_shards(g_uq), _cols_to_shards(g_ukv)], "scatter")
    a_delta_by_d, a_dout_by_d = _dswa_delta(dmixed, a_out, "dswa_delta")
    a_grads = [_dswa_bwd(qkv_by_d[c], a_dout_by_d[c], a_lse_by_d[c], a_delta_by_d[c], f"dswa_bwd_d{d}")
               for c, d in enumerate(DSWA_DILATIONS)]
    dproj = _dswa_combine(a_grads, cos_a, -sin_a, "dswa_combine", IN_PAD)
    dproj, dg_q = _rms_bwd(dcqn, proj, q_latent_norm, BF16, "q_latent_norm_bwd", width=Q_LORA,
                           col_block=3 * AW // Q_LORA, into=(dproj, 3 * AW // Q_LORA))
    dproj, dg_kv = _rms_bwd(dckvn, proj, kv_latent_norm, BF16, "kv_latent_norm_bwd", width=KV_LORA,
                            col_block=3 * AW // KV_LORA + 1, into=(dproj, 3 * AW // KV_LORA + 1))
    dproj = _shared_key_grad(dk_mla, cos_b, -sin_b, ROPE_MLA // 2, "rope_mla_k_bwd", (dproj, (IN_PAD - HD) // HD))
    g_in_pad = _mm(h1, dproj, "tn", [BF16], "proj_in_dw", tn=1408, comm=sc_uq)
    in_dev = _split_shards(g_in_pad, IN_COLS // N_DEV, "w_in_grad_split")
    pair_in = _Exchange([in_dev], "pair").standalone("pair_w_in")
    sc_in = _Exchange([_pair_sum(in_dev, pair_in[0], "pair_sum_w_in")], "chips")
    dh1 = _mm(dproj, Wi, "nt", [F32], "proj_in_dx", comm=sc_in)
    grad_x, dg_attn_pre = _rms_bwd(dh1, xs, norm_attn_pre, F32, "norm_attn_pre_bwd", residual=dx2)

    parts = dict(w_in=sc_in.results[0], w_uq=sc_uq.results[0], w_ukv=sc_uq.results[1], w_out=sc_out.results[0],
                 w_up=sc_up.results[0], w_down=[sc.results[0] for sc in sc_down])
    big_out = {n: _adamw(parts[n], *big[n], f"adamw_{n}") for n in big_names}

    gain_names = ["norm_attn_pre", "norm_attn_post", "q_latent_norm", "kv_latent_norm", "norm_mlp_pre", "norm_mlp_post"]
    gain_args = dict(norm_attn_pre=(norm_attn_pre, m_norm_attn_pre, v_norm_attn_pre),
                     norm_attn_post=(norm_attn_post, m_norm_attn_post, v_norm_attn_post),
                     q_latent_norm=(q_latent_norm, m_q_latent_norm, v_q_latent_norm),
                     kv_latent_norm=(kv_latent_norm, m_kv_latent_norm, v_kv_latent_norm),
                     norm_mlp_pre=(norm_mlp_pre, m_norm_mlp_pre, v_norm_mlp_pre),
                     norm_mlp_post=(norm_mlp_post, m_norm_mlp_post, v_norm_mlp_post))
    gain_grads = dict(norm_attn_pre=dg_attn_pre, norm_attn_post=dg_attn_post, q_latent_norm=dg_q,
                      kv_latent_norm=dg_kv, norm_mlp_pre=dg_mlp_pre, norm_mlp_post=dg_mlp_post)
    packed = jnp.concatenate([gain_grads[n] for n in gain_names], axis=1)
    gain_parts = _Exchange([packed], "gather").standalone("gather_gain_grads")[0]
    pack3 = lambda i: jnp.concatenate([gain_args[n][i] for n in gain_names], axis=1)
    gain_out = _adamw(gain_parts, pack3(0), pack3(1), pack3(2), "adamw_gains", tb=1)
    offs = [0]
    for n in gain_names:
        offs.append(offs[-1] + gain_args[n][0].shape[1])
    small_out = {n: tuple(o[:, offs[i]:offs[i + 1]] for o in gain_out) for i, n in enumerate(gain_names)}

    loss = lax.psum(loss_part[0, 0], ("x", "y", "c"))

    order = ["norm_attn_pre", "norm_attn_post", "w_in", "q_latent_norm", "kv_latent_norm", "w_uq", "w_ukv", "w_out",
             "norm_mlp_pre", "norm_mlp_post", "w_up", "w_down"]
    res = {n: (small_out[n] if n in small_out else tuple(o[None] for o in big_out[n])) for n in order}
    return (loss, grad_x[None], *[res[n][0] for n in order], *[res[n][1] for n in order],
            *[res[n][2] for n in order], *[res[n][3] for n in order])
```

```python
import functools
import math

import jax
import jax.numpy as jnp
from jax import lax
from jax.experimental import pallas as pl
from jax.experimental.pallas import tpu as pltpu

F32 = jnp.float32
BF16 = jnp.bfloat16

N_DEV = 8
HEADS = 8
HD = 128
AW = HEADS * HD
Q_LORA = 512
KV_LORA = 512
ROPE_MLA = 64
ROT_A = 32
IN_COLS = 3 * AW + Q_LORA + KV_LORA + ROPE_MLA
IN_PAD = 3 * AW + Q_LORA + KV_LORA + HD
QBLK = 128
DSWA_DILATIONS = (1, 4, 16)
ROPE_THETA = 500000.0
NORM_EPS = 1e-6
NEG_INF = -1e30
SCALE_A = HD ** -0.5
SCALE_B = (HD + ROPE_MLA) ** -0.5

ADAM_LR = 0.001
ADAM_B1 = 0.9
ADAM_B2 = 0.999
ADAM_EPS = 1e-08
ADAM_WD = 0.01
ADAM_STEP = 10

VMEM_LIMIT = 48 * 1024 * 1024

NT = (((1,), (1,)), ((), ()))
NN = (((1,), (0,)), ((), ()))
TN = (((0,), (0,)), ((), ()))


def _dot(a, b, dims):
    return lax.dot_general(a, b, dims, preferred_element_type=F32)


def _params(*sem):
    return pltpu.CompilerParams(dimension_semantics=sem, vmem_limit_bytes=VMEM_LIMIT)


def _tile(n, want):
    t = min(n, want)
    while n % t:
        t //= 2
    return t


def _tile128(n, want):
    if n % 128:
        return n
    units = n // 128
    return 128 * max(u for u in range(1, max(want // 128, 1) + 1) if units % u == 0)


class _Exchange:
    def __init__(self, arrs, mode, rows=None):
        self.arrs = list(arrs)
        self.mode = mode
        self.rows = rows
        self.n = len(self.arrs)
        self.results = None
        hbm = pl.BlockSpec(memory_space=pltpu.HBM)
        self.specs = [hbm] * self.n
        shape = {"gather": lambda a: (N_DEV,) + a.shape, "scatter": lambda a: a.shape,
                 "pair": lambda a: (4,) + a.shape[1:],
                 "chips": lambda a: a.shape if rows is None else (a.shape[0], rows[1]) + a.shape[2:]}[mode]
        self.out_shape = [jax.ShapeDtypeStruct(shape(a), a.dtype) for a in self.arrs]
        n_sem = self.n * (N_DEV - 1)
        self.scratch = [pltpu.SemaphoreType.DMA((n_sem,)), pltpu.SemaphoreType.DMA((n_sem,)),
                        pltpu.SemaphoreType.DMA((self.n,))]

    def hooks(self, ins, outs, send_sems, recv_sems, local_sems):
        x, y, c = lax.axis_index("x"), lax.axis_index("y"), lax.axis_index("c")
        me = (x, y, c)
        sib = (x, y, 1 - c)
        chips = [(1 - x, y), (x, 1 - y), (1 - x, 1 - y)]
        slot = lambda p: 4 * p[0] + 2 * p[1] + p[2]
        chip_of = lambda p: 2 * p[0] + p[1]

        def rcopy(a, k, src, dst, to):
            i = a * (N_DEV - 1) + k
            return pltpu.make_async_remote_copy(src_ref=src, dst_ref=dst, send_sem=send_sems.at[i],
                                                recv_sem=recv_sems.at[i], device_id=to,
                                                device_id_type=pl.DeviceIdType.MESH)

        def window(a, chip_slot):
            if self.rows is None:
                return ins[a].at[chip_slot]
            return ins[a].at[chip_slot, pl.ds(self.rows[0], self.rows[1])]

        def local(a):
            if self.mode == "chips":
                return pltpu.make_async_copy(window(a, chip_of(me)), outs[a].at[chip_of(me)], local_sems.at[a])
            src = ins[a].at[slot(me)] if self.mode == "scatter" else ins[a]
            return pltpu.make_async_copy(src, outs[a].at[slot(me)], local_sems.at[a])

        def peer(rel):
            return (1 - x if rel & 4 else x, 1 - y if rel & 2 else y, 1 - c if rel & 1 else c)

        if self.mode == "pair":
            def start():
                for a in range(self.n):
                    for p in range(4):
                        rcopy(a, p, ins[a].at[2 * p + 1 - c], outs[a].at[p], sib).start()

            def middle():
                pass

            def finish():
                for a in range(self.n):
                    for p in range(4):
                        cp = rcopy(a, p, ins[a].at[2 * p + 1 - c], outs[a].at[p], sib)
                        cp.wait_send()
                        cp.wait_recv()
        elif self.mode == "chips":
            def start():
                for a in range(self.n):
                    local(a).start()
                    for j, chip in enumerate(chips):
                        rcopy(a, j, window(a, chip_of(chip)), outs[a].at[chip_of(me)], (*chip, c)).start()

            def middle():
                pass

            def finish():
                for a in range(self.n):
                    for j, chip in enumerate(chips):
                        cp = rcopy(a, j, window(a, chip_of(chip)), outs[a].at[chip_of(chip)], (*chip, c))
                        cp.wait_send()
                        cp.wait_recv()
                    local(a).wait()
        elif self.mode == "scatter":
            def start():
                for a in range(self.n):
                    local(a).start()
                    for rel in range(1, N_DEV):
                        rcopy(a, rel - 1, ins[a].at[slot(peer(rel))], outs[a].at[slot(me)], peer(rel)).start()

            def middle():
                pass

            def finish():
                for a in range(self.n):
                    for rel in range(1, N_DEV):
                        cp = rcopy(a, rel - 1, ins[a].at[slot(peer(rel))], outs[a].at[slot(peer(rel))], peer(rel))
                        cp.wait_send()
                        cp.wait_recv()
                    local(a).wait()
        else:
            def start():
                for a in range(self.n):
                    local(a).start()
                    rcopy(a, 0, ins[a], outs[a].at[slot(me)], sib).start()
                    for j, chip in enumerate(chips):
                        rcopy(a, 1 + j, ins[a], outs[a].at[slot(me)], (*chip, c)).start()

            def middle():
                for a in range(self.n):
                    for j, chip in enumerate(chips):
                        landed = outs[a].at[slot((*chip, c))]
                        rcopy(a, 1 + j, ins[a], landed, me).wait_recv()
                        rcopy(a, 4 + j, landed, landed, sib).start()

            def finish():
                for a in range(self.n):
                    rcopy(a, 0, ins[a], outs[a].at[slot(sib)], me).wait_recv()
                    for j, chip in enumerate(chips):
                        rcopy(a, 4 + j, ins[a], outs[a].at[slot((*chip, 1 - c))], me).wait_recv()
                    for k in range(N_DEV - 1):
                        rcopy(a, k, ins[a], outs[a].at[slot(me)], me).wait_send()
                    local(a).wait()

        return start, middle, finish

    def set_results(self, res):
        self.results = list(res)

    def standalone(self, name):
        n = self.n

        def body(*refs):
            start, middle, finish = self.hooks(refs[:n], refs[n:2 * n], *refs[2 * n:])
            start()
            middle()
            finish()

        self.results = pl.pallas_call(
            body, name=name, in_specs=self.specs, out_specs=self.specs, out_shape=self.out_shape,
            scratch_shapes=self.scratch, compiler_params=pltpu.CompilerParams(has_side_effects=True),
        )(*self.arrs)
        return self.results


class _Carried:
    def __init__(self, parts):
        self.parts = list(parts)
        self.n = sum(p.n for p in self.parts)
        self.arrs = [a for p in self.parts for a in p.arrs]
        self.specs = [s for p in self.parts for s in p.specs]
        self.out_shape = [s for p in self.parts for s in p.out_shape]
        self.scratch = [s for p in self.parts for s in p.scratch]

    def hooks(self, ins, outs, *sems):
        hooks, i = [], 0
        for j, p in enumerate(self.parts):
            hooks.append(p.hooks(ins[i:i + p.n], outs[i:i + p.n], *sems[3 * j:3 * j + 3]))
            i += p.n
        def phase(k):
            def run():
                for h in hooks:
                    h[k]()
            return run

        return phase(0), phase(1), phase(2)

    def set_results(self, res):
        i = 0
        for p in self.parts:
            p.set_results(res[i:i + p.n])
            i += p.n


def _call(body, name, grid, in_specs, out_specs, out_shape, args, scratch=(), sem=(), comm=None, prefetch=(),
          aliases=None):
    npf = len(prefetch)
    if isinstance(comm, (list, tuple)):
        comm = _Carried(comm)
    if comm is None:
        spec = pltpu.PrefetchScalarGridSpec(num_scalar_prefetch=npf, grid=grid, in_specs=list(in_specs),
                                            out_specs=list(out_specs), scratch_shapes=list(scratch))
        return pl.pallas_call(body, name=name, grid_spec=spec, out_shape=list(out_shape),
                              input_output_aliases=aliases or {},
                              compiler_params=_params(*sem))(*prefetch, *args)
    assert aliases is None
    ni, no, ns, n = len(in_specs), len(out_specs), len(scratch), comm.n
    steps = math.prod(grid)

    def wrapped(*refs):
        pf, refs = refs[:npf], refs[npf:]
        ins, c_ins = refs[:ni], refs[ni:ni + n]
        outs, c_outs = refs[ni + n:ni + n + no], refs[ni + n + no:ni + 2 * n + no]
        scr, c_scr = refs[ni + 2 * n + no:ni + 2 * n + no + ns], refs[ni + 2 * n + no + ns:]
        start, middle, finish = comm.hooks(c_ins, c_outs, *c_scr)
        step = pl.program_id(0)
        for ax in range(1, len(grid)):
            step = step * grid[ax] + pl.program_id(ax)
        pl.when(step == 0)(start)
        pl.when(step == steps // 2)(middle)
        body(*pf, *ins, *outs, *scr)
        pl.when(step == steps - 1)(finish)

    spec = pltpu.PrefetchScalarGridSpec(num_scalar_prefetch=npf, grid=grid, in_specs=list(in_specs) + comm.specs,
                                        out_specs=list(out_specs) + comm.specs,
                                        scratch_shapes=list(scratch) + comm.scratch)
    res = pl.pallas_call(
        wrapped, name=name, grid_spec=spec, out_shape=list(out_shape) + comm.out_shape,
        compiler_params=pltpu.CompilerParams(dimension_semantics=("arbitrary",) * len(grid),
                                             vmem_limit_bytes=VMEM_LIMIT, has_side_effects=True),
    )(*prefetch, *args, *comm.arrs)
    comm.set_results(res[no:])
    return res[:no]


def _mm(a, b, mode, out_dtypes, name, epilogue=None, extras=(), tm=1024, tn=1024, tk=2048, comm=None,
        b_shards=False, out_shards=False):
    if mode == "tn":
        K, M = a.shape
    else:
        M, K = a.shape
    if b_shards:
        N = b.shape[1] if mode == "nt" else N_DEV * b.shape[2]
    else:
        N = b.shape[0] if mode == "nt" else b.shape[1]
    tm, tn, tk = _tile128(M, tm), _tile128(N, tn), _tile128(K, tk)
    pair_k = b_shards and mode == "nt"
    if pair_k:
        tk = 2 * K // N_DEV
        b = b.reshape(N_DEV // 2, 2, *b.shape[1:])
    elif b_shards or out_shards:
        tn = N // N_DEV
    nk = K // tk
    dims = {"nn": NN, "nt": NT, "tn": TN}[mode]
    a_spec = (pl.BlockSpec((tk, tm), lambda i, j, k: (k, i)) if mode == "tn"
              else pl.BlockSpec((tm, tk), lambda i, j, k: (i, k)))
    if b_shards:
        b_spec = (pl.BlockSpec((None, 2, tn, tk // 2), lambda i, j, k: (k, 0, j, 0)) if mode == "nt"
                  else pl.BlockSpec((None, tk, tn), lambda i, j, k: (j, k, 0)))
    else:
        b_spec = (pl.BlockSpec((tn, tk), lambda i, j, k: (j, k)) if mode == "nt"
                  else pl.BlockSpec((tk, tn), lambda i, j, k: (k, j)))
    mn_spec = pl.BlockSpec((tm, tn), lambda i, j, k: (i, j))
    out_spec = pl.BlockSpec((None, tm, tn), lambda i, j, k: (j, i, 0)) if out_shards else mn_spec
    out_dims = (N_DEV, M, N // N_DEV) if out_shards else (M, N)
    n_ex = len(extras)
    n_out = len(out_dtypes)

    def finish(acc, ex, outs):
        res = (acc,) if epilogue is None else epilogue(acc, *[e[...] for e in ex])
        for o, r in zip(outs, res):
            o[...] = r.astype(o.dtype)

    def product(a_ref, b_ref):
        if pair_k:
            return _dot(a_ref[:, :tk // 2], b_ref[0], NT) + _dot(a_ref[:, tk // 2:], b_ref[1], NT)
        return _dot(a_ref[...], b_ref[...], dims)

    def body(*refs):
        a_ref, b_ref = refs[:2]
        ex = refs[2:2 + n_ex]
        outs = refs[2 + n_ex:2 + n_ex + n_out]
        if nk == 1:
            finish(product(a_ref, b_ref), ex, outs)
            return
        acc = refs[-1]
        k = pl.program_id(2)

        @pl.when(k == 0)
        def _():
            acc[...] = product(a_ref, b_ref)

        @pl.when(jnp.logical_and(k > 0, k < nk - 1))
        def _():
            acc[...] += product(a_ref, b_ref)

        @pl.when(k == nk - 1)
        def _():
            finish(acc[...] + product(a_ref, b_ref), ex, outs)

    out = _call(
        body, name, (M // tm, N // tn, nk), [a_spec, b_spec] + [mn_spec] * n_ex, [out_spec] * n_out,
        [jax.ShapeDtypeStruct(out_dims, dt) for dt in out_dtypes], (a, b, *extras),
        scratch=[] if nk == 1 else [pltpu.VMEM((tm, tn), F32)], sem=("parallel", "parallel", "arbitrary"),
        comm=comm)
    return out[0] if n_out == 1 else out


def _rms_fwd(x, gain, out_dtype, name, width=None, col_block=0, residual=None, tb=256, comm=None):
    T = x.shape[0]
    W = x.shape[1] if width is None else width
    tb = _tile(T, tb)
    has_res = residual is not None

    def body(*refs):
        x_ref, g_ref = refs[:2]
        o_ref = refs[-1]
        xf = x_ref[...]
        y = xf * lax.rsqrt(jnp.mean(xf * xf, axis=-1, keepdims=True) + NORM_EPS) * g_ref[...]
        if has_res:
            y = refs[2][...] + y
        o_ref[...] = y.astype(o_ref.dtype)

    row = pl.BlockSpec((tb, W), lambda i: (i, 0))
    ins = [x, gain] + ([residual] if has_res else [])
    return _call(
        body, name, (T // tb,),
        [pl.BlockSpec((tb, W), lambda i: (i, col_block)),
         pl.BlockSpec((1, W), lambda i: (0, 0))] + ([row] if has_res else []),
        [row], [jax.ShapeDtypeStruct((T, W), out_dtype)], ins, sem=("parallel",), comm=comm)[0]


def _rms_bwd(dy, x, gain, out_dtype, name, width=None, col_block=0, residual=None, tb=256, comm=None, into=None):
    T = dy.shape[0]
    W = x.shape[1] if width is None else width
    tb = _tile(T, tb)
    has_res = residual is not None

    def body(*refs):
        dy_ref, x_ref, g_ref = refs[:3]
        dx_ref, dg_ref = refs[-2:]
        i = pl.program_id(0)
        xf = x_ref[...]
        r = lax.rsqrt(jnp.mean(xf * xf, axis=-1, keepdims=True) + NORM_EPS)
        xn = xf * r
        dyf = dy_ref[...].astype(F32)
        dyg = dyf * g_ref[...]
        dx = r * (dyg - xn * jnp.mean(dyg * xn, axis=-1, keepdims=True))
        if has_res:
            dx = refs[3][...] + dx
        dx_ref[...] = dx.astype(dx_ref.dtype)

        @pl.when(i == 0)
        def _():
            dg_ref[...] = jnp.zeros_like(dg_ref)

        dg_ref[...] += jnp.sum(dyf * xn, axis=0, keepdims=True)

    row = pl.BlockSpec((tb, W), lambda i: (i, 0))
    vec = pl.BlockSpec((1, W), lambda i: (0, 0))
    ins = [dy, x, gain] + ([residual] if has_res else [])
    in_specs = [row, pl.BlockSpec((tb, W), lambda i: (i, col_block)), vec] + ([row] if has_res else [])
    if into is None:
        out_spec, out_struct, aliases = row, jax.ShapeDtypeStruct((T, W), out_dtype), None
    else:
        buf, buf_block = into
        out_spec = pl.BlockSpec((tb, W), lambda i: (i, buf_block))
        out_struct, aliases = jax.ShapeDtypeStruct(buf.shape, buf.dtype), {len(ins): 0}
        ins, in_specs = ins + [buf], in_specs + [pl.BlockSpec(memory_space=pl.ANY)]
    return _call(
        body, name, (T // tb,), in_specs, [out_spec, vec], [out_struct, jax.ShapeDtypeStruct((1, W), F32)], ins,
        sem=("arbitrary",), comm=comm, aliases=aliases)


def _rms(xf):
    r = lax.rsqrt(jnp.mean(xf * xf, axis=-1, keepdims=True) + NORM_EPS)
    return r, xf * r


def _rms_grad(dyf, xn, r, gain):
    dyg = dyf * gain
    return r * (dyg - xn * jnp.mean(dyg * xn, axis=-1, keepdims=True)), dyf * xn


def _accumulate_rows(i, ref, rows):
    @pl.when(i == 0)
    def _():
        ref[...] = jnp.zeros_like(ref)

    ref[...] += jnp.sum(rows, axis=0, keepdims=True)


def _norm_pair_fwd(y1, xs, gain_post, gain_pre, name, tb=256):
    T, D = xs.shape
    tb = _tile(T, tb)

    def body(y1_ref, xs_ref, gp_ref, gq_ref, x2_ref, h2_ref):
        x2 = xs_ref[...] + _rms(y1_ref[...])[1] * gp_ref[...]
        x2_ref[...] = x2
        h2_ref[...] = (_rms(x2)[1] * gq_ref[...]).astype(BF16)

    row = pl.BlockSpec((tb, D), lambda i: (i, 0))
    vec = pl.BlockSpec((1, D), lambda i: (0, 0))
    return pl.pallas_call(
        body, name=name, grid=(T // tb,), in_specs=[row, row, vec, vec], out_specs=[row, row],
        out_shape=[jax.ShapeDtypeStruct((T, D), F32), jax.ShapeDtypeStruct((T, D), BF16)],
        compiler_params=_params("parallel"),
    )(y1, xs, gain_post, gain_pre)


def _norm_pair_bwd(dh2, x2, gain_pre, dx3, y1, gain_post, name, tb=256):
    T, D = x2.shape
    tb = _tile(T, tb)

    def body(dh2_ref, x2_ref, gq_ref, dx3_ref, y1_ref, gp_ref, dx2_ref, dy1_ref, dgq_ref, dgp_ref):
        i = pl.program_id(0)
        r2, xn2 = _rms(x2_ref[...])
        d2, rows_q = _rms_grad(dh2_ref[...], xn2, r2, gq_ref[...])
        dx2 = dx3_ref[...] + d2
        dx2_ref[...] = dx2
        r1, yn1 = _rms(y1_ref[...])
        d1, rows_p = _rms_grad(dx2, yn1, r1, gp_ref[...])
        dy1_ref[...] = d1.astype(BF16)
        _accumulate_rows(i, dgq_ref, rows_q)
        _accumulate_rows(i, dgp_ref, rows_p)

    row = pl.BlockSpec((tb, D), lambda i: (i, 0))
    vec = pl.BlockSpec((1, D), lambda i: (0, 0))
    return pl.pallas_call(
        body, name=name, grid=(T // tb,), in_specs=[row, row, vec, row, row, vec], out_specs=[row, row, vec, vec],
        out_shape=[jax.ShapeDtypeStruct((T, D), F32), jax.ShapeDtypeStruct((T, D), BF16),
                   jax.ShapeDtypeStruct((1, D), F32), jax.ShapeDtypeStruct((1, D), F32)],
        compiler_params=_params("arbitrary"),
    )(dh2, x2, gain_pre, dx3, y1, gain_post)


def _loss_head(x2, y2, gain, target, name, tb=256):
    T, D = x2.shape
    tb = _tile(T, tb)

    def body(x2_ref, y2_ref, g_ref, t_ref, dx3_ref, dy2_ref, loss_ref, dg_ref):
        i = pl.program_id(0)
        r, yn = _rms(y2_ref[...])
        e = x2_ref[...] + yn * g_ref[...] - t_ref[...]
        dx3 = e * (1.0 / D)
        dx3_ref[...] = dx3
        dy2, rows = _rms_grad(dx3, yn, r, g_ref[...])
        dy2_ref[...] = dy2.astype(BF16)
        _accumulate_rows(i, dg_ref, rows)
        _accumulate_rows(i, loss_ref, 0.5 * jnp.mean(e * e, axis=-1, keepdims=True))

    row = pl.BlockSpec((tb, D), lambda i: (i, 0))
    vec = pl.BlockSpec((1, D), lambda i: (0, 0))
    return pl.pallas_call(
        body, name=name, grid=(T // tb,),
        in_specs=[row, row, vec, row],
        out_specs=[row, row, pl.BlockSpec((1, 1), lambda i: (0, 0)), vec],
        out_shape=[jax.ShapeDtypeStruct((T, D), F32), jax.ShapeDtypeStruct((T, D), BF16),
                   jax.ShapeDtypeStruct((1, 1), F32), jax.ShapeDtypeStruct((1, D), F32)],
        compiler_params=_params("arbitrary"),
    )(x2, y2, gain, target)


def _rope_tables(positions, rot_dim):
    half = rot_dim // 2
    inv_freq = ROPE_THETA ** (-jnp.arange(0, rot_dim, 2, dtype=F32) / rot_dim)
    ang = positions.astype(F32)[:, None] * inv_freq[None, :]
    cos, sin = jnp.cos(ang), jnp.sin(ang)
    T = positions.shape[0]
    ones = jnp.ones((T, HD - rot_dim), F32)
    cos_t = jnp.concatenate([cos, cos, ones], axis=1)
    sin_t = jnp.concatenate([-sin, sin, jnp.zeros_like(ones)], axis=1)
    return cos_t, sin_t


def _rotate(x, cos_t, sin_t, half):
    lane = lax.broadcasted_iota(jnp.int32, x.shape, 1)
    swapped = jnp.where(lane < half, pltpu.roll(x, HD - half, 1), pltpu.roll(x, half, 1))
    return x * cos_t + swapped * sin_t


def _rope_apply(x, cos_t, sin_t, half, n_blocks, is_rope, out_dtype, name, window=0, tb=256):
    T = x.shape[0]
    tb = _tile(T, tb)
    W = n_blocks * HD

    def body(x_ref, c_ref, s_ref, o_ref):
        for j in range(n_blocks):
            sl = slice(j * HD, (j + 1) * HD)
            xj = x_ref[:, sl]
            if is_rope(j):
                xj = _rotate(xj.astype(F32), c_ref[...], s_ref[...], half)
            o_ref[:, sl] = xj.astype(o_ref.dtype)

    tab = pl.BlockSpec((tb, HD), lambda i: (i, 0))
    return pl.pallas_call(
        body, name=name, grid=(T // tb,),
        in_specs=[pl.BlockSpec((tb, W), lambda i: (i, window)), tab, tab],
        out_specs=pl.BlockSpec((tb, W), lambda i: (i, 0)),
        out_shape=jax.ShapeDtypeStruct((T, W), out_dtype),
        compiler_params=_params("parallel"),
    )(x, cos_t, sin_t)


DSWA_TB = 2048


def _deinterleave(src, dst_ref, d, dtype):
    rows = src.shape[0] // d
    for r in range(d):
        dst_ref[r] = src[pl.ds(r, rows, stride=d), :].astype(dtype)


def _rope_dswa(proj, cos_t, sin_t, name, comm=None):
    T = proj.shape[0]
    tb = _tile(T, 2 * DSWA_TB)
    half = ROT_A // 2

    def body(x_ref, c_ref, s_ref, *rest):
        outs, scr = rest[:-1], rest[-1]
        j = pl.program_id(1)

        @pl.when(j < 2 * HEADS)
        def _():
            scr[...] = _rotate(x_ref[...], c_ref[...], s_ref[...], half)

        @pl.when(j >= 2 * HEADS)
        def _():
            scr[...] = x_ref[...]

        for o_ref, d in zip(outs, DSWA_DILATIONS):
            _deinterleave(scr, o_ref, d, BF16)

    blk = pl.BlockSpec((tb, HD), lambda i, j: (i, j))
    tab = pl.BlockSpec((tb, HD), lambda i, j: (i, 0))
    return _call(
        body, name, (T // tb, 3 * HEADS), [blk, tab, tab],
        [pl.BlockSpec((d, tb // d, HD), lambda i, j: (0, i, j)) for d in DSWA_DILATIONS],
        [jax.ShapeDtypeStruct((d, T // d, 3 * AW), BF16) for d in DSWA_DILATIONS], (proj, cos_t, sin_t),
        scratch=[pltpu.VMEM((tb, HD), F32)], sem=("parallel", "parallel"), comm=comm)


def _shared_key_grad(dk, cos_t, sin_t_neg, half, name, into, tb=512):
    T = dk.shape[0]
    tb = _tile(T, tb)
    buf, buf_block = into

    def body(d_ref, c_ref, s_ref, buf_ref, o_ref):
        tot = d_ref[:, HD:2 * HD]
        for h in range(1, HEADS):
            tot = tot + d_ref[:, h * QK + HD:(h + 1) * QK]
        o_ref[...] = _rotate(tot, c_ref[...], s_ref[...], half).astype(o_ref.dtype)

    tab = pl.BlockSpec((tb, HD), lambda i: (i, 0))
    return pl.pallas_call(
        body, name=name, grid=(T // tb,),
        in_specs=[pl.BlockSpec((tb, HEADS * QK), lambda i: (i, 0)), tab, tab, pl.BlockSpec(memory_space=pl.ANY)],
        out_specs=pl.BlockSpec((tb, HD), lambda i: (i, buf_block)),
        out_shape=jax.ShapeDtypeStruct(buf.shape, buf.dtype), input_output_aliases={3: 0},
        compiler_params=_params("parallel"),
    )(dk, cos_t, sin_t_neg, buf)


def _band_mask(n):
    row = lax.broadcasted_iota(jnp.int32, (QBLK, 2 * QBLK), 0)
    col = lax.broadcasted_iota(jnp.int32, (QBLK, 2 * QBLK), 1)
    in_prev = jnp.logical_and(jnp.logical_and(col < QBLK, col >= row), n > 0)
    in_cur = jnp.logical_and(col >= QBLK, col - QBLK <= row)
    return jnp.logical_or(in_prev, in_cur)


def _dswa_specs(nb, reverse=False):
    pos = (lambda n: nb - 1 - n) if reverse else (lambda n: n)
    cur = lambda c: pl.BlockSpec((None, QBLK, AW), lambda r, n: (r, pos(n), c))
    prev = lambda c: pl.BlockSpec((None, QBLK, AW), lambda r, n: (r, jnp.maximum(pos(n) - 1, 0), c))
    stat = pl.BlockSpec((None, QBLK, HD), lambda r, n: (r, pos(n), 0))
    return cur, prev, stat


def _relayout_spec(d, tb, per_head=True):
    if per_head:
        return pl.BlockSpec((d, tb // d, HD), lambda i, h: (0, i, h))
    return pl.BlockSpec((d, tb // d, HD), lambda i, h: (0, i, 0))


def _head_lane(x, h):
    lane = lax.broadcasted_iota(jnp.int32, x.shape, 1)
    return jnp.sum(jnp.where(lane == h, x, 0.0), axis=-1, keepdims=True)


def _dswa_fwd(qkv, name, comm=None):
    d, sd = qkv.shape[:2]
    nb = sd // QBLK

    def body(q_ref, kc_ref, kp_ref, vc_ref, vp_ref, o_ref, l_ref):
        mask = _band_mask(pl.program_id(1))
        l_ref[...] = jnp.zeros_like(l_ref)
        for h in range(HEADS):
            sl = slice(h * HD, (h + 1) * HD)
            keys = jnp.concatenate([kp_ref[:, sl], kc_ref[:, sl]], axis=0)
            vals = jnp.concatenate([vp_ref[:, sl], vc_ref[:, sl]], axis=0)
            s = jnp.where(mask, _dot(q_ref[:, sl], keys, NT) * SCALE_A, NEG_INF)
            m = jnp.max(s, axis=-1, keepdims=True)
            p = jnp.exp(s - m)
            den = jnp.sum(p, axis=-1, keepdims=True)
            o_ref[:, sl] = _dot((p / den).astype(BF16), vals, NN)
            l_ref[:, h:h + 1] = m + jnp.log(den)

    cur, prev, stat = _dswa_specs(nb)
    return _call(
        body, name, (d, nb), [cur(0), cur(1), prev(1), cur(2), prev(2)], [cur(0), stat],
        [jax.ShapeDtypeStruct((d, sd, AW), F32), jax.ShapeDtypeStruct((d, sd, HD), F32)],
        (qkv, qkv, qkv, qkv, qkv), sem=("parallel", "parallel"), comm=comm)


def _dswa_merge(outs, lses, name, comm=None):
    nc = len(DSWA_DILATIONS)
    T = outs[0].shape[0] * outs[0].shape[1]
    tb = _tile(T, DSWA_TB)

    def body(*refs):
        o_refs, l_refs = refs[:nc], refs[nc:2 * nc]
        out_ref, outb_ref = refs[2 * nc:2 * nc + 2]
        lt_refs = refs[2 * nc + 2:3 * nc + 2]
        o_nat, l_nat, lt_nat = refs[3 * nc + 2:4 * nc + 2], refs[4 * nc + 2:5 * nc + 2], refs[-1]
        h = pl.program_id(1)
        for c, d in enumerate(DSWA_DILATIONS):
            for r in range(d):
                o_nat[c][pl.ds(r, tb // d, stride=d), :] = o_refs[c][r]
                l_nat[c][pl.ds(r, tb // d, stride=d), :] = l_refs[c][r]
        ls = [l[...] for l in l_nat]
        m = functools.reduce(jnp.maximum, ls)
        es = [jnp.exp(l - m) for l in ls]
        tot = functools.reduce(lambda a, b: a + b, es)
        acc = _head_lane(es[0] / tot, h) * o_nat[0][...]
        for c in range(1, nc):
            acc = acc + _head_lane(es[c] / tot, h) * o_nat[c][...]
        out_ref[...] = acc
        outb_ref[...] = acc.astype(BF16)
        lt_nat[...] = m + jnp.log(tot)
        for c, d in enumerate(DSWA_DILATIONS):
            _deinterleave(lt_nat, lt_refs[c], d, F32)

    nat = pl.BlockSpec((tb, HD), lambda i, h: (i, h))
    by_d = [_relayout_spec(d, tb) for d in DSWA_DILATIONS]
    stat_by_d = [_relayout_spec(d, tb, per_head=False) for d in DSWA_DILATIONS]
    res = _call(
        body, name, (T // tb, HEADS), by_d + stat_by_d, [nat, nat] + stat_by_d,
        [jax.ShapeDtypeStruct((T, AW), F32), jax.ShapeDtypeStruct((T, AW), BF16)]
        + [jax.ShapeDtypeStruct((d, T // d, HD), F32) for d in DSWA_DILATIONS], (*outs, *lses),
        scratch=[pltpu.VMEM((tb, HD), F32)] * (2 * nc + 1), sem=("parallel", "arbitrary"), comm=comm)
    return res[0], res[1], res[2:]


def _dswa_delta(dout, out, name):
    nc = len(DSWA_DILATIONS)
    T = out.shape[0]
    tb = _tile(T, DSWA_TB)

    def body(do_ref, o_ref, *rest):
        dl_refs, dob_refs, dl_nat = rest[:nc], rest[nc:2 * nc], rest[-1]
        h = pl.program_id(1)
        lane = lax.broadcasted_iota(jnp.int32, (tb, HD), 1)
        mine = jnp.where(lane == h, jnp.sum(do_ref[...] * o_ref[...], axis=-1, keepdims=True), 0.0)

        @pl.when(h == 0)
        def _():
            dl_nat[...] = mine

        @pl.when(h > 0)
        def _():
            dl_nat[...] += mine

        for c, d in enumerate(DSWA_DILATIONS):
            _deinterleave(dl_nat, dl_refs[c], d, F32)
            _deinterleave(do_ref, dob_refs[c], d, BF16)

    nat = pl.BlockSpec((tb, HD), lambda i, h: (i, h))
    by_d = [_relayout_spec(d, tb) for d in DSWA_DILATIONS]
    stat_by_d = [_relayout_spec(d, tb, per_head=False) for d in DSWA_DILATIONS]
    res = pl.pallas_call(
        body, name=name, grid=(T // tb, HEADS),
        in_specs=[nat, nat], out_specs=stat_by_d + by_d,
        out_shape=[jax.ShapeDtypeStruct((d, T // d, HD), F32) for d in DSWA_DILATIONS]
        + [jax.ShapeDtypeStruct((d, T // d, AW), BF16) for d in DSWA_DILATIONS],
        scratch_shapes=[pltpu.VMEM((tb, HD), F32)],
        compiler_params=_params("parallel", "arbitrary"),
    )(dout, out)
    return res[:nc], res[nc:]


def _delta_prep(dout, col_block, out, name, tb=256):
    T = out.shape[0]
    tb = _tile(T, tb)

    def body(do_ref, o_ref, delta_ref, dob_ref):
        delta_ref[...] = jnp.zeros_like(delta_ref)
        for h in range(HEADS):
            sl = slice(h * HD, (h + 1) * HD)
            doh = do_ref[:, sl]
            delta_ref[:, h:h + 1] = jnp.sum(doh * o_ref[:, sl], axis=-1, keepdims=True)
            dob_ref[:, sl] = doh.astype(BF16)

    row = pl.BlockSpec((tb, AW), lambda i: (i, 0))
    return pl.pallas_call(
        body, name=name, grid=(T // tb,),
        in_specs=[pl.BlockSpec((tb, AW), lambda i: (i, col_block)), row],
        out_specs=[pl.BlockSpec((tb, HD), lambda i: (i, 0)), row],
        out_shape=[jax.ShapeDtypeStruct((T, HD), F32), jax.ShapeDtypeStruct((T, AW), BF16)],
        compiler_params=_params("parallel"),
    )(dout, out)


def _dswa_bwd(qkv, dout_b, lse_tot, delta, name, comm=None):
    d, sd = qkv.shape[:2]
    nb = sd // QBLK

    def body(q_ref, kc_ref, kp_ref, vc_ref, vp_ref, do_ref, l_ref, dl_ref, g_ref, carry_k, carry_v):
        mask = _band_mask(nb - 1 - pl.program_id(1))

        @pl.when(pl.program_id(1) == 0)
        def _():
            carry_k[...] = jnp.zeros_like(carry_k)
            carry_v[...] = jnp.zeros_like(carry_v)

        for h in range(HEADS):
            sl = slice(h * HD, (h + 1) * HD)
            qh, doh = q_ref[:, sl], do_ref[:, sl]
            keys = jnp.concatenate([kp_ref[:, sl], kc_ref[:, sl]], axis=0)
            vals = jnp.concatenate([vp_ref[:, sl], vc_ref[:, sl]], axis=0)
            s = jnp.where(mask, _dot(qh, keys, NT) * SCALE_A, NEG_INF)
            p = jnp.exp(s - l_ref[:, h:h + 1])
            ds = (p * (_dot(doh, vals, NT) - dl_ref[:, h:h + 1]) * SCALE_A).astype(BF16)
            g_ref[:, sl] = _dot(ds, keys, NN).astype(BF16)
            dk = _dot(ds, qh, TN)
            dv = _dot(p.astype(BF16), doh, TN)
            g_ref[:, AW + h * HD:AW + (h + 1) * HD] = (dk[QBLK:] + carry_k[:, sl]).astype(BF16)
            g_ref[:, 2 * AW + h * HD:2 * AW + (h + 1) * HD] = (dv[QBLK:] + carry_v[:, sl]).astype(BF16)
            carry_k[:, sl] = dk[:QBLK]
            carry_v[:, sl] = dv[:QBLK]

    cur, prev, stat = _dswa_specs(nb, reverse=True)
    out_spec = pl.BlockSpec((None, QBLK, 3 * AW), lambda r, n: (r, nb - 1 - n, 0))
    return _call(
        body, name, (d, nb), [cur(0), cur(1), prev(1), cur(2), prev(2), cur(0), stat, stat], [out_spec],
        [jax.ShapeDtypeStruct((d, sd, 3 * AW), BF16)], (qkv, qkv, qkv, qkv, qkv, dout_b, lse_tot, delta),
        scratch=[pltpu.VMEM((QBLK, AW), F32)] * 2, sem=("parallel", "arbitrary"), comm=comm)[0]


def _dswa_combine(grads, cos_t, sin_t_neg, name, width):
    T = grads[0].shape[0] * grads[0].shape[1]
    tb = _tile(T, DSWA_TB)
    half = ROT_A // 2

    def body(*refs):
        g_refs = refs[:len(grads)]
        c_ref, s_ref, out_ref, acc = refs[len(grads):]
        j = pl.program_id(1)
        for c, d in enumerate(DSWA_DILATIONS):
            for r in range(d):
                if c == 0:
                    acc[...] = g_refs[c][r].astype(F32)
                else:
                    acc[pl.ds(r, tb // d, stride=d), :] += g_refs[c][r].astype(F32)
        val = acc[...]
        out_ref[...] = jnp.where(j < 2 * HEADS, _rotate(val, c_ref[...], s_ref[...], half), val).astype(BF16)

    tab = pl.BlockSpec((tb, HD), lambda i, j: (i, 0))
    return pl.pallas_call(
        body, name=name, grid=(T // tb, 3 * HEADS),
        in_specs=[pl.BlockSpec((d, tb // d, HD), lambda i, j: (0, i, j)) for d in DSWA_DILATIONS] + [tab, tab],
        out_specs=pl.BlockSpec((tb, HD), lambda i, j: (i, j)),
        out_shape=jax.ShapeDtypeStruct((T, width), BF16),
        scratch_shapes=[pltpu.VMEM((tb, HD), F32)],
        compiler_params=_params("parallel", "parallel"),
    )(*grads, cos_t, sin_t_neg)


MLA_TQ = 512
QK = 2 * HD
LOG2E = 1.4426950408889634


def _triangle(nq, key_major):
    pairs = [(q, k) for q in range(nq) for k in range(q + 1)]
    if key_major:
        pairs.sort(key=lambda p: (p[1], p[0]))
    return (jnp.array([p[0] for p in pairs], jnp.int32), jnp.array([p[1] for p in pairs], jnp.int32))


def _mla_specs(tq):
    q_spec = pl.BlockSpec((tq, HEADS * QK), lambda t, qi, ki: (qi[t], 0))
    k_spec = pl.BlockSpec((tq, HEADS * QK), lambda t, qi, ki: (ki[t], 0))
    v_spec = pl.BlockSpec((tq, AW), lambda t, qi, ki: (ki[t], 1))
    qrow = pl.BlockSpec((tq, AW), lambda t, qi, ki: (qi[t], 0))
    krow = pl.BlockSpec((tq, AW), lambda t, qi, ki: (ki[t], 0))
    return q_spec, k_spec, v_spec, qrow, krow


def _mla_pack(kv, kr, name, tb=512):
    T = kv.shape[0]
    tb = _tile(T, tb)

    def body(kv_ref, kr_ref, k_ref, v1_ref):
        lane = lax.broadcasted_iota(jnp.int32, (tb, HD), 1)
        one_hot = jnp.where(lane == 0, 1.0, 0.0).astype(BF16)
        for h in range(HEADS):
            k_ref[:, h * QK:h * QK + HD] = kv_ref[:, h * HD:(h + 1) * HD]
            k_ref[:, h * QK + HD:(h + 1) * QK] = kr_ref[...]
            v1_ref[:, h * QK:h * QK + HD] = kv_ref[:, AW + h * HD:AW + (h + 1) * HD]
            v1_ref[:, h * QK + HD:(h + 1) * QK] = one_hot

    wide = pl.BlockSpec((tb, HEADS * QK), lambda i: (i, 0))
    return pl.pallas_call(
        body, name=name, grid=(T // tb,),
        in_specs=[pl.BlockSpec((tb, 2 * AW), lambda i: (i, 0)), pl.BlockSpec((tb, HD), lambda i: (i, 0))],
        out_specs=[wide, wide], out_shape=[jax.ShapeDtypeStruct((T, HEADS * QK), BF16)] * 2,
        compiler_params=_params("parallel"),
    )(kv, kr)


def _mla_stat_spec(tq):
    return pl.BlockSpec((tq, HD), lambda t, qi, ki: (qi[t], 0))


def _mla_scores(q_ref, k_ref, h, qi, ki, tq):
    s = _dot(q_ref[:, h * QK:(h + 1) * QK], k_ref[:, h * QK:(h + 1) * QK], NT) * SCALE_B
    row = lax.broadcasted_iota(jnp.int32, s.shape, 0) + qi * tq
    col = lax.broadcasted_iota(jnp.int32, s.shape, 1) + ki * tq
    return jnp.where(col <= row, s, NEG_INF)


def _mla_fwd(q, k, v1, name, comm=None):
    T = q.shape[0]
    tq = _tile(T, MLA_TQ)
    tables = _triangle(T // tq, False)

    def body(qi_ref, ki_ref, q_ref, k_ref, v_ref, o_ref, ob_ref, l_ref, m_s, acc):
        t = pl.program_id(0)
        qi, ki = qi_ref[t], ki_ref[t]

        @pl.when(ki == 0)
        def _():
            m_s[...] = jnp.full_like(m_s, NEG_INF)
            acc[...] = jnp.zeros_like(acc)

        row = lax.broadcasted_iota(jnp.int32, (tq, tq), 0) + qi * tq
        col = lax.broadcasted_iota(jnp.int32, (tq, tq), 1) + ki * tq
        bias = jnp.where(col <= row, 0.0, NEG_INF)
        half = tq // 2
        for h in range(HEADS):
            for lo in (0, half):
                rows = slice(lo, lo + half)
                s = _dot(q_ref[rows, h * QK:(h + 1) * QK], k_ref[:, h * QK:(h + 1) * QK], NT) + bias[rows]
                m_old = m_s[h, rows]
                m_new = jnp.maximum(m_old, jnp.max(s, axis=-1, keepdims=True))
                p = jnp.exp2((s - m_new) * (SCALE_B * LOG2E)).astype(BF16)
                alpha = jnp.exp2((m_old - m_new) * (SCALE_B * LOG2E))
                acc[rows, h * QK:(h + 1) * QK] = (alpha * acc[rows, h * QK:(h + 1) * QK]
                                                  + _dot(p, v_ref[:, h * QK:(h + 1) * QK], NN))
                m_s[h, rows] = m_new

        @pl.when(ki == qi)
        def _():
            l_ref[...] = jnp.zeros_like(l_ref)
            for h in range(HEADS):
                sl = slice(h * HD, (h + 1) * HD)
                den = acc[:, h * QK + HD:h * QK + HD + 1]
                out = acc[:, h * QK:h * QK + HD] / den
                o_ref[:, sl] = out
                ob_ref[:, sl] = out.astype(BF16)
                l_ref[:, h:h + 1] = m_s[h] * SCALE_B + jnp.log(den)

    q_spec, k_spec, _, qrow, _ = _mla_specs(tq)
    return _call(
        body, name, (tables[0].shape[0],), [q_spec, k_spec, k_spec], [qrow, qrow, _mla_stat_spec(tq)],
        [jax.ShapeDtypeStruct((T, AW), F32), jax.ShapeDtypeStruct((T, AW), BF16), jax.ShapeDtypeStruct((T, HD), F32)],
        (q, k, v1),
        scratch=[pltpu.VMEM((HEADS, tq, 1), F32), pltpu.VMEM((tq, HEADS * QK), F32)],
        sem=("arbitrary",), comm=comm, prefetch=tables)


def _mla_ds(q_ref, k_ref, v_ref, do_ref, l_ref, dl_ref, h, qi, ki, tq):
    sl = slice(h * HD, (h + 1) * HD)
    p = jnp.exp(_mla_scores(q_ref, k_ref, h, qi, ki, tq) - l_ref[:, h:h + 1])
    ds = (p * (_dot(do_ref[:, sl], v_ref[:, sl], NT) - dl_ref[:, h:h + 1]) * SCALE_B).astype(BF16)
    return p, ds


def _mla_bwd_q(q, k, kv, dout_b, lse, delta, name, comm=None):
    T = q.shape[0]
    tq = _tile(T, MLA_TQ)
    tables = _triangle(T // tq, False)

    def body(qi_ref, ki_ref, q_ref, k_ref, v_ref, do_ref, l_ref, dl_ref, dq_ref):
        t = pl.program_id(0)
        qi, ki = qi_ref[t], ki_ref[t]

        @pl.when(ki == 0)
        def _():
            dq_ref[...] = jnp.zeros_like(dq_ref)

        for h in range(HEADS):
            _, ds = _mla_ds(q_ref, k_ref, v_ref, do_ref, l_ref, dl_ref, h, qi, ki, tq)
            dq_ref[:, h * QK:(h + 1) * QK] += _dot(ds, k_ref[:, h * QK:(h + 1) * QK], NN)

    q_spec, k_spec, v_spec, qrow, _ = _mla_specs(tq)
    return _call(
        body, name, (tables[0].shape[0],), [q_spec, k_spec, v_spec, qrow, _mla_stat_spec(tq), _mla_stat_spec(tq)], [q_spec],
        [jax.ShapeDtypeStruct((T, HEADS * QK), F32)], (q, k, kv, dout_b, lse, delta),
        sem=("arbitrary",), comm=comm, prefetch=tables)[0]


def _mla_bwd_kv(q, k, kv, dout_b, lse, delta, name, comm=None):
    T = q.shape[0]
    tq = _tile(T, MLA_TQ)
    nq = T // tq
    tables = _triangle(nq, True)

    def body(qi_ref, ki_ref, q_ref, k_ref, v_ref, do_ref, l_ref, dl_ref, dk_ref, dkv_ref, dv_acc):
        t = pl.program_id(0)
        qi, ki = qi_ref[t], ki_ref[t]

        @pl.when(qi == ki)
        def _():
            dk_ref[...] = jnp.zeros_like(dk_ref)
            dv_acc[...] = jnp.zeros_like(dv_acc)

        for h in range(HEADS):
            sl = slice(h * HD, (h + 1) * HD)
            p, ds = _mla_ds(q_ref, k_ref, v_ref, do_ref, l_ref, dl_ref, h, qi, ki, tq)
            dv_acc[:, sl] += _dot(p.astype(BF16), do_ref[:, sl], TN)
            dk_ref[:, h * QK:(h + 1) * QK] += _dot(ds, q_ref[:, h * QK:(h + 1) * QK], TN)

        @pl.when(qi == nq - 1)
        def _():
            for h in range(HEADS):
                dkv_ref[:, h * HD:(h + 1) * HD] = dk_ref[:, h * QK:h * QK + HD].astype(BF16)
                dkv_ref[:, AW + h * HD:AW + (h + 1) * HD] = dv_acc[:, h * HD:(h + 1) * HD].astype(BF16)

    q_spec, k_spec, v_spec, qrow, _ = _mla_specs(tq)
    return _call(
        body, name, (tables[0].shape[0],),
        [q_spec, k_spec, v_spec, qrow, _mla_stat_spec(tq), _mla_stat_spec(tq)],
        [k_spec, pl.BlockSpec((tq, 2 * AW), lambda t, qi, ki: (ki[t], 0))],
        [jax.ShapeDtypeStruct((T, HEADS * QK), F32), jax.ShapeDtypeStruct((T, 2 * AW), BF16)],
        (q, k, kv, dout_b, lse, delta), scratch=[pltpu.VMEM((tq, AW), F32)],
        sem=("arbitrary",), comm=comm, prefetch=tables)


def _pair_sum(by_device, from_sibling, name, tb=256):
    n_chip, R, C = from_sibling.shape
    tb = _tile(R, tb)
    core = jnp.reshape(lax.axis_index("c"), (1,)).astype(jnp.int32)

    def body(core_ref, mine_ref, theirs_ref, o_ref):
        o_ref[...] = (mine_ref[...].astype(F32) + theirs_ref[...].astype(F32)).astype(o_ref.dtype)

    blk = pl.BlockSpec((None, tb, C), lambda p, i, core_ref: (p, i, 0))
    return _call(
        body, name, (n_chip, R // tb),
        [pl.BlockSpec((None, tb, C), lambda p, i, core_ref: (2 * p + core_ref[0], i, 0)), blk], [blk],
        [jax.ShapeDtypeStruct((n_chip, R, C), BF16)], (by_device, from_sibling),
        sem=("parallel", "parallel"), prefetch=(core,))[0]


def _adamw(parts, w, m, v, name, tb=128, comm=None):
    R, C = w.shape
    pieces = list(parts) if isinstance(parts, (list, tuple)) else [parts]
    n_parts = pieces[0].shape[0]
    tb = _tile(R, tb)
    starts = [0]
    for p in pieces:
        assert p.shape[1] % tb == 0
        starts.append(starts[-1] + p.shape[1] // tb)
    c1 = 1.0 - ADAM_B1
    c2 = 1.0 - ADAM_B2
    bc1 = 1.0 - ADAM_B1 ** ADAM_STEP
    bc2 = 1.0 - ADAM_B2 ** ADAM_STEP

    def body(*refs):
        p_refs = refs[:len(pieces)]
        w_ref, m_ref, v_ref, g_ref, d_ref, nm_ref, nv_ref = refs[len(pieces):]
        i = pl.program_id(0)
        g = None
        for k, p_ref in enumerate(p_refs):
            gk = p_ref[0].astype(F32)
            for j in range(1, n_parts):
                gk = gk + p_ref[j].astype(F32)
            g = gk if g is None else jnp.where(i >= starts[k], gk, g)
        nm = ADAM_B1 * m_ref[...] + c1 * g
        nv = ADAM_B2 * v_ref[...] + c2 * (g * g)
        g_ref[...] = g
        nm_ref[...] = nm
        nv_ref[...] = nv
        d_ref[...] = -ADAM_LR * ((nm / bc1) / (jnp.sqrt(nv / bc2) + ADAM_EPS) + ADAM_WD * w_ref[...])

    row = pl.BlockSpec((tb, C), lambda i: (i, 0))
    piece_specs = [pl.BlockSpec((n_parts, tb, C),
                                functools.partial(lambda i, lo, hi: (0, jnp.clip(i - lo, 0, hi - lo - 1), 0),
                                                  lo=starts[k], hi=starts[k + 1]))
                   for k in range(len(pieces))]
    return _call(
        body, name, (R // tb,), piece_specs + [row, row, row], [row] * 4,
        [jax.ShapeDtypeStruct((R, C), F32)] * 4, (*pieces, w, m, v), sem=("parallel",), comm=comm)


def _cols_from_shards(g):
    return jnp.transpose(g, (1, 0, 2)).reshape(g.shape[1], N_DEV * g.shape[2])


def _cols_to_shards(w):
    return jnp.transpose(w.reshape(w.shape[0], N_DEV, w.shape[1] // N_DEV), (1, 0, 2))


def _split_heads(w, first):
    w3 = w.reshape(w.shape[0], HEADS, -1)
    return w3[:, :, :first].reshape(w.shape[0], -1), w3[:, :, first:].reshape(w.shape[0], -1)


def _join_heads(a, b):
    R = a.shape[0]
    return jnp.concatenate([a.reshape(R, HEADS, -1), b.reshape(R, HEADS, -1)], axis=2).reshape(R, -1)


def _join_shards_padded(g, width, name, tb=256):
    n, R, cs = g.shape
    tb = _tile(R, tb)

    def body(g_ref, o_ref):
        for j in range(n):
            o_ref[:, j * cs:(j + 1) * cs] = g_ref[j]
        o_ref[:, n * cs:] = jnp.zeros((tb, width - n * cs), o_ref.dtype)

    return pl.pallas_call(
        body, name=name, grid=(R // tb,),
        in_specs=[pl.BlockSpec((n, tb, cs), lambda i: (0, i, 0))],
        out_specs=pl.BlockSpec((tb, width), lambda i: (i, 0)),
        out_shape=jax.ShapeDtypeStruct((R, width), g.dtype),
        compiler_params=_params("parallel"),
    )(g)


def _split_shards(w, cs, name, tb=256):
    R, width = w.shape
    tb = _tile(R, tb)

    def body(w_ref, o_ref):
        for j in range(N_DEV):
            o_ref[j] = w_ref[:, j * cs:(j + 1) * cs]

    return pl.pallas_call(
        body, name=name, grid=(R // tb,),
        in_specs=[pl.BlockSpec((tb, width), lambda i: (i, 0))],
        out_specs=pl.BlockSpec((N_DEV, tb, cs), lambda i: (0, i, 0)),
        out_shape=jax.ShapeDtypeStruct((N_DEV, R, cs), w.dtype),
        compiler_params=_params("parallel"),
    )(w)


def _pad_heads(w, width):
    w3 = w.reshape(w.shape[0], HEADS, -1)
    return jnp.pad(w3, ((0, 0), (0, 0), (0, width - w3.shape[2]))).reshape(w.shape[0], HEADS * width)


def _unpad_heads(w, k):
    return w.reshape(w.shape[0], HEADS, -1)[:, :, :k].reshape(w.shape[0], HEADS * k)


def kernel(x, positions, norm_attn_pre, norm_attn_post, w_in, q_latent_norm, kv_latent_norm, w_uq, w_ukv, w_out, norm_mlp_pre, norm_mlp_post, w_up, w_down, loss_target, m_norm_attn_pre, m_norm_attn_post, m_w_in, m_q_latent_norm, m_kv_latent_norm, m_w_uq, m_w_ukv, m_w_out, m_norm_mlp_pre, m_norm_mlp_post, m_w_up, m_w_down, v_norm_attn_pre, v_norm_attn_post, v_w_in, v_q_latent_norm, v_kv_latent_norm, v_w_uq, v_w_ukv, v_w_out, v_norm_mlp_pre, v_norm_mlp_post, v_w_up, v_w_down):
    xs = x[0]
    tgt = loss_target[0]
    pos = positions[0]
    T, D = xs.shape
    big = dict(w_in=(w_in, m_w_in, v_w_in), w_uq=(w_uq, m_w_uq, v_w_uq), w_ukv=(w_ukv, m_w_ukv, v_w_ukv),
               w_out=(w_out, m_w_out, v_w_out), w_up=(w_up, m_w_up, v_w_up), w_down=(w_down, m_w_down, v_w_down))
    big = {n: tuple(t[0] for t in ts) for n, ts in big.items()}
    big_names = ["w_in", "w_uq", "w_ukv", "w_out", "w_up", "w_down"]
    col_sharded = {"w_in", "w_uq", "w_ukv", "w_up"}

    wb = {n: big[n][0].astype(BF16) for n in big_names}

    def gathered(ex, i, n):
        g = ex.results[i]
        return _cols_from_shards(g) if n in col_sharded else g.reshape(-1, g.shape[2])

    def by_device(g, n):
        return _cols_to_shards(g) if n in col_sharded else g.reshape(N_DEV, g.shape[0] // N_DEV, g.shape[1])

    def scatter_of(g, n):
        return _Exchange([by_device(g, n)], "scatter")

    cos_a, sin_a = _rope_tables(pos, ROT_A)
    cos_b, sin_b = _rope_tables(pos, ROPE_MLA)

    ex_in = _Exchange([wb["w_in"]], "gather")
    h1 = _rms_fwd(xs, norm_attn_pre, BF16, "norm_attn_pre_fwd", comm=ex_in)
    Wi = _join_shards_padded(ex_in.results[0], IN_PAD, "w_in_join")
    ex_mid = _Exchange([wb["w_uq"], wb["w_ukv"], wb["w_out"]], "gather")
    proj = _mm(h1, Wi, "nn", [F32], "proj_in", tn=1408, comm=ex_mid)
    Wuq = _pad_heads(gathered(ex_mid, 0, "w_uq"), QK)
    Wukv = jnp.concatenate(_split_heads(gathered(ex_mid, 1, "w_ukv"), HD), axis=1)
    Wo = gathered(ex_mid, 2, "w_out")
    n_piece = wb["w_down"].shape[0] // 8
    ex_down = [_Exchange([wb["w_down"][i * n_piece:(i + 1) * n_piece]], "gather") for i in range(8)]
    qkv_by_d = _rope_dswa(proj, cos_a, sin_a, "rope_dswa", comm=ex_down[0])
    outs, lses = [], []
    for d, qkv, ex in zip(DSWA_DILATIONS, qkv_by_d, ex_down[1:4]):
        o, l = _dswa_fwd(qkv, f"dswa_fwd_d{d}", comm=ex)
        outs.append(o)
        lses.append(l)
    a_out, a_out_b, a_lse_by_d = _dswa_merge(outs, lses, "dswa_merge", comm=ex_down[4])

    cqn = _rms_fwd(proj, q_latent_norm, BF16, "q_latent_norm_fwd", width=Q_LORA, col_block=3 * AW // Q_LORA)
    ckvn = _rms_fwd(proj, kv_latent_norm, BF16, "kv_latent_norm_fwd", width=KV_LORA, col_block=3 * AW // KV_LORA + 1)
    qb = _mm(cqn, Wuq, "nn", [F32], "q_up")
    kvb = _mm(ckvn, Wukv, "nn", [BF16], "kv_up")
    odd = lambda j: j % 2 == 1
    q_mla = _rope_apply(qb, cos_b, sin_b, ROPE_MLA // 2, 2 * HEADS, odd, BF16, "rope_mla_q")
    kr = _rope_apply(proj, cos_b, sin_b, ROPE_MLA // 2, 1, lambda j: True, BF16, "rope_mla_k",
                     window=(IN_PAD - HD) // HD)
    k_mla, v1_mla = _mla_pack(kvb, kr, "mla_pack")
    ex_up = _Exchange([wb["w_up"]], "gather")
    b_out, b_out_b, b_lse = _mla_fwd(q_mla, k_mla, v1_mla, "mla_fwd", comm=ex_up)
    Wup_shards = ex_up.results[0]

    mixed = jnp.concatenate([a_out_b, b_out_b], axis=1)
    y1 = _mm(mixed, Wo, "nn", [F32], "attn_out", comm=ex_down[5])

    x2, h2 = _norm_pair_fwd(y1, xs, norm_attn_post, norm_mlp_pre, "norm_attn_post_mlp_pre_fwd")

    def relu2(z):
        r = jnp.maximum(z, 0.0)
        return r * r, r

    u, zr = _mm(h2, Wup_shards, "nn", [BF16, BF16], "mlp_up", epilogue=relu2, comm=ex_down[6:8],
                b_shards=True)
    Wdn = jnp.concatenate([ex.results[0] for ex in ex_down], axis=1).reshape(-1, D)
    y2 = _mm(u, Wdn, "nn", [F32], "mlp_down")
    dx3, dy2, loss_part, dg_mlp_post = _loss_head(x2, y2, norm_mlp_post, tgt, "loss_head")

    dz =_mm(dy2, Wdn, "nt", [BF16], "mlp_down_dx", epilogue=lambda du, r: (du * (2.0 * r.astype(F32)),), extras=(zr,))
    g_down = _mm(u, dy2, "tn", [BF16], "mlp_down_dw")
    down_dev = by_device(g_down, "w_down")
    pair_down = _Exchange([down_dev], "pair")
    up_dev = _mm(h2, dz, "tn", [BF16], "mlp_up_dw", comm=pair_down, out_shards=True)
    down_chip = _pair_sum(down_dev, pair_down.results[0], "pair_sum_w_down")
    pair_up = _Exchange([up_dev], "pair")
    cut = 5 * down_chip.shape[1] // 8
    sc_down = [_Exchange([down_chip], "chips", rows=(0, cut)),
               _Exchange([down_chip], "chips", rows=(cut, down_chip.shape[1] - cut))]
    dh2 = _mm(dz, Wup_shards, "nt", [F32], "mlp_up_dx", comm=[pair_up, sc_down[0]], b_shards=True)
    up_chip = _pair_sum(up_dev, pair_up.results[0], "pair_sum_w_up")

    dx2, dy1, dg_mlp_pre, dg_attn_post = _norm_pair_bwd(dh2, x2, norm_mlp_pre, dx3, y1, norm_attn_post,
                                                        "norm_mlp_pre_attn_post_bwd")
    dmixed = _mm(dy1, Wo, "nt", [F32], "attn_out_dx")
    g_out = _mm(mixed, dy1, "tn", [BF16], "attn_out_dw")

    b_delta, b_dout = _delta_prep(dmixed, 1, b_out, "mla_delta")
    sc_up = _Exchange([up_chip], "chips")
    dq_mla = _mla_bwd_q(q_mla, k_mla, kvb, b_dout, b_lse, b_delta, "mla_bwd_q", comm=sc_up)
    sc_out = scatter_of(g_out, "w_out")
    dk_mla, dkvb = _mla_bwd_kv(q_mla, k_mla, kvb, b_dout, b_lse, b_delta, "mla_bwd_kv", comm=[sc_out, sc_down[1]])
    dqb = _rope_apply(dq_mla, cos_b, -sin_b, ROPE_MLA // 2, 2 * HEADS, odd, BF16, "rope_mla_q_bwd")
    g_uq_pad = _mm(cqn, dqb, "tn", [BF16], "q_up_dw")
    g_ukv_perm = _mm(ckvn, dkvb, "tn", [BF16], "kv_up_dw")
    dcqn = _mm(dqb, Wuq, "nt", [F32], "q_up_dx")
    dckvn = _mm(dkvb, Wukv, "nt", [F32], "kv_up_dx")

    g_uq = _unpad_heads(g_uq_pad, HD + ROPE_MLA)
    g_ukv = _join_heads(g_ukv_perm[:, :AW], g_ukv_perm[:, AW:])
    sc_uq = _Exchange([_cols_to_shards(g_uq), _cols_to_shards(g_ukv)], "scatter")
    a_delta_by_d, a_dout_by_d = _dswa_delta(dmixed, a_out, "dswa_delta")
    a_grads = [_dswa_bwd(qkv_by_d[c], a_dout_by_d[c], a_lse_by_d[c], a_delta_by_d[c], f"dswa_bwd_d{d}")
               for c, d in enumerate(DSWA_DILATIONS)]
    dproj = _dswa_combine(a_grads, cos_a, -sin_a, "dswa_combine", IN_PAD)
    dproj, dg_q = _rms_bwd(dcqn, proj, q_latent_norm, BF16, "q_latent_norm_bwd", width=Q_LORA,
                           col_block=3 * AW // Q_LORA, into=(dproj, 3 * AW // Q_LORA))
    dproj, dg_kv = _rms_bwd(dckvn, proj, kv_latent_norm, BF16, "kv_latent_norm_bwd", width=KV_LORA,
                            col_block=3 * AW // KV_LORA + 1, into=(dproj, 3 * AW // KV_LORA + 1))
    dproj = _shared_key_grad(dk_mla, cos_b, -sin_b, ROPE_MLA // 2, "rope_mla_k_bwd", (dproj, (IN_PAD - HD) // HD))
    g_in_pad = _mm(h1, dproj, "tn", [BF16], "proj_in_dw", tn=1408, comm=sc_uq)
    in_dev = _split_shards(g_in_pad, IN_COLS // N_DEV, "w_in_grad_split")
    pair_in = _Exchange([in_dev], "pair").standalone("pair_w_in")
    sc_in = _Exchange([_pair_sum(in_dev, pair_in[0], "pair_sum_w_in")], "chips")
    dh1 = _mm(dproj, Wi, "nt", [F32], "proj_in_dx", comm=sc_in)
    grad_x, dg_attn_pre = _rms_bwd(dh1, xs, norm_attn_pre, F32, "norm_attn_pre_bwd", residual=dx2)

    parts = dict(w_in=sc_in.results[0], w_uq=sc_uq.results[0], w_ukv=sc_uq.results[1], w_out=sc_out.results[0],
                 w_up=sc_up.results[0], w_down=[sc.results[0] for sc in sc_down])
    big_out = {n: _adamw(parts[n], *big[n], f"adamw_{n}") for n in big_names}

    gain_names = ["norm_attn_pre", "norm_attn_post", "q_latent_norm", "kv_latent_norm", "norm_mlp_pre", "norm_mlp_post"]
    gain_args = dict(norm_attn_pre=(norm_attn_pre, m_norm_attn_pre, v_norm_attn_pre),
                     norm_attn_post=(norm_attn_post, m_norm_attn_post, v_norm_attn_post),
                     q_latent_norm=(q_latent_norm, m_q_latent_norm, v_q_latent_norm),
                     kv_latent_norm=(kv_latent_norm, m_kv_latent_norm, v_kv_latent_norm),
                     norm_mlp_pre=(norm_mlp_pre, m_norm_mlp_pre, v_norm_mlp_pre),
                     norm_mlp_post=(norm_mlp_post, m_norm_mlp_post, v_norm_mlp_post))
    gain_grads = dict(norm_attn_pre=dg_attn_pre, norm_attn_post=dg_attn_post, q_latent_norm=dg_q,
                      kv_latent_norm=dg_kv, norm_mlp_pre=dg_mlp_pre, norm_mlp_post=dg_mlp_post)
    packed = jnp.concatenate([gain_grads[n] for n in gain_names], axis=1)
    gain_parts = _Exchange([packed], "gather").standalone("gather_gain_grads")[0]
    pack3 = lambda i: jnp.concatenate([gain_args[n][i] for n in gain_names], axis=1)
    gain_out = _adamw(gain_parts, pack3(0), pack3(1), pack3(2), "adamw_gains", tb=1)
    offs = [0]
    for n in gain_names:
        offs.append(offs[-1] + gain_args[n][0].shape[1])
    small_out = {n: tuple(o[:, offs[i]:offs[i + 1]] for o in gain_out) for i, n in enumerate(gain_names)}

    loss = lax.psum(loss_part[0, 0], ("x", "y", "c"))

    order = ["norm_attn_pre", "norm_attn_post", "w_in", "q_latent_norm", "kv_latent_norm", "w_uq", "w_ukv", "w_out",
             "norm_mlp_pre", "norm_mlp_post", "w_up", "w_down"]
    res = {n: (small_out[n] if n in small_out else tuple(o[None] for o in big_out[n])) for n in order}
    return (loss, grad_x[None], *[res[n][0] for n in order], *[res[n][1] for n in order],
            *[res[n][2] for n in order], *[res[n][3] for n in order])
```
